```python
import math
import jax, jax.numpy as jnp
from jax import lax
import numpy as np

D_MODEL = 1024
BATCH = 1
SEQ = 16384
DEPTH = 1

SSD_EXPAND = 2
D_SSD = SSD_EXPAND * D_MODEL
SSD_HEADDIM = 64
SSD_HEADS = D_SSD // SSD_HEADDIM
SSD_GROUPS = 8
SSD_STATE = 128
SSD_CONV = 4
SSD_CHUNK = 128
D_XBC = D_SSD + 2 * SSD_GROUPS * SSD_STATE
POOL_WINDOWS = (2, 4, 8, 16)
POOL_GROUPS = len(POOL_WINDOWS)
D_POOL = D_MODEL
POOL_GDIM = D_POOL // POOL_GROUPS
N_BRANCHES = 2
SPLIT_OFFSETS = (D_SSD, D_SSD + D_XBC, D_SSD + D_XBC + SSD_HEADS, D_SSD + D_XBC + SSD_HEADS + D_POOL)
D_IN_PROJ = D_SSD + D_XBC + SSD_HEADS + D_POOL + N_BRANCHES * D_MODEL
N_EXPERTS = 64
TOP_K = 8
N_EXPERT_GROUPS = 8
EXPERTS_PER_GROUP = N_EXPERTS // N_EXPERT_GROUPS
TOPK_GROUPS = 4
D_EXPERT = 256
D_SHARED = 256
ROUTED_SCALE = 2.5
MOE_BLOCK = 128
N_ADA = 6
EPS = 1e-6

kernel_name = 'hybrid_ssd_pool_moe_block'


def rmsnorm(x, g):
    xf = x.astype(jnp.float32)
    inv = lax.rsqrt(jnp.mean(xf * xf, axis=-1, keepdims=True) + EPS)
    return (xf * inv * g.astype(jnp.float32)).astype(x.dtype)


def modulate(x, shift, scale):
    return x * (1 + scale[:, None, :]) + shift[:, None, :]


def causal_dwconv(x, w, b):
    K, C = w.shape
    y = lax.conv_general_dilated(x, w[:, None, :].astype(x.dtype), window_strides=(1,),
                                 padding=[(K - 1, 0)], dimension_numbers=('NWC', 'WIO', 'NWC'),
                                 feature_group_count=C)
    return y + b


def segsum_exp(a):
    Q = a.shape[-1]
    cs = jnp.cumsum(a, axis=-1)
    diff = cs[..., :, None] - cs[..., None, :]
    mask = jnp.tril(jnp.ones((Q, Q), dtype=bool))
    return jnp.where(mask, jnp.exp(jnp.where(mask, diff, 0.0)), 0.0)


def ssd_chunked(xh, dt, A, Bm, Cm):
    Bsz, S, H, P = xh.shape
    G, N = Bm.shape[-2:]
    R = H // G
    Q = SSD_CHUNK
    nc = S // Q
    x = (xh * dt[..., None]).reshape(Bsz, nc, Q, G, R, P)
    a = (dt * A).reshape(Bsz, nc, Q, G, R).transpose(0, 1, 3, 4, 2)
    Bc = Bm.reshape(Bsz, nc, Q, G, N)
    Cc = Cm.reshape(Bsz, nc, Q, G, N)
    a_cs = jnp.cumsum(a, axis=-1)
    L = segsum_exp(a)
    CB = jnp.einsum('bclgn,bcsgn->bcgls', Cc, Bc)
    y_diag = jnp.einsum('bcgls,bcgrls,bcsgrp->bclgrp', CB, L, x)
    decay_states = jnp.exp(a_cs[..., -1:] - a_cs)
    states = jnp.einsum('bclgn,bcgrl,bclgrp->bcgrpn', Bc, decay_states, x)
    chunk_decay = jnp.exp(a_cs[..., -1])

    def step(carry, inp):
        st, dec = inp
        return carry * dec[..., None, None] + st, carry

    init = jnp.zeros(states.shape[:1] + states.shape[2:], states.dtype)
    _, prev = lax.scan(step, init, (jnp.moveaxis(states, 1, 0), jnp.moveaxis(chunk_decay, 1, 0)))
    prev = jnp.moveaxis(prev, 0, 1)
    y_off = jnp.einsum('bclgn,bcgrpn,bcgrl->bclgrp', Cc, prev, jnp.exp(a_cs))
    return (y_diag + y_off).reshape(Bsz, S, H, P)


def ssd_branch(z, xbc, dt_raw, conv_w, conv_b, dt_bias, A_log, D_skip, norm_g):
    xbc = jax.nn.silu(causal_dwconv(xbc, conv_w, conv_b))
    Bsz, S, _ = xbc.shape
    GN = SSD_GROUPS * SSD_STATE
    xs = xbc[..., :D_SSD].reshape(Bsz, S, SSD_HEADS, SSD_HEADDIM)
    Bm = xbc[..., D_SSD:D_SSD + GN].reshape(Bsz, S, SSD_GROUPS, SSD_STATE)
    Cm = xbc[..., D_SSD + GN:].reshape(Bsz, S, SSD_GROUPS, SSD_STATE)
    dt = jax.nn.softplus((dt_raw + dt_bias).astype(jnp.float32))
    A = -jnp.exp(A_log.astype(jnp.float32))
    y = ssd_chunked(xs, dt, A, Bm, Cm) + D_skip[:, None] * xs
    y = y.reshape(Bsz, S, D_SSD) * jax.nn.silu(z)
    y = rmsnorm(y.reshape(Bsz, S, SSD_GROUPS, D_SSD // SSD_GROUPS), norm_g.reshape(SSD_GROUPS, -1))
    return y.reshape(Bsz, S, D_SSD)


def pool_branch(xp, pool_w, pool_scale):
    Bsz, S, _ = xp.shape
    xg = xp.reshape(Bsz, S, POOL_GROUPS, POOL_GDIM).astype(jnp.float32)
    csp = jnp.pad(jnp.cumsum(xg, axis=1), ((0, 0), (1, 0), (0, 0), (0, 0)))
    pos = jnp.arange(1, S + 1, dtype=jnp.float32)
    outs = []
    for gi, w in enumerate(POOL_WINDOWS):
        c_g = csp[:, :, gi]
        lower = jnp.concatenate([jnp.zeros((Bsz, w - 1, POOL_GDIM), c_g.dtype), c_g[:, :S + 1 - w]], axis=1)
        mean = (c_g[:, 1:] - lower) / jnp.minimum(pos, w)[None, :, None]
        outs.append(mean - xg[:, :, gi])
    pooled = jnp.stack(outs, axis=2).astype(xp.dtype)
    mixed = jnp.einsum('bsgc,gcd->bsgd', pooled, pool_w)
    return mixed.reshape(Bsz, S, D_POOL) * pool_scale


def moe_ffn(u, w_router, router_bias, we_gate, we_up, we_down, ws_gate, ws_up, ws_down):
    Bsz, S, D = u.shape
    N = Bsz * S
    uf = u.reshape(N, D)
    scores = jax.nn.sigmoid((uf @ w_router).astype(jnp.float32))
    choice = scores + router_bias.astype(jnp.float32)
    grp = choice.reshape(N, N_EXPERT_GROUPS, EXPERTS_PER_GROUP)
    grp_score = lax.top_k(grp, 2)[0].sum(-1)
    _, top_grp = lax.top_k(grp_score, TOPK_GROUPS)
    grp_mask = jnp.any(top_grp[:, :, None] == jnp.arange(N_EXPERT_GROUPS)[None, None, :], axis=1)
    masked = jnp.where(jnp.repeat(grp_mask, EXPERTS_PER_GROUP, axis=1), choice, -jnp.inf)
    _, top_e = lax.top_k(masked, TOP_K)
    wts = jnp.take_along_axis(scores, top_e, axis=1)
    wts = wts / jnp.sum(wts, axis=-1, keepdims=True) * ROUTED_SCALE
    NK = N * TOP_K
    e_flat = top_e.reshape(NK).astype(jnp.int32)
    tok_flat = jnp.arange(NK, dtype=jnp.int32) // TOP_K
    order = jnp.argsort(e_flat)
    e_sorted = e_flat[order]
    counts = jax.ops.segment_sum(jnp.ones((NK,), jnp.int32), e_flat, num_segments=N_EXPERTS)
    padded = (counts + MOE_BLOCK - 1) // MOE_BLOCK * MOE_BLOCK
    starts = jnp.cumsum(counts) - counts
    pends = jnp.cumsum(padded)
    pstarts = pends - padded
    dest = pstarts[e_sorted] + jnp.arange(NK, dtype=jnp.int32) - starts[e_sorted]
    n_blocks = -(-NK // MOE_BLOCK) + N_EXPERTS
    n_slots = n_blocks * MOE_BLOCK
    slot_tok = jnp.full((n_slots,), N, jnp.int32).at[dest].set(tok_flat[order])
    slot_w = jnp.zeros((n_slots,), u.dtype).at[dest].set(wts.reshape(NK)[order].astype(u.dtype))
    block_e = jnp.minimum(jnp.searchsorted(pends, jnp.arange(n_blocks, dtype=jnp.int32) * MOE_BLOCK, side='right'),
                          N_EXPERTS - 1).astype(jnp.int32)
    u_pad = jnp.concatenate([uf, jnp.zeros((1, D), uf.dtype)], axis=0)

    def expert_block(args):
        toks, ws, e = args
        xb = u_pad[toks]
        hid = jax.nn.silu(xb @ we_gate[e]) * (xb @ we_up[e])
        return (hid @ we_down[e]) * ws[:, None]

    y_slots = lax.map(expert_block, (slot_tok.reshape(n_blocks, MOE_BLOCK),
                                     slot_w.reshape(n_blocks, MOE_BLOCK), block_e))
    routed = jax.ops.segment_sum(y_slots.reshape(n_slots, D), slot_tok, num_segments=N + 1)[:N]
    shared = (jax.nn.silu(uf @ ws_gate) * (uf @ ws_up)) @ ws_down
    return (routed + shared).reshape(Bsz, S, D)


def setup_inputs(seed: int = 0) -> dict:
    key = jax.random.key(seed)
    ks = jax.random.split(key, 32)
    f32 = jnp.float32

    def nrm(k, shape, scale):
        return jax.random.normal(k, shape, f32) * scale

    dt0 = jnp.exp(jax.random.uniform(ks[8], (DEPTH, SSD_HEADS), f32, math.log(1e-3), math.log(1e-1)))
    return {
        'x': nrm(ks[0], (BATCH, SEQ, D_MODEL), 1.0),
        'c': nrm(ks[1], (BATCH, D_MODEL), 1.0),
        'w_ada': nrm(ks[2], (DEPTH, D_MODEL, N_ADA * D_MODEL), 0.5 * D_MODEL ** -0.5),
        'b_ada': nrm(ks[3], (DEPTH, N_ADA * D_MODEL), 0.02),
        'norm_mix_g': 1.0 + nrm(ks[4], (DEPTH, D_MODEL), 0.02),
        'w_in': nrm(ks[5], (DEPTH, D_MODEL, D_IN_PROJ), D_MODEL ** -0.5),
        'conv_w': nrm(ks[6], (DEPTH, SSD_CONV, D_XBC), SSD_CONV ** -0.5),
        'conv_b': nrm(ks[7], (DEPTH, D_XBC), 0.02),
        'dt_bias': dt0 + jnp.log(-jnp.expm1(-dt0)),
        'A_log': jnp.log(jax.random.uniform(ks[9], (DEPTH, SSD_HEADS), f32, 1.0, 16.0)),
        'D_skip': 1.0 + nrm(ks[10], (DEPTH, SSD_HEADS), 0.02),
        'ssd_norm_g': 1.0 + nrm(ks[11], (DEPTH, D_SSD), 0.02),
        'pool_w': nrm(ks[12], (DEPTH, POOL_GROUPS, POOL_GDIM, POOL_GDIM), POOL_GDIM ** -0.5),
        'pool_scale': 1.0 + nrm(ks[13], (DEPTH, D_POOL), 0.02),
        'w_br_ssd': nrm(ks[14], (DEPTH, D_SSD, D_MODEL), D_SSD ** -0.5),
        'w_br_pool': nrm(ks[15], (DEPTH, D_POOL, D_MODEL), D_POOL ** -0.5),
        'w_out': nrm(ks[16], (DEPTH, D_MODEL, D_MODEL), D_MODEL ** -0.5),
        'norm_ffn_g': 1.0 + nrm(ks[17], (DEPTH, D_MODEL), 0.02),
        'w_router': nrm(ks[18], (DEPTH, D_MODEL, N_EXPERTS), D_MODEL ** -0.5),
        'router_bias': nrm(ks[19], (DEPTH, N_EXPERTS), 0.01),
        'we_gate': nrm(ks[20], (DEPTH, N_EXPERTS, D_MODEL, D_EXPERT), D_MODEL ** -0.5),
        'we_up': nrm(ks[21], (DEPTH, N_EXPERTS, D_MODEL, D_EXPERT), D_MODEL ** -0.5),
        'we_down': nrm(ks[22], (DEPTH, N_EXPERTS, D_EXPERT, D_MODEL), D_EXPERT ** -0.5),
        'ws_gate': nrm(ks[23], (DEPTH, D_MODEL, D_SHARED), D_MODEL ** -0.5),
        'ws_up': nrm(ks[24], (DEPTH, D_MODEL, D_SHARED), D_MODEL ** -0.5),
        'ws_down': nrm(ks[25], (DEPTH, D_SHARED, D_MODEL), D_SHARED ** -0.5),
        'final_norm_g': 1.0 + nrm(ks[26], (D_MODEL,), 0.02),
    }


def reference(x, c, w_ada, b_ada, norm_mix_g, w_in, conv_w, conv_b, dt_bias, A_log, D_skip, ssd_norm_g,
              pool_w, pool_scale, w_br_ssd, w_br_pool, w_out, norm_ffn_g, w_router, router_bias,
              we_gate, we_up, we_down, ws_gate, ws_up, ws_down, final_norm_g):
    h = x
    c_act = jax.nn.silu(c)
    for l in range(DEPTH):
        mod = c_act @ w_ada[l] + b_ada[l]
        sh1, sc1, g1, sh2, sc2, g2 = jnp.split(mod, N_ADA, axis=-1)
        u = modulate(rmsnorm(h, norm_mix_g[l]), sh1, sc1)
        proj = u @ w_in[l]
        z, xbc, dt_raw, xp, gts = jnp.split(proj, SPLIT_OFFSETS, axis=-1)
        y_ssd = ssd_branch(z, xbc, dt_raw, conv_w[l], conv_b[l], dt_bias[l], A_log[l], D_skip[l],
                           ssd_norm_g[l]) @ w_br_ssd[l]
        y_pool = pool_branch(xp, pool_w[l], pool_scale[l]) @ w_br_pool[l]
        g_ssd, g_pool = jnp.split(jax.nn.sigmoid(gts), N_BRANCHES, axis=-1)
        mixed = (g_ssd * y_ssd + g_pool * y_pool) @ w_out[l]
        h = h + g1[:, None, :] * mixed
        u2 = modulate(rmsnorm(h, norm_ffn_g[l]), sh2, sc2)
        h = h + g2[:, None, :] * moe_ffn(u2, w_router[l], router_bias[l], we_gate[l], we_up[l], we_down[l],
                                         ws_gate[l], ws_up[l], ws_down[l])
    return rmsnorm(h, final_norm_g)
```

```python
import functools

import jax
import jax.numpy as jnp
from jax import lax
from jax.experimental import pallas as pl
from jax.experimental.pallas import tpu as pltpu

F32 = jnp.float32
BF16 = jnp.bfloat16
HIGHEST = lax.Precision.HIGHEST

D_MODEL = 1024
D_SSD = 2048
HEADDIM = 64
N_HEADS = 32
N_GROUPS = 8
HEADS_PER_GROUP = N_HEADS // N_GROUPS
D_STATE = 128
CONV_K = 4
CHUNK = 128
GROUP_W = D_SSD // N_GROUPS
D_BC = N_GROUPS * D_STATE
POOL_WINDOWS = (2, 4, 8, 16)
POOL_GDIM = 256
N_EXPERTS = 64
TOP_K = 8
N_EGROUPS = 8
EXPERTS_PER_GROUP = 8
TOPK_GROUPS = 4
D_EXPERT = 256
D_SHARED = 256
ROUTED_SCALE = 2.5
MOE_BLOCK = 128
EPS = 1e-6

LANES = 128
SUBLANES = 8
ROW_TILES = D_MODEL // LANES

PROJ_W = D_SSD + D_SSD + 2 * D_BC + 3 * D_MODEL
PROJ_CHUNK = 512

VMEM_LIMIT = 56 * 1024 * 1024


def _cparams(sem=("arbitrary",)):
    return pltpu.CompilerParams(dimension_semantics=sem, vmem_limit_bytes=VMEM_LIMIT)


def _full(shape):
    nd = len(shape)
    return pl.BlockSpec(shape, lambda *_: (0,) * nd)


def _silu(v):
    return v * jax.nn.sigmoid(v)


def _ada_kernel(c_ref, w_ref, b_ref, o_ref):
    c = c_ref[...]
    o_ref[...] = jnp.dot(_silu(c), w_ref[...], preferred_element_type=F32, precision=HIGHEST) + b_ref[...]


def _ada(c, w_ada, b_ada):
    n_out = w_ada.shape[1]
    tn = 1536
    c8 = jnp.broadcast_to(c, (SUBLANES, D_MODEL))
    out = pl.pallas_call(
        _ada_kernel,
        grid=(n_out // tn,),
        in_specs=[_full((SUBLANES, D_MODEL)),
                  pl.BlockSpec((D_MODEL, tn), lambda j: (0, j)),
                  pl.BlockSpec((1, tn), lambda j: (0, j))],
        out_specs=pl.BlockSpec((SUBLANES, tn), lambda j: (0, j)),
        out_shape=jax.ShapeDtypeStruct((SUBLANES, n_out), F32),
        compiler_params=_cparams(),
        name="ada",
    )(c8, w_ada, b_ada.reshape(1, n_out))
    return out[0:1]


def _inproj_kernel(x_ref, g_ref, sh_ref, sc_ref, w_ref, wdt_ref, proj_ref, dt_ref):
    x = x_ref[...]
    inv = lax.rsqrt(jnp.mean(x * x, axis=-1, keepdims=True) + EPS)
    u = x * inv * g_ref[...]
    u = u * (1.0 + sc_ref[...]) + sh_ref[...]
    ub = u.astype(BF16)
    for c0 in range(0, PROJ_W, PROJ_CHUNK):
        proj_ref[:, c0:c0 + PROJ_CHUNK] = jnp.dot(
            ub, w_ref[:, c0:c0 + PROJ_CHUNK], preferred_element_type=F32).astype(BF16)
    dt_ref[...] = jnp.dot(u, wdt_ref[...], preferred_element_type=F32, precision=HIGHEST)


def _in_proj(x2, g, sh, sc, w_main, w_dt):
    n = x2.shape[0]
    tm = 256
    return pl.pallas_call(
        _inproj_kernel,
        grid=(n // tm,),
        in_specs=[pl.BlockSpec((tm, D_MODEL), lambda i: (i, 0)),
                  _full((1, D_MODEL)), _full((1, D_MODEL)), _full((1, D_MODEL)),
                  pl.BlockSpec((D_MODEL, PROJ_W), lambda i: (0, 0), pipeline_mode=pl.Buffered(1)),
                  _full((D_MODEL, LANES))],
        out_specs=[pl.BlockSpec((tm, PROJ_W), lambda i: (i, 0)),
                   pl.BlockSpec((tm, LANES), lambda i: (i, 0))],
        out_shape=[jax.ShapeDtypeStruct((n, PROJ_W), BF16),
                   jax.ShapeDtypeStruct((n, LANES), F32)],
        compiler_params=_cparams(),
        name="in_proj",
    )(x2, g, sh, sc, w_main, w_dt)


def _shift_rows(cur, tail, s, row8):
    rolled = pltpu.roll(cur, s, axis=0)
    top = jnp.where(row8 < s, pltpu.roll(tail, s, axis=0), rolled[0:SUBLANES])
    return jnp.concatenate([top, rolled[SUBLANES:]], axis=0)


def _conv_silu(cur_ref, tail_ref, w_ref, b_ref, out_ref, width, cw=512):
    t = cur_ref.shape[0]
    row8 = lax.broadcasted_iota(jnp.int32, (SUBLANES, cw), 0)
    for c0 in range(0, width, cw):
        sl = slice(c0, c0 + cw)
        cur = cur_ref[:, sl].astype(F32)
        tail = tail_ref[:, sl]
        acc = cur * w_ref[CONV_K - 1:CONV_K, sl] + b_ref[:, sl]
        for s in range(1, CONV_K):
            acc = acc + _shift_rows(cur, tail, s, row8) * w_ref[CONV_K - 1 - s:CONV_K - s, sl]
        out_ref[:, sl] = _silu(acc)
        tail_ref[:, sl] = cur[t - SUBLANES:t]


def _ssd_kernel(z_ref, xs_ref, bc_ref, dt_ref, cwx_ref, cwbc_ref, cbx_ref, cbbc_ref, dtb_ref, alog_ref,
                dskip_ref, ng_ref, o_ref, tailx, tailbc, state, xc, bcc, ybuf):
    q = CHUNK

    @pl.when(pl.program_id(0) == 0)
    def _():
        tailx[...] = jnp.zeros_like(tailx)
        tailbc[...] = jnp.zeros_like(tailbc)
        state[...] = jnp.zeros_like(state)

    _conv_silu(xs_ref, tailx, cwx_ref, cbx_ref, xc, D_SSD)
    _conv_silu(bc_ref, tailbc, cwbc_ref, cbbc_ref, bcc, 2 * D_BC)

    v = dt_ref[...] + dtb_ref[...]
    dt = jnp.maximum(v, 0.0) + jnp.log(1.0 + jnp.exp(-jnp.abs(v)))
    a = dt * (-jnp.exp(alog_ref[...]))
    ri = lax.broadcasted_iota(jnp.int32, (q, q), 0)
    ci = lax.broadcasted_iota(jnp.int32, (q, q), 1)
    causal = ri >= ci
    a_cs = jnp.dot(causal.astype(F32), a, preferred_element_type=F32, precision=HIGHEST)
    a_cs_t = a_cs.T
    dt_t = dt.T

    for g in range(N_GROUPS):
        b_g = bcc[:, g * D_STATE:(g + 1) * D_STATE]
        c_g = bcc[:, D_BC + g * D_STATE:D_BC + (g + 1) * D_STATE].astype(BF16)
        cb = lax.dot_general(c_g, b_g.astype(BF16), (((1,), (1,)), ((), ())),
                             preferred_element_type=F32)
        b_t = b_g.T
        st_g = state[g]
        y_off = jnp.dot(c_g, st_g.astype(BF16), preferred_element_type=F32)
        for r in range(HEADS_PER_GROUP):
            h = g * HEADS_PER_GROUP + r
            col = a_cs[:, h:h + 1]
            row = a_cs_t[h:h + 1, :]
            dtrow = dt_t[h:h + 1, :]
            decay = jnp.exp(jnp.where(causal, col - row, -jnp.inf))
            m = (cb * decay * dtrow).astype(BF16)
            xh = xc[:, h * HEADDIM:(h + 1) * HEADDIM].astype(BF16)
            y_h = jnp.dot(m, xh, preferred_element_type=F32)
            y_h = y_h + y_off[:, r * HEADDIM:(r + 1) * HEADDIM] * jnp.exp(col)
            ybuf[:, h * HEADDIM:(h + 1) * HEADDIM] = y_h
            last = a_cs_t[h:h + 1, q - 1:q]
            wrow = jnp.exp(last - row) * dtrow
            s_new = jnp.dot((b_t * wrow).astype(BF16), xh, preferred_element_type=F32)
            st_h = st_g[:, r * HEADDIM:(r + 1) * HEADDIM]
            state[g, :, r * HEADDIM:(r + 1) * HEADDIM] = st_h * jnp.exp(last) + s_new

    for g in range(N_GROUPS):
        sl = slice(g * GROUP_W, (g + 1) * GROUP_W)
        zg = z_ref[:, sl].astype(F32)
        yg = (ybuf[:, sl] + dskip_ref[:, sl] * xc[:, sl]) * _silu(zg)
        inv = lax.rsqrt(jnp.mean(yg * yg, axis=-1, keepdims=True) + EPS)
        o_ref[:, sl] = (yg * inv * ng_ref[:, sl]).astype(BF16)


def _ssd(proj, dt_raw, conv_w, conv_b, dt_bias, a_log, d_skip, norm_g):
    n = proj.shape[0]
    t = CHUNK
    return pl.pallas_call(
        _ssd_kernel,
        grid=(n // t,),
        in_specs=[pl.BlockSpec((t, D_SSD), lambda i: (i, 0)),
                  pl.BlockSpec((t, D_SSD), lambda i: (i, 1)),
                  pl.BlockSpec((t, 2 * D_BC), lambda i: (i, 2)),
                  pl.BlockSpec((t, LANES), lambda i: (i, 0)),
                  pl.BlockSpec((CONV_K, D_SSD), lambda i: (0, 0)),
                  pl.BlockSpec((CONV_K, 2 * D_BC), lambda i: (0, 1)),
                  pl.BlockSpec((1, D_SSD), lambda i: (0, 0)),
                  pl.BlockSpec((1, 2 * D_BC), lambda i: (0, 1)),
                  _full((1, LANES)), _full((1, LANES)),
                  _full((1, D_SSD)), _full((1, D_SSD))],
        out_specs=pl.BlockSpec((t, D_SSD), lambda i: (i, 0)),
        out_shape=jax.ShapeDtypeStruct((n, D_SSD), BF16),
        scratch_shapes=[pltpu.VMEM((SUBLANES, D_SSD), F32),
                        pltpu.VMEM((SUBLANES, 2 * D_BC), F32),
                        pltpu.VMEM((N_GROUPS, D_STATE, GROUP_W), F32),
                        pltpu.VMEM((t, D_SSD), F32),
                        pltpu.VMEM((t, 2 * D_BC), F32),
                        pltpu.VMEM((t, D_SSD), F32)],
        compiler_params=_cparams(),
        name="ssd",
    )(proj, proj, proj, dt_raw, conv_w, conv_w, conv_b, conv_b, dt_bias, a_log, d_skip, norm_g)


def _mix_kernel(xp_ref, gs_ref, gp_ref, yn_ref, x_ref, poolw_ref, pscale_ref, wbp_ref, wbs_ref, wout_ref,
                g1_ref, ng_ref, sh2_ref, sc2_ref, g2_ref, wr_ref, rb_ref, wsgu_ref, wsd_ref,
                base_ref, u2t_ref, e8_ref, w8_ref, pos8_ref, cnt_ref, ptail, run):
    t = x_ref.shape[0]
    i = pl.program_id(0)

    @pl.when(i == 0)
    def _():
        ptail[...] = jnp.zeros_like(ptail)
        run[...] = jnp.zeros_like(run)

    xp = xp_ref[...].astype(F32)
    ext = jnp.concatenate([ptail[...], xp], axis=0)
    ptail[...] = xp[t - 2 * SUBLANES:t]
    pos = (lax.broadcasted_iota(jnp.int32, (t, 1), 0) + (i * t + 1)).astype(F32)
    pooled = []
    for gi, w in enumerate(POOL_WINDOWS):
        sl = slice(gi * POOL_GDIM, (gi + 1) * POOL_GDIM)
        e = ext[:, sl]
        s = e
        span = 1
        while span < w:
            s = s + pltpu.roll(s, span, axis=0)
            span *= 2
        win = s[2 * SUBLANES:]
        mean = win / jnp.minimum(pos, float(w))
        pg = (mean - xp[:, sl]).astype(BF16)
        mixed = jnp.dot(pg, poolw_ref[gi], preferred_element_type=F32)
        pooled.append(mixed * pscale_ref[:, sl])
    pooled = jnp.concatenate(pooled, axis=1).astype(BF16)
    y_pool = jnp.dot(pooled, wbp_ref[...], preferred_element_type=F32)
    y_ssd = jnp.dot(yn_ref[...], wbs_ref[...], preferred_element_type=F32)
    g_ssd = jax.nn.sigmoid(gs_ref[...].astype(F32))
    g_pool = jax.nn.sigmoid(gp_ref[...].astype(F32))
    mixed = (g_ssd * y_ssd + g_pool * y_pool).astype(BF16)
    h = x_ref[...] + g1_ref[...] * jnp.dot(mixed, wout_ref[...], preferred_element_type=F32)

    inv = lax.rsqrt(jnp.mean(h * h, axis=-1, keepdims=True) + EPS)
    u2 = h * inv * ng_ref[...]
    u2 = u2 * (1.0 + sc2_ref[...]) + sh2_ref[...]
    u2b = u2.astype(BF16)
    for j in range(ROW_TILES):
        u2t_ref[pl.ds(j, t, stride=ROW_TILES), :] = u2[:, j * LANES:(j + 1) * LANES]

    hs = jnp.dot(u2b, wsgu_ref[...], preferred_element_type=F32)
    act = (_silu(hs[:, :D_SHARED]) * hs[:, D_SHARED:]).astype(BF16)
    shared = jnp.dot(act, wsd_ref[...], preferred_element_type=F32)
    base_ref[...] = h + g2_ref[...] * shared

    logits = jnp.dot(u2, wr_ref[...], preferred_element_type=F32, precision=HIGHEST)
    scores = jax.nn.sigmoid(logits)
    choice = scores + rb_ref[...]
    lane = lax.broadcasted_iota(jnp.int32, (t, N_EXPERTS), 1)
    lane_f = lane.astype(F32)
    lane_grp = lane // EXPERTS_PER_GROUP
    neg = -jnp.inf
    gscore = []
    for g in range(N_EGROUPS):
        vg = jnp.where(lane_grp == g, choice, neg)
        m1 = jnp.max(vg, axis=-1, keepdims=True)
        i1 = jnp.min(jnp.where(vg == m1, lane_f, float(N_EXPERTS)), axis=-1, keepdims=True)
        m2 = jnp.max(jnp.where(lane_f == i1, neg, vg), axis=-1, keepdims=True)
        gscore.append(m1 + m2)
    gmask = jnp.zeros((t, N_EXPERTS), jnp.bool_)
    for g in range(N_EGROUPS):
        rank = jnp.zeros((t, 1), F32)
        for g2 in range(N_EGROUPS):
            if g2 == g:
                continue
            better = (gscore[g2] > gscore[g]) | ((gscore[g2] == gscore[g]) & (g2 < g))
            rank = rank + better.astype(F32)
        gmask = gmask | ((rank < float(TOPK_GROUPS)) & (lane_grp == g))
    work = jnp.where(gmask, choice, neg)
    k_iota = lax.broadcasted_iota(jnp.int32, (t, TOP_K), 1)
    e8 = jnp.zeros((t, TOP_K), jnp.int32)
    s8 = jnp.zeros((t, TOP_K), F32)
    sel = jnp.zeros((t, N_EXPERTS), jnp.bool_)
    onehots = []
    for k in range(TOP_K):
        m = jnp.max(work, axis=-1, keepdims=True)
        idx = jnp.min(jnp.where(work == m, lane_f, float(N_EXPERTS)), axis=-1, keepdims=True)
        oh = lane_f == idx
        onehots.append(oh)
        sk = jnp.sum(jnp.where(oh, scores, 0.0), axis=-1, keepdims=True)
        e8 = jnp.where(k_iota == k, idx.astype(jnp.int32), e8)
        s8 = jnp.where(k_iota == k, sk, s8)
        sel = sel | oh
        work = jnp.where(oh, neg, work)
    w8_ref[...] = s8 / jnp.sum(s8, axis=-1, keepdims=True) * ROUTED_SCALE
    e8_ref[...] = e8

    ri = lax.broadcasted_iota(jnp.int32, (t, t), 0)
    ci = lax.broadcasted_iota(jnp.int32, (t, t), 1)
    before = (ri > ci).astype(BF16)
    sel_f = jnp.where(sel, 1.0, 0.0)
    pos_tile = jnp.dot(before, sel_f.astype(BF16), preferred_element_type=F32) + run[...]
    p8 = jnp.zeros((t, TOP_K), jnp.int32)
    for k in range(TOP_K):
        pk = jnp.sum(jnp.where(onehots[k], pos_tile, 0.0), axis=-1, keepdims=True)
        p8 = jnp.where(k_iota == k, pk.astype(jnp.int32), p8)
    pos8_ref[...] = p8
    total = run[...] + jnp.sum(sel_f, axis=0, keepdims=True)
    run[...] = total
    cnt_ref[...] = total


def _mix(proj, yn, x2, pool_w, pool_scale, w_br_pool, w_br_ssd, w_out, g1, ng, sh2, sc2, g2,
         w_router, router_bias, ws_gu, ws_down):
    n = x2.shape[0]
    t = 256
    xp_blk = (D_SSD + D_SSD + 2 * D_BC) // D_MODEL
    row = lambda i: (i, 0)
    return pl.pallas_call(
        _mix_kernel,
        grid=(n // t,),
        in_specs=[pl.BlockSpec((t, D_MODEL), lambda i: (i, xp_blk)),
                  pl.BlockSpec((t, D_MODEL), lambda i: (i, xp_blk + 1)),
                  pl.BlockSpec((t, D_MODEL), lambda i: (i, xp_blk + 2)),
                  pl.BlockSpec((t, D_SSD), row),
                  pl.BlockSpec((t, D_MODEL), row),
                  _full((len(POOL_WINDOWS), POOL_GDIM, POOL_GDIM)),
                  _full((1, D_MODEL)),
                  _full((D_MODEL, D_MODEL)),
                  _full((D_SSD, D_MODEL)),
                  _full((D_MODEL, D_MODEL)),
                  _full((1, D_MODEL)), _full((1, D_MODEL)), _full((1, D_MODEL)), _full((1, D_MODEL)),
                  _full((1, D_MODEL)),
                  _full((D_MODEL, N_EXPERTS)),
                  _full((1, N_EXPERTS)),
                  _full((D_MODEL, 2 * D_SHARED)),
                  _full((D_SHARED, D_MODEL))],
        out_specs=[pl.BlockSpec((t, D_MODEL), row),
                   pl.BlockSpec((t * ROW_TILES, LANES), row),
                   pl.BlockSpec((t, TOP_K), row),
                   pl.BlockSpec((t, TOP_K), row),
                   pl.BlockSpec((t, TOP_K), row),
                   _full((1, N_EXPERTS))],
        out_shape=[jax.ShapeDtypeStruct((n, D_MODEL), F32),
                   jax.ShapeDtypeStruct((n * ROW_TILES, LANES), F32),
                   jax.ShapeDtypeStruct((n, TOP_K), jnp.int32),
                   jax.ShapeDtypeStruct((n, TOP_K), F32),
                   jax.ShapeDtypeStruct((n, TOP_K), jnp.int32),
                   jax.ShapeDtypeStruct((1, N_EXPERTS), F32)],
        scratch_shapes=[pltpu.VMEM((2 * SUBLANES, D_MODEL), F32),
                        pltpu.VMEM((1, N_EXPERTS), F32)],
        compiler_params=_cparams(),
        name="mix",
    )(proj, proj, proj, yn, x2, pool_w, pool_scale, w_br_pool, w_br_ssd, w_out, g1, ng, sh2, sc2, g2,
      w_router, router_bias, ws_gu, ws_down)


def _row_copy_out(u_ref, xs_ref, sem, t, d):
    return pltpu.make_async_copy(u_ref.at[pl.ds(pl.multiple_of(t * ROW_TILES, ROW_TILES), ROW_TILES)],
                                 xs_ref.at[d], sem)


def _dispatch_kernel(dest_ref, u_ref, xs_in_ref, xs_ref, sem):
    del xs_in_ref
    t5 = u_ref.shape[0] // ROW_TILES
    base = pl.program_id(0) * (t5 * TOP_K)

    def issue(t, carry):
        for k in range(TOP_K):
            _row_copy_out(u_ref, xs_ref, sem, t, dest_ref[base + t * TOP_K + k]).start()
        return carry

    lax.fori_loop(0, t5, issue, 0)

    def drain(t, carry):
        for k in range(TOP_K):
            _row_copy_out(u_ref, xs_ref, sem, t, dest_ref[base + t * TOP_K + k]).wait()
        return carry

    lax.fori_loop(0, t5, drain, 0)


def _dispatch(dest, u2t, n_slots):
    n = u2t.shape[0] // ROW_TILES
    t5 = 256
    xs0 = jnp.zeros((n_slots, ROW_TILES, LANES), F32)
    return pl.pallas_call(
        _dispatch_kernel,
        grid_spec=pltpu.PrefetchScalarGridSpec(
            num_scalar_prefetch=1,
            grid=(n // t5,),
            in_specs=[pl.BlockSpec((t5 * ROW_TILES, LANES), lambda i, d: (i, 0)),
                      pl.BlockSpec(memory_space=pl.ANY)],
            out_specs=pl.BlockSpec(memory_space=pl.ANY),
            scratch_shapes=[pltpu.SemaphoreType.DMA(())]),
        out_shape=jax.ShapeDtypeStruct((n_slots, ROW_TILES, LANES), F32),
        input_output_aliases={2: 0},
        compiler_params=_cparams(),
        name="dispatch",
    )(dest, u2t, xs0)


def _expert_kernel(be_ref, nb_ref, x_ref, wg_ref, wu_ref, wd_ref, o_ref):
    b = pl.program_id(0)

    @pl.when(b < nb_ref[0])
    def _():
        x = jnp.concatenate([x_ref[pl.ds(j, MOE_BLOCK, stride=ROW_TILES), :] for j in range(ROW_TILES)],
                            axis=1).astype(BF16)
        gate = jnp.dot(x, wg_ref[0], preferred_element_type=F32)
        up = jnp.dot(x, wu_ref[0], preferred_element_type=F32)
        hid = (_silu(gate) * up).astype(BF16)
        y = jnp.dot(hid, wd_ref[0], preferred_element_type=F32)
        for j in range(ROW_TILES):
            o_ref[pl.ds(j, MOE_BLOCK, stride=ROW_TILES), :] = y[:, j * LANES:(j + 1) * LANES]

    @pl.when(b >= nb_ref[0])
    def _():
        o_ref[...] = jnp.zeros_like(o_ref)


def _experts(block_e, nb_used, xs2, we_gate, we_up, we_down):
    n_blocks = block_e.shape[0]
    rows = MOE_BLOCK * ROW_TILES
    return pl.pallas_call(
        _expert_kernel,
        grid_spec=pltpu.PrefetchScalarGridSpec(
            num_scalar_prefetch=2,
            grid=(n_blocks,),
            in_specs=[pl.BlockSpec((rows, LANES), lambda b, be, nb: (b, 0)),
                      pl.BlockSpec((1, D_MODEL, D_EXPERT), lambda b, be, nb: (be[b], 0, 0)),
                      pl.BlockSpec((1, D_MODEL, D_EXPERT), lambda b, be, nb: (be[b], 0, 0)),
                      pl.BlockSpec((1, D_EXPERT, D_MODEL), lambda b, be, nb: (be[b], 0, 0))],
            out_specs=pl.BlockSpec((rows, LANES), lambda b, be, nb: (b, 0))),
        out_shape=jax.ShapeDtypeStruct((n_blocks * rows, LANES), F32),
        compiler_params=_cparams(),
        name="experts",
    )(block_e, nb_used, xs2, we_gate, we_up, we_down)


def _row_copy_in(y_ref, buf, sem, r, d):
    return pltpu.make_async_copy(y_ref.at[d],
                                 buf.at[pl.ds(pl.multiple_of(r * ROW_TILES, ROW_TILES), ROW_TILES)], sem)


def _combine_kernel(dest_ref, y_ref, w8_ref, base_ref, g2_ref, fg_ref, o_ref, buf, sem):
    t7 = base_ref.shape[0]
    base = pl.program_id(0) * (t7 * TOP_K)

    def issue(t, carry):
        for k in range(TOP_K):
            _row_copy_in(y_ref, buf, sem, k * t7 + t, dest_ref[base + t * TOP_K + k]).start()
        return carry

    lax.fori_loop(0, t7, issue, 0)

    def drain(t, carry):
        for k in range(TOP_K):
            _row_copy_in(y_ref, buf, sem, k * t7 + t, dest_ref[base + t * TOP_K + k]).wait()
        return carry

    lax.fori_loop(0, t7, drain, 0)

    w8 = w8_ref[...]
    parts = []
    for j in range(ROW_TILES):
        acc = jnp.zeros((t7, LANES), F32)
        for k in range(TOP_K):
            acc = acc + w8[:, k:k + 1] * buf[pl.ds(k * t7 * ROW_TILES + j, t7, stride=ROW_TILES), :]
        parts.append(acc)
    routed = jnp.concatenate(parts, axis=1)
    h = base_ref[...] + g2_ref[...] * routed
    inv = lax.rsqrt(jnp.mean(h * h, axis=-1, keepdims=True) + EPS)
    o_ref[...] = h * inv * fg_ref[...]


def _combine(dest, y3, w8, base, g2, fg):
    n = base.shape[0]
    t7 = 128
    return pl.pallas_call(
        _combine_kernel,
        grid_spec=pltpu.PrefetchScalarGridSpec(
            num_scalar_prefetch=1,
            grid=(n // t7,),
            in_specs=[pl.BlockSpec(memory_space=pl.ANY),
                      pl.BlockSpec((t7, TOP_K), lambda i, d: (i, 0)),
                      pl.BlockSpec((t7, D_MODEL), lambda i, d: (i, 0)),
                      pl.BlockSpec((1, D_MODEL), lambda i, d: (0, 0)),
                      pl.BlockSpec((1, D_MODEL), lambda i, d: (0, 0))],
            out_specs=pl.BlockSpec((t7, D_MODEL), lambda i, d: (i, 0)),
            scratch_shapes=[pltpu.VMEM((TOP_K * t7 * ROW_TILES, LANES), F32),
                            pltpu.SemaphoreType.DMA(())]),
        out_shape=jax.ShapeDtypeStruct((n, D_MODEL), F32),
        compiler_params=_cparams(),
        name="combine",
    )(dest, y3, w8, base, g2, fg)


def kernel(x, c, w_ada, b_ada, norm_mix_g, w_in, conv_w, conv_b, dt_bias, A_log, D_skip, ssd_norm_g, pool_w,
           pool_scale, w_br_ssd, w_br_pool, w_out, norm_ffn_g, w_router, router_bias, we_gate, we_up, we_down,
           ws_gate, ws_up, ws_down, final_norm_g):
    bsz, seq, _ = x.shape
    assert bsz == 1 and w_ada.shape[0] == 1
    n = seq
    h2 = x.reshape(n, D_MODEL)

    mod = _ada(c, w_ada[0], b_ada[0])
    sh1, sc1, g1, sh2, sc2, g2 = [mod[:, k * D_MODEL:(k + 1) * D_MODEL] for k in range(6)]

    wi = w_in[0]
    o_dt = D_SSD + D_SSD + 2 * D_BC
    w_main = jnp.concatenate([wi[:, :o_dt], wi[:, o_dt + N_HEADS:]], axis=1).astype(BF16)
    w_dt = jnp.pad(wi[:, o_dt:o_dt + N_HEADS], ((0, 0), (0, LANES - N_HEADS)))
    proj, dt_raw = _in_proj(h2, norm_mix_g[0][None], sh1, sc1, w_main, w_dt)

    pad_h = lambda v: jnp.pad(v[None], ((0, 0), (0, LANES - N_HEADS)))
    yn = _ssd(proj, dt_raw, conv_w[0], conv_b[0][None], pad_h(dt_bias[0]), pad_h(A_log[0]),
              jnp.repeat(D_skip[0], HEADDIM)[None], ssd_norm_g[0][None])

    ws_gu = jnp.concatenate([ws_gate[0], ws_up[0]], axis=1).astype(BF16)
    base, u2t, e8, w8, pos8, cnt = _mix(
        proj, yn, h2, pool_w[0].astype(BF16), pool_scale[0][None], w_br_pool[0].astype(BF16),
        w_br_ssd[0].astype(BF16), w_out[0].astype(BF16), g1, norm_ffn_g[0][None], sh2, sc2, g2,
        w_router[0], router_bias[0][None], ws_gu, ws_down[0].astype(BF16))

    n_blocks = -(-(n * TOP_K) // MOE_BLOCK) + N_EXPERTS
    n_slots = n_blocks * MOE_BLOCK
    counts = cnt[0].astype(jnp.int32)
    padded = (counts + MOE_BLOCK - 1) // MOE_BLOCK * MOE_BLOCK
    pends = jnp.cumsum(padded)
    pstarts = pends - padded
    onehot_e = e8[:, :, None] == jnp.arange(N_EXPERTS, dtype=jnp.int32)[None, None, :]
    dest = (jnp.sum(jnp.where(onehot_e, pstarts[None, None, :], 0), axis=-1) + pos8).reshape(n * TOP_K)
    blk_start = jnp.arange(n_blocks, dtype=jnp.int32) * MOE_BLOCK
    block_e = jnp.minimum(jnp.sum((pends[None, :] <= blk_start[:, None]).astype(jnp.int32), axis=1),
                          N_EXPERTS - 1)
    nb_used = (pends[-1] // MOE_BLOCK).reshape(1)

    xs = _dispatch(dest, u2t, n_slots)
    y2 = _experts(block_e, nb_used, xs.reshape(n_slots * ROW_TILES, LANES),
                  we_gate[0].astype(BF16), we_up[0].astype(BF16), we_down[0].astype(BF16))
    out = _combine(dest, y2.reshape(n_slots, ROW_TILES, LANES), w8, base, g2, final_norm_g[None])
    return out.reshape(bsz, seq, D_MODEL)
```

```python
import functools

import jax
import jax.numpy as jnp
from jax import lax
from jax.experimental import pallas as pl
from jax.experimental.pallas import tpu as pltpu

F32 = jnp.float32
BF16 = jnp.bfloat16
HIGHEST = lax.Precision.HIGHEST

D_MODEL = 1024
D_SSD = 2048
HEADDIM = 64
N_HEADS = 32
N_GROUPS = 8
HEADS_PER_GROUP = N_HEADS // N_GROUPS
D_STATE = 128
CONV_K = 4
CHUNK = 128
GROUP_W = D_SSD // N_GROUPS
D_BC = N_GROUPS * D_STATE
POOL_WINDOWS = (2, 4, 8, 16)
POOL_GDIM = 256
N_EXPERTS = 64
TOP_K = 8
N_EGROUPS = 8
EXPERTS_PER_GROUP = 8
TOPK_GROUPS = 4
D_EXPERT = 256
D_SHARED = 256
ROUTED_SCALE = 2.5
MOE_BLOCK = 128
EPS = 1e-6

LANES = 128
SUBLANES = 8
ROW_TILES = D_MODEL // LANES
PACK_TILES = ROW_TILES // 2
PACK_W = PACK_TILES * LANES
ID_ROW = PACK_TILES
EXPERT_BLOCK = 512

PROJ_W = D_SSD + D_SSD + 2 * D_BC + 3 * D_MODEL
PROJ_CHUNK = 512

VMEM_LIMIT = 56 * 1024 * 1024


def _cparams(sem=("arbitrary",)):
    return pltpu.CompilerParams(dimension_semantics=sem, vmem_limit_bytes=VMEM_LIMIT)


def _full(shape):
    nd = len(shape)
    return pl.BlockSpec(shape, lambda *_: (0,) * nd)


def _silu(v):
    return v * jax.nn.sigmoid(v)


def _ada_kernel(c_ref, w_ref, b_ref, o_ref):
    c = c_ref[...]
    o_ref[...] = jnp.dot(_silu(c), w_ref[...], preferred_element_type=F32, precision=HIGHEST) + b_ref[...]


def _ada(c, w_ada, b_ada):
    n_out = w_ada.shape[1]
    tn = 1536
    c8 = jnp.broadcast_to(c, (SUBLANES, D_MODEL))
    out = pl.pallas_call(
        _ada_kernel,
        grid=(n_out // tn,),
        in_specs=[_full((SUBLANES, D_MODEL)),
                  pl.BlockSpec((D_MODEL, tn), lambda j: (0, j)),
                  pl.BlockSpec((1, tn), lambda j: (0, j))],
        out_specs=pl.BlockSpec((SUBLANES, tn), lambda j: (0, j)),
        out_shape=jax.ShapeDtypeStruct((SUBLANES, n_out), F32),
        compiler_params=_cparams(),
        name="ada",
    )(c8, w_ada, b_ada.reshape(1, n_out))
    return out[0:1]


def _inproj_kernel(x_ref, g_ref, sh_ref, sc_ref, w_ref, wdt_ref, proj_ref, dt_ref):
    x = x_ref[...]
    inv = lax.rsqrt(jnp.mean(x * x, axis=-1, keepdims=True) + EPS)
    u = x * inv * g_ref[...]
    u = u * (1.0 + sc_ref[...]) + sh_ref[...]
    ub = u.astype(BF16)
    for c0 in range(0, PROJ_W, PROJ_CHUNK):
        proj_ref[:, c0:c0 + PROJ_CHUNK] = jnp.dot(
            ub, w_ref[:, c0:c0 + PROJ_CHUNK], preferred_element_type=F32).astype(BF16)
    dt_ref[...] = jnp.dot(u, wdt_ref[...], preferred_element_type=F32, precision=HIGHEST)


def _in_proj(x2, g, sh, sc, w_main, w_dt):
    n = x2.shape[0]
    tm = 256
    return pl.pallas_call(
        _inproj_kernel,
        grid=(n // tm,),
        in_specs=[pl.BlockSpec((tm, D_MODEL), lambda i: (i, 0)),
                  _full((1, D_MODEL)), _full((1, D_MODEL)), _full((1, D_MODEL)),
                  pl.BlockSpec((D_MODEL, PROJ_W), lambda i: (0, 0), pipeline_mode=pl.Buffered(1)),
                  _full((D_MODEL, LANES))],
        out_specs=[pl.BlockSpec((tm, PROJ_W), lambda i: (i, 0)),
                   pl.BlockSpec((tm, LANES), lambda i: (i, 0))],
        out_shape=[jax.ShapeDtypeStruct((n, PROJ_W), BF16),
                   jax.ShapeDtypeStruct((n, LANES), F32)],
        compiler_params=_cparams(),
        name="in_proj",
    )(x2, g, sh, sc, w_main, w_dt)


def _shift_rows(cur, tail, s, row8):
    rolled = pltpu.roll(cur, s, axis=0)
    top = jnp.where(row8 < s, pltpu.roll(tail, s, axis=0), rolled[0:SUBLANES])
    return jnp.concatenate([top, rolled[SUBLANES:]], axis=0)


def _conv_silu(cur_ref, tail_ref, w_ref, b_ref, out_ref, width, cw=512):
    t = cur_ref.shape[0]
    row8 = lax.broadcasted_iota(jnp.int32, (SUBLANES, cw), 0)
    for c0 in range(0, width, cw):
        sl = slice(c0, c0 + cw)
        cur = cur_ref[:, sl].astype(F32)
        tail = tail_ref[:, sl]
        acc = cur * w_ref[CONV_K - 1:CONV_K, sl] + b_ref[:, sl]
        for s in range(1, CONV_K):
            acc = acc + _shift_rows(cur, tail, s, row8) * w_ref[CONV_K - 1 - s:CONV_K - s, sl]
        out_ref[:, sl] = _silu(acc)
        tail_ref[:, sl] = cur[t - SUBLANES:t]


def _ssd_kernel(z_ref, xs_ref, bc_ref, dt_ref, cwx_ref, cwbc_ref, cbx_ref, cbbc_ref, dtb_ref, alog_ref,
                dskip_ref, ng_ref, o_ref, tailx, tailbc, state, xc, bcc, ybuf):
    q = CHUNK

    @pl.when(pl.program_id(0) == 0)
    def _():
        tailx[...] = jnp.zeros_like(tailx)
        tailbc[...] = jnp.zeros_like(tailbc)
        state[...] = jnp.zeros_like(state)

    _conv_silu(xs_ref, tailx, cwx_ref, cbx_ref, xc, D_SSD)
    _conv_silu(bc_ref, tailbc, cwbc_ref, cbbc_ref, bcc, 2 * D_BC)

    v = dt_ref[...] + dtb_ref[...]
    dt = jnp.maximum(v, 0.0) + jnp.log(1.0 + jnp.exp(-jnp.abs(v)))
    a = dt * (-jnp.exp(alog_ref[...]))
    ri = lax.broadcasted_iota(jnp.int32, (q, q), 0)
    ci = lax.broadcasted_iota(jnp.int32, (q, q), 1)
    causal = ri >= ci
    a_cs = jnp.dot(causal.astype(F32), a, preferred_element_type=F32, precision=HIGHEST)
    a_cs_t = a_cs.T
    dt_t = dt.T

    for g in range(N_GROUPS):
        b_g = bcc[:, g * D_STATE:(g + 1) * D_STATE]
        c_g = bcc[:, D_BC + g * D_STATE:D_BC + (g + 1) * D_STATE].astype(BF16)
        cb = lax.dot_general(c_g, b_g.astype(BF16), (((1,), (1,)), ((), ())),
                             preferred_element_type=F32)
        b_t = b_g.T
        st_g = state[g]
        y_off = jnp.dot(c_g, st_g.astype(BF16), preferred_element_type=F32)
        for r in range(HEADS_PER_GROUP):
            h = g * HEADS_PER_GROUP + r
            col = a_cs[:, h:h + 1]
            row = a_cs_t[h:h + 1, :]
            dtrow = dt_t[h:h + 1, :]
            decay = jnp.exp(jnp.where(causal, col - row, -jnp.inf))
            m = (cb * decay * dtrow).astype(BF16)
            xh = xc[:, h * HEADDIM:(h + 1) * HEADDIM].astype(BF16)
            y_h = jnp.dot(m, xh, preferred_element_type=F32)
            y_h = y_h + y_off[:, r * HEADDIM:(r + 1) * HEADDIM] * jnp.exp(col)
            ybuf[:, h * HEADDIM:(h + 1) * HEADDIM] = y_h
            last = a_cs_t[h:h + 1, q - 1:q]
            wrow = jnp.exp(last - row) * dtrow
            s_new = jnp.dot((b_t * wrow).astype(BF16), xh, preferred_element_type=F32)
            st_h = st_g[:, r * HEADDIM:(r + 1) * HEADDIM]
            state[g, :, r * HEADDIM:(r + 1) * HEADDIM] = st_h * jnp.exp(last) + s_new

    for g in range(N_GROUPS):
        sl = slice(g * GROUP_W, (g + 1) * GROUP_W)
        zg = z_ref[:, sl].astype(F32)
        yg = (ybuf[:, sl] + dskip_ref[:, sl] * xc[:, sl]) * _silu(zg)
        inv = lax.rsqrt(jnp.mean(yg * yg, axis=-1, keepdims=True) + EPS)
        o_ref[:, sl] = (yg * inv * ng_ref[:, sl]).astype(BF16)


def _ssd(proj, dt_raw, conv_w, conv_b, dt_bias, a_log, d_skip, norm_g):
    n = proj.shape[0]
    t = CHUNK
    return pl.pallas_call(
        _ssd_kernel,
        grid=(n // t,),
        in_specs=[pl.BlockSpec((t, D_SSD), lambda i: (i, 0)),
                  pl.BlockSpec((t, D_SSD), lambda i: (i, 1)),
                  pl.BlockSpec((t, 2 * D_BC), lambda i: (i, 2)),
                  pl.BlockSpec((t, LANES), lambda i: (i, 0)),
                  pl.BlockSpec((CONV_K, D_SSD), lambda i: (0, 0)),
                  pl.BlockSpec((CONV_K, 2 * D_BC), lambda i: (0, 1)),
                  pl.BlockSpec((1, D_SSD), lambda i: (0, 0)),
                  pl.BlockSpec((1, 2 * D_BC), lambda i: (0, 1)),
                  _full((1, LANES)), _full((1, LANES)),
                  _full((1, D_SSD)), _full((1, D_SSD))],
        out_specs=pl.BlockSpec((t, D_SSD), lambda i: (i, 0)),
        out_shape=jax.ShapeDtypeStruct((n, D_SSD), BF16),
        scratch_shapes=[pltpu.VMEM((SUBLANES, D_SSD), F32),
                        pltpu.VMEM((SUBLANES, 2 * D_BC), F32),
                        pltpu.VMEM((N_GROUPS, D_STATE, GROUP_W), F32),
                        pltpu.VMEM((t, D_SSD), F32),
                        pltpu.VMEM((t, 2 * D_BC), F32),
                        pltpu.VMEM((t, D_SSD), F32)],
        compiler_params=_cparams(),
        name="ssd",
    )(proj, proj, proj, dt_raw, conv_w, conv_w, conv_b, conv_b, dt_bias, a_log, d_skip, norm_g)


def _mix_kernel(xp_ref, gs_ref, gp_ref, yn_ref, x_ref, poolw_ref, pscale_ref, wbp_ref, wbs_ref, wout_ref,
                g1_ref, ng_ref, sh2_ref, sc2_ref, g2_ref, wr_ref, rb_ref, wsgu_ref, wsd_ref,
                base_ref, u2p_ref, e8_ref, w8_ref, pos8_ref, cnt_ref, ptail, run):
    t = x_ref.shape[0]
    i = pl.program_id(0)

    @pl.when(i == 0)
    def _():
        ptail[...] = jnp.zeros_like(ptail)
        run[...] = jnp.zeros_like(run)

    xp = xp_ref[...].astype(F32)
    ext = jnp.concatenate([ptail[...], xp], axis=0)
    ptail[...] = xp[t - 2 * SUBLANES:t]
    pos = (lax.broadcasted_iota(jnp.int32, (t, 1), 0) + (i * t + 1)).astype(F32)
    pooled = []
    for gi, w in enumerate(POOL_WINDOWS):
        sl = slice(gi * POOL_GDIM, (gi + 1) * POOL_GDIM)
        e = ext[:, sl]
        s = e
        span = 1
        while span < w:
            s = s + pltpu.roll(s, span, axis=0)
            span *= 2
        win = s[2 * SUBLANES:]
        mean = win / jnp.minimum(pos, float(w))
        pg = (mean - xp[:, sl]).astype(BF16)
        mixed = jnp.dot(pg, poolw_ref[gi], preferred_element_type=F32)
        pooled.append(mixed * pscale_ref[:, sl])
    pooled = jnp.concatenate(pooled, axis=1).astype(BF16)
    y_pool = jnp.dot(pooled, wbp_ref[...], preferred_element_type=F32)
    y_ssd = jnp.dot(yn_ref[...], wbs_ref[...], preferred_element_type=F32)
    g_ssd = jax.nn.sigmoid(gs_ref[...].astype(F32))
    g_pool = jax.nn.sigmoid(gp_ref[...].astype(F32))
    mixed = (g_ssd * y_ssd + g_pool * y_pool).astype(BF16)
    h = x_ref[...] + g1_ref[...] * jnp.dot(mixed, wout_ref[...], preferred_element_type=F32)

    inv = lax.rsqrt(jnp.mean(h * h, axis=-1, keepdims=True) + EPS)
    u2 = h * inv * ng_ref[...]
    u2 = u2 * (1.0 + sc2_ref[...]) + sh2_ref[...]
    u2b = u2.astype(BF16)
    u2r = u2b.astype(F32)
    for j in range(PACK_TILES):
        lo = lax.bitcast_convert_type(u2r[:, j * LANES:(j + 1) * LANES], jnp.uint32)
        hi = lax.bitcast_convert_type(u2r[:, (j + PACK_TILES) * LANES:(j + PACK_TILES + 1) * LANES], jnp.uint32)
        u2p_ref[:, j * LANES:(j + 1) * LANES] = (lo >> 16) | (hi & jnp.uint32(0xFFFF0000))

    hs = jnp.dot(u2b, wsgu_ref[...], preferred_element_type=F32)
    act = (_silu(hs[:, :D_SHARED]) * hs[:, D_SHARED:]).astype(BF16)
    shared = jnp.dot(act, wsd_ref[...], preferred_element_type=F32)
    base_ref[...] = h + g2_ref[...] * shared

    logits = jnp.dot(u2b, wr_ref[...], preferred_element_type=F32)
    scores = jax.nn.sigmoid(logits)
    choice = scores + rb_ref[...]
    lane = lax.broadcasted_iota(jnp.int32, (t, N_EXPERTS), 1)
    lane_f = lane.astype(F32)
    lane_grp = lane // EXPERTS_PER_GROUP
    neg = -jnp.inf
    gscore = []
    for g in range(N_EGROUPS):
        vg = jnp.where(lane_grp == g, choice, neg)
        m1 = jnp.max(vg, axis=-1, keepdims=True)
        i1 = jnp.min(jnp.where(vg == m1, lane_f, float(N_EXPERTS)), axis=-1, keepdims=True)
        m2 = jnp.max(jnp.where(lane_f == i1, neg, vg), axis=-1, keepdims=True)
        gscore.append(m1 + m2)
    gmask = jnp.zeros((t, N_EXPERTS), jnp.bool_)
    for g in range(N_EGROUPS):
        rank = jnp.zeros((t, 1), F32)
        for g2 in range(N_EGROUPS):
            if g2 == g:
                continue
            better = (gscore[g2] > gscore[g]) | ((gscore[g2] == gscore[g]) & (g2 < g))
            rank = rank + better.astype(F32)
        gmask = gmask | ((rank < float(TOPK_GROUPS)) & (lane_grp == g))
    work = jnp.where(gmask, choice, neg)
    k_iota = lax.broadcasted_iota(jnp.int32, (t, TOP_K), 1)
    e8 = jnp.zeros((t, TOP_K), jnp.int32)
    s8 = jnp.zeros((t, TOP_K), F32)
    sel = jnp.zeros((t, N_EXPERTS), jnp.bool_)
    onehots = []
    for k in range(TOP_K):
        m = jnp.max(work, axis=-1, keepdims=True)
        idx = jnp.min(jnp.where(work == m, lane_f, float(N_EXPERTS)), axis=-1, keepdims=True)
        oh = lane_f == idx
        onehots.append(oh)
        sk = jnp.sum(jnp.where(oh, scores, 0.0), axis=-1, keepdims=True)
        e8 = jnp.where(k_iota == k, idx.astype(jnp.int32), e8)
        s8 = jnp.where(k_iota == k, sk, s8)
        sel = sel | oh
        work = jnp.where(oh, neg, work)
    w8_ref[...] = s8 / jnp.sum(s8, axis=-1, keepdims=True) * ROUTED_SCALE
    e8_ref[...] = e8

    ri = lax.broadcasted_iota(jnp.int32, (t, t), 0)
    ci = lax.broadcasted_iota(jnp.int32, (t, t), 1)
    before = (ri > ci).astype(BF16)
    sel_f = jnp.where(sel, 1.0, 0.0)
    pos_tile = jnp.dot(before, sel_f.astype(BF16), preferred_element_type=F32) + run[...]
    p8 = jnp.zeros((t, TOP_K), jnp.int32)
    for k in range(TOP_K):
        pk = jnp.sum(jnp.where(onehots[k], pos_tile, 0.0), axis=-1, keepdims=True)
        p8 = jnp.where(k_iota == k, pk.astype(jnp.int32), p8)
    pos8_ref[...] = p8
    total = run[...] + jnp.sum(sel_f, axis=0, keepdims=True)
    run[...] = total
    cnt_ref[...] = total


def _mix(proj, yn, x2, pool_w, pool_scale, w_br_pool, w_br_ssd, w_out, g1, ng, sh2, sc2, g2,
         w_router, router_bias, ws_gu, ws_down):
    n = x2.shape[0]
    t = 256
    xp_blk = (D_SSD + D_SSD + 2 * D_BC) // D_MODEL
    row = lambda i: (i, 0)
    return pl.pallas_call(
        _mix_kernel,
        grid=(n // t,),
        in_specs=[pl.BlockSpec((t, D_MODEL), lambda i: (i, xp_blk)),
                  pl.BlockSpec((t, D_MODEL), lambda i: (i, xp_blk + 1)),
                  pl.BlockSpec((t, D_MODEL), lambda i: (i, xp_blk + 2)),
                  pl.BlockSpec((t, D_SSD), row),
                  pl.BlockSpec((t, D_MODEL), row),
                  _full((len(POOL_WINDOWS), POOL_GDIM, POOL_GDIM)),
                  _full((1, D_MODEL)),
                  _full((D_MODEL, D_MODEL)),
                  _full((D_SSD, D_MODEL)),
                  _full((D_MODEL, D_MODEL)),
                  _full((1, D_MODEL)), _full((1, D_MODEL)), _full((1, D_MODEL)), _full((1, D_MODEL)),
                  _full((1, D_MODEL)),
                  _full((D_MODEL, N_EXPERTS)),
                  _full((1, N_EXPERTS)),
                  _full((D_MODEL, 2 * D_SHARED)),
                  _full((D_SHARED, D_MODEL))],
        out_specs=[pl.BlockSpec((t, D_MODEL), row),
                   pl.BlockSpec((t, PACK_W), row),
                   pl.BlockSpec((t, TOP_K), row),
                   pl.BlockSpec((t, TOP_K), row),
                   pl.BlockSpec((t, TOP_K), row),
                   _full((1, N_EXPERTS))],
        out_shape=[jax.ShapeDtypeStruct((n, D_MODEL), F32),
                   jax.ShapeDtypeStruct((n, PACK_W), jnp.uint32),
                   jax.ShapeDtypeStruct((n, TOP_K), jnp.int32),
                   jax.ShapeDtypeStruct((n, TOP_K), F32),
                   jax.ShapeDtypeStruct((n, TOP_K), jnp.int32),
                   jax.ShapeDtypeStruct((1, N_EXPERTS), F32)],
        scratch_shapes=[pltpu.VMEM((2 * SUBLANES, D_MODEL), F32),
                        pltpu.VMEM((1, N_EXPERTS), F32)],
        compiler_params=_cparams(),
        name="mix",
    )(proj, proj, proj, yn, x2, pool_w, pool_scale, w_br_pool, w_br_ssd, w_out, g1, ng, sh2, sc2, g2,
      w_router, router_bias, ws_gu, ws_down)


PAD_ID = 0xFFFFFFFF


def _tile_rows(ref, first_row):
    return ref.at[pl.ds(pl.multiple_of(first_row * SUBLANES, SUBLANES), SUBLANES)]


def _dispatch_kernel(dest_ref, u_ref, xs_ref, stage, sem):
    t5 = u_ref.shape[0]
    i = pl.program_id(0)
    rows = t5 * SUBLANES

    @pl.when(i == 0)
    def _():
        stage[...] = jnp.zeros_like(stage)

    tok = lax.broadcasted_iota(jnp.int32, (t5, LANES), 0) + i * t5
    for k in range(TOP_K):
        for j in range(PACK_TILES):
            stage[pl.ds(k * rows + j, t5, stride=SUBLANES), :] = u_ref[:, j * LANES:(j + 1) * LANES]
        stage[pl.ds(k * rows + ID_ROW, t5, stride=SUBLANES), :] = (tok * TOP_K + k).astype(jnp.uint32)

    base = i * (t5 * TOP_K)

    def issue(t, carry):
        for k in range(TOP_K):
            d = dest_ref[base + t * TOP_K + k]
            pltpu.make_async_copy(_tile_rows(stage, k * t5 + t), _tile_rows(xs_ref, d), sem).start()
        return carry

    lax.fori_loop(0, t5, issue, 0)
    for k in range(TOP_K):
        pltpu.make_async_copy(stage.at[pl.ds(k * rows, rows)], xs_ref.at[pl.ds(0, rows)], sem).wait()


def _dispatch(dest, u2p, n_slot_rows):
    n = u2p.shape[0]
    t5 = 256
    return pl.pallas_call(
        _dispatch_kernel,
        grid_spec=pltpu.PrefetchScalarGridSpec(
            num_scalar_prefetch=1,
            grid=(n // t5,),
            in_specs=[pl.BlockSpec((t5, PACK_W), lambda i, d: (i, 0))],
            out_specs=pl.BlockSpec(memory_space=pl.ANY),
            scratch_shapes=[pltpu.VMEM((TOP_K * t5 * SUBLANES, LANES), jnp.uint32),
                            pltpu.SemaphoreType.DMA(())]),
        out_shape=jax.ShapeDtypeStruct((n_slot_rows * SUBLANES, LANES), jnp.uint32),
        compiler_params=_cparams(),
        name="dispatch",
    )(dest, u2p)


def _padfill_kernel(cnt_ref, xs_in_ref, xs_ref, tile, sem, *, cap):
    del xs_in_ref
    e = pl.program_id(0)
    sub = lax.broadcasted_iota(jnp.int32, (SUBLANES, LANES), 0)
    tile[...] = jnp.where(sub == ID_ROW, jnp.uint32(PAD_ID), jnp.uint32(0))
    c = cnt_ref[e]
    n_pad = lax.rem(EXPERT_BLOCK - lax.rem(c, EXPERT_BLOCK), EXPERT_BLOCK)
    first = e * cap + c

    def issue(r, carry):
        pltpu.make_async_copy(tile, _tile_rows(xs_ref, first + r), sem).start()
        return carry

    lax.fori_loop(0, n_pad, issue, 0)

    def drain(r, carry):
        pltpu.make_async_copy(tile, _tile_rows(xs_ref, first + r), sem).wait()
        return carry

    lax.fori_loop(0, n_pad, drain, 0)


def _padfill(counts, xs, cap):
    return pl.pallas_call(
        functools.partial(_padfill_kernel, cap=cap),
        grid_spec=pltpu.PrefetchScalarGridSpec(
            num_scalar_prefetch=1,
            grid=(N_EXPERTS,),
            in_specs=[pl.BlockSpec(memory_space=pl.ANY)],
            out_specs=pl.BlockSpec(memory_space=pl.ANY),
            scratch_shapes=[pltpu.VMEM((SUBLANES, LANES), jnp.uint32),
                            pltpu.SemaphoreType.DMA(())]),
        out_shape=jax.ShapeDtypeStruct(xs.shape, xs.dtype),
        input_output_aliases={1: 0},
        compiler_params=_cparams(),
        name="padfill",
    )(counts, xs)


def _expert_kernel(be_ref, br_ref, nb_ref, x_ref, wg_ref, wu_ref, wd_ref, ytok_ref, ybuf, idv, ids, sem, sem_ids,
                   *, n_ids):
    del be_ref, br_ref
    m = EXPERT_BLOCK
    b = pl.program_id(0)
    nb = nb_ref[0]
    slot = lax.rem(b, 2)

    def drain(s):
        pltpu.make_async_copy(ybuf.at[s], ytok_ref.at[pl.ds(0, m * SUBLANES)], sem.at[s]).wait()

    @pl.when(b < nb)
    def _():
        idrep = x_ref[pl.ds(ID_ROW, m, stride=SUBLANES), :]
        r = lax.broadcasted_iota(jnp.int32, (m, LANES), 0)
        ln = lax.broadcasted_iota(jnp.int32, (m, LANES), 1)
        idi = jnp.where(idrep == jnp.uint32(PAD_ID), n_ids + r, idrep.astype(jnp.int32))
        diag = jnp.where((r & (LANES - 1)) == ln, idi, 0).astype(F32)
        idv[...] = jnp.sum(diag.reshape(m // LANES, LANES, LANES), axis=1).astype(jnp.int32)
        ids_copy = pltpu.make_async_copy(idv, ids, sem_ids)
        ids_copy.start()

        lo, hi = [], []
        for j in range(PACK_TILES):
            w = x_ref[pl.ds(j, m, stride=SUBLANES), :]
            lo.append(lax.bitcast_convert_type(w << 16, F32).astype(BF16))
            hi.append(lax.bitcast_convert_type(w & jnp.uint32(0xFFFF0000), F32).astype(BF16))
        x = jnp.concatenate(lo + hi, axis=1)
        gate = jnp.dot(x, wg_ref[0], preferred_element_type=F32)
        up = jnp.dot(x, wu_ref[0], preferred_element_type=F32)
        hid = (_silu(gate) * up).astype(BF16)
        y = jnp.dot(hid, wd_ref[0], preferred_element_type=F32)
        for j in range(ROW_TILES):
            ybuf[slot, pl.ds(j, m, stride=SUBLANES), :] = y[:, j * LANES:(j + 1) * LANES]

        ids_copy.wait()
        unroll = 8

        def issue(a, carry):
            for u in range(unroll):
                row = a * unroll + u
                d = ids[row // LANES, row % LANES]
                pltpu.make_async_copy(_tile_rows(ybuf.at[slot], row), _tile_rows(ytok_ref, d),
                                      sem.at[slot]).start()
            return carry

        lax.fori_loop(0, m // unroll, issue, 0)

    @pl.when((b >= 1) & (b - 1 < nb))
    def _():
        drain(1 - slot)

    @pl.when((b == pl.num_programs(0) - 1) & (b < nb))
    def _():
        drain(slot)


def _experts(block_e, block_row, nb_used, xs, we_gate, we_up, we_down, n_ids):
    n_blocks = block_e.shape[0]
    m = EXPERT_BLOCK
    rows = m * SUBLANES
    return pl.pallas_call(
        functools.partial(_expert_kernel, n_ids=n_ids),
        grid_spec=pltpu.PrefetchScalarGridSpec(
            num_scalar_prefetch=3,
            grid=(n_blocks,),
            in_specs=[pl.BlockSpec((rows, LANES), lambda b, be, br, nb: (br[b], 0)),
                      pl.BlockSpec((1, D_MODEL, D_EXPERT), lambda b, be, br, nb: (be[b], 0, 0)),
                      pl.BlockSpec((1, D_MODEL, D_EXPERT), lambda b, be, br, nb: (be[b], 0, 0)),
                      pl.BlockSpec((1, D_EXPERT, D_MODEL), lambda b, be, br, nb: (be[b], 0, 0))],
            out_specs=pl.BlockSpec(memory_space=pl.ANY),
            scratch_shapes=[pltpu.VMEM((2, rows, LANES), F32),
                            pltpu.VMEM((m // LANES, LANES), jnp.int32),
                            pltpu.SMEM((m // LANES, LANES), jnp.int32),
                            pltpu.SemaphoreType.DMA((2,)),
                            pltpu.SemaphoreType.DMA(())]),
        out_shape=jax.ShapeDtypeStruct(((n_ids + m) * SUBLANES, LANES), F32),
        compiler_params=_cparams(),
        name="experts",
    )(block_e, block_row, nb_used, xs, we_gate, we_up, we_down)


def _combine_kernel(y_ref, w8_ref, base_ref, g2_ref, fg_ref, o_ref):
    t7 = base_ref.shape[0]
    w8 = w8_ref[...]
    tok_rows = TOP_K * SUBLANES
    parts = []
    for j in range(ROW_TILES):
        acc = jnp.zeros((t7, LANES), F32)
        for k in range(TOP_K):
            acc = acc + w8[:, k:k + 1] * y_ref[pl.ds(k * SUBLANES + j, t7, stride=tok_rows), :]
        parts.append(acc)
    routed = jnp.concatenate(parts, axis=1)
    h = base_ref[...] + g2_ref[...] * routed
    inv = lax.rsqrt(jnp.mean(h * h, axis=-1, keepdims=True) + EPS)
    o_ref[...] = h * inv * fg_ref[...]


def _combine(ytok, w8, base, g2, fg):
    n = base.shape[0]
    t7 = 128
    return pl.pallas_call(
        _combine_kernel,
        grid=(n // t7,),
        in_specs=[pl.BlockSpec((t7 * TOP_K * SUBLANES, LANES), lambda i: (i, 0)),
                  pl.BlockSpec((t7, TOP_K), lambda i: (i, 0)),
                  pl.BlockSpec((t7, D_MODEL), lambda i: (i, 0)),
                  _full((1, D_MODEL)), _full((1, D_MODEL))],
        out_specs=pl.BlockSpec((t7, D_MODEL), lambda i: (i, 0)),
        out_shape=jax.ShapeDtypeStruct((n, D_MODEL), F32),
        compiler_params=_cparams(),
        name="combine",
    )(ytok, w8, base, g2, fg)


def kernel(x, c, w_ada, b_ada, norm_mix_g, w_in, conv_w, conv_b, dt_bias, A_log, D_skip, ssd_norm_g, pool_w,
           pool_scale, w_br_ssd, w_br_pool, w_out, norm_ffn_g, w_router, router_bias, we_gate, we_up, we_down,
           ws_gate, ws_up, ws_down, final_norm_g):
    bsz, seq, _ = x.shape
    assert bsz == 1 and w_ada.shape[0] == 1
    n = seq
    h2 = x.reshape(n, D_MODEL)

    mod = _ada(c, w_ada[0], b_ada[0])
    sh1, sc1, g1, sh2, sc2, g2 = [mod[:, k * D_MODEL:(k + 1) * D_MODEL] for k in range(6)]

    wi = w_in[0]
    o_dt = D_SSD + D_SSD + 2 * D_BC
    w_main = jnp.concatenate([wi[:, :o_dt], wi[:, o_dt + N_HEADS:]], axis=1).astype(BF16)
    w_dt = jnp.pad(wi[:, o_dt:o_dt + N_HEADS], ((0, 0), (0, LANES - N_HEADS)))
    proj, dt_raw = _in_proj(h2, norm_mix_g[0][None], sh1, sc1, w_main, w_dt)

    pad_h = lambda v: jnp.pad(v[None], ((0, 0), (0, LANES - N_HEADS)))
    yn = _ssd(proj, dt_raw, conv_w[0], conv_b[0][None], pad_h(dt_bias[0]), pad_h(A_log[0]),
              jnp.repeat(D_skip[0], HEADDIM)[None], ssd_norm_g[0][None])

    ws_gu = jnp.concatenate([ws_gate[0], ws_up[0]], axis=1).astype(BF16)
    base, u2p, e8, w8, pos8, cnt = _mix(
        proj, yn, h2, pool_w[0].astype(BF16), pool_scale[0][None], w_br_pool[0].astype(BF16),
        w_br_ssd[0].astype(BF16), w_out[0].astype(BF16), g1, norm_ffn_g[0][None], sh2, sc2, g2,
        w_router[0].astype(BF16), router_bias[0][None], ws_gu, ws_down[0].astype(BF16))

    m = EXPERT_BLOCK
    cap = -(-n // m) * m
    counts = cnt[0].astype(jnp.int32)
    dest = (e8 * cap + pos8).reshape(n * TOP_K)
    nblk = (counts + m - 1) // m
    bends = jnp.cumsum(nblk)
    bstarts = bends - nblk
    nb_used = bends[-1]
    n_blocks = -(-(n * TOP_K) // m) + N_EXPERTS
    b_eff = jnp.minimum(jnp.arange(n_blocks, dtype=jnp.int32), nb_used - 1)
    block_e = jnp.sum((bends[None, :] <= b_eff[:, None]).astype(jnp.int32), axis=1)
    block_row = block_e * (cap // m) + b_eff - jnp.take(bstarts, block_e)

    xs = _dispatch(dest, u2p, N_EXPERTS * cap)
    xs = _padfill(counts, xs, cap)
    ytok = _experts(block_e, block_row, nb_used.reshape(1), xs,
                    we_gate[0].astype(BF16), we_up[0].astype(BF16), we_down[0].astype(BF16), n * TOP_K)
    out = _combine(ytok, w8, base, g2, final_norm_g[None])
    return out.reshape(bsz, seq, D_MODEL)
```

```python
import functools

import jax
import jax.numpy as jnp
from jax import lax
from jax.experimental import pallas as pl
from jax.experimental.pallas import tpu as pltpu

F32 = jnp.float32
BF16 = jnp.bfloat16
HIGHEST = lax.Precision.HIGHEST

D_MODEL = 1024
D_SSD = 2048
HEADDIM = 64
N_HEADS = 32
N_GROUPS = 8
HEADS_PER_GROUP = N_HEADS // N_GROUPS
D_STATE = 128
CONV_K = 4
CHUNK = 128
GROUP_W = D_SSD // N_GROUPS
D_BC = N_GROUPS * D_STATE
POOL_WINDOWS = (2, 4, 8, 16)
POOL_GDIM = 256
N_EXPERTS = 64
TOP_K = 8
N_EGROUPS = 8
EXPERTS_PER_GROUP = 8
TOPK_GROUPS = 4
D_EXPERT = 256
D_SHARED = 256
ROUTED_SCALE = 2.5
MOE_BLOCK = 128
EPS = 1e-6

LANES = 128
SUBLANES = 8
ROW_TILES = D_MODEL // LANES
PACK_TILES = ROW_TILES // 2
PACK_W = PACK_TILES * LANES
ID_ROW = PACK_TILES
EXPERT_BLOCK = 512

PROJ_W = D_SSD + D_SSD + 2 * D_BC + 3 * D_MODEL
PROJ_CHUNK = 512

VMEM_LIMIT = 56 * 1024 * 1024


def _cparams(sem=("arbitrary",)):
    return pltpu.CompilerParams(dimension_semantics=sem, vmem_limit_bytes=VMEM_LIMIT)


def _full(shape):
    nd = len(shape)
    return pl.BlockSpec(shape, lambda *_: (0,) * nd)


def _silu(v):
    return v * jax.nn.sigmoid(v)


def _ada_kernel(c_ref, w_ref, b_ref, o_ref):
    c = c_ref[...]
    o_ref[...] = jnp.dot(_silu(c), w_ref[...], preferred_element_type=F32, precision=HIGHEST) + b_ref[...]


def _ada(c, w_ada, b_ada):
    n_out = w_ada.shape[1]
    tn = 1536
    c8 = jnp.broadcast_to(c, (SUBLANES, D_MODEL))
    out = pl.pallas_call(
        _ada_kernel,
        grid=(n_out // tn,),
        in_specs=[_full((SUBLANES, D_MODEL)),
                  pl.BlockSpec((D_MODEL, tn), lambda j: (0, j)),
                  pl.BlockSpec((1, tn), lambda j: (0, j))],
        out_specs=pl.BlockSpec((SUBLANES, tn), lambda j: (0, j)),
        out_shape=jax.ShapeDtypeStruct((SUBLANES, n_out), F32),
        compiler_params=_cparams(),
        name="ada",
    )(c8, w_ada, b_ada.reshape(1, n_out))
    return out[0:1]


def _inproj_kernel(x_ref, g_ref, sh_ref, sc_ref, w_ref, wdt_ref, proj_ref, dt_ref):
    x = x_ref[...]
    inv = lax.rsqrt(jnp.mean(x * x, axis=-1, keepdims=True) + EPS)
    u = x * inv * g_ref[...]
    u = u * (1.0 + sc_ref[...]) + sh_ref[...]
    ub = u.astype(BF16)
    for c0 in range(0, PROJ_W, PROJ_CHUNK):
        proj_ref[:, c0:c0 + PROJ_CHUNK] = jnp.dot(
            ub, w_ref[:, c0:c0 + PROJ_CHUNK], preferred_element_type=F32).astype(BF16)
    dt_ref[...] = jnp.dot(u, wdt_ref[...], preferred_element_type=F32, precision=HIGHEST)


def _in_proj(x2, g, sh, sc, w_main, w_dt):
    n = x2.shape[0]
    tm = 256
    return pl.pallas_call(
        _inproj_kernel,
        grid=(n // tm,),
        in_specs=[pl.BlockSpec((tm, D_MODEL), lambda i: (i, 0)),
                  _full((1, D_MODEL)), _full((1, D_MODEL)), _full((1, D_MODEL)),
                  pl.BlockSpec((D_MODEL, PROJ_W), lambda i: (0, 0), pipeline_mode=pl.Buffered(1)),
                  _full((D_MODEL, LANES))],
        out_specs=[pl.BlockSpec((tm, PROJ_W), lambda i: (i, 0)),
                   pl.BlockSpec((tm, LANES), lambda i: (i, 0))],
        out_shape=[jax.ShapeDtypeStruct((n, PROJ_W), BF16),
                   jax.ShapeDtypeStruct((n, LANES), F32)],
        compiler_params=_cparams(),
        name="in_proj",
    )(x2, g, sh, sc, w_main, w_dt)


def _shift_rows(cur, tail, s, row8):
    rolled = pltpu.roll(cur, s, axis=0)
    top = jnp.where(row8 < s, pltpu.roll(tail, s, axis=0), rolled[0:SUBLANES])
    return jnp.concatenate([top, rolled[SUBLANES:]], axis=0)


def _conv_silu(cur_ref, tail_ref, w_ref, b_ref, out_ref, width, cw=512):
    t = cur_ref.shape[0]
    row8 = lax.broadcasted_iota(jnp.int32, (SUBLANES, cw), 0)
    for c0 in range(0, width, cw):
        sl = slice(c0, c0 + cw)
        cur = cur_ref[:, sl].astype(F32)
        tail = tail_ref[:, sl]
        acc = cur * w_ref[CONV_K - 1:CONV_K, sl] + b_ref[:, sl]
        for s in range(1, CONV_K):
            acc = acc + _shift_rows(cur, tail, s, row8) * w_ref[CONV_K - 1 - s:CONV_K - s, sl]
        out_ref[:, sl] = _silu(acc)
        tail_ref[:, sl] = cur[t - SUBLANES:t]


def _ssd_kernel(z_ref, xs_ref, bc_ref, dt_ref, cwx_ref, cwbc_ref, cbx_ref, cbbc_ref, dtb_ref, alog_ref,
                dskip_ref, ng_ref, o_ref, tailx, tailbc, state, xc, bcc, ybuf):
    q = CHUNK

    @pl.when(pl.program_id(0) == 0)
    def _():
        tailx[...] = jnp.zeros_like(tailx)
        tailbc[...] = jnp.zeros_like(tailbc)
        state[...] = jnp.zeros_like(state)

    _conv_silu(xs_ref, tailx, cwx_ref, cbx_ref, xc, D_SSD)
    _conv_silu(bc_ref, tailbc, cwbc_ref, cbbc_ref, bcc, 2 * D_BC)

    v = dt_ref[...] + dtb_ref[...]
    dt = jnp.maximum(v, 0.0) + jnp.log(1.0 + jnp.exp(-jnp.abs(v)))
    a = dt * (-jnp.exp(alog_ref[...]))
    ri = lax.broadcasted_iota(jnp.int32, (q, q), 0)
    ci = lax.broadcasted_iota(jnp.int32, (q, q), 1)
    causal = ri >= ci
    a_cs = jnp.dot(causal.astype(F32), a, preferred_element_type=F32, precision=HIGHEST)
    a_cs_t = a_cs.T
    dt_t = dt.T

    for g in range(N_GROUPS):
        b_g = bcc[:, g * D_STATE:(g + 1) * D_STATE]
        c_g = bcc[:, D_BC + g * D_STATE:D_BC + (g + 1) * D_STATE].astype(BF16)
        cb = lax.dot_general(c_g, b_g.astype(BF16), (((1,), (1,)), ((), ())),
                             preferred_element_type=F32)
        b_t = b_g.T
        st_g = state[g]
        y_off = jnp.dot(c_g, st_g.astype(BF16), preferred_element_type=F32)
        for r in range(HEADS_PER_GROUP):
            h = g * HEADS_PER_GROUP + r
            col = a_cs[:, h:h + 1]
            row = a_cs_t[h:h + 1, :]
            dtrow = dt_t[h:h + 1, :]
            decay = jnp.exp(jnp.where(causal, col - row, -jnp.inf))
            m = (cb * decay * dtrow).astype(BF16)
            xh = xc[:, h * HEADDIM:(h + 1) * HEADDIM].astype(BF16)
            y_h = jnp.dot(m, xh, preferred_element_type=F32)
            y_h = y_h + y_off[:, r * HEADDIM:(r + 1) * HEADDIM] * jnp.exp(col)
            ybuf[:, h * HEADDIM:(h + 1) * HEADDIM] = y_h
            last = a_cs_t[h:h + 1, q - 1:q]
            wrow = jnp.exp(last - row) * dtrow
            s_new = jnp.dot((b_t * wrow).astype(BF16), xh, preferred_element_type=F32)
            st_h = st_g[:, r * HEADDIM:(r + 1) * HEADDIM]
            state[g, :, r * HEADDIM:(r + 1) * HEADDIM] = st_h * jnp.exp(last) + s_new

    for g in range(N_GROUPS):
        sl = slice(g * GROUP_W, (g + 1) * GROUP_W)
        zg = z_ref[:, sl].astype(F32)
        yg = (ybuf[:, sl] + dskip_ref[:, sl] * xc[:, sl]) * _silu(zg)
        inv = lax.rsqrt(jnp.mean(yg * yg, axis=-1, keepdims=True) + EPS)
        o_ref[:, sl] = (yg * inv * ng_ref[:, sl]).astype(BF16)


def _ssd(proj, dt_raw, conv_w, conv_b, dt_bias, a_log, d_skip, norm_g):
    n = proj.shape[0]
    t = CHUNK
    return pl.pallas_call(
        _ssd_kernel,
        grid=(n // t,),
        in_specs=[pl.BlockSpec((t, D_SSD), lambda i: (i, 0)),
                  pl.BlockSpec((t, D_SSD), lambda i: (i, 1)),
                  pl.BlockSpec((t, 2 * D_BC), lambda i: (i, 2)),
                  pl.BlockSpec((t, LANES), lambda i: (i, 0)),
                  pl.BlockSpec((CONV_K, D_SSD), lambda i: (0, 0)),
                  pl.BlockSpec((CONV_K, 2 * D_BC), lambda i: (0, 1)),
                  pl.BlockSpec((1, D_SSD), lambda i: (0, 0)),
                  pl.BlockSpec((1, 2 * D_BC), lambda i: (0, 1)),
                  _full((1, LANES)), _full((1, LANES)),
                  _full((1, D_SSD)), _full((1, D_SSD))],
        out_specs=pl.BlockSpec((t, D_SSD), lambda i: (i, 0)),
        out_shape=jax.ShapeDtypeStruct((n, D_SSD), BF16),
        scratch_shapes=[pltpu.VMEM((SUBLANES, D_SSD), F32),
                        pltpu.VMEM((SUBLANES, 2 * D_BC), F32),
                        pltpu.VMEM((N_GROUPS, D_STATE, GROUP_W), F32),
                        pltpu.VMEM((t, D_SSD), F32),
                        pltpu.VMEM((t, 2 * D_BC), F32),
                        pltpu.VMEM((t, D_SSD), F32)],
        compiler_params=_cparams(),
        name="ssd",
    )(proj, proj, proj, dt_raw, conv_w, conv_w, conv_b, conv_b, dt_bias, a_log, d_skip, norm_g)


def _mix_kernel(xp_ref, gs_ref, gp_ref, yn_ref, x_ref, poolw_ref, pscale_ref, wbp_ref, wbs_ref, wout_ref,
                g1_ref, ng_ref, sh2_ref, sc2_ref, g2_ref, wr_ref, rb_ref, wsgu_ref, wsd_ref,
                base_ref, u2p_ref, e8_ref, w8_ref, pos8_ref, cnt_ref, ptail, run):
    t = x_ref.shape[0]
    i = pl.program_id(0)

    @pl.when(i == 0)
    def _():
        ptail[...] = jnp.zeros_like(ptail)
        run[...] = jnp.zeros_like(run)

    xp = xp_ref[...].astype(F32)
    ext = jnp.concatenate([ptail[...], xp], axis=0)
    ptail[...] = xp[t - 2 * SUBLANES:t]
    pos = (lax.broadcasted_iota(jnp.int32, (t, 1), 0) + (i * t + 1)).astype(F32)
    pooled = []
    for gi, w in enumerate(POOL_WINDOWS):
        sl = slice(gi * POOL_GDIM, (gi + 1) * POOL_GDIM)
        e = ext[:, sl]
        s = e
        span = 1
        while span < w:
            s = s + pltpu.roll(s, span, axis=0)
            span *= 2
        win = s[2 * SUBLANES:]
        mean = win / jnp.minimum(pos, float(w))
        pg = (mean - xp[:, sl]).astype(BF16)
        mixed = jnp.dot(pg, poolw_ref[gi], preferred_element_type=F32)
        pooled.append(mixed * pscale_ref[:, sl])
    pooled = jnp.concatenate(pooled, axis=1).astype(BF16)
    y_pool = jnp.dot(pooled, wbp_ref[...], preferred_element_type=F32)
    y_ssd = jnp.dot(yn_ref[...], wbs_ref[...], preferred_element_type=F32)
    g_ssd = jax.nn.sigmoid(gs_ref[...].astype(F32))
    g_pool = jax.nn.sigmoid(gp_ref[...].astype(F32))
    mixed = (g_ssd * y_ssd + g_pool * y_pool).astype(BF16)
    h = x_ref[...] + g1_ref[...] * jnp.dot(mixed, wout_ref[...], preferred_element_type=F32)

    inv = lax.rsqrt(jnp.mean(h * h, axis=-1, keepdims=True) + EPS)
    u2 = h * inv * ng_ref[...]
    u2 = u2 * (1.0 + sc2_ref[...]) + sh2_ref[...]
    u2b = u2.astype(BF16)
    u2r = u2b.astype(F32)
    for j in range(PACK_TILES):
        lo = lax.bitcast_convert_type(u2r[:, j * LANES:(j + 1) * LANES], jnp.uint32)
        hi = lax.bitcast_convert_type(u2r[:, (j + PACK_TILES) * LANES:(j + PACK_TILES + 1) * LANES], jnp.uint32)
        u2p_ref[:, j * LANES:(j + 1) * LANES] = (lo >> 16) | (hi & jnp.uint32(0xFFFF0000))

    hs = jnp.dot(u2b, wsgu_ref[...], preferred_element_type=F32)
    act = (_silu(hs[:, :D_SHARED]) * hs[:, D_SHARED:]).astype(BF16)
    shared = jnp.dot(act, wsd_ref[...], preferred_element_type=F32)
    base_ref[...] = h + g2_ref[...] * shared

    logits = jnp.dot(u2b, wr_ref[...], preferred_element_type=F32)
    scores = jax.nn.sigmoid(logits)
    choice = scores + rb_ref[...]
    lane = lax.broadcasted_iota(jnp.int32, (t, N_EXPERTS), 1)
    lane_f = lane.astype(F32)
    lane_grp = lane // EXPERTS_PER_GROUP
    neg = -jnp.inf
    gscore = []
    for g in range(N_EGROUPS):
        vg = jnp.where(lane_grp == g, choice, neg)
        m1 = jnp.max(vg, axis=-1, keepdims=True)
        i1 = jnp.min(jnp.where(vg == m1, lane_f, float(N_EXPERTS)), axis=-1, keepdims=True)
        m2 = jnp.max(jnp.where(lane_f == i1, neg, vg), axis=-1, keepdims=True)
        gscore.append(m1 + m2)
    gmask = jnp.zeros((t, N_EXPERTS), jnp.bool_)
    for g in range(N_EGROUPS):
        rank = jnp.zeros((t, 1), F32)
        for g2 in range(N_EGROUPS):
            if g2 == g:
                continue
            better = (gscore[g2] > gscore[g]) | ((gscore[g2] == gscore[g]) & (g2 < g))
            rank = rank + better.astype(F32)
        gmask = gmask | ((rank < float(TOPK_GROUPS)) & (lane_grp == g))
    work = jnp.where(gmask, choice, neg)
    k_iota = lax.broadcasted_iota(jnp.int32, (t, TOP_K), 1)
    e8 = jnp.zeros((t, TOP_K), jnp.int32)
    s8 = jnp.zeros((t, TOP_K), F32)
    sel = jnp.zeros((t, N_EXPERTS), jnp.bool_)
    onehots = []
    for k in range(TOP_K):
        m = jnp.max(work, axis=-1, keepdims=True)
        idx = jnp.min(jnp.where(work == m, lane_f, float(N_EXPERTS)), axis=-1, keepdims=True)
        oh = lane_f == idx
        onehots.append(oh)
        sk = jnp.sum(jnp.where(oh, scores, 0.0), axis=-1, keepdims=True)
        e8 = jnp.where(k_iota == k, idx.astype(jnp.int32), e8)
        s8 = jnp.where(k_iota == k, sk, s8)
        sel = sel | oh
        work = jnp.where(oh, neg, work)
    w8_ref[...] = s8 / jnp.sum(s8, axis=-1, keepdims=True) * ROUTED_SCALE
    e8_ref[...] = e8

    ri = lax.broadcasted_iota(jnp.int32, (t, t), 0)
    ci = lax.broadcasted_iota(jnp.int32, (t, t), 1)
    before = (ri > ci).astype(BF16)
    sel_f = jnp.where(sel, 1.0, 0.0)
    pos_tile = jnp.dot(before, sel_f.astype(BF16), preferred_element_type=F32) + run[...]
    p8 = jnp.zeros((t, TOP_K), jnp.int32)
    for k in range(TOP_K):
        pk = jnp.sum(jnp.where(onehots[k], pos_tile, 0.0), axis=-1, keepdims=True)
        p8 = jnp.where(k_iota == k, pk.astype(jnp.int32), p8)
    pos8_ref[...] = p8
    total = run[...] + jnp.sum(sel_f, axis=0, keepdims=True)
    run[...] = total
    cnt_ref[...] = total


def _mix(proj, yn, x2, pool_w, pool_scale, w_br_pool, w_br_ssd, w_out, g1, ng, sh2, sc2, g2,
         w_router, router_bias, ws_gu, ws_down):
    n = x2.shape[0]
    t = 256
    xp_blk = (D_SSD + D_SSD + 2 * D_BC) // D_MODEL
    row = lambda i: (i, 0)
    return pl.pallas_call(
        _mix_kernel,
        grid=(n // t,),
        in_specs=[pl.BlockSpec((t, D_MODEL), lambda i: (i, xp_blk)),
                  pl.BlockSpec((t, D_MODEL), lambda i: (i, xp_blk + 1)),
                  pl.BlockSpec((t, D_MODEL), lambda i: (i, xp_blk + 2)),
                  pl.BlockSpec((t, D_SSD), row),
                  pl.BlockSpec((t, D_MODEL), row),
                  _full((len(POOL_WINDOWS), POOL_GDIM, POOL_GDIM)),
                  _full((1, D_MODEL)),
                  _full((D_MODEL, D_MODEL)),
                  _full((D_SSD, D_MODEL)),
                  _full((D_MODEL, D_MODEL)),
                  _full((1, D_MODEL)), _full((1, D_MODEL)), _full((1, D_MODEL)), _full((1, D_MODEL)),
                  _full((1, D_MODEL)),
                  _full((D_MODEL, N_EXPERTS)),
                  _full((1, N_EXPERTS)),
                  _full((D_MODEL, 2 * D_SHARED)),
                  _full((D_SHARED, D_MODEL))],
        out_specs=[pl.BlockSpec((t, D_MODEL), row),
                   pl.BlockSpec((t, PACK_W), row),
                   pl.BlockSpec((t, TOP_K), row),
                   pl.BlockSpec((t, TOP_K), row),
                   pl.BlockSpec((t, TOP_K), row),
                   _full((1, N_EXPERTS))],
        out_shape=[jax.ShapeDtypeStruct((n, D_MODEL), F32),
                   jax.ShapeDtypeStruct((n, PACK_W), jnp.uint32),
                   jax.ShapeDtypeStruct((n, TOP_K), jnp.int32),
                   jax.ShapeDtypeStruct((n, TOP_K), F32),
                   jax.ShapeDtypeStruct((n, TOP_K), jnp.int32),
                   jax.ShapeDtypeStruct((1, N_EXPERTS), F32)],
        scratch_shapes=[pltpu.VMEM((2 * SUBLANES, D_MODEL), F32),
                        pltpu.VMEM((1, N_EXPERTS), F32)],
        compiler_params=_cparams(),
        name="mix",
    )(proj, proj, proj, yn, x2, pool_w, pool_scale, w_br_pool, w_br_ssd, w_out, g1, ng, sh2, sc2, g2,
      w_router, router_bias, ws_gu, ws_down)


PAD_ID = 0xFFFFFFFF


def _tile_rows(ref, first_row):
    return ref.at[pl.ds(pl.multiple_of(first_row * SUBLANES, SUBLANES), SUBLANES)]


def _dispatch_kernel(dest_ref, u_ref, xs_ref, stage, sem):
    t5 = u_ref.shape[0]
    i = pl.program_id(0)
    rows = t5 * SUBLANES

    @pl.when(i == 0)
    def _():
        stage[...] = jnp.zeros_like(stage)

    tok = lax.broadcasted_iota(jnp.int32, (t5, LANES), 0) + i * t5
    n_tok = pl.num_programs(0) * t5
    for k in range(TOP_K):
        for j in range(PACK_TILES):
            stage[pl.ds(k * rows + j, t5, stride=SUBLANES), :] = u_ref[:, j * LANES:(j + 1) * LANES]
        stage[pl.ds(k * rows + ID_ROW, t5, stride=SUBLANES), :] = (k * n_tok + tok).astype(jnp.uint32)

    base = i * (t5 * TOP_K)

    def issue(t, carry):
        for k in range(TOP_K):
            d = dest_ref[base + t * TOP_K + k]
            pltpu.make_async_copy(_tile_rows(stage, k * t5 + t), _tile_rows(xs_ref, d), sem).start()
        return carry

    lax.fori_loop(0, t5, issue, 0)
    for k in range(TOP_K):
        pltpu.make_async_copy(stage.at[pl.ds(k * rows, rows)], xs_ref.at[pl.ds(0, rows)], sem).wait()


def _dispatch(dest, u2p, n_slot_rows):
    n = u2p.shape[0]
    t5 = 256
    return pl.pallas_call(
        _dispatch_kernel,
        grid_spec=pltpu.PrefetchScalarGridSpec(
            num_scalar_prefetch=1,
            grid=(n // t5,),
            in_specs=[pl.BlockSpec((t5, PACK_W), lambda i, d: (i, 0))],
            out_specs=pl.BlockSpec(memory_space=pl.ANY),
            scratch_shapes=[pltpu.VMEM((TOP_K * t5 * SUBLANES, LANES), jnp.uint32),
                            pltpu.SemaphoreType.DMA(())]),
        out_shape=jax.ShapeDtypeStruct((n_slot_rows * SUBLANES, LANES), jnp.uint32),
        compiler_params=_cparams(),
        name="dispatch",
    )(dest, u2p)


def _padfill_kernel(cnt_ref, xs_in_ref, xs_ref, padbuf, sem, *, cap):
    del xs_in_ref
    sub = lax.broadcasted_iota(jnp.int32, padbuf.shape, 0) & (SUBLANES - 1)
    padbuf[...] = jnp.where(sub == ID_ROW, jnp.uint32(PAD_ID), jnp.uint32(0))
    sizes = [1 << bit for bit in range(EXPERT_BLOCK.bit_length() - 1)]

    def pad_copies(e, fn):
        c = cnt_ref[e]
        n_pad = (EXPERT_BLOCK - (c & (EXPERT_BLOCK - 1))) & (EXPERT_BLOCK - 1)
        first = e * cap + c
        for size in sizes:
            @pl.when((n_pad & size) != 0)
            def _():
                start = first + (n_pad & (size - 1))
                dst = xs_ref.at[pl.ds(pl.multiple_of(start * SUBLANES, SUBLANES), size * SUBLANES)]
                fn(pltpu.make_async_copy(padbuf.at[pl.ds(0, size * SUBLANES)], dst, sem))

    def issue(e, carry):
        pad_copies(e, lambda cp: cp.start())
        return carry

    lax.fori_loop(0, N_EXPERTS, issue, 0)

    def drain(e, carry):
        pad_copies(e, lambda cp: cp.wait())
        return carry

    lax.fori_loop(0, N_EXPERTS, drain, 0)


def _padfill(counts, xs, cap):
    return pl.pallas_call(
        functools.partial(_padfill_kernel, cap=cap),
        grid_spec=pltpu.PrefetchScalarGridSpec(
            num_scalar_prefetch=1,
            grid=(1,),
            in_specs=[pl.BlockSpec(memory_space=pl.ANY)],
            out_specs=pl.BlockSpec(memory_space=pl.ANY),
            scratch_shapes=[pltpu.VMEM((EXPERT_BLOCK // 2 * SUBLANES, LANES), jnp.uint32),
                            pltpu.SemaphoreType.DMA(())]),
        out_shape=jax.ShapeDtypeStruct(xs.shape, xs.dtype),
        input_output_aliases={1: 0},
        compiler_params=_cparams(),
        name="padfill",
    )(counts, xs)


def _expert_kernel(be_ref, br_ref, nb_ref, x_ref, wg_ref, wu_ref, wd_ref, ytok_ref, ybuf, idv, ids, sem, sem_ids,
                   *, n_ids):
    del be_ref, br_ref
    m = EXPERT_BLOCK
    b = pl.program_id(0)
    nb = nb_ref[0]
    slot = lax.rem(b, 2)

    def drain(s):
        pltpu.make_async_copy(ybuf.at[s], ytok_ref.at[pl.ds(0, m * SUBLANES)], sem.at[s]).wait()

    @pl.when(b < nb)
    def _():
        idrep = x_ref[pl.ds(ID_ROW, m, stride=SUBLANES), :]
        r = lax.broadcasted_iota(jnp.int32, (m, LANES), 0)
        ln = lax.broadcasted_iota(jnp.int32, (m, LANES), 1)
        idi = jnp.where(idrep == jnp.uint32(PAD_ID), n_ids + r, idrep.astype(jnp.int32))
        diag = jnp.where((r & (LANES - 1)) == ln, idi, 0).astype(F32)
        idv[...] = jnp.sum(diag.reshape(m // LANES, LANES, LANES), axis=1).astype(jnp.int32)
        ids_copy = pltpu.make_async_copy(idv, ids, sem_ids)
        ids_copy.start()

        lo, hi = [], []
        for j in range(PACK_TILES):
            w = x_ref[pl.ds(j, m, stride=SUBLANES), :]
            lo.append(lax.bitcast_convert_type(w << 16, F32).astype(BF16))
            hi.append(lax.bitcast_convert_type(w & jnp.uint32(0xFFFF0000), F32).astype(BF16))
        x = jnp.concatenate(lo + hi, axis=1)
        gate = jnp.dot(x, wg_ref[0], preferred_element_type=F32)
        up = jnp.dot(x, wu_ref[0], preferred_element_type=F32)
        hid = (_silu(gate) * up).astype(BF16)
        y = jnp.dot(hid, wd_ref[0], preferred_element_type=F32)
        for j in range(ROW_TILES):
            ybuf[slot, pl.ds(j, m, stride=SUBLANES), :] = y[:, j * LANES:(j + 1) * LANES]

        ids_copy.wait()
        unroll = 8
        for s in range(2):
            @pl.when(slot == s)
            def _():
                for a0 in range(m // LANES):
                    def issue(a1, carry):
                        for u in range(unroll):
                            col = a1 * unroll + u
                            pltpu.make_async_copy(_tile_rows(ybuf.at[s], a0 * LANES + col),
                                                  _tile_rows(ytok_ref, ids[a0, col]), sem.at[s]).start()
                        return carry

                    lax.fori_loop(0, LANES // unroll, issue, 0)

    @pl.when((b >= 1) & (b - 1 < nb))
    def _():
        drain(1 - slot)

    @pl.when((b == pl.num_programs(0) - 1) & (b < nb))
    def _():
        drain(slot)


def _experts(block_e, block_row, nb_used, xs, we_gate, we_up, we_down, n_ids):
    n_blocks = block_e.shape[0]
    m = EXPERT_BLOCK
    rows = m * SUBLANES
    return pl.pallas_call(
        functools.partial(_expert_kernel, n_ids=n_ids),
        grid_spec=pltpu.PrefetchScalarGridSpec(
            num_scalar_prefetch=3,
            grid=(n_blocks,),
            in_specs=[pl.BlockSpec((rows, LANES), lambda b, be, br, nb: (br[b], 0)),
                      pl.BlockSpec((1, D_MODEL, D_EXPERT), lambda b, be, br, nb: (be[b], 0, 0)),
                      pl.BlockSpec((1, D_MODEL, D_EXPERT), lambda b, be, br, nb: (be[b], 0, 0)),
                      pl.BlockSpec((1, D_EXPERT, D_MODEL), lambda b, be, br, nb: (be[b], 0, 0))],
            out_specs=pl.BlockSpec(memory_space=pl.ANY),
            scratch_shapes=[pltpu.VMEM((2, rows, LANES), F32),
                            pltpu.VMEM((m // LANES, LANES), jnp.int32),
                            pltpu.SMEM((m // LANES, LANES), jnp.int32),
                            pltpu.SemaphoreType.DMA((2,)),
                            pltpu.SemaphoreType.DMA(())]),
        out_shape=jax.ShapeDtypeStruct(((n_ids + m) * SUBLANES, LANES), F32),
        compiler_params=_cparams(),
        name="experts",
    )(block_e, block_row, nb_used, xs, we_gate, we_up, we_down)


def _combine_kernel(*refs):
    y_refs = refs[:TOP_K]
    w8_ref, base_ref, g2_ref, fg_ref, o_ref = refs[TOP_K:]
    t7 = base_ref.shape[0]
    w8 = w8_ref[...]
    parts = []
    for j in range(ROW_TILES):
        acc = jnp.zeros((t7, LANES), F32)
        for k in range(TOP_K):
            acc = acc + w8[:, k:k + 1] * y_refs[k][pl.ds(j, t7, stride=SUBLANES), :]
        parts.append(acc)
    routed = jnp.concatenate(parts, axis=1)
    h = base_ref[...] + g2_ref[...] * routed
    inv = lax.rsqrt(jnp.mean(h * h, axis=-1, keepdims=True) + EPS)
    o_ref[...] = h * inv * fg_ref[...]


def _combine(ytok, w8, base, g2, fg):
    n = base.shape[0]
    t7 = 128
    y_specs = [pl.BlockSpec((t7 * SUBLANES, LANES), functools.partial(lambda i, k: (k * (n // t7) + i, 0), k=k))
               for k in range(TOP_K)]
    return pl.pallas_call(
        _combine_kernel,
        grid=(n // t7,),
        in_specs=y_specs + [
                  pl.BlockSpec((t7, TOP_K), lambda i: (i, 0)),
                  pl.BlockSpec((t7, D_MODEL), lambda i: (i, 0)),
                  _full((1, D_MODEL)), _full((1, D_MODEL))],
        out_specs=pl.BlockSpec((t7, D_MODEL), lambda i: (i, 0)),
        out_shape=jax.ShapeDtypeStruct((n, D_MODEL), F32),
        compiler_params=_cparams(),
        name="combine",
    )(*([ytok] * TOP_K), w8, base, g2, fg)


def kernel(x, c, w_ada, b_ada, norm_mix_g, w_in, conv_w, conv_b, dt_bias, A_log, D_skip, ssd_norm_g, pool_w,
           pool_scale, w_br_ssd, w_br_pool, w_out, norm_ffn_g, w_router, router_bias, we_gate, we_up, we_down,
           ws_gate, ws_up, ws_down, final_norm_g):
    bsz, seq, _ = x.shape
    assert bsz == 1 and w_ada.shape[0] == 1
    n = seq
    h2 = x.reshape(n, D_MODEL)

    mod = _ada(c, w_ada[0], b_ada[0])
    sh1, sc1, g1, sh2, sc2, g2 = [mod[:, k * D_MODEL:(k + 1) * D_MODEL] for k in range(6)]

    wi = w_in[0]
    o_dt = D_SSD + D_SSD + 2 * D_BC
    w_main = jnp.concatenate([wi[:, :o_dt], wi[:, o_dt + N_HEADS:]], axis=1).astype(BF16)
    w_dt = jnp.pad(wi[:, o_dt:o_dt + N_HEADS], ((0, 0), (0, LANES - N_HEADS)))
    proj, dt_raw = _in_proj(h2, norm_mix_g[0][None], sh1, sc1, w_main, w_dt)

    pad_h = lambda v: jnp.pad(v[None], ((0, 0), (0, LANES - N_HEADS)))
    yn = _ssd(proj, dt_raw, conv_w[0], conv_b[0][None], pad_h(dt_bias[0]), pad_h(A_log[0]),
              jnp.repeat(D_skip[0], HEADDIM)[None], ssd_norm_g[0][None])

    ws_gu = jnp.concatenate([ws_gate[0], ws_up[0]], axis=1).astype(BF16)
    base, u2p, e8, w8, pos8, cnt = _mix(
        proj, yn, h2, pool_w[0].astype(BF16), pool_scale[0][None], w_br_pool[0].astype(BF16),
        w_br_ssd[0].astype(BF16), w_out[0].astype(BF16), g1, norm_ffn_g[0][None], sh2, sc2, g2,
        w_router[0].astype(BF16), router_bias[0][None], ws_gu, ws_down[0].astype(BF16))

    m = EXPERT_BLOCK
    cap = -(-n // m) * m
    counts = cnt[0].astype(jnp.int32)
    dest = (e8 * cap + pos8).reshape(n * TOP_K)
    nblk = (counts + m - 1) // m
    bends = jnp.cumsum(nblk)
    bstarts = bends - nblk
    nb_used = bends[-1]
    n_blocks = -(-(n * TOP_K) // m) + N_EXPERTS
    b_eff = jnp.minimum(jnp.arange(n_blocks, dtype=jnp.int32), nb_used - 1)
    block_e = jnp.sum((bends[None, :] <= b_eff[:, None]).astype(jnp.int32), axis=1)
    block_row = block_e * (cap // m) + b_eff - jnp.take(bstarts, block_e)

    xs = _dispatch(dest, u2p, N_EXPERTS * cap)
    xs = _padfill(counts, xs, cap)
    ytok = _experts(block_e, block_row, nb_used.reshape(1), xs,
                    we_gate[0].astype(BF16), we_up[0].astype(BF16), we_down[0].astype(BF16), n * TOP_K)
    out = _combine(ytok, w8, base, g2, final_norm_g[None])
    return out.reshape(bsz, seq, D_MODEL)
```

```python
import functools

import jax
import jax.numpy as jnp
from jax import lax
from jax.experimental import pallas as pl
from jax.experimental.pallas import tpu as pltpu

F32 = jnp.float32
BF16 = jnp.bfloat16
HIGHEST = lax.Precision.HIGHEST

D_MODEL = 1024
D_SSD = 2048
HEADDIM = 64
N_HEADS = 32
N_GROUPS = 8
HEADS_PER_GROUP = N_HEADS // N_GROUPS
D_STATE = 128
CONV_K = 4
CHUNK = 128
GROUP_W = D_SSD // N_GROUPS
D_BC = N_GROUPS * D_STATE
POOL_WINDOWS = (2, 4, 8, 16)
POOL_GDIM = 256
N_EXPERTS = 64
TOP_K = 8
N_EGROUPS = 8
EXPERTS_PER_GROUP = 8
TOPK_GROUPS = 4
D_EXPERT = 256
D_SHARED = 256
ROUTED_SCALE = 2.5
MOE_BLOCK = 128
EPS = 1e-6

LANES = 128
SUBLANES = 8
ROW_TILES = D_MODEL // LANES
PACK_TILES = ROW_TILES // 2
PACK_W = PACK_TILES * LANES
ID_ROW = PACK_TILES
EXPERT_BLOCK = 512

PROJ_W = D_SSD + D_SSD + 2 * D_BC + 3 * D_MODEL
PROJ_CHUNK = 512

VMEM_LIMIT = 56 * 1024 * 1024


def _cparams(sem=("arbitrary",)):
    return pltpu.CompilerParams(dimension_semantics=sem, vmem_limit_bytes=VMEM_LIMIT)


def _full(shape):
    nd = len(shape)
    return pl.BlockSpec(shape, lambda *_: (0,) * nd)


def _silu(v):
    return v * jax.nn.sigmoid(v)


def _ada_kernel(c_ref, w_ref, b_ref, o_ref):
    c = c_ref[...]
    o_ref[...] = jnp.dot(_silu(c), w_ref[...], preferred_element_type=F32, precision=HIGHEST) + b_ref[...]


def _ada(c, w_ada, b_ada):
    n_out = w_ada.shape[1]
    tn = 1536
    c8 = jnp.broadcast_to(c, (SUBLANES, D_MODEL))
    out = pl.pallas_call(
        _ada_kernel,
        grid=(n_out // tn,),
        in_specs=[_full((SUBLANES, D_MODEL)),
                  pl.BlockSpec((D_MODEL, tn), lambda j: (0, j)),
                  pl.BlockSpec((1, tn), lambda j: (0, j))],
        out_specs=pl.BlockSpec((SUBLANES, tn), lambda j: (0, j)),
        out_shape=jax.ShapeDtypeStruct((SUBLANES, n_out), F32),
        compiler_params=_cparams(),
        name="ada",
    )(c8, w_ada, b_ada.reshape(1, n_out))
    return out[0:1]


def _inproj_kernel(x_ref, g_ref, sh_ref, sc_ref, w_ref, wdt_ref, proj_ref, dt_ref):
    x = x_ref[...]
    inv = lax.rsqrt(jnp.mean(x * x, axis=-1, keepdims=True) + EPS)
    u = x * inv * g_ref[...]
    u = u * (1.0 + sc_ref[...]) + sh_ref[...]
    ub = u.astype(BF16)
    for c0 in range(0, PROJ_W, PROJ_CHUNK):
        proj_ref[:, c0:c0 + PROJ_CHUNK] = jnp.dot(
            ub, w_ref[:, c0:c0 + PROJ_CHUNK], preferred_element_type=F32).astype(BF16)
    dt_ref[...] = jnp.dot(u, wdt_ref[...], preferred_element_type=F32, precision=HIGHEST)


def _in_proj(x2, g, sh, sc, w_main, w_dt):
    n = x2.shape[0]
    tm = 256
    return pl.pallas_call(
        _inproj_kernel,
        grid=(n // tm,),
        in_specs=[pl.BlockSpec((tm, D_MODEL), lambda i: (i, 0)),
                  _full((1, D_MODEL)), _full((1, D_MODEL)), _full((1, D_MODEL)),
                  pl.BlockSpec((D_MODEL, PROJ_W), lambda i: (0, 0), pipeline_mode=pl.Buffered(1)),
                  _full((D_MODEL, LANES))],
        out_specs=[pl.BlockSpec((tm, PROJ_W), lambda i: (i, 0)),
                   pl.BlockSpec((tm, LANES), lambda i: (i, 0))],
        out_shape=[jax.ShapeDtypeStruct((n, PROJ_W), BF16),
                   jax.ShapeDtypeStruct((n, LANES), F32)],
        compiler_params=_cparams(),
        name="in_proj",
    )(x2, g, sh, sc, w_main, w_dt)


def _shift_rows(cur, tail, s, row8):
    rolled = pltpu.roll(cur, s, axis=0)
    top = jnp.where(row8 < s, pltpu.roll(tail, s, axis=0), rolled[0:SUBLANES])
    return jnp.concatenate([top, rolled[SUBLANES:]], axis=0)


def _conv_silu(cur_ref, tail_ref, w_ref, b_ref, out_ref, width, cw=512):
    t = cur_ref.shape[0]
    row8 = lax.broadcasted_iota(jnp.int32, (SUBLANES, cw), 0)
    for c0 in range(0, width, cw):
        sl = slice(c0, c0 + cw)
        cur = cur_ref[:, sl].astype(F32)
        tail = tail_ref[:, sl]
        acc = cur * w_ref[CONV_K - 1:CONV_K, sl] + b_ref[:, sl]
        for s in range(1, CONV_K):
            acc = acc + _shift_rows(cur, tail, s, row8) * w_ref[CONV_K - 1 - s:CONV_K - s, sl]
        out_ref[:, sl] = _silu(acc)
        tail_ref[:, sl] = cur[t - SUBLANES:t]


def _ssd_kernel(z_ref, xs_ref, bc_ref, dt_ref, cwx_ref, cwbc_ref, cbx_ref, cbbc_ref, dtb_ref, alog_ref,
                dskip_ref, ng_ref, o_ref, tailx, tailbc, state, xc, bcc, ybuf):
    q = CHUNK

    @pl.when(pl.program_id(0) == 0)
    def _():
        tailx[...] = jnp.zeros_like(tailx)
        tailbc[...] = jnp.zeros_like(tailbc)
        state[...] = jnp.zeros_like(state)

    _conv_silu(xs_ref, tailx, cwx_ref, cbx_ref, xc, D_SSD)
    _conv_silu(bc_ref, tailbc, cwbc_ref, cbbc_ref, bcc, 2 * D_BC)

    v = dt_ref[...] + dtb_ref[...]
    dt = jnp.maximum(v, 0.0) + jnp.log(1.0 + jnp.exp(-jnp.abs(v)))
    a = dt * (-jnp.exp(alog_ref[...]))
    ri = lax.broadcasted_iota(jnp.int32, (q, q), 0)
    ci = lax.broadcasted_iota(jnp.int32, (q, q), 1)
    causal = ri >= ci
    a_cs = jnp.dot(causal.astype(F32), a, preferred_element_type=F32, precision=HIGHEST)
    a_cs_t = a_cs.T
    dt_t = dt.T

    for g in range(N_GROUPS):
        b_g = bcc[:, g * D_STATE:(g + 1) * D_STATE]
        c_g = bcc[:, D_BC + g * D_STATE:D_BC + (g + 1) * D_STATE].astype(BF16)
        cb = lax.dot_general(c_g, b_g.astype(BF16), (((1,), (1,)), ((), ())),
                             preferred_element_type=F32)
        b_t = b_g.T
        st_g = state[g]
        y_off = jnp.dot(c_g, st_g.astype(BF16), preferred_element_type=F32)
        for r in range(HEADS_PER_GROUP):
            h = g * HEADS_PER_GROUP + r
            col = a_cs[:, h:h + 1]
            row = a_cs_t[h:h + 1, :]
            dtrow = dt_t[h:h + 1, :]
            decay = jnp.exp(jnp.where(causal, col - row, -jnp.inf))
            m = (cb * decay * dtrow).astype(BF16)
            xh = xc[:, h * HEADDIM:(h + 1) * HEADDIM].astype(BF16)
            y_h = jnp.dot(m, xh, preferred_element_type=F32)
            y_h = y_h + y_off[:, r * HEADDIM:(r + 1) * HEADDIM] * jnp.exp(col)
            ybuf[:, h * HEADDIM:(h + 1) * HEADDIM] = y_h
            last = a_cs_t[h:h + 1, q - 1:q]
            wrow = jnp.exp(last - row) * dtrow
            s_new = jnp.dot((b_t * wrow).astype(BF16), xh, preferred_element_type=F32)
            st_h = st_g[:, r * HEADDIM:(r + 1) * HEADDIM]
            state[g, :, r * HEADDIM:(r + 1) * HEADDIM] = st_h * jnp.exp(last) + s_new

    for g in range(N_GROUPS):
        sl = slice(g * GROUP_W, (g + 1) * GROUP_W)
        zg = z_ref[:, sl].astype(F32)
        yg = (ybuf[:, sl] + dskip_ref[:, sl] * xc[:, sl]) * _silu(zg)
        inv = lax.rsqrt(jnp.mean(yg * yg, axis=-1, keepdims=True) + EPS)
        o_ref[:, sl] = (yg * inv * ng_ref[:, sl]).astype(BF16)


def _ssd(proj, dt_raw, conv_w, conv_b, dt_bias, a_log, d_skip, norm_g):
    n = proj.shape[0]
    t = CHUNK
    return pl.pallas_call(
        _ssd_kernel,
        grid=(n // t,),
        in_specs=[pl.BlockSpec((t, D_SSD), lambda i: (i, 0)),
                  pl.BlockSpec((t, D_SSD), lambda i: (i, 1)),
                  pl.BlockSpec((t, 2 * D_BC), lambda i: (i, 2)),
                  pl.BlockSpec((t, LANES), lambda i: (i, 0)),
                  pl.BlockSpec((CONV_K, D_SSD), lambda i: (0, 0)),
                  pl.BlockSpec((CONV_K, 2 * D_BC), lambda i: (0, 1)),
                  pl.BlockSpec((1, D_SSD), lambda i: (0, 0)),
                  pl.BlockSpec((1, 2 * D_BC), lambda i: (0, 1)),
                  _full((1, LANES)), _full((1, LANES)),
                  _full((1, D_SSD)), _full((1, D_SSD))],
        out_specs=pl.BlockSpec((t, D_SSD), lambda i: (i, 0)),
        out_shape=jax.ShapeDtypeStruct((n, D_SSD), BF16),
        scratch_shapes=[pltpu.VMEM((SUBLANES, D_SSD), F32),
                        pltpu.VMEM((SUBLANES, 2 * D_BC), F32),
                        pltpu.VMEM((N_GROUPS, D_STATE, GROUP_W), F32),
                        pltpu.VMEM((t, D_SSD), F32),
                        pltpu.VMEM((t, 2 * D_BC), F32),
                        pltpu.VMEM((t, D_SSD), F32)],
        compiler_params=_cparams(),
        name="ssd",
    )(proj, proj, proj, dt_raw, conv_w, conv_w, conv_b, conv_b, dt_bias, a_log, d_skip, norm_g)


def _mix_kernel(xp_ref, gs_ref, gp_ref, yn_ref, x_ref, poolw_ref, pscale_ref, wbp_ref, wbs_ref, wout_ref,
                g1_ref, ng_ref, sh2_ref, sc2_ref, g2_ref, wr_ref, rb_ref, wsgu_ref, wsd_ref,
                base_ref, u2p_ref, e8_ref, w8_ref, pos8_ref, cnt_ref, ptail, run):
    t = x_ref.shape[0]
    i = pl.program_id(0)

    @pl.when(i == 0)
    def _():
        ptail[...] = jnp.zeros_like(ptail)
        run[...] = jnp.zeros_like(run)

    xp = xp_ref[...].astype(F32)
    ext = jnp.concatenate([ptail[...], xp], axis=0)
    ptail[...] = xp[t - 2 * SUBLANES:t]
    pos = (lax.broadcasted_iota(jnp.int32, (t, 1), 0) + (i * t + 1)).astype(F32)
    pooled = []
    for gi, w in enumerate(POOL_WINDOWS):
        sl = slice(gi * POOL_GDIM, (gi + 1) * POOL_GDIM)
        e = ext[:, sl]
        s = e
        span = 1
        while span < w:
            s = s + pltpu.roll(s, span, axis=0)
            span *= 2
        win = s[2 * SUBLANES:]
        mean = win / jnp.minimum(pos, float(w))
        pg = (mean - xp[:, sl]).astype(BF16)
        mixed = jnp.dot(pg, poolw_ref[gi], preferred_element_type=F32)
        pooled.append(mixed * pscale_ref[:, sl])
    pooled = jnp.concatenate(pooled, axis=1).astype(BF16)
    y_pool = jnp.dot(pooled, wbp_ref[...], preferred_element_type=F32)
    y_ssd = jnp.dot(yn_ref[...], wbs_ref[...], preferred_element_type=F32)
    g_ssd = jax.nn.sigmoid(gs_ref[...].astype(F32))
    g_pool = jax.nn.sigmoid(gp_ref[...].astype(F32))
    mixed = (g_ssd * y_ssd + g_pool * y_pool).astype(BF16)
    h = x_ref[...] + g1_ref[...] * jnp.dot(mixed, wout_ref[...], preferred_element_type=F32)

    inv = lax.rsqrt(jnp.mean(h * h, axis=-1, keepdims=True) + EPS)
    u2 = h * inv * ng_ref[...]
    u2 = u2 * (1.0 + sc2_ref[...]) + sh2_ref[...]
    u2b = u2.astype(BF16)
    u2r = u2b.astype(F32)
    for j in range(PACK_TILES):
        lo = lax.bitcast_convert_type(u2r[:, j * LANES:(j + 1) * LANES], jnp.uint32)
        hi = lax.bitcast_convert_type(u2r[:, (j + PACK_TILES) * LANES:(j + PACK_TILES + 1) * LANES], jnp.uint32)
        u2p_ref[:, j * LANES:(j + 1) * LANES] = (lo >> 16) | (hi & jnp.uint32(0xFFFF0000))

    hs = jnp.dot(u2b, wsgu_ref[...], preferred_element_type=F32)
    act = (_silu(hs[:, :D_SHARED]) * hs[:, D_SHARED:]).astype(BF16)
    shared = jnp.dot(act, wsd_ref[...], preferred_element_type=F32)
    base_ref[...] = h + g2_ref[...] * shared

    logits = jnp.dot(u2b, wr_ref[...], preferred_element_type=F32)
    scores = jax.nn.sigmoid(logits)
    choice = scores + rb_ref[...]
    lane = lax.broadcasted_iota(jnp.int32, (t, N_EXPERTS), 1)
    lane_f = lane.astype(F32)
    lane_grp = lane // EXPERTS_PER_GROUP
    neg = -jnp.inf
    gscore = []
    for g in range(N_EGROUPS):
        vg = jnp.where(lane_grp == g, choice, neg)
        m1 = jnp.max(vg, axis=-1, keepdims=True)
        i1 = jnp.min(jnp.where(vg == m1, lane_f, float(N_EXPERTS)), axis=-1, keepdims=True)
        m2 = jnp.max(jnp.where(lane_f == i1, neg, vg), axis=-1, keepdims=True)
        gscore.append(m1 + m2)
    gmask = jnp.zeros((t, N_EXPERTS), jnp.bool_)
    for g in range(N_EGROUPS):
        rank = jnp.zeros((t, 1), F32)
        for g2 in range(N_EGROUPS):
            if g2 == g:
                continue
            better = (gscore[g2] > gscore[g]) | ((gscore[g2] == gscore[g]) & (g2 < g))
            rank = rank + better.astype(F32)
        gmask = gmask | ((rank < float(TOPK_GROUPS)) & (lane_grp == g))
    work = jnp.where(gmask, choice, neg)
    k_iota = lax.broadcasted_iota(jnp.int32, (t, TOP_K), 1)
    e8 = jnp.zeros((t, TOP_K), jnp.int32)
    s8 = jnp.zeros((t, TOP_K), F32)
    sel = jnp.zeros((t, N_EXPERTS), jnp.bool_)
    onehots = []
    for k in range(TOP_K):
        m = jnp.max(work, axis=-1, keepdims=True)
        idx = jnp.min(jnp.where(work == m, lane_f, float(N_EXPERTS)), axis=-1, keepdims=True)
        oh = lane_f == idx
        onehots.append(oh)
        sk = jnp.sum(jnp.where(oh, scores, 0.0), axis=-1, keepdims=True)
        e8 = jnp.where(k_iota == k, idx.astype(jnp.int32), e8)
        s8 = jnp.where(k_iota == k, sk, s8)
        sel = sel | oh
        work = jnp.where(oh, neg, work)
    w8_ref[...] = s8 / jnp.sum(s8, axis=-1, keepdims=True) * ROUTED_SCALE
    e8_ref[...] = e8

    ri = lax.broadcasted_iota(jnp.int32, (t, t), 0)
    ci = lax.broadcasted_iota(jnp.int32, (t, t), 1)
    before = (ri > ci).astype(BF16)
    sel_f = jnp.where(sel, 1.0, 0.0)
    pos_tile = jnp.dot(before, sel_f.astype(BF16), preferred_element_type=F32) + run[...]
    p8 = jnp.zeros((t, TOP_K), jnp.int32)
    for k in range(TOP_K):
        pk = jnp.sum(jnp.where(onehots[k], pos_tile, 0.0), axis=-1, keepdims=True)
        p8 = jnp.where(k_iota == k, pk.astype(jnp.int32), p8)
    pos8_ref[...] = p8
    total = run[...] + jnp.sum(sel_f, axis=0, keepdims=True)
    run[...] = total
    cnt_ref[...] = total


def _mix(proj, yn, x2, pool_w, pool_scale, w_br_pool, w_br_ssd, w_out, g1, ng, sh2, sc2, g2,
         w_router, router_bias, ws_gu, ws_down):
    n = x2.shape[0]
    t = 256
    xp_blk = (D_SSD + D_SSD + 2 * D_BC) // D_MODEL
    row = lambda i: (i, 0)
    return pl.pallas_call(
        _mix_kernel,
        grid=(n // t,),
        in_specs=[pl.BlockSpec((t, D_MODEL), lambda i: (i, xp_blk)),
                  pl.BlockSpec((t, D_MODEL), lambda i: (i, xp_blk + 1)),
                  pl.BlockSpec((t, D_MODEL), lambda i: (i, xp_blk + 2)),
                  pl.BlockSpec((t, D_SSD), row),
                  pl.BlockSpec((t, D_MODEL), row),
                  _full((len(POOL_WINDOWS), POOL_GDIM, POOL_GDIM)),
                  _full((1, D_MODEL)),
                  _full((D_MODEL, D_MODEL)),
                  _full((D_SSD, D_MODEL)),
                  _full((D_MODEL, D_MODEL)),
                  _full((1, D_MODEL)), _full((1, D_MODEL)), _full((1, D_MODEL)), _full((1, D_MODEL)),
                  _full((1, D_MODEL)),
                  _full((D_MODEL, N_EXPERTS)),
                  _full((1, N_EXPERTS)),
                  _full((D_MODEL, 2 * D_SHARED)),
                  _full((D_SHARED, D_MODEL))],
        out_specs=[pl.BlockSpec((t, D_MODEL), row),
                   pl.BlockSpec((t, PACK_W), row),
                   pl.BlockSpec((t, TOP_K), row),
                   pl.BlockSpec((t, TOP_K), row),
                   pl.BlockSpec((t, TOP_K), row),
                   _full((1, N_EXPERTS))],
        out_shape=[jax.ShapeDtypeStruct((n, D_MODEL), F32),
                   jax.ShapeDtypeStruct((n, PACK_W), jnp.uint32),
                   jax.ShapeDtypeStruct((n, TOP_K), jnp.int32),
                   jax.ShapeDtypeStruct((n, TOP_K), F32),
                   jax.ShapeDtypeStruct((n, TOP_K), jnp.int32),
                   jax.ShapeDtypeStruct((1, N_EXPERTS), F32)],
        scratch_shapes=[pltpu.VMEM((2 * SUBLANES, D_MODEL), F32),
                        pltpu.VMEM((1, N_EXPERTS), F32)],
        compiler_params=_cparams(),
        name="mix",
    )(proj, proj, proj, yn, x2, pool_w, pool_scale, w_br_pool, w_br_ssd, w_out, g1, ng, sh2, sc2, g2,
      w_router, router_bias, ws_gu, ws_down)


PAD_ID = 0xFFFFFFFF


def _tile_rows(ref, first_row):
    return ref.at[pl.ds(pl.multiple_of(first_row * SUBLANES, SUBLANES), SUBLANES)]


def _dispatch_kernel(dest_ref, u_ref, xs_ref, stage, sem):
    t5 = u_ref.shape[0]
    i = pl.program_id(0)
    rows = t5 * SUBLANES

    @pl.when(i == 0)
    def _():
        stage[...] = jnp.zeros_like(stage)

    tok = lax.broadcasted_iota(jnp.int32, (t5, LANES), 0) + i * t5
    n_tok = pl.num_programs(0) * t5
    for k in range(TOP_K):
        for j in range(PACK_TILES):
            stage[pl.ds(k * rows + j, t5, stride=SUBLANES), :] = u_ref[:, j * LANES:(j + 1) * LANES]
        stage[pl.ds(k * rows + ID_ROW, t5, stride=SUBLANES), :] = (k * n_tok + tok).astype(jnp.uint32)

    base = i * (t5 * TOP_K)

    def issue(t, carry):
        for k in range(TOP_K):
            d = dest_ref[base + t * TOP_K + k]
            pltpu.make_async_copy(_tile_rows(stage, k * t5 + t), _tile_rows(xs_ref, d), sem).start()
        return carry

    lax.fori_loop(0, t5, issue, 0)
    for k in range(TOP_K):
        pltpu.make_async_copy(stage.at[pl.ds(k * rows, rows)], xs_ref.at[pl.ds(0, rows)], sem).wait()


def _dispatch(dest, u2p, n_slot_rows):
    n = u2p.shape[0]
    t5 = 256
    return pl.pallas_call(
        _dispatch_kernel,
        grid_spec=pltpu.PrefetchScalarGridSpec(
            num_scalar_prefetch=1,
            grid=(n // t5,),
            in_specs=[pl.BlockSpec((t5, PACK_W), lambda i, d: (i, 0))],
            out_specs=pl.BlockSpec(memory_space=pl.ANY),
            scratch_shapes=[pltpu.VMEM((TOP_K * t5 * SUBLANES, LANES), jnp.uint32),
                            pltpu.SemaphoreType.DMA(())]),
        out_shape=jax.ShapeDtypeStruct((n_slot_rows * SUBLANES, LANES), jnp.uint32),
        compiler_params=_cparams(),
        name="dispatch",
    )(dest, u2p)


def _padfill_kernel(cnt_ref, xs_in_ref, xs_ref, padbuf, sem, *, cap):
    del xs_in_ref
    sub = lax.broadcasted_iota(jnp.int32, padbuf.shape, 0) & (SUBLANES - 1)
    padbuf[...] = jnp.where(sub == ID_ROW, jnp.uint32(PAD_ID), jnp.uint32(0))
    sizes = [1 << bit for bit in range(EXPERT_BLOCK.bit_length() - 1)]

    def pad_copies(e, fn):
        c = cnt_ref[e]
        n_pad = (EXPERT_BLOCK - (c & (EXPERT_BLOCK - 1))) & (EXPERT_BLOCK - 1)
        first = e * cap + c
        for size in sizes:
            @pl.when((n_pad & size) != 0)
            def _():
                start = first + (n_pad & (size - 1))
                dst = xs_ref.at[pl.ds(pl.multiple_of(start * SUBLANES, SUBLANES), size * SUBLANES)]
                fn(pltpu.make_async_copy(padbuf.at[pl.ds(0, size * SUBLANES)], dst, sem))

    def issue(e, carry):
        pad_copies(e, lambda cp: cp.start())
        return carry

    lax.fori_loop(0, N_EXPERTS, issue, 0)

    def drain(e, carry):
        pad_copies(e, lambda cp: cp.wait())
        return carry

    lax.fori_loop(0, N_EXPERTS, drain, 0)


def _padfill(counts, xs, cap):
    return pl.pallas_call(
        functools.partial(_padfill_kernel, cap=cap),
        grid_spec=pltpu.PrefetchScalarGridSpec(
            num_scalar_prefetch=1,
            grid=(1,),
            in_specs=[pl.BlockSpec(memory_space=pl.ANY)],
            out_specs=pl.BlockSpec(memory_space=pl.ANY),
            scratch_shapes=[pltpu.VMEM((EXPERT_BLOCK // 2 * SUBLANES, LANES), jnp.uint32),
                            pltpu.SemaphoreType.DMA(())]),
        out_shape=jax.ShapeDtypeStruct(xs.shape, xs.dtype),
        input_output_aliases={1: 0},
        compiler_params=_cparams(),
        name="padfill",
    )(counts, xs)


def _expert_kernel(be_ref, br_ref, nb_ref, x_ref, wg_ref, wu_ref, wd_ref, ytok_ref, ybuf0, ybuf1, idv, ids0, ids1,
                   wgb, wub, wdb, sem, sem_ids, *, n_ids):
    del br_ref
    m = EXPERT_BLOCK
    b = pl.program_id(0)
    nb = nb_ref[0]
    ybufs, idss = (ybuf0, ybuf1), (ids0, ids1)
    n_phases = m // LANES
    half = D_MODEL // 2

    @pl.when((b == 0) | (be_ref[b] != be_ref[jnp.maximum(b - 1, 0)]))
    def _():
        wgb[...] = wg_ref[0].astype(BF16)
        wub[...] = wu_ref[0].astype(BF16)
        wdb[...] = wd_ref[0].astype(BF16)

    def issue(s, a0):
        unroll = 8

        def body(a1, carry):
            for u in range(unroll):
                col = a1 * unroll + u
                pltpu.make_async_copy(_tile_rows(ybufs[s], a0 * LANES + col),
                                      _tile_rows(ytok_ref, idss[s][a0, col]), sem.at[s]).start()
            return carry

        lax.fori_loop(0, LANES // unroll, body, 0)

    def step(s, prev):
        idrep = x_ref[pl.ds(ID_ROW, m, stride=SUBLANES), :]
        r = lax.broadcasted_iota(jnp.int32, (m, LANES), 0)
        ln = lax.broadcasted_iota(jnp.int32, (m, LANES), 1)
        idi = jnp.where(idrep == jnp.uint32(PAD_ID), n_ids + r, idrep.astype(jnp.int32))
        diag = jnp.where((r & (LANES - 1)) == ln, idi, 0).astype(F32)
        idv[...] = jnp.sum(diag.reshape(m // LANES, LANES, LANES), axis=1).astype(jnp.int32)
        ids_copy = pltpu.make_async_copy(idv, idss[s], sem_ids)
        ids_copy.start()

        lo, hi = [], []
        for j in range(PACK_TILES):
            w = x_ref[pl.ds(j, m, stride=SUBLANES), :]
            lo.append(lax.bitcast_convert_type(w << 16, F32).astype(BF16))
            hi.append(lax.bitcast_convert_type(w & jnp.uint32(0xFFFF0000), F32).astype(BF16))
        x = jnp.concatenate(lo + hi, axis=1)
        phases = iter(range(n_phases))
        if prev is not None:
            issue(prev, next(phases))
        gate = jnp.dot(x, wgb[...], preferred_element_type=F32)
        if prev is not None:
            issue(prev, next(phases))
        up = jnp.dot(x, wub[...], preferred_element_type=F32)
        hid = (_silu(gate) * up).astype(BF16)
        for c0 in (0, half):
            if prev is not None:
                issue(prev, next(phases))
            y = jnp.dot(hid, wdb[:, c0:c0 + half], preferred_element_type=F32)
            for j in range(half // LANES):
                ybufs[s][pl.ds(c0 // LANES + j, m, stride=SUBLANES), :] = y[:, j * LANES:(j + 1) * LANES]
        ids_copy.wait()
        if prev is not None:
            for a0 in phases:
                issue(prev, a0)
            pltpu.make_async_copy(ybufs[prev], ytok_ref.at[pl.ds(0, m * SUBLANES)], sem.at[prev]).wait()

    @pl.when(b == 0)
    def _():
        step(0, None)

    for s in range(2):
        @pl.when((b >= 1) & (b <= nb) & (lax.rem(b, 2) == s))
        def _():
            step(s, 1 - s)


def _experts(block_e, block_row, nb_used, xs, we_gate, we_up, we_down, n_ids):
    n_blocks = block_e.shape[0]
    m = EXPERT_BLOCK
    rows = m * SUBLANES
    return pl.pallas_call(
        functools.partial(_expert_kernel, n_ids=n_ids),
        grid_spec=pltpu.PrefetchScalarGridSpec(
            num_scalar_prefetch=3,
            grid=(n_blocks,),
            in_specs=[pl.BlockSpec((rows, LANES), lambda b, be, br, nb: (br[b], 0)),
                      pl.BlockSpec((1, D_MODEL, D_EXPERT), lambda b, be, br, nb: (be[b], 0, 0)),
                      pl.BlockSpec((1, D_MODEL, D_EXPERT), lambda b, be, br, nb: (be[b], 0, 0)),
                      pl.BlockSpec((1, D_EXPERT, D_MODEL), lambda b, be, br, nb: (be[b], 0, 0))],
            out_specs=pl.BlockSpec(memory_space=pl.ANY),
            scratch_shapes=[pltpu.VMEM((rows, LANES), F32),
                            pltpu.VMEM((rows, LANES), F32),
                            pltpu.VMEM((m // LANES, LANES), jnp.int32),
                            pltpu.SMEM((m // LANES, LANES), jnp.int32),
                            pltpu.SMEM((m // LANES, LANES), jnp.int32),
                            pltpu.VMEM((D_MODEL, D_EXPERT), BF16),
                            pltpu.VMEM((D_MODEL, D_EXPERT), BF16),
                            pltpu.VMEM((D_EXPERT, D_MODEL), BF16),
                            pltpu.SemaphoreType.DMA((2,)),
                            pltpu.SemaphoreType.DMA(())]),
        out_shape=jax.ShapeDtypeStruct(((n_ids + m) * SUBLANES, LANES), F32),
        compiler_params=_cparams(),
        name="experts",
    )(block_e, block_row, nb_used, xs, we_gate, we_up, we_down)


def _combine_kernel(*refs):
    y_refs = refs[:TOP_K]
    w8_ref, base_ref, g2_ref, fg_ref, o_ref = refs[TOP_K:]
    t7 = base_ref.shape[0]
    w8 = w8_ref[...]
    parts = []
    for j in range(ROW_TILES):
        acc = jnp.zeros((t7, LANES), F32)
        for k in range(TOP_K):
            acc = acc + w8[:, k:k + 1] * y_refs[k][pl.ds(j, t7, stride=SUBLANES), :]
        parts.append(acc)
    routed = jnp.concatenate(parts, axis=1)
    h = base_ref[...] + g2_ref[...] * routed
    inv = lax.rsqrt(jnp.mean(h * h, axis=-1, keepdims=True) + EPS)
    o_ref[...] = h * inv * fg_ref[...]


def _combine(ytok, w8, base, g2, fg):
    n = base.shape[0]
    t7 = 128
    y_specs = [pl.BlockSpec((t7 * SUBLANES, LANES), functools.partial(lambda i, k: (k * (n // t7) + i, 0), k=k))
               for k in range(TOP_K)]
    return pl.pallas_call(
        _combine_kernel,
        grid=(n // t7,),
        in_specs=y_specs + [
                  pl.BlockSpec((t7, TOP_K), lambda i: (i, 0)),
                  pl.BlockSpec((t7, D_MODEL), lambda i: (i, 0)),
                  _full((1, D_MODEL)), _full((1, D_MODEL))],
        out_specs=pl.BlockSpec((t7, D_MODEL), lambda i: (i, 0)),
        out_shape=jax.ShapeDtypeStruct((n, D_MODEL), F32),
        compiler_params=_cparams(),
        name="combine",
    )(*([ytok] * TOP_K), w8, base, g2, fg)


def kernel(x, c, w_ada, b_ada, norm_mix_g, w_in, conv_w, conv_b, dt_bias, A_log, D_skip, ssd_norm_g, pool_w,
           pool_scale, w_br_ssd, w_br_pool, w_out, norm_ffn_g, w_router, router_bias, we_gate, we_up, we_down,
           ws_gate, ws_up, ws_down, final_norm_g):
    bsz, seq, _ = x.shape
    assert bsz == 1 and w_ada.shape[0] == 1
    n = seq
    h2 = x.reshape(n, D_MODEL)

    mod = _ada(c, w_ada[0], b_ada[0])
    sh1, sc1, g1, sh2, sc2, g2 = [mod[:, k * D_MODEL:(k + 1) * D_MODEL] for k in range(6)]

    wi = w_in[0]
    o_dt = D_SSD + D_SSD + 2 * D_BC
    w_main = jnp.concatenate([wi[:, :o_dt], wi[:, o_dt + N_HEADS:]], axis=1).astype(BF16)
    w_dt = jnp.pad(wi[:, o_dt:o_dt + N_HEADS], ((0, 0), (0, LANES - N_HEADS)))
    proj, dt_raw = _in_proj(h2, norm_mix_g[0][None], sh1, sc1, w_main, w_dt)

    pad_h = lambda v: jnp.pad(v[None], ((0, 0), (0, LANES - N_HEADS)))
    yn = _ssd(proj, dt_raw, conv_w[0], conv_b[0][None], pad_h(dt_bias[0]), pad_h(A_log[0]),
              jnp.repeat(D_skip[0], HEADDIM)[None], ssd_norm_g[0][None])

    ws_gu = jnp.concatenate([ws_gate[0], ws_up[0]], axis=1).astype(BF16)
    base, u2p, e8, w8, pos8, cnt = _mix(
        proj, yn, h2, pool_w[0].astype(BF16), pool_scale[0][None], w_br_pool[0].astype(BF16),
        w_br_ssd[0].astype(BF16), w_out[0].astype(BF16), g1, norm_ffn_g[0][None], sh2, sc2, g2,
        w_router[0].astype(BF16), router_bias[0][None], ws_gu, ws_down[0].astype(BF16))

    m = EXPERT_BLOCK
    cap = -(-n // m) * m
    counts = cnt[0].astype(jnp.int32)
    dest = (e8 * cap + pos8).reshape(n * TOP_K)
    nblk = (counts + m - 1) // m
    bends = jnp.cumsum(nblk)
    bstarts = bends - nblk
    nb_used = bends[-1]
    n_blocks = -(-(n * TOP_K) // m) + N_EXPERTS + 1
    b_eff = jnp.minimum(jnp.arange(n_blocks, dtype=jnp.int32), nb_used - 1)
    block_e = jnp.sum((bends[None, :] <= b_eff[:, None]).astype(jnp.int32), axis=1)
    block_row = block_e * (cap // m) + b_eff - jnp.take(bstarts, block_e)

    xs = _dispatch(dest, u2p, N_EXPERTS * cap)
    xs = _padfill(counts, xs, cap)
    ytok = _experts(block_e, block_row, nb_used.reshape(1), xs,
                    we_gate[0], we_up[0], we_down[0], n * TOP_K)
    out = _combine(ytok, w8, base, g2, final_norm_g[None])
    return out.reshape(bsz, seq, D_MODEL)
```

```python
import functools

import jax
import jax.numpy as jnp
from jax import lax
from jax.experimental import pallas as pl
from jax.experimental.pallas import tpu as pltpu

F32 = jnp.float32
BF16 = jnp.bfloat16
HIGHEST = lax.Precision.HIGHEST

D_MODEL = 1024
D_SSD = 2048
HEADDIM = 64
N_HEADS = 32
N_GROUPS = 8
HEADS_PER_GROUP = N_HEADS // N_GROUPS
D_STATE = 128
CONV_K = 4
CHUNK = 128
GROUP_W = D_SSD // N_GROUPS
D_BC = N_GROUPS * D_STATE
POOL_WINDOWS = (2, 4, 8, 16)
POOL_GDIM = 256
N_EXPERTS = 64
TOP_K = 8
N_EGROUPS = 8
EXPERTS_PER_GROUP = 8
TOPK_GROUPS = 4
D_EXPERT = 256
D_SHARED = 256
ROUTED_SCALE = 2.5
MOE_BLOCK = 128
EPS = 1e-6

LANES = 128
SUBLANES = 8
ROW_TILES = D_MODEL // LANES
PACK_TILES = ROW_TILES // 2
PACK_W = PACK_TILES * LANES
ID_ROW = PACK_TILES
EXPERT_BLOCK = 512

PROJ_W = D_SSD + D_SSD + 2 * D_BC + 3 * D_MODEL
PROJ_CHUNK = 512

VMEM_LIMIT = 56 * 1024 * 1024


def _cparams(sem=("arbitrary",)):
    return pltpu.CompilerParams(dimension_semantics=sem, vmem_limit_bytes=VMEM_LIMIT)


def _full(shape):
    nd = len(shape)
    return pl.BlockSpec(shape, lambda *_: (0,) * nd)


def _silu(v):
    return v * jax.nn.sigmoid(v)


def _ada_kernel(c_ref, w_ref, b_ref, o_ref):
    c = c_ref[...]
    o_ref[...] = jnp.dot(_silu(c), w_ref[...], preferred_element_type=F32, precision=HIGHEST) + b_ref[...]


def _ada(c, w_ada, b_ada):
    n_out = w_ada.shape[1]
    tn = 1536
    c8 = jnp.broadcast_to(c, (SUBLANES, D_MODEL))
    out = pl.pallas_call(
        _ada_kernel,
        grid=(n_out // tn,),
        in_specs=[_full((SUBLANES, D_MODEL)),
                  pl.BlockSpec((D_MODEL, tn), lambda j: (0, j)),
                  pl.BlockSpec((1, tn), lambda j: (0, j))],
        out_specs=pl.BlockSpec((SUBLANES, tn), lambda j: (0, j)),
        out_shape=jax.ShapeDtypeStruct((SUBLANES, n_out), F32),
        compiler_params=_cparams(),
        name="ada",
    )(c8, w_ada, b_ada.reshape(1, n_out))
    return out[0:1]


def _inproj_kernel(x_ref, g_ref, sh_ref, sc_ref, w_ref, wdt_ref, proj_ref, dt_ref):
    x = x_ref[...]
    inv = lax.rsqrt(jnp.mean(x * x, axis=-1, keepdims=True) + EPS)
    u = x * inv * g_ref[...]
    u = u * (1.0 + sc_ref[...]) + sh_ref[...]
    ub = u.astype(BF16)
    for c0 in range(0, PROJ_W, PROJ_CHUNK):
        proj_ref[:, c0:c0 + PROJ_CHUNK] = jnp.dot(
            ub, w_ref[:, c0:c0 + PROJ_CHUNK], preferred_element_type=F32).astype(BF16)
    dt_ref[...] = jnp.dot(u, wdt_ref[...], preferred_element_type=F32, precision=HIGHEST)


def _in_proj(x2, g, sh, sc, w_main, w_dt):
    n = x2.shape[0]
    tm = 256
    return pl.pallas_call(
        _inproj_kernel,
        grid=(n // tm,),
        in_specs=[pl.BlockSpec((tm, D_MODEL), lambda i: (i, 0)),
                  _full((1, D_MODEL)), _full((1, D_MODEL)), _full((1, D_MODEL)),
                  pl.BlockSpec((D_MODEL, PROJ_W), lambda i: (0, 0), pipeline_mode=pl.Buffered(1)),
                  _full((D_MODEL, LANES))],
        out_specs=[pl.BlockSpec((tm, PROJ_W), lambda i: (i, 0)),
                   pl.BlockSpec((tm, LANES), lambda i: (i, 0))],
        out_shape=[jax.ShapeDtypeStruct((n, PROJ_W), BF16),
                   jax.ShapeDtypeStruct((n, LANES), F32)],
        compiler_params=_cparams(),
        name="in_proj",
    )(x2, g, sh, sc, w_main, w_dt)


def _shift_rows(cur, tail, s, row8):
    rolled = pltpu.roll(cur, s, axis=0)
    top = jnp.where(row8 < s, pltpu.roll(tail, s, axis=0), rolled[0:SUBLANES])
    return jnp.concatenate([top, rolled[SUBLANES:]], axis=0)


def _conv_silu(cur_ref, tail_ref, w_ref, b_ref, out_ref, width, cw=512):
    t = cur_ref.shape[0]
    row8 = lax.broadcasted_iota(jnp.int32, (SUBLANES, cw), 0)
    for c0 in range(0, width, cw):
        sl = slice(c0, c0 + cw)
        cur = cur_ref[:, sl].astype(F32)
        tail = tail_ref[:, sl]
        acc = cur * w_ref[CONV_K - 1:CONV_K, sl] + b_ref[:, sl]
        for s in range(1, CONV_K):
            acc = acc + _shift_rows(cur, tail, s, row8) * w_ref[CONV_K - 1 - s:CONV_K - s, sl]
        out_ref[:, sl] = _silu(acc)
        tail_ref[:, sl] = cur[t - SUBLANES:t]


def _ssd_kernel(z_ref, xs_ref, bc_ref, dt_ref, cwx_ref, cwbc_ref, cbx_ref, cbbc_ref, dtb_ref, alog_ref,
                dskip_ref, ng_ref, o_ref, tailx, tailbc, state, xc, bcc, ybuf):
    q = CHUNK

    @pl.when(pl.program_id(0) == 0)
    def _():
        tailx[...] = jnp.zeros_like(tailx)
        tailbc[...] = jnp.zeros_like(tailbc)
        state[...] = jnp.zeros_like(state)

    _conv_silu(xs_ref, tailx, cwx_ref, cbx_ref, xc, D_SSD)
    _conv_silu(bc_ref, tailbc, cwbc_ref, cbbc_ref, bcc, 2 * D_BC)

    v = dt_ref[...] + dtb_ref[...]
    dt = jnp.maximum(v, 0.0) + jnp.log(1.0 + jnp.exp(-jnp.abs(v)))
    a = dt * (-jnp.exp(alog_ref[...]))
    ri = lax.broadcasted_iota(jnp.int32, (q, q), 0)
    ci = lax.broadcasted_iota(jnp.int32, (q, q), 1)
    causal = ri >= ci
    a_cs = jnp.dot(causal.astype(F32), a, preferred_element_type=F32, precision=HIGHEST)
    a_cs_t = a_cs.T
    dt_t = dt.T

    for g in range(N_GROUPS):
        b_g = bcc[:, g * D_STATE:(g + 1) * D_STATE]
        c_g = bcc[:, D_BC + g * D_STATE:D_BC + (g + 1) * D_STATE].astype(BF16)
        cb = lax.dot_general(c_g, b_g.astype(BF16), (((1,), (1,)), ((), ())),
                             preferred_element_type=F32)
        b_t = b_g.T
        st_g = state[g]
        y_off = jnp.dot(c_g, st_g.astype(BF16), preferred_element_type=F32)
        for r in range(HEADS_PER_GROUP):
            h = g * HEADS_PER_GROUP + r
            col = a_cs[:, h:h + 1]
            row = a_cs_t[h:h + 1, :]
            dtrow = dt_t[h:h + 1, :]
            decay = jnp.exp(jnp.where(causal, col - row, -jnp.inf))
            m = (cb * decay * dtrow).astype(BF16)
            xh = xc[:, h * HEADDIM:(h + 1) * HEADDIM].astype(BF16)
            y_h = jnp.dot(m, xh, preferred_element_type=F32)
            y_h = y_h + y_off[:, r * HEADDIM:(r + 1) * HEADDIM] * jnp.exp(col)
            ybuf[:, h * HEADDIM:(h + 1) * HEADDIM] = y_h
            last = a_cs_t[h:h + 1, q - 1:q]
            wrow = jnp.exp(last - row) * dtrow
            s_new = jnp.dot((b_t * wrow).astype(BF16), xh, preferred_element_type=F32)
            st_h = st_g[:, r * HEADDIM:(r + 1) * HEADDIM]
            state[g, :, r * HEADDIM:(r + 1) * HEADDIM] = st_h * jnp.exp(last) + s_new

    for g in range(N_GROUPS):
        sl = slice(g * GROUP_W, (g + 1) * GROUP_W)
        zg = z_ref[:, sl].astype(F32)
        yg = (ybuf[:, sl] + dskip_ref[:, sl] * xc[:, sl]) * _silu(zg)
        inv = lax.rsqrt(jnp.mean(yg * yg, axis=-1, keepdims=True) + EPS)
        o_ref[:, sl] = (yg * inv * ng_ref[:, sl]).astype(BF16)


def _ssd(proj, dt_raw, conv_w, conv_b, dt_bias, a_log, d_skip, norm_g):
    n = proj.shape[0]
    t = CHUNK
    return pl.pallas_call(
        _ssd_kernel,
        grid=(n // t,),
        in_specs=[pl.BlockSpec((t, D_SSD), lambda i: (i, 0)),
                  pl.BlockSpec((t, D_SSD), lambda i: (i, 1)),
                  pl.BlockSpec((t, 2 * D_BC), lambda i: (i, 2)),
                  pl.BlockSpec((t, LANES), lambda i: (i, 0)),
                  pl.BlockSpec((CONV_K, D_SSD), lambda i: (0, 0)),
                  pl.BlockSpec((CONV_K, 2 * D_BC), lambda i: (0, 1)),
                  pl.BlockSpec((1, D_SSD), lambda i: (0, 0)),
                  pl.BlockSpec((1, 2 * D_BC), lambda i: (0, 1)),
                  _full((1, LANES)), _full((1, LANES)),
                  _full((1, D_SSD)), _full((1, D_SSD))],
        out_specs=pl.BlockSpec((t, D_SSD), lambda i: (i, 0)),
        out_shape=jax.ShapeDtypeStruct((n, D_SSD), BF16),
        scratch_shapes=[pltpu.VMEM((SUBLANES, D_SSD), F32),
                        pltpu.VMEM((SUBLANES, 2 * D_BC), F32),
                        pltpu.VMEM((N_GROUPS, D_STATE, GROUP_W), F32),
                        pltpu.VMEM((t, D_SSD), F32),
                        pltpu.VMEM((t, 2 * D_BC), F32),
                        pltpu.VMEM((t, D_SSD), F32)],
        compiler_params=_cparams(),
        name="ssd",
    )(proj, proj, proj, dt_raw, conv_w, conv_w, conv_b, conv_b, dt_bias, a_log, d_skip, norm_g)


def _mix_kernel(xp_ref, gs_ref, gp_ref, yn_ref, x_ref, poolw_ref, pscale_ref, wbp_ref, wbs_ref, wout_ref,
                g1_ref, ng_ref, sh2_ref, sc2_ref, g2_ref, wr_ref, rb_ref, wsgu_ref, wsd_ref,
                base_ref, u2p_ref, e8_ref, w8_ref, pos8_ref, cnt_ref, ptail, run):
    t = x_ref.shape[0]
    i = pl.program_id(0)

    @pl.when(i == 0)
    def _():
        ptail[...] = jnp.zeros_like(ptail)
        run[...] = jnp.zeros_like(run)

    xp = xp_ref[...].astype(F32)
    ext = jnp.concatenate([ptail[...], xp], axis=0)
    ptail[...] = xp[t - 2 * SUBLANES:t]
    pos = (lax.broadcasted_iota(jnp.int32, (t, 1), 0) + (i * t + 1)).astype(F32)
    pooled = []
    for gi, w in enumerate(POOL_WINDOWS):
        sl = slice(gi * POOL_GDIM, (gi + 1) * POOL_GDIM)
        e = ext[:, sl]
        s = e
        span = 1
        while span < w:
            s = s + pltpu.roll(s, span, axis=0)
            span *= 2
        win = s[2 * SUBLANES:]
        mean = win / jnp.minimum(pos, float(w))
        pg = (mean - xp[:, sl]).astype(BF16)
        mixed = jnp.dot(pg, poolw_ref[gi], preferred_element_type=F32)
        pooled.append(mixed * pscale_ref[:, sl])
    pooled = jnp.concatenate(pooled, axis=1).astype(BF16)
    y_pool = jnp.dot(pooled, wbp_ref[...], preferred_element_type=F32)
    y_ssd = jnp.dot(yn_ref[...], wbs_ref[...], preferred_element_type=F32)
    g_ssd = jax.nn.sigmoid(gs_ref[...].astype(F32))
    g_pool = jax.nn.sigmoid(gp_ref[...].astype(F32))
    mixed = (g_ssd * y_ssd + g_pool * y_pool).astype(BF16)
    h = x_ref[...] + g1_ref[...] * jnp.dot(mixed, wout_ref[...], preferred_element_type=F32)

    inv = lax.rsqrt(jnp.mean(h * h, axis=-1, keepdims=True) + EPS)
    u2 = h * inv * ng_ref[...]
    u2 = u2 * (1.0 + sc2_ref[...]) + sh2_ref[...]
    u2b = u2.astype(BF16)
    u2r = u2b.astype(F32)
    for j in range(PACK_TILES):
        lo = lax.bitcast_convert_type(u2r[:, j * LANES:(j + 1) * LANES], jnp.uint32)
        hi = lax.bitcast_convert_type(u2r[:, (j + PACK_TILES) * LANES:(j + PACK_TILES + 1) * LANES], jnp.uint32)
        u2p_ref[:, j * LANES:(j + 1) * LANES] = (lo >> 16) | (hi & jnp.uint32(0xFFFF0000))

    hs = jnp.dot(u2b, wsgu_ref[...], preferred_element_type=F32)
    act = (_silu(hs[:, :D_SHARED]) * hs[:, D_SHARED:]).astype(BF16)
    shared = jnp.dot(act, wsd_ref[...], preferred_element_type=F32)
    base_ref[...] = h + g2_ref[...] * shared

    logits = jnp.dot(u2b, wr_ref[...], preferred_element_type=F32)
    scores = jax.nn.sigmoid(logits)
    choice = scores + rb_ref[...]
    lane = lax.broadcasted_iota(jnp.int32, (t, N_EXPERTS), 1)
    lane_f = lane.astype(F32)
    lane_grp = lane // EXPERTS_PER_GROUP
    neg = -jnp.inf
    gscore = []
    for g in range(N_EGROUPS):
        vg = jnp.where(lane_grp == g, choice, neg)
        m1 = jnp.max(vg, axis=-1, keepdims=True)
        i1 = jnp.min(jnp.where(vg == m1, lane_f, float(N_EXPERTS)), axis=-1, keepdims=True)
        m2 = jnp.max(jnp.where(lane_f == i1, neg, vg), axis=-1, keepdims=True)
        gscore.append(m1 + m2)
    gmask = jnp.zeros((t, N_EXPERTS), jnp.bool_)
    for g in range(N_EGROUPS):
        rank = jnp.zeros((t, 1), F32)
        for g2 in range(N_EGROUPS):
            if g2 == g:
                continue
            better = (gscore[g2] > gscore[g]) | ((gscore[g2] == gscore[g]) & (g2 < g))
            rank = rank + better.astype(F32)
        gmask = gmask | ((rank < float(TOPK_GROUPS)) & (lane_grp == g))
    work = jnp.where(gmask, choice, neg)
    k_iota = lax.broadcasted_iota(jnp.int32, (t, TOP_K), 1)
    e8 = jnp.zeros((t, TOP_K), jnp.int32)
    s8 = jnp.zeros((t, TOP_K), F32)
    sel = jnp.zeros((t, N_EXPERTS), jnp.bool_)
    onehots = []
    for k in range(TOP_K):
        m = jnp.max(work, axis=-1, keepdims=True)
        idx = jnp.min(jnp.where(work == m, lane_f, float(N_EXPERTS)), axis=-1, keepdims=True)
        oh = lane_f == idx
        onehots.append(oh)
        sk = jnp.sum(jnp.where(oh, scores, 0.0), axis=-1, keepdims=True)
        e8 = jnp.where(k_iota == k, idx.astype(jnp.int32), e8)
        s8 = jnp.where(k_iota == k, sk, s8)
        sel = sel | oh
        work = jnp.where(oh, neg, work)
    w8_ref[...] = s8 / jnp.sum(s8, axis=-1, keepdims=True) * ROUTED_SCALE
    e8_ref[...] = e8

    ri = lax.broadcasted_iota(jnp.int32, (t, t), 0)
    ci = lax.broadcasted_iota(jnp.int32, (t, t), 1)
    before = (ri > ci).astype(BF16)
    sel_f = jnp.where(sel, 1.0, 0.0)
    pos_tile = jnp.dot(before, sel_f.astype(BF16), preferred_element_type=F32) + run[...]
    p8 = jnp.zeros((t, TOP_K), jnp.int32)
    for k in range(TOP_K):
        pk = jnp.sum(jnp.where(onehots[k], pos_tile, 0.0), axis=-1, keepdims=True)
        p8 = jnp.where(k_iota == k, pk.astype(jnp.int32), p8)
    pos8_ref[...] = p8
    total = run[...] + jnp.sum(sel_f, axis=0, keepdims=True)
    run[...] = total
    cnt_ref[...] = total


def _mix(proj, yn, x2, pool_w, pool_scale, w_br_pool, w_br_ssd, w_out, g1, ng, sh2, sc2, g2,
         w_router, router_bias, ws_gu, ws_down):
    n = x2.shape[0]
    t = 256
    xp_blk = (D_SSD + D_SSD + 2 * D_BC) // D_MODEL
    row = lambda i: (i, 0)
    return pl.pallas_call(
        _mix_kernel,
        grid=(n // t,),
        in_specs=[pl.BlockSpec((t, D_MODEL), lambda i: (i, xp_blk)),
                  pl.BlockSpec((t, D_MODEL), lambda i: (i, xp_blk + 1)),
                  pl.BlockSpec((t, D_MODEL), lambda i: (i, xp_blk + 2)),
                  pl.BlockSpec((t, D_SSD), row),
                  pl.BlockSpec((t, D_MODEL), row),
                  _full((len(POOL_WINDOWS), POOL_GDIM, POOL_GDIM)),
                  _full((1, D_MODEL)),
                  _full((D_MODEL, D_MODEL)),
                  _full((D_SSD, D_MODEL)),
                  _full((D_MODEL, D_MODEL)),
                  _full((1, D_MODEL)), _full((1, D_MODEL)), _full((1, D_MODEL)), _full((1, D_MODEL)),
                  _full((1, D_MODEL)),
                  _full((D_MODEL, N_EXPERTS)),
                  _full((1, N_EXPERTS)),
                  _full((D_MODEL, 2 * D_SHARED)),
                  _full((D_SHARED, D_MODEL))],
        out_specs=[pl.BlockSpec((t, D_MODEL), row),
                   pl.BlockSpec((t, PACK_W), row),
                   pl.BlockSpec((t, TOP_K), row),
                   pl.BlockSpec((t, TOP_K), row),
                   pl.BlockSpec((t, TOP_K), row),
                   _full((1, N_EXPERTS))],
        out_shape=[jax.ShapeDtypeStruct((n, D_MODEL), F32),
                   jax.ShapeDtypeStruct((n, PACK_W), jnp.uint32),
                   jax.ShapeDtypeStruct((n, TOP_K), jnp.int32),
                   jax.ShapeDtypeStruct((n, TOP_K), F32),
                   jax.ShapeDtypeStruct((n, TOP_K), jnp.int32),
                   jax.ShapeDtypeStruct((1, N_EXPERTS), F32)],
        scratch_shapes=[pltpu.VMEM((2 * SUBLANES, D_MODEL), F32),
                        pltpu.VMEM((1, N_EXPERTS), F32)],
        compiler_params=_cparams(),
        name="mix",
    )(proj, proj, proj, yn, x2, pool_w, pool_scale, w_br_pool, w_br_ssd, w_out, g1, ng, sh2, sc2, g2,
      w_router, router_bias, ws_gu, ws_down)


PAD_ID = 0xFFFFFFFF


def _tile_rows(ref, first_row):
    return ref.at[pl.ds(pl.multiple_of(first_row * SUBLANES, SUBLANES), SUBLANES)]


def _dispatch_kernel(dest_ref, u_ref, xs_ref, stage, sem):
    t5 = u_ref.shape[0]
    i = pl.program_id(0)
    rows = t5 * SUBLANES

    @pl.when(i == 0)
    def _():
        stage[...] = jnp.zeros_like(stage)

    tok = lax.broadcasted_iota(jnp.int32, (t5, LANES), 0) + i * t5
    n_tok = pl.num_programs(0) * t5
    for k in range(TOP_K):
        for j in range(PACK_TILES):
            stage[pl.ds(k * rows + j, t5, stride=SUBLANES), :] = u_ref[:, j * LANES:(j + 1) * LANES]
        stage[pl.ds(k * rows + ID_ROW, t5, stride=SUBLANES), :] = (k * n_tok + tok).astype(jnp.uint32)

    base = i * (t5 * TOP_K)

    def issue(t, carry):
        for k in range(TOP_K):
            d = dest_ref[base + t * TOP_K + k]
            pltpu.make_async_copy(_tile_rows(stage, k * t5 + t), _tile_rows(xs_ref, d), sem).start()
        return carry

    lax.fori_loop(0, t5, issue, 0)
    for k in range(TOP_K):
        pltpu.make_async_copy(stage.at[pl.ds(k * rows, rows)], xs_ref.at[pl.ds(0, rows)], sem).wait()


def _dispatch(dest, u2p, n_slot_rows):
    n = u2p.shape[0]
    t5 = 256
    return pl.pallas_call(
        _dispatch_kernel,
        grid_spec=pltpu.PrefetchScalarGridSpec(
            num_scalar_prefetch=1,
            grid=(n // t5,),
            in_specs=[pl.BlockSpec((t5, PACK_W), lambda i, d: (i, 0))],
            out_specs=pl.BlockSpec(memory_space=pl.ANY),
            scratch_shapes=[pltpu.VMEM((TOP_K * t5 * SUBLANES, LANES), jnp.uint32),
                            pltpu.SemaphoreType.DMA(())]),
        out_shape=jax.ShapeDtypeStruct((n_slot_rows * SUBLANES, LANES), jnp.uint32),
        compiler_params=_cparams(),
        name="dispatch",
    )(dest, u2p)


def _padfill_kernel(cnt_ref, xs_in_ref, xs_ref, padbuf, sem, *, cap):
    del xs_in_ref
    sub = lax.broadcasted_iota(jnp.int32, padbuf.shape, 0) & (SUBLANES - 1)
    padbuf[...] = jnp.where(sub == ID_ROW, jnp.uint32(PAD_ID), jnp.uint32(0))
    sizes = [1 << bit for bit in range(EXPERT_BLOCK.bit_length() - 1)]

    def pad_copies(e, fn):
        c = cnt_ref[e]
        n_pad = (EXPERT_BLOCK - (c & (EXPERT_BLOCK - 1))) & (EXPERT_BLOCK - 1)
        first = e * cap + c
        for size in sizes:
            @pl.when((n_pad & size) != 0)
            def _():
                start = first + (n_pad & (size - 1))
                dst = xs_ref.at[pl.ds(pl.multiple_of(start * SUBLANES, SUBLANES), size * SUBLANES)]
                fn(pltpu.make_async_copy(padbuf.at[pl.ds(0, size * SUBLANES)], dst, sem))

    def issue(e, carry):
        pad_copies(e, lambda cp: cp.start())
        return carry

    lax.fori_loop(0, N_EXPERTS, issue, 0)

    def drain(e, carry):
        pad_copies(e, lambda cp: cp.wait())
        return carry

    lax.fori_loop(0, N_EXPERTS, drain, 0)


def _padfill(counts, xs, cap):
    return pl.pallas_call(
        functools.partial(_padfill_kernel, cap=cap),
        grid_spec=pltpu.PrefetchScalarGridSpec(
            num_scalar_prefetch=1,
            grid=(1,),
            in_specs=[pl.BlockSpec(memory_space=pl.ANY)],
            out_specs=pl.BlockSpec(memory_space=pl.ANY),
            scratch_shapes=[pltpu.VMEM((EXPERT_BLOCK // 2 * SUBLANES, LANES), jnp.uint32),
                            pltpu.SemaphoreType.DMA(())]),
        out_shape=jax.ShapeDtypeStruct(xs.shape, xs.dtype),
        input_output_aliases={1: 0},
        compiler_params=_cparams(),
        name="padfill",
    )(counts, xs)


def _expert_kernel(be_ref, br_ref, nb_ref, x_ref, wg_ref, wu_ref, wd_ref, ytok_ref, ybuf0, ybuf1, idv, ids0, ids1,
                   wgb, wub, wdb, sem, sem_ids, *, n_ids):
    del br_ref
    m = EXPERT_BLOCK
    b = pl.program_id(0)
    nb = nb_ref[0]
    ybufs, idss = (ybuf0, ybuf1), (ids0, ids1)
    n_phases = m // LANES
    half = D_MODEL // 2

    @pl.when((b == 0) | (be_ref[b] != be_ref[jnp.maximum(b - 1, 0)]))
    def _():
        wgb[...] = wg_ref[0].astype(BF16)
        wub[...] = wu_ref[0].astype(BF16)
        wdb[...] = wd_ref[0].astype(BF16)

    def issue(s, a0):
        for col in range(LANES):
            pltpu.make_async_copy(_tile_rows(ybufs[s], a0 * LANES + col),
                                  _tile_rows(ytok_ref, idss[s][a0, col]), sem.at[s]).start()

    def step(s, prev):
        idrep = x_ref[pl.ds(ID_ROW, m, stride=SUBLANES), :]
        r = lax.broadcasted_iota(jnp.int32, (m, LANES), 0)
        ln = lax.broadcasted_iota(jnp.int32, (m, LANES), 1)
        idi = jnp.where(idrep == jnp.uint32(PAD_ID), n_ids + r, idrep.astype(jnp.int32))
        diag = jnp.where((r & (LANES - 1)) == ln, idi, 0).astype(F32)
        idv[...] = jnp.sum(diag.reshape(m // LANES, LANES, LANES), axis=1).astype(jnp.int32)
        ids_copy = pltpu.make_async_copy(idv, idss[s], sem_ids)
        ids_copy.start()

        lo, hi = [], []
        for j in range(PACK_TILES):
            w = x_ref[pl.ds(j, m, stride=SUBLANES), :]
            lo.append(lax.bitcast_convert_type(w << 16, F32).astype(BF16))
            hi.append(lax.bitcast_convert_type(w & jnp.uint32(0xFFFF0000), F32).astype(BF16))
        x = jnp.concatenate(lo + hi, axis=1)
        phases = iter(range(n_phases))
        if prev is not None:
            issue(prev, next(phases))
        gate = jnp.dot(x, wgb[...], preferred_element_type=F32)
        if prev is not None:
            issue(prev, next(phases))
        up = jnp.dot(x, wub[...], preferred_element_type=F32)
        hid = (_silu(gate) * up).astype(BF16)
        for c0 in (0, half):
            if prev is not None:
                issue(prev, next(phases))
            y = jnp.dot(hid, wdb[:, c0:c0 + half], preferred_element_type=F32)
            for j in range(half // LANES):
                ybufs[s][pl.ds(c0 // LANES + j, m, stride=SUBLANES), :] = y[:, j * LANES:(j + 1) * LANES]
        ids_copy.wait()
        if prev is not None:
            for a0 in phases:
                issue(prev, a0)
            pltpu.make_async_copy(ybufs[prev], ytok_ref.at[pl.ds(0, m * SUBLANES)], sem.at[prev]).wait()

    @pl.when(b == 0)
    def _():
        step(0, None)

    for s in range(2):
        @pl.when((b >= 1) & (b <= nb) & (lax.rem(b, 2) == s))
        def _():
            step(s, 1 - s)


def _experts(block_e, block_row, nb_used, xs, we_gate, we_up, we_down, n_ids):
    n_blocks = block_e.shape[0]
    m = EXPERT_BLOCK
    rows = m * SUBLANES
    return pl.pallas_call(
        functools.partial(_expert_kernel, n_ids=n_ids),
        grid_spec=pltpu.PrefetchScalarGridSpec(
            num_scalar_prefetch=3,
            grid=(n_blocks,),
            in_specs=[pl.BlockSpec((rows, LANES), lambda b, be, br, nb: (br[b], 0)),
                      pl.BlockSpec((1, D_MODEL, D_EXPERT), lambda b, be, br, nb: (be[b], 0, 0)),
                      pl.BlockSpec((1, D_MODEL, D_EXPERT), lambda b, be, br, nb: (be[b], 0, 0)),
                      pl.BlockSpec((1, D_EXPERT, D_MODEL), lambda b, be, br, nb: (be[b], 0, 0))],
            out_specs=pl.BlockSpec(memory_space=pl.ANY),
            scratch_shapes=[pltpu.VMEM((rows, LANES), F32),
                            pltpu.VMEM((rows, LANES), F32),
                            pltpu.VMEM((m // LANES, LANES), jnp.int32),
                            pltpu.SMEM((m // LANES, LANES), jnp.int32),
                            pltpu.SMEM((m // LANES, LANES), jnp.int32),
                            pltpu.VMEM((D_MODEL, D_EXPERT), BF16),
                            pltpu.VMEM((D_MODEL, D_EXPERT), BF16),
                            pltpu.VMEM((D_EXPERT, D_MODEL), BF16),
                            pltpu.SemaphoreType.DMA((2,)),
                            pltpu.SemaphoreType.DMA(())]),
        out_shape=jax.ShapeDtypeStruct(((n_ids + m) * SUBLANES, LANES), F32),
        compiler_params=_cparams(),
        name="experts",
    )(block_e, block_row, nb_used, xs, we_gate, we_up, we_down)


def _combine_kernel(*refs):
    y_refs = refs[:TOP_K]
    w8_ref, base_ref, g2_ref, fg_ref, o_ref = refs[TOP_K:]
    t7 = base_ref.shape[0]
    w8 = w8_ref[...]
    parts = []
    for j in range(ROW_TILES):
        acc = jnp.zeros((t7, LANES), F32)
        for k in range(TOP_K):
            acc = acc + w8[:, k:k + 1] * y_refs[k][pl.ds(j, t7, stride=SUBLANES), :]
        parts.append(acc)
    routed = jnp.concatenate(parts, axis=1)
    h = base_ref[...] + g2_ref[...] * routed
    inv = lax.rsqrt(jnp.mean(h * h, axis=-1, keepdims=True) + EPS)
    o_ref[...] = h * inv * fg_ref[...]


def _combine(ytok, w8, base, g2, fg):
    n = base.shape[0]
    t7 = 128
    y_specs = [pl.BlockSpec((t7 * SUBLANES, LANES), functools.partial(lambda i, k: (k * (n // t7) + i, 0), k=k))
               for k in range(TOP_K)]
    return pl.pallas_call(
        _combine_kernel,
        grid=(n // t7,),
        in_specs=y_specs + [
                  pl.BlockSpec((t7, TOP_K), lambda i: (i, 0)),
                  pl.BlockSpec((t7, D_MODEL), lambda i: (i, 0)),
                  _full((1, D_MODEL)), _full((1, D_MODEL))],
        out_specs=pl.BlockSpec((t7, D_MODEL), lambda i: (i, 0)),
        out_shape=jax.ShapeDtypeStruct((n, D_MODEL), F32),
        compiler_params=_cparams(),
        name="combine",
    )(*([ytok] * TOP_K), w8, base, g2, fg)


def kernel(x, c, w_ada, b_ada, norm_mix_g, w_in, conv_w, conv_b, dt_bias, A_log, D_skip, ssd_norm_g, pool_w,
           pool_scale, w_br_ssd, w_br_pool, w_out, norm_ffn_g, w_router, router_bias, we_gate, we_up, we_down,
           ws_gate, ws_up, ws_down, final_norm_g):
    bsz, seq, _ = x.shape
    assert bsz == 1 and w_ada.shape[0] == 1
    n = seq
    h2 = x.reshape(n, D_MODEL)

    mod = _ada(c, w_ada[0], b_ada[0])
    sh1, sc1, g1, sh2, sc2, g2 = [mod[:, k * D_MODEL:(k + 1) * D_MODEL] for k in range(6)]

    wi = w_in[0]
    o_dt = D_SSD + D_SSD + 2 * D_BC
    w_main = jnp.concatenate([wi[:, :o_dt], wi[:, o_dt + N_HEADS:]], axis=1).astype(BF16)
    w_dt = jnp.pad(wi[:, o_dt:o_dt + N_HEADS], ((0, 0), (0, LANES - N_HEADS)))
    proj, dt_raw = _in_proj(h2, norm_mix_g[0][None], sh1, sc1, w_main, w_dt)

    pad_h = lambda v: jnp.pad(v[None], ((0, 0), (0, LANES - N_HEADS)))
    yn = _ssd(proj, dt_raw, conv_w[0], conv_b[0][None], pad_h(dt_bias[0]), pad_h(A_log[0]),
              jnp.repeat(D_skip[0], HEADDIM)[None], ssd_norm_g[0][None])

    ws_gu = jnp.concatenate([ws_gate[0], ws_up[0]], axis=1).astype(BF16)
    base, u2p, e8, w8, pos8, cnt = _mix(
        proj, yn, h2, pool_w[0].astype(BF16), pool_scale[0][None], w_br_pool[0].astype(BF16),
        w_br_ssd[0].astype(BF16), w_out[0].astype(BF16), g1, norm_ffn_g[0][None], sh2, sc2, g2,
        w_router[0].astype(BF16), router_bias[0][None], ws_gu, ws_down[0].astype(BF16))

    m = EXPERT_BLOCK
    cap = -(-n // m) * m
    counts = cnt[0].astype(jnp.int32)
    dest = (e8 * cap + pos8).reshape(n * TOP_K)
    nblk = (counts + m - 1) // m
    bends = jnp.cumsum(nblk)
    bstarts = bends - nblk
    nb_used = bends[-1]
    n_blocks = -(-(n * TOP_K) // m) + N_EXPERTS + 1
    b_eff = jnp.minimum(jnp.arange(n_blocks, dtype=jnp.int32), nb_used - 1)
    block_e = jnp.sum((bends[None, :] <= b_eff[:, None]).astype(jnp.int32), axis=1)
    block_row = block_e * (cap // m) + b_eff - jnp.take(bstarts, block_e)

    xs = _dispatch(dest, u2p, N_EXPERTS * cap)
    xs = _padfill(counts, xs, cap)
    ytok = _experts(block_e, block_row, nb_used.reshape(1), xs,
                    we_gate[0], we_up[0], we_down[0], n * TOP_K)
    out = _combine(ytok, w8, base, g2, final_norm_g[None])
    return out.reshape(bsz, seq, D_MODEL)
```

```python
import functools

import jax
import jax.numpy as jnp
from jax import lax
from jax.experimental import pallas as pl
from jax.experimental.pallas import tpu as pltpu

F32 = jnp.float32
BF16 = jnp.bfloat16
HIGHEST = lax.Precision.HIGHEST

D_MODEL = 1024
D_SSD = 2048
HEADDIM = 64
N_HEADS = 32
N_GROUPS = 8
HEADS_PER_GROUP = N_HEADS // N_GROUPS
D_STATE = 128
CONV_K = 4
CHUNK = 128
GROUP_W = D_SSD // N_GROUPS
D_BC = N_GROUPS * D_STATE
POOL_WINDOWS = (2, 4, 8, 16)
POOL_GDIM = 256
N_EXPERTS = 64
TOP_K = 8
N_EGROUPS = 8
EXPERTS_PER_GROUP = 8
TOPK_GROUPS = 4
D_EXPERT = 256
D_SHARED = 256
ROUTED_SCALE = 2.5
MOE_BLOCK = 128
EPS = 1e-6

LANES = 128
SUBLANES = 8
ROW_TILES = D_MODEL // LANES
PACK_TILES = ROW_TILES // 2
PACK_W = PACK_TILES * LANES
EXPERT_BLOCK = 512

PROJ_W = D_SSD + D_SSD + 2 * D_BC + 3 * D_MODEL
PROJ_CHUNK = 512

VMEM_LIMIT = 56 * 1024 * 1024


def _cparams(sem=("arbitrary",)):
    return pltpu.CompilerParams(dimension_semantics=sem, vmem_limit_bytes=VMEM_LIMIT)


def _full(shape):
    nd = len(shape)
    return pl.BlockSpec(shape, lambda *_: (0,) * nd)


def _silu(v):
    return v * jax.nn.sigmoid(v)


def _ada_kernel(c_ref, w_ref, b_ref, o_ref):
    c = c_ref[...]
    o_ref[...] = jnp.dot(_silu(c), w_ref[...], preferred_element_type=F32, precision=HIGHEST) + b_ref[...]


def _ada(c, w_ada, b_ada):
    n_out = w_ada.shape[1]
    tn = 1536
    c8 = jnp.broadcast_to(c, (SUBLANES, D_MODEL))
    out = pl.pallas_call(
        _ada_kernel,
        grid=(n_out // tn,),
        in_specs=[_full((SUBLANES, D_MODEL)),
                  pl.BlockSpec((D_MODEL, tn), lambda j: (0, j)),
                  pl.BlockSpec((1, tn), lambda j: (0, j))],
        out_specs=pl.BlockSpec((SUBLANES, tn), lambda j: (0, j)),
        out_shape=jax.ShapeDtypeStruct((SUBLANES, n_out), F32),
        compiler_params=_cparams(),
        name="ada",
    )(c8, w_ada, b_ada.reshape(1, n_out))
    return out[0:1]


def _inproj_kernel(x_ref, g_ref, sh_ref, sc_ref, w_ref, wdt_ref, proj_ref, dt_ref):
    x = x_ref[...]
    inv = lax.rsqrt(jnp.mean(x * x, axis=-1, keepdims=True) + EPS)
    u = x * inv * g_ref[...]
    u = u * (1.0 + sc_ref[...]) + sh_ref[...]
    ub = u.astype(BF16)
    for c0 in range(0, PROJ_W, PROJ_CHUNK):
        proj_ref[:, c0:c0 + PROJ_CHUNK] = jnp.dot(
            ub, w_ref[:, c0:c0 + PROJ_CHUNK], preferred_element_type=F32).astype(BF16)
    dt_ref[...] = jnp.dot(u, wdt_ref[...], preferred_element_type=F32, precision=HIGHEST)


def _in_proj(x2, g, sh, sc, w_main, w_dt):
    n = x2.shape[0]
    tm = 256
    return pl.pallas_call(
        _inproj_kernel,
        grid=(n // tm,),
        in_specs=[pl.BlockSpec((tm, D_MODEL), lambda i: (i, 0)),
                  _full((1, D_MODEL)), _full((1, D_MODEL)), _full((1, D_MODEL)),
                  pl.BlockSpec((D_MODEL, PROJ_W), lambda i: (0, 0), pipeline_mode=pl.Buffered(1)),
                  _full((D_MODEL, LANES))],
        out_specs=[pl.BlockSpec((tm, PROJ_W), lambda i: (i, 0)),
                   pl.BlockSpec((tm, LANES), lambda i: (i, 0))],
        out_shape=[jax.ShapeDtypeStruct((n, PROJ_W), BF16),
                   jax.ShapeDtypeStruct((n, LANES), F32)],
        compiler_params=_cparams(),
        name="in_proj",
    )(x2, g, sh, sc, w_main, w_dt)


def _conv_silu(cur_ref, ext_ref, w_ref, b_ref, out_ref, width, cw=512):
    t = cur_ref.shape[0]
    for c0 in range(0, width, cw):
        sl = slice(c0, c0 + cw)
        cur = cur_ref[:, sl].astype(F32)
        ext_ref[SUBLANES:, sl] = cur
        acc = cur * w_ref[CONV_K - 1:CONV_K, sl] + b_ref[:, sl]
        for s in range(1, CONV_K):
            acc = acc + ext_ref[pl.ds(SUBLANES - s, t), sl] * w_ref[CONV_K - 1 - s:CONV_K - s, sl]
        out_ref[:, sl] = _silu(acc)
        ext_ref[0:SUBLANES, sl] = cur[t - SUBLANES:t]


def _ssd_kernel(z_ref, xs_ref, bc_ref, dt_ref, cwx_ref, cwbc_ref, cbx_ref, cbbc_ref, dtb_ref, alog_ref,
                dskip_ref, ng_ref, o_ref, tailx, tailbc, state, xc, bcc, ybuf):
    q = CHUNK

    @pl.when(pl.program_id(0) == 0)
    def _():
        tailx[...] = jnp.zeros_like(tailx)
        tailbc[...] = jnp.zeros_like(tailbc)
        state[...] = jnp.zeros_like(state)

    _conv_silu(xs_ref, tailx, cwx_ref, cbx_ref, xc, D_SSD)
    _conv_silu(bc_ref, tailbc, cwbc_ref, cbbc_ref, bcc, 2 * D_BC)

    v = dt_ref[...] + dtb_ref[...]
    dt = jnp.maximum(v, 0.0) + jnp.log(1.0 + jnp.exp(-jnp.abs(v)))
    a = dt * (-jnp.exp(alog_ref[...]))
    ri = lax.broadcasted_iota(jnp.int32, (q, q), 0)
    ci = lax.broadcasted_iota(jnp.int32, (q, q), 1)
    causal = ri >= ci
    a_cs = jnp.dot(causal.astype(F32), a, preferred_element_type=F32, precision=HIGHEST)
    a_cs_t = a_cs.T
    dt_t = dt.T

    for g in range(N_GROUPS):
        b_g = bcc[:, g * D_STATE:(g + 1) * D_STATE]
        c_g = bcc[:, D_BC + g * D_STATE:D_BC + (g + 1) * D_STATE].astype(BF16)
        cb = lax.dot_general(c_g, b_g.astype(BF16), (((1,), (1,)), ((), ())),
                             preferred_element_type=F32)
        b_t = b_g.T
        st_g = state[g]
        y_off = jnp.dot(c_g, st_g.astype(BF16), preferred_element_type=F32)
        for r in range(HEADS_PER_GROUP):
            h = g * HEADS_PER_GROUP + r
            col = a_cs[:, h:h + 1]
            row = a_cs_t[h:h + 1, :]
            dtrow = dt_t[h:h + 1, :]
            decay = jnp.exp(jnp.where(causal, col - row, -jnp.inf))
            m = (cb * decay * dtrow).astype(BF16)
            xh = xc[:, h * HEADDIM:(h + 1) * HEADDIM].astype(BF16)
            y_h = jnp.dot(m, xh, preferred_element_type=F32)
            y_h = y_h + y_off[:, r * HEADDIM:(r + 1) * HEADDIM] * jnp.exp(col)
            ybuf[:, h * HEADDIM:(h + 1) * HEADDIM] = y_h
            last = a_cs_t[h:h + 1, q - 1:q]
            wrow = jnp.exp(last - row) * dtrow
            s_new = jnp.dot((b_t * wrow).astype(BF16), xh, preferred_element_type=F32)
            st_h = st_g[:, r * HEADDIM:(r + 1) * HEADDIM]
            state[g, :, r * HEADDIM:(r + 1) * HEADDIM] = st_h * jnp.exp(last) + s_new

    for g in range(N_GROUPS):
        sl = slice(g * GROUP_W, (g + 1) * GROUP_W)
        zg = z_ref[:, sl].astype(F32)
        yg = (ybuf[:, sl] + dskip_ref[:, sl] * xc[:, sl]) * _silu(zg)
        inv = lax.rsqrt(jnp.mean(yg * yg, axis=-1, keepdims=True) + EPS)
        o_ref[:, sl] = (yg * inv * ng_ref[:, sl]).astype(BF16)


def _ssd(proj, dt_raw, conv_w, conv_b, dt_bias, a_log, d_skip, norm_g):
    n = proj.shape[0]
    t = CHUNK
    return pl.pallas_call(
        _ssd_kernel,
        grid=(n // t,),
        in_specs=[pl.BlockSpec((t, D_SSD), lambda i: (i, 0)),
                  pl.BlockSpec((t, D_SSD), lambda i: (i, 1)),
                  pl.BlockSpec((t, 2 * D_BC), lambda i: (i, 2)),
                  pl.BlockSpec((t, LANES), lambda i: (i, 0)),
                  pl.BlockSpec((CONV_K, D_SSD), lambda i: (0, 0)),
                  pl.BlockSpec((CONV_K, 2 * D_BC), lambda i: (0, 1)),
                  pl.BlockSpec((1, D_SSD), lambda i: (0, 0)),
                  pl.BlockSpec((1, 2 * D_BC), lambda i: (0, 1)),
                  _full((1, LANES)), _full((1, LANES)),
                  _full((1, D_SSD)), _full((1, D_SSD))],
        out_specs=pl.BlockSpec((t, D_SSD), lambda i: (i, 0)),
        out_shape=jax.ShapeDtypeStruct((n, D_SSD), BF16),
        scratch_shapes=[pltpu.VMEM((SUBLANES + t, D_SSD), F32),
                        pltpu.VMEM((SUBLANES + t, 2 * D_BC), F32),
                        pltpu.VMEM((N_GROUPS, D_STATE, GROUP_W), F32),
                        pltpu.VMEM((t, D_SSD), F32),
                        pltpu.VMEM((t, 2 * D_BC), F32),
                        pltpu.VMEM((t, D_SSD), F32)],
        compiler_params=_cparams(),
        name="ssd",
    )(proj, proj, proj, dt_raw, conv_w, conv_w, conv_b, conv_b, dt_bias, a_log, d_skip, norm_g)


PAD_ID = 0xFFFFFFFF
TOK_ROW = PACK_TILES
MASK_ROWS = (PACK_TILES + 1, PACK_TILES + 2)


def _tile_rows(ref, first_row):
    return ref.at[pl.ds(pl.multiple_of(first_row * SUBLANES, SUBLANES), SUBLANES)]


def _mix_kernel(xp_ref, gs_ref, gp_ref, yn_ref, x_ref, poolw_ref, pscale_ref, wbp_ref, wbs_ref, wout_ref,
                g1_ref, ng_ref, sh2_ref, sc2_ref, g2_ref, wr_ref, rb_ref, wsgu_ref, wsd_ref,
                base_ref, ws_ref, cnt_ref, xs_ref, ptail, run, stage, dvm, dsm, sem, sem_s, *, n_tiles, cap):
    t = x_ref.shape[0]
    i = pl.program_id(0)
    slot = lax.rem(i, 2)

    @pl.when(i == 0)
    def _():
        ptail[...] = jnp.zeros_like(ptail)
        run[...] = jnp.zeros_like(run)
        stage[...] = jnp.zeros_like(stage)

    def issue_rows(s):
        for tt in range(t):
            for k in range(TOP_K):
                pltpu.make_async_copy(stage.at[s, pl.ds(tt * SUBLANES, SUBLANES)],
                                      _tile_rows(xs_ref, dsm[s, k, tt]), sem).start()

    def drain_rows(s):
        for _ in range(TOP_K):
            pltpu.make_async_copy(stage.at[s], xs_ref.at[pl.ds(0, t * SUBLANES)], sem).wait()

    def tile(s):
        _mix_tile_body(xp_ref, gs_ref, gp_ref, yn_ref, x_ref, poolw_ref, pscale_ref, wbp_ref, wbs_ref, wout_ref,
                       g1_ref, ng_ref, sh2_ref, sc2_ref, g2_ref, wr_ref, rb_ref, wsgu_ref, wsd_ref,
                       base_ref, ws_ref, cnt_ref, ptail, run, stage, dvm, dsm, sem_s, i, s, cap)

    @pl.when(i == 0)
    def _():
        tile(0)

    @pl.when((i >= 1) & (i < n_tiles))
    def _():
        issue_rows(1 - slot)
        tile(slot)
        drain_rows(1 - slot)

    @pl.when(i == n_tiles)
    def _():
        issue_rows(1 - slot)
        drain_rows(1 - slot)


def _mix_tile_body(xp_ref, gs_ref, gp_ref, yn_ref, x_ref, poolw_ref, pscale_ref, wbp_ref, wbs_ref, wout_ref,
                   g1_ref, ng_ref, sh2_ref, sc2_ref, g2_ref, wr_ref, rb_ref, wsgu_ref, wsd_ref,
                   base_ref, ws_ref, cnt_ref, ptail, run, stage, dvm, dsm, sem_s, i, slot, cap):
    t = x_ref.shape[0]

    xp = xp_ref[...].astype(F32)
    ext = jnp.concatenate([ptail[...], xp], axis=0)
    ptail[...] = xp[t - 2 * SUBLANES:t]
    pos = (lax.broadcasted_iota(jnp.int32, (t, 1), 0) + (i * t + 1)).astype(F32)
    pooled = []
    for gi, w in enumerate(POOL_WINDOWS):
        sl = slice(gi * POOL_GDIM, (gi + 1) * POOL_GDIM)
        e = ext[:, sl]
        s = e
        span = 1
        while span < w:
            s = s + pltpu.roll(s, span, axis=0)
            span *= 2
        win = s[2 * SUBLANES:]
        mean = win / jnp.minimum(pos, float(w))
        pg = (mean - xp[:, sl]).astype(BF16)
        mixed = jnp.dot(pg, poolw_ref[gi], preferred_element_type=F32)
        pooled.append(mixed * pscale_ref[:, sl])
    pooled = jnp.concatenate(pooled, axis=1).astype(BF16)
    y_pool = jnp.dot(pooled, wbp_ref[...], preferred_element_type=F32)
    y_ssd = jnp.dot(yn_ref[...], wbs_ref[...], preferred_element_type=F32)
    g_ssd = jax.nn.sigmoid(gs_ref[...].astype(F32))
    g_pool = jax.nn.sigmoid(gp_ref[...].astype(F32))
    mixed = (g_ssd * y_ssd + g_pool * y_pool).astype(BF16)
    h = x_ref[...] + g1_ref[...] * jnp.dot(mixed, wout_ref[...], preferred_element_type=F32)

    inv = lax.rsqrt(jnp.mean(h * h, axis=-1, keepdims=True) + EPS)
    u2 = h * inv * ng_ref[...]
    u2 = u2 * (1.0 + sc2_ref[...]) + sh2_ref[...]
    u2b = u2.astype(BF16)
    u2r = u2b.astype(F32)
    packed = []
    for j in range(PACK_TILES):
        lo = lax.bitcast_convert_type(u2r[:, j * LANES:(j + 1) * LANES], jnp.uint32)
        hi = lax.bitcast_convert_type(u2r[:, (j + PACK_TILES) * LANES:(j + PACK_TILES + 1) * LANES], jnp.uint32)
        packed.append((lo >> 16) | (hi & jnp.uint32(0xFFFF0000)))

    hs = jnp.dot(u2b, wsgu_ref[...], preferred_element_type=F32)
    act = (_silu(hs[:, :D_SHARED]) * hs[:, D_SHARED:]).astype(BF16)
    shared = jnp.dot(act, wsd_ref[...], preferred_element_type=F32)
    base_ref[...] = h + g2_ref[...] * shared

    logits = jnp.dot(u2b, wr_ref[...], preferred_element_type=F32)
    scores = jax.nn.sigmoid(logits)
    choice = scores + rb_ref[...]
    lane = lax.broadcasted_iota(jnp.int32, (t, N_EXPERTS), 1)
    lane_f = lane.astype(F32)
    lane_grp = lane // EXPERTS_PER_GROUP
    neg = -jnp.inf
    gscore = []
    for g in range(N_EGROUPS):
        vg = jnp.where(lane_grp == g, choice, neg)
        m1 = jnp.max(vg, axis=-1, keepdims=True)
        i1 = jnp.min(jnp.where(vg == m1, lane_f, float(N_EXPERTS)), axis=-1, keepdims=True)
        m2 = jnp.max(jnp.where(lane_f == i1, neg, vg), axis=-1, keepdims=True)
        gscore.append(m1 + m2)
    gmask = jnp.zeros((t, N_EXPERTS), jnp.bool_)
    for g in range(N_EGROUPS):
        rank = jnp.zeros((t, 1), F32)
        for g2 in range(N_EGROUPS):
            if g2 == g:
                continue
            better = (gscore[g2] > gscore[g]) | ((gscore[g2] == gscore[g]) & (g2 < g))
            rank = rank + better.astype(F32)
        gmask = gmask | ((rank < float(TOPK_GROUPS)) & (lane_grp == g))
    work = jnp.where(gmask, choice, neg)
    sel = jnp.zeros((t, N_EXPERTS), jnp.bool_)
    onehots, idxs, sks = [], [], []
    for k in range(TOP_K):
        m = jnp.max(work, axis=-1, keepdims=True)
        idx = jnp.min(jnp.where(work == m, lane_f, float(N_EXPERTS)), axis=-1, keepdims=True)
        oh = lane_f == idx
        onehots.append(oh)
        idxs.append(idx)
        sks.append(jnp.sum(jnp.where(oh, scores, 0.0), axis=-1, keepdims=True))
        sel = sel | oh
        work = jnp.where(oh, neg, work)
    denom = sks[0]
    for k in range(1, TOP_K):
        denom = denom + sks[k]

    ri = lax.broadcasted_iota(jnp.int32, (t, t), 0)
    ci = lax.broadcasted_iota(jnp.int32, (t, t), 1)
    before = (ri > ci).astype(BF16)
    sel_f = jnp.where(sel, 1.0, 0.0)
    sel_b = sel_f.astype(BF16)
    pos_tile = jnp.dot(before, sel_b, preferred_element_type=F32) + run[...]
    er = lax.broadcasted_iota(jnp.int32, (N_EXPERTS, N_EXPERTS), 0)
    ec = lax.broadcasted_iota(jnp.int32, (N_EXPERTS, N_EXPERTS), 1)
    rank_tile = jnp.dot(sel_b, (er < ec).astype(BF16), preferred_element_type=F32)
    lane128 = lax.broadcasted_iota(jnp.int32, (t, LANES), 1)
    k_iota = lax.broadcasted_iota(jnp.int32, (t, TOP_K), 1).astype(F32)
    dest = jnp.zeros((t, LANES), F32)
    w_sorted = jnp.zeros((t, TOP_K), F32)
    for k in range(TOP_K):
        pk = jnp.sum(jnp.where(onehots[k], pos_tile, 0.0), axis=-1, keepdims=True)
        rk = jnp.sum(jnp.where(onehots[k], rank_tile, 0.0), axis=-1, keepdims=True)
        dest = jnp.where(lane128 == k, idxs[k] * float(cap) + pk, dest)
        w_sorted = jnp.where(k_iota == rk, sks[k] / denom * ROUTED_SCALE, w_sorted)
    ws_ref[...] = w_sorted
    total = run[...] + jnp.sum(sel_f, axis=0, keepdims=True)
    run[...] = total
    cnt_ref[...] = total

    for j in range(PACK_TILES):
        stage[slot, pl.ds(j, t, stride=SUBLANES), :] = packed[j]
    tok = lax.broadcasted_iota(jnp.int32, (t, LANES), 0) + i * t
    stage[slot, pl.ds(TOK_ROW, t, stride=SUBLANES), :] = tok.astype(jnp.uint32)
    bit = jnp.where(sel, jnp.left_shift(1, lane & 15), 0).astype(F32)
    words = []
    for q in range(N_EXPERTS // 16):
        part = jnp.sum(jnp.where((lane >> 4) == q, bit, 0.0), axis=-1, keepdims=True)
        words.append(part.astype(jnp.int32).astype(jnp.uint32))
    for w, row in enumerate(MASK_ROWS):
        word = words[2 * w] | (words[2 * w + 1] << 16)
        stage[slot, pl.ds(row, t, stride=SUBLANES), :] = jnp.broadcast_to(word, (t, LANES))

    dvm[...] = dest.T[0:TOP_K, :].astype(jnp.int32)
    smem_copy = pltpu.make_async_copy(dvm, dsm.at[slot], sem_s)
    smem_copy.start()
    smem_copy.wait()


def _mix(proj, yn, x2, pool_w, pool_scale, w_br_pool, w_br_ssd, w_out, g1, ng, sh2, sc2, g2,
         w_router, router_bias, ws_gu, ws_down, cap):
    n = x2.shape[0]
    t = 256
    n_tiles = n // t
    xp_blk = (D_SSD + D_SSD + 2 * D_BC) // D_MODEL
    tile = lambda i: jnp.minimum(i, n_tiles - 1)
    row = lambda i: (tile(i), 0)
    return pl.pallas_call(
        functools.partial(_mix_kernel, n_tiles=n_tiles, cap=cap),
        grid=(n_tiles + 1,),
        in_specs=[pl.BlockSpec((t, D_MODEL), lambda i: (tile(i), xp_blk)),
                  pl.BlockSpec((t, D_MODEL), lambda i: (tile(i), xp_blk + 1)),
                  pl.BlockSpec((t, D_MODEL), lambda i: (tile(i), xp_blk + 2)),
                  pl.BlockSpec((t, D_SSD), row),
                  pl.BlockSpec((t, D_MODEL), row),
                  _full((len(POOL_WINDOWS), POOL_GDIM, POOL_GDIM)),
                  _full((1, D_MODEL)),
                  _full((D_MODEL, D_MODEL)),
                  _full((D_SSD, D_MODEL)),
                  _full((D_MODEL, D_MODEL)),
                  _full((1, D_MODEL)), _full((1, D_MODEL)), _full((1, D_MODEL)), _full((1, D_MODEL)),
                  _full((1, D_MODEL)),
                  _full((D_MODEL, N_EXPERTS)),
                  _full((1, N_EXPERTS)),
                  _full((D_MODEL, 2 * D_SHARED)),
                  _full((D_SHARED, D_MODEL))],
        out_specs=[pl.BlockSpec((t, D_MODEL), row),
                   pl.BlockSpec((t, TOP_K), row),
                   _full((1, N_EXPERTS)),
                   pl.BlockSpec(memory_space=pl.ANY)],
        out_shape=[jax.ShapeDtypeStruct((n, D_MODEL), F32),
                   jax.ShapeDtypeStruct((n, TOP_K), F32),
                   jax.ShapeDtypeStruct((1, N_EXPERTS), F32),
                   jax.ShapeDtypeStruct((N_EXPERTS * cap * SUBLANES, LANES), jnp.uint32)],
        scratch_shapes=[pltpu.VMEM((2 * SUBLANES, D_MODEL), F32),
                        pltpu.VMEM((1, N_EXPERTS), F32),
                        pltpu.VMEM((2, t * SUBLANES, LANES), jnp.uint32),
                        pltpu.VMEM((TOP_K, t), jnp.int32),
                        pltpu.SMEM((2, TOP_K, t), jnp.int32),
                        pltpu.SemaphoreType.DMA(()),
                        pltpu.SemaphoreType.DMA(())],
        compiler_params=_cparams(),
        name="mix",
    )(proj, proj, proj, yn, x2, pool_w, pool_scale, w_br_pool, w_br_ssd, w_out, g1, ng, sh2, sc2, g2,
      w_router, router_bias, ws_gu, ws_down)


def _padfill_kernel(cnt_ref, xs_in_ref, xs_ref, padbuf, sem, *, cap):
    del xs_in_ref
    sub = lax.broadcasted_iota(jnp.int32, padbuf.shape, 0) & (SUBLANES - 1)
    padbuf[...] = jnp.where(sub == TOK_ROW, jnp.uint32(PAD_ID), jnp.uint32(0))
    sizes = [1 << bit for bit in range(EXPERT_BLOCK.bit_length() - 1)]

    def pad_copies(e, fn):
        c = cnt_ref[e]
        n_pad = (EXPERT_BLOCK - (c & (EXPERT_BLOCK - 1))) & (EXPERT_BLOCK - 1)
        first = e * cap + c
        for size in sizes:
            @pl.when((n_pad & size) != 0)
            def _():
                start = first + (n_pad & (size - 1))
                dst = xs_ref.at[pl.ds(pl.multiple_of(start * SUBLANES, SUBLANES), size * SUBLANES)]
                fn(pltpu.make_async_copy(padbuf.at[pl.ds(0, size * SUBLANES)], dst, sem))

    def issue(e, carry):
        pad_copies(e, lambda cp: cp.start())
        return carry

    lax.fori_loop(0, N_EXPERTS, issue, 0)

    def drain(e, carry):
        pad_copies(e, lambda cp: cp.wait())
        return carry

    lax.fori_loop(0, N_EXPERTS, drain, 0)


def _padfill(counts, xs, cap):
    return pl.pallas_call(
        functools.partial(_padfill_kernel, cap=cap),
        grid_spec=pltpu.PrefetchScalarGridSpec(
            num_scalar_prefetch=1,
            grid=(1,),
            in_specs=[pl.BlockSpec(memory_space=pl.ANY)],
            out_specs=pl.BlockSpec(memory_space=pl.ANY),
            scratch_shapes=[pltpu.VMEM((EXPERT_BLOCK // 2 * SUBLANES, LANES), jnp.uint32),
                            pltpu.SemaphoreType.DMA(())]),
        out_shape=jax.ShapeDtypeStruct(xs.shape, xs.dtype),
        input_output_aliases={1: 0},
        compiler_params=_cparams(),
        name="padfill",
    )(counts, xs)


def _expert_kernel(be_ref, br_ref, nb_ref, x_ref, wg_ref, wu_ref, wd_ref, ytok_ref, ybuf0, ybuf1, idv, ids0, ids1,
                   wgb, wub, wdb, sem, sem_ids, *, n_ids):
    del br_ref
    m = EXPERT_BLOCK
    b = pl.program_id(0)
    nb = nb_ref[0]
    ybufs, idss = (ybuf0, ybuf1), (ids0, ids1)
    n_phases = m // LANES
    half = D_MODEL // 2

    @pl.when((b == 0) | (be_ref[b] != be_ref[jnp.maximum(b - 1, 0)]))
    def _():
        wgb[...] = wg_ref[0].astype(BF16)
        wub[...] = wu_ref[0].astype(BF16)
        wdb[...] = wd_ref[0].astype(BF16)

    def issue(s, a0):
        for col in range(LANES):
            pltpu.make_async_copy(_tile_rows(ybufs[s], a0 * LANES + col),
                                  _tile_rows(ytok_ref, idss[s][a0, col]), sem.at[s]).start()

    def step(s, prev):
        tokrep = x_ref[pl.ds(TOK_ROW, m, stride=SUBLANES), :]
        e = be_ref[b]
        below_lo = jnp.where(e >= 32, -1, (1 << jnp.minimum(e, 31)) - 1).astype(jnp.uint32)
        below_hi = jnp.where(e >= 32, (1 << jnp.maximum(e - 32, 0)) - 1, 0).astype(jnp.uint32)
        rank = (lax.population_count(x_ref[pl.ds(MASK_ROWS[0], m, stride=SUBLANES), :] & below_lo)
                + lax.population_count(x_ref[pl.ds(MASK_ROWS[1], m, stride=SUBLANES), :] & below_hi))
        r = lax.broadcasted_iota(jnp.int32, (m, LANES), 0)
        ln = lax.broadcasted_iota(jnp.int32, (m, LANES), 1)
        n_tok = n_ids // TOP_K
        idi = jnp.where(tokrep == jnp.uint32(PAD_ID), n_ids + r,
                        rank.astype(jnp.int32) * n_tok + tokrep.astype(jnp.int32))
        diag = jnp.where((r & (LANES - 1)) == ln, idi, 0).astype(F32)
        idv[...] = jnp.sum(diag.reshape(m // LANES, LANES, LANES), axis=1).astype(jnp.int32)
        ids_copy = pltpu.make_async_copy(idv, idss[s], sem_ids)
        ids_copy.start()

        lo, hi = [], []
        for j in range(PACK_TILES):
            w = x_ref[pl.ds(j, m, stride=SUBLANES), :]
            lo.append(lax.bitcast_convert_type(w << 16, F32).astype(BF16))
            hi.append(lax.bitcast_convert_type(w & jnp.uint32(0xFFFF0000), F32).astype(BF16))
        x = jnp.concatenate(lo + hi, axis=1)
        phases = iter(range(n_phases))
        if prev is not None:
            issue(prev, next(phases))
        gate = jnp.dot(x, wgb[...], preferred_element_type=F32)
        if prev is not None:
            issue(prev, next(phases))
        up = jnp.dot(x, wub[...], preferred_element_type=F32)
        hid = (_silu(gate) * up).astype(BF16)
        for c0 in (0, half):
            if prev is not None:
                issue(prev, next(phases))
            y = jnp.dot(hid, wdb[:, c0:c0 + half], preferred_element_type=F32)
            for j in range(half // LANES):
                ybufs[s][pl.ds(c0 // LANES + j, m, stride=SUBLANES), :] = y[:, j * LANES:(j + 1) * LANES]
        ids_copy.wait()
        if prev is not None:
            for a0 in phases:
                issue(prev, a0)
            pltpu.make_async_copy(ybufs[prev], ytok_ref.at[pl.ds(0, m * SUBLANES)], sem.at[prev]).wait()

    @pl.when(b == 0)
    def _():
        step(0, None)

    for s in range(2):
        @pl.when((b >= 1) & (b <= nb) & (lax.rem(b, 2) == s))
        def _():
            step(s, 1 - s)


def _experts(block_e, block_row, nb_used, xs, we_gate, we_up, we_down, n_ids):
    n_blocks = block_e.shape[0]
    m = EXPERT_BLOCK
    rows = m * SUBLANES
    return pl.pallas_call(
        functools.partial(_expert_kernel, n_ids=n_ids),
        grid_spec=pltpu.PrefetchScalarGridSpec(
            num_scalar_prefetch=3,
            grid=(n_blocks,),
            in_specs=[pl.BlockSpec((rows, LANES), lambda b, be, br, nb: (br[b], 0)),
                      pl.BlockSpec((1, D_MODEL, D_EXPERT), lambda b, be, br, nb: (be[b], 0, 0)),
                      pl.BlockSpec((1, D_MODEL, D_EXPERT), lambda b, be, br, nb: (be[b], 0, 0)),
                      pl.BlockSpec((1, D_EXPERT, D_MODEL), lambda b, be, br, nb: (be[b], 0, 0))],
            out_specs=pl.BlockSpec(memory_space=pl.ANY),
            scratch_shapes=[pltpu.VMEM((rows, LANES), F32),
                            pltpu.VMEM((rows, LANES), F32),
                            pltpu.VMEM((m // LANES, LANES), jnp.int32),
                            pltpu.SMEM((m // LANES, LANES), jnp.int32),
                            pltpu.SMEM((m // LANES, LANES), jnp.int32),
                            pltpu.VMEM((D_MODEL, D_EXPERT), BF16),
                            pltpu.VMEM((D_MODEL, D_EXPERT), BF16),
                            pltpu.VMEM((D_EXPERT, D_MODEL), BF16),
                            pltpu.SemaphoreType.DMA((2,)),
                            pltpu.SemaphoreType.DMA(())]),
        out_shape=jax.ShapeDtypeStruct(((n_ids + m) * SUBLANES, LANES), F32),
        compiler_params=_cparams(),
        name="experts",
    )(block_e, block_row, nb_used, xs, we_gate, we_up, we_down)


def _combine_kernel(*refs):
    y_refs = refs[:TOP_K]
    w8_ref, base_ref, g2_ref, fg_ref, o_ref = refs[TOP_K:]
    t7 = base_ref.shape[0]
    w8 = w8_ref[...]
    parts = []
    for j in range(ROW_TILES):
        acc = jnp.zeros((t7, LANES), F32)
        for k in range(TOP_K):
            acc = acc + w8[:, k:k + 1] * y_refs[k][pl.ds(j, t7, stride=SUBLANES), :]
        parts.append(acc)
    routed = jnp.concatenate(parts, axis=1)
    h = base_ref[...] + g2_ref[...] * routed
    inv = lax.rsqrt(jnp.mean(h * h, axis=-1, keepdims=True) + EPS)
    o_ref[...] = h * inv * fg_ref[...]


def _combine(ytok, w8, base, g2, fg):
    n = base.shape[0]
    t7 = 128
    y_specs = [pl.BlockSpec((t7 * SUBLANES, LANES), functools.partial(lambda i, k: (k * (n // t7) + i, 0), k=k))
               for k in range(TOP_K)]
    return pl.pallas_call(
        _combine_kernel,
        grid=(n // t7,),
        in_specs=y_specs + [
                  pl.BlockSpec((t7, TOP_K), lambda i: (i, 0)),
                  pl.BlockSpec((t7, D_MODEL), lambda i: (i, 0)),
                  _full((1, D_MODEL)), _full((1, D_MODEL))],
        out_specs=pl.BlockSpec((t7, D_MODEL), lambda i: (i, 0)),
        out_shape=jax.ShapeDtypeStruct((n, D_MODEL), F32),
        compiler_params=_cparams(),
        name="combine",
    )(*([ytok] * TOP_K), w8, base, g2, fg)


def kernel(x, c, w_ada, b_ada, norm_mix_g, w_in, conv_w, conv_b, dt_bias, A_log, D_skip, ssd_norm_g, pool_w,
           pool_scale, w_br_ssd, w_br_pool, w_out, norm_ffn_g, w_router, router_bias, we_gate, we_up, we_down,
           ws_gate, ws_up, ws_down, final_norm_g):
    bsz, seq, _ = x.shape
    assert bsz == 1 and w_ada.shape[0] == 1
    n = seq
    h2 = x.reshape(n, D_MODEL)

    mod = _ada(c, w_ada[0], b_ada[0])
    sh1, sc1, g1, sh2, sc2, g2 = [mod[:, k * D_MODEL:(k + 1) * D_MODEL] for k in range(6)]

    wi = w_in[0]
    o_dt = D_SSD + D_SSD + 2 * D_BC
    w_main = jnp.concatenate([wi[:, :o_dt], wi[:, o_dt + N_HEADS:]], axis=1).astype(BF16)
    w_dt = jnp.pad(wi[:, o_dt:o_dt + N_HEADS], ((0, 0), (0, LANES - N_HEADS)))
    proj, dt_raw = _in_proj(h2, norm_mix_g[0][None], sh1, sc1, w_main, w_dt)

    pad_h = lambda v: jnp.pad(v[None], ((0, 0), (0, LANES - N_HEADS)))
    yn = _ssd(proj, dt_raw, conv_w[0], conv_b[0][None], pad_h(dt_bias[0]), pad_h(A_log[0]),
              jnp.repeat(D_skip[0], HEADDIM)[None], ssd_norm_g[0][None])

    ws_gu = jnp.concatenate([ws_gate[0], ws_up[0]], axis=1).astype(BF16)
    m = EXPERT_BLOCK
    cap = -(-n // m) * m
    base, w8, cnt, xs = _mix(
        proj, yn, h2, pool_w[0].astype(BF16), pool_scale[0][None], w_br_pool[0].astype(BF16),
        w_br_ssd[0].astype(BF16), w_out[0].astype(BF16), g1, norm_ffn_g[0][None], sh2, sc2, g2,
        w_router[0].astype(BF16), router_bias[0][None], ws_gu, ws_down[0].astype(BF16), cap)

    counts = cnt[0].astype(jnp.int32)
    nblk = (counts + m - 1) // m
    bends = jnp.cumsum(nblk)
    bstarts = bends - nblk
    nb_used = bends[-1]
    n_blocks = -(-(n * TOP_K) // m) + N_EXPERTS + 1
    b_eff = jnp.minimum(jnp.arange(n_blocks, dtype=jnp.int32), nb_used - 1)
    block_e = jnp.sum((bends[None, :] <= b_eff[:, None]).astype(jnp.int32), axis=1)
    block_row = block_e * (cap // m) + b_eff - jnp.take(bstarts, block_e)

    xs = _padfill(counts, xs, cap)
    ytok = _experts(block_e, block_row, nb_used.reshape(1), xs,
                    we_gate[0], we_up[0], we_down[0], n * TOP_K)
    out = _combine(ytok, w8, base, g2, final_norm_g[None])
    return out.reshape(bsz, seq, D_MODEL)
```

```python
import functools

import jax
import jax.numpy as jnp
from jax import lax
from jax.experimental import pallas as pl
from jax.experimental.pallas import tpu as pltpu

F32 = jnp.float32
BF16 = jnp.bfloat16
HIGHEST = lax.Precision.HIGHEST

D_MODEL = 1024
D_SSD = 2048
HEADDIM = 64
N_HEADS = 32
N_GROUPS = 8
HEADS_PER_GROUP = N_HEADS // N_GROUPS
D_STATE = 128
CONV_K = 4
CHUNK = 128
GROUP_W = D_SSD // N_GROUPS
D_BC = N_GROUPS * D_STATE
POOL_WINDOWS = (2, 4, 8, 16)
POOL_GDIM = 256
N_EXPERTS = 64
TOP_K = 8
N_EGROUPS = 8
EXPERTS_PER_GROUP = 8
TOPK_GROUPS = 4
D_EXPERT = 256
D_SHARED = 256
ROUTED_SCALE = 2.5
MOE_BLOCK = 128
EPS = 1e-6

LANES = 128
SUBLANES = 8
ROW_TILES = D_MODEL // LANES
PACK_TILES = ROW_TILES // 2
PACK_W = PACK_TILES * LANES
EXPERT_BLOCK = 512
MIX_ISSUE_GROUPS = 8

PROJ_W = D_SSD + D_SSD + 2 * D_BC + 3 * D_MODEL
PROJ_CHUNK = 512

VMEM_LIMIT = 56 * 1024 * 1024


def _cparams(sem=("arbitrary",)):
    return pltpu.CompilerParams(dimension_semantics=sem, vmem_limit_bytes=VMEM_LIMIT)


def _full(shape):
    nd = len(shape)
    return pl.BlockSpec(shape, lambda *_: (0,) * nd)


def _silu(v):
    return v * jax.nn.sigmoid(v)


def _ada_kernel(c_ref, w_ref, b_ref, o_ref):
    c = c_ref[...]
    o_ref[...] = jnp.dot(_silu(c), w_ref[...], preferred_element_type=F32, precision=HIGHEST) + b_ref[...]


def _ada(c, w_ada, b_ada):
    n_out = w_ada.shape[1]
    tn = 1536
    c8 = jnp.broadcast_to(c, (SUBLANES, D_MODEL))
    out = pl.pallas_call(
        _ada_kernel,
        grid=(n_out // tn,),
        in_specs=[_full((SUBLANES, D_MODEL)),
                  pl.BlockSpec((D_MODEL, tn), lambda j: (0, j)),
                  pl.BlockSpec((1, tn), lambda j: (0, j))],
        out_specs=pl.BlockSpec((SUBLANES, tn), lambda j: (0, j)),
        out_shape=jax.ShapeDtypeStruct((SUBLANES, n_out), F32),
        compiler_params=_cparams(),
        name="ada",
    )(c8, w_ada, b_ada.reshape(1, n_out))
    return out[0:1]


def _inproj_kernel(x_ref, g_ref, sh_ref, sc_ref, w_ref, wdt_ref, proj_ref, dt_ref):
    x = x_ref[...]
    inv = lax.rsqrt(jnp.mean(x * x, axis=-1, keepdims=True) + EPS)
    u = x * inv * g_ref[...]
    u = u * (1.0 + sc_ref[...]) + sh_ref[...]
    ub = u.astype(BF16)
    for c0 in range(0, PROJ_W, PROJ_CHUNK):
        proj_ref[:, c0:c0 + PROJ_CHUNK] = jnp.dot(
            ub, w_ref[:, c0:c0 + PROJ_CHUNK], preferred_element_type=F32).astype(BF16)
    dt_ref[...] = jnp.dot(u, wdt_ref[...], preferred_element_type=F32, precision=HIGHEST)


def _in_proj(x2, g, sh, sc, w_main, w_dt):
    n = x2.shape[0]
    tm = 256
    return pl.pallas_call(
        _inproj_kernel,
        grid=(n // tm,),
        in_specs=[pl.BlockSpec((tm, D_MODEL), lambda i: (i, 0)),
                  _full((1, D_MODEL)), _full((1, D_MODEL)), _full((1, D_MODEL)),
                  pl.BlockSpec((D_MODEL, PROJ_W), lambda i: (0, 0), pipeline_mode=pl.Buffered(1)),
                  _full((D_MODEL, LANES))],
        out_specs=[pl.BlockSpec((tm, PROJ_W), lambda i: (i, 0)),
                   pl.BlockSpec((tm, LANES), lambda i: (i, 0))],
        out_shape=[jax.ShapeDtypeStruct((n, PROJ_W), BF16),
                   jax.ShapeDtypeStruct((n, LANES), F32)],
        compiler_params=_cparams(),
        name="in_proj",
    )(x2, g, sh, sc, w_main, w_dt)


def _conv_silu(cur_ref, ext_ref, w_ref, b_ref, out_ref, width, cw=512):
    t = cur_ref.shape[0]
    for c0 in range(0, width, cw):
        sl = slice(c0, c0 + cw)
        cur = cur_ref[:, sl].astype(F32)
        ext_ref[SUBLANES:, sl] = cur
        acc = cur * w_ref[CONV_K - 1:CONV_K, sl] + b_ref[:, sl]
        for s in range(1, CONV_K):
            acc = acc + ext_ref[pl.ds(SUBLANES - s, t), sl] * w_ref[CONV_K - 1 - s:CONV_K - s, sl]
        out_ref[:, sl] = _silu(acc)
        ext_ref[0:SUBLANES, sl] = cur[t - SUBLANES:t]


def _ssd_kernel(z_ref, xs_ref, bc_ref, dt_ref, cwx_ref, cwbc_ref, cbx_ref, cbbc_ref, dtb_ref, alog_ref,
                dskip_ref, ng_ref, o_ref, tailx, tailbc, state, xc, bcc, ybuf):
    q = CHUNK

    @pl.when(pl.program_id(0) == 0)
    def _():
        tailx[...] = jnp.zeros_like(tailx)
        tailbc[...] = jnp.zeros_like(tailbc)
        state[...] = jnp.zeros_like(state)

    _conv_silu(xs_ref, tailx, cwx_ref, cbx_ref, xc, D_SSD)
    _conv_silu(bc_ref, tailbc, cwbc_ref, cbbc_ref, bcc, 2 * D_BC)

    v = dt_ref[...] + dtb_ref[...]
    dt = jnp.maximum(v, 0.0) + jnp.log(1.0 + jnp.exp(-jnp.abs(v)))
    a = dt * (-jnp.exp(alog_ref[...]))
    ri = lax.broadcasted_iota(jnp.int32, (q, q), 0)
    ci = lax.broadcasted_iota(jnp.int32, (q, q), 1)
    causal = ri >= ci
    a_cs = jnp.dot(causal.astype(F32), a, preferred_element_type=F32, precision=HIGHEST)
    a_cs_t = a_cs.T
    dt_t = dt.T

    for g in range(N_GROUPS):
        b_g = bcc[:, g * D_STATE:(g + 1) * D_STATE]
        c_g = bcc[:, D_BC + g * D_STATE:D_BC + (g + 1) * D_STATE].astype(BF16)
        cb = lax.dot_general(c_g, b_g.astype(BF16), (((1,), (1,)), ((), ())),
                             preferred_element_type=F32)
        b_t = b_g.T
        st_g = state[g]
        y_off = jnp.dot(c_g, st_g.astype(BF16), preferred_element_type=F32)
        for r in range(HEADS_PER_GROUP):
            h = g * HEADS_PER_GROUP + r
            col = a_cs[:, h:h + 1]
            row = a_cs_t[h:h + 1, :]
            dtrow = dt_t[h:h + 1, :]
            decay = jnp.exp(jnp.where(causal, col - row, -jnp.inf))
            m = (cb * decay * dtrow).astype(BF16)
            xh = xc[:, h * HEADDIM:(h + 1) * HEADDIM].astype(BF16)
            y_h = jnp.dot(m, xh, preferred_element_type=F32)
            y_h = y_h + y_off[:, r * HEADDIM:(r + 1) * HEADDIM] * jnp.exp(col)
            ybuf[:, h * HEADDIM:(h + 1) * HEADDIM] = y_h
            last = a_cs_t[h:h + 1, q - 1:q]
            wrow = jnp.exp(last - row) * dtrow
            s_new = jnp.dot((b_t * wrow).astype(BF16), xh, preferred_element_type=F32)
            st_h = st_g[:, r * HEADDIM:(r + 1) * HEADDIM]
            state[g, :, r * HEADDIM:(r + 1) * HEADDIM] = st_h * jnp.exp(last) + s_new

    for g in range(N_GROUPS):
        sl = slice(g * GROUP_W, (g + 1) * GROUP_W)
        zg = z_ref[:, sl].astype(F32)
        yg = (ybuf[:, sl] + dskip_ref[:, sl] * xc[:, sl]) * _silu(zg)
        inv = lax.rsqrt(jnp.mean(yg * yg, axis=-1, keepdims=True) + EPS)
        o_ref[:, sl] = (yg * inv * ng_ref[:, sl]).astype(BF16)


def _ssd(proj, dt_raw, conv_w, conv_b, dt_bias, a_log, d_skip, norm_g):
    n = proj.shape[0]
    t = CHUNK
    return pl.pallas_call(
        _ssd_kernel,
        grid=(n // t,),
        in_specs=[pl.BlockSpec((t, D_SSD), lambda i: (i, 0)),
                  pl.BlockSpec((t, D_SSD), lambda i: (i, 1)),
                  pl.BlockSpec((t, 2 * D_BC), lambda i: (i, 2)),
                  pl.BlockSpec((t, LANES), lambda i: (i, 0)),
                  pl.BlockSpec((CONV_K, D_SSD), lambda i: (0, 0)),
                  pl.BlockSpec((CONV_K, 2 * D_BC), lambda i: (0, 1)),
                  pl.BlockSpec((1, D_SSD), lambda i: (0, 0)),
                  pl.BlockSpec((1, 2 * D_BC), lambda i: (0, 1)),
                  _full((1, LANES)), _full((1, LANES)),
                  _full((1, D_SSD)), _full((1, D_SSD))],
        out_specs=pl.BlockSpec((t, D_SSD), lambda i: (i, 0)),
        out_shape=jax.ShapeDtypeStruct((n, D_SSD), BF16),
        scratch_shapes=[pltpu.VMEM((SUBLANES + t, D_SSD), F32),
                        pltpu.VMEM((SUBLANES + t, 2 * D_BC), F32),
                        pltpu.VMEM((N_GROUPS, D_STATE, GROUP_W), F32),
                        pltpu.VMEM((t, D_SSD), F32),
                        pltpu.VMEM((t, 2 * D_BC), F32),
                        pltpu.VMEM((t, D_SSD), F32)],
        compiler_params=_cparams(),
        name="ssd",
    )(proj, proj, proj, dt_raw, conv_w, conv_w, conv_b, conv_b, dt_bias, a_log, d_skip, norm_g)


PAD_ID = 0xFFFFFFFF
TOK_ROW = PACK_TILES
MASK_ROWS = (PACK_TILES + 1, PACK_TILES + 2)


def _tile_rows(ref, first_row):
    return ref.at[pl.ds(pl.multiple_of(first_row * SUBLANES, SUBLANES), SUBLANES)]


def _mix_kernel(xp_ref, gs_ref, gp_ref, yn_ref, x_ref, poolw_ref, pscale_ref, wbp_ref, wbs_ref, wout_ref,
                g1_ref, ng_ref, sh2_ref, sc2_ref, g2_ref, wr_ref, rb_ref, wsgu_ref, wsd_ref, zero_ref,
                base_ref, ws_ref, cnt_ref, xs_ref, ptail, run, stage, dvm, dsm, sem, sem_s, *, n_tiles, cap):
    t = x_ref.shape[0]
    i = pl.program_id(0)
    slot = lax.rem(i, 2)

    @pl.when(i == 0)
    def _():
        ptail[...] = jnp.zeros_like(ptail)
        run[...] = jnp.zeros_like(run)
        stage[...] = jnp.zeros_like(stage)

    def issue_rows(s, group, after=None):
        zero = 0
        if after is not None:
            probe = jnp.max(jnp.abs(after[0:SUBLANES, 0:min(LANES, after.shape[1])])).astype(jnp.int32)
            zero = probe * zero_ref[0]
        per = t // MIX_ISSUE_GROUPS
        for tt in range(group * per, (group + 1) * per):
            for k in range(TOP_K):
                pltpu.make_async_copy(stage.at[s, pl.ds(tt * SUBLANES, SUBLANES)],
                                      _tile_rows(xs_ref, dsm[s, k, tt] + zero), sem).start()

    def drain_rows(s):
        for _ in range(TOP_K):
            pltpu.make_async_copy(stage.at[s], xs_ref.at[pl.ds(0, t * SUBLANES)], sem).wait()

    def tile(s, milestone=None):
        _mix_tile_body(xp_ref, gs_ref, gp_ref, yn_ref, x_ref, poolw_ref, pscale_ref, wbp_ref, wbs_ref, wout_ref,
                       g1_ref, ng_ref, sh2_ref, sc2_ref, g2_ref, wr_ref, rb_ref, wsgu_ref, wsd_ref,
                       base_ref, ws_ref, cnt_ref, ptail, run, stage, dvm, dsm, sem_s, i, s, cap, milestone)

    @pl.when(i == 0)
    def _():
        tile(0)

    @pl.when((i >= 1) & (i < n_tiles))
    def _():
        issue_rows(1 - slot, 0)
        tile(slot, lambda group, value: issue_rows(1 - slot, group, value))
        drain_rows(1 - slot)

    @pl.when(i == n_tiles)
    def _():
        for group in range(MIX_ISSUE_GROUPS):
            issue_rows(1 - slot, group)
        drain_rows(1 - slot)


def _mix_tile_body(xp_ref, gs_ref, gp_ref, yn_ref, x_ref, poolw_ref, pscale_ref, wbp_ref, wbs_ref, wout_ref,
                   g1_ref, ng_ref, sh2_ref, sc2_ref, g2_ref, wr_ref, rb_ref, wsgu_ref, wsd_ref,
                   base_ref, ws_ref, cnt_ref, ptail, run, stage, dvm, dsm, sem_s, i, slot, cap, milestone):
    t = x_ref.shape[0]
    if milestone is None:
        milestone = lambda group, value: None

    xp = xp_ref[...].astype(F32)
    ext = jnp.concatenate([ptail[...], xp], axis=0)
    ptail[...] = xp[t - 2 * SUBLANES:t]
    pos = (lax.broadcasted_iota(jnp.int32, (t, 1), 0) + (i * t + 1)).astype(F32)
    pooled = []
    for gi, w in enumerate(POOL_WINDOWS):
        sl = slice(gi * POOL_GDIM, (gi + 1) * POOL_GDIM)
        e = ext[:, sl]
        s = e
        span = 1
        while span < w:
            s = s + pltpu.roll(s, span, axis=0)
            span *= 2
        win = s[2 * SUBLANES:]
        mean = win / jnp.minimum(pos, float(w))
        pg = (mean - xp[:, sl]).astype(BF16)
        mixed = jnp.dot(pg, poolw_ref[gi], preferred_element_type=F32)
        pooled.append(mixed * pscale_ref[:, sl])
    pooled = jnp.concatenate(pooled, axis=1).astype(BF16)
    y_pool = jnp.dot(pooled, wbp_ref[...], preferred_element_type=F32)
    milestone(1, y_pool)
    y_ssd = jnp.dot(yn_ref[...], wbs_ref[...], preferred_element_type=F32)
    milestone(2, y_ssd)
    g_ssd = jax.nn.sigmoid(gs_ref[...].astype(F32))
    g_pool = jax.nn.sigmoid(gp_ref[...].astype(F32))
    mixed = (g_ssd * y_ssd + g_pool * y_pool).astype(BF16)
    h = x_ref[...] + g1_ref[...] * jnp.dot(mixed, wout_ref[...], preferred_element_type=F32)
    milestone(3, h)

    inv = lax.rsqrt(jnp.mean(h * h, axis=-1, keepdims=True) + EPS)
    u2 = h * inv * ng_ref[...]
    u2 = u2 * (1.0 + sc2_ref[...]) + sh2_ref[...]
    u2b = u2.astype(BF16)
    u2r = u2b.astype(F32)
    packed = []
    for j in range(PACK_TILES):
        lo = lax.bitcast_convert_type(u2r[:, j * LANES:(j + 1) * LANES], jnp.uint32)
        hi = lax.bitcast_convert_type(u2r[:, (j + PACK_TILES) * LANES:(j + PACK_TILES + 1) * LANES], jnp.uint32)
        packed.append((lo >> 16) | (hi & jnp.uint32(0xFFFF0000)))

    hs = jnp.dot(u2b, wsgu_ref[...], preferred_element_type=F32)
    milestone(4, hs)
    act = (_silu(hs[:, :D_SHARED]) * hs[:, D_SHARED:]).astype(BF16)
    shared = jnp.dot(act, wsd_ref[...], preferred_element_type=F32)
    milestone(5, shared)
    base_ref[...] = h + g2_ref[...] * shared

    logits = jnp.dot(u2b, wr_ref[...], preferred_element_type=F32)
    scores = jax.nn.sigmoid(logits)
    choice = scores + rb_ref[...]
    lane = lax.broadcasted_iota(jnp.int32, (t, N_EXPERTS), 1)
    lane_f = lane.astype(F32)
    lane_grp = lane // EXPERTS_PER_GROUP
    neg = -jnp.inf
    gscore = []
    for g in range(N_EGROUPS):
        vg = jnp.where(lane_grp == g, choice, neg)
        m1 = jnp.max(vg, axis=-1, keepdims=True)
        i1 = jnp.min(jnp.where(vg == m1, lane_f, float(N_EXPERTS)), axis=-1, keepdims=True)
        m2 = jnp.max(jnp.where(lane_f == i1, neg, vg), axis=-1, keepdims=True)
        gscore.append(m1 + m2)
    milestone(6, gscore[-1])
    gmask = jnp.zeros((t, N_EXPERTS), jnp.bool_)
    for g in range(N_EGROUPS):
        rank = jnp.zeros((t, 1), F32)
        for g2 in range(N_EGROUPS):
            if g2 == g:
                continue
            better = (gscore[g2] > gscore[g]) | ((gscore[g2] == gscore[g]) & (g2 < g))
            rank = rank + better.astype(F32)
        gmask = gmask | ((rank < float(TOPK_GROUPS)) & (lane_grp == g))
    work = jnp.where(gmask, choice, neg)
    sel = jnp.zeros((t, N_EXPERTS), jnp.bool_)
    onehots, idxs, sks = [], [], []
    for k in range(TOP_K):
        m = jnp.max(work, axis=-1, keepdims=True)
        idx = jnp.min(jnp.where(work == m, lane_f, float(N_EXPERTS)), axis=-1, keepdims=True)
        oh = lane_f == idx
        onehots.append(oh)
        idxs.append(idx)
        sks.append(jnp.sum(jnp.where(oh, scores, 0.0), axis=-1, keepdims=True))
        sel = sel | oh
        work = jnp.where(oh, neg, work)
    denom = sks[0]
    for k in range(1, TOP_K):
        denom = denom + sks[k]
    milestone(7, denom)

    ri = lax.broadcasted_iota(jnp.int32, (t, t), 0)
    ci = lax.broadcasted_iota(jnp.int32, (t, t), 1)
    before = (ri > ci).astype(BF16)
    sel_f = jnp.where(sel, 1.0, 0.0)
    sel_b = sel_f.astype(BF16)
    pos_tile = jnp.dot(before, sel_b, preferred_element_type=F32) + run[...]
    er = lax.broadcasted_iota(jnp.int32, (N_EXPERTS, N_EXPERTS), 0)
    ec = lax.broadcasted_iota(jnp.int32, (N_EXPERTS, N_EXPERTS), 1)
    rank_tile = jnp.dot(sel_b, (er < ec).astype(BF16), preferred_element_type=F32)
    lane128 = lax.broadcasted_iota(jnp.int32, (t, LANES), 1)
    k_iota = lax.broadcasted_iota(jnp.int32, (t, TOP_K), 1).astype(F32)
    dest = jnp.zeros((t, LANES), F32)
    w_sorted = jnp.zeros((t, TOP_K), F32)
    for k in range(TOP_K):
        pk = jnp.sum(jnp.where(onehots[k], pos_tile, 0.0), axis=-1, keepdims=True)
        rk = jnp.sum(jnp.where(onehots[k], rank_tile, 0.0), axis=-1, keepdims=True)
        dest = jnp.where(lane128 == k, idxs[k] * float(cap) + pk, dest)
        w_sorted = jnp.where(k_iota == rk, sks[k] / denom * ROUTED_SCALE, w_sorted)
    ws_ref[...] = w_sorted
    total = run[...] + jnp.sum(sel_f, axis=0, keepdims=True)
    run[...] = total
    cnt_ref[...] = total

    for j in range(PACK_TILES):
        stage[slot, pl.ds(j, t, stride=SUBLANES), :] = packed[j]
    tok = lax.broadcasted_iota(jnp.int32, (t, LANES), 0) + i * t
    stage[slot, pl.ds(TOK_ROW, t, stride=SUBLANES), :] = tok.astype(jnp.uint32)
    bit = jnp.where(sel, jnp.left_shift(1, lane & 15), 0).astype(F32)
    words = []
    for q in range(N_EXPERTS // 16):
        part = jnp.sum(jnp.where((lane >> 4) == q, bit, 0.0), axis=-1, keepdims=True)
        words.append(part.astype(jnp.int32).astype(jnp.uint32))
    for w, row in enumerate(MASK_ROWS):
        word = words[2 * w] | (words[2 * w + 1] << 16)
        stage[slot, pl.ds(row, t, stride=SUBLANES), :] = jnp.broadcast_to(word, (t, LANES))

    dvm[...] = dest.T[0:TOP_K, :].astype(jnp.int32)
    smem_copy = pltpu.make_async_copy(dvm, dsm.at[slot], sem_s)
    smem_copy.start()
    smem_copy.wait()


def _mix(proj, yn, x2, pool_w, pool_scale, w_br_pool, w_br_ssd, w_out, g1, ng, sh2, sc2, g2,
         w_router, router_bias, ws_gu, ws_down, cap):
    n = x2.shape[0]
    t = 256
    n_tiles = n // t
    xp_blk = (D_SSD + D_SSD + 2 * D_BC) // D_MODEL
    tile = lambda i: jnp.minimum(i, n_tiles - 1)
    row = lambda i: (tile(i), 0)
    return pl.pallas_call(
        functools.partial(_mix_kernel, n_tiles=n_tiles, cap=cap),
        grid=(n_tiles + 1,),
        in_specs=[pl.BlockSpec((t, D_MODEL), lambda i: (tile(i), xp_blk)),
                  pl.BlockSpec((t, D_MODEL), lambda i: (tile(i), xp_blk + 1)),
                  pl.BlockSpec((t, D_MODEL), lambda i: (tile(i), xp_blk + 2)),
                  pl.BlockSpec((t, D_SSD), row),
                  pl.BlockSpec((t, D_MODEL), row),
                  _full((len(POOL_WINDOWS), POOL_GDIM, POOL_GDIM)),
                  _full((1, D_MODEL)),
                  _full((D_MODEL, D_MODEL)),
                  _full((D_SSD, D_MODEL)),
                  _full((D_MODEL, D_MODEL)),
                  _full((1, D_MODEL)), _full((1, D_MODEL)), _full((1, D_MODEL)), _full((1, D_MODEL)),
                  _full((1, D_MODEL)),
                  _full((D_MODEL, N_EXPERTS)),
                  _full((1, N_EXPERTS)),
                  _full((D_MODEL, 2 * D_SHARED)),
                  _full((D_SHARED, D_MODEL)),
                  pl.BlockSpec(memory_space=pltpu.SMEM)],
        out_specs=[pl.BlockSpec((t, D_MODEL), row),
                   pl.BlockSpec((t, TOP_K), row),
                   _full((1, N_EXPERTS)),
                   pl.BlockSpec(memory_space=pl.ANY)],
        out_shape=[jax.ShapeDtypeStruct((n, D_MODEL), F32),
                   jax.ShapeDtypeStruct((n, TOP_K), F32),
                   jax.ShapeDtypeStruct((1, N_EXPERTS), F32),
                   jax.ShapeDtypeStruct((N_EXPERTS * cap * SUBLANES, LANES), jnp.uint32)],
        scratch_shapes=[pltpu.VMEM((2 * SUBLANES, D_MODEL), F32),
                        pltpu.VMEM((1, N_EXPERTS), F32),
                        pltpu.VMEM((2, t * SUBLANES, LANES), jnp.uint32),
                        pltpu.VMEM((TOP_K, t), jnp.int32),
                        pltpu.SMEM((2, TOP_K, t), jnp.int32),
                        pltpu.SemaphoreType.DMA(()),
                        pltpu.SemaphoreType.DMA(())],
        compiler_params=_cparams(),
        name="mix",
    )(proj, proj, proj, yn, x2, pool_w, pool_scale, w_br_pool, w_br_ssd, w_out, g1, ng, sh2, sc2, g2,
      w_router, router_bias, ws_gu, ws_down, jnp.zeros((1,), jnp.int32))


def _padfill_kernel(cnt_ref, xs_in_ref, xs_ref, padbuf, sem, *, cap):
    del xs_in_ref
    sub = lax.broadcasted_iota(jnp.int32, padbuf.shape, 0) & (SUBLANES - 1)
    padbuf[...] = jnp.where(sub == TOK_ROW, jnp.uint32(PAD_ID), jnp.uint32(0))
    sizes = [1 << bit for bit in range(EXPERT_BLOCK.bit_length() - 1)]

    def pad_copies(e, fn):
        c = cnt_ref[e]
        n_pad = (EXPERT_BLOCK - (c & (EXPERT_BLOCK - 1))) & (EXPERT_BLOCK - 1)
        first = e * cap + c
        for size in sizes:
            @pl.when((n_pad & size) != 0)
            def _():
                start = first + (n_pad & (size - 1))
                dst = xs_ref.at[pl.ds(pl.multiple_of(start * SUBLANES, SUBLANES), size * SUBLANES)]
                fn(pltpu.make_async_copy(padbuf.at[pl.ds(0, size * SUBLANES)], dst, sem))

    def issue(e, carry):
        pad_copies(e, lambda cp: cp.start())
        return carry

    lax.fori_loop(0, N_EXPERTS, issue, 0)

    def drain(e, carry):
        pad_copies(e, lambda cp: cp.wait())
        return carry

    lax.fori_loop(0, N_EXPERTS, drain, 0)


def _padfill(counts, xs, cap):
    return pl.pallas_call(
        functools.partial(_padfill_kernel, cap=cap),
        grid_spec=pltpu.PrefetchScalarGridSpec(
            num_scalar_prefetch=1,
            grid=(1,),
            in_specs=[pl.BlockSpec(memory_space=pl.ANY)],
            out_specs=pl.BlockSpec(memory_space=pl.ANY),
            scratch_shapes=[pltpu.VMEM((EXPERT_BLOCK // 2 * SUBLANES, LANES), jnp.uint32),
                            pltpu.SemaphoreType.DMA(())]),
        out_shape=jax.ShapeDtypeStruct(xs.shape, xs.dtype),
        input_output_aliases={1: 0},
        compiler_params=_cparams(),
        name="padfill",
    )(counts, xs)


def _expert_kernel(be_ref, br_ref, nb_ref, x_ref, wg_ref, wu_ref, wd_ref, ytok_ref, ybuf0, ybuf1, idv, ids0, ids1,
                   wgb, wub, wdb, sem, sem_ids, *, n_ids):
    del br_ref
    m = EXPERT_BLOCK
    b = pl.program_id(0)
    nb = nb_ref[0]
    ybufs, idss = (ybuf0, ybuf1), (ids0, ids1)
    n_phases = m // LANES
    half = D_MODEL // 2

    @pl.when((b == 0) | (be_ref[b] != be_ref[jnp.maximum(b - 1, 0)]))
    def _():
        wgb[...] = wg_ref[0].astype(BF16)
        wub[...] = wu_ref[0].astype(BF16)
        wdb[...] = wd_ref[0].astype(BF16)

    def issue(s, a0):
        for col in range(LANES):
            pltpu.make_async_copy(_tile_rows(ybufs[s], a0 * LANES + col),
                                  _tile_rows(ytok_ref, idss[s][a0, col]), sem.at[s]).start()

    def step(s, prev):
        tokrep = x_ref[pl.ds(TOK_ROW, m, stride=SUBLANES), :]
        e = be_ref[b]
        below_lo = jnp.where(e >= 32, -1, (1 << jnp.minimum(e, 31)) - 1).astype(jnp.uint32)
        below_hi = jnp.where(e >= 32, (1 << jnp.maximum(e - 32, 0)) - 1, 0).astype(jnp.uint32)
        rank = (lax.population_count(x_ref[pl.ds(MASK_ROWS[0], m, stride=SUBLANES), :] & below_lo)
                + lax.population_count(x_ref[pl.ds(MASK_ROWS[1], m, stride=SUBLANES), :] & below_hi))
        r = lax.broadcasted_iota(jnp.int32, (m, LANES), 0)
        ln = lax.broadcasted_iota(jnp.int32, (m, LANES), 1)
        n_tok = n_ids // TOP_K
        idi = jnp.where(tokrep == jnp.uint32(PAD_ID), n_ids + r,
                        rank.astype(jnp.int32) * n_tok + tokrep.astype(jnp.int32))
        diag = jnp.where((r & (LANES - 1)) == ln, idi, 0).astype(F32)
        idv[...] = jnp.sum(diag.reshape(m // LANES, LANES, LANES), axis=1).astype(jnp.int32)
        ids_copy = pltpu.make_async_copy(idv, idss[s], sem_ids)
        ids_copy.start()

        lo, hi = [], []
        for j in range(PACK_TILES):
            w = x_ref[pl.ds(j, m, stride=SUBLANES), :]
            lo.append(lax.bitcast_convert_type(w << 16, F32).astype(BF16))
            hi.append(lax.bitcast_convert_type(w & jnp.uint32(0xFFFF0000), F32).astype(BF16))
        x = jnp.concatenate(lo + hi, axis=1)
        phases = iter(range(n_phases))
        if prev is not None:
            issue(prev, next(phases))
        gate = jnp.dot(x, wgb[...], preferred_element_type=F32)
        if prev is not None:
            issue(prev, next(phases))
        up = jnp.dot(x, wub[...], preferred_element_type=F32)
        hid = (_silu(gate) * up).astype(BF16)
        for c0 in (0, half):
            if prev is not None:
                issue(prev, next(phases))
            y = jnp.dot(hid, wdb[:, c0:c0 + half], preferred_element_type=F32)
            for j in range(half // LANES):
                ybufs[s][pl.ds(c0 // LANES + j, m, stride=SUBLANES), :] = y[:, j * LANES:(j + 1) * LANES]
        ids_copy.wait()
        if prev is not None:
            for a0 in phases:
                issue(prev, a0)
            pltpu.make_async_copy(ybufs[prev], ytok_ref.at[pl.ds(0, m * SUBLANES)], sem.at[prev]).wait()

    @pl.when(b == 0)
    def _():
        step(0, None)

    for s in range(2):
        @pl.when((b >= 1) & (b <= nb) & (lax.rem(b, 2) == s))
        def _():
            step(s, 1 - s)


def _experts(block_e, block_row, nb_used, xs, we_gate, we_up, we_down, n_ids):
    n_blocks = block_e.shape[0]
    m = EXPERT_BLOCK
    rows = m * SUBLANES
    return pl.pallas_call(
        functools.partial(_expert_kernel, n_ids=n_ids),
        grid_spec=pltpu.PrefetchScalarGridSpec(
            num_scalar_prefetch=3,
            grid=(n_blocks,),
            in_specs=[pl.BlockSpec((rows, LANES), lambda b, be, br, nb: (br[b], 0)),
                      pl.BlockSpec((1, D_MODEL, D_EXPERT), lambda b, be, br, nb: (be[b], 0, 0)),
                      pl.BlockSpec((1, D_MODEL, D_EXPERT), lambda b, be, br, nb: (be[b], 0, 0)),
                      pl.BlockSpec((1, D_EXPERT, D_MODEL), lambda b, be, br, nb: (be[b], 0, 0))],
            out_specs=pl.BlockSpec(memory_space=pl.ANY),
            scratch_shapes=[pltpu.VMEM((rows, LANES), F32),
                            pltpu.VMEM((rows, LANES), F32),
                            pltpu.VMEM((m // LANES, LANES), jnp.int32),
                            pltpu.SMEM((m // LANES, LANES), jnp.int32),
                            pltpu.SMEM((m // LANES, LANES), jnp.int32),
                            pltpu.VMEM((D_MODEL, D_EXPERT), BF16),
                            pltpu.VMEM((D_MODEL, D_EXPERT), BF16),
                            pltpu.VMEM((D_EXPERT, D_MODEL), BF16),
                            pltpu.SemaphoreType.DMA((2,)),
                            pltpu.SemaphoreType.DMA(())]),
        out_shape=jax.ShapeDtypeStruct(((n_ids + m) * SUBLANES, LANES), F32),
        compiler_params=_cparams(),
        name="experts",
    )(block_e, block_row, nb_used, xs, we_gate, we_up, we_down)


def _combine_kernel(*refs):
    y_refs = refs[:TOP_K]
    w8_ref, base_ref, g2_ref, fg_ref, o_ref = refs[TOP_K:]
    t7 = base_ref.shape[0]
    w8 = w8_ref[...]
    parts = []
    for j in range(ROW_TILES):
        acc = jnp.zeros((t7, LANES), F32)
        for k in range(TOP_K):
            acc = acc + w8[:, k:k + 1] * y_refs[k][pl.ds(j, t7, stride=SUBLANES), :]
        parts.append(acc)
    routed = jnp.concatenate(parts, axis=1)
    h = base_ref[...] + g2_ref[...] * routed
    inv = lax.rsqrt(jnp.mean(h * h, axis=-1, keepdims=True) + EPS)
    o_ref[...] = h * inv * fg_ref[...]


def _combine(ytok, w8, base, g2, fg):
    n = base.shape[0]
    t7 = 128
    y_specs = [pl.BlockSpec((t7 * SUBLANES, LANES), functools.partial(lambda i, k: (k * (n // t7) + i, 0), k=k))
               for k in range(TOP_K)]
    return pl.pallas_call(
        _combine_kernel,
        grid=(n // t7,),
        in_specs=y_specs + [
                  pl.BlockSpec((t7, TOP_K), lambda i: (i, 0)),
                  pl.BlockSpec((t7, D_MODEL), lambda i: (i, 0)),
                  _full((1, D_MODEL)), _full((1, D_MODEL))],
        out_specs=pl.BlockSpec((t7, D_MODEL), lambda i: (i, 0)),
        out_shape=jax.ShapeDtypeStruct((n, D_MODEL), F32),
        compiler_params=_cparams(),
        name="combine",
    )(*([ytok] * TOP_K), w8, base, g2, fg)


def kernel(x, c, w_ada, b_ada, norm_mix_g, w_in, conv_w, conv_b, dt_bias, A_log, D_skip, ssd_norm_g, pool_w,
           pool_scale, w_br_ssd, w_br_pool, w_out, norm_ffn_g, w_router, router_bias, we_gate, we_up, we_down,
           ws_gate, ws_up, ws_down, final_norm_g):
    bsz, seq, _ = x.shape
    assert bsz == 1 and w_ada.shape[0] == 1
    n = seq
    h2 = x.reshape(n, D_MODEL)

    mod = _ada(c, w_ada[0], b_ada[0])
    sh1, sc1, g1, sh2, sc2, g2 = [mod[:, k * D_MODEL:(k + 1) * D_MODEL] for k in range(6)]

    wi = w_in[0]
    o_dt = D_SSD + D_SSD + 2 * D_BC
    w_main = jnp.concatenate([wi[:, :o_dt], wi[:, o_dt + N_HEADS:]], axis=1).astype(BF16)
    w_dt = jnp.pad(wi[:, o_dt:o_dt + N_HEADS], ((0, 0), (0, LANES - N_HEADS)))
    proj, dt_raw = _in_proj(h2, norm_mix_g[0][None], sh1, sc1, w_main, w_dt)

    pad_h = lambda v: jnp.pad(v[None], ((0, 0), (0, LANES - N_HEADS)))
    yn = _ssd(proj, dt_raw, conv_w[0], conv_b[0][None], pad_h(dt_bias[0]), pad_h(A_log[0]),
              jnp.repeat(D_skip[0], HEADDIM)[None], ssd_norm_g[0][None])

    ws_gu = jnp.concatenate([ws_gate[0], ws_up[0]], axis=1).astype(BF16)
    m = EXPERT_BLOCK
    cap = -(-n // m) * m
    base, w8, cnt, xs = _mix(
        proj, yn, h2, pool_w[0].astype(BF16), pool_scale[0][None], w_br_pool[0].astype(BF16),
        w_br_ssd[0].astype(BF16), w_out[0].astype(BF16), g1, norm_ffn_g[0][None], sh2, sc2, g2,
        w_router[0].astype(BF16), router_bias[0][None], ws_gu, ws_down[0].astype(BF16), cap)

    counts = cnt[0].astype(jnp.int32)
    nblk = (counts + m - 1) // m
    bends = jnp.cumsum(nblk)
    bstarts = bends - nblk
    nb_used = bends[-1]
    n_blocks = -(-(n * TOP_K) // m) + N_EXPERTS + 1
    b_eff = jnp.minimum(jnp.arange(n_blocks, dtype=jnp.int32), nb_used - 1)
    block_e = jnp.sum((bends[None, :] <= b_eff[:, None]).astype(jnp.int32), axis=1)
    block_row = block_e * (cap // m) + b_eff - jnp.take(bstarts, block_e)

    xs = _padfill(counts, xs, cap)
    ytok = _experts(block_e, block_row, nb_used.reshape(1), xs,
                    we_gate[0], we_up[0], we_down[0], n * TOP_K)
    out = _combine(ytok, w8, base, g2, final_norm_g[None])
    return out.reshape(bsz, seq, D_MODEL)
```

```python
import functools

import jax
import jax.numpy as jnp
from jax import lax
from jax.experimental import pallas as pl
from jax.experimental.pallas import tpu as pltpu

F32 = jnp.float32
BF16 = jnp.bfloat16
HIGHEST = lax.Precision.HIGHEST

D_MODEL = 1024
D_SSD = 2048
HEADDIM = 64
N_HEADS = 32
N_GROUPS = 8
HEADS_PER_GROUP = N_HEADS // N_GROUPS
D_STATE = 128
CONV_K = 4
CHUNK = 128
GROUP_W = D_SSD // N_GROUPS
D_BC = N_GROUPS * D_STATE
POOL_WINDOWS = (2, 4, 8, 16)
POOL_GDIM = 256
N_EXPERTS = 64
TOP_K = 8
N_EGROUPS = 8
EXPERTS_PER_GROUP = 8
TOPK_GROUPS = 4
D_EXPERT = 256
D_SHARED = 256
ROUTED_SCALE = 2.5
MOE_BLOCK = 128
EPS = 1e-6

LANES = 128
SUBLANES = 8
ROW_TILES = D_MODEL // LANES
PACK_TILES = ROW_TILES // 2
PACK_W = PACK_TILES * LANES
EXPERT_BLOCK = 512
MIX_ISSUE_GROUPS = 8

PROJ_W = D_SSD + D_SSD + 2 * D_BC + 3 * D_MODEL
PROJ_CHUNK = 512

VMEM_LIMIT = 56 * 1024 * 1024


def _cparams(sem=("arbitrary",)):
    return pltpu.CompilerParams(dimension_semantics=sem, vmem_limit_bytes=VMEM_LIMIT)


def _full(shape):
    nd = len(shape)
    return pl.BlockSpec(shape, lambda *_: (0,) * nd)


def _silu(v):
    return v * jax.nn.sigmoid(v)


def _ada_kernel(c_ref, w_ref, b_ref, o_ref):
    c = c_ref[...]
    o_ref[...] = jnp.dot(_silu(c), w_ref[...], preferred_element_type=F32, precision=HIGHEST) + b_ref[...]


def _ada(c, w_ada, b_ada):
    n_out = w_ada.shape[1]
    tn = 1536
    c8 = jnp.broadcast_to(c, (SUBLANES, D_MODEL))
    out = pl.pallas_call(
        _ada_kernel,
        grid=(n_out // tn,),
        in_specs=[_full((SUBLANES, D_MODEL)),
                  pl.BlockSpec((D_MODEL, tn), lambda j: (0, j)),
                  pl.BlockSpec((1, tn), lambda j: (0, j))],
        out_specs=pl.BlockSpec((SUBLANES, tn), lambda j: (0, j)),
        out_shape=jax.ShapeDtypeStruct((SUBLANES, n_out), F32),
        compiler_params=_cparams(),
        name="ada",
    )(c8, w_ada, b_ada.reshape(1, n_out))
    return out[0:1]


def _inproj_kernel(x_ref, g_ref, sh_ref, sc_ref, w_ref, wdt_ref, proj_ref, dt_ref):
    x = x_ref[...]
    inv = lax.rsqrt(jnp.mean(x * x, axis=-1, keepdims=True) + EPS)
    u = x * inv * g_ref[...]
    u = u * (1.0 + sc_ref[...]) + sh_ref[...]
    ub = u.astype(BF16)
    for c0 in range(0, PROJ_W, PROJ_CHUNK):
        proj_ref[:, c0:c0 + PROJ_CHUNK] = jnp.dot(
            ub, w_ref[:, c0:c0 + PROJ_CHUNK], preferred_element_type=F32).astype(BF16)
    dt_ref[...] = jnp.dot(ub, wdt_ref[...], preferred_element_type=F32)


def _in_proj(x2, g, sh, sc, w_main, w_dt):
    n = x2.shape[0]
    tm = 256
    return pl.pallas_call(
        _inproj_kernel,
        grid=(n // tm,),
        in_specs=[pl.BlockSpec((tm, D_MODEL), lambda i: (i, 0)),
                  _full((1, D_MODEL)), _full((1, D_MODEL)), _full((1, D_MODEL)),
                  pl.BlockSpec((D_MODEL, PROJ_W), lambda i: (0, 0), pipeline_mode=pl.Buffered(1)),
                  _full((D_MODEL, LANES))],
        out_specs=[pl.BlockSpec((tm, PROJ_W), lambda i: (i, 0)),
                   pl.BlockSpec((tm, LANES), lambda i: (i, 0))],
        out_shape=[jax.ShapeDtypeStruct((n, PROJ_W), BF16),
                   jax.ShapeDtypeStruct((n, LANES), F32)],
        compiler_params=_cparams(),
        name="in_proj",
    )(x2, g, sh, sc, w_main, w_dt)


def _conv_silu(cur_ref, ext_ref, w_ref, b_ref, out_ref, width, cw=512):
    t = cur_ref.shape[0]
    for c0 in range(0, width, cw):
        sl = slice(c0, c0 + cw)
        cur = cur_ref[:, sl].astype(F32)
        ext_ref[SUBLANES:, sl] = cur
        acc = cur * w_ref[CONV_K - 1:CONV_K, sl] + b_ref[:, sl]
        for s in range(1, CONV_K):
            acc = acc + ext_ref[pl.ds(SUBLANES - s, t), sl] * w_ref[CONV_K - 1 - s:CONV_K - s, sl]
        out_ref[:, sl] = _silu(acc).astype(out_ref.dtype)
        ext_ref[0:SUBLANES, sl] = cur[t - SUBLANES:t]


def _ssd_kernel(z_ref, xs_ref, bc_ref, dt_ref, cwx_ref, cwbc_ref, cbx_ref, cbbc_ref, dtb_ref, alog_ref,
                dskip_ref, ng_ref, o_ref, tailx, tailbc, state, xc, bcc, ybuf):
    q = CHUNK
    pair_w = 2 * HEADDIM

    @pl.when(pl.program_id(0) == 0)
    def _():
        tailx[...] = jnp.zeros_like(tailx)
        tailbc[...] = jnp.zeros_like(tailbc)
        state[...] = jnp.zeros_like(state)

    _conv_silu(xs_ref, tailx, cwx_ref, cbx_ref, xc, D_SSD)
    _conv_silu(bc_ref, tailbc, cwbc_ref, cbbc_ref, bcc, 2 * D_BC)

    v = dt_ref[...] + dtb_ref[...]
    dt = jnp.maximum(v, 0.0) + jnp.log(1.0 + jnp.exp(-jnp.abs(v)))
    a = dt * (-jnp.exp(alog_ref[...]))
    ri = lax.broadcasted_iota(jnp.int32, (q, q), 0)
    ci = lax.broadcasted_iota(jnp.int32, (q, q), 1)
    causal = ri >= ci
    a_cs = jnp.dot(causal.astype(F32), a, preferred_element_type=F32, precision=HIGHEST)
    a_cs_t = a_cs.T
    dt_t = dt.T
    first_half = ci < HEADDIM
    first_half_row = first_half[0:1, :]

    for g in range(N_GROUPS):
        b_g = bcc[:, g * D_STATE:(g + 1) * D_STATE]
        c_g = bcc[:, D_BC + g * D_STATE:D_BC + (g + 1) * D_STATE]
        cb = lax.dot_general(c_g, b_g, (((1,), (1,)), ((), ())), preferred_element_type=F32)
        b_t = b_g.astype(F32).T
        st_g = state[g]
        y_off = jnp.dot(c_g, st_g.astype(BF16), preferred_element_type=F32)
        for pi in range(HEADS_PER_GROUP // 2):
            h0 = g * HEADS_PER_GROUP + 2 * pi
            lanes = slice(h0 * HEADDIM, h0 * HEADDIM + pair_w)
            gl = slice(pi * pair_w, (pi + 1) * pair_w)
            ms, ws, cols, lasts = [], [], [], []
            for h in (h0, h0 + 1):
                col = jnp.broadcast_to(a_cs[:, h:h + 1], (q, q))
                row = a_cs_t[h:h + 1, :]
                dtrow = dt_t[h:h + 1, :]
                decay = jnp.exp(jnp.where(causal, col - row, -jnp.inf))
                ms.append((cb * decay * dtrow).astype(BF16))
                last = a_cs_t[h:h + 1, q - 1:q]
                ws.append((b_t * (jnp.exp(last - row) * dtrow)).astype(BF16))
                cols.append(col)
                lasts.append(last)
            xp = xc[:, lanes]
            zero = jnp.zeros_like(xp)
            x_bd = jnp.concatenate([jnp.where(first_half, xp, zero), jnp.where(first_half, zero, xp)], axis=0)
            y_p = jnp.dot(jnp.concatenate(ms, axis=1), x_bd, preferred_element_type=F32)
            y_p = y_p + y_off[:, gl] * jnp.exp(jnp.where(first_half, cols[0], cols[1]))
            ybuf[:, lanes] = y_p
            s_new = jnp.dot(jnp.concatenate(ws, axis=1), x_bd, preferred_element_type=F32)
            carry = jnp.exp(jnp.where(first_half_row, lasts[0], lasts[1]))
            state[g, :, gl] = st_g[:, gl] * carry + s_new

    for g in range(N_GROUPS):
        sl = slice(g * GROUP_W, (g + 1) * GROUP_W)
        zg = z_ref[:, sl].astype(F32)
        yg = (ybuf[:, sl] + dskip_ref[:, sl] * xc[:, sl].astype(F32)) * _silu(zg)
        inv = lax.rsqrt(jnp.mean(yg * yg, axis=-1, keepdims=True) + EPS)
        o_ref[:, sl] = (yg * inv * ng_ref[:, sl]).astype(BF16)


def _ssd(proj, dt_raw, conv_w, conv_b, dt_bias, a_log, d_skip, norm_g):
    n = proj.shape[0]
    t = CHUNK
    return pl.pallas_call(
        _ssd_kernel,
        grid=(n // t,),
        in_specs=[pl.BlockSpec((t, D_SSD), lambda i: (i, 0)),
                  pl.BlockSpec((t, D_SSD), lambda i: (i, 1)),
                  pl.BlockSpec((t, 2 * D_BC), lambda i: (i, 2)),
                  pl.BlockSpec((t, LANES), lambda i: (i, 0)),
                  pl.BlockSpec((CONV_K, D_SSD), lambda i: (0, 0)),
                  pl.BlockSpec((CONV_K, 2 * D_BC), lambda i: (0, 1)),
                  pl.BlockSpec((1, D_SSD), lambda i: (0, 0)),
                  pl.BlockSpec((1, 2 * D_BC), lambda i: (0, 1)),
                  _full((1, LANES)), _full((1, LANES)),
                  _full((1, D_SSD)), _full((1, D_SSD))],
        out_specs=pl.BlockSpec((t, D_SSD), lambda i: (i, 0)),
        out_shape=jax.ShapeDtypeStruct((n, D_SSD), BF16),
        scratch_shapes=[pltpu.VMEM((SUBLANES + t, D_SSD), F32),
                        pltpu.VMEM((SUBLANES + t, 2 * D_BC), F32),
                        pltpu.VMEM((N_GROUPS, D_STATE, GROUP_W), F32),
                        pltpu.VMEM((t, D_SSD), BF16),
                        pltpu.VMEM((t, 2 * D_BC), BF16),
                        pltpu.VMEM((t, D_SSD), F32)],
        compiler_params=_cparams(),
        name="ssd",
    )(proj, proj, proj, dt_raw, conv_w, conv_w, conv_b, conv_b, dt_bias, a_log, d_skip, norm_g)


PAD_ID = 0xFFFFFFFF
TOK_ROW = PACK_TILES
MASK_ROWS = (PACK_TILES + 1, PACK_TILES + 2)


def _tile_rows(ref, first_row):
    return ref.at[pl.ds(pl.multiple_of(first_row * SUBLANES, SUBLANES), SUBLANES)]


def _mix_kernel(xp_ref, gs_ref, gp_ref, yn_ref, x_ref, poolw_ref, pscale_ref, wbp_ref, wbs_ref, wout_ref,
                g1_ref, ng_ref, sh2_ref, sc2_ref, g2_ref, wr_ref, rb_ref, wsgu_ref, wsd_ref, zero_ref,
                base_ref, ws_ref, cnt_ref, xs_ref, ptail, run, stage, dvm, dsm, sem, sem_s, *, n_tiles, cap):
    t = x_ref.shape[0]
    i = pl.program_id(0)
    slot = lax.rem(i, 2)

    @pl.when(i == 0)
    def _():
        ptail[...] = jnp.zeros_like(ptail)
        run[...] = jnp.zeros_like(run)
        stage[...] = jnp.zeros_like(stage)

    def issue_rows(s, group, after=None):
        zero = 0
        if after is not None:
            probe = jnp.max(jnp.abs(after[0:SUBLANES, 0:min(LANES, after.shape[1])])).astype(jnp.int32)
            zero = probe * zero_ref[0]
        per = t // MIX_ISSUE_GROUPS
        for tt in range(group * per, (group + 1) * per):
            for k in range(TOP_K):
                pltpu.make_async_copy(stage.at[s, pl.ds(tt * SUBLANES, SUBLANES)],
                                      _tile_rows(xs_ref, dsm[s, k, tt] + zero), sem).start()

    def drain_rows(s):
        for _ in range(TOP_K):
            pltpu.make_async_copy(stage.at[s], xs_ref.at[pl.ds(0, t * SUBLANES)], sem).wait()

    def tile(s, milestone=None):
        _mix_tile_body(xp_ref, gs_ref, gp_ref, yn_ref, x_ref, poolw_ref, pscale_ref, wbp_ref, wbs_ref, wout_ref,
                       g1_ref, ng_ref, sh2_ref, sc2_ref, g2_ref, wr_ref, rb_ref, wsgu_ref, wsd_ref,
                       base_ref, ws_ref, cnt_ref, ptail, run, stage, dvm, dsm, sem_s, i, s, cap, milestone)

    @pl.when(i == 0)
    def _():
        tile(0)

    @pl.when((i >= 1) & (i < n_tiles))
    def _():
        issue_rows(1 - slot, 0)
        tile(slot, lambda group, value: issue_rows(1 - slot, group, value))
        drain_rows(1 - slot)

    @pl.when(i == n_tiles)
    def _():
        for group in range(MIX_ISSUE_GROUPS):
            issue_rows(1 - slot, group)
        drain_rows(1 - slot)


def _mix_tile_body(xp_ref, gs_ref, gp_ref, yn_ref, x_ref, poolw_ref, pscale_ref, wbp_ref, wbs_ref, wout_ref,
                   g1_ref, ng_ref, sh2_ref, sc2_ref, g2_ref, wr_ref, rb_ref, wsgu_ref, wsd_ref,
                   base_ref, ws_ref, cnt_ref, ptail, run, stage, dvm, dsm, sem_s, i, slot, cap, milestone):
    t = x_ref.shape[0]
    if milestone is None:
        milestone = lambda group, value: None

    xp = xp_ref[...].astype(F32)
    ext = jnp.concatenate([ptail[...], xp], axis=0)
    ptail[...] = xp[t - 2 * SUBLANES:t]
    pos = (lax.broadcasted_iota(jnp.int32, (t, 1), 0) + (i * t + 1)).astype(F32)
    pooled = []
    for gi, w in enumerate(POOL_WINDOWS):
        sl = slice(gi * POOL_GDIM, (gi + 1) * POOL_GDIM)
        e = ext[:, sl]
        s = e
        span = 1
        while span < w:
            s = s + pltpu.roll(s, span, axis=0)
            span *= 2
        win = s[2 * SUBLANES:]
        mean = win / jnp.minimum(pos, float(w))
        pg = (mean - xp[:, sl]).astype(BF16)
        mixed = jnp.dot(pg, poolw_ref[gi], preferred_element_type=F32)
        pooled.append(mixed * pscale_ref[:, sl])
    pooled = jnp.concatenate(pooled, axis=1).astype(BF16)
    y_pool = jnp.dot(pooled, wbp_ref[...], preferred_element_type=F32)
    milestone(1, y_pool)
    y_ssd = jnp.dot(yn_ref[...], wbs_ref[...], preferred_element_type=F32)
    milestone(2, y_ssd)
    g_ssd = jax.nn.sigmoid(gs_ref[...].astype(F32))
    g_pool = jax.nn.sigmoid(gp_ref[...].astype(F32))
    mixed = (g_ssd * y_ssd + g_pool * y_pool).astype(BF16)
    h = x_ref[...] + g1_ref[...] * jnp.dot(mixed, wout_ref[...], preferred_element_type=F32)
    milestone(3, h)

    inv = lax.rsqrt(jnp.mean(h * h, axis=-1, keepdims=True) + EPS)
    u2 = h * inv * ng_ref[...]
    u2 = u2 * (1.0 + sc2_ref[...]) + sh2_ref[...]
    u2b = u2.astype(BF16)
    u2r = u2b.astype(F32)
    packed = []
    for j in range(PACK_TILES):
        lo = lax.bitcast_convert_type(u2r[:, j * LANES:(j + 1) * LANES], jnp.uint32)
        hi = lax.bitcast_convert_type(u2r[:, (j + PACK_TILES) * LANES:(j + PACK_TILES + 1) * LANES], jnp.uint32)
        packed.append((lo >> 16) | (hi & jnp.uint32(0xFFFF0000)))

    hs = jnp.dot(u2b, wsgu_ref[...], preferred_element_type=F32)
    milestone(4, hs)
    act = (_silu(hs[:, :D_SHARED]) * hs[:, D_SHARED:]).astype(BF16)
    shared = jnp.dot(act, wsd_ref[...], preferred_element_type=F32)
    milestone(5, shared)
    base_ref[...] = h + g2_ref[...] * shared

    logits = jnp.dot(u2b, wr_ref[...], preferred_element_type=F32)
    scores = jax.nn.sigmoid(logits)
    choice = scores + rb_ref[...]
    lane = lax.broadcasted_iota(jnp.int32, (t, N_EXPERTS), 1)
    lane_f = lane.astype(F32)
    lane_grp = lane // EXPERTS_PER_GROUP
    neg = -jnp.inf
    gscore = []
    for g in range(N_EGROUPS):
        vg = jnp.where(lane_grp == g, choice, neg)
        m1 = jnp.max(vg, axis=-1, keepdims=True)
        i1 = jnp.min(jnp.where(vg == m1, lane_f, float(N_EXPERTS)), axis=-1, keepdims=True)
        m2 = jnp.max(jnp.where(lane_f == i1, neg, vg), axis=-1, keepdims=True)
        gscore.append(m1 + m2)
    milestone(6, gscore[-1])
    gmask = jnp.zeros((t, N_EXPERTS), jnp.bool_)
    for g in range(N_EGROUPS):
        rank = jnp.zeros((t, 1), F32)
        for g2 in range(N_EGROUPS):
            if g2 == g:
                continue
            better = (gscore[g2] > gscore[g]) | ((gscore[g2] == gscore[g]) & (g2 < g))
            rank = rank + better.astype(F32)
        gmask = gmask | ((rank < float(TOPK_GROUPS)) & (lane_grp == g))
    work = jnp.where(gmask, choice, neg)
    sel = jnp.zeros((t, N_EXPERTS), jnp.bool_)
    onehots, idxs, sks = [], [], []
    for k in range(TOP_K):
        m = jnp.max(work, axis=-1, keepdims=True)
        idx = jnp.min(jnp.where(work == m, lane_f, float(N_EXPERTS)), axis=-1, keepdims=True)
        oh = lane_f == idx
        onehots.append(oh)
        idxs.append(idx)
        sks.append(jnp.sum(jnp.where(oh, scores, 0.0), axis=-1, keepdims=True))
        sel = sel | oh
        work = jnp.where(oh, neg, work)
    denom = sks[0]
    for k in range(1, TOP_K):
        denom = denom + sks[k]
    milestone(7, denom)

    ri = lax.broadcasted_iota(jnp.int32, (t, t), 0)
    ci = lax.broadcasted_iota(jnp.int32, (t, t), 1)
    before = (ri > ci).astype(BF16)
    sel_f = jnp.where(sel, 1.0, 0.0)
    sel_b = sel_f.astype(BF16)
    pos_tile = jnp.dot(before, sel_b, preferred_element_type=F32) + run[...]
    er = lax.broadcasted_iota(jnp.int32, (N_EXPERTS, N_EXPERTS), 0)
    ec = lax.broadcasted_iota(jnp.int32, (N_EXPERTS, N_EXPERTS), 1)
    rank_tile = jnp.dot(sel_b, (er < ec).astype(BF16), preferred_element_type=F32)
    lane128 = lax.broadcasted_iota(jnp.int32, (t, LANES), 1)
    k_iota = lax.broadcasted_iota(jnp.int32, (t, TOP_K), 1).astype(F32)
    dest = jnp.zeros((t, LANES), F32)
    w_sorted = jnp.zeros((t, TOP_K), F32)
    for k in range(TOP_K):
        pk = jnp.sum(jnp.where(onehots[k], pos_tile, 0.0), axis=-1, keepdims=True)
        rk = jnp.sum(jnp.where(onehots[k], rank_tile, 0.0), axis=-1, keepdims=True)
        dest = jnp.where(lane128 == k, idxs[k] * float(cap) + pk, dest)
        w_sorted = jnp.where(k_iota == rk, sks[k] / denom * ROUTED_SCALE, w_sorted)
    ws_ref[...] = w_sorted
    total = run[...] + jnp.sum(sel_f, axis=0, keepdims=True)
    run[...] = total
    cnt_ref[...] = total

    for j in range(PACK_TILES):
        stage[slot, pl.ds(j, t, stride=SUBLANES), :] = packed[j]
    tok = lax.broadcasted_iota(jnp.int32, (t, LANES), 0) + i * t
    stage[slot, pl.ds(TOK_ROW, t, stride=SUBLANES), :] = tok.astype(jnp.uint32)
    bit = jnp.where(sel, jnp.left_shift(1, lane & 15), 0).astype(F32)
    words = []
    for q in range(N_EXPERTS // 16):
        part = jnp.sum(jnp.where((lane >> 4) == q, bit, 0.0), axis=-1, keepdims=True)
        words.append(part.astype(jnp.int32).astype(jnp.uint32))
    for w, row in enumerate(MASK_ROWS):
        word = words[2 * w] | (words[2 * w + 1] << 16)
        stage[slot, pl.ds(row, t, stride=SUBLANES), :] = jnp.broadcast_to(word, (t, LANES))

    dvm[...] = dest.T[0:TOP_K, :].astype(jnp.int32)
    smem_copy = pltpu.make_async_copy(dvm, dsm.at[slot], sem_s)
    smem_copy.start()
    smem_copy.wait()


def _mix(proj, yn, x2, pool_w, pool_scale, w_br_pool, w_br_ssd, w_out, g1, ng, sh2, sc2, g2,
         w_router, router_bias, ws_gu, ws_down, cap):
    n = x2.shape[0]
    t = 256
    n_tiles = n // t
    xp_blk = (D_SSD + D_SSD + 2 * D_BC) // D_MODEL
    tile = lambda i: jnp.minimum(i, n_tiles - 1)
    row = lambda i: (tile(i), 0)
    return pl.pallas_call(
        functools.partial(_mix_kernel, n_tiles=n_tiles, cap=cap),
        grid=(n_tiles + 1,),
        in_specs=[pl.BlockSpec((t, D_MODEL), lambda i: (tile(i), xp_blk)),
                  pl.BlockSpec((t, D_MODEL), lambda i: (tile(i), xp_blk + 1)),
                  pl.BlockSpec((t, D_MODEL), lambda i: (tile(i), xp_blk + 2)),
                  pl.BlockSpec((t, D_SSD), row),
                  pl.BlockSpec((t, D_MODEL), row),
                  _full((len(POOL_WINDOWS), POOL_GDIM, POOL_GDIM)),
                  _full((1, D_MODEL)),
                  _full((D_MODEL, D_MODEL)),
                  _full((D_SSD, D_MODEL)),
                  _full((D_MODEL, D_MODEL)),
                  _full((1, D_MODEL)), _full((1, D_MODEL)), _full((1, D_MODEL)), _full((1, D_MODEL)),
                  _full((1, D_MODEL)),
                  _full((D_MODEL, N_EXPERTS)),
                  _full((1, N_EXPERTS)),
                  _full((D_MODEL, 2 * D_SHARED)),
                  _full((D_SHARED, D_MODEL)),
                  pl.BlockSpec(memory_space=pltpu.SMEM)],
        out_specs=[pl.BlockSpec((t, D_MODEL), row),
                   pl.BlockSpec((t, TOP_K), row),
                   _full((1, N_EXPERTS)),
                   pl.BlockSpec(memory_space=pl.ANY)],
        out_shape=[jax.ShapeDtypeStruct((n, D_MODEL), F32),
                   jax.ShapeDtypeStruct((n, TOP_K), F32),
                   jax.ShapeDtypeStruct((1, N_EXPERTS), F32),
                   jax.ShapeDtypeStruct((N_EXPERTS * cap * SUBLANES, LANES), jnp.uint32)],
        scratch_shapes=[pltpu.VMEM((2 * SUBLANES, D_MODEL), F32),
                        pltpu.VMEM((1, N_EXPERTS), F32),
                        pltpu.VMEM((2, t * SUBLANES, LANES), jnp.uint32),
                        pltpu.VMEM((TOP_K, t), jnp.int32),
                        pltpu.SMEM((2, TOP_K, t), jnp.int32),
                        pltpu.SemaphoreType.DMA(()),
                        pltpu.SemaphoreType.DMA(())],
        compiler_params=_cparams(),
        name="mix",
    )(proj, proj, proj, yn, x2, pool_w, pool_scale, w_br_pool, w_br_ssd, w_out, g1, ng, sh2, sc2, g2,
      w_router, router_bias, ws_gu, ws_down, jnp.zeros((1,), jnp.int32))


def _padfill_kernel(cnt_ref, xs_in_ref, xs_ref, padbuf, sem, *, cap):
    del xs_in_ref
    sub = lax.broadcasted_iota(jnp.int32, padbuf.shape, 0) & (SUBLANES - 1)
    padbuf[...] = jnp.where(sub == TOK_ROW, jnp.uint32(PAD_ID), jnp.uint32(0))
    sizes = [1 << bit for bit in range(EXPERT_BLOCK.bit_length() - 1)]

    def pad_copies(e, fn):
        c = cnt_ref[e]
        n_pad = (EXPERT_BLOCK - (c & (EXPERT_BLOCK - 1))) & (EXPERT_BLOCK - 1)
        first = e * cap + c
        for size in sizes:
            @pl.when((n_pad & size) != 0)
            def _():
                start = first + (n_pad & (size - 1))
                dst = xs_ref.at[pl.ds(pl.multiple_of(start * SUBLANES, SUBLANES), size * SUBLANES)]
                fn(pltpu.make_async_copy(padbuf.at[pl.ds(0, size * SUBLANES)], dst, sem))

    def issue(e, carry):
        pad_copies(e, lambda cp: cp.start())
        return carry

    lax.fori_loop(0, N_EXPERTS, issue, 0)

    def drain(e, carry):
        pad_copies(e, lambda cp: cp.wait())
        return carry

    lax.fori_loop(0, N_EXPERTS, drain, 0)


def _padfill(counts, xs, cap):
    return pl.pallas_call(
        functools.partial(_padfill_kernel, cap=cap),
        grid_spec=pltpu.PrefetchScalarGridSpec(
            num_scalar_prefetch=1,
            grid=(1,),
            in_specs=[pl.BlockSpec(memory_space=pl.ANY)],
            out_specs=pl.BlockSpec(memory_space=pl.ANY),
            scratch_shapes=[pltpu.VMEM((EXPERT_BLOCK // 2 * SUBLANES, LANES), jnp.uint32),
                            pltpu.SemaphoreType.DMA(())]),
        out_shape=jax.ShapeDtypeStruct(xs.shape, xs.dtype),
        input_output_aliases={1: 0},
        compiler_params=_cparams(),
        name="padfill",
    )(counts, xs)


def _expert_kernel(be_ref, br_ref, nb_ref, x_ref, wg_ref, wu_ref, wd_ref, ytok_ref, ybuf0, ybuf1, idv, ids0, ids1,
                   wgb, wub, wdb, sem, sem_ids, *, n_ids):
    del br_ref
    m = EXPERT_BLOCK
    b = pl.program_id(0)
    nb = nb_ref[0]
    ybufs, idss = (ybuf0, ybuf1), (ids0, ids1)
    n_phases = m // LANES
    half = D_MODEL // 2

    @pl.when((b == 0) | (be_ref[b] != be_ref[jnp.maximum(b - 1, 0)]))
    def _():
        wgb[...] = wg_ref[0].astype(BF16)
        wub[...] = wu_ref[0].astype(BF16)
        wdb[...] = wd_ref[0].astype(BF16)

    def issue(s, a0):
        for col in range(LANES):
            pltpu.make_async_copy(_tile_rows(ybufs[s], a0 * LANES + col),
                                  _tile_rows(ytok_ref, idss[s][a0, col]), sem.at[s]).start()

    def step(s, prev):
        tokrep = x_ref[pl.ds(TOK_ROW, m, stride=SUBLANES), :]
        e = be_ref[b]
        below_lo = jnp.where(e >= 32, -1, (1 << jnp.minimum(e, 31)) - 1).astype(jnp.uint32)
        below_hi = jnp.where(e >= 32, (1 << jnp.maximum(e - 32, 0)) - 1, 0).astype(jnp.uint32)
        rank = (lax.population_count(x_ref[pl.ds(MASK_ROWS[0], m, stride=SUBLANES), :] & below_lo)
                + lax.population_count(x_ref[pl.ds(MASK_ROWS[1], m, stride=SUBLANES), :] & below_hi))
        r = lax.broadcasted_iota(jnp.int32, (m, LANES), 0)
        ln = lax.broadcasted_iota(jnp.int32, (m, LANES), 1)
        n_tok = n_ids // TOP_K
        idi = jnp.where(tokrep == jnp.uint32(PAD_ID), n_ids + r,
                        rank.astype(jnp.int32) * n_tok + tokrep.astype(jnp.int32))
        diag = jnp.where((r & (LANES - 1)) == ln, idi, 0).astype(F32)
        idv[...] = jnp.sum(diag.reshape(m // LANES, LANES, LANES), axis=1).astype(jnp.int32)
        ids_copy = pltpu.make_async_copy(idv, idss[s], sem_ids)
        ids_copy.start()

        lo, hi = [], []
        for j in range(PACK_TILES):
            w = x_ref[pl.ds(j, m, stride=SUBLANES), :]
            lo.append(lax.bitcast_convert_type(w << 16, F32).astype(BF16))
            hi.append(lax.bitcast_convert_type(w & jnp.uint32(0xFFFF0000), F32).astype(BF16))
        x = jnp.concatenate(lo + hi, axis=1)
        phases = iter(range(n_phases))
        if prev is not None:
            issue(prev, next(phases))
        gate = jnp.dot(x, wgb[...], preferred_element_type=F32)
        if prev is not None:
            issue(prev, next(phases))
        up = jnp.dot(x, wub[...], preferred_element_type=F32)
        hid = (_silu(gate) * up).astype(BF16)
        for c0 in (0, half):
            if prev is not None:
                issue(prev, next(phases))
            y = jnp.dot(hid, wdb[:, c0:c0 + half], preferred_element_type=F32)
            for j in range(half // LANES):
                ybufs[s][pl.ds(c0 // LANES + j, m, stride=SUBLANES), :] = y[:, j * LANES:(j + 1) * LANES]
        ids_copy.wait()
        if prev is not None:
            for a0 in phases:
                issue(prev, a0)
            pltpu.make_async_copy(ybufs[prev], ytok_ref.at[pl.ds(0, m * SUBLANES)], sem.at[prev]).wait()

    @pl.when(b == 0)
    def _():
        step(0, None)

    for s in range(2):
        @pl.when((b >= 1) & (b <= nb) & (lax.rem(b, 2) == s))
        def _():
            step(s, 1 - s)


def _experts(block_e, block_row, nb_used, xs, we_gate, we_up, we_down, n_ids):
    n_blocks = block_e.shape[0]
    m = EXPERT_BLOCK
    rows = m * SUBLANES
    return pl.pallas_call(
        functools.partial(_expert_kernel, n_ids=n_ids),
        grid_spec=pltpu.PrefetchScalarGridSpec(
            num_scalar_prefetch=3,
            grid=(n_blocks,),
            in_specs=[pl.BlockSpec((rows, LANES), lambda b, be, br, nb: (br[b], 0)),
                      pl.BlockSpec((1, D_MODEL, D_EXPERT), lambda b, be, br, nb: (be[b], 0, 0)),
                      pl.BlockSpec((1, D_MODEL, D_EXPERT), lambda b, be, br, nb: (be[b], 0, 0)),
                      pl.BlockSpec((1, D_EXPERT, D_MODEL), lambda b, be, br, nb: (be[b], 0, 0))],
            out_specs=pl.BlockSpec(memory_space=pl.ANY),
            scratch_shapes=[pltpu.VMEM((rows, LANES), F32),
                            pltpu.VMEM((rows, LANES), F32),
                            pltpu.VMEM((m // LANES, LANES), jnp.int32),
                            pltpu.SMEM((m // LANES, LANES), jnp.int32),
                            pltpu.SMEM((m // LANES, LANES), jnp.int32),
                            pltpu.VMEM((D_MODEL, D_EXPERT), BF16),
                            pltpu.VMEM((D_MODEL, D_EXPERT), BF16),
                            pltpu.VMEM((D_EXPERT, D_MODEL), BF16),
                            pltpu.SemaphoreType.DMA((2,)),
                            pltpu.SemaphoreType.DMA(())]),
        out_shape=jax.ShapeDtypeStruct(((n_ids + m) * SUBLANES, LANES), F32),
        compiler_params=_cparams(),
        name="experts",
    )(block_e, block_row, nb_used, xs, we_gate, we_up, we_down)


def _combine_kernel(*refs):
    y_refs = refs[:TOP_K]
    w8_ref, base_ref, g2_ref, fg_ref, o_ref = refs[TOP_K:]
    t7 = base_ref.shape[0]
    w8 = w8_ref[...]
    parts = []
    for j in range(ROW_TILES):
        acc = jnp.zeros((t7, LANES), F32)
        for k in range(TOP_K):
            acc = acc + w8[:, k:k + 1] * y_refs[k][pl.ds(j, t7, stride=SUBLANES), :]
        parts.append(acc)
    routed = jnp.concatenate(parts, axis=1)
    h = base_ref[...] + g2_ref[...] * routed
    inv = lax.rsqrt(jnp.mean(h * h, axis=-1, keepdims=True) + EPS)
    o_ref[...] = h * inv * fg_ref[...]


def _combine(ytok, w8, base, g2, fg):
    n = base.shape[0]
    t7 = 128
    y_specs = [pl.BlockSpec((t7 * SUBLANES, LANES), functools.partial(lambda i, k: (k * (n // t7) + i, 0), k=k))
               for k in range(TOP_K)]
    return pl.pallas_call(
        _combine_kernel,
        grid=(n // t7,),
        in_specs=y_specs + [
                  pl.BlockSpec((t7, TOP_K), lambda i: (i, 0)),
                  pl.BlockSpec((t7, D_MODEL), lambda i: (i, 0)),
                  _full((1, D_MODEL)), _full((1, D_MODEL))],
        out_specs=pl.BlockSpec((t7, D_MODEL), lambda i: (i, 0)),
        out_shape=jax.ShapeDtypeStruct((n, D_MODEL), F32),
        compiler_params=_cparams(),
        name="combine",
    )(*([ytok] * TOP_K), w8, base, g2, fg)


def kernel(x, c, w_ada, b_ada, norm_mix_g, w_in, conv_w, conv_b, dt_bias, A_log, D_skip, ssd_norm_g, pool_w,
           pool_scale, w_br_ssd, w_br_pool, w_out, norm_ffn_g, w_router, router_bias, we_gate, we_up, we_down,
           ws_gate, ws_up, ws_down, final_norm_g):
    bsz, seq, _ = x.shape
    assert bsz == 1 and w_ada.shape[0] == 1
    n = seq
    h2 = x.reshape(n, D_MODEL)

    mod = _ada(c, w_ada[0], b_ada[0])
    sh1, sc1, g1, sh2, sc2, g2 = [mod[:, k * D_MODEL:(k + 1) * D_MODEL] for k in range(6)]

    wi = w_in[0]
    o_dt = D_SSD + D_SSD + 2 * D_BC
    w_main = jnp.concatenate([wi[:, :o_dt], wi[:, o_dt + N_HEADS:]], axis=1).astype(BF16)
    w_dt = jnp.pad(wi[:, o_dt:o_dt + N_HEADS], ((0, 0), (0, LANES - N_HEADS))).astype(BF16)
    proj, dt_raw = _in_proj(h2, norm_mix_g[0][None], sh1, sc1, w_main, w_dt)

    pad_h = lambda v: jnp.pad(v[None], ((0, 0), (0, LANES - N_HEADS)))
    yn = _ssd(proj, dt_raw, conv_w[0], conv_b[0][None], pad_h(dt_bias[0]), pad_h(A_log[0]),
              jnp.repeat(D_skip[0], HEADDIM)[None], ssd_norm_g[0][None])

    ws_gu = jnp.concatenate([ws_gate[0], ws_up[0]], axis=1).astype(BF16)
    m = EXPERT_BLOCK
    cap = -(-n // m) * m
    base, w8, cnt, xs = _mix(
        proj, yn, h2, pool_w[0].astype(BF16), pool_scale[0][None], w_br_pool[0].astype(BF16),
        w_br_ssd[0].astype(BF16), w_out[0].astype(BF16), g1, norm_ffn_g[0][None], sh2, sc2, g2,
        w_router[0].astype(BF16), router_bias[0][None], ws_gu, ws_down[0].astype(BF16), cap)

    counts = cnt[0].astype(jnp.int32)
    nblk = (counts + m - 1) // m
    bends = jnp.cumsum(nblk)
    bstarts = bends - nblk
    nb_used = bends[-1]
    n_blocks = -(-(n * TOP_K) // m) + N_EXPERTS + 1
    b_eff = jnp.minimum(jnp.arange(n_blocks, dtype=jnp.int32), nb_used - 1)
    block_e = jnp.sum((bends[None, :] <= b_eff[:, None]).astype(jnp.int32), axis=1)
    block_row = block_e * (cap // m) + b_eff - jnp.take(bstarts, block_e)

    xs = _padfill(counts, xs, cap)
    ytok = _experts(block_e, block_row, nb_used.reshape(1), xs,
                    we_gate[0], we_up[0], we_down[0], n * TOP_K)
    out = _combine(ytok, w8, base, g2, final_norm_g[None])
    return out.reshape(bsz, seq, D_MODEL)
```

```python
import functools

import jax
import jax.numpy as jnp
from jax import lax
from jax.experimental import pallas as pl
from jax.experimental.pallas import tpu as pltpu

F32 = jnp.float32
BF16 = jnp.bfloat16
HIGHEST = lax.Precision.HIGHEST

D_MODEL = 1024
D_SSD = 2048
HEADDIM = 64
N_HEADS = 32
N_GROUPS = 8
HEADS_PER_GROUP = N_HEADS // N_GROUPS
D_STATE = 128
CONV_K = 4
CHUNK = 128
GROUP_W = D_SSD // N_GROUPS
D_BC = N_GROUPS * D_STATE
POOL_WINDOWS = (2, 4, 8, 16)
POOL_GDIM = 256
N_EXPERTS = 64
TOP_K = 8
N_EGROUPS = 8
EXPERTS_PER_GROUP = 8
TOPK_GROUPS = 4
D_EXPERT = 256
D_SHARED = 256
ROUTED_SCALE = 2.5
MOE_BLOCK = 128
EPS = 1e-6

LANES = 128
SUBLANES = 8
ROW_TILES = D_MODEL // LANES
PACK_TILES = ROW_TILES // 2
PACK_W = PACK_TILES * LANES
EXPERT_BLOCK = 512
MIX_ISSUE_GROUPS = 8

PROJ_W = D_SSD + D_SSD + 2 * D_BC + 3 * D_MODEL
PROJ_CHUNK = 512

VMEM_LIMIT = 56 * 1024 * 1024


def _cparams(sem=("arbitrary",)):
    return pltpu.CompilerParams(dimension_semantics=sem, vmem_limit_bytes=VMEM_LIMIT)


def _full(shape):
    nd = len(shape)
    return pl.BlockSpec(shape, lambda *_: (0,) * nd)


def _silu(v):
    return v * jax.nn.sigmoid(v)


def _ada_kernel(c_ref, w_ref, b_ref, o_ref):
    c = c_ref[...]
    o_ref[...] = jnp.dot(_silu(c), w_ref[...], preferred_element_type=F32, precision=HIGHEST) + b_ref[...]


def _ada(c, w_ada, b_ada):
    n_out = w_ada.shape[1]
    tn = 1536
    c8 = jnp.broadcast_to(c, (SUBLANES, D_MODEL))
    out = pl.pallas_call(
        _ada_kernel,
        grid=(n_out // tn,),
        in_specs=[_full((SUBLANES, D_MODEL)),
                  pl.BlockSpec((D_MODEL, tn), lambda j: (0, j)),
                  pl.BlockSpec((1, tn), lambda j: (0, j))],
        out_specs=pl.BlockSpec((SUBLANES, tn), lambda j: (0, j)),
        out_shape=jax.ShapeDtypeStruct((SUBLANES, n_out), F32),
        compiler_params=_cparams(),
        name="ada",
    )(c8, w_ada, b_ada.reshape(1, n_out))
    return out[0:1]


def _inproj_kernel(x_ref, g_ref, sh_ref, sc_ref, w_ref, wdt_ref, proj_ref, dt_ref):
    x = x_ref[...]
    inv = lax.rsqrt(jnp.mean(x * x, axis=-1, keepdims=True) + EPS)
    u = x * inv * g_ref[...]
    u = u * (1.0 + sc_ref[...]) + sh_ref[...]
    ub = u.astype(BF16)
    for c0 in range(0, PROJ_W, PROJ_CHUNK):
        proj_ref[:, c0:c0 + PROJ_CHUNK] = jnp.dot(
            ub, w_ref[:, c0:c0 + PROJ_CHUNK], preferred_element_type=F32).astype(BF16)
    dt_ref[...] = jnp.dot(ub, wdt_ref[...], preferred_element_type=F32)


def _in_proj(x2, g, sh, sc, w_main, w_dt):
    n = x2.shape[0]
    tm = 256
    return pl.pallas_call(
        _inproj_kernel,
        grid=(n // tm,),
        in_specs=[pl.BlockSpec((tm, D_MODEL), lambda i: (i, 0)),
                  _full((1, D_MODEL)), _full((1, D_MODEL)), _full((1, D_MODEL)),
                  pl.BlockSpec((D_MODEL, PROJ_W), lambda i: (0, 0), pipeline_mode=pl.Buffered(1)),
                  _full((D_MODEL, LANES))],
        out_specs=[pl.BlockSpec((tm, PROJ_W), lambda i: (i, 0)),
                   pl.BlockSpec((tm, LANES), lambda i: (i, 0))],
        out_shape=[jax.ShapeDtypeStruct((n, PROJ_W), BF16),
                   jax.ShapeDtypeStruct((n, LANES), F32)],
        compiler_params=_cparams(),
        name="in_proj",
    )(x2, g, sh, sc, w_main, w_dt)


def _conv_silu(cur_ref, ext_ref, w_ref, b_ref, out_ref, width, cw=512):
    t = cur_ref.shape[0]
    for c0 in range(0, width, cw):
        sl = slice(c0, c0 + cw)
        cur = cur_ref[:, sl].astype(F32)
        ext_ref[SUBLANES:, sl] = cur
        acc = cur * w_ref[CONV_K - 1:CONV_K, sl] + b_ref[:, sl]
        for s in range(1, CONV_K):
            acc = acc + ext_ref[pl.ds(SUBLANES - s, t), sl] * w_ref[CONV_K - 1 - s:CONV_K - s, sl]
        out_ref[:, sl] = _silu(acc).astype(out_ref.dtype)
        ext_ref[0:SUBLANES, sl] = cur[t - SUBLANES:t]


def _ssd_kernel(z_ref, xs_ref, bc_ref, dt_ref, cwx_ref, cwbc_ref, cbx_ref, cbbc_ref, dtb_ref, alog_ref,
                dskip_ref, ng_ref, o_ref, tailx, tailbc, state, xc, bcc, ybuf):
    q = CHUNK
    pair_w = 2 * HEADDIM

    @pl.when(pl.program_id(0) == 0)
    def _():
        tailx[...] = jnp.zeros_like(tailx)
        tailbc[...] = jnp.zeros_like(tailbc)
        state[...] = jnp.zeros_like(state)

    _conv_silu(xs_ref, tailx, cwx_ref, cbx_ref, xc, D_SSD)
    _conv_silu(bc_ref, tailbc, cwbc_ref, cbbc_ref, bcc, 2 * D_BC)

    v = dt_ref[...] + dtb_ref[...]
    dt = jnp.maximum(v, 0.0) + jnp.log(1.0 + jnp.exp(-jnp.abs(v)))
    a = dt * (-jnp.exp(alog_ref[...]))
    ri = lax.broadcasted_iota(jnp.int32, (q, q), 0)
    ci = lax.broadcasted_iota(jnp.int32, (q, q), 1)
    causal = ri >= ci
    a_cs = jnp.dot(causal.astype(F32), a, preferred_element_type=F32, precision=HIGHEST)
    a_cs_t = a_cs.T
    dt_t = dt.T
    first_half = ci < HEADDIM
    first_half_row = first_half[0:1, :]

    for g in range(N_GROUPS):
        b_g = bcc[:, g * D_STATE:(g + 1) * D_STATE]
        c_g = bcc[:, D_BC + g * D_STATE:D_BC + (g + 1) * D_STATE]
        cb = lax.dot_general(c_g, b_g, (((1,), (1,)), ((), ())), preferred_element_type=F32)
        b_t = b_g.astype(F32).T
        st_g = state[g]
        y_off = jnp.dot(c_g, st_g.astype(BF16), preferred_element_type=F32)
        for pi in range(HEADS_PER_GROUP // 2):
            h0 = g * HEADS_PER_GROUP + 2 * pi
            lanes = slice(h0 * HEADDIM, h0 * HEADDIM + pair_w)
            gl = slice(pi * pair_w, (pi + 1) * pair_w)
            ms, ws, cols, lasts = [], [], [], []
            for h in (h0, h0 + 1):
                col = jnp.broadcast_to(a_cs[:, h:h + 1], (q, q))
                row = a_cs_t[h:h + 1, :]
                dtrow = dt_t[h:h + 1, :]
                decay = jnp.exp(jnp.where(causal, col - row, -jnp.inf))
                ms.append((cb * decay * dtrow).astype(BF16))
                last = a_cs_t[h:h + 1, q - 1:q]
                ws.append((b_t * (jnp.exp(last - row) * dtrow)).astype(BF16))
                cols.append(col)
                lasts.append(last)
            xp = xc[:, lanes]
            zero = jnp.zeros_like(xp)
            x_bd = jnp.concatenate([jnp.where(first_half, xp, zero), jnp.where(first_half, zero, xp)], axis=0)
            y_p = jnp.dot(jnp.concatenate(ms, axis=1), x_bd, preferred_element_type=F32)
            y_p = y_p + y_off[:, gl] * jnp.exp(jnp.where(first_half, cols[0], cols[1]))
            ybuf[:, lanes] = y_p
            s_new = jnp.dot(jnp.concatenate(ws, axis=1), x_bd, preferred_element_type=F32)
            carry = jnp.exp(jnp.where(first_half_row, lasts[0], lasts[1]))
            state[g, :, gl] = st_g[:, gl] * carry + s_new

    for g in range(N_GROUPS):
        sl = slice(g * GROUP_W, (g + 1) * GROUP_W)
        zg = z_ref[:, sl].astype(F32)
        yg = (ybuf[:, sl] + dskip_ref[:, sl] * xc[:, sl].astype(F32)) * _silu(zg)
        inv = lax.rsqrt(jnp.mean(yg * yg, axis=-1, keepdims=True) + EPS)
        o_ref[:, sl] = (yg * inv * ng_ref[:, sl]).astype(BF16)


def _ssd(proj, dt_raw, conv_w, conv_b, dt_bias, a_log, d_skip, norm_g):
    n = proj.shape[0]
    t = CHUNK
    return pl.pallas_call(
        _ssd_kernel,
        grid=(n // t,),
        in_specs=[pl.BlockSpec((t, D_SSD), lambda i: (i, 0)),
                  pl.BlockSpec((t, D_SSD), lambda i: (i, 1)),
                  pl.BlockSpec((t, 2 * D_BC), lambda i: (i, 2)),
                  pl.BlockSpec((t, LANES), lambda i: (i, 0)),
                  pl.BlockSpec((CONV_K, D_SSD), lambda i: (0, 0)),
                  pl.BlockSpec((CONV_K, 2 * D_BC), lambda i: (0, 1)),
                  pl.BlockSpec((1, D_SSD), lambda i: (0, 0)),
                  pl.BlockSpec((1, 2 * D_BC), lambda i: (0, 1)),
                  _full((1, LANES)), _full((1, LANES)),
                  _full((1, D_SSD)), _full((1, D_SSD))],
        out_specs=pl.BlockSpec((t, D_SSD), lambda i: (i, 0)),
        out_shape=jax.ShapeDtypeStruct((n, D_SSD), BF16),
        scratch_shapes=[pltpu.VMEM((SUBLANES + t, D_SSD), F32),
                        pltpu.VMEM((SUBLANES + t, 2 * D_BC), F32),
                        pltpu.VMEM((N_GROUPS, D_STATE, GROUP_W), F32),
                        pltpu.VMEM((t, D_SSD), BF16),
                        pltpu.VMEM((t, 2 * D_BC), BF16),
                        pltpu.VMEM((t, D_SSD), F32)],
        compiler_params=_cparams(),
        name="ssd",
    )(proj, proj, proj, dt_raw, conv_w, conv_w, conv_b, conv_b, dt_bias, a_log, d_skip, norm_g)


PAD_ID = 0xFFFFFFFF
TOK_ROW = PACK_TILES
MASK_ROWS = (PACK_TILES + 1, PACK_TILES + 2)


def _tile_rows(ref, first_row):
    return ref.at[pl.ds(pl.multiple_of(first_row * SUBLANES, SUBLANES), SUBLANES)]


def _mix_kernel(xp_ref, gs_ref, gp_ref, yn_ref, x_ref, poolw_ref, pscale_ref, wbp_ref, wbs_ref, wout_ref,
                g1_ref, ng_ref, sh2_ref, sc2_ref, g2_ref, wr_ref, rb_ref, wsgu_ref, wsd_ref, zero_ref,
                base_ref, ws_ref, cnt_ref, xs_ref, ptail, run, stage, dvm, dsm, sem, sem_s, *, n_tiles, cap):
    t = x_ref.shape[0]
    i = pl.program_id(0)
    slot = lax.rem(i, 2)

    @pl.when(i == 0)
    def _():
        ptail[...] = jnp.zeros_like(ptail)
        run[...] = jnp.zeros_like(run)
        stage[...] = jnp.zeros_like(stage)

    def issue_rows(s, group, after=None):
        zero = 0
        if after is not None:
            probe = jnp.max(jnp.abs(after[0:SUBLANES, 0:min(LANES, after.shape[1])])).astype(jnp.int32)
            zero = probe * zero_ref[0]
        per = t // MIX_ISSUE_GROUPS
        for tt in range(group * per, (group + 1) * per):
            for k in range(TOP_K):
                pltpu.make_async_copy(stage.at[s, pl.ds(tt * SUBLANES, SUBLANES)],
                                      _tile_rows(xs_ref, dsm[s, k, tt] + zero), sem).start()

    def drain_rows(s):
        for _ in range(TOP_K):
            pltpu.make_async_copy(stage.at[s], xs_ref.at[pl.ds(0, t * SUBLANES)], sem).wait()

    def tile(s, milestone=None):
        _mix_tile_body(xp_ref, gs_ref, gp_ref, yn_ref, x_ref, poolw_ref, pscale_ref, wbp_ref, wbs_ref, wout_ref,
                       g1_ref, ng_ref, sh2_ref, sc2_ref, g2_ref, wr_ref, rb_ref, wsgu_ref, wsd_ref,
                       base_ref, ws_ref, cnt_ref, ptail, run, stage, dvm, dsm, sem_s, i, s, cap, milestone)

    @pl.when(i == 0)
    def _():
        tile(0)

    @pl.when((i >= 1) & (i < n_tiles))
    def _():
        issue_rows(1 - slot, 0)
        tile(slot, lambda group, value: issue_rows(1 - slot, group, value))
        drain_rows(1 - slot)

    @pl.when(i == n_tiles)
    def _():
        for group in range(MIX_ISSUE_GROUPS):
            issue_rows(1 - slot, group)
        drain_rows(1 - slot)


def _mix_tile_body(xp_ref, gs_ref, gp_ref, yn_ref, x_ref, poolw_ref, pscale_ref, wbp_ref, wbs_ref, wout_ref,
                   g1_ref, ng_ref, sh2_ref, sc2_ref, g2_ref, wr_ref, rb_ref, wsgu_ref, wsd_ref,
                   base_ref, ws_ref, cnt_ref, ptail, run, stage, dvm, dsm, sem_s, i, slot, cap, milestone):
    t = x_ref.shape[0]
    if milestone is None:
        milestone = lambda group, value: None

    xp = xp_ref[...].astype(F32)
    ext = jnp.concatenate([ptail[...], xp], axis=0)
    ptail[...] = xp[t - 2 * SUBLANES:t]
    pos = (lax.broadcasted_iota(jnp.int32, (t, 1), 0) + (i * t + 1)).astype(F32)
    pooled = []
    for gi, w in enumerate(POOL_WINDOWS):
        sl = slice(gi * POOL_GDIM, (gi + 1) * POOL_GDIM)
        e = ext[:, sl]
        s = e
        span = 1
        while span < w:
            s = s + pltpu.roll(s, span, axis=0)
            span *= 2
        win = s[2 * SUBLANES:]
        mean = win / jnp.minimum(pos, float(w))
        pg = (mean - xp[:, sl]).astype(BF16)
        mixed = jnp.dot(pg, poolw_ref[gi], preferred_element_type=F32)
        pooled.append(mixed * pscale_ref[:, sl])
    pooled = jnp.concatenate(pooled, axis=1).astype(BF16)
    y_pool = jnp.dot(pooled, wbp_ref[...], preferred_element_type=F32)
    milestone(1, y_pool)
    y_ssd = jnp.dot(yn_ref[...], wbs_ref[...], preferred_element_type=F32)
    milestone(2, y_ssd)
    g_ssd = jax.nn.sigmoid(gs_ref[...].astype(F32))
    g_pool = jax.nn.sigmoid(gp_ref[...].astype(F32))
    mixed = (g_ssd * y_ssd + g_pool * y_pool).astype(BF16)
    h = x_ref[...] + g1_ref[...] * jnp.dot(mixed, wout_ref[...], preferred_element_type=F32)
    milestone(3, h)

    inv = lax.rsqrt(jnp.mean(h * h, axis=-1, keepdims=True) + EPS)
    u2 = h * inv * ng_ref[...]
    u2 = u2 * (1.0 + sc2_ref[...]) + sh2_ref[...]
    u2b = u2.astype(BF16)
    u2r = u2b.astype(F32)
    packed = []
    for j in range(PACK_TILES):
        lo = lax.bitcast_convert_type(u2r[:, j * LANES:(j + 1) * LANES], jnp.uint32)
        hi = lax.bitcast_convert_type(u2r[:, (j + PACK_TILES) * LANES:(j + PACK_TILES + 1) * LANES], jnp.uint32)
        packed.append((lo >> 16) | (hi & jnp.uint32(0xFFFF0000)))

    hs = jnp.dot(u2b, wsgu_ref[...], preferred_element_type=F32)
    milestone(4, hs)
    act = (_silu(hs[:, :D_SHARED]) * hs[:, D_SHARED:]).astype(BF16)
    shared = jnp.dot(act, wsd_ref[...], preferred_element_type=F32)
    milestone(5, shared)
    base_ref[...] = h + g2_ref[...] * shared

    logits = jnp.dot(u2b, wr_ref[...], preferred_element_type=F32)
    scores = jax.nn.sigmoid(logits)
    choice = scores + rb_ref[...]
    lane = lax.broadcasted_iota(jnp.int32, (t, N_EXPERTS), 1)
    lane_f = lane.astype(F32)
    lane_grp = lane // EXPERTS_PER_GROUP
    neg = -jnp.inf
    gscore = []
    for g in range(N_EGROUPS):
        vg = jnp.where(lane_grp == g, choice, neg)
        m1 = jnp.max(vg, axis=-1, keepdims=True)
        i1 = jnp.min(jnp.where(vg == m1, lane_f, float(N_EXPERTS)), axis=-1, keepdims=True)
        m2 = jnp.max(jnp.where(lane_f == i1, neg, vg), axis=-1, keepdims=True)
        gscore.append(m1 + m2)
    milestone(6, gscore[-1])
    gmask = jnp.zeros((t, N_EXPERTS), jnp.bool_)
    for g in range(N_EGROUPS):
        rank = jnp.zeros((t, 1), F32)
        for g2 in range(N_EGROUPS):
            if g2 == g:
                continue
            better = (gscore[g2] > gscore[g]) | ((gscore[g2] == gscore[g]) & (g2 < g))
            rank = rank + better.astype(F32)
        gmask = gmask | ((rank < float(TOPK_GROUPS)) & (lane_grp == g))
    work = jnp.where(gmask, choice, neg)
    sel = jnp.zeros((t, N_EXPERTS), jnp.bool_)
    onehots, idxs, sks = [], [], []
    for k in range(TOP_K):
        m = jnp.max(work, axis=-1, keepdims=True)
        idx = jnp.min(jnp.where(work == m, lane_f, float(N_EXPERTS)), axis=-1, keepdims=True)
        oh = lane_f == idx
        onehots.append(oh)
        idxs.append(idx)
        sks.append(jnp.sum(jnp.where(oh, scores, 0.0), axis=-1, keepdims=True))
        sel = sel | oh
        work = jnp.where(oh, neg, work)
    denom = sks[0]
    for k in range(1, TOP_K):
        denom = denom + sks[k]
    milestone(7, denom)

    ri = lax.broadcasted_iota(jnp.int32, (t, t), 0)
    ci = lax.broadcasted_iota(jnp.int32, (t, t), 1)
    before = (ri > ci).astype(BF16)
    sel_f = jnp.where(sel, 1.0, 0.0)
    sel_b = sel_f.astype(BF16)
    pos_tile = jnp.dot(before, sel_b, preferred_element_type=F32) + run[...]
    er = lax.broadcasted_iota(jnp.int32, (N_EXPERTS, N_EXPERTS), 0)
    ec = lax.broadcasted_iota(jnp.int32, (N_EXPERTS, N_EXPERTS), 1)
    rank_tile = jnp.dot(sel_b, (er < ec).astype(BF16), preferred_element_type=F32)
    lane128 = lax.broadcasted_iota(jnp.int32, (t, LANES), 1)
    k_iota = lax.broadcasted_iota(jnp.int32, (t, TOP_K), 1).astype(F32)
    dest = jnp.zeros((t, LANES), F32)
    w_sorted = jnp.zeros((t, TOP_K), F32)
    for k in range(TOP_K):
        pk = jnp.sum(jnp.where(onehots[k], pos_tile, 0.0), axis=-1, keepdims=True)
        rk = jnp.sum(jnp.where(onehots[k], rank_tile, 0.0), axis=-1, keepdims=True)
        dest = jnp.where(lane128 == k, idxs[k] * float(cap) + pk, dest)
        w_sorted = jnp.where(k_iota == rk, sks[k] / denom * ROUTED_SCALE, w_sorted)
    ws_ref[...] = w_sorted
    total = run[...] + jnp.sum(sel_f, axis=0, keepdims=True)
    run[...] = total
    cnt_ref[...] = total

    for j in range(PACK_TILES):
        stage[slot, pl.ds(j, t, stride=SUBLANES), :] = packed[j]
    tok = lax.broadcasted_iota(jnp.int32, (t, LANES), 0) + i * t
    stage[slot, pl.ds(TOK_ROW, t, stride=SUBLANES), :] = tok.astype(jnp.uint32)
    bit = jnp.where(sel, jnp.left_shift(1, lane & 15), 0).astype(F32)
    words = []
    for q in range(N_EXPERTS // 16):
        part = jnp.sum(jnp.where((lane >> 4) == q, bit, 0.0), axis=-1, keepdims=True)
        words.append(part.astype(jnp.int32).astype(jnp.uint32))
    for w, row in enumerate(MASK_ROWS):
        word = words[2 * w] | (words[2 * w + 1] << 16)
        stage[slot, pl.ds(row, t, stride=SUBLANES), :] = jnp.broadcast_to(word, (t, LANES))

    dvm[...] = dest.T[0:TOP_K, :].astype(jnp.int32)
    smem_copy = pltpu.make_async_copy(dvm, dsm.at[slot], sem_s)
    smem_copy.start()
    smem_copy.wait()


def _mix(proj, yn, x2, pool_w, pool_scale, w_br_pool, w_br_ssd, w_out, g1, ng, sh2, sc2, g2,
         w_router, router_bias, ws_gu, ws_down, cap):
    n = x2.shape[0]
    t = 256
    n_tiles = n // t
    xp_blk = (D_SSD + D_SSD + 2 * D_BC) // D_MODEL
    tile = lambda i: jnp.minimum(i, n_tiles - 1)
    row = lambda i: (tile(i), 0)
    return pl.pallas_call(
        functools.partial(_mix_kernel, n_tiles=n_tiles, cap=cap),
        grid=(n_tiles + 1,),
        in_specs=[pl.BlockSpec((t, D_MODEL), lambda i: (tile(i), xp_blk)),
                  pl.BlockSpec((t, D_MODEL), lambda i: (tile(i), xp_blk + 1)),
                  pl.BlockSpec((t, D_MODEL), lambda i: (tile(i), xp_blk + 2)),
                  pl.BlockSpec((t, D_SSD), row),
                  pl.BlockSpec((t, D_MODEL), row),
                  _full((len(POOL_WINDOWS), POOL_GDIM, POOL_GDIM)),
                  _full((1, D_MODEL)),
                  _full((D_MODEL, D_MODEL)),
                  _full((D_SSD, D_MODEL)),
                  _full((D_MODEL, D_MODEL)),
                  _full((1, D_MODEL)), _full((1, D_MODEL)), _full((1, D_MODEL)), _full((1, D_MODEL)),
                  _full((1, D_MODEL)),
                  _full((D_MODEL, N_EXPERTS)),
                  _full((1, N_EXPERTS)),
                  _full((D_MODEL, 2 * D_SHARED)),
                  _full((D_SHARED, D_MODEL)),
                  pl.BlockSpec(memory_space=pltpu.SMEM)],
        out_specs=[pl.BlockSpec((t, D_MODEL), row),
                   pl.BlockSpec((t, TOP_K), row),
                   _full((1, N_EXPERTS)),
                   pl.BlockSpec(memory_space=pl.ANY)],
        out_shape=[jax.ShapeDtypeStruct((n, D_MODEL), F32),
                   jax.ShapeDtypeStruct((n, TOP_K), F32),
                   jax.ShapeDtypeStruct((1, N_EXPERTS), F32),
                   jax.ShapeDtypeStruct((N_EXPERTS * cap * SUBLANES, LANES), jnp.uint32)],
        scratch_shapes=[pltpu.VMEM((2 * SUBLANES, D_MODEL), F32),
                        pltpu.VMEM((1, N_EXPERTS), F32),
                        pltpu.VMEM((2, t * SUBLANES, LANES), jnp.uint32),
                        pltpu.VMEM((TOP_K, t), jnp.int32),
                        pltpu.SMEM((2, TOP_K, t), jnp.int32),
                        pltpu.SemaphoreType.DMA(()),
                        pltpu.SemaphoreType.DMA(())],
        compiler_params=_cparams(),
        name="mix",
    )(proj, proj, proj, yn, x2, pool_w, pool_scale, w_br_pool, w_br_ssd, w_out, g1, ng, sh2, sc2, g2,
      w_router, router_bias, ws_gu, ws_down, jnp.zeros((1,), jnp.int32))


def _padfill_kernel(cnt_ref, xs_in_ref, xs_ref, padbuf, sem, *, cap):
    del xs_in_ref
    sub = lax.broadcasted_iota(jnp.int32, padbuf.shape, 0) & (SUBLANES - 1)
    padbuf[...] = jnp.where(sub == TOK_ROW, jnp.uint32(PAD_ID), jnp.uint32(0))
    sizes = [1 << bit for bit in range(EXPERT_BLOCK.bit_length() - 1)]

    def pad_copies(e, fn):
        c = cnt_ref[e]
        n_pad = (EXPERT_BLOCK - (c & (EXPERT_BLOCK - 1))) & (EXPERT_BLOCK - 1)
        first = e * cap + c
        for size in sizes:
            @pl.when((n_pad & size) != 0)
            def _():
                start = first + (n_pad & (size - 1))
                dst = xs_ref.at[pl.ds(pl.multiple_of(start * SUBLANES, SUBLANES), size * SUBLANES)]
                fn(pltpu.make_async_copy(padbuf.at[pl.ds(0, size * SUBLANES)], dst, sem))

    def issue(e, carry):
        pad_copies(e, lambda cp: cp.start())
        return carry

    lax.fori_loop(0, N_EXPERTS, issue, 0)

    def drain(e, carry):
        pad_copies(e, lambda cp: cp.wait())
        return carry

    lax.fori_loop(0, N_EXPERTS, drain, 0)


def _padfill(counts, xs, cap):
    return pl.pallas_call(
        functools.partial(_padfill_kernel, cap=cap),
        grid_spec=pltpu.PrefetchScalarGridSpec(
            num_scalar_prefetch=1,
            grid=(1,),
            in_specs=[pl.BlockSpec(memory_space=pl.ANY)],
            out_specs=pl.BlockSpec(memory_space=pl.ANY),
            scratch_shapes=[pltpu.VMEM((EXPERT_BLOCK // 2 * SUBLANES, LANES), jnp.uint32),
                            pltpu.SemaphoreType.DMA(())]),
        out_shape=jax.ShapeDtypeStruct(xs.shape, xs.dtype),
        input_output_aliases={1: 0},
        compiler_params=_cparams(),
        name="padfill",
    )(counts, xs)


def _expert_kernel(be_ref, br_ref, bv_ref, nb_ref, x_ref, wg_ref, wu_ref, wd_ref, ytok_ref, ybuf0, ybuf1, idv, ids0, ids1,
                   wgb, wub, wdb, sem, sem_ids, *, n_ids):
    del br_ref
    m = EXPERT_BLOCK
    b = pl.program_id(0)
    nb = nb_ref[0]
    ybufs, idss = (ybuf0, ybuf1), (ids0, ids1)
    n_phases = m // LANES
    half = D_MODEL // 2

    @pl.when((b == 0) | (be_ref[b] != be_ref[jnp.maximum(b - 1, 0)]))
    def _():
        wgb[...] = wg_ref[0].astype(BF16)
        wub[...] = wu_ref[0].astype(BF16)
        wdb[...] = wd_ref[0].astype(BF16)

    n_prev = bv_ref[jnp.maximum(b - 1, 0)]

    def issue(s, a0):
        for col in range(LANES):
            @pl.when(a0 * LANES + col < n_prev)
            def _():
                pltpu.make_async_copy(_tile_rows(ybufs[s], a0 * LANES + col),
                                      _tile_rows(ytok_ref, idss[s][a0, col]), sem.at[s]).start()

    def drain(s):
        for size in [1 << k for k in range(m.bit_length())]:
            @pl.when((n_prev & size) != 0)
            def _():
                pltpu.make_async_copy(ybufs[s].at[pl.ds(0, size * SUBLANES)],
                                      ytok_ref.at[pl.ds(0, size * SUBLANES)], sem.at[s]).wait()

    def step(s, prev):
        tokrep = x_ref[pl.ds(TOK_ROW, m, stride=SUBLANES), :]
        e = be_ref[b]
        below_lo = jnp.where(e >= 32, -1, (1 << jnp.minimum(e, 31)) - 1).astype(jnp.uint32)
        below_hi = jnp.where(e >= 32, (1 << jnp.maximum(e - 32, 0)) - 1, 0).astype(jnp.uint32)
        rank = (lax.population_count(x_ref[pl.ds(MASK_ROWS[0], m, stride=SUBLANES), :] & below_lo)
                + lax.population_count(x_ref[pl.ds(MASK_ROWS[1], m, stride=SUBLANES), :] & below_hi))
        r = lax.broadcasted_iota(jnp.int32, (m, LANES), 0)
        ln = lax.broadcasted_iota(jnp.int32, (m, LANES), 1)
        n_tok = n_ids // TOP_K
        idi = jnp.where(tokrep == jnp.uint32(PAD_ID), n_ids + r,
                        rank.astype(jnp.int32) * n_tok + tokrep.astype(jnp.int32))
        diag = jnp.where((r & (LANES - 1)) == ln, idi, 0).astype(F32)
        idv[...] = jnp.sum(diag.reshape(m // LANES, LANES, LANES), axis=1).astype(jnp.int32)
        ids_copy = pltpu.make_async_copy(idv, idss[s], sem_ids)
        ids_copy.start()

        lo, hi = [], []
        for j in range(PACK_TILES):
            w = x_ref[pl.ds(j, m, stride=SUBLANES), :]
            lo.append(lax.bitcast_convert_type(w << 16, F32).astype(BF16))
            hi.append(lax.bitcast_convert_type(w & jnp.uint32(0xFFFF0000), F32).astype(BF16))
        x = jnp.concatenate(lo + hi, axis=1)
        phases = iter(range(n_phases))
        if prev is not None:
            issue(prev, next(phases))
        gate = jnp.dot(x, wgb[...], preferred_element_type=F32)
        if prev is not None:
            issue(prev, next(phases))
        up = jnp.dot(x, wub[...], preferred_element_type=F32)
        hid = (_silu(gate) * up).astype(BF16)
        for c0 in (0, half):
            if prev is not None:
                issue(prev, next(phases))
            y = jnp.dot(hid, wdb[:, c0:c0 + half], preferred_element_type=F32)
            for j in range(half // LANES):
                ybufs[s][pl.ds(c0 // LANES + j, m, stride=SUBLANES), :] = y[:, j * LANES:(j + 1) * LANES]
        ids_copy.wait()
        if prev is not None:
            for a0 in phases:
                issue(prev, a0)
            drain(prev)

    @pl.when(b == 0)
    def _():
        step(0, None)

    for s in range(2):
        @pl.when((b >= 1) & (b <= nb) & (lax.rem(b, 2) == s))
        def _():
            step(s, 1 - s)


def _experts(block_e, block_row, block_valid, nb_used, xs, we_gate, we_up, we_down, n_ids):
    n_blocks = block_e.shape[0]
    m = EXPERT_BLOCK
    rows = m * SUBLANES
    return pl.pallas_call(
        functools.partial(_expert_kernel, n_ids=n_ids),
        grid_spec=pltpu.PrefetchScalarGridSpec(
            num_scalar_prefetch=4,
            grid=(n_blocks,),
            in_specs=[pl.BlockSpec((rows, LANES), lambda b, be, br, bv, nb: (br[b], 0)),
                      pl.BlockSpec((1, D_MODEL, D_EXPERT), lambda b, be, br, bv, nb: (be[b], 0, 0)),
                      pl.BlockSpec((1, D_MODEL, D_EXPERT), lambda b, be, br, bv, nb: (be[b], 0, 0)),
                      pl.BlockSpec((1, D_EXPERT, D_MODEL), lambda b, be, br, bv, nb: (be[b], 0, 0))],
            out_specs=pl.BlockSpec(memory_space=pl.ANY),
            scratch_shapes=[pltpu.VMEM((rows, LANES), F32),
                            pltpu.VMEM((rows, LANES), F32),
                            pltpu.VMEM((m // LANES, LANES), jnp.int32),
                            pltpu.SMEM((m // LANES, LANES), jnp.int32),
                            pltpu.SMEM((m // LANES, LANES), jnp.int32),
                            pltpu.VMEM((D_MODEL, D_EXPERT), BF16),
                            pltpu.VMEM((D_MODEL, D_EXPERT), BF16),
                            pltpu.VMEM((D_EXPERT, D_MODEL), BF16),
                            pltpu.SemaphoreType.DMA((2,)),
                            pltpu.SemaphoreType.DMA(())]),
        out_shape=jax.ShapeDtypeStruct(((n_ids + m) * SUBLANES, LANES), F32),
        compiler_params=_cparams(),
        name="experts",
    )(block_e, block_row, block_valid, nb_used, xs, we_gate, we_up, we_down)


def _combine_kernel(*refs):
    y_refs = refs[:TOP_K]
    w8_ref, base_ref, g2_ref, fg_ref, o_ref = refs[TOP_K:]
    t7 = base_ref.shape[0]
    w8 = w8_ref[...]
    parts = []
    for j in range(ROW_TILES):
        acc = jnp.zeros((t7, LANES), F32)
        for k in range(TOP_K):
            acc = acc + w8[:, k:k + 1] * y_refs[k][pl.ds(j, t7, stride=SUBLANES), :]
        parts.append(acc)
    routed = jnp.concatenate(parts, axis=1)
    h = base_ref[...] + g2_ref[...] * routed
    inv = lax.rsqrt(jnp.mean(h * h, axis=-1, keepdims=True) + EPS)
    o_ref[...] = h * inv * fg_ref[...]


def _combine(ytok, w8, base, g2, fg):
    n = base.shape[0]
    t7 = 128
    y_specs = [pl.BlockSpec((t7 * SUBLANES, LANES), functools.partial(lambda i, k: (k * (n // t7) + i, 0), k=k))
               for k in range(TOP_K)]
    return pl.pallas_call(
        _combine_kernel,
        grid=(n // t7,),
        in_specs=y_specs + [
                  pl.BlockSpec((t7, TOP_K), lambda i: (i, 0)),
                  pl.BlockSpec((t7, D_MODEL), lambda i: (i, 0)),
                  _full((1, D_MODEL)), _full((1, D_MODEL))],
        out_specs=pl.BlockSpec((t7, D_MODEL), lambda i: (i, 0)),
        out_shape=jax.ShapeDtypeStruct((n, D_MODEL), F32),
        compiler_params=_cparams(),
        name="combine",
    )(*([ytok] * TOP_K), w8, base, g2, fg)


def kernel(x, c, w_ada, b_ada, norm_mix_g, w_in, conv_w, conv_b, dt_bias, A_log, D_skip, ssd_norm_g, pool_w,
           pool_scale, w_br_ssd, w_br_pool, w_out, norm_ffn_g, w_router, router_bias, we_gate, we_up, we_down,
           ws_gate, ws_up, ws_down, final_norm_g):
    bsz, seq, _ = x.shape
    assert bsz == 1 and w_ada.shape[0] == 1
    n = seq
    h2 = x.reshape(n, D_MODEL)

    mod = _ada(c, w_ada[0], b_ada[0])
    sh1, sc1, g1, sh2, sc2, g2 = [mod[:, k * D_MODEL:(k + 1) * D_MODEL] for k in range(6)]

    wi = w_in[0]
    o_dt = D_SSD + D_SSD + 2 * D_BC
    w_main = jnp.concatenate([wi[:, :o_dt], wi[:, o_dt + N_HEADS:]], axis=1).astype(BF16)
    w_dt = jnp.pad(wi[:, o_dt:o_dt + N_HEADS], ((0, 0), (0, LANES - N_HEADS))).astype(BF16)
    proj, dt_raw = _in_proj(h2, norm_mix_g[0][None], sh1, sc1, w_main, w_dt)

    pad_h = lambda v: jnp.pad(v[None], ((0, 0), (0, LANES - N_HEADS)))
    yn = _ssd(proj, dt_raw, conv_w[0], conv_b[0][None], pad_h(dt_bias[0]), pad_h(A_log[0]),
              jnp.repeat(D_skip[0], HEADDIM)[None], ssd_norm_g[0][None])

    ws_gu = jnp.concatenate([ws_gate[0], ws_up[0]], axis=1).astype(BF16)
    m = EXPERT_BLOCK
    cap = -(-n // m) * m
    base, w8, cnt, xs = _mix(
        proj, yn, h2, pool_w[0].astype(BF16), pool_scale[0][None], w_br_pool[0].astype(BF16),
        w_br_ssd[0].astype(BF16), w_out[0].astype(BF16), g1, norm_ffn_g[0][None], sh2, sc2, g2,
        w_router[0].astype(BF16), router_bias[0][None], ws_gu, ws_down[0].astype(BF16), cap)

    counts = cnt[0].astype(jnp.int32)
    nblk = (counts + m - 1) // m
    bends = jnp.cumsum(nblk)
    bstarts = bends - nblk
    nb_used = bends[-1]
    n_blocks = -(-(n * TOP_K) // m) + N_EXPERTS + 1
    b_eff = jnp.minimum(jnp.arange(n_blocks, dtype=jnp.int32), nb_used - 1)
    block_e = jnp.sum((bends[None, :] <= b_eff[:, None]).astype(jnp.int32), axis=1)
    block_row = block_e * (cap // m) + b_eff - jnp.take(bstarts, block_e)

    xs = _padfill(counts, xs, cap)
    block_valid = jnp.clip(jnp.take(counts, block_e) - (b_eff - jnp.take(bstarts, block_e)) * m, 0, m)
    ytok = _experts(block_e, block_row, block_valid, nb_used.reshape(1), xs,
                    we_gate[0], we_up[0], we_down[0], n * TOP_K)
    out = _combine(ytok, w8, base, g2, final_norm_g[None])
    return out.reshape(bsz, seq, D_MODEL)
```

```python
import functools

import jax
import jax.numpy as jnp
from jax import lax
from jax.experimental import pallas as pl
from jax.experimental.pallas import tpu as pltpu

F32 = jnp.float32
BF16 = jnp.bfloat16
HIGHEST = lax.Precision.HIGHEST

D_MODEL = 1024
D_SSD = 2048
HEADDIM = 64
N_HEADS = 32
N_GROUPS = 8
HEADS_PER_GROUP = N_HEADS // N_GROUPS
D_STATE = 128
CONV_K = 4
CHUNK = 128
GROUP_W = D_SSD // N_GROUPS
D_BC = N_GROUPS * D_STATE
POOL_WINDOWS = (2, 4, 8, 16)
POOL_GDIM = 256
N_EXPERTS = 64
TOP_K = 8
N_EGROUPS = 8
EXPERTS_PER_GROUP = 8
TOPK_GROUPS = 4
D_EXPERT = 256
D_SHARED = 256
ROUTED_SCALE = 2.5
MOE_BLOCK = 128
EPS = 1e-6

LANES = 128
SUBLANES = 8
ROW_TILES = D_MODEL // LANES
PACK_TILES = ROW_TILES // 2
PACK_W = PACK_TILES * LANES
EXPERT_BLOCK = 512
MIX_ISSUE_GROUPS = 8

PROJ_W = D_SSD + D_SSD + 2 * D_BC + 3 * D_MODEL
PROJ_CHUNK = 512

VMEM_LIMIT = 56 * 1024 * 1024


def _cparams(sem=("arbitrary",)):
    return pltpu.CompilerParams(dimension_semantics=sem, vmem_limit_bytes=VMEM_LIMIT)


def _full(shape):
    nd = len(shape)
    return pl.BlockSpec(shape, lambda *_: (0,) * nd)


def _silu(v):
    return v * jax.nn.sigmoid(v)


def _ada_kernel(c_ref, w_ref, b_ref, o_ref):
    c = c_ref[...]
    o_ref[...] = jnp.dot(_silu(c), w_ref[...], preferred_element_type=F32, precision=HIGHEST) + b_ref[...]


def _ada(c, w_ada, b_ada):
    n_out = w_ada.shape[1]
    tn = 1536
    c8 = jnp.broadcast_to(c, (SUBLANES, D_MODEL))
    out = pl.pallas_call(
        _ada_kernel,
        grid=(n_out // tn,),
        in_specs=[_full((SUBLANES, D_MODEL)),
                  pl.BlockSpec((D_MODEL, tn), lambda j: (0, j)),
                  pl.BlockSpec((1, tn), lambda j: (0, j))],
        out_specs=pl.BlockSpec((SUBLANES, tn), lambda j: (0, j)),
        out_shape=jax.ShapeDtypeStruct((SUBLANES, n_out), F32),
        compiler_params=_cparams(),
        name="ada",
    )(c8, w_ada, b_ada.reshape(1, n_out))
    return out[0:1]


def _inproj_kernel(x_ref, g_ref, sh_ref, sc_ref, wa_ref, wb_ref, wdt_ref, proj_ref, dt_ref):
    x = x_ref[...]
    inv = lax.rsqrt(jnp.mean(x * x, axis=-1, keepdims=True) + EPS)
    u = x * inv * g_ref[...]
    u = u * (1.0 + sc_ref[...]) + sh_ref[...]
    ub = u.astype(BF16)
    wa = wa_ref.shape[1]
    for c0 in range(0, PROJ_W, PROJ_CHUNK):
        w = wa_ref[:, c0:c0 + PROJ_CHUNK] if c0 < wa else wb_ref[:, c0 - wa:c0 - wa + PROJ_CHUNK]
        proj_ref[:, c0:c0 + PROJ_CHUNK] = jnp.dot(ub, w, preferred_element_type=F32).astype(BF16)
    dt_ref[...] = jnp.dot(ub, wdt_ref[...], preferred_element_type=F32)


def _in_proj(x2, g, sh, sc, w_a, w_b, w_dt):
    n = x2.shape[0]
    tm = 256
    return pl.pallas_call(
        _inproj_kernel,
        grid=(n // tm,),
        in_specs=[pl.BlockSpec((tm, D_MODEL), lambda i: (i, 0)),
                  _full((1, D_MODEL)), _full((1, D_MODEL)), _full((1, D_MODEL)),
                  pl.BlockSpec(w_a.shape, lambda i: (0, 0), pipeline_mode=pl.Buffered(1)),
                  pl.BlockSpec(w_b.shape, lambda i: (0, 0), pipeline_mode=pl.Buffered(1)),
                  _full((D_MODEL, LANES))],
        out_specs=[pl.BlockSpec((tm, PROJ_W), lambda i: (i, 0)),
                   pl.BlockSpec((tm, LANES), lambda i: (i, 0))],
        out_shape=[jax.ShapeDtypeStruct((n, PROJ_W), BF16),
                   jax.ShapeDtypeStruct((n, LANES), F32)],
        compiler_params=_cparams(),
        name="in_proj",
    )(x2, g, sh, sc, w_a, w_b, w_dt)


def _conv_silu(cur_ref, ext_ref, w_ref, b_ref, out_ref, width, cw=512):
    t = cur_ref.shape[0]
    for c0 in range(0, width, cw):
        sl = slice(c0, c0 + cw)
        cur = cur_ref[:, sl].astype(F32)
        ext_ref[SUBLANES:, sl] = cur
        acc = cur * w_ref[CONV_K - 1:CONV_K, sl] + b_ref[:, sl]
        for s in range(1, CONV_K):
            acc = acc + ext_ref[pl.ds(SUBLANES - s, t), sl] * w_ref[CONV_K - 1 - s:CONV_K - s, sl]
        out_ref[:, sl] = _silu(acc).astype(out_ref.dtype)
        ext_ref[0:SUBLANES, sl] = cur[t - SUBLANES:t]


def _ssd_kernel(z_ref, xs_ref, bc_ref, dt_ref, cwx_ref, cwbc_ref, cbx_ref, cbbc_ref, dtb_ref, alog_ref,
                dskip_ref, ng_ref, o_ref, tailx, tailbc, state, xc, bcc, ybuf):
    q = CHUNK
    pair_w = 2 * HEADDIM

    @pl.when(pl.program_id(0) == 0)
    def _():
        tailx[...] = jnp.zeros_like(tailx)
        tailbc[...] = jnp.zeros_like(tailbc)
        state[...] = jnp.zeros_like(state)

    _conv_silu(xs_ref, tailx, cwx_ref, cbx_ref, xc, D_SSD)
    _conv_silu(bc_ref, tailbc, cwbc_ref, cbbc_ref, bcc, 2 * D_BC)

    v = dt_ref[...] + dtb_ref[...]
    dt = jnp.maximum(v, 0.0) + jnp.log(1.0 + jnp.exp(-jnp.abs(v)))
    a = dt * (-jnp.exp(alog_ref[...]))
    ri = lax.broadcasted_iota(jnp.int32, (q, q), 0)
    ci = lax.broadcasted_iota(jnp.int32, (q, q), 1)
    causal = ri >= ci
    a_cs = jnp.dot(causal.astype(F32), a, preferred_element_type=F32, precision=HIGHEST)
    a_cs_t = a_cs.T
    dt_t = dt.T
    first_half = ci < HEADDIM
    first_half_row = first_half[0:1, :]

    for g in range(N_GROUPS):
        b_g = bcc[:, g * D_STATE:(g + 1) * D_STATE]
        c_g = bcc[:, D_BC + g * D_STATE:D_BC + (g + 1) * D_STATE]
        cb = lax.dot_general(c_g, b_g, (((1,), (1,)), ((), ())), preferred_element_type=F32)
        b_t = b_g.astype(F32).T
        st_g = state[g]
        y_off = jnp.dot(c_g, st_g.astype(BF16), preferred_element_type=F32)
        for pi in range(HEADS_PER_GROUP // 2):
            h0 = g * HEADS_PER_GROUP + 2 * pi
            lanes = slice(h0 * HEADDIM, h0 * HEADDIM + pair_w)
            gl = slice(pi * pair_w, (pi + 1) * pair_w)
            ms, ws, cols, lasts = [], [], [], []
            for h in (h0, h0 + 1):
                col = jnp.broadcast_to(a_cs[:, h:h + 1], (q, q))
                row = a_cs_t[h:h + 1, :]
                dtrow = dt_t[h:h + 1, :]
                decay = jnp.exp(jnp.where(causal, col - row, -jnp.inf))
                ms.append((cb * decay * dtrow).astype(BF16))
                last = a_cs_t[h:h + 1, q - 1:q]
                ws.append((b_t * (jnp.exp(last - row) * dtrow)).astype(BF16))
                cols.append(col)
                lasts.append(last)
            xp = xc[:, lanes]
            zero = jnp.zeros_like(xp)
            x_bd = jnp.concatenate([jnp.where(first_half, xp, zero), jnp.where(first_half, zero, xp)], axis=0)
            y_p = jnp.dot(jnp.concatenate(ms, axis=1), x_bd, preferred_element_type=F32)
            y_p = y_p + y_off[:, gl] * jnp.exp(jnp.where(first_half, cols[0], cols[1]))
            ybuf[:, lanes] = y_p
            s_new = jnp.dot(jnp.concatenate(ws, axis=1), x_bd, preferred_element_type=F32)
            carry = jnp.exp(jnp.where(first_half_row, lasts[0], lasts[1]))
            state[g, :, gl] = st_g[:, gl] * carry + s_new

    for g in range(N_GROUPS):
        sl = slice(g * GROUP_W, (g + 1) * GROUP_W)
        zg = z_ref[:, sl].astype(F32)
        yg = (ybuf[:, sl] + dskip_ref[:, sl] * xc[:, sl].astype(F32)) * _silu(zg)
        inv = lax.rsqrt(jnp.mean(yg * yg, axis=-1, keepdims=True) + EPS)
        o_ref[:, sl] = (yg * inv * ng_ref[:, sl]).astype(BF16)


def _ssd(proj, dt_raw, conv_w, conv_b, dt_bias, a_log, d_skip, norm_g):
    n = proj.shape[0]
    t = CHUNK
    return pl.pallas_call(
        _ssd_kernel,
        grid=(n // t,),
        in_specs=[pl.BlockSpec((t, D_SSD), lambda i: (i, 0)),
                  pl.BlockSpec((t, D_SSD), lambda i: (i, 1)),
                  pl.BlockSpec((t, 2 * D_BC), lambda i: (i, 2)),
                  pl.BlockSpec((t, LANES), lambda i: (i, 0)),
                  pl.BlockSpec((CONV_K, D_SSD), lambda i: (0, 0)),
                  pl.BlockSpec((CONV_K, 2 * D_BC), lambda i: (0, 1)),
                  pl.BlockSpec((1, D_SSD), lambda i: (0, 0)),
                  pl.BlockSpec((1, 2 * D_BC), lambda i: (0, 1)),
                  _full((1, LANES)), _full((1, LANES)),
                  _full((1, D_SSD)), _full((1, D_SSD))],
        out_specs=pl.BlockSpec((t, D_SSD), lambda i: (i, 0)),
        out_shape=jax.ShapeDtypeStruct((n, D_SSD), BF16),
        scratch_shapes=[pltpu.VMEM((SUBLANES + t, D_SSD), F32),
                        pltpu.VMEM((SUBLANES + t, 2 * D_BC), F32),
                        pltpu.VMEM((N_GROUPS, D_STATE, GROUP_W), F32),
                        pltpu.VMEM((t, D_SSD), BF16),
                        pltpu.VMEM((t, 2 * D_BC), BF16),
                        pltpu.VMEM((t, D_SSD), F32)],
        compiler_params=_cparams(),
        name="ssd",
    )(proj, proj, proj, dt_raw, conv_w, conv_w, conv_b, conv_b, dt_bias, a_log, d_skip, norm_g)


PAD_ID = 0xFFFFFFFF
TOK_ROW = PACK_TILES
MASK_ROWS = (PACK_TILES + 1, PACK_TILES + 2)


def _tile_rows(ref, first_row):
    return ref.at[pl.ds(pl.multiple_of(first_row * SUBLANES, SUBLANES), SUBLANES)]


def _mix_kernel(xp_ref, gs_ref, gp_ref, yn_ref, x_ref, poolw_ref, pscale_ref, wbp_ref, wbs_ref, wout_ref,
                g1_ref, ng_ref, sh2_ref, sc2_ref, g2_ref, wr_ref, rb_ref, wsgu_ref, wsd_ref, zero_ref,
                base_ref, ws_ref, cnt_ref, xs_ref, ptail, run, stage, dvm, dsm, sem, sem_s, *, n_tiles, cap):
    t = x_ref.shape[0]
    i = pl.program_id(0)
    slot = lax.rem(i, 2)

    @pl.when(i == 0)
    def _():
        ptail[...] = jnp.zeros_like(ptail)
        run[...] = jnp.zeros_like(run)
        stage[...] = jnp.zeros_like(stage)

    def issue_rows(s, group, after=None):
        zero = 0
        if after is not None:
            probe = jnp.max(jnp.abs(after[0:SUBLANES, 0:min(LANES, after.shape[1])])).astype(jnp.int32)
            zero = probe * zero_ref[0]
        per = t // MIX_ISSUE_GROUPS
        for tt in range(group * per, (group + 1) * per):
            for k in range(TOP_K):
                pltpu.make_async_copy(stage.at[s, pl.ds(tt * SUBLANES, SUBLANES)],
                                      _tile_rows(xs_ref, dsm[s, k, tt] + zero), sem).start()

    def drain_rows(s):
        for _ in range(TOP_K):
            pltpu.make_async_copy(stage.at[s], xs_ref.at[pl.ds(0, t * SUBLANES)], sem).wait()

    def tile(s, milestone=None):
        _mix_tile_body(xp_ref, gs_ref, gp_ref, yn_ref, x_ref, poolw_ref, pscale_ref, wbp_ref, wbs_ref, wout_ref,
                       g1_ref, ng_ref, sh2_ref, sc2_ref, g2_ref, wr_ref, rb_ref, wsgu_ref, wsd_ref,
                       base_ref, ws_ref, cnt_ref, ptail, run, stage, dvm, dsm, sem_s, i, s, cap, milestone)

    @pl.when(i == 0)
    def _():
        tile(0)

    @pl.when((i >= 1) & (i < n_tiles))
    def _():
        issue_rows(1 - slot, 0)
        tile(slot, lambda group, value: issue_rows(1 - slot, group, value))
        drain_rows(1 - slot)

    @pl.when(i == n_tiles)
    def _():
        for group in range(MIX_ISSUE_GROUPS):
            issue_rows(1 - slot, group)
        drain_rows(1 - slot)


def _mix_tile_body(xp_ref, gs_ref, gp_ref, yn_ref, x_ref, poolw_ref, pscale_ref, wbp_ref, wbs_ref, wout_ref,
                   g1_ref, ng_ref, sh2_ref, sc2_ref, g2_ref, wr_ref, rb_ref, wsgu_ref, wsd_ref,
                   base_ref, ws_ref, cnt_ref, ptail, run, stage, dvm, dsm, sem_s, i, slot, cap, milestone):
    t = x_ref.shape[0]
    if milestone is None:
        milestone = lambda group, value: None

    xp = xp_ref[...].astype(F32)
    ext = jnp.concatenate([ptail[...], xp], axis=0)
    ptail[...] = xp[t - 2 * SUBLANES:t]
    pos = (lax.broadcasted_iota(jnp.int32, (t, 1), 0) + (i * t + 1)).astype(F32)
    pooled = []
    for gi, w in enumerate(POOL_WINDOWS):
        sl = slice(gi * POOL_GDIM, (gi + 1) * POOL_GDIM)
        e = ext[:, sl]
        s = e
        span = 1
        while span < w:
            s = s + pltpu.roll(s, span, axis=0)
            span *= 2
        win = s[2 * SUBLANES:]
        mean = win / jnp.minimum(pos, float(w))
        pg = (mean - xp[:, sl]).astype(BF16)
        mixed = jnp.dot(pg, poolw_ref[gi], preferred_element_type=F32)
        pooled.append(mixed * pscale_ref[:, sl])
    pooled = jnp.concatenate(pooled, axis=1).astype(BF16)
    y_pool = jnp.dot(pooled, wbp_ref[...], preferred_element_type=F32)
    milestone(1, y_pool)
    y_ssd = jnp.dot(yn_ref[...], wbs_ref[...], preferred_element_type=F32)
    milestone(2, y_ssd)
    g_ssd = jax.nn.sigmoid(gs_ref[...].astype(F32))
    g_pool = jax.nn.sigmoid(gp_ref[...].astype(F32))
    mixed = (g_ssd * y_ssd + g_pool * y_pool).astype(BF16)
    h = x_ref[...] + g1_ref[...] * jnp.dot(mixed, wout_ref[...], preferred_element_type=F32)
    milestone(3, h)

    inv = lax.rsqrt(jnp.mean(h * h, axis=-1, keepdims=True) + EPS)
    u2 = h * inv * ng_ref[...]
    u2 = u2 * (1.0 + sc2_ref[...]) + sh2_ref[...]
    u2b = u2.astype(BF16)
    u2r = u2b.astype(F32)
    packed = []
    for j in range(PACK_TILES):
        lo = lax.bitcast_convert_type(u2r[:, j * LANES:(j + 1) * LANES], jnp.uint32)
        hi = lax.bitcast_convert_type(u2r[:, (j + PACK_TILES) * LANES:(j + PACK_TILES + 1) * LANES], jnp.uint32)
        packed.append((lo >> 16) | (hi & jnp.uint32(0xFFFF0000)))

    hs = jnp.dot(u2b, wsgu_ref[...], preferred_element_type=F32)
    milestone(4, hs)
    act = (_silu(hs[:, :D_SHARED]) * hs[:, D_SHARED:]).astype(BF16)
    shared = jnp.dot(act, wsd_ref[...], preferred_element_type=F32)
    milestone(5, shared)
    base_ref[...] = h + g2_ref[...] * shared

    logits = jnp.dot(u2b, wr_ref[...], preferred_element_type=F32)
    scores = jax.nn.sigmoid(logits)
    choice = scores + rb_ref[...]
    lane = lax.broadcasted_iota(jnp.int32, (t, N_EXPERTS), 1)
    lane_f = lane.astype(F32)
    lane_grp = lane // EXPERTS_PER_GROUP
    neg = -jnp.inf
    gscore = []
    for g in range(N_EGROUPS):
        vg = jnp.where(lane_grp == g, choice, neg)
        m1 = jnp.max(vg, axis=-1, keepdims=True)
        i1 = jnp.min(jnp.where(vg == m1, lane_f, float(N_EXPERTS)), axis=-1, keepdims=True)
        m2 = jnp.max(jnp.where(lane_f == i1, neg, vg), axis=-1, keepdims=True)
        gscore.append(m1 + m2)
    milestone(6, gscore[-1])
    gmask = jnp.zeros((t, N_EXPERTS), jnp.bool_)
    for g in range(N_EGROUPS):
        rank = jnp.zeros((t, 1), F32)
        for g2 in range(N_EGROUPS):
            if g2 == g:
                continue
            better = (gscore[g2] > gscore[g]) | ((gscore[g2] == gscore[g]) & (g2 < g))
            rank = rank + better.astype(F32)
        gmask = gmask | ((rank < float(TOPK_GROUPS)) & (lane_grp == g))
    work = jnp.where(gmask, choice, neg)
    sel = jnp.zeros((t, N_EXPERTS), jnp.bool_)
    onehots, idxs, sks = [], [], []
    for k in range(TOP_K):
        m = jnp.max(work, axis=-1, keepdims=True)
        idx = jnp.min(jnp.where(work == m, lane_f, float(N_EXPERTS)), axis=-1, keepdims=True)
        oh = lane_f == idx
        onehots.append(oh)
        idxs.append(idx)
        sks.append(jnp.sum(jnp.where(oh, scores, 0.0), axis=-1, keepdims=True))
        sel = sel | oh
        work = jnp.where(oh, neg, work)
    denom = sks[0]
    for k in range(1, TOP_K):
        denom = denom + sks[k]
    milestone(7, denom)

    ri = lax.broadcasted_iota(jnp.int32, (t, t), 0)
    ci = lax.broadcasted_iota(jnp.int32, (t, t), 1)
    before = (ri > ci).astype(BF16)
    sel_f = jnp.where(sel, 1.0, 0.0)
    sel_b = sel_f.astype(BF16)
    pos_tile = jnp.dot(before, sel_b, preferred_element_type=F32) + run[...]
    er = lax.broadcasted_iota(jnp.int32, (N_EXPERTS, N_EXPERTS), 0)
    ec = lax.broadcasted_iota(jnp.int32, (N_EXPERTS, N_EXPERTS), 1)
    rank_tile = jnp.dot(sel_b, (er < ec).astype(BF16), preferred_element_type=F32)
    lane128 = lax.broadcasted_iota(jnp.int32, (t, LANES), 1)
    k_iota = lax.broadcasted_iota(jnp.int32, (t, TOP_K), 1).astype(F32)
    dest = jnp.zeros((t, LANES), F32)
    w_sorted = jnp.zeros((t, TOP_K), F32)
    for k in range(TOP_K):
        pk = jnp.sum(jnp.where(onehots[k], pos_tile, 0.0), axis=-1, keepdims=True)
        rk = jnp.sum(jnp.where(onehots[k], rank_tile, 0.0), axis=-1, keepdims=True)
        dest = jnp.where(lane128 == k, idxs[k] * float(cap) + pk, dest)
        w_sorted = jnp.where(k_iota == rk, sks[k] / denom * ROUTED_SCALE, w_sorted)
    ws_ref[...] = w_sorted
    total = run[...] + jnp.sum(sel_f, axis=0, keepdims=True)
    run[...] = total
    cnt_ref[...] = total

    for j in range(PACK_TILES):
        stage[slot, pl.ds(j, t, stride=SUBLANES), :] = packed[j]
    tok = lax.broadcasted_iota(jnp.int32, (t, LANES), 0) + i * t
    stage[slot, pl.ds(TOK_ROW, t, stride=SUBLANES), :] = tok.astype(jnp.uint32)
    bit = jnp.where(sel, jnp.left_shift(1, lane & 15), 0).astype(F32)
    words = []
    for q in range(N_EXPERTS // 16):
        part = jnp.sum(jnp.where((lane >> 4) == q, bit, 0.0), axis=-1, keepdims=True)
        words.append(part.astype(jnp.int32).astype(jnp.uint32))
    for w, row in enumerate(MASK_ROWS):
        word = words[2 * w] | (words[2 * w + 1] << 16)
        stage[slot, pl.ds(row, t, stride=SUBLANES), :] = jnp.broadcast_to(word, (t, LANES))

    dvm[...] = dest.T[0:TOP_K, :].astype(jnp.int32)
    smem_copy = pltpu.make_async_copy(dvm, dsm.at[slot], sem_s)
    smem_copy.start()
    smem_copy.wait()


def _mix(proj, yn, x2, pool_w, pool_scale, w_br_pool, w_br_ssd, w_out, g1, ng, sh2, sc2, g2,
         w_router, router_bias, ws_gu, ws_down, cap):
    n = x2.shape[0]
    t = 256
    n_tiles = n // t
    xp_blk = (D_SSD + D_SSD + 2 * D_BC) // D_MODEL
    tile = lambda i: jnp.minimum(i, n_tiles - 1)
    row = lambda i: (tile(i), 0)
    return pl.pallas_call(
        functools.partial(_mix_kernel, n_tiles=n_tiles, cap=cap),
        grid=(n_tiles + 1,),
        in_specs=[pl.BlockSpec((t, D_MODEL), lambda i: (tile(i), xp_blk)),
                  pl.BlockSpec((t, D_MODEL), lambda i: (tile(i), xp_blk + 1)),
                  pl.BlockSpec((t, D_MODEL), lambda i: (tile(i), xp_blk + 2)),
                  pl.BlockSpec((t, D_SSD), row),
                  pl.BlockSpec((t, D_MODEL), row),
                  _full((len(POOL_WINDOWS), POOL_GDIM, POOL_GDIM)),
                  _full((1, D_MODEL)),
                  _full((D_MODEL, D_MODEL)),
                  _full((D_SSD, D_MODEL)),
                  _full((D_MODEL, D_MODEL)),
                  _full((1, D_MODEL)), _full((1, D_MODEL)), _full((1, D_MODEL)), _full((1, D_MODEL)),
                  _full((1, D_MODEL)),
                  _full((D_MODEL, N_EXPERTS)),
                  _full((1, N_EXPERTS)),
                  _full((D_MODEL, 2 * D_SHARED)),
                  _full((D_SHARED, D_MODEL)),
                  pl.BlockSpec(memory_space=pltpu.SMEM)],
        out_specs=[pl.BlockSpec((t, D_MODEL), row),
                   pl.BlockSpec((t, TOP_K), row),
                   _full((1, N_EXPERTS)),
                   pl.BlockSpec(memory_space=pl.ANY)],
        out_shape=[jax.ShapeDtypeStruct((n, D_MODEL), F32),
                   jax.ShapeDtypeStruct((n, TOP_K), F32),
                   jax.ShapeDtypeStruct((1, N_EXPERTS), F32),
                   jax.ShapeDtypeStruct((N_EXPERTS * cap * SUBLANES, LANES), jnp.uint32)],
        scratch_shapes=[pltpu.VMEM((2 * SUBLANES, D_MODEL), F32),
                        pltpu.VMEM((1, N_EXPERTS), F32),
                        pltpu.VMEM((2, t * SUBLANES, LANES), jnp.uint32),
                        pltpu.VMEM((TOP_K, t), jnp.int32),
                        pltpu.SMEM((2, TOP_K, t), jnp.int32),
                        pltpu.SemaphoreType.DMA(()),
                        pltpu.SemaphoreType.DMA(())],
        compiler_params=_cparams(),
        name="mix",
    )(proj, proj, proj, yn, x2, pool_w, pool_scale, w_br_pool, w_br_ssd, w_out, g1, ng, sh2, sc2, g2,
      w_router, router_bias, ws_gu, ws_down, jnp.zeros((1,), jnp.int32))


def _padfill_kernel(cnt_ref, xs_in_ref, xs_ref, padbuf, sem, *, cap):
    del xs_in_ref
    sub = lax.broadcasted_iota(jnp.int32, padbuf.shape, 0) & (SUBLANES - 1)
    padbuf[...] = jnp.where(sub == TOK_ROW, jnp.uint32(PAD_ID), jnp.uint32(0))
    sizes = [1 << bit for bit in range(EXPERT_BLOCK.bit_length() - 1)]

    def pad_copies(e, fn):
        c = cnt_ref[e]
        n_pad = (EXPERT_BLOCK - (c & (EXPERT_BLOCK - 1))) & (EXPERT_BLOCK - 1)
        first = e * cap + c
        for size in sizes:
            @pl.when((n_pad & size) != 0)
            def _():
                start = first + (n_pad & (size - 1))
                dst = xs_ref.at[pl.ds(pl.multiple_of(start * SUBLANES, SUBLANES), size * SUBLANES)]
                fn(pltpu.make_async_copy(padbuf.at[pl.ds(0, size * SUBLANES)], dst, sem))

    def issue(e, carry):
        pad_copies(e, lambda cp: cp.start())
        return carry

    lax.fori_loop(0, N_EXPERTS, issue, 0)

    def drain(e, carry):
        pad_copies(e, lambda cp: cp.wait())
        return carry

    lax.fori_loop(0, N_EXPERTS, drain, 0)


def _padfill(counts, xs, cap):
    return pl.pallas_call(
        functools.partial(_padfill_kernel, cap=cap),
        grid_spec=pltpu.PrefetchScalarGridSpec(
            num_scalar_prefetch=1,
            grid=(1,),
            in_specs=[pl.BlockSpec(memory_space=pl.ANY)],
            out_specs=pl.BlockSpec(memory_space=pl.ANY),
            scratch_shapes=[pltpu.VMEM((EXPERT_BLOCK // 2 * SUBLANES, LANES), jnp.uint32),
                            pltpu.SemaphoreType.DMA(())]),
        out_shape=jax.ShapeDtypeStruct(xs.shape, xs.dtype),
        input_output_aliases={1: 0},
        compiler_params=_cparams(),
        name="padfill",
    )(counts, xs)


def _expert_kernel(be_ref, br_ref, nb_ref, x_ref, wg_ref, wu_ref, wd_ref, ytok_ref, ybuf0, ybuf1, idv, ids0, ids1,
                   wgb, wub, wdb, sem, sem_ids, *, n_ids):
    del br_ref
    m = EXPERT_BLOCK
    b = pl.program_id(0)
    nb = nb_ref[0]
    ybufs, idss = (ybuf0, ybuf1), (ids0, ids1)
    n_phases = m // LANES
    half = D_MODEL // 2

    @pl.when((b == 0) | (be_ref[b] != be_ref[jnp.maximum(b - 1, 0)]))
    def _():
        wgb[...] = wg_ref[0].astype(BF16)
        wub[...] = wu_ref[0].astype(BF16)
        wdb[...] = wd_ref[0].astype(BF16)

    def issue(s, a0):
        for col in range(LANES):
            pltpu.make_async_copy(_tile_rows(ybufs[s], a0 * LANES + col),
                                  _tile_rows(ytok_ref, idss[s][a0, col]), sem.at[s]).start()

    def step(s, prev):
        tokrep = x_ref[pl.ds(TOK_ROW, m, stride=SUBLANES), :]
        e = be_ref[b]
        below_lo = jnp.where(e >= 32, -1, (1 << jnp.minimum(e, 31)) - 1).astype(jnp.uint32)
        below_hi = jnp.where(e >= 32, (1 << jnp.maximum(e - 32, 0)) - 1, 0).astype(jnp.uint32)
        rank = (lax.population_count(x_ref[pl.ds(MASK_ROWS[0], m, stride=SUBLANES), :] & below_lo)
                + lax.population_count(x_ref[pl.ds(MASK_ROWS[1], m, stride=SUBLANES), :] & below_hi))
        r = lax.broadcasted_iota(jnp.int32, (m, LANES), 0)
        ln = lax.broadcasted_iota(jnp.int32, (m, LANES), 1)
        n_tok = n_ids // TOP_K
        idi = jnp.where(tokrep == jnp.uint32(PAD_ID), n_ids + r,
                        rank.astype(jnp.int32) * n_tok + tokrep.astype(jnp.int32))
        diag = jnp.where((r & (LANES - 1)) == ln, idi, 0).astype(F32)
        idv[...] = jnp.sum(diag.reshape(m // LANES, LANES, LANES), axis=1).astype(jnp.int32)
        ids_copy = pltpu.make_async_copy(idv, idss[s], sem_ids)
        ids_copy.start()

        lo, hi = [], []
        for j in range(PACK_TILES):
            w = x_ref[pl.ds(j, m, stride=SUBLANES), :]
            lo.append(lax.bitcast_convert_type(w << 16, F32).astype(BF16))
            hi.append(lax.bitcast_convert_type(w & jnp.uint32(0xFFFF0000), F32).astype(BF16))
        x = jnp.concatenate(lo + hi, axis=1)
        phases = iter(range(n_phases))
        if prev is not None:
            issue(prev, next(phases))
        gate = jnp.dot(x, wgb[...], preferred_element_type=F32)
        if prev is not None:
            issue(prev, next(phases))
        up = jnp.dot(x, wub[...], preferred_element_type=F32)
        hid = (_silu(gate) * up).astype(BF16)
        for c0 in (0, half):
            if prev is not None:
                issue(prev, next(phases))
            y = jnp.dot(hid, wdb[:, c0:c0 + half], preferred_element_type=F32)
            for j in range(half // LANES):
                ybufs[s][pl.ds(c0 // LANES + j, m, stride=SUBLANES), :] = y[:, j * LANES:(j + 1) * LANES]
        ids_copy.wait()
        if prev is not None:
            for a0 in phases:
                issue(prev, a0)
            pltpu.make_async_copy(ybufs[prev], ytok_ref.at[pl.ds(0, m * SUBLANES)], sem.at[prev]).wait()

    @pl.when(b == 0)
    def _():
        step(0, None)

    for s in range(2):
        @pl.when((b >= 1) & (b <= nb) & (lax.rem(b, 2) == s))
        def _():
            step(s, 1 - s)


def _experts(block_e, block_row, nb_used, xs, we_gate, we_up, we_down, n_ids):
    n_blocks = block_e.shape[0]
    m = EXPERT_BLOCK
    rows = m * SUBLANES
    return pl.pallas_call(
        functools.partial(_expert_kernel, n_ids=n_ids),
        grid_spec=pltpu.PrefetchScalarGridSpec(
            num_scalar_prefetch=3,
            grid=(n_blocks,),
            in_specs=[pl.BlockSpec((rows, LANES), lambda b, be, br, nb: (br[b], 0)),
                      pl.BlockSpec((1, D_MODEL, D_EXPERT), lambda b, be, br, nb: (be[b], 0, 0)),
                      pl.BlockSpec((1, D_MODEL, D_EXPERT), lambda b, be, br, nb: (be[b], 0, 0)),
                      pl.BlockSpec((1, D_EXPERT, D_MODEL), lambda b, be, br, nb: (be[b], 0, 0))],
            out_specs=pl.BlockSpec(memory_space=pl.ANY),
            scratch_shapes=[pltpu.VMEM((rows, LANES), F32),
                            pltpu.VMEM((rows, LANES), F32),
                            pltpu.VMEM((m // LANES, LANES), jnp.int32),
                            pltpu.SMEM((m // LANES, LANES), jnp.int32),
                            pltpu.SMEM((m // LANES, LANES), jnp.int32),
                            pltpu.VMEM((D_MODEL, D_EXPERT), BF16),
                            pltpu.VMEM((D_MODEL, D_EXPERT), BF16),
                            pltpu.VMEM((D_EXPERT, D_MODEL), BF16),
                            pltpu.SemaphoreType.DMA((2,)),
                            pltpu.SemaphoreType.DMA(())]),
        out_shape=jax.ShapeDtypeStruct(((n_ids + m) * SUBLANES, LANES), F32),
        compiler_params=_cparams(),
        name="experts",
    )(block_e, block_row, nb_used, xs, we_gate, we_up, we_down)


def _combine_kernel(*refs):
    y_refs = refs[:TOP_K]
    w8_ref, base_ref, g2_ref, fg_ref, o_ref = refs[TOP_K:]
    t7 = base_ref.shape[0]
    w8 = w8_ref[...]
    parts = []
    for j in range(ROW_TILES):
        acc = jnp.zeros((t7, LANES), F32)
        for k in range(TOP_K):
            acc = acc + w8[:, k:k + 1] * y_refs[k][pl.ds(j, t7, stride=SUBLANES), :]
        parts.append(acc)
    routed = jnp.concatenate(parts, axis=1)
    h = base_ref[...] + g2_ref[...] * routed
    inv = lax.rsqrt(jnp.mean(h * h, axis=-1, keepdims=True) + EPS)
    o_ref[...] = h * inv * fg_ref[...]


def _combine(ytok, w8, base, g2, fg):
    n = base.shape[0]
    t7 = 128
    y_specs = [pl.BlockSpec((t7 * SUBLANES, LANES), functools.partial(lambda i, k: (k * (n // t7) + i, 0), k=k))
               for k in range(TOP_K)]
    return pl.pallas_call(
        _combine_kernel,
        grid=(n // t7,),
        in_specs=y_specs + [
                  pl.BlockSpec((t7, TOP_K), lambda i: (i, 0)),
                  pl.BlockSpec((t7, D_MODEL), lambda i: (i, 0)),
                  _full((1, D_MODEL)), _full((1, D_MODEL))],
        out_specs=pl.BlockSpec((t7, D_MODEL), lambda i: (i, 0)),
        out_shape=jax.ShapeDtypeStruct((n, D_MODEL), F32),
        compiler_params=_cparams(),
        name="combine",
    )(*([ytok] * TOP_K), w8, base, g2, fg)


def kernel(x, c, w_ada, b_ada, norm_mix_g, w_in, conv_w, conv_b, dt_bias, A_log, D_skip, ssd_norm_g, pool_w,
           pool_scale, w_br_ssd, w_br_pool, w_out, norm_ffn_g, w_router, router_bias, we_gate, we_up, we_down,
           ws_gate, ws_up, ws_down, final_norm_g):
    bsz, seq, _ = x.shape
    assert bsz == 1 and w_ada.shape[0] == 1
    n = seq
    h2 = x.reshape(n, D_MODEL)

    mod = _ada(c, w_ada[0], b_ada[0])
    sh1, sc1, g1, sh2, sc2, g2 = [mod[:, k * D_MODEL:(k + 1) * D_MODEL] for k in range(6)]

    wi = w_in[0]
    o_dt = D_SSD + D_SSD + 2 * D_BC
    w_dt = jnp.pad(wi[:, o_dt:o_dt + N_HEADS], ((0, 0), (0, LANES - N_HEADS))).astype(BF16)
    proj, dt_raw = _in_proj(h2, norm_mix_g[0][None], sh1, sc1, wi[:, :o_dt].astype(BF16),
                            wi[:, o_dt + N_HEADS:].astype(BF16), w_dt)

    pad_h = lambda v: jnp.pad(v[None], ((0, 0), (0, LANES - N_HEADS)))
    yn = _ssd(proj, dt_raw, conv_w[0], conv_b[0][None], pad_h(dt_bias[0]), pad_h(A_log[0]),
              jnp.repeat(D_skip[0], HEADDIM)[None], ssd_norm_g[0][None])

    ws_gu = jnp.concatenate([ws_gate[0], ws_up[0]], axis=1).astype(BF16)
    m = EXPERT_BLOCK
    cap = -(-n // m) * m
    base, w8, cnt, xs = _mix(
        proj, yn, h2, pool_w[0].astype(BF16), pool_scale[0][None], w_br_pool[0].astype(BF16),
        w_br_ssd[0].astype(BF16), w_out[0].astype(BF16), g1, norm_ffn_g[0][None], sh2, sc2, g2,
        w_router[0].astype(BF16), router_bias[0][None], ws_gu, ws_down[0].astype(BF16), cap)

    counts = cnt[0].astype(jnp.int32)
    nblk = (counts + m - 1) // m
    bends = jnp.cumsum(nblk)
    bstarts = bends - nblk
    nb_used = bends[-1]
    n_blocks = -(-(n * TOP_K) // m) + N_EXPERTS + 1
    b_eff = jnp.minimum(jnp.arange(n_blocks, dtype=jnp.int32), nb_used - 1)
    done = bends[None, :] <= b_eff[:, None]
    block_e = jnp.sum(done.astype(jnp.int32), axis=1)
    block_row = block_e * (cap // m) + b_eff - jnp.sum(jnp.where(done, nblk[None, :], 0), axis=1)

    xs = _padfill(counts, xs, cap)
    ytok = _experts(block_e, block_row, nb_used.reshape(1), xs,
                    we_gate[0], we_up[0], we_down[0], n * TOP_K)
    out = _combine(ytok, w8, base, g2, final_norm_g[None])
    return out.reshape(bsz, seq, D_MODEL)
```

```python
import functools

import jax
import jax.numpy as jnp
from jax import lax
from jax.experimental import pallas as pl
from jax.experimental.pallas import tpu as pltpu

F32 = jnp.float32
BF16 = jnp.bfloat16
HIGHEST = lax.Precision.HIGHEST

D_MODEL = 1024
D_SSD = 2048
HEADDIM = 64
N_HEADS = 32
N_GROUPS = 8
HEADS_PER_GROUP = N_HEADS // N_GROUPS
D_STATE = 128
CONV_K = 4
CHUNK = 128
GROUP_W = D_SSD // N_GROUPS
D_BC = N_GROUPS * D_STATE
POOL_WINDOWS = (2, 4, 8, 16)
POOL_GDIM = 256
N_EXPERTS = 64
TOP_K = 8
N_EGROUPS = 8
EXPERTS_PER_GROUP = 8
TOPK_GROUPS = 4
D_EXPERT = 256
D_SHARED = 256
ROUTED_SCALE = 2.5
MOE_BLOCK = 128
EPS = 1e-6

LANES = 128
SUBLANES = 8
ROW_TILES = D_MODEL // LANES
PACK_TILES = ROW_TILES // 2
PACK_W = PACK_TILES * LANES
EXPERT_BLOCK = 512
MIX_ISSUE_GROUPS = 8

PROJ_W = D_SSD + D_SSD + 2 * D_BC + 3 * D_MODEL
PROJ_CHUNK = 512

VMEM_LIMIT = 56 * 1024 * 1024


def _cparams(sem=("arbitrary",)):
    return pltpu.CompilerParams(dimension_semantics=sem, vmem_limit_bytes=VMEM_LIMIT)


def _full(shape):
    nd = len(shape)
    return pl.BlockSpec(shape, lambda *_: (0,) * nd)


def _silu(v):
    return v * jax.nn.sigmoid(v)


def _ada_kernel(c_ref, w_ref, b_ref, o_ref):
    c = c_ref[...]
    o_ref[...] = jnp.dot(_silu(c), w_ref[...], preferred_element_type=F32, precision=HIGHEST) + b_ref[...]


def _ada(c, w_ada, b_ada):
    n_out = w_ada.shape[1]
    tn = 1536
    c8 = jnp.broadcast_to(c, (SUBLANES, D_MODEL))
    out = pl.pallas_call(
        _ada_kernel,
        grid=(n_out // tn,),
        in_specs=[_full((SUBLANES, D_MODEL)),
                  pl.BlockSpec((D_MODEL, tn), lambda j: (0, j)),
                  pl.BlockSpec((1, tn), lambda j: (0, j))],
        out_specs=pl.BlockSpec((SUBLANES, tn), lambda j: (0, j)),
        out_shape=jax.ShapeDtypeStruct((SUBLANES, n_out), F32),
        compiler_params=_cparams(),
        name="ada",
    )(c8, w_ada, b_ada.reshape(1, n_out))
    return out[0:1]


def _inproj_kernel(x_ref, g_ref, sh_ref, sc_ref, wa_ref, wb_ref, wdt_ref, proj_ref, dt_ref):
    x = x_ref[...]
    inv = lax.rsqrt(jnp.mean(x * x, axis=-1, keepdims=True) + EPS)
    u = x * inv * g_ref[...]
    u = u * (1.0 + sc_ref[...]) + sh_ref[...]
    ub = u.astype(BF16)
    wa = wa_ref.shape[1]
    for c0 in range(0, PROJ_W, PROJ_CHUNK):
        w = wa_ref[:, c0:c0 + PROJ_CHUNK] if c0 < wa else wb_ref[:, c0 - wa:c0 - wa + PROJ_CHUNK]
        proj_ref[:, c0:c0 + PROJ_CHUNK] = jnp.dot(ub, w, preferred_element_type=F32).astype(BF16)
    dt_ref[...] = jnp.dot(ub, wdt_ref[...], preferred_element_type=F32)


def _in_proj(x2, g, sh, sc, w_a, w_b, w_dt):
    n = x2.shape[0]
    tm = 256
    return pl.pallas_call(
        _inproj_kernel,
        grid=(n // tm,),
        in_specs=[pl.BlockSpec((tm, D_MODEL), lambda i: (i, 0)),
                  _full((1, D_MODEL)), _full((1, D_MODEL)), _full((1, D_MODEL)),
                  pl.BlockSpec(w_a.shape, lambda i: (0, 0), pipeline_mode=pl.Buffered(1)),
                  pl.BlockSpec(w_b.shape, lambda i: (0, 0), pipeline_mode=pl.Buffered(1)),
                  _full((D_MODEL, LANES))],
        out_specs=[pl.BlockSpec((tm, PROJ_W), lambda i: (i, 0)),
                   pl.BlockSpec((tm, LANES), lambda i: (i, 0))],
        out_shape=[jax.ShapeDtypeStruct((n, PROJ_W), BF16),
                   jax.ShapeDtypeStruct((n, LANES), F32)],
        compiler_params=_cparams(),
        name="in_proj",
    )(x2, g, sh, sc, w_a, w_b, w_dt)


def _conv_silu(cur_ref, ext_ref, w_ref, b_ref, out_ref, width, cw=512):
    t = cur_ref.shape[0]
    for c0 in range(0, width, cw):
        sl = slice(c0, c0 + cw)
        cur = cur_ref[:, sl].astype(F32)
        ext_ref[SUBLANES:, sl] = cur
        acc = cur * w_ref[CONV_K - 1:CONV_K, sl] + b_ref[:, sl]
        for s in range(1, CONV_K):
            acc = acc + ext_ref[pl.ds(SUBLANES - s, t), sl] * w_ref[CONV_K - 1 - s:CONV_K - s, sl]
        out_ref[:, sl] = _silu(acc).astype(out_ref.dtype)
        ext_ref[0:SUBLANES, sl] = cur[t - SUBLANES:t]


def _ssd_kernel(z_ref, xs_ref, bc_ref, dt_ref, cwx_ref, cwbc_ref, cbx_ref, cbbc_ref, dtb_ref, alog_ref,
                dskip_ref, ng_ref, o_ref, tailx, tailbc, state, xc, bcc, ybuf):
    q = CHUNK
    pair_w = 2 * HEADDIM

    @pl.when(pl.program_id(0) == 0)
    def _():
        tailx[...] = jnp.zeros_like(tailx)
        tailbc[...] = jnp.zeros_like(tailbc)
        state[...] = jnp.zeros_like(state)

    _conv_silu(xs_ref, tailx, cwx_ref, cbx_ref, xc, D_SSD)
    _conv_silu(bc_ref, tailbc, cwbc_ref, cbbc_ref, bcc, 2 * D_BC)

    v = dt_ref[...] + dtb_ref[...]
    dt = jnp.maximum(v, 0.0) + jnp.log(1.0 + jnp.exp(-jnp.abs(v)))
    a = dt * (-jnp.exp(alog_ref[...]))
    ri = lax.broadcasted_iota(jnp.int32, (q, q), 0)
    ci = lax.broadcasted_iota(jnp.int32, (q, q), 1)
    causal = ri >= ci
    a_cs = jnp.dot(causal.astype(F32), a, preferred_element_type=F32, precision=HIGHEST)
    a_cs_t = a_cs.T
    dt_t = dt.T
    first_half = ci < HEADDIM
    first_half_row = first_half[0:1, :]

    for g in range(N_GROUPS):
        b_g = bcc[:, g * D_STATE:(g + 1) * D_STATE]
        c_g = bcc[:, D_BC + g * D_STATE:D_BC + (g + 1) * D_STATE]
        cb = lax.dot_general(c_g, b_g, (((1,), (1,)), ((), ())), preferred_element_type=F32)
        b_t = b_g.astype(F32).T
        st_g = state[g]
        y_off = jnp.dot(c_g, st_g.astype(BF16), preferred_element_type=F32)
        for pi in range(HEADS_PER_GROUP // 2):
            h0 = g * HEADS_PER_GROUP + 2 * pi
            lanes = slice(h0 * HEADDIM, h0 * HEADDIM + pair_w)
            gl = slice(pi * pair_w, (pi + 1) * pair_w)
            ms, ws, cols, lasts = [], [], [], []
            for h in (h0, h0 + 1):
                col = jnp.broadcast_to(a_cs[:, h:h + 1], (q, q))
                row = a_cs_t[h:h + 1, :]
                dtrow = dt_t[h:h + 1, :]
                decay = jnp.exp(jnp.where(causal, col - row, -jnp.inf))
                ms.append((cb * decay * dtrow).astype(BF16))
                last = a_cs_t[h:h + 1, q - 1:q]
                ws.append((b_t * (jnp.exp(last - row) * dtrow)).astype(BF16))
                cols.append(col)
                lasts.append(last)
            xp = xc[:, lanes]
            zero = jnp.zeros_like(xp)
            x_bd = jnp.concatenate([jnp.where(first_half, xp, zero), jnp.where(first_half, zero, xp)], axis=0)
            y_p = jnp.dot(jnp.concatenate(ms, axis=1), x_bd, preferred_element_type=F32)
            y_p = y_p + y_off[:, gl] * jnp.exp(jnp.where(first_half, cols[0], cols[1]))
            ybuf[:, lanes] = y_p
            s_new = jnp.dot(jnp.concatenate(ws, axis=1), x_bd, preferred_element_type=F32)
            carry = jnp.exp(jnp.where(first_half_row, lasts[0], lasts[1]))
            state[g, :, gl] = st_g[:, gl] * carry + s_new

    for g in range(N_GROUPS):
        sl = slice(g * GROUP_W, (g + 1) * GROUP_W)
        zg = z_ref[:, sl].astype(F32)
        yg = (ybuf[:, sl] + dskip_ref[:, sl] * xc[:, sl].astype(F32)) * _silu(zg)
        inv = lax.rsqrt(jnp.mean(yg * yg, axis=-1, keepdims=True) + EPS)
        o_ref[:, sl] = (yg * inv * ng_ref[:, sl]).astype(BF16)


def _ssd(proj, dt_raw, conv_w, conv_b, dt_bias, a_log, d_skip, norm_g):
    n = proj.shape[0]
    t = CHUNK
    return pl.pallas_call(
        _ssd_kernel,
        grid=(n // t,),
        in_specs=[pl.BlockSpec((t, D_SSD), lambda i: (i, 0)),
                  pl.BlockSpec((t, D_SSD), lambda i: (i, 1)),
                  pl.BlockSpec((t, 2 * D_BC), lambda i: (i, 2)),
                  pl.BlockSpec((t, LANES), lambda i: (i, 0)),
                  pl.BlockSpec((CONV_K, D_SSD), lambda i: (0, 0)),
                  pl.BlockSpec((CONV_K, 2 * D_BC), lambda i: (0, 1)),
                  pl.BlockSpec((1, D_SSD), lambda i: (0, 0)),
                  pl.BlockSpec((1, 2 * D_BC), lambda i: (0, 1)),
                  _full((1, LANES)), _full((1, LANES)),
                  _full((1, D_SSD)), _full((1, D_SSD))],
        out_specs=pl.BlockSpec((t, D_SSD), lambda i: (i, 0)),
        out_shape=jax.ShapeDtypeStruct((n, D_SSD), BF16),
        scratch_shapes=[pltpu.VMEM((SUBLANES + t, D_SSD), F32),
                        pltpu.VMEM((SUBLANES + t, 2 * D_BC), F32),
                        pltpu.VMEM((N_GROUPS, D_STATE, GROUP_W), F32),
                        pltpu.VMEM((t, D_SSD), BF16),
                        pltpu.VMEM((t, 2 * D_BC), BF16),
                        pltpu.VMEM((t, D_SSD), F32)],
        compiler_params=_cparams(),
        name="ssd",
    )(proj, proj, proj, dt_raw, conv_w, conv_w, conv_b, conv_b, dt_bias, a_log, d_skip, norm_g)


PAD_ID = 0xFFFFFFFF
TOK_ROW = PACK_TILES
MASK_ROWS = (PACK_TILES + 1, PACK_TILES + 2)


def _tile_rows(ref, first_row):
    return ref.at[pl.ds(pl.multiple_of(first_row * SUBLANES, SUBLANES), SUBLANES)]


def _mix_kernel(xp_ref, gs_ref, gp_ref, yn_ref, x_ref, poolw_ref, pscale_ref, wbp_ref, wbs_ref, wout_ref,
                g1_ref, ng_ref, sh2_ref, sc2_ref, g2_ref, wr_ref, rb_ref, wsgu_ref, wsd_ref, zero_ref,
                base_ref, ws_ref, cnt_ref, xs_ref, ptail, run, stage, dvm, dsm, sem, sem_s, *, n_tiles, cap):
    t = x_ref.shape[0]
    i = pl.program_id(0)
    slot = lax.rem(i, 2)

    @pl.when(i == 0)
    def _():
        ptail[...] = jnp.zeros_like(ptail)
        run[...] = jnp.zeros_like(run)
        stage[...] = jnp.zeros_like(stage)

    def issue_rows(s, group, after=None):
        zero = 0
        if after is not None:
            probe = jnp.max(jnp.abs(after[0:SUBLANES, 0:min(LANES, after.shape[1])])).astype(jnp.int32)
            zero = probe * zero_ref[0]
        per = t // MIX_ISSUE_GROUPS
        for tt in range(group * per, (group + 1) * per):
            for k in range(TOP_K):
                pltpu.make_async_copy(stage.at[s, pl.ds(tt * SUBLANES, SUBLANES)],
                                      _tile_rows(xs_ref, dsm[s, k, tt] + zero), sem).start()

    def drain_rows(s):
        for _ in range(TOP_K):
            pltpu.make_async_copy(stage.at[s], xs_ref.at[pl.ds(0, t * SUBLANES)], sem).wait()

    def tile(s, milestone=None):
        _mix_tile_body(xp_ref, gs_ref, gp_ref, yn_ref, x_ref, poolw_ref, pscale_ref, wbp_ref, wbs_ref, wout_ref,
                       g1_ref, ng_ref, sh2_ref, sc2_ref, g2_ref, wr_ref, rb_ref, wsgu_ref, wsd_ref,
                       base_ref, ws_ref, cnt_ref, ptail, run, stage, dvm, dsm, sem_s, i, s, cap, milestone)

    @pl.when(i == 0)
    def _():
        tile(0)

    @pl.when((i >= 1) & (i < n_tiles))
    def _():
        issue_rows(1 - slot, 0)
        tile(slot, lambda group, value: issue_rows(1 - slot, group, value))
        drain_rows(1 - slot)

    @pl.when(i == n_tiles)
    def _():
        for group in range(MIX_ISSUE_GROUPS):
            issue_rows(1 - slot, group)
        drain_rows(1 - slot)


def _mix_tile_body(xp_ref, gs_ref, gp_ref, yn_ref, x_ref, poolw_ref, pscale_ref, wbp_ref, wbs_ref, wout_ref,
                   g1_ref, ng_ref, sh2_ref, sc2_ref, g2_ref, wr_ref, rb_ref, wsgu_ref, wsd_ref,
                   base_ref, ws_ref, cnt_ref, ptail, run, stage, dvm, dsm, sem_s, i, slot, cap, milestone):
    t = x_ref.shape[0]
    if milestone is None:
        milestone = lambda group, value: None

    xp = xp_ref[...].astype(F32)
    ext = jnp.concatenate([ptail[...], xp], axis=0)
    ptail[...] = xp[t - 2 * SUBLANES:t]
    pos = (lax.broadcasted_iota(jnp.int32, (t, 1), 0) + (i * t + 1)).astype(F32)
    pooled = []
    for gi, w in enumerate(POOL_WINDOWS):
        sl = slice(gi * POOL_GDIM, (gi + 1) * POOL_GDIM)
        e = ext[:, sl]
        s = e
        span = 1
        while span < w:
            s = s + pltpu.roll(s, span, axis=0)
            span *= 2
        win = s[2 * SUBLANES:]
        mean = win / jnp.minimum(pos, float(w))
        pg = (mean - xp[:, sl]).astype(BF16)
        mixed = jnp.dot(pg, poolw_ref[gi], preferred_element_type=F32)
        pooled.append(mixed * pscale_ref[:, sl])
    pooled = jnp.concatenate(pooled, axis=1).astype(BF16)
    y_pool = jnp.dot(pooled, wbp_ref[...], preferred_element_type=F32)
    milestone(1, y_pool)
    y_ssd = jnp.dot(yn_ref[...], wbs_ref[...], preferred_element_type=F32)
    milestone(2, y_ssd)
    g_ssd = jax.nn.sigmoid(gs_ref[...].astype(F32))
    g_pool = jax.nn.sigmoid(gp_ref[...].astype(F32))
    mixed = (g_ssd * y_ssd + g_pool * y_pool).astype(BF16)
    h = x_ref[...] + g1_ref[...] * jnp.dot(mixed, wout_ref[...], preferred_element_type=F32)
    milestone(3, h)

    inv = lax.rsqrt(jnp.mean(h * h, axis=-1, keepdims=True) + EPS)
    u2 = h * inv * ng_ref[...]
    u2 = u2 * (1.0 + sc2_ref[...]) + sh2_ref[...]
    u2b = u2.astype(BF16)
    u2r = u2b.astype(F32)
    packed = []
    for j in range(PACK_TILES):
        lo = lax.bitcast_convert_type(u2r[:, j * LANES:(j + 1) * LANES], jnp.uint32)
        hi = lax.bitcast_convert_type(u2r[:, (j + PACK_TILES) * LANES:(j + PACK_TILES + 1) * LANES], jnp.uint32)
        packed.append((lo >> 16) | (hi & jnp.uint32(0xFFFF0000)))

    hs = jnp.dot(u2b, wsgu_ref[...], preferred_element_type=F32)
    milestone(4, hs)
    act = (_silu(hs[:, :D_SHARED]) * hs[:, D_SHARED:]).astype(BF16)
    shared = jnp.dot(act, wsd_ref[...], preferred_element_type=F32)
    milestone(5, shared)
    base_ref[...] = h + g2_ref[...] * shared

    logits = jnp.dot(u2b, wr_ref[...], preferred_element_type=F32)
    scores = jax.nn.sigmoid(logits)
    choice = scores + rb_ref[...]
    lane = lax.broadcasted_iota(jnp.int32, (t, N_EXPERTS), 1)
    lane_f = lane.astype(F32)
    lane_grp = lane // EXPERTS_PER_GROUP
    neg = -jnp.inf
    gscore = []
    for g in range(N_EGROUPS):
        vg = jnp.where(lane_grp == g, choice, neg)
        m1 = jnp.max(vg, axis=-1, keepdims=True)
        i1 = jnp.min(jnp.where(vg == m1, lane_f, float(N_EXPERTS)), axis=-1, keepdims=True)
        m2 = jnp.max(jnp.where(lane_f == i1, neg, vg), axis=-1, keepdims=True)
        gscore.append(m1 + m2)
    milestone(6, gscore[-1])
    gmask = jnp.zeros((t, N_EXPERTS), jnp.bool_)
    for g in range(N_EGROUPS):
        rank = jnp.zeros((t, 1), F32)
        for g2 in range(N_EGROUPS):
            if g2 == g:
                continue
            better = (gscore[g2] > gscore[g]) | ((gscore[g2] == gscore[g]) & (g2 < g))
            rank = rank + better.astype(F32)
        gmask = gmask | ((rank < float(TOPK_GROUPS)) & (lane_grp == g))
    work = jnp.where(gmask, choice, neg)
    sel = jnp.zeros((t, N_EXPERTS), jnp.bool_)
    onehots, idxs, sks = [], [], []
    for k in range(TOP_K):
        m = jnp.max(work, axis=-1, keepdims=True)
        idx = jnp.min(jnp.where(work == m, lane_f, float(N_EXPERTS)), axis=-1, keepdims=True)
        oh = lane_f == idx
        onehots.append(oh)
        idxs.append(idx)
        sks.append(jnp.sum(jnp.where(oh, scores, 0.0), axis=-1, keepdims=True))
        sel = sel | oh
        work = jnp.where(oh, neg, work)
    denom = sks[0]
    for k in range(1, TOP_K):
        denom = denom + sks[k]
    milestone(7, denom)

    ri = lax.broadcasted_iota(jnp.int32, (t, t), 0)
    ci = lax.broadcasted_iota(jnp.int32, (t, t), 1)
    before = (ri > ci).astype(BF16)
    sel_f = jnp.where(sel, 1.0, 0.0)
    sel_b = sel_f.astype(BF16)
    pos_tile = jnp.dot(before, sel_b, preferred_element_type=F32) + run[...]
    er = lax.broadcasted_iota(jnp.int32, (N_EXPERTS, N_EXPERTS), 0)
    ec = lax.broadcasted_iota(jnp.int32, (N_EXPERTS, N_EXPERTS), 1)
    rank_tile = jnp.dot(sel_b, (er < ec).astype(BF16), preferred_element_type=F32)
    lane128 = lax.broadcasted_iota(jnp.int32, (t, LANES), 1)
    k_iota = lax.broadcasted_iota(jnp.int32, (t, TOP_K), 1).astype(F32)
    dest = jnp.zeros((t, LANES), F32)
    w_sorted = jnp.zeros((t, TOP_K), F32)
    for k in range(TOP_K):
        pk = jnp.sum(jnp.where(onehots[k], pos_tile, 0.0), axis=-1, keepdims=True)
        rk = jnp.sum(jnp.where(onehots[k], rank_tile, 0.0), axis=-1, keepdims=True)
        dest = jnp.where(lane128 == k, idxs[k] * float(cap) + pk, dest)
        w_sorted = jnp.where(k_iota == rk, sks[k] / denom * ROUTED_SCALE, w_sorted)
    ws_ref[...] = w_sorted
    total = run[...] + jnp.sum(sel_f, axis=0, keepdims=True)
    run[...] = total
    cnt_ref[...] = total

    for j in range(PACK_TILES):
        stage[slot, pl.ds(j, t, stride=SUBLANES), :] = packed[j]
    tok = lax.broadcasted_iota(jnp.int32, (t, LANES), 0) + i * t
    stage[slot, pl.ds(TOK_ROW, t, stride=SUBLANES), :] = tok.astype(jnp.uint32)
    bit = jnp.where(sel, jnp.left_shift(1, lane & 15), 0).astype(F32)
    words = []
    for q in range(N_EXPERTS // 16):
        part = jnp.sum(jnp.where((lane >> 4) == q, bit, 0.0), axis=-1, keepdims=True)
        words.append(part.astype(jnp.int32).astype(jnp.uint32))
    for w, row in enumerate(MASK_ROWS):
        word = words[2 * w] | (words[2 * w + 1] << 16)
        stage[slot, pl.ds(row, t, stride=SUBLANES), :] = jnp.broadcast_to(word, (t, LANES))

    dvm[...] = dest.T[0:TOP_K, :].astype(jnp.int32)
    smem_copy = pltpu.make_async_copy(dvm, dsm.at[slot], sem_s)
    smem_copy.start()
    smem_copy.wait()


def _mix(proj, yn, x2, pool_w, pool_scale, w_br_pool, w_br_ssd, w_out, g1, ng, sh2, sc2, g2,
         w_router, router_bias, ws_gu, ws_down, cap):
    n = x2.shape[0]
    t = 256
    n_tiles = n // t
    xp_blk = (D_SSD + D_SSD + 2 * D_BC) // D_MODEL
    tile = lambda i: jnp.minimum(i, n_tiles - 1)
    row = lambda i: (tile(i), 0)
    return pl.pallas_call(
        functools.partial(_mix_kernel, n_tiles=n_tiles, cap=cap),
        grid=(n_tiles + 1,),
        in_specs=[pl.BlockSpec((t, D_MODEL), lambda i: (tile(i), xp_blk)),
                  pl.BlockSpec((t, D_MODEL), lambda i: (tile(i), xp_blk + 1)),
                  pl.BlockSpec((t, D_MODEL), lambda i: (tile(i), xp_blk + 2)),
                  pl.BlockSpec((t, D_SSD), row),
                  pl.BlockSpec((t, D_MODEL), row),
                  _full((len(POOL_WINDOWS), POOL_GDIM, POOL_GDIM)),
                  _full((1, D_MODEL)),
                  _full((D_MODEL, D_MODEL)),
                  _full((D_SSD, D_MODEL)),
                  _full((D_MODEL, D_MODEL)),
                  _full((1, D_MODEL)), _full((1, D_MODEL)), _full((1, D_MODEL)), _full((1, D_MODEL)),
                  _full((1, D_MODEL)),
                  _full((D_MODEL, N_EXPERTS)),
                  _full((1, N_EXPERTS)),
                  _full((D_MODEL, 2 * D_SHARED)),
                  _full((D_SHARED, D_MODEL)),
                  pl.BlockSpec(memory_space=pltpu.SMEM)],
        out_specs=[pl.BlockSpec((t, D_MODEL), row),
                   pl.BlockSpec((t, TOP_K), row),
                   _full((1, N_EXPERTS)),
                   pl.BlockSpec(memory_space=pl.ANY)],
        out_shape=[jax.ShapeDtypeStruct((n, D_MODEL), F32),
                   jax.ShapeDtypeStruct((n, TOP_K), F32),
                   jax.ShapeDtypeStruct((1, N_EXPERTS), F32),
                   jax.ShapeDtypeStruct((N_EXPERTS * cap * SUBLANES, LANES), jnp.uint32)],
        scratch_shapes=[pltpu.VMEM((2 * SUBLANES, D_MODEL), F32),
                        pltpu.VMEM((1, N_EXPERTS), F32),
                        pltpu.VMEM((2, t * SUBLANES, LANES), jnp.uint32),
                        pltpu.VMEM((TOP_K, t), jnp.int32),
                        pltpu.SMEM((2, TOP_K, t), jnp.int32),
                        pltpu.SemaphoreType.DMA(()),
                        pltpu.SemaphoreType.DMA(())],
        compiler_params=_cparams(),
        name="mix",
    )(proj, proj, proj, yn, x2, pool_w, pool_scale, w_br_pool, w_br_ssd, w_out, g1, ng, sh2, sc2, g2,
      w_router, router_bias, ws_gu, ws_down, jnp.zeros((1,), jnp.int32))


def _padfill_kernel(cnt_ref, xs_in_ref, xs_ref, padbuf, sem, *, cap):
    del xs_in_ref
    sub = lax.broadcasted_iota(jnp.int32, padbuf.shape, 0) & (SUBLANES - 1)
    padbuf[...] = jnp.where(sub == TOK_ROW, jnp.uint32(PAD_ID), jnp.uint32(0))
    sizes = [1 << bit for bit in range(EXPERT_BLOCK.bit_length() - 1)]

    def pad_copies(e, fn):
        c = cnt_ref[e]
        n_pad = (EXPERT_BLOCK - (c & (EXPERT_BLOCK - 1))) & (EXPERT_BLOCK - 1)
        first = e * cap + c
        for size in sizes:
            @pl.when((n_pad & size) != 0)
            def _():
                start = first + (n_pad & (size - 1))
                dst = xs_ref.at[pl.ds(pl.multiple_of(start * SUBLANES, SUBLANES), size * SUBLANES)]
                fn(pltpu.make_async_copy(padbuf.at[pl.ds(0, size * SUBLANES)], dst, sem))

    def issue(e, carry):
        pad_copies(e, lambda cp: cp.start())
        return carry

    lax.fori_loop(0, N_EXPERTS, issue, 0)

    def drain(e, carry):
        pad_copies(e, lambda cp: cp.wait())
        return carry

    lax.fori_loop(0, N_EXPERTS, drain, 0)


def _padfill(counts, xs, cap):
    return pl.pallas_call(
        functools.partial(_padfill_kernel, cap=cap),
        grid_spec=pltpu.PrefetchScalarGridSpec(
            num_scalar_prefetch=1,
            grid=(1,),
            in_specs=[pl.BlockSpec(memory_space=pl.ANY)],
            out_specs=pl.BlockSpec(memory_space=pl.ANY),
            scratch_shapes=[pltpu.VMEM((EXPERT_BLOCK // 2 * SUBLANES, LANES), jnp.uint32),
                            pltpu.SemaphoreType.DMA(())]),
        out_shape=jax.ShapeDtypeStruct(xs.shape, xs.dtype),
        input_output_aliases={1: 0},
        compiler_params=_cparams(),
        name="padfill",
    )(counts, xs)


def _expert_kernel(be_ref, br_ref, nb_ref, x_ref, wg_ref, wu_ref, wd_ref, ytok_ref, ybuf0, ybuf1, idv, ids0, ids1,
                   wgb, wub, wdb, sem, sem_ids, *, n_ids):
    del br_ref
    m = EXPERT_BLOCK
    b = pl.program_id(0)
    nb = nb_ref[0]
    ybufs, idss = (ybuf0, ybuf1), (ids0, ids1)
    n_phases = m // LANES
    half = D_MODEL // 2

    @pl.when((b == 0) | (be_ref[b] != be_ref[jnp.maximum(b - 1, 0)]))
    def _():
        wgb[...] = wg_ref[0].astype(BF16)
        wub[...] = wu_ref[0].astype(BF16)
        wdb[...] = wd_ref[0].astype(BF16)

    def issue(s, a0):
        for col in range(LANES):
            pltpu.make_async_copy(_tile_rows(ybufs[s], a0 * LANES + col),
                                  _tile_rows(ytok_ref, idss[s][a0, col]), sem.at[s]).start()

    def step(s, prev):
        tokrep = x_ref[pl.ds(TOK_ROW, m, stride=SUBLANES), :]
        e = be_ref[b]
        below_lo = jnp.where(e >= 32, -1, (1 << jnp.minimum(e, 31)) - 1).astype(jnp.uint32)
        below_hi = jnp.where(e >= 32, (1 << jnp.maximum(e - 32, 0)) - 1, 0).astype(jnp.uint32)
        rank = (lax.population_count(x_ref[pl.ds(MASK_ROWS[0], m, stride=SUBLANES), :] & below_lo)
                + lax.population_count(x_ref[pl.ds(MASK_ROWS[1], m, stride=SUBLANES), :] & below_hi))
        r = lax.broadcasted_iota(jnp.int32, (m, LANES), 0)
        ln = lax.broadcasted_iota(jnp.int32, (m, LANES), 1)
        n_tok = n_ids // TOP_K
        idi = jnp.where(tokrep == jnp.uint32(PAD_ID), n_ids + r,
                        rank.astype(jnp.int32) * n_tok + tokrep.astype(jnp.int32))
        diag = jnp.where((r & (LANES - 1)) == ln, idi, 0).astype(F32)
        idv[...] = jnp.sum(diag.reshape(m // LANES, LANES, LANES), axis=1).astype(jnp.int32)
        ids_copy = pltpu.make_async_copy(idv, idss[s], sem_ids)
        ids_copy.start()

        lo, hi = [], []
        for j in range(PACK_TILES):
            w = x_ref[pl.ds(j, m, stride=SUBLANES), :]
            lo.append(lax.bitcast_convert_type(w << 16, F32).astype(BF16))
            hi.append(lax.bitcast_convert_type(w & jnp.uint32(0xFFFF0000), F32).astype(BF16))
        x = jnp.concatenate(lo + hi, axis=1)
        phases = iter(range(n_phases))
        if prev is not None:
            issue(prev, next(phases))
        gate = jnp.dot(x, wgb[...], preferred_element_type=F32)
        if prev is not None:
            issue(prev, next(phases))
        up = jnp.dot(x, wub[...], preferred_element_type=F32)
        hid = (_silu(gate) * up).astype(BF16)

        def drain(which):
            pltpu.make_async_copy(ybufs[which], ytok_ref.at[pl.ds(0, m * SUBLANES)], sem.at[which]).wait()

        if prev is not None:
            @pl.when(b >= 2)
            def _():
                drain(s)
        for c0 in (0, half):
            if prev is not None:
                issue(prev, next(phases))
            y = jnp.dot(hid, wdb[:, c0:c0 + half], preferred_element_type=F32)
            for j in range(half // LANES):
                ybufs[s][pl.ds(c0 // LANES + j, m, stride=SUBLANES), :] = y[:, j * LANES:(j + 1) * LANES]
        ids_copy.wait()
        if prev is not None:
            for a0 in phases:
                issue(prev, a0)

            @pl.when(b == nb)
            def _():
                drain(prev)

    @pl.when(b == 0)
    def _():
        step(0, None)

    for s in range(2):
        @pl.when((b >= 1) & (b <= nb) & (lax.rem(b, 2) == s))
        def _():
            step(s, 1 - s)


def _experts(block_e, block_row, nb_used, xs, we_gate, we_up, we_down, n_ids):
    n_blocks = block_e.shape[0]
    m = EXPERT_BLOCK
    rows = m * SUBLANES
    return pl.pallas_call(
        functools.partial(_expert_kernel, n_ids=n_ids),
        grid_spec=pltpu.PrefetchScalarGridSpec(
            num_scalar_prefetch=3,
            grid=(n_blocks,),
            in_specs=[pl.BlockSpec((rows, LANES), lambda b, be, br, nb: (br[b], 0)),
                      pl.BlockSpec((1, D_MODEL, D_EXPERT), lambda b, be, br, nb: (be[b], 0, 0)),
                      pl.BlockSpec((1, D_MODEL, D_EXPERT), lambda b, be, br, nb: (be[b], 0, 0)),
                      pl.BlockSpec((1, D_EXPERT, D_MODEL), lambda b, be, br, nb: (be[b], 0, 0))],
            out_specs=pl.BlockSpec(memory_space=pl.ANY),
            scratch_shapes=[pltpu.VMEM((rows, LANES), F32),
                            pltpu.VMEM((rows, LANES), F32),
                            pltpu.VMEM((m // LANES, LANES), jnp.int32),
                            pltpu.SMEM((m // LANES, LANES), jnp.int32),
                            pltpu.SMEM((m // LANES, LANES), jnp.int32),
                            pltpu.VMEM((D_MODEL, D_EXPERT), BF16),
                            pltpu.VMEM((D_MODEL, D_EXPERT), BF16),
                            pltpu.VMEM((D_EXPERT, D_MODEL), BF16),
                            pltpu.SemaphoreType.DMA((2,)),
                            pltpu.SemaphoreType.DMA(())]),
        out_shape=jax.ShapeDtypeStruct(((n_ids + m) * SUBLANES, LANES), F32),
        compiler_params=_cparams(),
        name="experts",
    )(block_e, block_row, nb_used, xs, we_gate, we_up, we_down)


def _combine_kernel(*refs):
    y_refs = refs[:TOP_K]
    w8_ref, base_ref, g2_ref, fg_ref, o_ref = refs[TOP_K:]
    t7 = base_ref.shape[0]
    w8 = w8_ref[...]
    parts = []
    for j in range(ROW_TILES):
        acc = jnp.zeros((t7, LANES), F32)
        for k in range(TOP_K):
            acc = acc + w8[:, k:k + 1] * y_refs[k][pl.ds(j, t7, stride=SUBLANES), :]
        parts.append(acc)
    routed = jnp.concatenate(parts, axis=1)
    h = base_ref[...] + g2_ref[...] * routed
    inv = lax.rsqrt(jnp.mean(h * h, axis=-1, keepdims=True) + EPS)
    o_ref[...] = h * inv * fg_ref[...]


def _combine(ytok, w8, base, g2, fg):
    n = base.shape[0]
    t7 = 128
    y_specs = [pl.BlockSpec((t7 * SUBLANES, LANES), functools.partial(lambda i, k: (k * (n // t7) + i, 0), k=k))
               for k in range(TOP_K)]
    return pl.pallas_call(
        _combine_kernel,
        grid=(n // t7,),
        in_specs=y_specs + [
                  pl.BlockSpec((t7, TOP_K), lambda i: (i, 0)),
                  pl.BlockSpec((t7, D_MODEL), lambda i: (i, 0)),
                  _full((1, D_MODEL)), _full((1, D_MODEL))],
        out_specs=pl.BlockSpec((t7, D_MODEL), lambda i: (i, 0)),
        out_shape=jax.ShapeDtypeStruct((n, D_MODEL), F32),
        compiler_params=_cparams(),
        name="combine",
    )(*([ytok] * TOP_K), w8, base, g2, fg)


def kernel(x, c, w_ada, b_ada, norm_mix_g, w_in, conv_w, conv_b, dt_bias, A_log, D_skip, ssd_norm_g, pool_w,
           pool_scale, w_br_ssd, w_br_pool, w_out, norm_ffn_g, w_router, router_bias, we_gate, we_up, we_down,
           ws_gate, ws_up, ws_down, final_norm_g):
    bsz, seq, _ = x.shape
    assert bsz == 1 and w_ada.shape[0] == 1
    n = seq
    h2 = x.reshape(n, D_MODEL)

    mod = _ada(c, w_ada[0], b_ada[0])
    sh1, sc1, g1, sh2, sc2, g2 = [mod[:, k * D_MODEL:(k + 1) * D_MODEL] for k in range(6)]

    wi = w_in[0]
    o_dt = D_SSD + D_SSD + 2 * D_BC
    w_dt = jnp.pad(wi[:, o_dt:o_dt + N_HEADS], ((0, 0), (0, LANES - N_HEADS))).astype(BF16)
    proj, dt_raw = _in_proj(h2, norm_mix_g[0][None], sh1, sc1, wi[:, :o_dt].astype(BF16),
                            wi[:, o_dt + N_HEADS:].astype(BF16), w_dt)

    pad_h = lambda v: jnp.pad(v[None], ((0, 0), (0, LANES - N_HEADS)))
    yn = _ssd(proj, dt_raw, conv_w[0], conv_b[0][None], pad_h(dt_bias[0]), pad_h(A_log[0]),
              jnp.repeat(D_skip[0], HEADDIM)[None], ssd_norm_g[0][None])

    ws_gu = jnp.concatenate([ws_gate[0], ws_up[0]], axis=1).astype(BF16)
    m = EXPERT_BLOCK
    cap = -(-n // m) * m
    base, w8, cnt, xs = _mix(
        proj, yn, h2, pool_w[0].astype(BF16), pool_scale[0][None], w_br_pool[0].astype(BF16),
        w_br_ssd[0].astype(BF16), w_out[0].astype(BF16), g1, norm_ffn_g[0][None], sh2, sc2, g2,
        w_router[0].astype(BF16), router_bias[0][None], ws_gu, ws_down[0].astype(BF16), cap)

    counts = cnt[0].astype(jnp.int32)
    nblk = (counts + m - 1) // m
    bends = jnp.cumsum(nblk)
    bstarts = bends - nblk
    nb_used = bends[-1]
    n_blocks = -(-(n * TOP_K) // m) + N_EXPERTS + 1
    b_eff = jnp.minimum(jnp.arange(n_blocks, dtype=jnp.int32), nb_used - 1)
    done = bends[None, :] <= b_eff[:, None]
    block_e = jnp.sum(done.astype(jnp.int32), axis=1)
    block_row = block_e * (cap // m) + b_eff - jnp.sum(jnp.where(done, nblk[None, :], 0), axis=1)

    xs = _padfill(counts, xs, cap)
    ytok = _experts(block_e, block_row, nb_used.reshape(1), xs,
                    we_gate[0], we_up[0], we_down[0], n * TOP_K)
    out = _combine(ytok, w8, base, g2, final_norm_g[None])
    return out.reshape(bsz, seq, D_MODEL)
```

```python
import functools

import jax
import jax.numpy as jnp
from jax import lax
from jax.experimental import pallas as pl
from jax.experimental.pallas import tpu as pltpu

F32 = jnp.float32
BF16 = jnp.bfloat16
HIGHEST = lax.Precision.HIGHEST

D_MODEL = 1024
D_SSD = 2048
HEADDIM = 64
N_HEADS = 32
N_GROUPS = 8
HEADS_PER_GROUP = N_HEADS // N_GROUPS
D_STATE = 128
CONV_K = 4
CHUNK = 128
GROUP_W = D_SSD // N_GROUPS
D_BC = N_GROUPS * D_STATE
POOL_WINDOWS = (2, 4, 8, 16)
POOL_GDIM = 256
N_EXPERTS = 64
TOP_K = 8
N_EGROUPS = 8
EXPERTS_PER_GROUP = 8
TOPK_GROUPS = 4
D_EXPERT = 256
D_SHARED = 256
ROUTED_SCALE = 2.5
MOE_BLOCK = 128
EPS = 1e-6

LANES = 128
SUBLANES = 8
ROW_TILES = D_MODEL // LANES
PACK_TILES = ROW_TILES // 2
PACK_W = PACK_TILES * LANES
EXPERT_BLOCK = 512
MIX_ISSUE_GROUPS = 8
BEFORE_STAGING = "before_staging"

PROJ_W = D_SSD + D_SSD + 2 * D_BC + 3 * D_MODEL
PROJ_CHUNK = 512

VMEM_LIMIT = 56 * 1024 * 1024


def _cparams(sem=("arbitrary",)):
    return pltpu.CompilerParams(dimension_semantics=sem, vmem_limit_bytes=VMEM_LIMIT)


def _full(shape):
    nd = len(shape)
    return pl.BlockSpec(shape, lambda *_: (0,) * nd)


def _silu(v):
    return v * jax.nn.sigmoid(v)


def _ada_kernel(c_ref, w_ref, b_ref, o_ref):
    c = c_ref[...]
    o_ref[...] = jnp.dot(_silu(c), w_ref[...], preferred_element_type=F32, precision=HIGHEST) + b_ref[...]


def _ada(c, w_ada, b_ada):
    n_out = w_ada.shape[1]
    tn = 1536
    c8 = jnp.broadcast_to(c, (SUBLANES, D_MODEL))
    out = pl.pallas_call(
        _ada_kernel,
        grid=(n_out // tn,),
        in_specs=[_full((SUBLANES, D_MODEL)),
                  pl.BlockSpec((D_MODEL, tn), lambda j: (0, j)),
                  pl.BlockSpec((1, tn), lambda j: (0, j))],
        out_specs=pl.BlockSpec((SUBLANES, tn), lambda j: (0, j)),
        out_shape=jax.ShapeDtypeStruct((SUBLANES, n_out), F32),
        compiler_params=_cparams(),
        name="ada",
    )(c8, w_ada, b_ada.reshape(1, n_out))
    return out[0:1]


def _inproj_kernel(x_ref, g_ref, sh_ref, sc_ref, wa_ref, wb_ref, wdt_ref, proj_ref, dt_ref):
    x = x_ref[...]
    inv = lax.rsqrt(jnp.mean(x * x, axis=-1, keepdims=True) + EPS)
    u = x * inv * g_ref[...]
    u = u * (1.0 + sc_ref[...]) + sh_ref[...]
    ub = u.astype(BF16)
    wa = wa_ref.shape[1]
    for c0 in range(0, PROJ_W, PROJ_CHUNK):
        w = wa_ref[:, c0:c0 + PROJ_CHUNK] if c0 < wa else wb_ref[:, c0 - wa:c0 - wa + PROJ_CHUNK]
        proj_ref[:, c0:c0 + PROJ_CHUNK] = jnp.dot(ub, w, preferred_element_type=F32).astype(BF16)
    dt_ref[...] = jnp.dot(ub, wdt_ref[...], preferred_element_type=F32)


def _in_proj(x2, g, sh, sc, w_a, w_b, w_dt):
    n = x2.shape[0]
    tm = 256
    return pl.pallas_call(
        _inproj_kernel,
        grid=(n // tm,),
        in_specs=[pl.BlockSpec((tm, D_MODEL), lambda i: (i, 0)),
                  _full((1, D_MODEL)), _full((1, D_MODEL)), _full((1, D_MODEL)),
                  pl.BlockSpec(w_a.shape, lambda i: (0, 0), pipeline_mode=pl.Buffered(1)),
                  pl.BlockSpec(w_b.shape, lambda i: (0, 0), pipeline_mode=pl.Buffered(1)),
                  _full((D_MODEL, LANES))],
        out_specs=[pl.BlockSpec((tm, PROJ_W), lambda i: (i, 0)),
                   pl.BlockSpec((tm, LANES), lambda i: (i, 0))],
        out_shape=[jax.ShapeDtypeStruct((n, PROJ_W), BF16),
                   jax.ShapeDtypeStruct((n, LANES), F32)],
        compiler_params=_cparams(),
        name="in_proj",
    )(x2, g, sh, sc, w_a, w_b, w_dt)


def _conv_silu(cur_ref, ext_ref, w_ref, b_ref, out_ref, width, cw=512):
    t = cur_ref.shape[0]
    for c0 in range(0, width, cw):
        sl = slice(c0, c0 + cw)
        cur = cur_ref[:, sl].astype(F32)
        ext_ref[SUBLANES:, sl] = cur
        acc = cur * w_ref[CONV_K - 1:CONV_K, sl] + b_ref[:, sl]
        for s in range(1, CONV_K):
            acc = acc + ext_ref[pl.ds(SUBLANES - s, t), sl] * w_ref[CONV_K - 1 - s:CONV_K - s, sl]
        out_ref[:, sl] = _silu(acc).astype(out_ref.dtype)
        ext_ref[0:SUBLANES, sl] = cur[t - SUBLANES:t]


def _ssd_kernel(z_ref, xs_ref, bc_ref, dt_ref, cwx_ref, cwbc_ref, cbx_ref, cbbc_ref, dtb_ref, alog_ref,
                dskip_ref, ng_ref, o_ref, tailx, tailbc, state, xc, bcc, ybuf):
    q = CHUNK
    pair_w = 2 * HEADDIM

    @pl.when(pl.program_id(0) == 0)
    def _():
        tailx[...] = jnp.zeros_like(tailx)
        tailbc[...] = jnp.zeros_like(tailbc)
        state[...] = jnp.zeros_like(state)

    _conv_silu(xs_ref, tailx, cwx_ref, cbx_ref, xc, D_SSD)
    _conv_silu(bc_ref, tailbc, cwbc_ref, cbbc_ref, bcc, 2 * D_BC)

    v = dt_ref[...] + dtb_ref[...]
    dt = jnp.maximum(v, 0.0) + jnp.log(1.0 + jnp.exp(-jnp.abs(v)))
    a = dt * (-jnp.exp(alog_ref[...]))
    ri = lax.broadcasted_iota(jnp.int32, (q, q), 0)
    ci = lax.broadcasted_iota(jnp.int32, (q, q), 1)
    causal = ri >= ci
    a_cs = jnp.dot(causal.astype(F32), a, preferred_element_type=F32, precision=HIGHEST)
    a_cs_t = a_cs.T
    dt_t = dt.T
    first_half = ci < HEADDIM
    first_half_row = first_half[0:1, :]

    for g in range(N_GROUPS):
        b_g = bcc[:, g * D_STATE:(g + 1) * D_STATE]
        c_g = bcc[:, D_BC + g * D_STATE:D_BC + (g + 1) * D_STATE]
        cb = lax.dot_general(c_g, b_g, (((1,), (1,)), ((), ())), preferred_element_type=F32)
        b_t = b_g.astype(F32).T
        st_g = state[g]
        y_off = jnp.dot(c_g, st_g.astype(BF16), preferred_element_type=F32)
        for pi in range(HEADS_PER_GROUP // 2):
            h0 = g * HEADS_PER_GROUP + 2 * pi
            lanes = slice(h0 * HEADDIM, h0 * HEADDIM + pair_w)
            gl = slice(pi * pair_w, (pi + 1) * pair_w)
            ms, ws, cols, lasts = [], [], [], []
            for h in (h0, h0 + 1):
                col = jnp.broadcast_to(a_cs[:, h:h + 1], (q, q))
                row = a_cs_t[h:h + 1, :]
                dtrow = dt_t[h:h + 1, :]
                decay = jnp.exp(jnp.where(causal, col - row, -jnp.inf))
                ms.append((cb * decay * dtrow).astype(BF16))
                last = a_cs_t[h:h + 1, q - 1:q]
                ws.append((b_t * (jnp.exp(last - row) * dtrow)).astype(BF16))
                cols.append(col)
                lasts.append(last)
            xp = xc[:, lanes]
            zero = jnp.zeros_like(xp)
            x_bd = jnp.concatenate([jnp.where(first_half, xp, zero), jnp.where(first_half, zero, xp)], axis=0)
            y_p = jnp.dot(jnp.concatenate(ms, axis=1), x_bd, preferred_element_type=F32)
            y_p = y_p + y_off[:, gl] * jnp.exp(jnp.where(first_half, cols[0], cols[1]))
            ybuf[:, lanes] = y_p
            s_new = jnp.dot(jnp.concatenate(ws, axis=1), x_bd, preferred_element_type=F32)
            carry = jnp.exp(jnp.where(first_half_row, lasts[0], lasts[1]))
            state[g, :, gl] = st_g[:, gl] * carry + s_new

    for g in range(N_GROUPS):
        sl = slice(g * GROUP_W, (g + 1) * GROUP_W)
        zg = z_ref[:, sl].astype(F32)
        yg = (ybuf[:, sl] + dskip_ref[:, sl] * xc[:, sl].astype(F32)) * _silu(zg)
        inv = lax.rsqrt(jnp.mean(yg * yg, axis=-1, keepdims=True) + EPS)
        o_ref[:, sl] = (yg * inv * ng_ref[:, sl]).astype(BF16)


def _ssd(proj, dt_raw, conv_w, conv_b, dt_bias, a_log, d_skip, norm_g):
    n = proj.shape[0]
    t = CHUNK
    return pl.pallas_call(
        _ssd_kernel,
        grid=(n // t,),
        in_specs=[pl.BlockSpec((t, D_SSD), lambda i: (i, 0)),
                  pl.BlockSpec((t, D_SSD), lambda i: (i, 1)),
                  pl.BlockSpec((t, 2 * D_BC), lambda i: (i, 2)),
                  pl.BlockSpec((t, LANES), lambda i: (i, 0)),
                  pl.BlockSpec((CONV_K, D_SSD), lambda i: (0, 0)),
                  pl.BlockSpec((CONV_K, 2 * D_BC), lambda i: (0, 1)),
                  pl.BlockSpec((1, D_SSD), lambda i: (0, 0)),
                  pl.BlockSpec((1, 2 * D_BC), lambda i: (0, 1)),
                  _full((1, LANES)), _full((1, LANES)),
                  _full((1, D_SSD)), _full((1, D_SSD))],
        out_specs=pl.BlockSpec((t, D_SSD), lambda i: (i, 0)),
        out_shape=jax.ShapeDtypeStruct((n, D_SSD), BF16),
        scratch_shapes=[pltpu.VMEM((SUBLANES + t, D_SSD), F32),
                        pltpu.VMEM((SUBLANES + t, 2 * D_BC), F32),
                        pltpu.VMEM((N_GROUPS, D_STATE, GROUP_W), F32),
                        pltpu.VMEM((t, D_SSD), BF16),
                        pltpu.VMEM((t, 2 * D_BC), BF16),
                        pltpu.VMEM((t, D_SSD), F32)],
        compiler_params=_cparams(),
        name="ssd",
    )(proj, proj, proj, dt_raw, conv_w, conv_w, conv_b, conv_b, dt_bias, a_log, d_skip, norm_g)


PAD_ID = 0xFFFFFFFF
TOK_ROW = PACK_TILES
MASK_ROWS = (PACK_TILES + 1, PACK_TILES + 2)


def _tile_rows(ref, first_row):
    return ref.at[pl.ds(pl.multiple_of(first_row * SUBLANES, SUBLANES), SUBLANES)]


def _mix_kernel(xp_ref, gs_ref, gp_ref, yn_ref, x_ref, poolw_ref, pscale_ref, wbp_ref, wbs_ref, wout_ref,
                g1_ref, ng_ref, sh2_ref, sc2_ref, g2_ref, wr_ref, rb_ref, wsgu_ref, wsd_ref, zero_ref,
                base_ref, ws_ref, cnt_ref, xs_ref, ptail, run, stage, dvm, dsm, sem, sem_s, *, n_tiles, cap):
    t = x_ref.shape[0]
    i = pl.program_id(0)
    slot = lax.rem(i, 2)

    @pl.when(i == 0)
    def _():
        ptail[...] = jnp.zeros_like(ptail)
        run[...] = jnp.zeros_like(run)
        stage[...] = jnp.zeros_like(stage)

    def issue_rows(s, group, after=None):
        zero = 0
        if after is not None:
            probe = jnp.max(jnp.abs(after[0:SUBLANES, 0:min(LANES, after.shape[1])])).astype(jnp.int32)
            zero = probe * zero_ref[0]
        per = t // MIX_ISSUE_GROUPS
        for tt in range(group * per, (group + 1) * per):
            for k in range(TOP_K):
                pltpu.make_async_copy(stage.at[s, pl.ds(tt * SUBLANES, SUBLANES)],
                                      _tile_rows(xs_ref, dsm[s, k, tt] + zero), sem.at[s]).start()

    def drain_rows(s):
        for _ in range(TOP_K):
            pltpu.make_async_copy(stage.at[s], xs_ref.at[pl.ds(0, t * SUBLANES)], sem.at[s]).wait()

    def tile(s, milestone=None):
        _mix_tile_body(xp_ref, gs_ref, gp_ref, yn_ref, x_ref, poolw_ref, pscale_ref, wbp_ref, wbs_ref, wout_ref,
                       g1_ref, ng_ref, sh2_ref, sc2_ref, g2_ref, wr_ref, rb_ref, wsgu_ref, wsd_ref,
                       base_ref, ws_ref, cnt_ref, ptail, run, stage, dvm, dsm, sem_s, i, s, cap, milestone)

    @pl.when(i == 0)
    def _():
        tile(0)

    def milestone(group, value):
        if group == BEFORE_STAGING:
            @pl.when(i >= 2)
            def _():
                drain_rows(slot)
        else:
            issue_rows(1 - slot, group, value)

    @pl.when((i >= 1) & (i < n_tiles))
    def _():
        issue_rows(1 - slot, 0)
        tile(slot, milestone)

    @pl.when(i == n_tiles)
    def _():
        for group in range(MIX_ISSUE_GROUPS):
            issue_rows(1 - slot, group)
        if n_tiles >= 2:
            drain_rows(slot)
        drain_rows(1 - slot)


def _mix_tile_body(xp_ref, gs_ref, gp_ref, yn_ref, x_ref, poolw_ref, pscale_ref, wbp_ref, wbs_ref, wout_ref,
                   g1_ref, ng_ref, sh2_ref, sc2_ref, g2_ref, wr_ref, rb_ref, wsgu_ref, wsd_ref,
                   base_ref, ws_ref, cnt_ref, ptail, run, stage, dvm, dsm, sem_s, i, slot, cap, milestone):
    t = x_ref.shape[0]
    if milestone is None:
        milestone = lambda group, value: None

    xp = xp_ref[...].astype(F32)
    ext = jnp.concatenate([ptail[...], xp], axis=0)
    ptail[...] = xp[t - 2 * SUBLANES:t]
    pos = (lax.broadcasted_iota(jnp.int32, (t, 1), 0) + (i * t + 1)).astype(F32)
    pooled = []
    for gi, w in enumerate(POOL_WINDOWS):
        sl = slice(gi * POOL_GDIM, (gi + 1) * POOL_GDIM)
        e = ext[:, sl]
        s = e
        span = 1
        while span < w:
            s = s + pltpu.roll(s, span, axis=0)
            span *= 2
        win = s[2 * SUBLANES:]
        mean = win / jnp.minimum(pos, float(w))
        pg = (mean - xp[:, sl]).astype(BF16)
        mixed = jnp.dot(pg, poolw_ref[gi], preferred_element_type=F32)
        pooled.append(mixed * pscale_ref[:, sl])
    pooled = jnp.concatenate(pooled, axis=1).astype(BF16)
    y_pool = jnp.dot(pooled, wbp_ref[...], preferred_element_type=F32)
    milestone(1, y_pool)
    y_ssd = jnp.dot(yn_ref[...], wbs_ref[...], preferred_element_type=F32)
    milestone(2, y_ssd)
    g_ssd = jax.nn.sigmoid(gs_ref[...].astype(F32))
    g_pool = jax.nn.sigmoid(gp_ref[...].astype(F32))
    mixed = (g_ssd * y_ssd + g_pool * y_pool).astype(BF16)
    h = x_ref[...] + g1_ref[...] * jnp.dot(mixed, wout_ref[...], preferred_element_type=F32)
    milestone(3, h)

    inv = lax.rsqrt(jnp.mean(h * h, axis=-1, keepdims=True) + EPS)
    u2 = h * inv * ng_ref[...]
    u2 = u2 * (1.0 + sc2_ref[...]) + sh2_ref[...]
    u2b = u2.astype(BF16)
    u2r = u2b.astype(F32)
    packed = []
    for j in range(PACK_TILES):
        lo = lax.bitcast_convert_type(u2r[:, j * LANES:(j + 1) * LANES], jnp.uint32)
        hi = lax.bitcast_convert_type(u2r[:, (j + PACK_TILES) * LANES:(j + PACK_TILES + 1) * LANES], jnp.uint32)
        packed.append((lo >> 16) | (hi & jnp.uint32(0xFFFF0000)))

    hs = jnp.dot(u2b, wsgu_ref[...], preferred_element_type=F32)
    milestone(4, hs)
    act = (_silu(hs[:, :D_SHARED]) * hs[:, D_SHARED:]).astype(BF16)
    shared = jnp.dot(act, wsd_ref[...], preferred_element_type=F32)
    milestone(5, shared)
    base_ref[...] = h + g2_ref[...] * shared

    logits = jnp.dot(u2b, wr_ref[...], preferred_element_type=F32)
    scores = jax.nn.sigmoid(logits)
    choice = scores + rb_ref[...]
    lane = lax.broadcasted_iota(jnp.int32, (t, N_EXPERTS), 1)
    lane_f = lane.astype(F32)
    lane_grp = lane // EXPERTS_PER_GROUP
    neg = -jnp.inf
    gscore = []
    for g in range(N_EGROUPS):
        vg = jnp.where(lane_grp == g, choice, neg)
        m1 = jnp.max(vg, axis=-1, keepdims=True)
        i1 = jnp.min(jnp.where(vg == m1, lane_f, float(N_EXPERTS)), axis=-1, keepdims=True)
        m2 = jnp.max(jnp.where(lane_f == i1, neg, vg), axis=-1, keepdims=True)
        gscore.append(m1 + m2)
    milestone(6, gscore[-1])
    gmask = jnp.zeros((t, N_EXPERTS), jnp.bool_)
    for g in range(N_EGROUPS):
        rank = jnp.zeros((t, 1), F32)
        for g2 in range(N_EGROUPS):
            if g2 == g:
                continue
            better = (gscore[g2] > gscore[g]) | ((gscore[g2] == gscore[g]) & (g2 < g))
            rank = rank + better.astype(F32)
        gmask = gmask | ((rank < float(TOPK_GROUPS)) & (lane_grp == g))
    work = jnp.where(gmask, choice, neg)
    sel = jnp.zeros((t, N_EXPERTS), jnp.bool_)
    onehots, idxs, sks = [], [], []
    for k in range(TOP_K):
        m = jnp.max(work, axis=-1, keepdims=True)
        idx = jnp.min(jnp.where(work == m, lane_f, float(N_EXPERTS)), axis=-1, keepdims=True)
        oh = lane_f == idx
        onehots.append(oh)
        idxs.append(idx)
        sks.append(jnp.sum(jnp.where(oh, scores, 0.0), axis=-1, keepdims=True))
        sel = sel | oh
        work = jnp.where(oh, neg, work)
    denom = sks[0]
    for k in range(1, TOP_K):
        denom = denom + sks[k]
    milestone(7, denom)

    ri = lax.broadcasted_iota(jnp.int32, (t, t), 0)
    ci = lax.broadcasted_iota(jnp.int32, (t, t), 1)
    before = (ri > ci).astype(BF16)
    sel_f = jnp.where(sel, 1.0, 0.0)
    sel_b = sel_f.astype(BF16)
    pos_tile = jnp.dot(before, sel_b, preferred_element_type=F32) + run[...]
    er = lax.broadcasted_iota(jnp.int32, (N_EXPERTS, N_EXPERTS), 0)
    ec = lax.broadcasted_iota(jnp.int32, (N_EXPERTS, N_EXPERTS), 1)
    rank_tile = jnp.dot(sel_b, (er < ec).astype(BF16), preferred_element_type=F32)
    lane128 = lax.broadcasted_iota(jnp.int32, (t, LANES), 1)
    k_iota = lax.broadcasted_iota(jnp.int32, (t, TOP_K), 1).astype(F32)
    dest = jnp.zeros((t, LANES), F32)
    w_sorted = jnp.zeros((t, TOP_K), F32)
    for k in range(TOP_K):
        pk = jnp.sum(jnp.where(onehots[k], pos_tile, 0.0), axis=-1, keepdims=True)
        rk = jnp.sum(jnp.where(onehots[k], rank_tile, 0.0), axis=-1, keepdims=True)
        dest = jnp.where(lane128 == k, idxs[k] * float(cap) + pk, dest)
        w_sorted = jnp.where(k_iota == rk, sks[k] / denom * ROUTED_SCALE, w_sorted)
    ws_ref[...] = w_sorted
    total = run[...] + jnp.sum(sel_f, axis=0, keepdims=True)
    run[...] = total
    cnt_ref[...] = total

    milestone(BEFORE_STAGING, None)
    for j in range(PACK_TILES):
        stage[slot, pl.ds(j, t, stride=SUBLANES), :] = packed[j]
    tok = lax.broadcasted_iota(jnp.int32, (t, LANES), 0) + i * t
    stage[slot, pl.ds(TOK_ROW, t, stride=SUBLANES), :] = tok.astype(jnp.uint32)
    bit = jnp.where(sel, jnp.left_shift(1, lane & 15), 0).astype(F32)
    words = []
    for q in range(N_EXPERTS // 16):
        part = jnp.sum(jnp.where((lane >> 4) == q, bit, 0.0), axis=-1, keepdims=True)
        words.append(part.astype(jnp.int32).astype(jnp.uint32))
    for w, row in enumerate(MASK_ROWS):
        word = words[2 * w] | (words[2 * w + 1] << 16)
        stage[slot, pl.ds(row, t, stride=SUBLANES), :] = jnp.broadcast_to(word, (t, LANES))

    dvm[...] = dest.T[0:TOP_K, :].astype(jnp.int32)
    smem_copy = pltpu.make_async_copy(dvm, dsm.at[slot], sem_s)
    smem_copy.start()
    smem_copy.wait()


def _mix(proj, yn, x2, pool_w, pool_scale, w_br_pool, w_br_ssd, w_out, g1, ng, sh2, sc2, g2,
         w_router, router_bias, ws_gu, ws_down, cap):
    n = x2.shape[0]
    t = 256
    n_tiles = n // t
    xp_blk = (D_SSD + D_SSD + 2 * D_BC) // D_MODEL
    tile = lambda i: jnp.minimum(i, n_tiles - 1)
    row = lambda i: (tile(i), 0)
    return pl.pallas_call(
        functools.partial(_mix_kernel, n_tiles=n_tiles, cap=cap),
        grid=(n_tiles + 1,),
        in_specs=[pl.BlockSpec((t, D_MODEL), lambda i: (tile(i), xp_blk)),
                  pl.BlockSpec((t, D_MODEL), lambda i: (tile(i), xp_blk + 1)),
                  pl.BlockSpec((t, D_MODEL), lambda i: (tile(i), xp_blk + 2)),
                  pl.BlockSpec((t, D_SSD), row),
                  pl.BlockSpec((t, D_MODEL), row),
                  _full((len(POOL_WINDOWS), POOL_GDIM, POOL_GDIM)),
                  _full((1, D_MODEL)),
                  _full((D_MODEL, D_MODEL)),
                  _full((D_SSD, D_MODEL)),
                  _full((D_MODEL, D_MODEL)),
                  _full((1, D_MODEL)), _full((1, D_MODEL)), _full((1, D_MODEL)), _full((1, D_MODEL)),
                  _full((1, D_MODEL)),
                  _full((D_MODEL, N_EXPERTS)),
                  _full((1, N_EXPERTS)),
                  _full((D_MODEL, 2 * D_SHARED)),
                  _full((D_SHARED, D_MODEL)),
                  pl.BlockSpec(memory_space=pltpu.SMEM)],
        out_specs=[pl.BlockSpec((t, D_MODEL), row),
                   pl.BlockSpec((t, TOP_K), row),
                   _full((1, N_EXPERTS)),
                   pl.BlockSpec(memory_space=pl.ANY)],
        out_shape=[jax.ShapeDtypeStruct((n, D_MODEL), F32),
                   jax.ShapeDtypeStruct((n, TOP_K), F32),
                   jax.ShapeDtypeStruct((1, N_EXPERTS), F32),
                   jax.ShapeDtypeStruct((N_EXPERTS * cap * SUBLANES, LANES), jnp.uint32)],
        scratch_shapes=[pltpu.VMEM((2 * SUBLANES, D_MODEL), F32),
                        pltpu.VMEM((1, N_EXPERTS), F32),
                        pltpu.VMEM((2, t * SUBLANES, LANES), jnp.uint32),
                        pltpu.VMEM((TOP_K, t), jnp.int32),
                        pltpu.SMEM((2, TOP_K, t), jnp.int32),
                        pltpu.SemaphoreType.DMA((2,)),
                        pltpu.SemaphoreType.DMA(())],
        compiler_params=_cparams(),
        name="mix",
    )(proj, proj, proj, yn, x2, pool_w, pool_scale, w_br_pool, w_br_ssd, w_out, g1, ng, sh2, sc2, g2,
      w_router, router_bias, ws_gu, ws_down, jnp.zeros((1,), jnp.int32))


def _padfill_kernel(cnt_ref, xs_in_ref, xs_ref, padbuf, sem, *, cap):
    del xs_in_ref
    sub = lax.broadcasted_iota(jnp.int32, padbuf.shape, 0) & (SUBLANES - 1)
    padbuf[...] = jnp.where(sub == TOK_ROW, jnp.uint32(PAD_ID), jnp.uint32(0))
    sizes = [1 << bit for bit in range(EXPERT_BLOCK.bit_length() - 1)]

    def pad_copies(e, fn):
        c = cnt_ref[e]
        n_pad = (EXPERT_BLOCK - (c & (EXPERT_BLOCK - 1))) & (EXPERT_BLOCK - 1)
        first = e * cap + c
        for size in sizes:
            @pl.when((n_pad & size) != 0)
            def _():
                start = first + (n_pad & (size - 1))
                dst = xs_ref.at[pl.ds(pl.multiple_of(start * SUBLANES, SUBLANES), size * SUBLANES)]
                fn(pltpu.make_async_copy(padbuf.at[pl.ds(0, size * SUBLANES)], dst, sem))

    def issue(e, carry):
        pad_copies(e, lambda cp: cp.start())
        return carry

    lax.fori_loop(0, N_EXPERTS, issue, 0)

    def drain(e, carry):
        pad_copies(e, lambda cp: cp.wait())
        return carry

    lax.fori_loop(0, N_EXPERTS, drain, 0)


def _padfill(counts, xs, cap):
    return pl.pallas_call(
        functools.partial(_padfill_kernel, cap=cap),
        grid_spec=pltpu.PrefetchScalarGridSpec(
            num_scalar_prefetch=1,
            grid=(1,),
            in_specs=[pl.BlockSpec(memory_space=pl.ANY)],
            out_specs=pl.BlockSpec(memory_space=pl.ANY),
            scratch_shapes=[pltpu.VMEM((EXPERT_BLOCK // 2 * SUBLANES, LANES), jnp.uint32),
                            pltpu.SemaphoreType.DMA(())]),
        out_shape=jax.ShapeDtypeStruct(xs.shape, xs.dtype),
        input_output_aliases={1: 0},
        compiler_params=_cparams(),
        name="padfill",
    )(counts, xs)


def _expert_kernel(be_ref, br_ref, nb_ref, x_ref, wg_ref, wu_ref, wd_ref, ytok_ref, ybuf0, ybuf1, idv, ids0, ids1,
                   wgb, wub, wdb, sem, sem_ids, *, n_ids):
    del br_ref
    m = EXPERT_BLOCK
    b = pl.program_id(0)
    nb = nb_ref[0]
    ybufs, idss = (ybuf0, ybuf1), (ids0, ids1)
    n_phases = m // LANES
    half = D_MODEL // 2

    @pl.when((b == 0) | (be_ref[b] != be_ref[jnp.maximum(b - 1, 0)]))
    def _():
        wgb[...] = wg_ref[0].astype(BF16)
        wub[...] = wu_ref[0].astype(BF16)
        wdb[...] = wd_ref[0].astype(BF16)

    def issue(s, a0):
        for col in range(LANES):
            pltpu.make_async_copy(_tile_rows(ybufs[s], a0 * LANES + col),
                                  _tile_rows(ytok_ref, idss[s][a0, col]), sem.at[s]).start()

    def step(s, prev):
        tokrep = x_ref[pl.ds(TOK_ROW, m, stride=SUBLANES), :]
        e = be_ref[b]
        below_lo = jnp.where(e >= 32, -1, (1 << jnp.minimum(e, 31)) - 1).astype(jnp.uint32)
        below_hi = jnp.where(e >= 32, (1 << jnp.maximum(e - 32, 0)) - 1, 0).astype(jnp.uint32)
        rank = (lax.population_count(x_ref[pl.ds(MASK_ROWS[0], m, stride=SUBLANES), :] & below_lo)
                + lax.population_count(x_ref[pl.ds(MASK_ROWS[1], m, stride=SUBLANES), :] & below_hi))
        r = lax.broadcasted_iota(jnp.int32, (m, LANES), 0)
        ln = lax.broadcasted_iota(jnp.int32, (m, LANES), 1)
        n_tok = n_ids // TOP_K
        idi = jnp.where(tokrep == jnp.uint32(PAD_ID), n_ids + r,
                        rank.astype(jnp.int32) * n_tok + tokrep.astype(jnp.int32))
        diag = jnp.where((r & (LANES - 1)) == ln, idi, 0).astype(F32)
        idv[...] = jnp.sum(diag.reshape(m // LANES, LANES, LANES), axis=1).astype(jnp.int32)
        ids_copy = pltpu.make_async_copy(idv, idss[s], sem_ids)
        ids_copy.start()

        lo, hi = [], []
        for j in range(PACK_TILES):
            w = x_ref[pl.ds(j, m, stride=SUBLANES), :]
            lo.append(lax.bitcast_convert_type(w << 16, F32).astype(BF16))
            hi.append(lax.bitcast_convert_type(w & jnp.uint32(0xFFFF0000), F32).astype(BF16))
        x = jnp.concatenate(lo + hi, axis=1)
        phases = iter(range(n_phases))
        if prev is not None:
            issue(prev, next(phases))
        gate = jnp.dot(x, wgb[...], preferred_element_type=F32)
        if prev is not None:
            issue(prev, next(phases))
        up = jnp.dot(x, wub[...], preferred_element_type=F32)
        hid = (_silu(gate) * up).astype(BF16)

        def drain(which):
            pltpu.make_async_copy(ybufs[which], ytok_ref.at[pl.ds(0, m * SUBLANES)], sem.at[which]).wait()

        if prev is not None:
            @pl.when(b >= 2)
            def _():
                drain(s)
        for c0 in (0, half):
            if prev is not None:
                issue(prev, next(phases))
            y = jnp.dot(hid, wdb[:, c0:c0 + half], preferred_element_type=F32)
            for j in range(half // LANES):
                ybufs[s][pl.ds(c0 // LANES + j, m, stride=SUBLANES), :] = y[:, j * LANES:(j + 1) * LANES]
        ids_copy.wait()
        if prev is not None:
            for a0 in phases:
                issue(prev, a0)

            @pl.when(b == nb)
            def _():
                drain(prev)

    @pl.when(b == 0)
    def _():
        step(0, None)

    for s in range(2):
        @pl.when((b >= 1) & (b <= nb) & (lax.rem(b, 2) == s))
        def _():
            step(s, 1 - s)


def _experts(block_e, block_row, nb_used, xs, we_gate, we_up, we_down, n_ids):
    n_blocks = block_e.shape[0]
    m = EXPERT_BLOCK
    rows = m * SUBLANES
    return pl.pallas_call(
        functools.partial(_expert_kernel, n_ids=n_ids),
        grid_spec=pltpu.PrefetchScalarGridSpec(
            num_scalar_prefetch=3,
            grid=(n_blocks,),
            in_specs=[pl.BlockSpec((rows, LANES), lambda b, be, br, nb: (br[b], 0)),
                      pl.BlockSpec((1, D_MODEL, D_EXPERT), lambda b, be, br, nb: (be[b], 0, 0)),
                      pl.BlockSpec((1, D_MODEL, D_EXPERT), lambda b, be, br, nb: (be[b], 0, 0)),
                      pl.BlockSpec((1, D_EXPERT, D_MODEL), lambda b, be, br, nb: (be[b], 0, 0))],
            out_specs=pl.BlockSpec(memory_space=pl.ANY),
            scratch_shapes=[pltpu.VMEM((rows, LANES), F32),
                            pltpu.VMEM((rows, LANES), F32),
                            pltpu.VMEM((m // LANES, LANES), jnp.int32),
                            pltpu.SMEM((m // LANES, LANES), jnp.int32),
                            pltpu.SMEM((m // LANES, LANES), jnp.int32),
                            pltpu.VMEM((D_MODEL, D_EXPERT), BF16),
                            pltpu.VMEM((D_MODEL, D_EXPERT), BF16),
                            pltpu.VMEM((D_EXPERT, D_MODEL), BF16),
                            pltpu.SemaphoreType.DMA((2,)),
                            pltpu.SemaphoreType.DMA(())]),
        out_shape=jax.ShapeDtypeStruct(((n_ids + m) * SUBLANES, LANES), F32),
        compiler_params=_cparams(),
        name="experts",
    )(block_e, block_row, nb_used, xs, we_gate, we_up, we_down)


def _combine_kernel(*refs):
    y_refs = refs[:TOP_K]
    w8_ref, base_ref, g2_ref, fg_ref, o_ref = refs[TOP_K:]
    t7 = base_ref.shape[0]
    w8 = w8_ref[...]
    parts = []
    for j in range(ROW_TILES):
        acc = jnp.zeros((t7, LANES), F32)
        for k in range(TOP_K):
            acc = acc + w8[:, k:k + 1] * y_refs[k][pl.ds(j, t7, stride=SUBLANES), :]
        parts.append(acc)
    routed = jnp.concatenate(parts, axis=1)
    h = base_ref[...] + g2_ref[...] * routed
    inv = lax.rsqrt(jnp.mean(h * h, axis=-1, keepdims=True) + EPS)
    o_ref[...] = h * inv * fg_ref[...]


def _combine(ytok, w8, base, g2, fg):
    n = base.shape[0]
    t7 = 128
    y_specs = [pl.BlockSpec((t7 * SUBLANES, LANES), functools.partial(lambda i, k: (k * (n // t7) + i, 0), k=k))
               for k in range(TOP_K)]
    return pl.pallas_call(
        _combine_kernel,
        grid=(n // t7,),
        in_specs=y_specs + [
                  pl.BlockSpec((t7, TOP_K), lambda i: (i, 0)),
                  pl.BlockSpec((t7, D_MODEL), lambda i: (i, 0)),
                  _full((1, D_MODEL)), _full((1, D_MODEL))],
        out_specs=pl.BlockSpec((t7, D_MODEL), lambda i: (i, 0)),
        out_shape=jax.ShapeDtypeStruct((n, D_MODEL), F32),
        compiler_params=_cparams(),
        name="combine",
    )(*([ytok] * TOP_K), w8, base, g2, fg)


def kernel(x, c, w_ada, b_ada, norm_mix_g, w_in, conv_w, conv_b, dt_bias, A_log, D_skip, ssd_norm_g, pool_w,
           pool_scale, w_br_ssd, w_br_pool, w_out, norm_ffn_g, w_router, router_bias, we_gate, we_up, we_down,
           ws_gate, ws_up, ws_down, final_norm_g):
    bsz, seq, _ = x.shape
    assert bsz == 1 and w_ada.shape[0] == 1
    n = seq
    h2 = x.reshape(n, D_MODEL)

    mod = _ada(c, w_ada[0], b_ada[0])
    sh1, sc1, g1, sh2, sc2, g2 = [mod[:, k * D_MODEL:(k + 1) * D_MODEL] for k in range(6)]

    wi = w_in[0]
    o_dt = D_SSD + D_SSD + 2 * D_BC
    w_dt = jnp.pad(wi[:, o_dt:o_dt + N_HEADS], ((0, 0), (0, LANES - N_HEADS))).astype(BF16)
    proj, dt_raw = _in_proj(h2, norm_mix_g[0][None], sh1, sc1, wi[:, :o_dt].astype(BF16),
                            wi[:, o_dt + N_HEADS:].astype(BF16), w_dt)

    pad_h = lambda v: jnp.pad(v[None], ((0, 0), (0, LANES - N_HEADS)))
    yn = _ssd(proj, dt_raw, conv_w[0], conv_b[0][None], pad_h(dt_bias[0]), pad_h(A_log[0]),
              jnp.repeat(D_skip[0], HEADDIM)[None], ssd_norm_g[0][None])

    ws_gu = jnp.concatenate([ws_gate[0], ws_up[0]], axis=1).astype(BF16)
    m = EXPERT_BLOCK
    cap = -(-n // m) * m
    base, w8, cnt, xs = _mix(
        proj, yn, h2, pool_w[0].astype(BF16), pool_scale[0][None], w_br_pool[0].astype(BF16),
        w_br_ssd[0].astype(BF16), w_out[0].astype(BF16), g1, norm_ffn_g[0][None], sh2, sc2, g2,
        w_router[0].astype(BF16), router_bias[0][None], ws_gu, ws_down[0].astype(BF16), cap)

    counts = cnt[0].astype(jnp.int32)
    nblk = (counts + m - 1) // m
    bends = jnp.cumsum(nblk)
    bstarts = bends - nblk
    nb_used = bends[-1]
    n_blocks = -(-(n * TOP_K) // m) + N_EXPERTS + 1
    b_eff = jnp.minimum(jnp.arange(n_blocks, dtype=jnp.int32), nb_used - 1)
    done = bends[None, :] <= b_eff[:, None]
    block_e = jnp.sum(done.astype(jnp.int32), axis=1)
    block_row = block_e * (cap // m) + b_eff - jnp.sum(jnp.where(done, nblk[None, :], 0), axis=1)

    xs = _padfill(counts, xs, cap)
    ytok = _experts(block_e, block_row, nb_used.reshape(1), xs,
                    we_gate[0], we_up[0], we_down[0], n * TOP_K)
    out = _combine(ytok, w8, base, g2, final_norm_g[None])
    return out.reshape(bsz, seq, D_MODEL)
```

```python
import functools

import jax
import jax.numpy as jnp
from jax import lax
from jax.experimental import pallas as pl
from jax.experimental.pallas import tpu as pltpu

F32 = jnp.float32
BF16 = jnp.bfloat16
HIGHEST = lax.Precision.HIGHEST

D_MODEL = 1024
D_SSD = 2048
HEADDIM = 64
N_HEADS = 32
N_GROUPS = 8
HEADS_PER_GROUP = N_HEADS // N_GROUPS
D_STATE = 128
CONV_K = 4
CHUNK = 128
GROUP_W = D_SSD // N_GROUPS
D_BC = N_GROUPS * D_STATE
POOL_WINDOWS = (2, 4, 8, 16)
POOL_GDIM = 256
N_EXPERTS = 64
TOP_K = 8
N_EGROUPS = 8
EXPERTS_PER_GROUP = 8
TOPK_GROUPS = 4
D_EXPERT = 256
D_SHARED = 256
ROUTED_SCALE = 2.5
MOE_BLOCK = 128
EPS = 1e-6

LANES = 128
SUBLANES = 8
ROW_TILES = D_MODEL // LANES
PACK_TILES = ROW_TILES // 2
PACK_W = PACK_TILES * LANES
EXPERT_BLOCK = 512
MIX_ISSUE_GROUPS = 8
BEFORE_STAGING = "before_staging"

PROJ_W = D_SSD + 2 * D_BC + D_MODEL
PROJ_CHUNK = 512

VMEM_LIMIT = 56 * 1024 * 1024


def _cparams(sem=("arbitrary",)):
    return pltpu.CompilerParams(dimension_semantics=sem, vmem_limit_bytes=VMEM_LIMIT)


def _full(shape):
    nd = len(shape)
    return pl.BlockSpec(shape, lambda *_: (0,) * nd)


def _silu(v):
    return v * jax.nn.sigmoid(v)


def _ada_kernel(c_ref, w_ref, b_ref, o_ref):
    c = c_ref[...]
    o_ref[...] = jnp.dot(_silu(c), w_ref[...], preferred_element_type=F32, precision=HIGHEST) + b_ref[...]


def _ada(c, w_ada, b_ada):
    n_out = w_ada.shape[1]
    tn = 1536
    c8 = jnp.broadcast_to(c, (SUBLANES, D_MODEL))
    out = pl.pallas_call(
        _ada_kernel,
        grid=(n_out // tn,),
        in_specs=[_full((SUBLANES, D_MODEL)),
                  pl.BlockSpec((D_MODEL, tn), lambda j: (0, j)),
                  pl.BlockSpec((1, tn), lambda j: (0, j))],
        out_specs=pl.BlockSpec((SUBLANES, tn), lambda j: (0, j)),
        out_shape=jax.ShapeDtypeStruct((SUBLANES, n_out), F32),
        compiler_params=_cparams(),
        name="ada",
    )(c8, w_ada, b_ada.reshape(1, n_out))
    return out[0:1]


def _inproj_kernel(x_ref, g_ref, sh_ref, sc_ref, wa_ref, wb_ref, wdt_ref, proj_ref, dt_ref, u_ref):
    x = x_ref[...]
    inv = lax.rsqrt(jnp.mean(x * x, axis=-1, keepdims=True) + EPS)
    u = x * inv * g_ref[...]
    u = u * (1.0 + sc_ref[...]) + sh_ref[...]
    ub = u.astype(BF16)
    u_ref[...] = ub
    wa = wa_ref.shape[1]
    for c0 in range(0, PROJ_W, PROJ_CHUNK):
        w = wa_ref[:, c0:c0 + PROJ_CHUNK] if c0 < wa else wb_ref[:, c0 - wa:c0 - wa + PROJ_CHUNK]
        proj_ref[:, c0:c0 + PROJ_CHUNK] = jnp.dot(ub, w, preferred_element_type=F32).astype(BF16)
    dt_ref[...] = jnp.dot(ub, wdt_ref[...], preferred_element_type=F32)


def _in_proj(x2, g, sh, sc, w_a, w_b, w_dt):
    n = x2.shape[0]
    tm = 256
    return pl.pallas_call(
        _inproj_kernel,
        grid=(n // tm,),
        in_specs=[pl.BlockSpec((tm, D_MODEL), lambda i: (i, 0)),
                  _full((1, D_MODEL)), _full((1, D_MODEL)), _full((1, D_MODEL)),
                  pl.BlockSpec(w_a.shape, lambda i: (0, 0), pipeline_mode=pl.Buffered(1)),
                  pl.BlockSpec(w_b.shape, lambda i: (0, 0), pipeline_mode=pl.Buffered(1)),
                  _full((D_MODEL, LANES))],
        out_specs=[pl.BlockSpec((tm, PROJ_W), lambda i: (i, 0)),
                   pl.BlockSpec((tm, LANES), lambda i: (i, 0)),
                   pl.BlockSpec((tm, D_MODEL), lambda i: (i, 0))],
        out_shape=[jax.ShapeDtypeStruct((n, PROJ_W), BF16),
                   jax.ShapeDtypeStruct((n, LANES), F32),
                   jax.ShapeDtypeStruct((n, D_MODEL), BF16)],
        compiler_params=_cparams(),
        name="in_proj",
    )(x2, g, sh, sc, w_a, w_b, w_dt)


def _conv_silu(cur_ref, ext_ref, w_ref, b_ref, out_ref, width, cw=512):
    t = cur_ref.shape[0]
    for c0 in range(0, width, cw):
        sl = slice(c0, c0 + cw)
        cur = cur_ref[:, sl].astype(F32)
        ext_ref[SUBLANES:, sl] = cur
        acc = cur * w_ref[CONV_K - 1:CONV_K, sl] + b_ref[:, sl]
        for s in range(1, CONV_K):
            acc = acc + ext_ref[pl.ds(SUBLANES - s, t), sl] * w_ref[CONV_K - 1 - s:CONV_K - s, sl]
        out_ref[:, sl] = _silu(acc).astype(out_ref.dtype)
        ext_ref[0:SUBLANES, sl] = cur[t - SUBLANES:t]


def _ssd_kernel(xs_ref, bc_ref, dt_ref, cwx_ref, cwbc_ref, cbx_ref, cbbc_ref, dtb_ref, alog_ref,
                dskip_ref, o_ref, tailx, tailbc, state, xc, bcc):
    q = CHUNK
    pair_w = 2 * HEADDIM

    @pl.when(pl.program_id(0) == 0)
    def _():
        tailx[...] = jnp.zeros_like(tailx)
        tailbc[...] = jnp.zeros_like(tailbc)
        state[...] = jnp.zeros_like(state)

    _conv_silu(xs_ref, tailx, cwx_ref, cbx_ref, xc, D_SSD)
    _conv_silu(bc_ref, tailbc, cwbc_ref, cbbc_ref, bcc, 2 * D_BC)

    v = dt_ref[...] + dtb_ref[...]
    dt = jnp.maximum(v, 0.0) + jnp.log(1.0 + jnp.exp(-jnp.abs(v)))
    a = dt * (-jnp.exp(alog_ref[...]))
    ri = lax.broadcasted_iota(jnp.int32, (q, q), 0)
    ci = lax.broadcasted_iota(jnp.int32, (q, q), 1)
    causal = ri >= ci
    a_cs = jnp.dot(causal.astype(F32), a, preferred_element_type=F32, precision=HIGHEST)
    a_cs_t = a_cs.T
    dt_t = dt.T
    first_half = ci < HEADDIM
    first_half_row = first_half[0:1, :]

    for g in range(N_GROUPS):
        b_g = bcc[:, g * D_STATE:(g + 1) * D_STATE]
        c_g = bcc[:, D_BC + g * D_STATE:D_BC + (g + 1) * D_STATE]
        cb = lax.dot_general(c_g, b_g, (((1,), (1,)), ((), ())), preferred_element_type=F32)
        b_t = b_g.astype(F32).T
        st_g = state[g]
        y_off = jnp.dot(c_g, st_g.astype(BF16), preferred_element_type=F32)
        for pi in range(HEADS_PER_GROUP // 2):
            h0 = g * HEADS_PER_GROUP + 2 * pi
            lanes = slice(h0 * HEADDIM, h0 * HEADDIM + pair_w)
            gl = slice(pi * pair_w, (pi + 1) * pair_w)
            ms, ws, cols, lasts = [], [], [], []
            for h in (h0, h0 + 1):
                col = jnp.broadcast_to(a_cs[:, h:h + 1], (q, q))
                row = a_cs_t[h:h + 1, :]
                dtrow = dt_t[h:h + 1, :]
                decay = jnp.exp(jnp.where(causal, col - row, -jnp.inf))
                ms.append((cb * decay * dtrow).astype(BF16))
                last = a_cs_t[h:h + 1, q - 1:q]
                ws.append((b_t * (jnp.exp(last - row) * dtrow)).astype(BF16))
                cols.append(col)
                lasts.append(last)
            xp = xc[:, lanes]
            zero = jnp.zeros_like(xp)
            x_bd = jnp.concatenate([jnp.where(first_half, xp, zero), jnp.where(first_half, zero, xp)], axis=0)
            y_p = jnp.dot(jnp.concatenate(ms, axis=1), x_bd, preferred_element_type=F32)
            y_p = y_p + y_off[:, gl] * jnp.exp(jnp.where(first_half, cols[0], cols[1]))
            o_ref[:, lanes] = (y_p + dskip_ref[:, lanes] * xp.astype(F32)).astype(BF16)
            s_new = jnp.dot(jnp.concatenate(ws, axis=1), x_bd, preferred_element_type=F32)
            carry = jnp.exp(jnp.where(first_half_row, lasts[0], lasts[1]))
            state[g, :, gl] = st_g[:, gl] * carry + s_new


def _ssd(proj, dt_raw, conv_w, conv_b, dt_bias, a_log, d_skip):
    n = proj.shape[0]
    t = CHUNK
    return pl.pallas_call(
        _ssd_kernel,
        grid=(n // t,),
        in_specs=[pl.BlockSpec((t, D_SSD), lambda i: (i, 0)),
                  pl.BlockSpec((t, 2 * D_BC), lambda i: (i, 1)),
                  pl.BlockSpec((t, LANES), lambda i: (i, 0)),
                  pl.BlockSpec((CONV_K, D_SSD), lambda i: (0, 0)),
                  pl.BlockSpec((CONV_K, 2 * D_BC), lambda i: (0, 1)),
                  pl.BlockSpec((1, D_SSD), lambda i: (0, 0)),
                  pl.BlockSpec((1, 2 * D_BC), lambda i: (0, 1)),
                  _full((1, LANES)), _full((1, LANES)),
                  _full((1, D_SSD))],
        out_specs=pl.BlockSpec((t, D_SSD), lambda i: (i, 0)),
        out_shape=jax.ShapeDtypeStruct((n, D_SSD), BF16),
        scratch_shapes=[pltpu.VMEM((SUBLANES + t, D_SSD), F32),
                        pltpu.VMEM((SUBLANES + t, 2 * D_BC), F32),
                        pltpu.VMEM((N_GROUPS, D_STATE, GROUP_W), F32),
                        pltpu.VMEM((t, D_SSD), BF16),
                        pltpu.VMEM((t, 2 * D_BC), BF16)],
        compiler_params=_cparams(),
        name="ssd",
    )(proj, proj, dt_raw, conv_w, conv_w, conv_b, conv_b, dt_bias, a_log, d_skip)


PAD_ID = 0xFFFFFFFF
TOK_ROW = PACK_TILES
MASK_ROWS = (PACK_TILES + 1, PACK_TILES + 2)


def _tile_rows(ref, first_row):
    return ref.at[pl.ds(pl.multiple_of(first_row * SUBLANES, SUBLANES), SUBLANES)]


def _mix_kernel(xp_ref, u_ref, ypre_ref, x_ref, wz_ref, wgt_ref, sng_ref, poolw_ref, pscale_ref, wbp_ref, wbs_ref, wout_ref,
                g1_ref, ng_ref, sh2_ref, sc2_ref, g2_ref, wr_ref, rb_ref, wsgu_ref, wsd_ref, zero_ref,
                base_ref, ws_ref, cnt_ref, xs_ref, ptail, run, stage, dvm, dsm, sem, sem_s, *, n_tiles, cap):
    t = x_ref.shape[0]
    i = pl.program_id(0)
    slot = lax.rem(i, 2)

    @pl.when(i == 0)
    def _():
        ptail[...] = jnp.zeros_like(ptail)
        run[...] = jnp.zeros_like(run)
        stage[...] = jnp.zeros_like(stage)

    def issue_rows(s, group, after=None):
        zero = 0
        if after is not None:
            probe = jnp.max(jnp.abs(after[0:SUBLANES, 0:min(LANES, after.shape[1])])).astype(jnp.int32)
            zero = probe * zero_ref[0]
        per = t // MIX_ISSUE_GROUPS
        for tt in range(group * per, (group + 1) * per):
            for k in range(TOP_K):
                pltpu.make_async_copy(stage.at[s, pl.ds(tt * SUBLANES, SUBLANES)],
                                      _tile_rows(xs_ref, dsm[s, k, tt] + zero), sem.at[s]).start()

    def drain_rows(s):
        for _ in range(TOP_K):
            pltpu.make_async_copy(stage.at[s], xs_ref.at[pl.ds(0, t * SUBLANES)], sem.at[s]).wait()

    def tile(s, milestone=None):
        _mix_tile_body(xp_ref, u_ref, ypre_ref, x_ref, wz_ref, wgt_ref, sng_ref, poolw_ref, pscale_ref, wbp_ref, wbs_ref, wout_ref,
                       g1_ref, ng_ref, sh2_ref, sc2_ref, g2_ref, wr_ref, rb_ref, wsgu_ref, wsd_ref,
                       base_ref, ws_ref, cnt_ref, ptail, run, stage, dvm, dsm, sem_s, i, s, cap, milestone)

    @pl.when(i == 0)
    def _():
        tile(0)

    def milestone(group, value):
        if group == BEFORE_STAGING:
            @pl.when(i >= 2)
            def _():
                drain_rows(slot)
        else:
            issue_rows(1 - slot, group, value)

    @pl.when((i >= 1) & (i < n_tiles))
    def _():
        issue_rows(1 - slot, 0)
        tile(slot, milestone)

    @pl.when(i == n_tiles)
    def _():
        for group in range(MIX_ISSUE_GROUPS):
            issue_rows(1 - slot, group)
        if n_tiles >= 2:
            drain_rows(slot)
        drain_rows(1 - slot)


def _mix_tile_body(xp_ref, u_ref, ypre_ref, x_ref, wz_ref, wgt_ref, sng_ref, poolw_ref, pscale_ref, wbp_ref, wbs_ref, wout_ref,
                   g1_ref, ng_ref, sh2_ref, sc2_ref, g2_ref, wr_ref, rb_ref, wsgu_ref, wsd_ref,
                   base_ref, ws_ref, cnt_ref, ptail, run, stage, dvm, dsm, sem_s, i, slot, cap, milestone):
    t = x_ref.shape[0]
    if milestone is None:
        milestone = lambda group, value: None

    xp = xp_ref[...].astype(F32)
    ext = jnp.concatenate([ptail[...], xp], axis=0)
    ptail[...] = xp[t - 2 * SUBLANES:t]
    pos = (lax.broadcasted_iota(jnp.int32, (t, 1), 0) + (i * t + 1)).astype(F32)
    pooled = []
    for gi, w in enumerate(POOL_WINDOWS):
        sl = slice(gi * POOL_GDIM, (gi + 1) * POOL_GDIM)
        e = ext[:, sl]
        s = e
        span = 1
        while span < w:
            s = s + pltpu.roll(s, span, axis=0)
            span *= 2
        win = s[2 * SUBLANES:]
        mean = win / jnp.minimum(pos, float(w))
        pg = (mean - xp[:, sl]).astype(BF16)
        mixed = jnp.dot(pg, poolw_ref[gi], preferred_element_type=F32)
        pooled.append(mixed * pscale_ref[:, sl])
    pooled = jnp.concatenate(pooled, axis=1).astype(BF16)
    y_pool = jnp.dot(pooled, wbp_ref[...], preferred_element_type=F32)
    milestone(1, y_pool)
    ub = u_ref[...]
    z = jnp.dot(ub, wz_ref[...], preferred_element_type=F32)
    yn = []
    for g in range(N_GROUPS):
        sl = slice(g * GROUP_W, (g + 1) * GROUP_W)
        yg = ypre_ref[:, sl].astype(F32) * _silu(z[:, sl])
        inv_g = lax.rsqrt(jnp.mean(yg * yg, axis=-1, keepdims=True) + EPS)
        yn.append((yg * inv_g * sng_ref[:, sl]).astype(BF16))
    y_ssd = jnp.dot(jnp.concatenate(yn, axis=1), wbs_ref[...], preferred_element_type=F32)
    milestone(2, y_ssd)
    gates = jnp.dot(ub, wgt_ref[...], preferred_element_type=F32)
    g_ssd = jax.nn.sigmoid(gates[:, :D_MODEL])
    g_pool = jax.nn.sigmoid(gates[:, D_MODEL:])
    mixed = (g_ssd * y_ssd + g_pool * y_pool).astype(BF16)
    h = x_ref[...] + g1_ref[...] * jnp.dot(mixed, wout_ref[...], preferred_element_type=F32)
    milestone(3, h)

    inv = lax.rsqrt(jnp.mean(h * h, axis=-1, keepdims=True) + EPS)
    u2 = h * inv * ng_ref[...]
    u2 = u2 * (1.0 + sc2_ref[...]) + sh2_ref[...]
    u2b = u2.astype(BF16)
    u2r = u2b.astype(F32)
    packed = []
    for j in range(PACK_TILES):
        lo = lax.bitcast_convert_type(u2r[:, j * LANES:(j + 1) * LANES], jnp.uint32)
        hi = lax.bitcast_convert_type(u2r[:, (j + PACK_TILES) * LANES:(j + PACK_TILES + 1) * LANES], jnp.uint32)
        packed.append((lo >> 16) | (hi & jnp.uint32(0xFFFF0000)))

    hs = jnp.dot(u2b, wsgu_ref[...], preferred_element_type=F32)
    milestone(4, hs)
    act = (_silu(hs[:, :D_SHARED]) * hs[:, D_SHARED:]).astype(BF16)
    shared = jnp.dot(act, wsd_ref[...], preferred_element_type=F32)
    milestone(5, shared)
    base_ref[...] = h + g2_ref[...] * shared

    logits = jnp.dot(u2b, wr_ref[...], preferred_element_type=F32)
    scores = jax.nn.sigmoid(logits)
    choice = scores + rb_ref[...]
    lane = lax.broadcasted_iota(jnp.int32, (t, N_EXPERTS), 1)
    lane_f = lane.astype(F32)
    lane_grp = lane // EXPERTS_PER_GROUP
    neg = -jnp.inf
    gscore = []
    for g in range(N_EGROUPS):
        vg = jnp.where(lane_grp == g, choice, neg)
        m1 = jnp.max(vg, axis=-1, keepdims=True)
        i1 = jnp.min(jnp.where(vg == m1, lane_f, float(N_EXPERTS)), axis=-1, keepdims=True)
        m2 = jnp.max(jnp.where(lane_f == i1, neg, vg), axis=-1, keepdims=True)
        gscore.append(m1 + m2)
    milestone(6, gscore[-1])
    gmask = jnp.zeros((t, N_EXPERTS), jnp.bool_)
    for g in range(N_EGROUPS):
        rank = jnp.zeros((t, 1), F32)
        for g2 in range(N_EGROUPS):
            if g2 == g:
                continue
            better = (gscore[g2] > gscore[g]) | ((gscore[g2] == gscore[g]) & (g2 < g))
            rank = rank + better.astype(F32)
        gmask = gmask | ((rank < float(TOPK_GROUPS)) & (lane_grp == g))
    work = jnp.where(gmask, choice, neg)
    sel = jnp.zeros((t, N_EXPERTS), jnp.bool_)
    onehots, idxs, sks = [], [], []
    for k in range(TOP_K):
        m = jnp.max(work, axis=-1, keepdims=True)
        idx = jnp.min(jnp.where(work == m, lane_f, float(N_EXPERTS)), axis=-1, keepdims=True)
        oh = lane_f == idx
        onehots.append(oh)
        idxs.append(idx)
        sks.append(jnp.sum(jnp.where(oh, scores, 0.0), axis=-1, keepdims=True))
        sel = sel | oh
        work = jnp.where(oh, neg, work)
    denom = sks[0]
    for k in range(1, TOP_K):
        denom = denom + sks[k]
    milestone(7, denom)

    ri = lax.broadcasted_iota(jnp.int32, (t, t), 0)
    ci = lax.broadcasted_iota(jnp.int32, (t, t), 1)
    before = (ri > ci).astype(BF16)
    sel_f = jnp.where(sel, 1.0, 0.0)
    sel_b = sel_f.astype(BF16)
    pos_tile = jnp.dot(before, sel_b, preferred_element_type=F32) + run[...]
    er = lax.broadcasted_iota(jnp.int32, (N_EXPERTS, N_EXPERTS), 0)
    ec = lax.broadcasted_iota(jnp.int32, (N_EXPERTS, N_EXPERTS), 1)
    rank_tile = jnp.dot(sel_b, (er < ec).astype(BF16), preferred_element_type=F32)
    lane128 = lax.broadcasted_iota(jnp.int32, (t, LANES), 1)
    k_iota = lax.broadcasted_iota(jnp.int32, (t, TOP_K), 1).astype(F32)
    dest = jnp.zeros((t, LANES), F32)
    w_sorted = jnp.zeros((t, TOP_K), F32)
    for k in range(TOP_K):
        pk = jnp.sum(jnp.where(onehots[k], pos_tile, 0.0), axis=-1, keepdims=True)
        rk = jnp.sum(jnp.where(onehots[k], rank_tile, 0.0), axis=-1, keepdims=True)
        dest = jnp.where(lane128 == k, idxs[k] * float(cap) + pk, dest)
        w_sorted = jnp.where(k_iota == rk, sks[k] / denom * ROUTED_SCALE, w_sorted)
    ws_ref[...] = w_sorted
    total = run[...] + jnp.sum(sel_f, axis=0, keepdims=True)
    run[...] = total
    cnt_ref[...] = total

    milestone(BEFORE_STAGING, None)
    for j in range(PACK_TILES):
        stage[slot, pl.ds(j, t, stride=SUBLANES), :] = packed[j]
    tok = lax.broadcasted_iota(jnp.int32, (t, LANES), 0) + i * t
    stage[slot, pl.ds(TOK_ROW, t, stride=SUBLANES), :] = tok.astype(jnp.uint32)
    bit = jnp.where(sel, jnp.left_shift(1, lane & 15), 0).astype(F32)
    words = []
    for q in range(N_EXPERTS // 16):
        part = jnp.sum(jnp.where((lane >> 4) == q, bit, 0.0), axis=-1, keepdims=True)
        words.append(part.astype(jnp.int32).astype(jnp.uint32))
    for w, row in enumerate(MASK_ROWS):
        word = words[2 * w] | (words[2 * w + 1] << 16)
        stage[slot, pl.ds(row, t, stride=SUBLANES), :] = jnp.broadcast_to(word, (t, LANES))

    dvm[...] = dest.T[0:TOP_K, :].astype(jnp.int32)
    smem_copy = pltpu.make_async_copy(dvm, dsm.at[slot], sem_s)
    smem_copy.start()
    smem_copy.wait()


def _mix(proj, u, ypre, x2, w_z, w_gates, ssd_ng, pool_w, pool_scale, w_br_pool, w_br_ssd, w_out, g1, ng, sh2, sc2,
         g2, w_router, router_bias, ws_gu, ws_down, cap):
    n = x2.shape[0]
    t = 256
    n_tiles = n // t
    xp_blk = (D_SSD + 2 * D_BC) // D_MODEL
    const = lambda shape: pl.BlockSpec(shape, lambda i: (0,) * len(shape), pipeline_mode=pl.Buffered(1))
    tile = lambda i: jnp.minimum(i, n_tiles - 1)
    row = lambda i: (tile(i), 0)
    return pl.pallas_call(
        functools.partial(_mix_kernel, n_tiles=n_tiles, cap=cap),
        grid=(n_tiles + 1,),
        in_specs=[pl.BlockSpec((t, D_MODEL), lambda i: (tile(i), xp_blk)),
                  pl.BlockSpec((t, D_MODEL), row),
                  pl.BlockSpec((t, D_SSD), row),
                  pl.BlockSpec((t, D_MODEL), row),
                  const((D_MODEL, D_SSD)),
                  const((D_MODEL, 2 * D_MODEL)),
                  _full((1, D_SSD)),
                  const((len(POOL_WINDOWS), POOL_GDIM, POOL_GDIM)),
                  _full((1, D_MODEL)),
                  const((D_MODEL, D_MODEL)),
                  const((D_SSD, D_MODEL)),
                  const((D_MODEL, D_MODEL)),
                  _full((1, D_MODEL)), _full((1, D_MODEL)), _full((1, D_MODEL)), _full((1, D_MODEL)),
                  _full((1, D_MODEL)),
                  _full((D_MODEL, N_EXPERTS)),
                  _full((1, N_EXPERTS)),
                  const((D_MODEL, 2 * D_SHARED)),
                  const((D_SHARED, D_MODEL)),
                  pl.BlockSpec(memory_space=pltpu.SMEM)],
        out_specs=[pl.BlockSpec((t, D_MODEL), row),
                   pl.BlockSpec((t, TOP_K), row),
                   _full((1, N_EXPERTS)),
                   pl.BlockSpec(memory_space=pl.ANY)],
        out_shape=[jax.ShapeDtypeStruct((n, D_MODEL), F32),
                   jax.ShapeDtypeStruct((n, TOP_K), F32),
                   jax.ShapeDtypeStruct((1, N_EXPERTS), F32),
                   jax.ShapeDtypeStruct((N_EXPERTS * cap * SUBLANES, LANES), jnp.uint32)],
        scratch_shapes=[pltpu.VMEM((2 * SUBLANES, D_MODEL), F32),
                        pltpu.VMEM((1, N_EXPERTS), F32),
                        pltpu.VMEM((2, t * SUBLANES, LANES), jnp.uint32),
                        pltpu.VMEM((TOP_K, t), jnp.int32),
                        pltpu.SMEM((2, TOP_K, t), jnp.int32),
                        pltpu.SemaphoreType.DMA((2,)),
                        pltpu.SemaphoreType.DMA(())],
        compiler_params=_cparams(),
        name="mix",
    )(proj, u, ypre, x2, w_z, w_gates, ssd_ng, pool_w, pool_scale, w_br_pool, w_br_ssd, w_out, g1, ng, sh2, sc2, g2,
      w_router, router_bias, ws_gu, ws_down, jnp.zeros((1,), jnp.int32))


def _padfill_kernel(cnt_ref, xs_in_ref, xs_ref, padbuf, sem, *, cap):
    del xs_in_ref
    sub = lax.broadcasted_iota(jnp.int32, padbuf.shape, 0) & (SUBLANES - 1)
    padbuf[...] = jnp.where(sub == TOK_ROW, jnp.uint32(PAD_ID), jnp.uint32(0))
    sizes = [1 << bit for bit in range(EXPERT_BLOCK.bit_length() - 1)]

    def pad_copies(e, fn):
        c = cnt_ref[e]
        n_pad = (EXPERT_BLOCK - (c & (EXPERT_BLOCK - 1))) & (EXPERT_BLOCK - 1)
        first = e * cap + c
        for size in sizes:
            @pl.when((n_pad & size) != 0)
            def _():
                start = first + (n_pad & (size - 1))
                dst = xs_ref.at[pl.ds(pl.multiple_of(start * SUBLANES, SUBLANES), size * SUBLANES)]
                fn(pltpu.make_async_copy(padbuf.at[pl.ds(0, size * SUBLANES)], dst, sem))

    def issue(e, carry):
        pad_copies(e, lambda cp: cp.start())
        return carry

    lax.fori_loop(0, N_EXPERTS, issue, 0)

    def drain(e, carry):
        pad_copies(e, lambda cp: cp.wait())
        return carry

    lax.fori_loop(0, N_EXPERTS, drain, 0)


def _padfill(counts, xs, cap):
    return pl.pallas_call(
        functools.partial(_padfill_kernel, cap=cap),
        grid_spec=pltpu.PrefetchScalarGridSpec(
            num_scalar_prefetch=1,
            grid=(1,),
            in_specs=[pl.BlockSpec(memory_space=pl.ANY)],
            out_specs=pl.BlockSpec(memory_space=pl.ANY),
            scratch_shapes=[pltpu.VMEM((EXPERT_BLOCK // 2 * SUBLANES, LANES), jnp.uint32),
                            pltpu.SemaphoreType.DMA(())]),
        out_shape=jax.ShapeDtypeStruct(xs.shape, xs.dtype),
        input_output_aliases={1: 0},
        compiler_params=_cparams(),
        name="padfill",
    )(counts, xs)


def _expert_kernel(be_ref, br_ref, nb_ref, x_ref, wg_ref, wu_ref, wd_ref, ytok_ref, ybuf0, ybuf1, idv, ids0, ids1,
                   wgb, wub, wdb, sem, sem_ids, *, n_ids):
    del br_ref
    m = EXPERT_BLOCK
    b = pl.program_id(0)
    nb = nb_ref[0]
    ybufs, idss = (ybuf0, ybuf1), (ids0, ids1)
    n_phases = m // LANES
    half = D_MODEL // 2

    @pl.when((b == 0) | (be_ref[b] != be_ref[jnp.maximum(b - 1, 0)]))
    def _():
        wgb[...] = wg_ref[0].astype(BF16)
        wub[...] = wu_ref[0].astype(BF16)
        wdb[...] = wd_ref[0].astype(BF16)

    def issue(s, a0):
        for col in range(LANES):
            pltpu.make_async_copy(_tile_rows(ybufs[s], a0 * LANES + col),
                                  _tile_rows(ytok_ref, idss[s][a0, col]), sem.at[s]).start()

    def step(s, prev):
        tokrep = x_ref[pl.ds(TOK_ROW, m, stride=SUBLANES), :]
        e = be_ref[b]
        below_lo = jnp.where(e >= 32, -1, (1 << jnp.minimum(e, 31)) - 1).astype(jnp.uint32)
        below_hi = jnp.where(e >= 32, (1 << jnp.maximum(e - 32, 0)) - 1, 0).astype(jnp.uint32)
        rank = (lax.population_count(x_ref[pl.ds(MASK_ROWS[0], m, stride=SUBLANES), :] & below_lo)
                + lax.population_count(x_ref[pl.ds(MASK_ROWS[1], m, stride=SUBLANES), :] & below_hi))
        r = lax.broadcasted_iota(jnp.int32, (m, LANES), 0)
        ln = lax.broadcasted_iota(jnp.int32, (m, LANES), 1)
        n_tok = n_ids // TOP_K
        idi = jnp.where(tokrep == jnp.uint32(PAD_ID), n_ids + r,
                        rank.astype(jnp.int32) * n_tok + tokrep.astype(jnp.int32))
        diag = jnp.where((r & (LANES - 1)) == ln, idi, 0).astype(F32)
        idv[...] = jnp.sum(diag.reshape(m // LANES, LANES, LANES), axis=1).astype(jnp.int32)
        ids_copy = pltpu.make_async_copy(idv, idss[s], sem_ids)
        ids_copy.start()

        lo, hi = [], []
        for j in range(PACK_TILES):
            w = x_ref[pl.ds(j, m, stride=SUBLANES), :]
            lo.append(lax.bitcast_convert_type(w << 16, F32).astype(BF16))
            hi.append(lax.bitcast_convert_type(w & jnp.uint32(0xFFFF0000), F32).astype(BF16))
        x = jnp.concatenate(lo + hi, axis=1)
        phases = iter(range(n_phases))
        if prev is not None:
            issue(prev, next(phases))
        gate = jnp.dot(x, wgb[...], preferred_element_type=F32)
        if prev is not None:
            issue(prev, next(phases))
        up = jnp.dot(x, wub[...], preferred_element_type=F32)
        hid = (_silu(gate) * up).astype(BF16)

        def drain(which):
            pltpu.make_async_copy(ybufs[which], ytok_ref.at[pl.ds(0, m * SUBLANES)], sem.at[which]).wait()

        if prev is not None:
            @pl.when(b >= 2)
            def _():
                drain(s)
        for c0 in (0, half):
            if prev is not None:
                issue(prev, next(phases))
            y = jnp.dot(hid, wdb[:, c0:c0 + half], preferred_element_type=F32)
            for j in range(half // LANES):
                ybufs[s][pl.ds(c0 // LANES + j, m, stride=SUBLANES), :] = y[:, j * LANES:(j + 1) * LANES]
        ids_copy.wait()
        if prev is not None:
            for a0 in phases:
                issue(prev, a0)

            @pl.when(b == nb)
            def _():
                drain(prev)

    @pl.when(b == 0)
    def _():
        step(0, None)

    for s in range(2):
        @pl.when((b >= 1) & (b <= nb) & (lax.rem(b, 2) == s))
        def _():
            step(s, 1 - s)


def _experts(block_e, block_row, nb_used, xs, we_gate, we_up, we_down, n_ids):
    n_blocks = block_e.shape[0]
    m = EXPERT_BLOCK
    rows = m * SUBLANES
    return pl.pallas_call(
        functools.partial(_expert_kernel, n_ids=n_ids),
        grid_spec=pltpu.PrefetchScalarGridSpec(
            num_scalar_prefetch=3,
            grid=(n_blocks,),
            in_specs=[pl.BlockSpec((rows, LANES), lambda b, be, br, nb: (br[b], 0)),
                      pl.BlockSpec((1, D_MODEL, D_EXPERT), lambda b, be, br, nb: (be[b], 0, 0)),
                      pl.BlockSpec((1, D_MODEL, D_EXPERT), lambda b, be, br, nb: (be[b], 0, 0)),
                      pl.BlockSpec((1, D_EXPERT, D_MODEL), lambda b, be, br, nb: (be[b], 0, 0))],
            out_specs=pl.BlockSpec(memory_space=pl.ANY),
            scratch_shapes=[pltpu.VMEM((rows, LANES), F32),
                            pltpu.VMEM((rows, LANES), F32),
                            pltpu.VMEM((m // LANES, LANES), jnp.int32),
                            pltpu.SMEM((m // LANES, LANES), jnp.int32),
                            pltpu.SMEM((m // LANES, LANES), jnp.int32),
                            pltpu.VMEM((D_MODEL, D_EXPERT), BF16),
                            pltpu.VMEM((D_MODEL, D_EXPERT), BF16),
                            pltpu.VMEM((D_EXPERT, D_MODEL), BF16),
                            pltpu.SemaphoreType.DMA((2,)),
                            pltpu.SemaphoreType.DMA(())]),
        out_shape=jax.ShapeDtypeStruct(((n_ids + m) * SUBLANES, LANES), F32),
        compiler_params=_cparams(),
        name="experts",
    )(block_e, block_row, nb_used, xs, we_gate, we_up, we_down)


def _combine_kernel(*refs):
    y_refs = refs[:TOP_K]
    w8_ref, base_ref, g2_ref, fg_ref, o_ref = refs[TOP_K:]
    t7 = base_ref.shape[0]
    w8 = w8_ref[...]
    parts = []
    for j in range(ROW_TILES):
        acc = jnp.zeros((t7, LANES), F32)
        for k in range(TOP_K):
            acc = acc + w8[:, k:k + 1] * y_refs[k][pl.ds(j, t7, stride=SUBLANES), :]
        parts.append(acc)
    routed = jnp.concatenate(parts, axis=1)
    h = base_ref[...] + g2_ref[...] * routed
    inv = lax.rsqrt(jnp.mean(h * h, axis=-1, keepdims=True) + EPS)
    o_ref[...] = h * inv * fg_ref[...]


def _combine(ytok, w8, base, g2, fg):
    n = base.shape[0]
    t7 = 128
    y_specs = [pl.BlockSpec((t7 * SUBLANES, LANES), functools.partial(lambda i, k: (k * (n // t7) + i, 0), k=k))
               for k in range(TOP_K)]
    return pl.pallas_call(
        _combine_kernel,
        grid=(n // t7,),
        in_specs=y_specs + [
                  pl.BlockSpec((t7, TOP_K), lambda i: (i, 0)),
                  pl.BlockSpec((t7, D_MODEL), lambda i: (i, 0)),
                  _full((1, D_MODEL)), _full((1, D_MODEL))],
        out_specs=pl.BlockSpec((t7, D_MODEL), lambda i: (i, 0)),
        out_shape=jax.ShapeDtypeStruct((n, D_MODEL), F32),
        compiler_params=_cparams(),
        name="combine",
    )(*([ytok] * TOP_K), w8, base, g2, fg)


def kernel(x, c, w_ada, b_ada, norm_mix_g, w_in, conv_w, conv_b, dt_bias, A_log, D_skip, ssd_norm_g, pool_w,
           pool_scale, w_br_ssd, w_br_pool, w_out, norm_ffn_g, w_router, router_bias, we_gate, we_up, we_down,
           ws_gate, ws_up, ws_down, final_norm_g):
    bsz, seq, _ = x.shape
    assert bsz == 1 and w_ada.shape[0] == 1
    n = seq
    h2 = x.reshape(n, D_MODEL)

    mod = _ada(c, w_ada[0], b_ada[0])
    sh1, sc1, g1, sh2, sc2, g2 = [mod[:, k * D_MODEL:(k + 1) * D_MODEL] for k in range(6)]

    wi = w_in[0]
    o_xbc, o_dt = D_SSD, D_SSD + D_SSD + 2 * D_BC
    o_pool = o_dt + N_HEADS
    o_gates = o_pool + D_MODEL
    w_dt = jnp.pad(wi[:, o_dt:o_pool], ((0, 0), (0, LANES - N_HEADS))).astype(BF16)
    proj, dt_raw, u = _in_proj(h2, norm_mix_g[0][None], sh1, sc1, wi[:, o_xbc:o_dt].astype(BF16),
                               wi[:, o_pool:o_gates].astype(BF16), w_dt)

    pad_h = lambda v: jnp.pad(v[None], ((0, 0), (0, LANES - N_HEADS)))
    ypre = _ssd(proj, dt_raw, conv_w[0], conv_b[0][None], pad_h(dt_bias[0]), pad_h(A_log[0]),
                jnp.repeat(D_skip[0], HEADDIM)[None])

    ws_gu = jnp.concatenate([ws_gate[0], ws_up[0]], axis=1).astype(BF16)
    m = EXPERT_BLOCK
    cap = -(-n // m) * m
    base, w8, cnt, xs = _mix(
        proj, u, ypre, h2, wi[:, :o_xbc].astype(BF16), wi[:, o_gates:].astype(BF16), ssd_norm_g[0][None],
        pool_w[0].astype(BF16), pool_scale[0][None], w_br_pool[0].astype(BF16),
        w_br_ssd[0].astype(BF16), w_out[0].astype(BF16), g1, norm_ffn_g[0][None], sh2, sc2, g2,
        w_router[0].astype(BF16), router_bias[0][None], ws_gu, ws_down[0].astype(BF16), cap)

    counts = cnt[0].astype(jnp.int32)
    nblk = (counts + m - 1) // m
    bends = jnp.cumsum(nblk)
    bstarts = bends - nblk
    nb_used = bends[-1]
    n_blocks = -(-(n * TOP_K) // m) + N_EXPERTS + 1
    b_eff = jnp.minimum(jnp.arange(n_blocks, dtype=jnp.int32), nb_used - 1)
    done = bends[None, :] <= b_eff[:, None]
    block_e = jnp.sum(done.astype(jnp.int32), axis=1)
    block_row = block_e * (cap // m) + b_eff - jnp.sum(jnp.where(done, nblk[None, :], 0), axis=1)

    xs = _padfill(counts, xs, cap)
    ytok = _experts(block_e, block_row, nb_used.reshape(1), xs,
                    we_gate[0], we_up[0], we_down[0], n * TOP_K)
    out = _combine(ytok, w8, base, g2, final_norm_g[None])
    return out.reshape(bsz, seq, D_MODEL)
```

```python
import functools

import jax
import jax.numpy as jnp
from jax import lax
from jax.experimental import pallas as pl
from jax.experimental.pallas import tpu as pltpu

F32 = jnp.float32
BF16 = jnp.bfloat16
HIGHEST = lax.Precision.HIGHEST

D_MODEL = 1024
D_SSD = 2048
HEADDIM = 64
N_HEADS = 32
N_GROUPS = 8
HEADS_PER_GROUP = N_HEADS // N_GROUPS
D_STATE = 128
CONV_K = 4
CHUNK = 128
GROUP_W = D_SSD // N_GROUPS
D_BC = N_GROUPS * D_STATE
POOL_WINDOWS = (2, 4, 8, 16)
POOL_GDIM = 256
N_EXPERTS = 64
TOP_K = 8
N_EGROUPS = 8
EXPERTS_PER_GROUP = 8
TOPK_GROUPS = 4
D_EXPERT = 256
D_SHARED = 256
ROUTED_SCALE = 2.5
MOE_BLOCK = 128
EPS = 1e-6

LANES = 128
SUBLANES = 8
ROW_TILES = D_MODEL // LANES
PACK_TILES = ROW_TILES // 2
PACK_W = PACK_TILES * LANES
EXPERT_BLOCK = 512
MIX_ISSUE_GROUPS = 16
BEFORE_STAGING = "before_staging"

PROJ_W = D_SSD + 2 * D_BC + D_MODEL
PROJ_CHUNK = 512

VMEM_LIMIT = 56 * 1024 * 1024


def _cparams(sem=("arbitrary",)):
    return pltpu.CompilerParams(dimension_semantics=sem, vmem_limit_bytes=VMEM_LIMIT)


def _full(shape):
    nd = len(shape)
    return pl.BlockSpec(shape, lambda *_: (0,) * nd)


def _silu(v):
    return v * jax.nn.sigmoid(v)


def _ada_kernel(c_ref, w_ref, b_ref, o_ref):
    c = c_ref[...]
    o_ref[...] = jnp.dot(_silu(c), w_ref[...], preferred_element_type=F32, precision=HIGHEST) + b_ref[...]


def _ada(c, w_ada, b_ada):
    n_out = w_ada.shape[1]
    tn = 1536
    c8 = jnp.broadcast_to(c, (SUBLANES, D_MODEL))
    out = pl.pallas_call(
        _ada_kernel,
        grid=(n_out // tn,),
        in_specs=[_full((SUBLANES, D_MODEL)),
                  pl.BlockSpec((D_MODEL, tn), lambda j: (0, j)),
                  pl.BlockSpec((1, tn), lambda j: (0, j))],
        out_specs=pl.BlockSpec((SUBLANES, tn), lambda j: (0, j)),
        out_shape=jax.ShapeDtypeStruct((SUBLANES, n_out), F32),
        compiler_params=_cparams(),
        name="ada",
    )(c8, w_ada, b_ada.reshape(1, n_out))
    return out[0:1]


def _inproj_kernel(x_ref, g_ref, sh_ref, sc_ref, wa_ref, wb_ref, wdt_ref, proj_ref, dt_ref, u_ref):
    x = x_ref[...]
    inv = lax.rsqrt(jnp.mean(x * x, axis=-1, keepdims=True) + EPS)
    u = x * inv * g_ref[...]
    u = u * (1.0 + sc_ref[...]) + sh_ref[...]
    ub = u.astype(BF16)
    u_ref[...] = ub
    wa = wa_ref.shape[1]
    for c0 in range(0, PROJ_W, PROJ_CHUNK):
        w = wa_ref[:, c0:c0 + PROJ_CHUNK] if c0 < wa else wb_ref[:, c0 - wa:c0 - wa + PROJ_CHUNK]
        proj_ref[:, c0:c0 + PROJ_CHUNK] = jnp.dot(ub, w, preferred_element_type=F32).astype(BF16)
    dt_ref[...] = jnp.dot(ub, wdt_ref[...], preferred_element_type=F32)


def _in_proj(x2, g, sh, sc, w_a, w_b, w_dt):
    n = x2.shape[0]
    tm = 256
    return pl.pallas_call(
        _inproj_kernel,
        grid=(n // tm,),
        in_specs=[pl.BlockSpec((tm, D_MODEL), lambda i: (i, 0)),
                  _full((1, D_MODEL)), _full((1, D_MODEL)), _full((1, D_MODEL)),
                  pl.BlockSpec(w_a.shape, lambda i: (0, 0), pipeline_mode=pl.Buffered(1)),
                  pl.BlockSpec(w_b.shape, lambda i: (0, 0), pipeline_mode=pl.Buffered(1)),
                  _full((D_MODEL, LANES))],
        out_specs=[pl.BlockSpec((tm, PROJ_W), lambda i: (i, 0)),
                   pl.BlockSpec((tm, LANES), lambda i: (i, 0)),
                   pl.BlockSpec((tm, D_MODEL), lambda i: (i, 0))],
        out_shape=[jax.ShapeDtypeStruct((n, PROJ_W), BF16),
                   jax.ShapeDtypeStruct((n, LANES), F32),
                   jax.ShapeDtypeStruct((n, D_MODEL), BF16)],
        compiler_params=_cparams(),
        name="in_proj",
    )(x2, g, sh, sc, w_a, w_b, w_dt)


def _conv_silu(cur_ref, ext_ref, w_ref, b_ref, out_ref, width, cw=512):
    t = cur_ref.shape[0]
    for c0 in range(0, width, cw):
        sl = slice(c0, c0 + cw)
        cur = cur_ref[:, sl].astype(F32)
        ext_ref[SUBLANES:, sl] = cur
        acc = cur * w_ref[CONV_K - 1:CONV_K, sl] + b_ref[:, sl]
        for s in range(1, CONV_K):
            acc = acc + ext_ref[pl.ds(SUBLANES - s, t), sl] * w_ref[CONV_K - 1 - s:CONV_K - s, sl]
        out_ref[:, sl] = _silu(acc).astype(out_ref.dtype)
        ext_ref[0:SUBLANES, sl] = cur[t - SUBLANES:t]


def _ssd_kernel(xs_ref, bc_ref, dt_ref, cwx_ref, cwbc_ref, cbx_ref, cbbc_ref, dtb_ref, alog_ref,
                dskip_ref, o_ref, tailx, tailbc, state, xc, bcc):
    q = CHUNK
    pair_w = 2 * HEADDIM

    @pl.when(pl.program_id(0) == 0)
    def _():
        tailx[...] = jnp.zeros_like(tailx)
        tailbc[...] = jnp.zeros_like(tailbc)
        state[...] = jnp.zeros_like(state)

    _conv_silu(xs_ref, tailx, cwx_ref, cbx_ref, xc, D_SSD)
    _conv_silu(bc_ref, tailbc, cwbc_ref, cbbc_ref, bcc, 2 * D_BC)

    v = dt_ref[...] + dtb_ref[...]
    dt = jnp.maximum(v, 0.0) + jnp.log(1.0 + jnp.exp(-jnp.abs(v)))
    a = dt * (-jnp.exp(alog_ref[...]))
    ri = lax.broadcasted_iota(jnp.int32, (q, q), 0)
    ci = lax.broadcasted_iota(jnp.int32, (q, q), 1)
    causal = ri >= ci
    a_cs = jnp.dot(causal.astype(F32), a, preferred_element_type=F32, precision=HIGHEST)
    a_cs_t = a_cs.T
    dt_t = dt.T
    first_half = ci < HEADDIM
    first_half_row = first_half[0:1, :]

    for g in range(N_GROUPS):
        b_g = bcc[:, g * D_STATE:(g + 1) * D_STATE]
        c_g = bcc[:, D_BC + g * D_STATE:D_BC + (g + 1) * D_STATE]
        cb = lax.dot_general(c_g, b_g, (((1,), (1,)), ((), ())), preferred_element_type=F32)
        b_t = b_g.astype(F32).T
        st_g = state[g]
        y_off = jnp.dot(c_g, st_g.astype(BF16), preferred_element_type=F32)
        for pi in range(HEADS_PER_GROUP // 2):
            h0 = g * HEADS_PER_GROUP + 2 * pi
            lanes = slice(h0 * HEADDIM, h0 * HEADDIM + pair_w)
            gl = slice(pi * pair_w, (pi + 1) * pair_w)
            ms, ws, cols, lasts = [], [], [], []
            for h in (h0, h0 + 1):
                col = jnp.broadcast_to(a_cs[:, h:h + 1], (q, q))
                row = a_cs_t[h:h + 1, :]
                dtrow = dt_t[h:h + 1, :]
                decay = jnp.exp(jnp.where(causal, col - row, -jnp.inf))
                ms.append((cb * decay * dtrow).astype(BF16))
                last = a_cs_t[h:h + 1, q - 1:q]
                ws.append((b_t * (jnp.exp(last - row) * dtrow)).astype(BF16))
                cols.append(col)
                lasts.append(last)
            xp = xc[:, lanes]
            zero = jnp.zeros_like(xp)
            x_bd = jnp.concatenate([jnp.where(first_half, xp, zero), jnp.where(first_half, zero, xp)], axis=0)
            y_p = jnp.dot(jnp.concatenate(ms, axis=1), x_bd, preferred_element_type=F32)
            y_p = y_p + y_off[:, gl] * jnp.exp(jnp.where(first_half, cols[0], cols[1]))
            o_ref[:, lanes] = (y_p + dskip_ref[:, lanes] * xp.astype(F32)).astype(BF16)
            s_new = jnp.dot(jnp.concatenate(ws, axis=1), x_bd, preferred_element_type=F32)
            carry = jnp.exp(jnp.where(first_half_row, lasts[0], lasts[1]))
            state[g, :, gl] = st_g[:, gl] * carry + s_new


def _ssd(proj, dt_raw, conv_w, conv_b, dt_bias, a_log, d_skip):
    n = proj.shape[0]
    t = CHUNK
    return pl.pallas_call(
        _ssd_kernel,
        grid=(n // t,),
        in_specs=[pl.BlockSpec((t, D_SSD), lambda i: (i, 0)),
                  pl.BlockSpec((t, 2 * D_BC), lambda i: (i, 1)),
                  pl.BlockSpec((t, LANES), lambda i: (i, 0)),
                  pl.BlockSpec((CONV_K, D_SSD), lambda i: (0, 0)),
                  pl.BlockSpec((CONV_K, 2 * D_BC), lambda i: (0, 1)),
                  pl.BlockSpec((1, D_SSD), lambda i: (0, 0)),
                  pl.BlockSpec((1, 2 * D_BC), lambda i: (0, 1)),
                  _full((1, LANES)), _full((1, LANES)),
                  _full((1, D_SSD))],
        out_specs=pl.BlockSpec((t, D_SSD), lambda i: (i, 0)),
        out_shape=jax.ShapeDtypeStruct((n, D_SSD), BF16),
        scratch_shapes=[pltpu.VMEM((SUBLANES + t, D_SSD), F32),
                        pltpu.VMEM((SUBLANES + t, 2 * D_BC), F32),
                        pltpu.VMEM((N_GROUPS, D_STATE, GROUP_W), F32),
                        pltpu.VMEM((t, D_SSD), BF16),
                        pltpu.VMEM((t, 2 * D_BC), BF16)],
        compiler_params=_cparams(),
        name="ssd",
    )(proj, proj, dt_raw, conv_w, conv_w, conv_b, conv_b, dt_bias, a_log, d_skip)


PAD_ID = 0xFFFFFFFF
TOK_ROW = PACK_TILES
MASK_ROWS = (PACK_TILES + 1, PACK_TILES + 2)


def _tile_rows(ref, first_row):
    return ref.at[pl.ds(pl.multiple_of(first_row * SUBLANES, SUBLANES), SUBLANES)]


def _mix_kernel(xp_ref, u_ref, ypre_ref, x_ref, wz_ref, wgt_ref, sng_ref, poolw_ref, pscale_ref, wbp_ref, wbs_ref, wout_ref,
                g1_ref, ng_ref, sh2_ref, sc2_ref, g2_ref, wr_ref, rb_ref, wsgu_ref, wsd_ref, zero_ref,
                base_ref, ws_ref, cnt_ref, xs_ref, ptail, run, stage, dvm, dsm, sem, sem_s, *, n_tiles, cap):
    t = x_ref.shape[0]
    i = pl.program_id(0)
    slot = lax.rem(i, 2)

    @pl.when(i == 0)
    def _():
        ptail[...] = jnp.zeros_like(ptail)
        run[...] = jnp.zeros_like(run)
        stage[...] = jnp.zeros_like(stage)

    def issue_rows(s, group, after=None):
        zero = 0
        if after is not None:
            probe = jnp.max(jnp.abs(after[0:SUBLANES, 0:min(LANES, after.shape[1])])).astype(jnp.int32)
            zero = probe * zero_ref[0]
        per = t // MIX_ISSUE_GROUPS
        for tt in range(group * per, (group + 1) * per):
            for k in range(TOP_K):
                pltpu.make_async_copy(stage.at[s, pl.ds(tt * SUBLANES, SUBLANES)],
                                      _tile_rows(xs_ref, dsm[s, k, tt] + zero), sem.at[s]).start()

    def drain_rows(s):
        for _ in range(TOP_K):
            pltpu.make_async_copy(stage.at[s], xs_ref.at[pl.ds(0, t * SUBLANES)], sem.at[s]).wait()

    def tile(s, milestone=None):
        _mix_tile_body(xp_ref, u_ref, ypre_ref, x_ref, wz_ref, wgt_ref, sng_ref, poolw_ref, pscale_ref, wbp_ref, wbs_ref, wout_ref,
                       g1_ref, ng_ref, sh2_ref, sc2_ref, g2_ref, wr_ref, rb_ref, wsgu_ref, wsd_ref,
                       base_ref, ws_ref, cnt_ref, ptail, run, stage, dvm, dsm, sem_s, i, s, cap, milestone)

    @pl.when(i == 0)
    def _():
        tile(0)

    def milestone(group, value):
        if group == BEFORE_STAGING:
            @pl.when(i >= 2)
            def _():
                drain_rows(slot)
        else:
            issue_rows(1 - slot, group, value)

    @pl.when((i >= 1) & (i < n_tiles))
    def _():
        issue_rows(1 - slot, 0)
        tile(slot, milestone)

    @pl.when(i == n_tiles)
    def _():
        for group in range(MIX_ISSUE_GROUPS):
            issue_rows(1 - slot, group)
        if n_tiles >= 2:
            drain_rows(slot)
        drain_rows(1 - slot)


def _mix_tile_body(xp_ref, u_ref, ypre_ref, x_ref, wz_ref, wgt_ref, sng_ref, poolw_ref, pscale_ref, wbp_ref, wbs_ref, wout_ref,
                   g1_ref, ng_ref, sh2_ref, sc2_ref, g2_ref, wr_ref, rb_ref, wsgu_ref, wsd_ref,
                   base_ref, ws_ref, cnt_ref, ptail, run, stage, dvm, dsm, sem_s, i, slot, cap, milestone):
    t = x_ref.shape[0]
    if milestone is None:
        milestone = lambda group, value: None

    xp = xp_ref[...].astype(F32)
    ext = jnp.concatenate([ptail[...], xp], axis=0)
    ptail[...] = xp[t - 2 * SUBLANES:t]
    pos = (lax.broadcasted_iota(jnp.int32, (t, 1), 0) + (i * t + 1)).astype(F32)
    pooled = []
    for gi, w in enumerate(POOL_WINDOWS):
        sl = slice(gi * POOL_GDIM, (gi + 1) * POOL_GDIM)
        e = ext[:, sl]
        s = e
        span = 1
        while span < w:
            s = s + pltpu.roll(s, span, axis=0)
            span *= 2
        win = s[2 * SUBLANES:]
        mean = win / jnp.minimum(pos, float(w))
        pg = (mean - xp[:, sl]).astype(BF16)
        mixed = jnp.dot(pg, poolw_ref[gi], preferred_element_type=F32)
        pooled.append(mixed * pscale_ref[:, sl])
        if gi == 1:
            milestone(1, pooled[-1])
    pooled = jnp.concatenate(pooled, axis=1).astype(BF16)
    y_pool = jnp.dot(pooled, wbp_ref[...], preferred_element_type=F32)
    milestone(2, y_pool)
    ub = u_ref[...]
    z = jnp.dot(ub, wz_ref[...], preferred_element_type=F32)
    milestone(3, z)
    yn = []
    for g in range(N_GROUPS):
        sl = slice(g * GROUP_W, (g + 1) * GROUP_W)
        yg = ypre_ref[:, sl].astype(F32) * _silu(z[:, sl])
        inv_g = lax.rsqrt(jnp.mean(yg * yg, axis=-1, keepdims=True) + EPS)
        yn.append((yg * inv_g * sng_ref[:, sl]).astype(BF16))
    milestone(4, inv_g)
    y_ssd = jnp.dot(jnp.concatenate(yn, axis=1), wbs_ref[...], preferred_element_type=F32)
    milestone(5, y_ssd)
    gates = jnp.dot(ub, wgt_ref[...], preferred_element_type=F32)
    milestone(6, gates)
    g_ssd = jax.nn.sigmoid(gates[:, :D_MODEL])
    g_pool = jax.nn.sigmoid(gates[:, D_MODEL:])
    mixed = (g_ssd * y_ssd + g_pool * y_pool).astype(BF16)
    h = x_ref[...] + g1_ref[...] * jnp.dot(mixed, wout_ref[...], preferred_element_type=F32)
    milestone(7, h)

    inv = lax.rsqrt(jnp.mean(h * h, axis=-1, keepdims=True) + EPS)
    milestone(8, inv)
    u2 = h * inv * ng_ref[...]
    u2 = u2 * (1.0 + sc2_ref[...]) + sh2_ref[...]
    u2b = u2.astype(BF16)
    u2r = u2b.astype(F32)
    packed = []
    for j in range(PACK_TILES):
        lo = lax.bitcast_convert_type(u2r[:, j * LANES:(j + 1) * LANES], jnp.uint32)
        hi = lax.bitcast_convert_type(u2r[:, (j + PACK_TILES) * LANES:(j + PACK_TILES + 1) * LANES], jnp.uint32)
        packed.append((lo >> 16) | (hi & jnp.uint32(0xFFFF0000)))

    hs = jnp.dot(u2b, wsgu_ref[...], preferred_element_type=F32)
    milestone(9, hs)
    act = (_silu(hs[:, :D_SHARED]) * hs[:, D_SHARED:]).astype(BF16)
    shared = jnp.dot(act, wsd_ref[...], preferred_element_type=F32)
    milestone(10, shared)
    base_ref[...] = h + g2_ref[...] * shared

    logits = jnp.dot(u2b, wr_ref[...], preferred_element_type=F32)
    scores = jax.nn.sigmoid(logits)
    milestone(11, scores)
    choice = scores + rb_ref[...]
    lane = lax.broadcasted_iota(jnp.int32, (t, N_EXPERTS), 1)
    lane_f = lane.astype(F32)
    lane_grp = lane // EXPERTS_PER_GROUP
    neg = -jnp.inf
    gscore = []
    for g in range(N_EGROUPS):
        vg = jnp.where(lane_grp == g, choice, neg)
        m1 = jnp.max(vg, axis=-1, keepdims=True)
        i1 = jnp.min(jnp.where(vg == m1, lane_f, float(N_EXPERTS)), axis=-1, keepdims=True)
        m2 = jnp.max(jnp.where(lane_f == i1, neg, vg), axis=-1, keepdims=True)
        gscore.append(m1 + m2)
        if g == N_EGROUPS // 2 - 1:
            milestone(12, gscore[-1])
    milestone(13, gscore[-1])
    gmask = jnp.zeros((t, N_EXPERTS), jnp.bool_)
    for g in range(N_EGROUPS):
        rank = jnp.zeros((t, 1), F32)
        for g2 in range(N_EGROUPS):
            if g2 == g:
                continue
            better = (gscore[g2] > gscore[g]) | ((gscore[g2] == gscore[g]) & (g2 < g))
            rank = rank + better.astype(F32)
        gmask = gmask | ((rank < float(TOPK_GROUPS)) & (lane_grp == g))
    work = jnp.where(gmask, choice, neg)
    sel = jnp.zeros((t, N_EXPERTS), jnp.bool_)
    onehots, idxs, sks = [], [], []
    for k in range(TOP_K):
        m = jnp.max(work, axis=-1, keepdims=True)
        idx = jnp.min(jnp.where(work == m, lane_f, float(N_EXPERTS)), axis=-1, keepdims=True)
        oh = lane_f == idx
        onehots.append(oh)
        idxs.append(idx)
        sks.append(jnp.sum(jnp.where(oh, scores, 0.0), axis=-1, keepdims=True))
        if k == TOP_K // 2 - 1:
            milestone(14, sks[-1])
        sel = sel | oh
        work = jnp.where(oh, neg, work)
    denom = sks[0]
    for k in range(1, TOP_K):
        denom = denom + sks[k]
    milestone(15, denom)

    ri = lax.broadcasted_iota(jnp.int32, (t, t), 0)
    ci = lax.broadcasted_iota(jnp.int32, (t, t), 1)
    before = (ri > ci).astype(BF16)
    sel_f = jnp.where(sel, 1.0, 0.0)
    sel_b = sel_f.astype(BF16)
    pos_tile = jnp.dot(before, sel_b, preferred_element_type=F32) + run[...]
    er = lax.broadcasted_iota(jnp.int32, (N_EXPERTS, N_EXPERTS), 0)
    ec = lax.broadcasted_iota(jnp.int32, (N_EXPERTS, N_EXPERTS), 1)
    rank_tile = jnp.dot(sel_b, (er < ec).astype(BF16), preferred_element_type=F32)
    lane128 = lax.broadcasted_iota(jnp.int32, (t, LANES), 1)
    k_iota = lax.broadcasted_iota(jnp.int32, (t, TOP_K), 1).astype(F32)
    dest = jnp.zeros((t, LANES), F32)
    w_sorted = jnp.zeros((t, TOP_K), F32)
    for k in range(TOP_K):
        pk = jnp.sum(jnp.where(onehots[k], pos_tile, 0.0), axis=-1, keepdims=True)
        rk = jnp.sum(jnp.where(onehots[k], rank_tile, 0.0), axis=-1, keepdims=True)
        dest = jnp.where(lane128 == k, idxs[k] * float(cap) + pk, dest)
        w_sorted = jnp.where(k_iota == rk, sks[k] / denom * ROUTED_SCALE, w_sorted)
    ws_ref[...] = w_sorted
    total = run[...] + jnp.sum(sel_f, axis=0, keepdims=True)
    run[...] = total
    cnt_ref[...] = total

    milestone(BEFORE_STAGING, None)
    for j in range(PACK_TILES):
        stage[slot, pl.ds(j, t, stride=SUBLANES), :] = packed[j]
    tok = lax.broadcasted_iota(jnp.int32, (t, LANES), 0) + i * t
    stage[slot, pl.ds(TOK_ROW, t, stride=SUBLANES), :] = tok.astype(jnp.uint32)
    bit = jnp.where(sel, jnp.left_shift(1, lane & 15), 0).astype(F32)
    words = []
    for q in range(N_EXPERTS // 16):
        part = jnp.sum(jnp.where((lane >> 4) == q, bit, 0.0), axis=-1, keepdims=True)
        words.append(part.astype(jnp.int32).astype(jnp.uint32))
    for w, row in enumerate(MASK_ROWS):
        word = words[2 * w] | (words[2 * w + 1] << 16)
        stage[slot, pl.ds(row, t, stride=SUBLANES), :] = jnp.broadcast_to(word, (t, LANES))

    dvm[...] = dest.T[0:TOP_K, :].astype(jnp.int32)
    smem_copy = pltpu.make_async_copy(dvm, dsm.at[slot], sem_s)
    smem_copy.start()
    smem_copy.wait()


def _mix(proj, u, ypre, x2, w_z, w_gates, ssd_ng, pool_w, pool_scale, w_br_pool, w_br_ssd, w_out, g1, ng, sh2, sc2,
         g2, w_router, router_bias, ws_gu, ws_down, cap):
    n = x2.shape[0]
    t = 256
    n_tiles = n // t
    xp_blk = (D_SSD + 2 * D_BC) // D_MODEL
    const = lambda shape: pl.BlockSpec(shape, lambda i: (0,) * len(shape), pipeline_mode=pl.Buffered(1))
    tile = lambda i: jnp.minimum(i, n_tiles - 1)
    row = lambda i: (tile(i), 0)
    return pl.pallas_call(
        functools.partial(_mix_kernel, n_tiles=n_tiles, cap=cap),
        grid=(n_tiles + 1,),
        in_specs=[pl.BlockSpec((t, D_MODEL), lambda i: (tile(i), xp_blk)),
                  pl.BlockSpec((t, D_MODEL), row),
                  pl.BlockSpec((t, D_SSD), row),
                  pl.BlockSpec((t, D_MODEL), row),
                  const((D_MODEL, D_SSD)),
                  const((D_MODEL, 2 * D_MODEL)),
                  _full((1, D_SSD)),
                  const((len(POOL_WINDOWS), POOL_GDIM, POOL_GDIM)),
                  _full((1, D_MODEL)),
                  const((D_MODEL, D_MODEL)),
                  const((D_SSD, D_MODEL)),
                  const((D_MODEL, D_MODEL)),
                  _full((1, D_MODEL)), _full((1, D_MODEL)), _full((1, D_MODEL)), _full((1, D_MODEL)),
                  _full((1, D_MODEL)),
                  _full((D_MODEL, N_EXPERTS)),
                  _full((1, N_EXPERTS)),
                  const((D_MODEL, 2 * D_SHARED)),
                  const((D_SHARED, D_MODEL)),
                  pl.BlockSpec(memory_space=pltpu.SMEM)],
        out_specs=[pl.BlockSpec((t, D_MODEL), row),
                   pl.BlockSpec((t, TOP_K), row),
                   _full((1, N_EXPERTS)),
                   pl.BlockSpec(memory_space=pl.ANY)],
        out_shape=[jax.ShapeDtypeStruct((n, D_MODEL), F32),
                   jax.ShapeDtypeStruct((n, TOP_K), F32),
                   jax.ShapeDtypeStruct((1, N_EXPERTS), F32),
                   jax.ShapeDtypeStruct((N_EXPERTS * cap * SUBLANES, LANES), jnp.uint32)],
        scratch_shapes=[pltpu.VMEM((2 * SUBLANES, D_MODEL), F32),
                        pltpu.VMEM((1, N_EXPERTS), F32),
                        pltpu.VMEM((2, t * SUBLANES, LANES), jnp.uint32),
                        pltpu.VMEM((TOP_K, t), jnp.int32),
                        pltpu.SMEM((2, TOP_K, t), jnp.int32),
                        pltpu.SemaphoreType.DMA((2,)),
                        pltpu.SemaphoreType.DMA(())],
        compiler_params=_cparams(),
        name="mix",
    )(proj, u, ypre, x2, w_z, w_gates, ssd_ng, pool_w, pool_scale, w_br_pool, w_br_ssd, w_out, g1, ng, sh2, sc2, g2,
      w_router, router_bias, ws_gu, ws_down, jnp.zeros((1,), jnp.int32))


def _padfill_kernel(cnt_ref, xs_in_ref, xs_ref, padbuf, sem, *, cap):
    del xs_in_ref
    sub = lax.broadcasted_iota(jnp.int32, padbuf.shape, 0) & (SUBLANES - 1)
    padbuf[...] = jnp.where(sub == TOK_ROW, jnp.uint32(PAD_ID), jnp.uint32(0))
    sizes = [1 << bit for bit in range(EXPERT_BLOCK.bit_length() - 1)]

    def pad_copies(e, fn):
        c = cnt_ref[e]
        n_pad = (EXPERT_BLOCK - (c & (EXPERT_BLOCK - 1))) & (EXPERT_BLOCK - 1)
        first = e * cap + c
        for size in sizes:
            @pl.when((n_pad & size) != 0)
            def _():
                start = first + (n_pad & (size - 1))
                dst = xs_ref.at[pl.ds(pl.multiple_of(start * SUBLANES, SUBLANES), size * SUBLANES)]
                fn(pltpu.make_async_copy(padbuf.at[pl.ds(0, size * SUBLANES)], dst, sem))

    def issue(e, carry):
        pad_copies(e, lambda cp: cp.start())
        return carry

    lax.fori_loop(0, N_EXPERTS, issue, 0)

    def drain(e, carry):
        pad_copies(e, lambda cp: cp.wait())
        return carry

    lax.fori_loop(0, N_EXPERTS, drain, 0)


def _padfill(counts, xs, cap):
    return pl.pallas_call(
        functools.partial(_padfill_kernel, cap=cap),
        grid_spec=pltpu.PrefetchScalarGridSpec(
            num_scalar_prefetch=1,
            grid=(1,),
            in_specs=[pl.BlockSpec(memory_space=pl.ANY)],
            out_specs=pl.BlockSpec(memory_space=pl.ANY),
            scratch_shapes=[pltpu.VMEM((EXPERT_BLOCK // 2 * SUBLANES, LANES), jnp.uint32),
                            pltpu.SemaphoreType.DMA(())]),
        out_shape=jax.ShapeDtypeStruct(xs.shape, xs.dtype),
        input_output_aliases={1: 0},
        compiler_params=_cparams(),
        name="padfill",
    )(counts, xs)


def _expert_kernel(be_ref, br_ref, nb_ref, x_ref, wg_ref, wu_ref, wd_ref, ytok_ref, ybuf0, ybuf1, idv, ids0, ids1,
                   wgb, wub, wdb, sem, sem_ids, *, n_ids):
    del br_ref
    m = EXPERT_BLOCK
    b = pl.program_id(0)
    nb = nb_ref[0]
    ybufs, idss = (ybuf0, ybuf1), (ids0, ids1)
    n_phases = m // LANES
    half = D_MODEL // 2

    @pl.when((b == 0) | (be_ref[b] != be_ref[jnp.maximum(b - 1, 0)]))
    def _():
        wgb[...] = wg_ref[0].astype(BF16)
        wub[...] = wu_ref[0].astype(BF16)
        wdb[...] = wd_ref[0].astype(BF16)

    def issue(s, a0):
        for col in range(LANES):
            pltpu.make_async_copy(_tile_rows(ybufs[s], a0 * LANES + col),
                                  _tile_rows(ytok_ref, idss[s][a0, col]), sem.at[s]).start()

    def step(s, prev):
        tokrep = x_ref[pl.ds(TOK_ROW, m, stride=SUBLANES), :]
        e = be_ref[b]
        below_lo = jnp.where(e >= 32, -1, (1 << jnp.minimum(e, 31)) - 1).astype(jnp.uint32)
        below_hi = jnp.where(e >= 32, (1 << jnp.maximum(e - 32, 0)) - 1, 0).astype(jnp.uint32)
        rank = (lax.population_count(x_ref[pl.ds(MASK_ROWS[0], m, stride=SUBLANES), :] & below_lo)
                + lax.population_count(x_ref[pl.ds(MASK_ROWS[1], m, stride=SUBLANES), :] & below_hi))
        r = lax.broadcasted_iota(jnp.int32, (m, LANES), 0)
        ln = lax.broadcasted_iota(jnp.int32, (m, LANES), 1)
        n_tok = n_ids // TOP_K
        idi = jnp.where(tokrep == jnp.uint32(PAD_ID), n_ids + r,
                        rank.astype(jnp.int32) * n_tok + tokrep.astype(jnp.int32))
        diag = jnp.where((r & (LANES - 1)) == ln, idi, 0).astype(F32)
        idv[...] = jnp.sum(diag.reshape(m // LANES, LANES, LANES), axis=1).astype(jnp.int32)
        ids_copy = pltpu.make_async_copy(idv, idss[s], sem_ids)
        ids_copy.start()

        lo, hi = [], []
        for j in range(PACK_TILES):
            w = x_ref[pl.ds(j, m, stride=SUBLANES), :]
            lo.append(lax.bitcast_convert_type(w << 16, F32).astype(BF16))
            hi.append(lax.bitcast_convert_type(w & jnp.uint32(0xFFFF0000), F32).astype(BF16))
        x = jnp.concatenate(lo + hi, axis=1)
        phases = iter(range(n_phases))
        if prev is not None:
            issue(prev, next(phases))
        gate = jnp.dot(x, wgb[...], preferred_element_type=F32)
        if prev is not None:
            issue(prev, next(phases))
        up = jnp.dot(x, wub[...], preferred_element_type=F32)
        hid = (_silu(gate) * up).astype(BF16)

        def drain(which):
            pltpu.make_async_copy(ybufs[which], ytok_ref.at[pl.ds(0, m * SUBLANES)], sem.at[which]).wait()

        if prev is not None:
            @pl.when(b >= 2)
            def _():
                drain(s)
        for c0 in (0, half):
            if prev is not None:
                issue(prev, next(phases))
            y = jnp.dot(hid, wdb[:, c0:c0 + half], preferred_element_type=F32)
            for j in range(half // LANES):
                ybufs[s][pl.ds(c0 // LANES + j, m, stride=SUBLANES), :] = y[:, j * LANES:(j + 1) * LANES]
        ids_copy.wait()
        if prev is not None:
            for a0 in phases:
                issue(prev, a0)

            @pl.when(b == nb)
            def _():
                drain(prev)

    @pl.when(b == 0)
    def _():
        step(0, None)

    for s in range(2):
        @pl.when((b >= 1) & (b <= nb) & (lax.rem(b, 2) == s))
        def _():
            step(s, 1 - s)


def _experts(block_e, block_row, nb_used, xs, we_gate, we_up, we_down, n_ids):
    n_blocks = block_e.shape[0]
    m = EXPERT_BLOCK
    rows = m * SUBLANES
    return pl.pallas_call(
        functools.partial(_expert_kernel, n_ids=n_ids),
        grid_spec=pltpu.PrefetchScalarGridSpec(
            num_scalar_prefetch=3,
            grid=(n_blocks,),
            in_specs=[pl.BlockSpec((rows, LANES), lambda b, be, br, nb: (br[b], 0)),
                      pl.BlockSpec((1, D_MODEL, D_EXPERT), lambda b, be, br, nb: (be[b], 0, 0)),
                      pl.BlockSpec((1, D_MODEL, D_EXPERT), lambda b, be, br, nb: (be[b], 0, 0)),
                      pl.BlockSpec((1, D_EXPERT, D_MODEL), lambda b, be, br, nb: (be[b], 0, 0))],
            out_specs=pl.BlockSpec(memory_space=pl.ANY),
            scratch_shapes=[pltpu.VMEM((rows, LANES), F32),
                            pltpu.VMEM((rows, LANES), F32),
                            pltpu.VMEM((m // LANES, LANES), jnp.int32),
                            pltpu.SMEM((m // LANES, LANES), jnp.int32),
                            pltpu.SMEM((m // LANES, LANES), jnp.int32),
                            pltpu.VMEM((D_MODEL, D_EXPERT), BF16),
                            pltpu.VMEM((D_MODEL, D_EXPERT), BF16),
                            pltpu.VMEM((D_EXPERT, D_MODEL), BF16),
                            pltpu.SemaphoreType.DMA((2,)),
                            pltpu.SemaphoreType.DMA(())]),
        out_shape=jax.ShapeDtypeStruct(((n_ids + m) * SUBLANES, LANES), F32),
        compiler_params=_cparams(),
        name="experts",
    )(block_e, block_row, nb_used, xs, we_gate, we_up, we_down)


def _combine_kernel(*refs):
    y_refs = refs[:TOP_K]
    w8_ref, base_ref, g2_ref, fg_ref, o_ref = refs[TOP_K:]
    t7 = base_ref.shape[0]
    w8 = w8_ref[...]
    parts = []
    for j in range(ROW_TILES):
        acc = jnp.zeros((t7, LANES), F32)
        for k in range(TOP_K):
            acc = acc + w8[:, k:k + 1] * y_refs[k][pl.ds(j, t7, stride=SUBLANES), :]
        parts.append(acc)
    routed = jnp.concatenate(parts, axis=1)
    h = base_ref[...] + g2_ref[...] * routed
    inv = lax.rsqrt(jnp.mean(h * h, axis=-1, keepdims=True) + EPS)
    o_ref[...] = h * inv * fg_ref[...]


def _combine(ytok, w8, base, g2, fg):
    n = base.shape[0]
    t7 = 128
    y_specs = [pl.BlockSpec((t7 * SUBLANES, LANES), functools.partial(lambda i, k: (k * (n // t7) + i, 0), k=k))
               for k in range(TOP_K)]
    return pl.pallas_call(
        _combine_kernel,
        grid=(n // t7,),
        in_specs=y_specs + [
                  pl.BlockSpec((t7, TOP_K), lambda i: (i, 0)),
                  pl.BlockSpec((t7, D_MODEL), lambda i: (i, 0)),
                  _full((1, D_MODEL)), _full((1, D_MODEL))],
        out_specs=pl.BlockSpec((t7, D_MODEL), lambda i: (i, 0)),
        out_shape=jax.ShapeDtypeStruct((n, D_MODEL), F32),
        compiler_params=_cparams(),
        name="combine",
    )(*([ytok] * TOP_K), w8, base, g2, fg)


def kernel(x, c, w_ada, b_ada, norm_mix_g, w_in, conv_w, conv_b, dt_bias, A_log, D_skip, ssd_norm_g, pool_w,
           pool_scale, w_br_ssd, w_br_pool, w_out, norm_ffn_g, w_router, router_bias, we_gate, we_up, we_down,
           ws_gate, ws_up, ws_down, final_norm_g):
    bsz, seq, _ = x.shape
    assert bsz == 1 and w_ada.shape[0] == 1
    n = seq
    h2 = x.reshape(n, D_MODEL)

    mod = _ada(c, w_ada[0], b_ada[0])
    sh1, sc1, g1, sh2, sc2, g2 = [mod[:, k * D_MODEL:(k + 1) * D_MODEL] for k in range(6)]

    wi = w_in[0]
    o_xbc, o_dt = D_SSD, D_SSD + D_SSD + 2 * D_BC
    o_pool = o_dt + N_HEADS
    o_gates = o_pool + D_MODEL
    w_dt = jnp.pad(wi[:, o_dt:o_pool], ((0, 0), (0, LANES - N_HEADS))).astype(BF16)
    proj, dt_raw, u = _in_proj(h2, norm_mix_g[0][None], sh1, sc1, wi[:, o_xbc:o_dt].astype(BF16),
                               wi[:, o_pool:o_gates].astype(BF16), w_dt)

    pad_h = lambda v: jnp.pad(v[None], ((0, 0), (0, LANES - N_HEADS)))
    ypre = _ssd(proj, dt_raw, conv_w[0], conv_b[0][None], pad_h(dt_bias[0]), pad_h(A_log[0]),
                jnp.repeat(D_skip[0], HEADDIM)[None])

    ws_gu = jnp.concatenate([ws_gate[0], ws_up[0]], axis=1).astype(BF16)
    m = EXPERT_BLOCK
    cap = -(-n // m) * m
    base, w8, cnt, xs = _mix(
        proj, u, ypre, h2, wi[:, :o_xbc].astype(BF16), wi[:, o_gates:].astype(BF16), ssd_norm_g[0][None],
        pool_w[0].astype(BF16), pool_scale[0][None], w_br_pool[0].astype(BF16),
        w_br_ssd[0].astype(BF16), w_out[0].astype(BF16), g1, norm_ffn_g[0][None], sh2, sc2, g2,
        w_router[0].astype(BF16), router_bias[0][None], ws_gu, ws_down[0].astype(BF16), cap)

    counts = cnt[0].astype(jnp.int32)
    nblk = (counts + m - 1) // m
    bends = jnp.cumsum(nblk)
    bstarts = bends - nblk
    nb_used = bends[-1]
    n_blocks = -(-(n * TOP_K) // m) + N_EXPERTS + 1
    b_eff = jnp.minimum(jnp.arange(n_blocks, dtype=jnp.int32), nb_used - 1)
    done = bends[None, :] <= b_eff[:, None]
    block_e = jnp.sum(done.astype(jnp.int32), axis=1)
    block_row = block_e * (cap // m) + b_eff - jnp.sum(jnp.where(done, nblk[None, :], 0), axis=1)

    xs = _padfill(counts, xs, cap)
    ytok = _experts(block_e, block_row, nb_used.reshape(1), xs,
                    we_gate[0], we_up[0], we_down[0], n * TOP_K)
    out = _combine(ytok, w8, base, g2, final_norm_g[None])
    return out.reshape(bsz, seq, D_MODEL)
```

```python
import functools

import jax
import jax.numpy as jnp
from jax import lax
from jax.experimental import pallas as pl
from jax.experimental.pallas import tpu as pltpu

F32 = jnp.float32
BF16 = jnp.bfloat16
HIGHEST = lax.Precision.HIGHEST

D_MODEL = 1024
D_SSD = 2048
HEADDIM = 64
N_HEADS = 32
N_GROUPS = 8
HEADS_PER_GROUP = N_HEADS // N_GROUPS
D_STATE = 128
CONV_K = 4
CHUNK = 128
GROUP_W = D_SSD // N_GROUPS
D_BC = N_GROUPS * D_STATE
POOL_WINDOWS = (2, 4, 8, 16)
POOL_GDIM = 256
N_EXPERTS = 64
TOP_K = 8
N_EGROUPS = 8
EXPERTS_PER_GROUP = 8
TOPK_GROUPS = 4
D_EXPERT = 256
D_SHARED = 256
ROUTED_SCALE = 2.5
MOE_BLOCK = 128
EPS = 1e-6

LANES = 128
SUBLANES = 8
ROW_TILES = D_MODEL // LANES
PACK_TILES = ROW_TILES // 2
PACK_W = PACK_TILES * LANES
EXPERT_BLOCK = 512
MIX_ISSUE_GROUPS = 32
BEFORE_STAGING = "before_staging"

PROJ_W = D_SSD + 2 * D_BC + D_MODEL
PROJ_CHUNK = 512

VMEM_LIMIT = 56 * 1024 * 1024


def _cparams(sem=("arbitrary",)):
    return pltpu.CompilerParams(dimension_semantics=sem, vmem_limit_bytes=VMEM_LIMIT)


def _full(shape):
    nd = len(shape)
    return pl.BlockSpec(shape, lambda *_: (0,) * nd)


def _silu(v):
    return v * jax.nn.sigmoid(v)


def _ada_kernel(c_ref, w_ref, b_ref, o_ref):
    c = c_ref[...]
    o_ref[...] = jnp.dot(_silu(c), w_ref[...], preferred_element_type=F32, precision=HIGHEST) + b_ref[...]


def _ada(c, w_ada, b_ada):
    n_out = w_ada.shape[1]
    tn = 1536
    c8 = jnp.broadcast_to(c, (SUBLANES, D_MODEL))
    out = pl.pallas_call(
        _ada_kernel,
        grid=(n_out // tn,),
        in_specs=[_full((SUBLANES, D_MODEL)),
                  pl.BlockSpec((D_MODEL, tn), lambda j: (0, j)),
                  pl.BlockSpec((1, tn), lambda j: (0, j))],
        out_specs=pl.BlockSpec((SUBLANES, tn), lambda j: (0, j)),
        out_shape=jax.ShapeDtypeStruct((SUBLANES, n_out), F32),
        compiler_params=_cparams(),
        name="ada",
    )(c8, w_ada, b_ada.reshape(1, n_out))
    return out[0:1]


def _inproj_kernel(x_ref, g_ref, sh_ref, sc_ref, wa_ref, wb_ref, wdt_ref, proj_ref, dt_ref, u_ref):
    x = x_ref[...]
    inv = lax.rsqrt(jnp.mean(x * x, axis=-1, keepdims=True) + EPS)
    u = x * inv * g_ref[...]
    u = u * (1.0 + sc_ref[...]) + sh_ref[...]
    ub = u.astype(BF16)
    u_ref[...] = ub
    wa = wa_ref.shape[1]
    for c0 in range(0, PROJ_W, PROJ_CHUNK):
        w = wa_ref[:, c0:c0 + PROJ_CHUNK] if c0 < wa else wb_ref[:, c0 - wa:c0 - wa + PROJ_CHUNK]
        proj_ref[:, c0:c0 + PROJ_CHUNK] = jnp.dot(ub, w, preferred_element_type=F32).astype(BF16)
    dt_ref[...] = jnp.dot(ub, wdt_ref[...], preferred_element_type=F32)


def _in_proj(x2, g, sh, sc, w_a, w_b, w_dt):
    n = x2.shape[0]
    tm = 256
    return pl.pallas_call(
        _inproj_kernel,
        grid=(n // tm,),
        in_specs=[pl.BlockSpec((tm, D_MODEL), lambda i: (i, 0)),
                  _full((1, D_MODEL)), _full((1, D_MODEL)), _full((1, D_MODEL)),
                  pl.BlockSpec(w_a.shape, lambda i: (0, 0), pipeline_mode=pl.Buffered(1)),
                  pl.BlockSpec(w_b.shape, lambda i: (0, 0), pipeline_mode=pl.Buffered(1)),
                  _full((D_MODEL, LANES))],
        out_specs=[pl.BlockSpec((tm, PROJ_W), lambda i: (i, 0)),
                   pl.BlockSpec((tm, LANES), lambda i: (i, 0)),
                   pl.BlockSpec((tm, D_MODEL), lambda i: (i, 0))],
        out_shape=[jax.ShapeDtypeStruct((n, PROJ_W), BF16),
                   jax.ShapeDtypeStruct((n, LANES), F32),
                   jax.ShapeDtypeStruct((n, D_MODEL), BF16)],
        compiler_params=_cparams(),
        name="in_proj",
    )(x2, g, sh, sc, w_a, w_b, w_dt)


def _conv_silu(cur_ref, ext_ref, w_ref, b_ref, out_ref, width, cw=512):
    t = cur_ref.shape[0]
    for c0 in range(0, width, cw):
        sl = slice(c0, c0 + cw)
        cur = cur_ref[:, sl].astype(F32)
        ext_ref[SUBLANES:, sl] = cur
        acc = cur * w_ref[CONV_K - 1:CONV_K, sl] + b_ref[:, sl]
        for s in range(1, CONV_K):
            acc = acc + ext_ref[pl.ds(SUBLANES - s, t), sl] * w_ref[CONV_K - 1 - s:CONV_K - s, sl]
        out_ref[:, sl] = _silu(acc).astype(out_ref.dtype)
        ext_ref[0:SUBLANES, sl] = cur[t - SUBLANES:t]


def _ssd_kernel(xs_ref, bc_ref, dt_ref, cwx_ref, cwbc_ref, cbx_ref, cbbc_ref, dtb_ref, alog_ref,
                dskip_ref, o_ref, tailx, tailbc, state, xc, bcc):
    q = CHUNK
    pair_w = 2 * HEADDIM

    @pl.when(pl.program_id(0) == 0)
    def _():
        tailx[...] = jnp.zeros_like(tailx)
        tailbc[...] = jnp.zeros_like(tailbc)
        state[...] = jnp.zeros_like(state)

    _conv_silu(xs_ref, tailx, cwx_ref, cbx_ref, xc, D_SSD)
    _conv_silu(bc_ref, tailbc, cwbc_ref, cbbc_ref, bcc, 2 * D_BC)

    v = dt_ref[...] + dtb_ref[...]
    dt = jnp.maximum(v, 0.0) + jnp.log(1.0 + jnp.exp(-jnp.abs(v)))
    a = dt * (-jnp.exp(alog_ref[...]))
    ri = lax.broadcasted_iota(jnp.int32, (q, q), 0)
    ci = lax.broadcasted_iota(jnp.int32, (q, q), 1)
    causal = ri >= ci
    a_cs = jnp.dot(causal.astype(F32), a, preferred_element_type=F32, precision=HIGHEST)
    a_cs_t = a_cs.T
    dt_t = dt.T
    first_half = ci < HEADDIM
    first_half_row = first_half[0:1, :]

    for g in range(N_GROUPS):
        b_g = bcc[:, g * D_STATE:(g + 1) * D_STATE]
        c_g = bcc[:, D_BC + g * D_STATE:D_BC + (g + 1) * D_STATE]
        cb = lax.dot_general(c_g, b_g, (((1,), (1,)), ((), ())), preferred_element_type=F32)
        b_t = b_g.astype(F32).T
        st_g = state[g]
        y_off = jnp.dot(c_g, st_g.astype(BF16), preferred_element_type=F32)
        for pi in range(HEADS_PER_GROUP // 2):
            h0 = g * HEADS_PER_GROUP + 2 * pi
            lanes = slice(h0 * HEADDIM, h0 * HEADDIM + pair_w)
            gl = slice(pi * pair_w, (pi + 1) * pair_w)
            ms, ws, cols, lasts = [], [], [], []
            for h in (h0, h0 + 1):
                col = jnp.broadcast_to(a_cs[:, h:h + 1], (q, q))
                row = a_cs_t[h:h + 1, :]
                dtrow = dt_t[h:h + 1, :]
                decay = jnp.exp(jnp.where(causal, col - row, -jnp.inf))
                ms.append((cb * decay * dtrow).astype(BF16))
                last = a_cs_t[h:h + 1, q - 1:q]
                ws.append((b_t * (jnp.exp(last - row) * dtrow)).astype(BF16))
                cols.append(col)
                lasts.append(last)
            xp = xc[:, lanes]
            zero = jnp.zeros_like(xp)
            x_bd = jnp.concatenate([jnp.where(first_half, xp, zero), jnp.where(first_half, zero, xp)], axis=0)
            y_p = jnp.dot(jnp.concatenate(ms, axis=1), x_bd, preferred_element_type=F32)
            y_p = y_p + y_off[:, gl] * jnp.exp(jnp.where(first_half, cols[0], cols[1]))
            o_ref[:, lanes] = (y_p + dskip_ref[:, lanes] * xp.astype(F32)).astype(BF16)
            s_new = jnp.dot(jnp.concatenate(ws, axis=1), x_bd, preferred_element_type=F32)
            carry = jnp.exp(jnp.where(first_half_row, lasts[0], lasts[1]))
            state[g, :, gl] = st_g[:, gl] * carry + s_new


def _ssd(proj, dt_raw, conv_w, conv_b, dt_bias, a_log, d_skip):
    n = proj.shape[0]
    t = CHUNK
    return pl.pallas_call(
        _ssd_kernel,
        grid=(n // t,),
        in_specs=[pl.BlockSpec((t, D_SSD), lambda i: (i, 0)),
                  pl.BlockSpec((t, 2 * D_BC), lambda i: (i, 1)),
                  pl.BlockSpec((t, LANES), lambda i: (i, 0)),
                  pl.BlockSpec((CONV_K, D_SSD), lambda i: (0, 0)),
                  pl.BlockSpec((CONV_K, 2 * D_BC), lambda i: (0, 1)),
                  pl.BlockSpec((1, D_SSD), lambda i: (0, 0)),
                  pl.BlockSpec((1, 2 * D_BC), lambda i: (0, 1)),
                  _full((1, LANES)), _full((1, LANES)),
                  _full((1, D_SSD))],
        out_specs=pl.BlockSpec((t, D_SSD), lambda i: (i, 0)),
        out_shape=jax.ShapeDtypeStruct((n, D_SSD), BF16),
        scratch_shapes=[pltpu.VMEM((SUBLANES + t, D_SSD), F32),
                        pltpu.VMEM((SUBLANES + t, 2 * D_BC), F32),
                        pltpu.VMEM((N_GROUPS, D_STATE, GROUP_W), F32),
                        pltpu.VMEM((t, D_SSD), BF16),
                        pltpu.VMEM((t, 2 * D_BC), BF16)],
        compiler_params=_cparams(),
        name="ssd",
    )(proj, proj, dt_raw, conv_w, conv_w, conv_b, conv_b, dt_bias, a_log, d_skip)


PAD_ID = 0xFFFFFFFF
TOK_ROW = PACK_TILES
MASK_ROWS = (PACK_TILES + 1, PACK_TILES + 2)


def _tile_rows(ref, first_row):
    return ref.at[pl.ds(pl.multiple_of(first_row * SUBLANES, SUBLANES), SUBLANES)]


def _mix_kernel(xp_ref, u_ref, ypre_ref, x_ref, wz_ref, wgt_ref, sng_ref, poolw_ref, pscale_ref, wbp_ref, wbs_ref, wout_ref,
                g1_ref, ng_ref, sh2_ref, sc2_ref, g2_ref, wr_ref, rb_ref, wsgu_ref, wsd_ref, zero_ref,
                base_ref, ws_ref, cnt_ref, xs_ref, ptail, run, stage, dvm, dsm, sem, sem_s, *, n_tiles, cap):
    t = x_ref.shape[0]
    i = pl.program_id(0)
    slot = lax.rem(i, 2)

    @pl.when(i == 0)
    def _():
        ptail[...] = jnp.zeros_like(ptail)
        run[...] = jnp.zeros_like(run)
        stage[...] = jnp.zeros_like(stage)

    def issue_rows(s, group, after=None):
        zero = 0
        if after is not None:
            probe = jnp.max(jnp.abs(after[0:SUBLANES, 0:min(LANES, after.shape[1])])).astype(jnp.int32)
            zero = probe * zero_ref[0]
        per = t // MIX_ISSUE_GROUPS
        for tt in range(group * per, (group + 1) * per):
            for k in range(TOP_K):
                pltpu.make_async_copy(stage.at[s, pl.ds(tt * SUBLANES, SUBLANES)],
                                      _tile_rows(xs_ref, dsm[s, k, tt] + zero), sem.at[s]).start()

    def drain_rows(s):
        for _ in range(TOP_K):
            pltpu.make_async_copy(stage.at[s], xs_ref.at[pl.ds(0, t * SUBLANES)], sem.at[s]).wait()

    def tile(s, milestone=None):
        _mix_tile_body(xp_ref, u_ref, ypre_ref, x_ref, wz_ref, wgt_ref, sng_ref, poolw_ref, pscale_ref, wbp_ref, wbs_ref, wout_ref,
                       g1_ref, ng_ref, sh2_ref, sc2_ref, g2_ref, wr_ref, rb_ref, wsgu_ref, wsd_ref,
                       base_ref, ws_ref, cnt_ref, ptail, run, stage, dvm, dsm, sem_s, i, s, cap, milestone)

    @pl.when(i == 0)
    def _():
        tile(0)

    groups = iter(range(1, MIX_ISSUE_GROUPS))

    def milestone(value):
        if value is BEFORE_STAGING:
            assert next(groups, None) is None
            @pl.when(i >= 2)
            def _():
                drain_rows(slot)
        else:
            issue_rows(1 - slot, next(groups), value)

    def wait_slot_rows(s):
        pltpu.make_async_copy(dvm, dsm.at[s], sem_s).wait()

    @pl.when((i >= 1) & (i < n_tiles))
    def _():
        wait_slot_rows(1 - slot)
        issue_rows(1 - slot, 0)
        tile(slot, milestone)

    @pl.when(i == n_tiles)
    def _():
        wait_slot_rows(1 - slot)
        for group in range(MIX_ISSUE_GROUPS):
            issue_rows(1 - slot, group)
        if n_tiles >= 2:
            drain_rows(slot)
        drain_rows(1 - slot)


def _mix_tile_body(xp_ref, u_ref, ypre_ref, x_ref, wz_ref, wgt_ref, sng_ref, poolw_ref, pscale_ref, wbp_ref, wbs_ref, wout_ref,
                   g1_ref, ng_ref, sh2_ref, sc2_ref, g2_ref, wr_ref, rb_ref, wsgu_ref, wsd_ref,
                   base_ref, ws_ref, cnt_ref, ptail, run, stage, dvm, dsm, sem_s, i, slot, cap, milestone):
    t = x_ref.shape[0]
    if milestone is None:
        milestone = lambda value: None

    xp = xp_ref[...].astype(F32)
    ext = jnp.concatenate([ptail[...], xp], axis=0)
    ptail[...] = xp[t - 2 * SUBLANES:t]
    pos = (lax.broadcasted_iota(jnp.int32, (t, 1), 0) + (i * t + 1)).astype(F32)
    pooled = []
    for gi, w in enumerate(POOL_WINDOWS):
        sl = slice(gi * POOL_GDIM, (gi + 1) * POOL_GDIM)
        e = ext[:, sl]
        s = e
        span = 1
        while span < w:
            s = s + pltpu.roll(s, span, axis=0)
            span *= 2
        win = s[2 * SUBLANES:]
        mean = win / jnp.minimum(pos, float(w))
        pg = (mean - xp[:, sl]).astype(BF16)
        mixed = jnp.dot(pg, poolw_ref[gi], preferred_element_type=F32)
        pooled.append(mixed * pscale_ref[:, sl])
        milestone(pooled[-1])
    pooled = jnp.concatenate(pooled, axis=1).astype(BF16)
    y_pool = jnp.dot(pooled, wbp_ref[...], preferred_element_type=F32)
    milestone(y_pool)
    ub = u_ref[...]
    z = jnp.dot(ub, wz_ref[...], preferred_element_type=F32)
    milestone(z)
    yn = []
    for g in range(N_GROUPS):
        sl = slice(g * GROUP_W, (g + 1) * GROUP_W)
        yg = ypre_ref[:, sl].astype(F32) * _silu(z[:, sl])
        inv_g = lax.rsqrt(jnp.mean(yg * yg, axis=-1, keepdims=True) + EPS)
        yn.append((yg * inv_g * sng_ref[:, sl]).astype(BF16))
        milestone(inv_g)
    y_ssd = jnp.dot(jnp.concatenate(yn, axis=1), wbs_ref[...], preferred_element_type=F32)
    milestone(y_ssd)
    gates = jnp.dot(ub, wgt_ref[...], preferred_element_type=F32)
    milestone(gates)
    g_ssd = jax.nn.sigmoid(gates[:, :D_MODEL])
    g_pool = jax.nn.sigmoid(gates[:, D_MODEL:])
    mixed = (g_ssd * y_ssd + g_pool * y_pool).astype(BF16)
    h = x_ref[...] + g1_ref[...] * jnp.dot(mixed, wout_ref[...], preferred_element_type=F32)
    milestone(h)

    inv = lax.rsqrt(jnp.mean(h * h, axis=-1, keepdims=True) + EPS)
    milestone(inv)
    u2 = h * inv * ng_ref[...]
    u2 = u2 * (1.0 + sc2_ref[...]) + sh2_ref[...]
    u2b = u2.astype(BF16)
    u2r = u2b.astype(F32)
    packed = []
    for j in range(PACK_TILES):
        lo = lax.bitcast_convert_type(u2r[:, j * LANES:(j + 1) * LANES], jnp.uint32)
        hi = lax.bitcast_convert_type(u2r[:, (j + PACK_TILES) * LANES:(j + PACK_TILES + 1) * LANES], jnp.uint32)
        packed.append((lo >> 16) | (hi & jnp.uint32(0xFFFF0000)))

    hs = jnp.dot(u2b, wsgu_ref[...], preferred_element_type=F32)
    milestone(hs)
    act = (_silu(hs[:, :D_SHARED]) * hs[:, D_SHARED:]).astype(BF16)
    shared = jnp.dot(act, wsd_ref[...], preferred_element_type=F32)
    milestone(shared)
    base_ref[...] = h + g2_ref[...] * shared

    logits = jnp.dot(u2b, wr_ref[...], preferred_element_type=F32)
    scores = jax.nn.sigmoid(logits)
    milestone(scores)
    choice = scores + rb_ref[...]
    lane = lax.broadcasted_iota(jnp.int32, (t, N_EXPERTS), 1)
    lane_f = lane.astype(F32)
    lane_grp = lane // EXPERTS_PER_GROUP
    neg = -jnp.inf
    gscore = []
    for g in range(N_EGROUPS):
        vg = jnp.where(lane_grp == g, choice, neg)
        m1 = jnp.max(vg, axis=-1, keepdims=True)
        i1 = jnp.min(jnp.where(vg == m1, lane_f, float(N_EXPERTS)), axis=-1, keepdims=True)
        m2 = jnp.max(jnp.where(lane_f == i1, neg, vg), axis=-1, keepdims=True)
        gscore.append(m1 + m2)
        milestone(gscore[-1])
    gmask = jnp.zeros((t, N_EXPERTS), jnp.bool_)
    for g in range(N_EGROUPS):
        rank = jnp.zeros((t, 1), F32)
        for g2 in range(N_EGROUPS):
            if g2 == g:
                continue
            better = (gscore[g2] > gscore[g]) | ((gscore[g2] == gscore[g]) & (g2 < g))
            rank = rank + better.astype(F32)
        gmask = gmask | ((rank < float(TOPK_GROUPS)) & (lane_grp == g))
    work = jnp.where(gmask, choice, neg)
    sel = jnp.zeros((t, N_EXPERTS), jnp.bool_)
    onehots, idxs, sks = [], [], []
    for k in range(TOP_K):
        m = jnp.max(work, axis=-1, keepdims=True)
        idx = jnp.min(jnp.where(work == m, lane_f, float(N_EXPERTS)), axis=-1, keepdims=True)
        oh = lane_f == idx
        onehots.append(oh)
        idxs.append(idx)
        sks.append(jnp.sum(jnp.where(oh, scores, 0.0), axis=-1, keepdims=True))
        if k == TOP_K // 2 - 1:
            milestone(sks[-1])
        sel = sel | oh
        work = jnp.where(oh, neg, work)
    denom = sks[0]
    for k in range(1, TOP_K):
        denom = denom + sks[k]
    milestone(denom)

    ri = lax.broadcasted_iota(jnp.int32, (t, t), 0)
    ci = lax.broadcasted_iota(jnp.int32, (t, t), 1)
    before = (ri > ci).astype(BF16)
    sel_f = jnp.where(sel, 1.0, 0.0)
    sel_b = sel_f.astype(BF16)
    pos_tile = jnp.dot(before, sel_b, preferred_element_type=F32) + run[...]
    er = lax.broadcasted_iota(jnp.int32, (N_EXPERTS, N_EXPERTS), 0)
    ec = lax.broadcasted_iota(jnp.int32, (N_EXPERTS, N_EXPERTS), 1)
    rank_tile = jnp.dot(sel_b, (er < ec).astype(BF16), preferred_element_type=F32)
    lane128 = lax.broadcasted_iota(jnp.int32, (t, LANES), 1)
    k_iota = lax.broadcasted_iota(jnp.int32, (t, TOP_K), 1).astype(F32)
    dest = jnp.zeros((t, LANES), F32)
    w_sorted = jnp.zeros((t, TOP_K), F32)
    for k in range(TOP_K):
        pk = jnp.sum(jnp.where(onehots[k], pos_tile, 0.0), axis=-1, keepdims=True)
        rk = jnp.sum(jnp.where(onehots[k], rank_tile, 0.0), axis=-1, keepdims=True)
        dest = jnp.where(lane128 == k, idxs[k] * float(cap) + pk, dest)
        w_sorted = jnp.where(k_iota == rk, sks[k] / denom * ROUTED_SCALE, w_sorted)
    ws_ref[...] = w_sorted
    total = run[...] + jnp.sum(sel_f, axis=0, keepdims=True)
    run[...] = total
    cnt_ref[...] = total

    milestone(BEFORE_STAGING)
    for j in range(PACK_TILES):
        stage[slot, pl.ds(j, t, stride=SUBLANES), :] = packed[j]
    tok = lax.broadcasted_iota(jnp.int32, (t, LANES), 0) + i * t
    stage[slot, pl.ds(TOK_ROW, t, stride=SUBLANES), :] = tok.astype(jnp.uint32)
    bit = jnp.where(sel, jnp.left_shift(1, lane & 15), 0).astype(F32)
    words = []
    for q in range(N_EXPERTS // 16):
        part = jnp.sum(jnp.where((lane >> 4) == q, bit, 0.0), axis=-1, keepdims=True)
        words.append(part.astype(jnp.int32).astype(jnp.uint32))
    for w, row in enumerate(MASK_ROWS):
        word = words[2 * w] | (words[2 * w + 1] << 16)
        stage[slot, pl.ds(row, t, stride=SUBLANES), :] = jnp.broadcast_to(word, (t, LANES))

    dvm[...] = dest.T[0:TOP_K, :].astype(jnp.int32)
    pltpu.make_async_copy(dvm, dsm.at[slot], sem_s).start()


def _mix(proj, u, ypre, x2, w_z, w_gates, ssd_ng, pool_w, pool_scale, w_br_pool, w_br_ssd, w_out, g1, ng, sh2, sc2,
         g2, w_router, router_bias, ws_gu, ws_down, cap):
    n = x2.shape[0]
    t = 256
    n_tiles = n // t
    xp_blk = (D_SSD + 2 * D_BC) // D_MODEL
    const = lambda shape: pl.BlockSpec(shape, lambda i: (0,) * len(shape), pipeline_mode=pl.Buffered(1))
    tile = lambda i: jnp.minimum(i, n_tiles - 1)
    row = lambda i: (tile(i), 0)
    return pl.pallas_call(
        functools.partial(_mix_kernel, n_tiles=n_tiles, cap=cap),
        grid=(n_tiles + 1,),
        in_specs=[pl.BlockSpec((t, D_MODEL), lambda i: (tile(i), xp_blk)),
                  pl.BlockSpec((t, D_MODEL), row),
                  pl.BlockSpec((t, D_SSD), row),
                  pl.BlockSpec((t, D_MODEL), row),
                  const((D_MODEL, D_SSD)),
                  const((D_MODEL, 2 * D_MODEL)),
                  _full((1, D_SSD)),
                  const((len(POOL_WINDOWS), POOL_GDIM, POOL_GDIM)),
                  _full((1, D_MODEL)),
                  const((D_MODEL, D_MODEL)),
                  const((D_SSD, D_MODEL)),
                  const((D_MODEL, D_MODEL)),
                  _full((1, D_MODEL)), _full((1, D_MODEL)), _full((1, D_MODEL)), _full((1, D_MODEL)),
                  _full((1, D_MODEL)),
                  _full((D_MODEL, N_EXPERTS)),
                  _full((1, N_EXPERTS)),
                  const((D_MODEL, 2 * D_SHARED)),
                  const((D_SHARED, D_MODEL)),
                  pl.BlockSpec(memory_space=pltpu.SMEM)],
        out_specs=[pl.BlockSpec((t, D_MODEL), row),
                   pl.BlockSpec((t, TOP_K), row),
                   _full((1, N_EXPERTS)),
                   pl.BlockSpec(memory_space=pl.ANY)],
        out_shape=[jax.ShapeDtypeStruct((n, D_MODEL), F32),
                   jax.ShapeDtypeStruct((n, TOP_K), F32),
                   jax.ShapeDtypeStruct((1, N_EXPERTS), F32),
                   jax.ShapeDtypeStruct((N_EXPERTS * cap * SUBLANES, LANES), jnp.uint32)],
        scratch_shapes=[pltpu.VMEM((2 * SUBLANES, D_MODEL), F32),
                        pltpu.VMEM((1, N_EXPERTS), F32),
                        pltpu.VMEM((2, t * SUBLANES, LANES), jnp.uint32),
                        pltpu.VMEM((TOP_K, t), jnp.int32),
                        pltpu.SMEM((2, TOP_K, t), jnp.int32),
                        pltpu.SemaphoreType.DMA((2,)),
                        pltpu.SemaphoreType.DMA(())],
        compiler_params=_cparams(),
        name="mix",
    )(proj, u, ypre, x2, w_z, w_gates, ssd_ng, pool_w, pool_scale, w_br_pool, w_br_ssd, w_out, g1, ng, sh2, sc2, g2,
      w_router, router_bias, ws_gu, ws_down, jnp.zeros((1,), jnp.int32))


def _padfill_kernel(cnt_ref, xs_in_ref, xs_ref, padbuf, sem, *, cap):
    del xs_in_ref
    sub = lax.broadcasted_iota(jnp.int32, padbuf.shape, 0) & (SUBLANES - 1)
    padbuf[...] = jnp.where(sub == TOK_ROW, jnp.uint32(PAD_ID), jnp.uint32(0))
    sizes = [1 << bit for bit in range(EXPERT_BLOCK.bit_length() - 1)]

    def pad_copies(e, fn):
        c = cnt_ref[e]
        n_pad = (EXPERT_BLOCK - (c & (EXPERT_BLOCK - 1))) & (EXPERT_BLOCK - 1)
        first = e * cap + c
        for size in sizes:
            @pl.when((n_pad & size) != 0)
            def _():
                start = first + (n_pad & (size - 1))
                dst = xs_ref.at[pl.ds(pl.multiple_of(start * SUBLANES, SUBLANES), size * SUBLANES)]
                fn(pltpu.make_async_copy(padbuf.at[pl.ds(0, size * SUBLANES)], dst, sem))

    def issue(e, carry):
        pad_copies(e, lambda cp: cp.start())
        return carry

    lax.fori_loop(0, N_EXPERTS, issue, 0)

    def drain(e, carry):
        pad_copies(e, lambda cp: cp.wait())
        return carry

    lax.fori_loop(0, N_EXPERTS, drain, 0)


def _padfill(counts, xs, cap):
    return pl.pallas_call(
        functools.partial(_padfill_kernel, cap=cap),
        grid_spec=pltpu.PrefetchScalarGridSpec(
            num_scalar_prefetch=1,
            grid=(1,),
            in_specs=[pl.BlockSpec(memory_space=pl.ANY)],
            out_specs=pl.BlockSpec(memory_space=pl.ANY),
            scratch_shapes=[pltpu.VMEM((EXPERT_BLOCK // 2 * SUBLANES, LANES), jnp.uint32),
                            pltpu.SemaphoreType.DMA(())]),
        out_shape=jax.ShapeDtypeStruct(xs.shape, xs.dtype),
        input_output_aliases={1: 0},
        compiler_params=_cparams(),
        name="padfill",
    )(counts, xs)


def _expert_kernel(be_ref, br_ref, nb_ref, x_ref, wg_ref, wu_ref, wd_ref, ytok_ref, ybuf0, ybuf1, idv, ids0, ids1,
                   wgb, wub, wdb, sem, sem_ids, *, n_ids):
    del br_ref
    m = EXPERT_BLOCK
    b = pl.program_id(0)
    nb = nb_ref[0]
    ybufs, idss = (ybuf0, ybuf1), (ids0, ids1)
    n_phases = m // LANES
    half = D_MODEL // 2

    @pl.when((b == 0) | (be_ref[b] != be_ref[jnp.maximum(b - 1, 0)]))
    def _():
        wgb[...] = wg_ref[0].astype(BF16)
        wub[...] = wu_ref[0].astype(BF16)
        wdb[...] = wd_ref[0].astype(BF16)

    def issue(s, a0):
        for col in range(LANES):
            pltpu.make_async_copy(_tile_rows(ybufs[s], a0 * LANES + col),
                                  _tile_rows(ytok_ref, idss[s][a0, col]), sem.at[s]).start()

    def step(s, prev):
        tokrep = x_ref[pl.ds(TOK_ROW, m, stride=SUBLANES), :]
        e = be_ref[b]
        below_lo = jnp.where(e >= 32, -1, (1 << jnp.minimum(e, 31)) - 1).astype(jnp.uint32)
        below_hi = jnp.where(e >= 32, (1 << jnp.maximum(e - 32, 0)) - 1, 0).astype(jnp.uint32)
        rank = (lax.population_count(x_ref[pl.ds(MASK_ROWS[0], m, stride=SUBLANES), :] & below_lo)
                + lax.population_count(x_ref[pl.ds(MASK_ROWS[1], m, stride=SUBLANES), :] & below_hi))
        r = lax.broadcasted_iota(jnp.int32, (m, LANES), 0)
        ln = lax.broadcasted_iota(jnp.int32, (m, LANES), 1)
        n_tok = n_ids // TOP_K
        idi = jnp.where(tokrep == jnp.uint32(PAD_ID), n_ids + r,
                        rank.astype(jnp.int32) * n_tok + tokrep.astype(jnp.int32))
        diag = jnp.where((r & (LANES - 1)) == ln, idi, 0).astype(F32)
        idv[...] = jnp.sum(diag.reshape(m // LANES, LANES, LANES), axis=1).astype(jnp.int32)
        ids_copy = pltpu.make_async_copy(idv, idss[s], sem_ids)
        ids_copy.start()

        lo, hi = [], []
        for j in range(PACK_TILES):
            w = x_ref[pl.ds(j, m, stride=SUBLANES), :]
            lo.append(lax.bitcast_convert_type(w << 16, F32).astype(BF16))
            hi.append(lax.bitcast_convert_type(w & jnp.uint32(0xFFFF0000), F32).astype(BF16))
        x = jnp.concatenate(lo + hi, axis=1)
        phases = iter(range(n_phases))
        if prev is not None:
            issue(prev, next(phases))
        gate = jnp.dot(x, wgb[...], preferred_element_type=F32)
        if prev is not None:
            issue(prev, next(phases))
        up = jnp.dot(x, wub[...], preferred_element_type=F32)
        hid = (_silu(gate) * up).astype(BF16)

        def drain(which):
            pltpu.make_async_copy(ybufs[which], ytok_ref.at[pl.ds(0, m * SUBLANES)], sem.at[which]).wait()

        if prev is not None:
            @pl.when(b >= 2)
            def _():
                drain(s)
        for c0 in (0, half):
            if prev is not None:
                issue(prev, next(phases))
            y = jnp.dot(hid, wdb[:, c0:c0 + half], preferred_element_type=F32)
            for j in range(half // LANES):
                ybufs[s][pl.ds(c0 // LANES + j, m, stride=SUBLANES), :] = y[:, j * LANES:(j + 1) * LANES]
        ids_copy.wait()
        if prev is not None:
            for a0 in phases:
                issue(prev, a0)

            @pl.when(b == nb)
            def _():
                drain(prev)

    @pl.when(b == 0)
    def _():
        step(0, None)

    for s in range(2):
        @pl.when((b >= 1) & (b <= nb) & (lax.rem(b, 2) == s))
        def _():
            step(s, 1 - s)


def _experts(block_e, block_row, nb_used, xs, we_gate, we_up, we_down, n_ids):
    n_blocks = block_e.shape[0]
    m = EXPERT_BLOCK
    rows = m * SUBLANES
    return pl.pallas_call(
        functools.partial(_expert_kernel, n_ids=n_ids),
        grid_spec=pltpu.PrefetchScalarGridSpec(
            num_scalar_prefetch=3,
            grid=(n_blocks,),
            in_specs=[pl.BlockSpec((rows, LANES), lambda b, be, br, nb: (br[b], 0)),
                      pl.BlockSpec((1, D_MODEL, D_EXPERT), lambda b, be, br, nb: (be[b], 0, 0)),
                      pl.BlockSpec((1, D_MODEL, D_EXPERT), lambda b, be, br, nb: (be[b], 0, 0)),
                      pl.BlockSpec((1, D_EXPERT, D_MODEL), lambda b, be, br, nb: (be[b], 0, 0))],
            out_specs=pl.BlockSpec(memory_space=pl.ANY),
            scratch_shapes=[pltpu.VMEM((rows, LANES), F32),
                            pltpu.VMEM((rows, LANES), F32),
                            pltpu.VMEM((m // LANES, LANES), jnp.int32),
                            pltpu.SMEM((m // LANES, LANES), jnp.int32),
                            pltpu.SMEM((m // LANES, LANES), jnp.int32),
                            pltpu.VMEM((D_MODEL, D_EXPERT), BF16),
                            pltpu.VMEM((D_MODEL, D_EXPERT), BF16),
                            pltpu.VMEM((D_EXPERT, D_MODEL), BF16),
                            pltpu.SemaphoreType.DMA((2,)),
                            pltpu.SemaphoreType.DMA(())]),
        out_shape=jax.ShapeDtypeStruct(((n_ids + m) * SUBLANES, LANES), F32),
        compiler_params=_cparams(),
        name="experts",
    )(block_e, block_row, nb_used, xs, we_gate, we_up, we_down)


def _combine_kernel(*refs):
    y_refs = refs[:TOP_K]
    w8_ref, base_ref, g2_ref, fg_ref, o_ref = refs[TOP_K:]
    t7 = base_ref.shape[0]
    w8 = w8_ref[...]
    parts = []
    for j in range(ROW_TILES):
        acc = jnp.zeros((t7, LANES), F32)
        for k in range(TOP_K):
            acc = acc + w8[:, k:k + 1] * y_refs[k][pl.ds(j, t7, stride=SUBLANES), :]
        parts.append(acc)
    routed = jnp.concatenate(parts, axis=1)
    h = base_ref[...] + g2_ref[...] * routed
    inv = lax.rsqrt(jnp.mean(h * h, axis=-1, keepdims=True) + EPS)
    o_ref[...] = h * inv * fg_ref[...]


def _combine(ytok, w8, base, g2, fg):
    n = base.shape[0]
    t7 = 128
    y_specs = [pl.BlockSpec((t7 * SUBLANES, LANES), functools.partial(lambda i, k: (k * (n // t7) + i, 0), k=k))
               for k in range(TOP_K)]
    return pl.pallas_call(
        _combine_kernel,
        grid=(n // t7,),
        in_specs=y_specs + [
                  pl.BlockSpec((t7, TOP_K), lambda i: (i, 0)),
                  pl.BlockSpec((t7, D_MODEL), lambda i: (i, 0)),
                  _full((1, D_MODEL)), _full((1, D_MODEL))],
        out_specs=pl.BlockSpec((t7, D_MODEL), lambda i: (i, 0)),
        out_shape=jax.ShapeDtypeStruct((n, D_MODEL), F32),
        compiler_params=_cparams(),
        name="combine",
    )(*([ytok] * TOP_K), w8, base, g2, fg)


def kernel(x, c, w_ada, b_ada, norm_mix_g, w_in, conv_w, conv_b, dt_bias, A_log, D_skip, ssd_norm_g, pool_w,
           pool_scale, w_br_ssd, w_br_pool, w_out, norm_ffn_g, w_router, router_bias, we_gate, we_up, we_down,
           ws_gate, ws_up, ws_down, final_norm_g):
    bsz, seq, _ = x.shape
    assert bsz == 1 and w_ada.shape[0] == 1
    n = seq
    h2 = x.reshape(n, D_MODEL)

    mod = _ada(c, w_ada[0], b_ada[0])
    sh1, sc1, g1, sh2, sc2, g2 = [mod[:, k * D_MODEL:(k + 1) * D_MODEL] for k in range(6)]

    wi = w_in[0]
    o_xbc, o_dt = D_SSD, D_SSD + D_SSD + 2 * D_BC
    o_pool = o_dt + N_HEADS
    o_gates = o_pool + D_MODEL
    w_dt = jnp.pad(wi[:, o_dt:o_pool], ((0, 0), (0, LANES - N_HEADS))).astype(BF16)
    proj, dt_raw, u = _in_proj(h2, norm_mix_g[0][None], sh1, sc1, wi[:, o_xbc:o_dt].astype(BF16),
                               wi[:, o_pool:o_gates].astype(BF16), w_dt)

    pad_h = lambda v: jnp.pad(v[None], ((0, 0), (0, LANES - N_HEADS)))
    ypre = _ssd(proj, dt_raw, conv_w[0], conv_b[0][None], pad_h(dt_bias[0]), pad_h(A_log[0]),
                jnp.repeat(D_skip[0], HEADDIM)[None])

    ws_gu = jnp.concatenate([ws_gate[0], ws_up[0]], axis=1).astype(BF16)
    m = EXPERT_BLOCK
    cap = -(-n // m) * m
    base, w8, cnt, xs = _mix(
        proj, u, ypre, h2, wi[:, :o_xbc].astype(BF16), wi[:, o_gates:].astype(BF16), ssd_norm_g[0][None],
        pool_w[0].astype(BF16), pool_scale[0][None], w_br_pool[0].astype(BF16),
        w_br_ssd[0].astype(BF16), w_out[0].astype(BF16), g1, norm_ffn_g[0][None], sh2, sc2, g2,
        w_router[0].astype(BF16), router_bias[0][None], ws_gu, ws_down[0].astype(BF16), cap)

    counts = cnt[0].astype(jnp.int32)
    nblk = (counts + m - 1) // m
    bends = jnp.cumsum(nblk)
    bstarts = bends - nblk
    nb_used = bends[-1]
    n_blocks = -(-(n * TOP_K) // m) + N_EXPERTS + 1
    b_eff = jnp.minimum(jnp.arange(n_blocks, dtype=jnp.int32), nb_used - 1)
    done = bends[None, :] <= b_eff[:, None]
    block_e = jnp.sum(done.astype(jnp.int32), axis=1)
    block_row = block_e * (cap // m) + b_eff - jnp.sum(jnp.where(done, nblk[None, :], 0), axis=1)

    xs = _padfill(counts, xs, cap)
    ytok = _experts(block_e, block_row, nb_used.reshape(1), xs,
                    we_gate[0], we_up[0], we_down[0], n * TOP_K)
    out = _combine(ytok, w8, base, g2, final_norm_g[None])
    return out.reshape(bsz, seq, D_MODEL)
```

```python
import functools

import jax
import jax.numpy as jnp
from jax import lax
from jax.experimental import pallas as pl
from jax.experimental.pallas import tpu as pltpu

F32 = jnp.float32
BF16 = jnp.bfloat16
HIGHEST = lax.Precision.HIGHEST

D_MODEL = 1024
D_SSD = 2048
HEADDIM = 64
N_HEADS = 32
N_GROUPS = 8
HEADS_PER_GROUP = N_HEADS // N_GROUPS
D_STATE = 128
CONV_K = 4
CHUNK = 128
SSD_CHUNKS_PER_STEP = 2
GROUP_W = D_SSD // N_GROUPS
D_BC = N_GROUPS * D_STATE
POOL_WINDOWS = (2, 4, 8, 16)
POOL_GDIM = 256
N_EXPERTS = 64
TOP_K = 8
N_EGROUPS = 8
EXPERTS_PER_GROUP = 8
TOPK_GROUPS = 4
D_EXPERT = 256
D_SHARED = 256
ROUTED_SCALE = 2.5
MOE_BLOCK = 128
EPS = 1e-6

LANES = 128
SUBLANES = 8
ROW_TILES = D_MODEL // LANES
PACK_TILES = ROW_TILES // 2
PACK_W = PACK_TILES * LANES
EXPERT_BLOCK = 512
MIX_ISSUE_GROUPS = 32
BEFORE_STAGING = "before_staging"

PROJ_W = D_SSD + 2 * D_BC + D_MODEL
PROJ_CHUNK = 512

VMEM_LIMIT = 56 * 1024 * 1024


def _cparams(sem=("arbitrary",)):
    return pltpu.CompilerParams(dimension_semantics=sem, vmem_limit_bytes=VMEM_LIMIT)


def _full(shape):
    nd = len(shape)
    return pl.BlockSpec(shape, lambda *_: (0,) * nd)


def _silu(v):
    return v * jax.nn.sigmoid(v)


def _ada_kernel(c_ref, w_ref, b_ref, o_ref):
    c = c_ref[...]
    o_ref[...] = jnp.dot(_silu(c), w_ref[...], preferred_element_type=F32, precision=HIGHEST) + b_ref[...]


def _ada(c, w_ada, b_ada):
    n_out = w_ada.shape[1]
    tn = 1536
    c8 = jnp.broadcast_to(c, (SUBLANES, D_MODEL))
    out = pl.pallas_call(
        _ada_kernel,
        grid=(n_out // tn,),
        in_specs=[_full((SUBLANES, D_MODEL)),
                  pl.BlockSpec((D_MODEL, tn), lambda j: (0, j)),
                  pl.BlockSpec((1, tn), lambda j: (0, j))],
        out_specs=pl.BlockSpec((SUBLANES, tn), lambda j: (0, j)),
        out_shape=jax.ShapeDtypeStruct((SUBLANES, n_out), F32),
        compiler_params=_cparams(),
        name="ada",
    )(c8, w_ada, b_ada.reshape(1, n_out))
    return out[0:1]


def _inproj_kernel(x_ref, g_ref, sh_ref, sc_ref, wa_ref, wb_ref, wdt_ref, proj_ref, dt_ref, u_ref):
    x = x_ref[...]
    inv = lax.rsqrt(jnp.mean(x * x, axis=-1, keepdims=True) + EPS)
    u = x * inv * g_ref[...]
    u = u * (1.0 + sc_ref[...]) + sh_ref[...]
    ub = u.astype(BF16)
    u_ref[...] = ub
    wa = wa_ref.shape[1]
    for c0 in range(0, PROJ_W, PROJ_CHUNK):
        w = wa_ref[:, c0:c0 + PROJ_CHUNK] if c0 < wa else wb_ref[:, c0 - wa:c0 - wa + PROJ_CHUNK]
        proj_ref[:, c0:c0 + PROJ_CHUNK] = jnp.dot(ub, w, preferred_element_type=F32).astype(BF16)
    dt_ref[...] = jnp.dot(ub, wdt_ref[...], preferred_element_type=F32)


def _in_proj(x2, g, sh, sc, w_a, w_b, w_dt):
    n = x2.shape[0]
    tm = 512
    return pl.pallas_call(
        _inproj_kernel,
        grid=(n // tm,),
        in_specs=[pl.BlockSpec((tm, D_MODEL), lambda i: (i, 0)),
                  _full((1, D_MODEL)), _full((1, D_MODEL)), _full((1, D_MODEL)),
                  pl.BlockSpec(w_a.shape, lambda i: (0, 0), pipeline_mode=pl.Buffered(1)),
                  pl.BlockSpec(w_b.shape, lambda i: (0, 0), pipeline_mode=pl.Buffered(1)),
                  _full((D_MODEL, LANES))],
        out_specs=[pl.BlockSpec((tm, PROJ_W), lambda i: (i, 0)),
                   pl.BlockSpec((tm, LANES), lambda i: (i, 0)),
                   pl.BlockSpec((tm, D_MODEL), lambda i: (i, 0))],
        out_shape=[jax.ShapeDtypeStruct((n, PROJ_W), BF16),
                   jax.ShapeDtypeStruct((n, LANES), F32),
                   jax.ShapeDtypeStruct((n, D_MODEL), BF16)],
        compiler_params=_cparams(),
        name="in_proj",
    )(x2, g, sh, sc, w_a, w_b, w_dt)


def _conv_silu(cur_ref, ext_ref, w_ref, b_ref, out_ref, width, cw=512):
    t = cur_ref.shape[0]
    for c0 in range(0, width, cw):
        sl = slice(c0, c0 + cw)
        cur = cur_ref[:, sl].astype(F32)
        ext_ref[SUBLANES:, sl] = cur
        acc = cur * w_ref[CONV_K - 1:CONV_K, sl] + b_ref[:, sl]
        for s in range(1, CONV_K):
            acc = acc + ext_ref[pl.ds(SUBLANES - s, t), sl] * w_ref[CONV_K - 1 - s:CONV_K - s, sl]
        out_ref[:, sl] = _silu(acc).astype(out_ref.dtype)
        ext_ref[0:SUBLANES, sl] = cur[t - SUBLANES:t]


def _ssd_kernel(xs_ref, bc_ref, dt_ref, cwx_ref, cwbc_ref, cbx_ref, cbbc_ref, dtb_ref, alog_ref,
                dskip_ref, o_ref, tailx, tailbc, state, xc, bcc):
    q = CHUNK
    pair_w = 2 * HEADDIM

    @pl.when(pl.program_id(0) == 0)
    def _():
        tailx[...] = jnp.zeros_like(tailx)
        tailbc[...] = jnp.zeros_like(tailbc)
        state[...] = jnp.zeros_like(state)

    _conv_silu(xs_ref, tailx, cwx_ref, cbx_ref, xc, D_SSD)
    _conv_silu(bc_ref, tailbc, cwbc_ref, cbbc_ref, bcc, 2 * D_BC)

    for c in range(xs_ref.shape[0] // q):
        rows = slice(c * q, (c + 1) * q)
        v = dt_ref[rows, :] + dtb_ref[...]
        dt = jnp.maximum(v, 0.0) + jnp.log(1.0 + jnp.exp(-jnp.abs(v)))
        a = dt * (-jnp.exp(alog_ref[...]))
        ri = lax.broadcasted_iota(jnp.int32, (q, q), 0)
        ci = lax.broadcasted_iota(jnp.int32, (q, q), 1)
        causal = ri >= ci
        a_cs = jnp.dot(causal.astype(F32), a, preferred_element_type=F32, precision=HIGHEST)
        a_cs_t = a_cs.T
        dt_t = dt.T
        first_half = ci < HEADDIM
        first_half_row = first_half[0:1, :]

        for g in range(N_GROUPS):
            b_g = bcc[rows, g * D_STATE:(g + 1) * D_STATE]
            c_g = bcc[rows, D_BC + g * D_STATE:D_BC + (g + 1) * D_STATE]
            cb = lax.dot_general(c_g, b_g, (((1,), (1,)), ((), ())), preferred_element_type=F32)
            b_t = b_g.astype(F32).T
            st_g = state[g]
            y_off = jnp.dot(c_g, st_g.astype(BF16), preferred_element_type=F32)
            for pi in range(HEADS_PER_GROUP // 2):
                h0 = g * HEADS_PER_GROUP + 2 * pi
                lanes = slice(h0 * HEADDIM, h0 * HEADDIM + pair_w)
                gl = slice(pi * pair_w, (pi + 1) * pair_w)
                ms, ws, cols, lasts = [], [], [], []
                for h in (h0, h0 + 1):
                    col = jnp.broadcast_to(a_cs[:, h:h + 1], (q, q))
                    row = a_cs_t[h:h + 1, :]
                    dtrow = dt_t[h:h + 1, :]
                    decay = jnp.exp(jnp.where(causal, col - row, -jnp.inf))
                    ms.append((cb * decay * dtrow).astype(BF16))
                    last = a_cs_t[h:h + 1, q - 1:q]
                    ws.append((b_t * (jnp.exp(last - row) * dtrow)).astype(BF16))
                    cols.append(col)
                    lasts.append(last)
                xp = xc[rows, lanes]
                zero = jnp.zeros_like(xp)
                x_bd = jnp.concatenate([jnp.where(first_half, xp, zero), jnp.where(first_half, zero, xp)], axis=0)
                y_p = jnp.dot(jnp.concatenate(ms, axis=1), x_bd, preferred_element_type=F32)
                y_p = y_p + y_off[:, gl] * jnp.exp(jnp.where(first_half, cols[0], cols[1]))
                o_ref[rows, lanes] = (y_p + dskip_ref[:, lanes] * xp.astype(F32)).astype(BF16)
                s_new = jnp.dot(jnp.concatenate(ws, axis=1), x_bd, preferred_element_type=F32)
                carry = jnp.exp(jnp.where(first_half_row, lasts[0], lasts[1]))
                state[g, :, gl] = st_g[:, gl] * carry + s_new


def _ssd(proj, dt_raw, conv_w, conv_b, dt_bias, a_log, d_skip):
    n = proj.shape[0]
    t = SSD_CHUNKS_PER_STEP * CHUNK
    return pl.pallas_call(
        _ssd_kernel,
        grid=(n // t,),
        in_specs=[pl.BlockSpec((t, D_SSD), lambda i: (i, 0)),
                  pl.BlockSpec((t, 2 * D_BC), lambda i: (i, 1)),
                  pl.BlockSpec((t, LANES), lambda i: (i, 0)),
                  pl.BlockSpec((CONV_K, D_SSD), lambda i: (0, 0)),
                  pl.BlockSpec((CONV_K, 2 * D_BC), lambda i: (0, 1)),
                  pl.BlockSpec((1, D_SSD), lambda i: (0, 0)),
                  pl.BlockSpec((1, 2 * D_BC), lambda i: (0, 1)),
                  _full((1, LANES)), _full((1, LANES)),
                  _full((1, D_SSD))],
        out_specs=pl.BlockSpec((t, D_SSD), lambda i: (i, 0)),
        out_shape=jax.ShapeDtypeStruct((n, D_SSD), BF16),
        scratch_shapes=[pltpu.VMEM((SUBLANES + t, D_SSD), F32),
                        pltpu.VMEM((SUBLANES + t, 2 * D_BC), F32),
                        pltpu.VMEM((N_GROUPS, D_STATE, GROUP_W), F32),
                        pltpu.VMEM((t, D_SSD), BF16),
                        pltpu.VMEM((t, 2 * D_BC), BF16)],
        compiler_params=_cparams(),
        name="ssd",
    )(proj, proj, dt_raw, conv_w, conv_w, conv_b, conv_b, dt_bias, a_log, d_skip)


PAD_ID = 0xFFFFFFFF
TOK_ROW = PACK_TILES
MASK_ROWS = (PACK_TILES + 1, PACK_TILES + 2)


def _tile_rows(ref, first_row):
    return ref.at[pl.ds(pl.multiple_of(first_row * SUBLANES, SUBLANES), SUBLANES)]


def _mix_kernel(xp_ref, u_ref, ypre_ref, x_ref, wz_ref, wgt_ref, sng_ref, poolw_ref, pscale_ref, wbp_ref, wbs_ref, wout_ref,
                g1_ref, ng_ref, sh2_ref, sc2_ref, g2_ref, wr_ref, rb_ref, wsgu_ref, wsd_ref, zero_ref,
                base_ref, ws_ref, cnt_ref, xs_ref, ptail, run, stage, dvm, dsm, sem, sem_s, *, n_tiles, cap):
    t = x_ref.shape[0]
    i = pl.program_id(0)
    slot = lax.rem(i, 2)

    @pl.when(i == 0)
    def _():
        ptail[...] = jnp.zeros_like(ptail)
        run[...] = jnp.zeros_like(run)
        stage[...] = jnp.zeros_like(stage)

    def issue_rows(s, group, after=None):
        zero = 0
        if after is not None:
            probe = jnp.max(jnp.abs(after[0:SUBLANES, 0:min(LANES, after.shape[1])])).astype(jnp.int32)
            zero = probe * zero_ref[0]
        per = t // MIX_ISSUE_GROUPS
        for tt in range(group * per, (group + 1) * per):
            for k in range(TOP_K):
                pltpu.make_async_copy(stage.at[s, pl.ds(tt * SUBLANES, SUBLANES)],
                                      _tile_rows(xs_ref, dsm[s, k, tt] + zero), sem.at[s]).start()

    def drain_rows(s):
        for _ in range(TOP_K):
            pltpu.make_async_copy(stage.at[s], xs_ref.at[pl.ds(0, t * SUBLANES)], sem.at[s]).wait()

    def tile(s, milestone=None):
        _mix_tile_body(xp_ref, u_ref, ypre_ref, x_ref, wz_ref, wgt_ref, sng_ref, poolw_ref, pscale_ref, wbp_ref, wbs_ref, wout_ref,
                       g1_ref, ng_ref, sh2_ref, sc2_ref, g2_ref, wr_ref, rb_ref, wsgu_ref, wsd_ref,
                       base_ref, ws_ref, cnt_ref, ptail, run, stage, dvm, dsm, sem_s, i, s, cap, milestone)

    @pl.when(i == 0)
    def _():
        tile(0)

    groups = iter(range(1, MIX_ISSUE_GROUPS))

    def milestone(value):
        if value is BEFORE_STAGING:
            assert next(groups, None) is None
            @pl.when(i >= 2)
            def _():
                drain_rows(slot)
        else:
            issue_rows(1 - slot, next(groups), value)

    def wait_slot_rows(s):
        pltpu.make_async_copy(dvm, dsm.at[s], sem_s).wait()

    @pl.when((i >= 1) & (i < n_tiles))
    def _():
        wait_slot_rows(1 - slot)
        issue_rows(1 - slot, 0)
        tile(slot, milestone)

    @pl.when(i == n_tiles)
    def _():
        wait_slot_rows(1 - slot)
        for group in range(MIX_ISSUE_GROUPS):
            issue_rows(1 - slot, group)
        if n_tiles >= 2:
            drain_rows(slot)
        drain_rows(1 - slot)


def _mix_tile_body(xp_ref, u_ref, ypre_ref, x_ref, wz_ref, wgt_ref, sng_ref, poolw_ref, pscale_ref, wbp_ref, wbs_ref, wout_ref,
                   g1_ref, ng_ref, sh2_ref, sc2_ref, g2_ref, wr_ref, rb_ref, wsgu_ref, wsd_ref,
                   base_ref, ws_ref, cnt_ref, ptail, run, stage, dvm, dsm, sem_s, i, slot, cap, milestone):
    t = x_ref.shape[0]
    if milestone is None:
        milestone = lambda value: None

    xp = xp_ref[...].astype(F32)
    ext = jnp.concatenate([ptail[...], xp], axis=0)
    ptail[...] = xp[t - 2 * SUBLANES:t]
    pos = (lax.broadcasted_iota(jnp.int32, (t, 1), 0) + (i * t + 1)).astype(F32)
    pooled = []
    for gi, w in enumerate(POOL_WINDOWS):
        sl = slice(gi * POOL_GDIM, (gi + 1) * POOL_GDIM)
        e = ext[:, sl]
        s = e
        span = 1
        while span < w:
            s = s + pltpu.roll(s, span, axis=0)
            span *= 2
        win = s[2 * SUBLANES:]
        mean = win / jnp.minimum(pos, float(w))
        pg = (mean - xp[:, sl]).astype(BF16)
        mixed = jnp.dot(pg, poolw_ref[gi], preferred_element_type=F32)
        pooled.append(mixed * pscale_ref[:, sl])
        milestone(pooled[-1])
    pooled = jnp.concatenate(pooled, axis=1).astype(BF16)
    y_pool = jnp.dot(pooled, wbp_ref[...], preferred_element_type=F32)
    milestone(y_pool)
    ub = u_ref[...]
    z = jnp.dot(ub, wz_ref[...], preferred_element_type=F32)
    milestone(z)
    yn = []
    for g in range(N_GROUPS):
        sl = slice(g * GROUP_W, (g + 1) * GROUP_W)
        yg = ypre_ref[:, sl].astype(F32) * _silu(z[:, sl])
        inv_g = lax.rsqrt(jnp.mean(yg * yg, axis=-1, keepdims=True) + EPS)
        yn.append((yg * inv_g * sng_ref[:, sl]).astype(BF16))
        milestone(inv_g)
    y_ssd = jnp.dot(jnp.concatenate(yn, axis=1), wbs_ref[...], preferred_element_type=F32)
    milestone(y_ssd)
    gates = jnp.dot(ub, wgt_ref[...], preferred_element_type=F32)
    milestone(gates)
    g_ssd = jax.nn.sigmoid(gates[:, :D_MODEL])
    g_pool = jax.nn.sigmoid(gates[:, D_MODEL:])
    mixed = (g_ssd * y_ssd + g_pool * y_pool).astype(BF16)
    h = x_ref[...] + g1_ref[...] * jnp.dot(mixed, wout_ref[...], preferred_element_type=F32)
    milestone(h)

    inv = lax.rsqrt(jnp.mean(h * h, axis=-1, keepdims=True) + EPS)
    milestone(inv)
    u2 = h * inv * ng_ref[...]
    u2 = u2 * (1.0 + sc2_ref[...]) + sh2_ref[...]
    u2b = u2.astype(BF16)
    u2r = u2b.astype(F32)
    packed = []
    for j in range(PACK_TILES):
        lo = lax.bitcast_convert_type(u2r[:, j * LANES:(j + 1) * LANES], jnp.uint32)
        hi = lax.bitcast_convert_type(u2r[:, (j + PACK_TILES) * LANES:(j + PACK_TILES + 1) * LANES], jnp.uint32)
        packed.append((lo >> 16) | (hi & jnp.uint32(0xFFFF0000)))

    hs = jnp.dot(u2b, wsgu_ref[...], preferred_element_type=F32)
    milestone(hs)
    act = (_silu(hs[:, :D_SHARED]) * hs[:, D_SHARED:]).astype(BF16)
    shared = jnp.dot(act, wsd_ref[...], preferred_element_type=F32)
    milestone(shared)
    base_ref[...] = h + g2_ref[...] * shared

    logits = jnp.dot(u2b, wr_ref[...], preferred_element_type=F32)
    scores = jax.nn.sigmoid(logits)
    milestone(scores)
    choice = scores + rb_ref[...]
    lane = lax.broadcasted_iota(jnp.int32, (t, N_EXPERTS), 1)
    lane_f = lane.astype(F32)
    lane_grp = lane // EXPERTS_PER_GROUP
    neg = -jnp.inf
    gscore = []
    for g in range(N_EGROUPS):
        vg = jnp.where(lane_grp == g, choice, neg)
        m1 = jnp.max(vg, axis=-1, keepdims=True)
        i1 = jnp.min(jnp.where(vg == m1, lane_f, float(N_EXPERTS)), axis=-1, keepdims=True)
        m2 = jnp.max(jnp.where(lane_f == i1, neg, vg), axis=-1, keepdims=True)
        gscore.append(m1 + m2)
        milestone(gscore[-1])
    gmask = jnp.zeros((t, N_EXPERTS), jnp.bool_)
    for g in range(N_EGROUPS):
        rank = jnp.zeros((t, 1), F32)
        for g2 in range(N_EGROUPS):
            if g2 == g:
                continue
            better = (gscore[g2] > gscore[g]) | ((gscore[g2] == gscore[g]) & (g2 < g))
            rank = rank + better.astype(F32)
        gmask = gmask | ((rank < float(TOPK_GROUPS)) & (lane_grp == g))
    work = jnp.where(gmask, choice, neg)
    sel = jnp.zeros((t, N_EXPERTS), jnp.bool_)
    onehots, idxs, sks = [], [], []
    for k in range(TOP_K):
        m = jnp.max(work, axis=-1, keepdims=True)
        idx = jnp.min(jnp.where(work == m, lane_f, float(N_EXPERTS)), axis=-1, keepdims=True)
        oh = lane_f == idx
        onehots.append(oh)
        idxs.append(idx)
        sks.append(jnp.sum(jnp.where(oh, scores, 0.0), axis=-1, keepdims=True))
        if k == TOP_K // 2 - 1:
            milestone(sks[-1])
        sel = sel | oh
        work = jnp.where(oh, neg, work)
    denom = sks[0]
    for k in range(1, TOP_K):
        denom = denom + sks[k]
    milestone(denom)

    ri = lax.broadcasted_iota(jnp.int32, (t, t), 0)
    ci = lax.broadcasted_iota(jnp.int32, (t, t), 1)
    before = (ri > ci).astype(BF16)
    sel_f = jnp.where(sel, 1.0, 0.0)
    sel_b = sel_f.astype(BF16)
    pos_tile = jnp.dot(before, sel_b, preferred_element_type=F32) + run[...]
    er = lax.broadcasted_iota(jnp.int32, (N_EXPERTS, N_EXPERTS), 0)
    ec = lax.broadcasted_iota(jnp.int32, (N_EXPERTS, N_EXPERTS), 1)
    rank_tile = jnp.dot(sel_b, (er < ec).astype(BF16), preferred_element_type=F32)
    lane128 = lax.broadcasted_iota(jnp.int32, (t, LANES), 1)
    k_iota = lax.broadcasted_iota(jnp.int32, (t, TOP_K), 1).astype(F32)
    dest = jnp.zeros((t, LANES), F32)
    w_sorted = jnp.zeros((t, TOP_K), F32)
    for k in range(TOP_K):
        pk = jnp.sum(jnp.where(onehots[k], pos_tile, 0.0), axis=-1, keepdims=True)
        rk = jnp.sum(jnp.where(onehots[k], rank_tile, 0.0), axis=-1, keepdims=True)
        dest = jnp.where(lane128 == k, idxs[k] * float(cap) + pk, dest)
        w_sorted = jnp.where(k_iota == rk, sks[k] / denom * ROUTED_SCALE, w_sorted)
    ws_ref[...] = w_sorted
    total = run[...] + jnp.sum(sel_f, axis=0, keepdims=True)
    run[...] = total
    cnt_ref[...] = total

    milestone(BEFORE_STAGING)
    for j in range(PACK_TILES):
        stage[slot, pl.ds(j, t, stride=SUBLANES), :] = packed[j]
    tok = lax.broadcasted_iota(jnp.int32, (t, LANES), 0) + i * t
    stage[slot, pl.ds(TOK_ROW, t, stride=SUBLANES), :] = tok.astype(jnp.uint32)
    bit = jnp.where(sel, jnp.left_shift(1, lane & 15), 0).astype(F32)
    words = []
    for q in range(N_EXPERTS // 16):
        part = jnp.sum(jnp.where((lane >> 4) == q, bit, 0.0), axis=-1, keepdims=True)
        words.append(part.astype(jnp.int32).astype(jnp.uint32))
    for w, row in enumerate(MASK_ROWS):
        word = words[2 * w] | (words[2 * w + 1] << 16)
        stage[slot, pl.ds(row, t, stride=SUBLANES), :] = jnp.broadcast_to(word, (t, LANES))

    dvm[...] = dest.T[0:TOP_K, :].astype(jnp.int32)
    pltpu.make_async_copy(dvm, dsm.at[slot], sem_s).start()


def _mix(proj, u, ypre, x2, w_z, w_gates, ssd_ng, pool_w, pool_scale, w_br_pool, w_br_ssd, w_out, g1, ng, sh2, sc2,
         g2, w_router, router_bias, ws_gu, ws_down, cap):
    n = x2.shape[0]
    t = 256
    n_tiles = n // t
    xp_blk = (D_SSD + 2 * D_BC) // D_MODEL
    const = lambda shape: pl.BlockSpec(shape, lambda i: (0,) * len(shape), pipeline_mode=pl.Buffered(1))
    tile = lambda i: jnp.minimum(i, n_tiles - 1)
    row = lambda i: (tile(i), 0)
    return pl.pallas_call(
        functools.partial(_mix_kernel, n_tiles=n_tiles, cap=cap),
        grid=(n_tiles + 1,),
        in_specs=[pl.BlockSpec((t, D_MODEL), lambda i: (tile(i), xp_blk)),
                  pl.BlockSpec((t, D_MODEL), row),
                  pl.BlockSpec((t, D_SSD), row),
                  pl.BlockSpec((t, D_MODEL), row),
                  const((D_MODEL, D_SSD)),
                  const((D_MODEL, 2 * D_MODEL)),
                  _full((1, D_SSD)),
                  const((len(POOL_WINDOWS), POOL_GDIM, POOL_GDIM)),
                  _full((1, D_MODEL)),
                  const((D_MODEL, D_MODEL)),
                  const((D_SSD, D_MODEL)),
                  const((D_MODEL, D_MODEL)),
                  _full((1, D_MODEL)), _full((1, D_MODEL)), _full((1, D_MODEL)), _full((1, D_MODEL)),
                  _full((1, D_MODEL)),
                  _full((D_MODEL, N_EXPERTS)),
                  _full((1, N_EXPERTS)),
                  const((D_MODEL, 2 * D_SHARED)),
                  const((D_SHARED, D_MODEL)),
                  pl.BlockSpec(memory_space=pltpu.SMEM)],
        out_specs=[pl.BlockSpec((t, D_MODEL), row),
                   pl.BlockSpec((t, TOP_K), row),
                   _full((1, N_EXPERTS)),
                   pl.BlockSpec(memory_space=pl.ANY)],
        out_shape=[jax.ShapeDtypeStruct((n, D_MODEL), F32),
                   jax.ShapeDtypeStruct((n, TOP_K), F32),
                   jax.ShapeDtypeStruct((1, N_EXPERTS), F32),
                   jax.ShapeDtypeStruct((N_EXPERTS * cap * SUBLANES, LANES), jnp.uint32)],
        scratch_shapes=[pltpu.VMEM((2 * SUBLANES, D_MODEL), F32),
                        pltpu.VMEM((1, N_EXPERTS), F32),
                        pltpu.VMEM((2, t * SUBLANES, LANES), jnp.uint32),
                        pltpu.VMEM((TOP_K, t), jnp.int32),
                        pltpu.SMEM((2, TOP_K, t), jnp.int32),
                        pltpu.SemaphoreType.DMA((2,)),
                        pltpu.SemaphoreType.DMA(())],
        compiler_params=_cparams(),
        name="mix",
    )(proj, u, ypre, x2, w_z, w_gates, ssd_ng, pool_w, pool_scale, w_br_pool, w_br_ssd, w_out, g1, ng, sh2, sc2, g2,
      w_router, router_bias, ws_gu, ws_down, jnp.zeros((1,), jnp.int32))


def _padfill_kernel(cnt_ref, xs_in_ref, xs_ref, padbuf, sem, *, cap):
    del xs_in_ref
    sub = lax.broadcasted_iota(jnp.int32, padbuf.shape, 0) & (SUBLANES - 1)
    padbuf[...] = jnp.where(sub == TOK_ROW, jnp.uint32(PAD_ID), jnp.uint32(0))
    sizes = [1 << bit for bit in range(EXPERT_BLOCK.bit_length() - 1)]

    def pad_copies(e, fn):
        c = cnt_ref[e]
        n_pad = (EXPERT_BLOCK - (c & (EXPERT_BLOCK - 1))) & (EXPERT_BLOCK - 1)
        first = e * cap + c
        for size in sizes:
            @pl.when((n_pad & size) != 0)
            def _():
                start = first + (n_pad & (size - 1))
                dst = xs_ref.at[pl.ds(pl.multiple_of(start * SUBLANES, SUBLANES), size * SUBLANES)]
                fn(pltpu.make_async_copy(padbuf.at[pl.ds(0, size * SUBLANES)], dst, sem))

    def issue(e, carry):
        pad_copies(e, lambda cp: cp.start())
        return carry

    lax.fori_loop(0, N_EXPERTS, issue, 0)

    def drain(e, carry):
        pad_copies(e, lambda cp: cp.wait())
        return carry

    lax.fori_loop(0, N_EXPERTS, drain, 0)


def _padfill(counts, xs, cap):
    return pl.pallas_call(
        functools.partial(_padfill_kernel, cap=cap),
        grid_spec=pltpu.PrefetchScalarGridSpec(
            num_scalar_prefetch=1,
            grid=(1,),
            in_specs=[pl.BlockSpec(memory_space=pl.ANY)],
            out_specs=pl.BlockSpec(memory_space=pl.ANY),
            scratch_shapes=[pltpu.VMEM((EXPERT_BLOCK // 2 * SUBLANES, LANES), jnp.uint32),
                            pltpu.SemaphoreType.DMA(())]),
        out_shape=jax.ShapeDtypeStruct(xs.shape, xs.dtype),
        input_output_aliases={1: 0},
        compiler_params=_cparams(),
        name="padfill",
    )(counts, xs)


def _expert_kernel(be_ref, br_ref, nb_ref, x_ref, wg_ref, wu_ref, wd_ref, ytok_ref, ybuf0, ybuf1, idv, ids0, ids1,
                   wgb, wub, wdb, sem, sem_ids, *, n_ids):
    del br_ref
    m = EXPERT_BLOCK
    b = pl.program_id(0)
    nb = nb_ref[0]
    ybufs, idss = (ybuf0, ybuf1), (ids0, ids1)
    n_phases = m // LANES
    half = D_MODEL // 2

    @pl.when((b == 0) | (be_ref[b] != be_ref[jnp.maximum(b - 1, 0)]))
    def _():
        wgb[...] = wg_ref[0].astype(BF16)
        wub[...] = wu_ref[0].astype(BF16)
        wdb[...] = wd_ref[0].astype(BF16)

    def issue(s, a0):
        for col in range(LANES):
            pltpu.make_async_copy(_tile_rows(ybufs[s], a0 * LANES + col),
                                  _tile_rows(ytok_ref, idss[s][a0, col]), sem.at[s]).start()

    def step(s, prev):
        tokrep = x_ref[pl.ds(TOK_ROW, m, stride=SUBLANES), :]
        e = be_ref[b]
        below_lo = jnp.where(e >= 32, -1, (1 << jnp.minimum(e, 31)) - 1).astype(jnp.uint32)
        below_hi = jnp.where(e >= 32, (1 << jnp.maximum(e - 32, 0)) - 1, 0).astype(jnp.uint32)
        rank = (lax.population_count(x_ref[pl.ds(MASK_ROWS[0], m, stride=SUBLANES), :] & below_lo)
                + lax.population_count(x_ref[pl.ds(MASK_ROWS[1], m, stride=SUBLANES), :] & below_hi))
        r = lax.broadcasted_iota(jnp.int32, (m, LANES), 0)
        ln = lax.broadcasted_iota(jnp.int32, (m, LANES), 1)
        n_tok = n_ids // TOP_K
        idi = jnp.where(tokrep == jnp.uint32(PAD_ID), n_ids + r,
                        rank.astype(jnp.int32) * n_tok + tokrep.astype(jnp.int32))
        diag = jnp.where((r & (LANES - 1)) == ln, idi, 0).astype(F32)
        idv[...] = jnp.sum(diag.reshape(m // LANES, LANES, LANES), axis=1).astype(jnp.int32)
        ids_copy = pltpu.make_async_copy(idv, idss[s], sem_ids)
        ids_copy.start()

        lo, hi = [], []
        for j in range(PACK_TILES):
            w = x_ref[pl.ds(j, m, stride=SUBLANES), :]
            lo.append(lax.bitcast_convert_type(w << 16, F32).astype(BF16))
            hi.append(lax.bitcast_convert_type(w & jnp.uint32(0xFFFF0000), F32).astype(BF16))
        x = jnp.concatenate(lo + hi, axis=1)
        phases = iter(range(n_phases))
        if prev is not None:
            issue(prev, next(phases))
        gate = jnp.dot(x, wgb[...], preferred_element_type=F32)
        if prev is not None:
            issue(prev, next(phases))
        up = jnp.dot(x, wub[...], preferred_element_type=F32)
        hid = (_silu(gate) * up).astype(BF16)

        def drain(which):
            pltpu.make_async_copy(ybufs[which], ytok_ref.at[pl.ds(0, m * SUBLANES)], sem.at[which]).wait()

        if prev is not None:
            @pl.when(b >= 2)
            def _():
                drain(s)
        for c0 in (0, half):
            if prev is not None:
                issue(prev, next(phases))
            y = jnp.dot(hid, wdb[:, c0:c0 + half], preferred_element_type=F32)
            for j in range(half // LANES):
                ybufs[s][pl.ds(c0 // LANES + j, m, stride=SUBLANES), :] = y[:, j * LANES:(j + 1) * LANES]
        ids_copy.wait()
        if prev is not None:
            for a0 in phases:
                issue(prev, a0)

            @pl.when(b == nb)
            def _():
                drain(prev)

    @pl.when(b == 0)
    def _():
        step(0, None)

    for s in range(2):
        @pl.when((b >= 1) & (b <= nb) & (lax.rem(b, 2) == s))
        def _():
            step(s, 1 - s)


def _experts(block_e, block_row, nb_used, xs, we_gate, we_up, we_down, n_ids):
    n_blocks = block_e.shape[0]
    m = EXPERT_BLOCK
    rows = m * SUBLANES
    return pl.pallas_call(
        functools.partial(_expert_kernel, n_ids=n_ids),
        grid_spec=pltpu.PrefetchScalarGridSpec(
            num_scalar_prefetch=3,
            grid=(n_blocks,),
            in_specs=[pl.BlockSpec((rows, LANES), lambda b, be, br, nb: (br[b], 0)),
                      pl.BlockSpec((1, D_MODEL, D_EXPERT), lambda b, be, br, nb: (be[b], 0, 0)),
                      pl.BlockSpec((1, D_MODEL, D_EXPERT), lambda b, be, br, nb: (be[b], 0, 0)),
                      pl.BlockSpec((1, D_EXPERT, D_MODEL), lambda b, be, br, nb: (be[b], 0, 0))],
            out_specs=pl.BlockSpec(memory_space=pl.ANY),
            scratch_shapes=[pltpu.VMEM((rows, LANES), F32),
                            pltpu.VMEM((rows, LANES), F32),
                            pltpu.VMEM((m // LANES, LANES), jnp.int32),
                            pltpu.SMEM((m // LANES, LANES), jnp.int32),
                            pltpu.SMEM((m // LANES, LANES), jnp.int32),
                            pltpu.VMEM((D_MODEL, D_EXPERT), BF16),
                            pltpu.VMEM((D_MODEL, D_EXPERT), BF16),
                            pltpu.VMEM((D_EXPERT, D_MODEL), BF16),
                            pltpu.SemaphoreType.DMA((2,)),
                            pltpu.SemaphoreType.DMA(())]),
        out_shape=jax.ShapeDtypeStruct(((n_ids + m) * SUBLANES, LANES), F32),
        compiler_params=_cparams(),
        name="experts",
    )(block_e, block_row, nb_used, xs, we_gate, we_up, we_down)


def _combine_kernel(*refs):
    y_refs = refs[:TOP_K]
    w8_ref, base_ref, g2_ref, fg_ref, o_ref = refs[TOP_K:]
    t7 = base_ref.shape[0]
    w8 = w8_ref[...]
    parts = []
    for j in range(ROW_TILES):
        acc = jnp.zeros((t7, LANES), F32)
        for k in range(TOP_K):
            acc = acc + w8[:, k:k + 1] * y_refs[k][pl.ds(j, t7, stride=SUBLANES), :]
        parts.append(acc)
    routed = jnp.concatenate(parts, axis=1)
    h = base_ref[...] + g2_ref[...] * routed
    inv = lax.rsqrt(jnp.mean(h * h, axis=-1, keepdims=True) + EPS)
    o_ref[...] = h * inv * fg_ref[...]


def _combine(ytok, w8, base, g2, fg):
    n = base.shape[0]
    t7 = 256
    y_specs = [pl.BlockSpec((t7 * SUBLANES, LANES), functools.partial(lambda i, k: (k * (n // t7) + i, 0), k=k))
               for k in range(TOP_K)]
    return pl.pallas_call(
        _combine_kernel,
        grid=(n // t7,),
        in_specs=y_specs + [
                  pl.BlockSpec((t7, TOP_K), lambda i: (i, 0)),
                  pl.BlockSpec((t7, D_MODEL), lambda i: (i, 0)),
                  _full((1, D_MODEL)), _full((1, D_MODEL))],
        out_specs=pl.BlockSpec((t7, D_MODEL), lambda i: (i, 0)),
        out_shape=jax.ShapeDtypeStruct((n, D_MODEL), F32),
        compiler_params=_cparams(),
        name="combine",
    )(*([ytok] * TOP_K), w8, base, g2, fg)


def kernel(x, c, w_ada, b_ada, norm_mix_g, w_in, conv_w, conv_b, dt_bias, A_log, D_skip, ssd_norm_g, pool_w,
           pool_scale, w_br_ssd, w_br_pool, w_out, norm_ffn_g, w_router, router_bias, we_gate, we_up, we_down,
           ws_gate, ws_up, ws_down, final_norm_g):
    bsz, seq, _ = x.shape
    assert bsz == 1 and w_ada.shape[0] == 1
    n = seq
    h2 = x.reshape(n, D_MODEL)

    mod = _ada(c, w_ada[0], b_ada[0])
    sh1, sc1, g1, sh2, sc2, g2 = [mod[:, k * D_MODEL:(k + 1) * D_MODEL] for k in range(6)]

    wi = w_in[0]
    o_xbc, o_dt = D_SSD, D_SSD + D_SSD + 2 * D_BC
    o_pool = o_dt + N_HEADS
    o_gates = o_pool + D_MODEL
    w_dt = jnp.pad(wi[:, o_dt:o_pool], ((0, 0), (0, LANES - N_HEADS))).astype(BF16)
    proj, dt_raw, u = _in_proj(h2, norm_mix_g[0][None], sh1, sc1, wi[:, o_xbc:o_dt].astype(BF16),
                               wi[:, o_pool:o_gates].astype(BF16), w_dt)

    pad_h = lambda v: jnp.pad(v[None], ((0, 0), (0, LANES - N_HEADS)))
    ypre = _ssd(proj, dt_raw, conv_w[0], conv_b[0][None], pad_h(dt_bias[0]), pad_h(A_log[0]),
                jnp.repeat(D_skip[0], HEADDIM)[None])

    ws_gu = jnp.concatenate([ws_gate[0], ws_up[0]], axis=1).astype(BF16)
    m = EXPERT_BLOCK
    cap = -(-n // m) * m
    base, w8, cnt, xs = _mix(
        proj, u, ypre, h2, wi[:, :o_xbc].astype(BF16), wi[:, o_gates:].astype(BF16), ssd_norm_g[0][None],
        pool_w[0].astype(BF16), pool_scale[0][None], w_br_pool[0].astype(BF16),
        w_br_ssd[0].astype(BF16), w_out[0].astype(BF16), g1, norm_ffn_g[0][None], sh2, sc2, g2,
        w_router[0].astype(BF16), router_bias[0][None], ws_gu, ws_down[0].astype(BF16), cap)

    counts = cnt[0].astype(jnp.int32)
    nblk = (counts + m - 1) // m
    bends = jnp.cumsum(nblk)
    bstarts = bends - nblk
    nb_used = bends[-1]
    n_blocks = -(-(n * TOP_K) // m) + N_EXPERTS + 1
    b_eff = jnp.minimum(jnp.arange(n_blocks, dtype=jnp.int32), nb_used - 1)
    done = bends[None, :] <= b_eff[:, None]
    block_e = jnp.sum(done.astype(jnp.int32), axis=1)
    block_row = block_e * (cap // m) + b_eff - jnp.sum(jnp.where(done, nblk[None, :], 0), axis=1)

    xs = _padfill(counts, xs, cap)
    ytok = _experts(block_e, block_row, nb_used.reshape(1), xs,
                    we_gate[0], we_up[0], we_down[0], n * TOP_K)
    out = _combine(ytok, w8, base, g2, final_norm_g[None])
    return out.reshape(bsz, seq, D_MODEL)
```

```python
import functools

import jax
import jax.numpy as jnp
from jax import lax
from jax.experimental import pallas as pl
from jax.experimental.pallas import tpu as pltpu

F32 = jnp.float32
BF16 = jnp.bfloat16
HIGHEST = lax.Precision.HIGHEST

D_MODEL = 1024
D_SSD = 2048
HEADDIM = 64
N_HEADS = 32
N_GROUPS = 8
HEADS_PER_GROUP = N_HEADS // N_GROUPS
D_STATE = 128
CONV_K = 4
CHUNK = 128
SSD_CHUNKS_PER_STEP = 2
GROUP_W = D_SSD // N_GROUPS
D_BC = N_GROUPS * D_STATE
POOL_WINDOWS = (2, 4, 8, 16)
POOL_GDIM = 256
N_EXPERTS = 64
TOP_K = 8
N_EGROUPS = 8
EXPERTS_PER_GROUP = 8
TOPK_GROUPS = 4
D_EXPERT = 256
D_SHARED = 256
ROUTED_SCALE = 2.5
MOE_BLOCK = 128
EPS = 1e-6

LANES = 128
SUBLANES = 8
ROW_TILES = D_MODEL // LANES
PACK_TILES = ROW_TILES // 2
PACK_W = PACK_TILES * LANES
EXPERT_BLOCK = 512
MIX_ISSUE_GROUPS = 32
BEFORE_STAGING = "before_staging"

PROJ_W = D_SSD + 2 * D_BC + D_MODEL
PROJ_CHUNK = 512

VMEM_LIMIT = 56 * 1024 * 1024


def _cparams(sem=("arbitrary",)):
    return pltpu.CompilerParams(dimension_semantics=sem, vmem_limit_bytes=VMEM_LIMIT)


def _full(shape):
    nd = len(shape)
    return pl.BlockSpec(shape, lambda *_: (0,) * nd)


def _silu(v):
    return v * jax.nn.sigmoid(v)


def _pack_bf16_pair(lo, hi):
    lo_bits = lax.bitcast_convert_type(lo.astype(BF16).astype(F32), jnp.uint32)
    hi_bits = lax.bitcast_convert_type(hi.astype(BF16).astype(F32), jnp.uint32)
    return (lo_bits >> 16) | (hi_bits & jnp.uint32(0xFFFF0000))


def _unpack_bf16_pair(words):
    return (lax.bitcast_convert_type(words << 16, F32),
            lax.bitcast_convert_type(words & jnp.uint32(0xFFFF0000), F32))


def _ada_kernel(c_ref, w_ref, b_ref, o_ref):
    c = c_ref[...]
    o_ref[...] = jnp.dot(_silu(c), w_ref[...], preferred_element_type=F32, precision=HIGHEST) + b_ref[...]


def _ada(c, w_ada, b_ada):
    n_out = w_ada.shape[1]
    tn = 1536
    c8 = jnp.broadcast_to(c, (SUBLANES, D_MODEL))
    out = pl.pallas_call(
        _ada_kernel,
        grid=(n_out // tn,),
        in_specs=[_full((SUBLANES, D_MODEL)),
                  pl.BlockSpec((D_MODEL, tn), lambda j: (0, j)),
                  pl.BlockSpec((1, tn), lambda j: (0, j))],
        out_specs=pl.BlockSpec((SUBLANES, tn), lambda j: (0, j)),
        out_shape=jax.ShapeDtypeStruct((SUBLANES, n_out), F32),
        compiler_params=_cparams(),
        name="ada",
    )(c8, w_ada, b_ada.reshape(1, n_out))
    return out[0:1]


def _inproj_kernel(x_ref, g_ref, sh_ref, sc_ref, wa_ref, wb_ref, wdt_ref, proj_ref, dt_ref, u_ref):
    x = x_ref[...]
    inv = lax.rsqrt(jnp.mean(x * x, axis=-1, keepdims=True) + EPS)
    u = x * inv * g_ref[...]
    u = u * (1.0 + sc_ref[...]) + sh_ref[...]
    ub = u.astype(BF16)
    u_ref[...] = ub
    wa = wa_ref.shape[1]
    for c0 in range(0, PROJ_W, PROJ_CHUNK):
        w = wa_ref[:, c0:c0 + PROJ_CHUNK] if c0 < wa else wb_ref[:, c0 - wa:c0 - wa + PROJ_CHUNK]
        proj_ref[:, c0:c0 + PROJ_CHUNK] = jnp.dot(ub, w, preferred_element_type=F32).astype(BF16)
    dt_ref[...] = jnp.dot(ub, wdt_ref[...], preferred_element_type=F32)


def _in_proj(x2, g, sh, sc, w_a, w_b, w_dt):
    n = x2.shape[0]
    tm = 512
    return pl.pallas_call(
        _inproj_kernel,
        grid=(n // tm,),
        in_specs=[pl.BlockSpec((tm, D_MODEL), lambda i: (i, 0)),
                  _full((1, D_MODEL)), _full((1, D_MODEL)), _full((1, D_MODEL)),
                  pl.BlockSpec(w_a.shape, lambda i: (0, 0), pipeline_mode=pl.Buffered(1)),
                  pl.BlockSpec(w_b.shape, lambda i: (0, 0), pipeline_mode=pl.Buffered(1)),
                  _full((D_MODEL, LANES))],
        out_specs=[pl.BlockSpec((tm, PROJ_W), lambda i: (i, 0)),
                   pl.BlockSpec((tm, LANES), lambda i: (i, 0)),
                   pl.BlockSpec((tm, D_MODEL), lambda i: (i, 0))],
        out_shape=[jax.ShapeDtypeStruct((n, PROJ_W), BF16),
                   jax.ShapeDtypeStruct((n, LANES), F32),
                   jax.ShapeDtypeStruct((n, D_MODEL), BF16)],
        compiler_params=_cparams(),
        name="in_proj",
    )(x2, g, sh, sc, w_a, w_b, w_dt)


def _conv_silu(cur_ref, ext_ref, w_ref, b_ref, out_ref, width, cw=512):
    t = cur_ref.shape[0]
    for c0 in range(0, width, cw):
        sl = slice(c0, c0 + cw)
        cur = cur_ref[:, sl].astype(F32)
        ext_ref[SUBLANES:, sl] = cur
        acc = cur * w_ref[CONV_K - 1:CONV_K, sl] + b_ref[:, sl]
        for s in range(1, CONV_K):
            acc = acc + ext_ref[pl.ds(SUBLANES - s, t), sl] * w_ref[CONV_K - 1 - s:CONV_K - s, sl]
        out_ref[:, sl] = _silu(acc).astype(out_ref.dtype)
        ext_ref[0:SUBLANES, sl] = cur[t - SUBLANES:t]


def _ssd_kernel(xs_ref, bc_ref, dt_ref, cwx_ref, cwbc_ref, cbx_ref, cbbc_ref, dtb_ref, alog_ref,
                dskip_ref, o_ref, tailx, tailbc, state, xc, bcc):
    q = CHUNK
    pair_w = 2 * HEADDIM

    @pl.when(pl.program_id(0) == 0)
    def _():
        tailx[...] = jnp.zeros_like(tailx)
        tailbc[...] = jnp.zeros_like(tailbc)
        state[...] = jnp.zeros_like(state)

    _conv_silu(xs_ref, tailx, cwx_ref, cbx_ref, xc, D_SSD)
    _conv_silu(bc_ref, tailbc, cwbc_ref, cbbc_ref, bcc, 2 * D_BC)

    for c in range(xs_ref.shape[0] // q):
        rows = slice(c * q, (c + 1) * q)
        v = dt_ref[rows, :] + dtb_ref[...]
        dt = jnp.maximum(v, 0.0) + jnp.log(1.0 + jnp.exp(-jnp.abs(v)))
        a = dt * (-jnp.exp(alog_ref[...]))
        ri = lax.broadcasted_iota(jnp.int32, (q, q), 0)
        ci = lax.broadcasted_iota(jnp.int32, (q, q), 1)
        causal = ri >= ci
        a_cs = jnp.dot(causal.astype(F32), a, preferred_element_type=F32, precision=HIGHEST)
        a_cs_t = a_cs.T
        dt_t = dt.T
        first_half = ci < HEADDIM
        first_half_row = first_half[0:1, :]

        for g in range(N_GROUPS):
            b_g = bcc[rows, g * D_STATE:(g + 1) * D_STATE]
            c_g = bcc[rows, D_BC + g * D_STATE:D_BC + (g + 1) * D_STATE]
            cb = lax.dot_general(c_g, b_g, (((1,), (1,)), ((), ())), preferred_element_type=F32)
            b_t = b_g.astype(F32).T
            st_g = state[g]
            y_off = jnp.dot(c_g, st_g.astype(BF16), preferred_element_type=F32)
            for pi in range(HEADS_PER_GROUP // 2):
                h0 = g * HEADS_PER_GROUP + 2 * pi
                lanes = slice(h0 * HEADDIM, h0 * HEADDIM + pair_w)
                gl = slice(pi * pair_w, (pi + 1) * pair_w)
                ms, ws, cols, lasts = [], [], [], []
                for h in (h0, h0 + 1):
                    col = jnp.broadcast_to(a_cs[:, h:h + 1], (q, q))
                    row = a_cs_t[h:h + 1, :]
                    dtrow = dt_t[h:h + 1, :]
                    decay = jnp.exp(jnp.where(causal, col - row, -jnp.inf))
                    ms.append((cb * decay * dtrow).astype(BF16))
                    last = a_cs_t[h:h + 1, q - 1:q]
                    ws.append((b_t * (jnp.exp(last - row) * dtrow)).astype(BF16))
                    cols.append(col)
                    lasts.append(last)
                xp = xc[rows, lanes]
                zero = jnp.zeros_like(xp)
                x_bd = jnp.concatenate([jnp.where(first_half, xp, zero), jnp.where(first_half, zero, xp)], axis=0)
                y_p = jnp.dot(jnp.concatenate(ms, axis=1), x_bd, preferred_element_type=F32)
                y_p = y_p + y_off[:, gl] * jnp.exp(jnp.where(first_half, cols[0], cols[1]))
                o_ref[rows, lanes] = (y_p + dskip_ref[:, lanes] * xp.astype(F32)).astype(BF16)
                s_new = jnp.dot(jnp.concatenate(ws, axis=1), x_bd, preferred_element_type=F32)
                carry = jnp.exp(jnp.where(first_half_row, lasts[0], lasts[1]))
                state[g, :, gl] = st_g[:, gl] * carry + s_new


def _ssd(proj, dt_raw, conv_w, conv_b, dt_bias, a_log, d_skip):
    n = proj.shape[0]
    t = SSD_CHUNKS_PER_STEP * CHUNK
    return pl.pallas_call(
        _ssd_kernel,
        grid=(n // t,),
        in_specs=[pl.BlockSpec((t, D_SSD), lambda i: (i, 0)),
                  pl.BlockSpec((t, 2 * D_BC), lambda i: (i, 1)),
                  pl.BlockSpec((t, LANES), lambda i: (i, 0)),
                  pl.BlockSpec((CONV_K, D_SSD), lambda i: (0, 0)),
                  pl.BlockSpec((CONV_K, 2 * D_BC), lambda i: (0, 1)),
                  pl.BlockSpec((1, D_SSD), lambda i: (0, 0)),
                  pl.BlockSpec((1, 2 * D_BC), lambda i: (0, 1)),
                  _full((1, LANES)), _full((1, LANES)),
                  _full((1, D_SSD))],
        out_specs=pl.BlockSpec((t, D_SSD), lambda i: (i, 0)),
        out_shape=jax.ShapeDtypeStruct((n, D_SSD), BF16),
        scratch_shapes=[pltpu.VMEM((SUBLANES + t, D_SSD), F32),
                        pltpu.VMEM((SUBLANES + t, 2 * D_BC), F32),
                        pltpu.VMEM((N_GROUPS, D_STATE, GROUP_W), F32),
                        pltpu.VMEM((t, D_SSD), BF16),
                        pltpu.VMEM((t, 2 * D_BC), BF16)],
        compiler_params=_cparams(),
        name="ssd",
    )(proj, proj, dt_raw, conv_w, conv_w, conv_b, conv_b, dt_bias, a_log, d_skip)


PAD_ID = 0xFFFFFFFF
TOK_ROW = PACK_TILES
MASK_ROWS = (PACK_TILES + 1, PACK_TILES + 2)


def _tile_rows(ref, first_row):
    return ref.at[pl.ds(pl.multiple_of(first_row * SUBLANES, SUBLANES), SUBLANES)]


def _mix_kernel(xp_ref, u_ref, ypre_ref, x_ref, wz_ref, wgt_ref, sng_ref, poolw_ref, pscale_ref, wbp_ref, wbs_ref, wout_ref,
                g1_ref, ng_ref, sh2_ref, sc2_ref, g2_ref, wr_ref, rb_ref, wsgu_ref, wsd_ref, zero_ref,
                base_ref, ws_ref, cnt_ref, xs_ref, ptail, run, stage, dvm, dsm, sem, sem_s, *, n_tiles, cap):
    t = x_ref.shape[0]
    i = pl.program_id(0)
    slot = lax.rem(i, 2)

    @pl.when(i == 0)
    def _():
        ptail[...] = jnp.zeros_like(ptail)
        run[...] = jnp.zeros_like(run)
        stage[...] = jnp.zeros_like(stage)

    def issue_rows(s, group, after=None):
        zero = 0
        if after is not None:
            probe = jnp.max(jnp.abs(after[0:SUBLANES, 0:min(LANES, after.shape[1])])).astype(jnp.int32)
            zero = probe * zero_ref[0]
        per = t // MIX_ISSUE_GROUPS
        for tt in range(group * per, (group + 1) * per):
            for k in range(TOP_K):
                pltpu.make_async_copy(stage.at[s, pl.ds(tt * SUBLANES, SUBLANES)],
                                      _tile_rows(xs_ref, dsm[s, k, tt] + zero), sem.at[s]).start()

    def drain_rows(s):
        for _ in range(TOP_K):
            pltpu.make_async_copy(stage.at[s], xs_ref.at[pl.ds(0, t * SUBLANES)], sem.at[s]).wait()

    def tile(s, milestone=None):
        _mix_tile_body(xp_ref, u_ref, ypre_ref, x_ref, wz_ref, wgt_ref, sng_ref, poolw_ref, pscale_ref, wbp_ref, wbs_ref, wout_ref,
                       g1_ref, ng_ref, sh2_ref, sc2_ref, g2_ref, wr_ref, rb_ref, wsgu_ref, wsd_ref,
                       base_ref, ws_ref, cnt_ref, ptail, run, stage, dvm, dsm, sem_s, i, s, cap, milestone)

    @pl.when(i == 0)
    def _():
        tile(0)

    groups = iter(range(1, MIX_ISSUE_GROUPS))

    def milestone(value):
        if value is BEFORE_STAGING:
            assert next(groups, None) is None
            @pl.when(i >= 2)
            def _():
                drain_rows(slot)
        else:
            issue_rows(1 - slot, next(groups), value)

    def wait_slot_rows(s):
        pltpu.make_async_copy(dvm, dsm.at[s], sem_s).wait()

    @pl.when((i >= 1) & (i < n_tiles))
    def _():
        wait_slot_rows(1 - slot)
        issue_rows(1 - slot, 0)
        tile(slot, milestone)

    @pl.when(i == n_tiles)
    def _():
        wait_slot_rows(1 - slot)
        for group in range(MIX_ISSUE_GROUPS):
            issue_rows(1 - slot, group)
        if n_tiles >= 2:
            drain_rows(slot)
        drain_rows(1 - slot)


def _mix_tile_body(xp_ref, u_ref, ypre_ref, x_ref, wz_ref, wgt_ref, sng_ref, poolw_ref, pscale_ref, wbp_ref, wbs_ref, wout_ref,
                   g1_ref, ng_ref, sh2_ref, sc2_ref, g2_ref, wr_ref, rb_ref, wsgu_ref, wsd_ref,
                   base_ref, ws_ref, cnt_ref, ptail, run, stage, dvm, dsm, sem_s, i, slot, cap, milestone):
    t = x_ref.shape[0]
    if milestone is None:
        milestone = lambda value: None

    xp = xp_ref[...].astype(F32)
    ext = jnp.concatenate([ptail[...], xp], axis=0)
    ptail[...] = xp[t - 2 * SUBLANES:t]
    pos = (lax.broadcasted_iota(jnp.int32, (t, 1), 0) + (i * t + 1)).astype(F32)
    pooled = []
    for gi, w in enumerate(POOL_WINDOWS):
        sl = slice(gi * POOL_GDIM, (gi + 1) * POOL_GDIM)
        e = ext[:, sl]
        s = e
        span = 1
        while span < w:
            s = s + pltpu.roll(s, span, axis=0)
            span *= 2
        win = s[2 * SUBLANES:]
        mean = win / jnp.minimum(pos, float(w))
        pg = (mean - xp[:, sl]).astype(BF16)
        mixed = jnp.dot(pg, poolw_ref[gi], preferred_element_type=F32)
        pooled.append(mixed * pscale_ref[:, sl])
        milestone(pooled[-1])
    pooled = jnp.concatenate(pooled, axis=1).astype(BF16)
    y_pool = jnp.dot(pooled, wbp_ref[...], preferred_element_type=F32)
    milestone(y_pool)
    ub = u_ref[...]
    z = jnp.dot(ub, wz_ref[...], preferred_element_type=F32)
    milestone(z)
    yn = []
    for g in range(N_GROUPS):
        sl = slice(g * GROUP_W, (g + 1) * GROUP_W)
        yg = ypre_ref[:, sl].astype(F32) * _silu(z[:, sl])
        inv_g = lax.rsqrt(jnp.mean(yg * yg, axis=-1, keepdims=True) + EPS)
        yn.append((yg * inv_g * sng_ref[:, sl]).astype(BF16))
        milestone(inv_g)
    y_ssd = jnp.dot(jnp.concatenate(yn, axis=1), wbs_ref[...], preferred_element_type=F32)
    milestone(y_ssd)
    gates = jnp.dot(ub, wgt_ref[...], preferred_element_type=F32)
    milestone(gates)
    g_ssd = jax.nn.sigmoid(gates[:, :D_MODEL])
    g_pool = jax.nn.sigmoid(gates[:, D_MODEL:])
    mixed = (g_ssd * y_ssd + g_pool * y_pool).astype(BF16)
    h = x_ref[...] + g1_ref[...] * jnp.dot(mixed, wout_ref[...], preferred_element_type=F32)
    milestone(h)

    inv = lax.rsqrt(jnp.mean(h * h, axis=-1, keepdims=True) + EPS)
    milestone(inv)
    u2 = h * inv * ng_ref[...]
    u2 = u2 * (1.0 + sc2_ref[...]) + sh2_ref[...]
    u2b = u2.astype(BF16)
    packed = [_pack_bf16_pair(u2[:, j * LANES:(j + 1) * LANES],
                              u2[:, (j + PACK_TILES) * LANES:(j + PACK_TILES + 1) * LANES])
              for j in range(PACK_TILES)]

    hs = jnp.dot(u2b, wsgu_ref[...], preferred_element_type=F32)
    milestone(hs)
    act = (_silu(hs[:, :D_SHARED]) * hs[:, D_SHARED:]).astype(BF16)
    shared = jnp.dot(act, wsd_ref[...], preferred_element_type=F32)
    milestone(shared)
    base_ref[...] = h + g2_ref[...] * shared

    logits = jnp.dot(u2b, wr_ref[...], preferred_element_type=F32)
    scores = jax.nn.sigmoid(logits)
    milestone(scores)
    choice = scores + rb_ref[...]
    lane = lax.broadcasted_iota(jnp.int32, (t, N_EXPERTS), 1)
    lane_f = lane.astype(F32)
    lane_grp = lane // EXPERTS_PER_GROUP
    neg = -jnp.inf
    gscore = []
    for g in range(N_EGROUPS):
        vg = jnp.where(lane_grp == g, choice, neg)
        m1 = jnp.max(vg, axis=-1, keepdims=True)
        i1 = jnp.min(jnp.where(vg == m1, lane_f, float(N_EXPERTS)), axis=-1, keepdims=True)
        m2 = jnp.max(jnp.where(lane_f == i1, neg, vg), axis=-1, keepdims=True)
        gscore.append(m1 + m2)
        milestone(gscore[-1])
    gmask = jnp.zeros((t, N_EXPERTS), jnp.bool_)
    for g in range(N_EGROUPS):
        rank = jnp.zeros((t, 1), F32)
        for g2 in range(N_EGROUPS):
            if g2 == g:
                continue
            better = (gscore[g2] > gscore[g]) | ((gscore[g2] == gscore[g]) & (g2 < g))
            rank = rank + better.astype(F32)
        gmask = gmask | ((rank < float(TOPK_GROUPS)) & (lane_grp == g))
    work = jnp.where(gmask, choice, neg)
    sel = jnp.zeros((t, N_EXPERTS), jnp.bool_)
    onehots, idxs, sks = [], [], []
    for k in range(TOP_K):
        m = jnp.max(work, axis=-1, keepdims=True)
        idx = jnp.min(jnp.where(work == m, lane_f, float(N_EXPERTS)), axis=-1, keepdims=True)
        oh = lane_f == idx
        onehots.append(oh)
        idxs.append(idx)
        sks.append(jnp.sum(jnp.where(oh, scores, 0.0), axis=-1, keepdims=True))
        if k == TOP_K // 2 - 1:
            milestone(sks[-1])
        sel = sel | oh
        work = jnp.where(oh, neg, work)
    denom = sks[0]
    for k in range(1, TOP_K):
        denom = denom + sks[k]
    milestone(denom)

    ri = lax.broadcasted_iota(jnp.int32, (t, t), 0)
    ci = lax.broadcasted_iota(jnp.int32, (t, t), 1)
    before = (ri > ci).astype(BF16)
    sel_f = jnp.where(sel, 1.0, 0.0)
    sel_b = sel_f.astype(BF16)
    pos_tile = jnp.dot(before, sel_b, preferred_element_type=F32) + run[...]
    er = lax.broadcasted_iota(jnp.int32, (N_EXPERTS, N_EXPERTS), 0)
    ec = lax.broadcasted_iota(jnp.int32, (N_EXPERTS, N_EXPERTS), 1)
    rank_tile = jnp.dot(sel_b, (er < ec).astype(BF16), preferred_element_type=F32)
    lane128 = lax.broadcasted_iota(jnp.int32, (t, LANES), 1)
    k_iota = lax.broadcasted_iota(jnp.int32, (t, TOP_K), 1).astype(F32)
    dest = jnp.zeros((t, LANES), F32)
    w_sorted = jnp.zeros((t, TOP_K), F32)
    for k in range(TOP_K):
        pk = jnp.sum(jnp.where(onehots[k], pos_tile, 0.0), axis=-1, keepdims=True)
        rk = jnp.sum(jnp.where(onehots[k], rank_tile, 0.0), axis=-1, keepdims=True)
        dest = jnp.where(lane128 == k, idxs[k] * float(cap) + pk, dest)
        w_sorted = jnp.where(k_iota == rk, sks[k] / denom * ROUTED_SCALE, w_sorted)
    ws_ref[...] = w_sorted
    total = run[...] + jnp.sum(sel_f, axis=0, keepdims=True)
    run[...] = total
    cnt_ref[...] = total

    milestone(BEFORE_STAGING)
    for j in range(PACK_TILES):
        stage[slot, pl.ds(j, t, stride=SUBLANES), :] = packed[j]
    tok = lax.broadcasted_iota(jnp.int32, (t, LANES), 0) + i * t
    stage[slot, pl.ds(TOK_ROW, t, stride=SUBLANES), :] = tok.astype(jnp.uint32)
    bit = jnp.where(sel, jnp.left_shift(1, lane & 15), 0).astype(F32)
    words = []
    for q in range(N_EXPERTS // 16):
        part = jnp.sum(jnp.where((lane >> 4) == q, bit, 0.0), axis=-1, keepdims=True)
        words.append(part.astype(jnp.int32).astype(jnp.uint32))
    for w, row in enumerate(MASK_ROWS):
        word = words[2 * w] | (words[2 * w + 1] << 16)
        stage[slot, pl.ds(row, t, stride=SUBLANES), :] = jnp.broadcast_to(word, (t, LANES))

    dvm[...] = dest.T[0:TOP_K, :].astype(jnp.int32)
    pltpu.make_async_copy(dvm, dsm.at[slot], sem_s).start()


def _mix(proj, u, ypre, x2, w_z, w_gates, ssd_ng, pool_w, pool_scale, w_br_pool, w_br_ssd, w_out, g1, ng, sh2, sc2,
         g2, w_router, router_bias, ws_gu, ws_down, cap):
    n = x2.shape[0]
    t = 256
    n_tiles = n // t
    xp_blk = (D_SSD + 2 * D_BC) // D_MODEL
    const = lambda shape: pl.BlockSpec(shape, lambda i: (0,) * len(shape), pipeline_mode=pl.Buffered(1))
    tile = lambda i: jnp.minimum(i, n_tiles - 1)
    row = lambda i: (tile(i), 0)
    return pl.pallas_call(
        functools.partial(_mix_kernel, n_tiles=n_tiles, cap=cap),
        grid=(n_tiles + 1,),
        in_specs=[pl.BlockSpec((t, D_MODEL), lambda i: (tile(i), xp_blk)),
                  pl.BlockSpec((t, D_MODEL), row),
                  pl.BlockSpec((t, D_SSD), row),
                  pl.BlockSpec((t, D_MODEL), row),
                  const((D_MODEL, D_SSD)),
                  const((D_MODEL, 2 * D_MODEL)),
                  _full((1, D_SSD)),
                  const((len(POOL_WINDOWS), POOL_GDIM, POOL_GDIM)),
                  _full((1, D_MODEL)),
                  const((D_MODEL, D_MODEL)),
                  const((D_SSD, D_MODEL)),
                  const((D_MODEL, D_MODEL)),
                  _full((1, D_MODEL)), _full((1, D_MODEL)), _full((1, D_MODEL)), _full((1, D_MODEL)),
                  _full((1, D_MODEL)),
                  _full((D_MODEL, N_EXPERTS)),
                  _full((1, N_EXPERTS)),
                  const((D_MODEL, 2 * D_SHARED)),
                  const((D_SHARED, D_MODEL)),
                  pl.BlockSpec(memory_space=pltpu.SMEM)],
        out_specs=[pl.BlockSpec((t, D_MODEL), row),
                   pl.BlockSpec((t, TOP_K), row),
                   _full((1, N_EXPERTS)),
                   pl.BlockSpec(memory_space=pl.ANY)],
        out_shape=[jax.ShapeDtypeStruct((n, D_MODEL), F32),
                   jax.ShapeDtypeStruct((n, TOP_K), F32),
                   jax.ShapeDtypeStruct((1, N_EXPERTS), F32),
                   jax.ShapeDtypeStruct((N_EXPERTS * cap * SUBLANES, LANES), jnp.uint32)],
        scratch_shapes=[pltpu.VMEM((2 * SUBLANES, D_MODEL), F32),
                        pltpu.VMEM((1, N_EXPERTS), F32),
                        pltpu.VMEM((2, t * SUBLANES, LANES), jnp.uint32),
                        pltpu.VMEM((TOP_K, t), jnp.int32),
                        pltpu.SMEM((2, TOP_K, t), jnp.int32),
                        pltpu.SemaphoreType.DMA((2,)),
                        pltpu.SemaphoreType.DMA(())],
        compiler_params=_cparams(),
        name="mix",
    )(proj, u, ypre, x2, w_z, w_gates, ssd_ng, pool_w, pool_scale, w_br_pool, w_br_ssd, w_out, g1, ng, sh2, sc2, g2,
      w_router, router_bias, ws_gu, ws_down, jnp.zeros((1,), jnp.int32))


def _padfill_kernel(cnt_ref, xs_in_ref, xs_ref, padbuf, sem, *, cap):
    del xs_in_ref
    sub = lax.broadcasted_iota(jnp.int32, padbuf.shape, 0) & (SUBLANES - 1)
    padbuf[...] = jnp.where(sub == TOK_ROW, jnp.uint32(PAD_ID), jnp.uint32(0))
    sizes = [1 << bit for bit in range(EXPERT_BLOCK.bit_length() - 1)]

    def pad_copies(e, fn):
        c = cnt_ref[e]
        n_pad = (EXPERT_BLOCK - (c & (EXPERT_BLOCK - 1))) & (EXPERT_BLOCK - 1)
        first = e * cap + c
        for size in sizes:
            @pl.when((n_pad & size) != 0)
            def _():
                start = first + (n_pad & (size - 1))
                dst = xs_ref.at[pl.ds(pl.multiple_of(start * SUBLANES, SUBLANES), size * SUBLANES)]
                fn(pltpu.make_async_copy(padbuf.at[pl.ds(0, size * SUBLANES)], dst, sem))

    def issue(e, carry):
        pad_copies(e, lambda cp: cp.start())
        return carry

    lax.fori_loop(0, N_EXPERTS, issue, 0)

    def drain(e, carry):
        pad_copies(e, lambda cp: cp.wait())
        return carry

    lax.fori_loop(0, N_EXPERTS, drain, 0)


def _padfill(counts, xs, cap):
    return pl.pallas_call(
        functools.partial(_padfill_kernel, cap=cap),
        grid_spec=pltpu.PrefetchScalarGridSpec(
            num_scalar_prefetch=1,
            grid=(1,),
            in_specs=[pl.BlockSpec(memory_space=pl.ANY)],
            out_specs=pl.BlockSpec(memory_space=pl.ANY),
            scratch_shapes=[pltpu.VMEM((EXPERT_BLOCK // 2 * SUBLANES, LANES), jnp.uint32),
                            pltpu.SemaphoreType.DMA(())]),
        out_shape=jax.ShapeDtypeStruct(xs.shape, xs.dtype),
        input_output_aliases={1: 0},
        compiler_params=_cparams(),
        name="padfill",
    )(counts, xs)


def _expert_kernel(be_ref, br_ref, nb_ref, x_ref, wg_ref, wu_ref, wd_ref, ytok_ref, ybuf0, ybuf1, idv, ids0, ids1,
                   wgb, wub, wdb, sem, sem_ids, *, n_ids):
    del br_ref
    m = EXPERT_BLOCK
    b = pl.program_id(0)
    nb = nb_ref[0]
    ybufs, idss = (ybuf0, ybuf1), (ids0, ids1)
    n_phases = m // LANES
    half = D_MODEL // 2

    @pl.when((b == 0) | (be_ref[b] != be_ref[jnp.maximum(b - 1, 0)]))
    def _():
        wgb[...] = wg_ref[0].astype(BF16)
        wub[...] = wu_ref[0].astype(BF16)
        wdb[...] = wd_ref[0].astype(BF16)

    def half_tile(ref, unit):
        return ref.at[pl.ds(pl.multiple_of(unit * PACK_TILES, PACK_TILES), PACK_TILES)]

    def issue(s, a0):
        for col in range(LANES):
            pltpu.make_async_copy(half_tile(ybufs[s], a0 * LANES + col),
                                  half_tile(ytok_ref, idss[s][a0, col]), sem.at[s]).start()

    def step(s, prev):
        tokrep = x_ref[pl.ds(TOK_ROW, m, stride=SUBLANES), :]
        e = be_ref[b]
        below_lo = jnp.where(e >= 32, -1, (1 << jnp.minimum(e, 31)) - 1).astype(jnp.uint32)
        below_hi = jnp.where(e >= 32, (1 << jnp.maximum(e - 32, 0)) - 1, 0).astype(jnp.uint32)
        rank = (lax.population_count(x_ref[pl.ds(MASK_ROWS[0], m, stride=SUBLANES), :] & below_lo)
                + lax.population_count(x_ref[pl.ds(MASK_ROWS[1], m, stride=SUBLANES), :] & below_hi))
        r = lax.broadcasted_iota(jnp.int32, (m, LANES), 0)
        ln = lax.broadcasted_iota(jnp.int32, (m, LANES), 1)
        n_tok = n_ids // TOP_K
        rank = rank.astype(jnp.int32)
        half_rank = TOP_K // 2
        unit = ((rank % half_rank) * n_tok + tokrep.astype(jnp.int32)) * 2 + rank // half_rank
        idi = jnp.where(tokrep == jnp.uint32(PAD_ID), n_ids + r, unit)
        diag = jnp.where((r & (LANES - 1)) == ln, idi, 0).astype(F32)
        idv[...] = jnp.sum(diag.reshape(m // LANES, LANES, LANES), axis=1).astype(jnp.int32)
        ids_copy = pltpu.make_async_copy(idv, idss[s], sem_ids)
        ids_copy.start()

        halves = [_unpack_bf16_pair(x_ref[pl.ds(j, m, stride=SUBLANES), :]) for j in range(PACK_TILES)]
        x = jnp.concatenate([lo for lo, _ in halves] + [hi for _, hi in halves], axis=1).astype(BF16)
        phases = iter(range(n_phases))
        if prev is not None:
            issue(prev, next(phases))
        gate = jnp.dot(x, wgb[...], preferred_element_type=F32)
        if prev is not None:
            issue(prev, next(phases))
        up = jnp.dot(x, wub[...], preferred_element_type=F32)
        hid = (_silu(gate) * up).astype(BF16)

        def drain(which):
            pltpu.make_async_copy(ybufs[which], ytok_ref.at[pl.ds(0, m * PACK_TILES)], sem.at[which]).wait()

        if prev is not None:
            @pl.when(b >= 2)
            def _():
                drain(s)
        ys = []
        for c0 in (0, half):
            if prev is not None:
                issue(prev, next(phases))
            ys.append(jnp.dot(hid, wdb[:, c0:c0 + half], preferred_element_type=F32))
        for j in range(PACK_TILES):
            sl = slice(j * LANES, (j + 1) * LANES)
            ybufs[s][pl.ds(j, m, stride=PACK_TILES), :] = _pack_bf16_pair(ys[0][:, sl], ys[1][:, sl])
        ids_copy.wait()
        if prev is not None:
            for a0 in phases:
                issue(prev, a0)

            @pl.when(b == nb)
            def _():
                drain(prev)

    @pl.when(b == 0)
    def _():
        step(0, None)

    for s in range(2):
        @pl.when((b >= 1) & (b <= nb) & (lax.rem(b, 2) == s))
        def _():
            step(s, 1 - s)


def _experts(block_e, block_row, nb_used, xs, we_gate, we_up, we_down, n_ids):
    n_blocks = block_e.shape[0]
    m = EXPERT_BLOCK
    rows = m * SUBLANES
    return pl.pallas_call(
        functools.partial(_expert_kernel, n_ids=n_ids),
        grid_spec=pltpu.PrefetchScalarGridSpec(
            num_scalar_prefetch=3,
            grid=(n_blocks,),
            in_specs=[pl.BlockSpec((rows, LANES), lambda b, be, br, nb: (br[b], 0)),
                      pl.BlockSpec((1, D_MODEL, D_EXPERT), lambda b, be, br, nb: (be[b], 0, 0)),
                      pl.BlockSpec((1, D_MODEL, D_EXPERT), lambda b, be, br, nb: (be[b], 0, 0)),
                      pl.BlockSpec((1, D_EXPERT, D_MODEL), lambda b, be, br, nb: (be[b], 0, 0))],
            out_specs=pl.BlockSpec(memory_space=pl.ANY),
            scratch_shapes=[pltpu.VMEM((m * PACK_TILES, LANES), jnp.uint32),
                            pltpu.VMEM((m * PACK_TILES, LANES), jnp.uint32),
                            pltpu.VMEM((m // LANES, LANES), jnp.int32),
                            pltpu.SMEM((m // LANES, LANES), jnp.int32),
                            pltpu.SMEM((m // LANES, LANES), jnp.int32),
                            pltpu.VMEM((D_MODEL, D_EXPERT), BF16),
                            pltpu.VMEM((D_MODEL, D_EXPERT), BF16),
                            pltpu.VMEM((D_EXPERT, D_MODEL), BF16),
                            pltpu.SemaphoreType.DMA((2,)),
                            pltpu.SemaphoreType.DMA(())]),
        out_shape=jax.ShapeDtypeStruct(((n_ids + m) * PACK_TILES, LANES), jnp.uint32),
        compiler_params=_cparams(),
        name="experts",
    )(block_e, block_row, nb_used, xs, we_gate, we_up, we_down)


def _combine_kernel(*refs):
    half_rank = TOP_K // 2
    y_refs = refs[:half_rank]
    w8_ref, base_ref, g2_ref, fg_ref, o_ref = refs[half_rank:]
    t7 = base_ref.shape[0]
    w8 = w8_ref[...]
    parts = [jnp.zeros((t7, LANES), F32) for _ in range(ROW_TILES)]
    for p in range(half_rank):
        for h in range(2):
            wk = w8[:, p + h * half_rank:p + h * half_rank + 1]
            for j in range(PACK_TILES):
                lo, hi = _unpack_bf16_pair(y_refs[p][pl.ds(h * PACK_TILES + j, t7, stride=SUBLANES), :])
                parts[j] = parts[j] + wk * lo
                parts[j + PACK_TILES] = parts[j + PACK_TILES] + wk * hi
    routed = jnp.concatenate(parts, axis=1)
    h = base_ref[...] + g2_ref[...] * routed
    inv = lax.rsqrt(jnp.mean(h * h, axis=-1, keepdims=True) + EPS)
    o_ref[...] = h * inv * fg_ref[...]


def _combine(ytok, w8, base, g2, fg):
    n = base.shape[0]
    t7 = 256
    y_specs = [pl.BlockSpec((t7 * SUBLANES, LANES), functools.partial(lambda i, p: (p * (n // t7) + i, 0), p=p))
               for p in range(TOP_K // 2)]
    return pl.pallas_call(
        _combine_kernel,
        grid=(n // t7,),
        in_specs=y_specs + [
                  pl.BlockSpec((t7, TOP_K), lambda i: (i, 0)),
                  pl.BlockSpec((t7, D_MODEL), lambda i: (i, 0)),
                  _full((1, D_MODEL)), _full((1, D_MODEL))],
        out_specs=pl.BlockSpec((t7, D_MODEL), lambda i: (i, 0)),
        out_shape=jax.ShapeDtypeStruct((n, D_MODEL), F32),
        compiler_params=_cparams(),
        name="combine",
    )(*([ytok] * (TOP_K // 2)), w8, base, g2, fg)


def kernel(x, c, w_ada, b_ada, norm_mix_g, w_in, conv_w, conv_b, dt_bias, A_log, D_skip, ssd_norm_g, pool_w,
           pool_scale, w_br_ssd, w_br_pool, w_out, norm_ffn_g, w_router, router_bias, we_gate, we_up, we_down,
           ws_gate, ws_up, ws_down, final_norm_g):
    bsz, seq, _ = x.shape
    assert bsz == 1 and w_ada.shape[0] == 1
    n = seq
    h2 = x.reshape(n, D_MODEL)

    mod = _ada(c, w_ada[0], b_ada[0])
    sh1, sc1, g1, sh2, sc2, g2 = [mod[:, k * D_MODEL:(k + 1) * D_MODEL] for k in range(6)]

    wi = w_in[0]
    o_xbc, o_dt = D_SSD, D_SSD + D_SSD + 2 * D_BC
    o_pool = o_dt + N_HEADS
    o_gates = o_pool + D_MODEL
    w_dt = jnp.pad(wi[:, o_dt:o_pool], ((0, 0), (0, LANES - N_HEADS))).astype(BF16)
    proj, dt_raw, u = _in_proj(h2, norm_mix_g[0][None], sh1, sc1, wi[:, o_xbc:o_dt].astype(BF16),
                               wi[:, o_pool:o_gates].astype(BF16), w_dt)

    pad_h = lambda v: jnp.pad(v[None], ((0, 0), (0, LANES - N_HEADS)))
    ypre = _ssd(proj, dt_raw, conv_w[0], conv_b[0][None], pad_h(dt_bias[0]), pad_h(A_log[0]),
                jnp.repeat(D_skip[0], HEADDIM)[None])

    ws_gu = jnp.concatenate([ws_gate[0], ws_up[0]], axis=1).astype(BF16)
    m = EXPERT_BLOCK
    cap = -(-n // m) * m
    base, w8, cnt, xs = _mix(
        proj, u, ypre, h2, wi[:, :o_xbc].astype(BF16), wi[:, o_gates:].astype(BF16), ssd_norm_g[0][None],
        pool_w[0].astype(BF16), pool_scale[0][None], w_br_pool[0].astype(BF16),
        w_br_ssd[0].astype(BF16), w_out[0].astype(BF16), g1, norm_ffn_g[0][None], sh2, sc2, g2,
        w_router[0].astype(BF16), router_bias[0][None], ws_gu, ws_down[0].astype(BF16), cap)

    counts = cnt[0].astype(jnp.int32)
    nblk = (counts + m - 1) // m
    bends = jnp.cumsum(nblk)
    bstarts = bends - nblk
    nb_used = bends[-1]
    n_blocks = -(-(n * TOP_K) // m) + N_EXPERTS + 1
    b_eff = jnp.minimum(jnp.arange(n_blocks, dtype=jnp.int32), nb_used - 1)
    done = bends[None, :] <= b_eff[:, None]
    block_e = jnp.sum(done.astype(jnp.int32), axis=1)
    block_row = block_e * (cap // m) + b_eff - jnp.sum(jnp.where(done, nblk[None, :], 0), axis=1)

    xs = _padfill(counts, xs, cap)
    ytok = _experts(block_e, block_row, nb_used.reshape(1), xs,
                    we_gate[0], we_up[0], we_down[0], n * TOP_K)
    out = _combine(ytok, w8, base, g2, final_norm_g[None])
    return out.reshape(bsz, seq, D_MODEL)
```

```python
import functools

import jax
import jax.numpy as jnp
from jax import lax
from jax.experimental import pallas as pl
from jax.experimental.pallas import tpu as pltpu

F32 = jnp.float32
BF16 = jnp.bfloat16
HIGHEST = lax.Precision.HIGHEST

D_MODEL = 1024
D_SSD = 2048
HEADDIM = 64
N_HEADS = 32
N_GROUPS = 8
HEADS_PER_GROUP = N_HEADS // N_GROUPS
D_STATE = 128
CONV_K = 4
CHUNK = 128
SSD_CHUNKS_PER_STEP = 2
GROUP_W = D_SSD // N_GROUPS
D_BC = N_GROUPS * D_STATE
POOL_WINDOWS = (2, 4, 8, 16)
POOL_GDIM = 256
N_EXPERTS = 64
TOP_K = 8
N_EGROUPS = 8
EXPERTS_PER_GROUP = 8
TOPK_GROUPS = 4
D_EXPERT = 256
D_SHARED = 256
ROUTED_SCALE = 2.5
MOE_BLOCK = 128
EPS = 1e-6

LANES = 128
SUBLANES = 8
ROW_TILES = D_MODEL // LANES
PACK_TILES = ROW_TILES // 2
PACK_W = PACK_TILES * LANES
EXPERT_BLOCK = 512
MIX_ISSUE_GROUPS = 32
BEFORE_STAGING = "before_staging"

PROJ_W = D_SSD + 2 * D_BC + D_MODEL
PROJ_CHUNK = 512

VMEM_LIMIT = 56 * 1024 * 1024


def _cparams(sem=("arbitrary",)):
    return pltpu.CompilerParams(dimension_semantics=sem, vmem_limit_bytes=VMEM_LIMIT)


def _full(shape):
    nd = len(shape)
    return pl.BlockSpec(shape, lambda *_: (0,) * nd)


def _silu(v):
    return v * jax.nn.sigmoid(v)


def _pack_bf16_pair(lo, hi):
    lo_bits = lax.bitcast_convert_type(lo.astype(BF16).astype(F32), jnp.uint32)
    hi_bits = lax.bitcast_convert_type(hi.astype(BF16).astype(F32), jnp.uint32)
    return (lo_bits >> 16) | (hi_bits & jnp.uint32(0xFFFF0000))


def _unpack_bf16_pair(words):
    return (lax.bitcast_convert_type(words << 16, F32),
            lax.bitcast_convert_type(words & jnp.uint32(0xFFFF0000), F32))


def _ada_kernel(c_ref, w_ref, b_ref, o_ref):
    c = c_ref[...]
    o_ref[...] = jnp.dot(_silu(c), w_ref[...], preferred_element_type=F32, precision=HIGHEST) + b_ref[...]


def _ada(c, w_ada, b_ada):
    n_out = w_ada.shape[1]
    tn = 1536
    c8 = jnp.broadcast_to(c, (SUBLANES, D_MODEL))
    out = pl.pallas_call(
        _ada_kernel,
        grid=(n_out // tn,),
        in_specs=[_full((SUBLANES, D_MODEL)),
                  pl.BlockSpec((D_MODEL, tn), lambda j: (0, j)),
                  pl.BlockSpec((1, tn), lambda j: (0, j))],
        out_specs=pl.BlockSpec((SUBLANES, tn), lambda j: (0, j)),
        out_shape=jax.ShapeDtypeStruct((SUBLANES, n_out), F32),
        compiler_params=_cparams(),
        name="ada",
    )(c8, w_ada, b_ada.reshape(1, n_out))
    return out[0:1]


def _inproj_kernel(x_ref, g_ref, sh_ref, sc_ref, wa_ref, wb_ref, wdt_ref, proj_ref, dt_ref, u_ref):
    x = x_ref[...]
    inv = lax.rsqrt(jnp.mean(x * x, axis=-1, keepdims=True) + EPS)
    u = x * inv * g_ref[...]
    u = u * (1.0 + sc_ref[...]) + sh_ref[...]
    ub = u.astype(BF16)
    u_ref[...] = ub
    wa = wa_ref.shape[1]
    for c0 in range(0, PROJ_W, PROJ_CHUNK):
        w = wa_ref[:, c0:c0 + PROJ_CHUNK] if c0 < wa else wb_ref[:, c0 - wa:c0 - wa + PROJ_CHUNK]
        proj_ref[:, c0:c0 + PROJ_CHUNK] = jnp.dot(ub, w, preferred_element_type=F32).astype(BF16)
    dt_ref[...] = jnp.dot(ub, wdt_ref[...], preferred_element_type=F32)


def _in_proj(x2, g, sh, sc, w_a, w_b, w_dt):
    n = x2.shape[0]
    tm = 512
    return pl.pallas_call(
        _inproj_kernel,
        grid=(n // tm,),
        in_specs=[pl.BlockSpec((tm, D_MODEL), lambda i: (i, 0)),
                  _full((1, D_MODEL)), _full((1, D_MODEL)), _full((1, D_MODEL)),
                  pl.BlockSpec(w_a.shape, lambda i: (0, 0), pipeline_mode=pl.Buffered(1)),
                  pl.BlockSpec(w_b.shape, lambda i: (0, 0), pipeline_mode=pl.Buffered(1)),
                  _full((D_MODEL, LANES))],
        out_specs=[pl.BlockSpec((tm, PROJ_W), lambda i: (i, 0)),
                   pl.BlockSpec((tm, LANES), lambda i: (i, 0)),
                   pl.BlockSpec((tm, D_MODEL), lambda i: (i, 0))],
        out_shape=[jax.ShapeDtypeStruct((n, PROJ_W), BF16),
                   jax.ShapeDtypeStruct((n, LANES), F32),
                   jax.ShapeDtypeStruct((n, D_MODEL), BF16)],
        compiler_params=_cparams(),
        name="in_proj",
    )(x2, g, sh, sc, w_a, w_b, w_dt)


def _conv_silu(cur_ref, ext_ref, w_ref, b_ref, out_ref, width, cw=512):
    t = cur_ref.shape[0]
    for c0 in range(0, width, cw):
        sl = slice(c0, c0 + cw)
        cur = cur_ref[:, sl].astype(F32)
        ext_ref[SUBLANES:, sl] = cur
        acc = cur * w_ref[CONV_K - 1:CONV_K, sl] + b_ref[:, sl]
        for s in range(1, CONV_K):
            acc = acc + ext_ref[pl.ds(SUBLANES - s, t), sl] * w_ref[CONV_K - 1 - s:CONV_K - s, sl]
        out_ref[:, sl] = _silu(acc).astype(out_ref.dtype)
        ext_ref[0:SUBLANES, sl] = cur[t - SUBLANES:t]


def _ssd_kernel(xs_ref, bc_ref, dt_ref, u_ref, wz_ref, wgt_ref, cwx_ref, cwbc_ref, cbx_ref, cbbc_ref, dtb_ref, alog_ref,
                dskip_ref, o_ref, z_ref, gts_ref, tailx, tailbc, state, xc, bcc):
    q = CHUNK
    pair_w = 2 * HEADDIM

    @pl.when(pl.program_id(0) == 0)
    def _():
        tailx[...] = jnp.zeros_like(tailx)
        tailbc[...] = jnp.zeros_like(tailbc)
        state[...] = jnp.zeros_like(state)

    u = u_ref[...]
    z_ref[...] = jnp.dot(u, wz_ref[...], preferred_element_type=F32).astype(BF16)
    gts_ref[...] = jnp.dot(u, wgt_ref[...], preferred_element_type=F32).astype(BF16)

    _conv_silu(xs_ref, tailx, cwx_ref, cbx_ref, xc, D_SSD)
    _conv_silu(bc_ref, tailbc, cwbc_ref, cbbc_ref, bcc, 2 * D_BC)

    for c in range(xs_ref.shape[0] // q):
        rows = slice(c * q, (c + 1) * q)
        v = dt_ref[rows, :] + dtb_ref[...]
        dt = jnp.maximum(v, 0.0) + jnp.log(1.0 + jnp.exp(-jnp.abs(v)))
        a = dt * (-jnp.exp(alog_ref[...]))
        ri = lax.broadcasted_iota(jnp.int32, (q, q), 0)
        ci = lax.broadcasted_iota(jnp.int32, (q, q), 1)
        causal = ri >= ci
        a_cs = jnp.dot(causal.astype(F32), a, preferred_element_type=F32, precision=HIGHEST)
        a_cs_t = a_cs.T
        dt_t = dt.T
        first_half = ci < HEADDIM
        first_half_row = first_half[0:1, :]

        for g in range(N_GROUPS):
            b_g = bcc[rows, g * D_STATE:(g + 1) * D_STATE]
            c_g = bcc[rows, D_BC + g * D_STATE:D_BC + (g + 1) * D_STATE]
            cb = lax.dot_general(c_g, b_g, (((1,), (1,)), ((), ())), preferred_element_type=F32)
            b_t = b_g.astype(F32).T
            st_g = state[g]
            y_off = jnp.dot(c_g, st_g.astype(BF16), preferred_element_type=F32)
            for pi in range(HEADS_PER_GROUP // 2):
                h0 = g * HEADS_PER_GROUP + 2 * pi
                lanes = slice(h0 * HEADDIM, h0 * HEADDIM + pair_w)
                gl = slice(pi * pair_w, (pi + 1) * pair_w)
                ms, ws, cols, lasts = [], [], [], []
                for h in (h0, h0 + 1):
                    col = jnp.broadcast_to(a_cs[:, h:h + 1], (q, q))
                    row = a_cs_t[h:h + 1, :]
                    dtrow = dt_t[h:h + 1, :]
                    decay = jnp.exp(jnp.where(causal, col - row, -jnp.inf))
                    ms.append((cb * decay * dtrow).astype(BF16))
                    last = a_cs_t[h:h + 1, q - 1:q]
                    ws.append((b_t * (jnp.exp(last - row) * dtrow)).astype(BF16))
                    cols.append(col)
                    lasts.append(last)
                xp = xc[rows, lanes]
                zero = jnp.zeros_like(xp)
                x_bd = jnp.concatenate([jnp.where(first_half, xp, zero), jnp.where(first_half, zero, xp)], axis=0)
                y_p = jnp.dot(jnp.concatenate(ms, axis=1), x_bd, preferred_element_type=F32)
                y_p = y_p + y_off[:, gl] * jnp.exp(jnp.where(first_half, cols[0], cols[1]))
                o_ref[rows, lanes] = (y_p + dskip_ref[:, lanes] * xp.astype(F32)).astype(BF16)
                s_new = jnp.dot(jnp.concatenate(ws, axis=1), x_bd, preferred_element_type=F32)
                carry = jnp.exp(jnp.where(first_half_row, lasts[0], lasts[1]))
                state[g, :, gl] = st_g[:, gl] * carry + s_new


def _ssd(proj, dt_raw, u, w_z, w_gates, conv_w, conv_b, dt_bias, a_log, d_skip):
    n = proj.shape[0]
    t = SSD_CHUNKS_PER_STEP * CHUNK
    return pl.pallas_call(
        _ssd_kernel,
        grid=(n // t,),
        in_specs=[pl.BlockSpec((t, D_SSD), lambda i: (i, 0)),
                  pl.BlockSpec((t, 2 * D_BC), lambda i: (i, 1)),
                  pl.BlockSpec((t, LANES), lambda i: (i, 0)),
                  pl.BlockSpec((t, D_MODEL), lambda i: (i, 0)),
                  pl.BlockSpec((D_MODEL, D_SSD), lambda i: (0, 0), pipeline_mode=pl.Buffered(1)),
                  pl.BlockSpec((D_MODEL, 2 * D_MODEL), lambda i: (0, 0), pipeline_mode=pl.Buffered(1)),
                  pl.BlockSpec((CONV_K, D_SSD), lambda i: (0, 0)),
                  pl.BlockSpec((CONV_K, 2 * D_BC), lambda i: (0, 1)),
                  pl.BlockSpec((1, D_SSD), lambda i: (0, 0)),
                  pl.BlockSpec((1, 2 * D_BC), lambda i: (0, 1)),
                  _full((1, LANES)), _full((1, LANES)),
                  _full((1, D_SSD))],
        out_specs=[pl.BlockSpec((t, D_SSD), lambda i: (i, 0)),
                   pl.BlockSpec((t, D_SSD), lambda i: (i, 0)),
                   pl.BlockSpec((t, 2 * D_MODEL), lambda i: (i, 0))],
        out_shape=[jax.ShapeDtypeStruct((n, D_SSD), BF16),
                   jax.ShapeDtypeStruct((n, D_SSD), BF16),
                   jax.ShapeDtypeStruct((n, 2 * D_MODEL), BF16)],
        scratch_shapes=[pltpu.VMEM((SUBLANES + t, D_SSD), F32),
                        pltpu.VMEM((SUBLANES + t, 2 * D_BC), F32),
                        pltpu.VMEM((N_GROUPS, D_STATE, GROUP_W), F32),
                        pltpu.VMEM((t, D_SSD), BF16),
                        pltpu.VMEM((t, 2 * D_BC), BF16)],
        compiler_params=_cparams(),
        name="ssd",
    )(proj, proj, dt_raw, u, w_z, w_gates, conv_w, conv_w, conv_b, conv_b, dt_bias, a_log, d_skip)


PAD_ID = 0xFFFFFFFF
TOK_ROW = PACK_TILES
MASK_ROWS = (PACK_TILES + 1, PACK_TILES + 2)


def _tile_rows(ref, first_row):
    return ref.at[pl.ds(pl.multiple_of(first_row * SUBLANES, SUBLANES), SUBLANES)]


def _mix_kernel(xp_ref, z_ref, ypre_ref, x_ref, gts_ref, sng_ref, poolw_ref, pscale_ref, wbp_ref, wbs_ref, wout_ref,
                g1_ref, ng_ref, sh2_ref, sc2_ref, g2_ref, wr_ref, rb_ref, wsgu_ref, wsd_ref, zero_ref,
                base_ref, ws_ref, cnt_ref, xs_ref, ptail, run, stage, dvm, dsm, sem, sem_s, *, n_tiles, cap):
    t = x_ref.shape[0]
    i = pl.program_id(0)
    slot = lax.rem(i, 2)

    @pl.when(i == 0)
    def _():
        ptail[...] = jnp.zeros_like(ptail)
        run[...] = jnp.zeros_like(run)
        stage[...] = jnp.zeros_like(stage)

    def issue_rows(s, group, after=None):
        zero = 0
        if after is not None:
            probe = jnp.max(jnp.abs(after[0:SUBLANES, 0:min(LANES, after.shape[1])])).astype(jnp.int32)
            zero = probe * zero_ref[0]
        per = t // MIX_ISSUE_GROUPS
        for tt in range(group * per, (group + 1) * per):
            for k in range(TOP_K):
                pltpu.make_async_copy(stage.at[s, pl.ds(tt * SUBLANES, SUBLANES)],
                                      _tile_rows(xs_ref, dsm[s, k, tt] + zero), sem.at[s]).start()

    def drain_rows(s):
        for _ in range(TOP_K):
            pltpu.make_async_copy(stage.at[s], xs_ref.at[pl.ds(0, t * SUBLANES)], sem.at[s]).wait()

    def tile(s, milestone=None):
        _mix_tile_body(xp_ref, z_ref, ypre_ref, x_ref, gts_ref, sng_ref, poolw_ref, pscale_ref, wbp_ref, wbs_ref, wout_ref,
                       g1_ref, ng_ref, sh2_ref, sc2_ref, g2_ref, wr_ref, rb_ref, wsgu_ref, wsd_ref,
                       base_ref, ws_ref, cnt_ref, ptail, run, stage, dvm, dsm, sem_s, i, s, cap, milestone)

    @pl.when(i == 0)
    def _():
        tile(0)

    groups = iter(range(1, MIX_ISSUE_GROUPS))

    def milestone(value):
        if value is BEFORE_STAGING:
            assert next(groups, None) is None
            @pl.when(i >= 2)
            def _():
                drain_rows(slot)
        else:
            issue_rows(1 - slot, next(groups), value)

    def wait_slot_rows(s):
        pltpu.make_async_copy(dvm, dsm.at[s], sem_s).wait()

    @pl.when((i >= 1) & (i < n_tiles))
    def _():
        wait_slot_rows(1 - slot)
        issue_rows(1 - slot, 0)
        tile(slot, milestone)

    @pl.when(i == n_tiles)
    def _():
        wait_slot_rows(1 - slot)
        for group in range(MIX_ISSUE_GROUPS):
            issue_rows(1 - slot, group)
        if n_tiles >= 2:
            drain_rows(slot)
        drain_rows(1 - slot)


def _mix_tile_body(xp_ref, z_ref, ypre_ref, x_ref, gts_ref, sng_ref, poolw_ref, pscale_ref, wbp_ref, wbs_ref, wout_ref,
                   g1_ref, ng_ref, sh2_ref, sc2_ref, g2_ref, wr_ref, rb_ref, wsgu_ref, wsd_ref,
                   base_ref, ws_ref, cnt_ref, ptail, run, stage, dvm, dsm, sem_s, i, slot, cap, milestone):
    t = x_ref.shape[0]
    if milestone is None:
        milestone = lambda value: None

    xp = xp_ref[...].astype(F32)
    ext = jnp.concatenate([ptail[...], xp], axis=0)
    ptail[...] = xp[t - 2 * SUBLANES:t]
    pos = (lax.broadcasted_iota(jnp.int32, (t, 1), 0) + (i * t + 1)).astype(F32)
    pooled = []
    for gi, w in enumerate(POOL_WINDOWS):
        sl = slice(gi * POOL_GDIM, (gi + 1) * POOL_GDIM)
        e = ext[:, sl]
        s = e
        span = 1
        while span < w:
            s = s + pltpu.roll(s, span, axis=0)
            span *= 2
        win = s[2 * SUBLANES:]
        mean = win / jnp.minimum(pos, float(w))
        pg = (mean - xp[:, sl]).astype(BF16)
        mixed = jnp.dot(pg, poolw_ref[gi], preferred_element_type=F32)
        pooled.append(mixed * pscale_ref[:, sl])
        milestone(pooled[-1])
    pooled = jnp.concatenate(pooled, axis=1).astype(BF16)
    y_pool = jnp.dot(pooled, wbp_ref[...], preferred_element_type=F32)
    milestone(y_pool)
    z = z_ref[...].astype(F32)
    milestone(z)
    yn = []
    for g in range(N_GROUPS):
        sl = slice(g * GROUP_W, (g + 1) * GROUP_W)
        yg = ypre_ref[:, sl].astype(F32) * _silu(z[:, sl])
        inv_g = lax.rsqrt(jnp.mean(yg * yg, axis=-1, keepdims=True) + EPS)
        yn.append((yg * inv_g * sng_ref[:, sl]).astype(BF16))
        milestone(inv_g)
    y_ssd = jnp.dot(jnp.concatenate(yn, axis=1), wbs_ref[...], preferred_element_type=F32)
    milestone(y_ssd)
    gates = gts_ref[...].astype(F32)
    milestone(gates)
    g_ssd = jax.nn.sigmoid(gates[:, :D_MODEL])
    g_pool = jax.nn.sigmoid(gates[:, D_MODEL:])
    mixed = (g_ssd * y_ssd + g_pool * y_pool).astype(BF16)
    h = x_ref[...] + g1_ref[...] * jnp.dot(mixed, wout_ref[...], preferred_element_type=F32)
    milestone(h)

    inv = lax.rsqrt(jnp.mean(h * h, axis=-1, keepdims=True) + EPS)
    milestone(inv)
    u2 = h * inv * ng_ref[...]
    u2 = u2 * (1.0 + sc2_ref[...]) + sh2_ref[...]
    u2b = u2.astype(BF16)
    packed = [_pack_bf16_pair(u2[:, j * LANES:(j + 1) * LANES],
                              u2[:, (j + PACK_TILES) * LANES:(j + PACK_TILES + 1) * LANES])
              for j in range(PACK_TILES)]

    hs = jnp.dot(u2b, wsgu_ref[...], preferred_element_type=F32)
    milestone(hs)
    act = (_silu(hs[:, :D_SHARED]) * hs[:, D_SHARED:]).astype(BF16)
    shared = jnp.dot(act, wsd_ref[...], preferred_element_type=F32)
    milestone(shared)
    base_ref[...] = h + g2_ref[...] * shared

    logits = jnp.dot(u2b, wr_ref[...], preferred_element_type=F32)
    scores = jax.nn.sigmoid(logits)
    milestone(scores)
    choice = scores + rb_ref[...]
    lane = lax.broadcasted_iota(jnp.int32, (t, N_EXPERTS), 1)
    lane_f = lane.astype(F32)
    lane_grp = lane // EXPERTS_PER_GROUP
    neg = -jnp.inf
    gscore = []
    for g in range(N_EGROUPS):
        vg = jnp.where(lane_grp == g, choice, neg)
        m1 = jnp.max(vg, axis=-1, keepdims=True)
        i1 = jnp.min(jnp.where(vg == m1, lane_f, float(N_EXPERTS)), axis=-1, keepdims=True)
        m2 = jnp.max(jnp.where(lane_f == i1, neg, vg), axis=-1, keepdims=True)
        gscore.append(m1 + m2)
        milestone(gscore[-1])
    gmask = jnp.zeros((t, N_EXPERTS), jnp.bool_)
    for g in range(N_EGROUPS):
        rank = jnp.zeros((t, 1), F32)
        for g2 in range(N_EGROUPS):
            if g2 == g:
                continue
            better = (gscore[g2] > gscore[g]) | ((gscore[g2] == gscore[g]) & (g2 < g))
            rank = rank + better.astype(F32)
        gmask = gmask | ((rank < float(TOPK_GROUPS)) & (lane_grp == g))
    work = jnp.where(gmask, choice, neg)
    sel = jnp.zeros((t, N_EXPERTS), jnp.bool_)
    onehots, idxs, sks = [], [], []
    for k in range(TOP_K):
        m = jnp.max(work, axis=-1, keepdims=True)
        idx = jnp.min(jnp.where(work == m, lane_f, float(N_EXPERTS)), axis=-1, keepdims=True)
        oh = lane_f == idx
        onehots.append(oh)
        idxs.append(idx)
        sks.append(jnp.sum(jnp.where(oh, scores, 0.0), axis=-1, keepdims=True))
        if k == TOP_K // 2 - 1:
            milestone(sks[-1])
        sel = sel | oh
        work = jnp.where(oh, neg, work)
    denom = sks[0]
    for k in range(1, TOP_K):
        denom = denom + sks[k]
    milestone(denom)

    ri = lax.broadcasted_iota(jnp.int32, (t, t), 0)
    ci = lax.broadcasted_iota(jnp.int32, (t, t), 1)
    before = (ri > ci).astype(BF16)
    sel_f = jnp.where(sel, 1.0, 0.0)
    sel_b = sel_f.astype(BF16)
    pos_tile = jnp.dot(before, sel_b, preferred_element_type=F32) + run[...]
    er = lax.broadcasted_iota(jnp.int32, (N_EXPERTS, N_EXPERTS), 0)
    ec = lax.broadcasted_iota(jnp.int32, (N_EXPERTS, N_EXPERTS), 1)
    rank_tile = jnp.dot(sel_b, (er < ec).astype(BF16), preferred_element_type=F32)
    lane128 = lax.broadcasted_iota(jnp.int32, (t, LANES), 1)
    k_iota = lax.broadcasted_iota(jnp.int32, (t, TOP_K), 1).astype(F32)
    dest = jnp.zeros((t, LANES), F32)
    w_sorted = jnp.zeros((t, TOP_K), F32)
    for k in range(TOP_K):
        pk = jnp.sum(jnp.where(onehots[k], pos_tile, 0.0), axis=-1, keepdims=True)
        rk = jnp.sum(jnp.where(onehots[k], rank_tile, 0.0), axis=-1, keepdims=True)
        dest = jnp.where(lane128 == k, idxs[k] * float(cap) + pk, dest)
        w_sorted = jnp.where(k_iota == rk, sks[k] / denom * ROUTED_SCALE, w_sorted)
    ws_ref[...] = w_sorted
    total = run[...] + jnp.sum(sel_f, axis=0, keepdims=True)
    run[...] = total
    cnt_ref[...] = total

    milestone(BEFORE_STAGING)
    for j in range(PACK_TILES):
        stage[slot, pl.ds(j, t, stride=SUBLANES), :] = packed[j]
    tok = lax.broadcasted_iota(jnp.int32, (t, LANES), 0) + i * t
    stage[slot, pl.ds(TOK_ROW, t, stride=SUBLANES), :] = tok.astype(jnp.uint32)
    bit = jnp.where(sel, jnp.left_shift(1, lane & 15), 0).astype(F32)
    words = []
    for q in range(N_EXPERTS // 16):
        part = jnp.sum(jnp.where((lane >> 4) == q, bit, 0.0), axis=-1, keepdims=True)
        words.append(part.astype(jnp.int32).astype(jnp.uint32))
    for w, row in enumerate(MASK_ROWS):
        word = words[2 * w] | (words[2 * w + 1] << 16)
        stage[slot, pl.ds(row, t, stride=SUBLANES), :] = jnp.broadcast_to(word, (t, LANES))

    dvm[...] = dest.T[0:TOP_K, :].astype(jnp.int32)
    pltpu.make_async_copy(dvm, dsm.at[slot], sem_s).start()


def _mix(proj, z, ypre, x2, gates, ssd_ng, pool_w, pool_scale, w_br_pool, w_br_ssd, w_out, g1, ng, sh2, sc2,
         g2, w_router, router_bias, ws_gu, ws_down, cap):
    n = x2.shape[0]
    t = 256
    n_tiles = n // t
    xp_blk = (D_SSD + 2 * D_BC) // D_MODEL
    const = lambda shape: pl.BlockSpec(shape, lambda i: (0,) * len(shape), pipeline_mode=pl.Buffered(1))
    tile = lambda i: jnp.minimum(i, n_tiles - 1)
    row = lambda i: (tile(i), 0)
    return pl.pallas_call(
        functools.partial(_mix_kernel, n_tiles=n_tiles, cap=cap),
        grid=(n_tiles + 1,),
        in_specs=[pl.BlockSpec((t, D_MODEL), lambda i: (tile(i), xp_blk)),
                  pl.BlockSpec((t, D_SSD), row),
                  pl.BlockSpec((t, D_SSD), row),
                  pl.BlockSpec((t, D_MODEL), row),
                  pl.BlockSpec((t, 2 * D_MODEL), row),
                  _full((1, D_SSD)),
                  const((len(POOL_WINDOWS), POOL_GDIM, POOL_GDIM)),
                  _full((1, D_MODEL)),
                  const((D_MODEL, D_MODEL)),
                  const((D_SSD, D_MODEL)),
                  const((D_MODEL, D_MODEL)),
                  _full((1, D_MODEL)), _full((1, D_MODEL)), _full((1, D_MODEL)), _full((1, D_MODEL)),
                  _full((1, D_MODEL)),
                  _full((D_MODEL, N_EXPERTS)),
                  _full((1, N_EXPERTS)),
                  const((D_MODEL, 2 * D_SHARED)),
                  const((D_SHARED, D_MODEL)),
                  pl.BlockSpec(memory_space=pltpu.SMEM)],
        out_specs=[pl.BlockSpec((t, D_MODEL), row),
                   pl.BlockSpec((t, TOP_K), row),
                   _full((1, N_EXPERTS)),
                   pl.BlockSpec(memory_space=pl.ANY)],
        out_shape=[jax.ShapeDtypeStruct((n, D_MODEL), F32),
                   jax.ShapeDtypeStruct((n, TOP_K), F32),
                   jax.ShapeDtypeStruct((1, N_EXPERTS), F32),
                   jax.ShapeDtypeStruct((N_EXPERTS * cap * SUBLANES, LANES), jnp.uint32)],
        scratch_shapes=[pltpu.VMEM((2 * SUBLANES, D_MODEL), F32),
                        pltpu.VMEM((1, N_EXPERTS), F32),
                        pltpu.VMEM((2, t * SUBLANES, LANES), jnp.uint32),
                        pltpu.VMEM((TOP_K, t), jnp.int32),
                        pltpu.SMEM((2, TOP_K, t), jnp.int32),
                        pltpu.SemaphoreType.DMA((2,)),
                        pltpu.SemaphoreType.DMA(())],
        compiler_params=_cparams(),
        name="mix",
    )(proj, z, ypre, x2, gates, ssd_ng, pool_w, pool_scale, w_br_pool, w_br_ssd, w_out, g1, ng, sh2, sc2, g2,
      w_router, router_bias, ws_gu, ws_down, jnp.zeros((1,), jnp.int32))


def _padfill_kernel(cnt_ref, xs_in_ref, xs_ref, padbuf, sem, *, cap):
    del xs_in_ref
    sub = lax.broadcasted_iota(jnp.int32, padbuf.shape, 0) & (SUBLANES - 1)
    padbuf[...] = jnp.where(sub == TOK_ROW, jnp.uint32(PAD_ID), jnp.uint32(0))
    sizes = [1 << bit for bit in range(EXPERT_BLOCK.bit_length() - 1)]

    def pad_copies(e, fn):
        c = cnt_ref[e]
        n_pad = (EXPERT_BLOCK - (c & (EXPERT_BLOCK - 1))) & (EXPERT_BLOCK - 1)
        first = e * cap + c
        for size in sizes:
            @pl.when((n_pad & size) != 0)
            def _():
                start = first + (n_pad & (size - 1))
                dst = xs_ref.at[pl.ds(pl.multiple_of(start * SUBLANES, SUBLANES), size * SUBLANES)]
                fn(pltpu.make_async_copy(padbuf.at[pl.ds(0, size * SUBLANES)], dst, sem))

    def issue(e, carry):
        pad_copies(e, lambda cp: cp.start())
        return carry

    lax.fori_loop(0, N_EXPERTS, issue, 0)

    def drain(e, carry):
        pad_copies(e, lambda cp: cp.wait())
        return carry

    lax.fori_loop(0, N_EXPERTS, drain, 0)


def _padfill(counts, xs, cap):
    return pl.pallas_call(
        functools.partial(_padfill_kernel, cap=cap),
        grid_spec=pltpu.PrefetchScalarGridSpec(
            num_scalar_prefetch=1,
            grid=(1,),
            in_specs=[pl.BlockSpec(memory_space=pl.ANY)],
            out_specs=pl.BlockSpec(memory_space=pl.ANY),
            scratch_shapes=[pltpu.VMEM((EXPERT_BLOCK // 2 * SUBLANES, LANES), jnp.uint32),
                            pltpu.SemaphoreType.DMA(())]),
        out_shape=jax.ShapeDtypeStruct(xs.shape, xs.dtype),
        input_output_aliases={1: 0},
        compiler_params=_cparams(),
        name="padfill",
    )(counts, xs)


def _expert_kernel(be_ref, br_ref, nb_ref, x_ref, wg_ref, wu_ref, wd_ref, ytok_ref, ybuf0, ybuf1, idv, ids0, ids1,
                   wgb, wub, wdb, sem, sem_ids, *, n_ids):
    del br_ref
    m = EXPERT_BLOCK
    b = pl.program_id(0)
    nb = nb_ref[0]
    ybufs, idss = (ybuf0, ybuf1), (ids0, ids1)
    n_phases = m // LANES
    half = D_MODEL // 2

    @pl.when((b == 0) | (be_ref[b] != be_ref[jnp.maximum(b - 1, 0)]))
    def _():
        wgb[...] = wg_ref[0].astype(BF16)
        wub[...] = wu_ref[0].astype(BF16)
        wdb[...] = wd_ref[0].astype(BF16)

    def half_tile(ref, unit):
        return ref.at[pl.ds(pl.multiple_of(unit * PACK_TILES, PACK_TILES), PACK_TILES)]

    def issue(s, a0):
        for col in range(LANES):
            pltpu.make_async_copy(half_tile(ybufs[s], a0 * LANES + col),
                                  half_tile(ytok_ref, idss[s][a0, col]), sem.at[s]).start()

    def step(s, prev):
        tokrep = x_ref[pl.ds(TOK_ROW, m, stride=SUBLANES), :]
        e = be_ref[b]
        below_lo = jnp.where(e >= 32, -1, (1 << jnp.minimum(e, 31)) - 1).astype(jnp.uint32)
        below_hi = jnp.where(e >= 32, (1 << jnp.maximum(e - 32, 0)) - 1, 0).astype(jnp.uint32)
        rank = (lax.population_count(x_ref[pl.ds(MASK_ROWS[0], m, stride=SUBLANES), :] & below_lo)
                + lax.population_count(x_ref[pl.ds(MASK_ROWS[1], m, stride=SUBLANES), :] & below_hi))
        r = lax.broadcasted_iota(jnp.int32, (m, LANES), 0)
        ln = lax.broadcasted_iota(jnp.int32, (m, LANES), 1)
        n_tok = n_ids // TOP_K
        rank = rank.astype(jnp.int32)
        half_rank = TOP_K // 2
        unit = ((rank % half_rank) * n_tok + tokrep.astype(jnp.int32)) * 2 + rank // half_rank
        idi = jnp.where(tokrep == jnp.uint32(PAD_ID), n_ids + r, unit)
        diag = jnp.where((r & (LANES - 1)) == ln, idi, 0).astype(F32)
        idv[...] = jnp.sum(diag.reshape(m // LANES, LANES, LANES), axis=1).astype(jnp.int32)
        ids_copy = pltpu.make_async_copy(idv, idss[s], sem_ids)
        ids_copy.start()

        halves = [_unpack_bf16_pair(x_ref[pl.ds(j, m, stride=SUBLANES), :]) for j in range(PACK_TILES)]
        x = jnp.concatenate([lo for lo, _ in halves] + [hi for _, hi in halves], axis=1).astype(BF16)
        phases = iter(range(n_phases))
        if prev is not None:
            issue(prev, next(phases))
        gate = jnp.dot(x, wgb[...], preferred_element_type=F32)
        if prev is not None:
            issue(prev, next(phases))
        up = jnp.dot(x, wub[...], preferred_element_type=F32)
        hid = (_silu(gate) * up).astype(BF16)

        def drain(which):
            pltpu.make_async_copy(ybufs[which], ytok_ref.at[pl.ds(0, m * PACK_TILES)], sem.at[which]).wait()

        if prev is not None:
            @pl.when(b >= 2)
            def _():
                drain(s)
        ys = []
        for c0 in (0, half):
            if prev is not None:
                issue(prev, next(phases))
            ys.append(jnp.dot(hid, wdb[:, c0:c0 + half], preferred_element_type=F32))
        for j in range(PACK_TILES):
            sl = slice(j * LANES, (j + 1) * LANES)
            ybufs[s][pl.ds(j, m, stride=PACK_TILES), :] = _pack_bf16_pair(ys[0][:, sl], ys[1][:, sl])
        ids_copy.wait()
        if prev is not None:
            for a0 in phases:
                issue(prev, a0)

            @pl.when(b == nb)
            def _():
                drain(prev)

    @pl.when(b == 0)
    def _():
        step(0, None)

    for s in range(2):
        @pl.when((b >= 1) & (b <= nb) & (lax.rem(b, 2) == s))
        def _():
            step(s, 1 - s)


def _experts(block_e, block_row, nb_used, xs, we_gate, we_up, we_down, n_ids):
    n_blocks = block_e.shape[0]
    m = EXPERT_BLOCK
    rows = m * SUBLANES
    return pl.pallas_call(
        functools.partial(_expert_kernel, n_ids=n_ids),
        grid_spec=pltpu.PrefetchScalarGridSpec(
            num_scalar_prefetch=3,
            grid=(n_blocks,),
            in_specs=[pl.BlockSpec((rows, LANES), lambda b, be, br, nb: (br[b], 0)),
                      pl.BlockSpec((1, D_MODEL, D_EXPERT), lambda b, be, br, nb: (be[b], 0, 0)),
                      pl.BlockSpec((1, D_MODEL, D_EXPERT), lambda b, be, br, nb: (be[b], 0, 0)),
                      pl.BlockSpec((1, D_EXPERT, D_MODEL), lambda b, be, br, nb: (be[b], 0, 0))],
            out_specs=pl.BlockSpec(memory_space=pl.ANY),
            scratch_shapes=[pltpu.VMEM((m * PACK_TILES, LANES), jnp.uint32),
                            pltpu.VMEM((m * PACK_TILES, LANES), jnp.uint32),
                            pltpu.VMEM((m // LANES, LANES), jnp.int32),
                            pltpu.SMEM((m // LANES, LANES), jnp.int32),
                            pltpu.SMEM((m // LANES, LANES), jnp.int32),
                            pltpu.VMEM((D_MODEL, D_EXPERT), BF16),
                            pltpu.VMEM((D_MODEL, D_EXPERT), BF16),
                            pltpu.VMEM((D_EXPERT, D_MODEL), BF16),
                            pltpu.SemaphoreType.DMA((2,)),
                            pltpu.SemaphoreType.DMA(())]),
        out_shape=jax.ShapeDtypeStruct(((n_ids + m) * PACK_TILES, LANES), jnp.uint32),
        compiler_params=_cparams(),
        name="experts",
    )(block_e, block_row, nb_used, xs, we_gate, we_up, we_down)


def _combine_kernel(*refs):
    half_rank = TOP_K // 2
    y_refs = refs[:half_rank]
    w8_ref, base_ref, g2_ref, fg_ref, o_ref = refs[half_rank:]
    t7 = base_ref.shape[0]
    w8 = w8_ref[...]
    parts = [jnp.zeros((t7, LANES), F32) for _ in range(ROW_TILES)]
    for p in range(half_rank):
        for h in range(2):
            wk = w8[:, p + h * half_rank:p + h * half_rank + 1]
            for j in range(PACK_TILES):
                lo, hi = _unpack_bf16_pair(y_refs[p][pl.ds(h * PACK_TILES + j, t7, stride=SUBLANES), :])
                parts[j] = parts[j] + wk * lo
                parts[j + PACK_TILES] = parts[j + PACK_TILES] + wk * hi
    routed = jnp.concatenate(parts, axis=1)
    h = base_ref[...] + g2_ref[...] * routed
    inv = lax.rsqrt(jnp.mean(h * h, axis=-1, keepdims=True) + EPS)
    o_ref[...] = h * inv * fg_ref[...]


def _combine(ytok, w8, base, g2, fg):
    n = base.shape[0]
    t7 = 256
    y_specs = [pl.BlockSpec((t7 * SUBLANES, LANES), functools.partial(lambda i, p: (p * (n // t7) + i, 0), p=p))
               for p in range(TOP_K // 2)]
    return pl.pallas_call(
        _combine_kernel,
        grid=(n // t7,),
        in_specs=y_specs + [
                  pl.BlockSpec((t7, TOP_K), lambda i: (i, 0)),
                  pl.BlockSpec((t7, D_MODEL), lambda i: (i, 0)),
                  _full((1, D_MODEL)), _full((1, D_MODEL))],
        out_specs=pl.BlockSpec((t7, D_MODEL), lambda i: (i, 0)),
        out_shape=jax.ShapeDtypeStruct((n, D_MODEL), F32),
        compiler_params=_cparams(),
        name="combine",
    )(*([ytok] * (TOP_K // 2)), w8, base, g2, fg)


def kernel(x, c, w_ada, b_ada, norm_mix_g, w_in, conv_w, conv_b, dt_bias, A_log, D_skip, ssd_norm_g, pool_w,
           pool_scale, w_br_ssd, w_br_pool, w_out, norm_ffn_g, w_router, router_bias, we_gate, we_up, we_down,
           ws_gate, ws_up, ws_down, final_norm_g):
    bsz, seq, _ = x.shape
    assert bsz == 1 and w_ada.shape[0] == 1
    n = seq
    h2 = x.reshape(n, D_MODEL)

    mod = _ada(c, w_ada[0], b_ada[0])
    sh1, sc1, g1, sh2, sc2, g2 = [mod[:, k * D_MODEL:(k + 1) * D_MODEL] for k in range(6)]

    wi = w_in[0]
    o_xbc, o_dt = D_SSD, D_SSD + D_SSD + 2 * D_BC
    o_pool = o_dt + N_HEADS
    o_gates = o_pool + D_MODEL
    w_dt = jnp.pad(wi[:, o_dt:o_pool], ((0, 0), (0, LANES - N_HEADS))).astype(BF16)
    proj, dt_raw, u = _in_proj(h2, norm_mix_g[0][None], sh1, sc1, wi[:, o_xbc:o_dt].astype(BF16),
                               wi[:, o_pool:o_gates].astype(BF16), w_dt)

    pad_h = lambda v: jnp.pad(v[None], ((0, 0), (0, LANES - N_HEADS)))
    ypre, z, gates = _ssd(proj, dt_raw, u, wi[:, :o_xbc].astype(BF16), wi[:, o_gates:].astype(BF16), conv_w[0],
                          conv_b[0][None], pad_h(dt_bias[0]), pad_h(A_log[0]), jnp.repeat(D_skip[0], HEADDIM)[None])

    ws_gu = jnp.concatenate([ws_gate[0], ws_up[0]], axis=1).astype(BF16)
    m = EXPERT_BLOCK
    cap = -(-n // m) * m
    base, w8, cnt, xs = _mix(
        proj, z, ypre, h2, gates, ssd_norm_g[0][None],
        pool_w[0].astype(BF16), pool_scale[0][None], w_br_pool[0].astype(BF16),
        w_br_ssd[0].astype(BF16), w_out[0].astype(BF16), g1, norm_ffn_g[0][None], sh2, sc2, g2,
        w_router[0].astype(BF16), router_bias[0][None], ws_gu, ws_down[0].astype(BF16), cap)

    counts = cnt[0].astype(jnp.int32)
    nblk = (counts + m - 1) // m
    bends = jnp.cumsum(nblk)
    bstarts = bends - nblk
    nb_used = bends[-1]
    n_blocks = -(-(n * TOP_K) // m) + N_EXPERTS + 1
    b_eff = jnp.minimum(jnp.arange(n_blocks, dtype=jnp.int32), nb_used - 1)
    done = bends[None, :] <= b_eff[:, None]
    block_e = jnp.sum(done.astype(jnp.int32), axis=1)
    block_row = block_e * (cap // m) + b_eff - jnp.sum(jnp.where(done, nblk[None, :], 0), axis=1)

    xs = _padfill(counts, xs, cap)
    ytok = _experts(block_e, block_row, nb_used.reshape(1), xs,
                    we_gate[0], we_up[0], we_down[0], n * TOP_K)
    out = _combine(ytok, w8, base, g2, final_norm_g[None])
    return out.reshape(bsz, seq, D_MODEL)
```

```python
import functools

import jax
import jax.numpy as jnp
from jax import lax
from jax.experimental import pallas as pl
from jax.experimental.pallas import tpu as pltpu

F32 = jnp.float32
BF16 = jnp.bfloat16
HIGHEST = lax.Precision.HIGHEST

D_MODEL = 1024
D_SSD = 2048
HEADDIM = 64
N_HEADS = 32
N_GROUPS = 8
HEADS_PER_GROUP = N_HEADS // N_GROUPS
D_STATE = 128
CONV_K = 4
CHUNK = 128
SSD_CHUNKS_PER_STEP = 2
GROUP_W = D_SSD // N_GROUPS
D_BC = N_GROUPS * D_STATE
POOL_WINDOWS = (2, 4, 8, 16)
POOL_GDIM = 256
N_EXPERTS = 64
TOP_K = 8
N_EGROUPS = 8
EXPERTS_PER_GROUP = 8
TOPK_GROUPS = 4
D_EXPERT = 256
D_SHARED = 256
ROUTED_SCALE = 2.5
MOE_BLOCK = 128
EPS = 1e-6

LANES = 128
SUBLANES = 8
ROW_TILES = D_MODEL // LANES
PACK_TILES = ROW_TILES // 2
PACK_W = PACK_TILES * LANES
EXPERT_BLOCK = 512
MIX_ISSUE_GROUPS = 32
BEFORE_STAGING = "before_staging"

PROJ_W = D_SSD + 2 * D_BC + D_MODEL
PROJ_CHUNK = 512

VMEM_LIMIT = 56 * 1024 * 1024


def _cparams(sem=("arbitrary",)):
    return pltpu.CompilerParams(dimension_semantics=sem, vmem_limit_bytes=VMEM_LIMIT)


def _full(shape):
    nd = len(shape)
    return pl.BlockSpec(shape, lambda *_: (0,) * nd)


def _silu(v):
    return v * jax.nn.sigmoid(v)


def _pack_bf16_pair(lo, hi):
    lo_bits = lax.bitcast_convert_type(lo.astype(BF16).astype(F32), jnp.uint32)
    hi_bits = lax.bitcast_convert_type(hi.astype(BF16).astype(F32), jnp.uint32)
    return (lo_bits >> 16) | (hi_bits & jnp.uint32(0xFFFF0000))


def _unpack_bf16_pair(words):
    return (lax.bitcast_convert_type(words << 16, F32),
            lax.bitcast_convert_type(words & jnp.uint32(0xFFFF0000), F32))


def _ada_kernel(c_ref, w_ref, b_ref, o_ref):
    c = c_ref[...]
    o_ref[...] = jnp.dot(_silu(c), w_ref[...], preferred_element_type=F32, precision=HIGHEST) + b_ref[...]


def _ada(c, w_ada, b_ada):
    n_out = w_ada.shape[1]
    tn = 1536
    c8 = jnp.broadcast_to(c, (SUBLANES, D_MODEL))
    out = pl.pallas_call(
        _ada_kernel,
        grid=(n_out // tn,),
        in_specs=[_full((SUBLANES, D_MODEL)),
                  pl.BlockSpec((D_MODEL, tn), lambda j: (0, j)),
                  pl.BlockSpec((1, tn), lambda j: (0, j))],
        out_specs=pl.BlockSpec((SUBLANES, tn), lambda j: (0, j)),
        out_shape=jax.ShapeDtypeStruct((SUBLANES, n_out), F32),
        compiler_params=_cparams(),
        name="ada",
    )(c8, w_ada, b_ada.reshape(1, n_out))
    return out[0:1]


def _inproj_kernel(x_ref, g_ref, sh_ref, sc_ref, wa_ref, wb_ref, wdt_ref, proj_ref, dt_ref, u_ref):
    x = x_ref[...]
    inv = lax.rsqrt(jnp.mean(x * x, axis=-1, keepdims=True) + EPS)
    u = x * inv * g_ref[...]
    u = u * (1.0 + sc_ref[...]) + sh_ref[...]
    ub = u.astype(BF16)
    u_ref[...] = ub
    wa = wa_ref.shape[1]
    for c0 in range(0, PROJ_W, PROJ_CHUNK):
        w = wa_ref[:, c0:c0 + PROJ_CHUNK] if c0 < wa else wb_ref[:, c0 - wa:c0 - wa + PROJ_CHUNK]
        proj_ref[:, c0:c0 + PROJ_CHUNK] = jnp.dot(ub, w, preferred_element_type=F32).astype(BF16)
    dt_ref[...] = jnp.dot(ub, wdt_ref[...], preferred_element_type=F32)


def _in_proj(x2, g, sh, sc, w_a, w_b, w_dt):
    n = x2.shape[0]
    tm = 512
    return pl.pallas_call(
        _inproj_kernel,
        grid=(n // tm,),
        in_specs=[pl.BlockSpec((tm, D_MODEL), lambda i: (i, 0)),
                  _full((1, D_MODEL)), _full((1, D_MODEL)), _full((1, D_MODEL)),
                  pl.BlockSpec(w_a.shape, lambda i: (0, 0), pipeline_mode=pl.Buffered(1)),
                  pl.BlockSpec(w_b.shape, lambda i: (0, 0), pipeline_mode=pl.Buffered(1)),
                  _full((D_MODEL, LANES))],
        out_specs=[pl.BlockSpec((tm, PROJ_W), lambda i: (i, 0)),
                   pl.BlockSpec((tm, LANES), lambda i: (i, 0)),
                   pl.BlockSpec((tm, D_MODEL), lambda i: (i, 0))],
        out_shape=[jax.ShapeDtypeStruct((n, PROJ_W), BF16),
                   jax.ShapeDtypeStruct((n, LANES), F32),
                   jax.ShapeDtypeStruct((n, D_MODEL), BF16)],
        compiler_params=_cparams(),
        name="in_proj",
    )(x2, g, sh, sc, w_a, w_b, w_dt)


def _conv_silu(cur_ref, ext_ref, w_ref, b_ref, out_ref, width, cw=512):
    t = cur_ref.shape[0]
    for c0 in range(0, width, cw):
        sl = slice(c0, c0 + cw)
        cur = cur_ref[:, sl].astype(F32)
        ext_ref[SUBLANES:, sl] = cur
        acc = cur * w_ref[CONV_K - 1:CONV_K, sl] + b_ref[:, sl]
        for s in range(1, CONV_K):
            acc = acc + ext_ref[pl.ds(SUBLANES - s, t), sl] * w_ref[CONV_K - 1 - s:CONV_K - s, sl]
        out_ref[:, sl] = _silu(acc).astype(out_ref.dtype)
        ext_ref[0:SUBLANES, sl] = cur[t - SUBLANES:t]


def _ssd_kernel(xs_ref, bc_ref, dt_ref, cwx_ref, cwbc_ref, cbx_ref, cbbc_ref, dtb_ref, alog_ref,
                dskip_ref, o_ref, tailx, tailbc, state, xc, bcc):
    q = CHUNK
    pair_w = 2 * HEADDIM

    @pl.when(pl.program_id(0) == 0)
    def _():
        tailx[...] = jnp.zeros_like(tailx)
        tailbc[...] = jnp.zeros_like(tailbc)
        state[...] = jnp.zeros_like(state)

    _conv_silu(xs_ref, tailx, cwx_ref, cbx_ref, xc, D_SSD)
    _conv_silu(bc_ref, tailbc, cwbc_ref, cbbc_ref, bcc, 2 * D_BC)

    for c in range(xs_ref.shape[0] // q):
        rows = slice(c * q, (c + 1) * q)
        v = dt_ref[rows, :] + dtb_ref[...]
        dt = jnp.maximum(v, 0.0) + jnp.log(1.0 + jnp.exp(-jnp.abs(v)))
        a = dt * (-jnp.exp(alog_ref[...]))
        ri = lax.broadcasted_iota(jnp.int32, (q, q), 0)
        ci = lax.broadcasted_iota(jnp.int32, (q, q), 1)
        causal = ri >= ci
        a_cs = jnp.dot(causal.astype(F32), a, preferred_element_type=F32, precision=HIGHEST)
        a_cs_t = a_cs.T
        dt_t = dt.T
        first_half = ci < HEADDIM
        first_half_row = first_half[0:1, :]

        for g in range(N_GROUPS):
            b_g = bcc[rows, g * D_STATE:(g + 1) * D_STATE]
            c_g = bcc[rows, D_BC + g * D_STATE:D_BC + (g + 1) * D_STATE]
            cb = lax.dot_general(c_g, b_g, (((1,), (1,)), ((), ())), preferred_element_type=F32)
            b_t = b_g.astype(F32).T
            st_g = state[g]
            y_off = jnp.dot(c_g, st_g.astype(BF16), preferred_element_type=F32)
            for pi in range(HEADS_PER_GROUP // 2):
                h0 = g * HEADS_PER_GROUP + 2 * pi
                lanes = slice(h0 * HEADDIM, h0 * HEADDIM + pair_w)
                gl = slice(pi * pair_w, (pi + 1) * pair_w)
                ms, ws, cols, lasts = [], [], [], []
                for h in (h0, h0 + 1):
                    col = jnp.broadcast_to(a_cs[:, h:h + 1], (q, q))
                    row = a_cs_t[h:h + 1, :]
                    dtrow = dt_t[h:h + 1, :]
                    decay = jnp.exp(jnp.where(causal, col - row, -jnp.inf))
                    ms.append((cb * decay * dtrow).astype(BF16))
                    last = a_cs_t[h:h + 1, q - 1:q]
                    ws.append((b_t * (jnp.exp(last - row) * dtrow)).astype(BF16))
                    cols.append(col)
                    lasts.append(last)
                xp = xc[rows, lanes]
                zero = jnp.zeros_like(xp)
                x_bd = jnp.concatenate([jnp.where(first_half, xp, zero), jnp.where(first_half, zero, xp)], axis=0)
                y_p = jnp.dot(jnp.concatenate(ms, axis=1), x_bd, preferred_element_type=F32)
                y_p = y_p + y_off[:, gl] * jnp.exp(jnp.where(first_half, cols[0], cols[1]))
                o_ref[rows, lanes] = (y_p + dskip_ref[:, lanes] * xp.astype(F32)).astype(BF16)
                s_new = jnp.dot(jnp.concatenate(ws, axis=1), x_bd, preferred_element_type=F32)
                carry = jnp.exp(jnp.where(first_half_row, lasts[0], lasts[1]))
                state[g, :, gl] = st_g[:, gl] * carry + s_new


def _ssd(proj, dt_raw, conv_w, conv_b, dt_bias, a_log, d_skip):
    n = proj.shape[0]
    t = SSD_CHUNKS_PER_STEP * CHUNK
    return pl.pallas_call(
        _ssd_kernel,
        grid=(n // t,),
        in_specs=[pl.BlockSpec((t, D_SSD), lambda i: (i, 0)),
                  pl.BlockSpec((t, 2 * D_BC), lambda i: (i, 1)),
                  pl.BlockSpec((t, LANES), lambda i: (i, 0)),
                  pl.BlockSpec((CONV_K, D_SSD), lambda i: (0, 0)),
                  pl.BlockSpec((CONV_K, 2 * D_BC), lambda i: (0, 1)),
                  pl.BlockSpec((1, D_SSD), lambda i: (0, 0)),
                  pl.BlockSpec((1, 2 * D_BC), lambda i: (0, 1)),
                  _full((1, LANES)), _full((1, LANES)),
                  _full((1, D_SSD))],
        out_specs=pl.BlockSpec((t, D_SSD), lambda i: (i, 0)),
        out_shape=jax.ShapeDtypeStruct((n, D_SSD), BF16),
        scratch_shapes=[pltpu.VMEM((SUBLANES + t, D_SSD), F32),
                        pltpu.VMEM((SUBLANES + t, 2 * D_BC), F32),
                        pltpu.VMEM((N_GROUPS, D_STATE, GROUP_W), F32),
                        pltpu.VMEM((t, D_SSD), BF16),
                        pltpu.VMEM((t, 2 * D_BC), BF16)],
        compiler_params=_cparams(),
        name="ssd",
    )(proj, proj, dt_raw, conv_w, conv_w, conv_b, conv_b, dt_bias, a_log, d_skip)


PAD_ID = 0xFFFFFFFF
TOK_ROW = PACK_TILES
MASK_ROWS = (PACK_TILES + 1, PACK_TILES + 2)


def _tile_rows(ref, first_row):
    return ref.at[pl.ds(pl.multiple_of(first_row * SUBLANES, SUBLANES), SUBLANES)]


def _mix_kernel(xp_ref, u_ref, ypre_ref, x_ref, wz_ref, wgt_ref, sng_ref, poolw_ref, pscale_ref, wbp_ref, wbs_ref, wout_ref,
                g1_ref, ng_ref, sh2_ref, sc2_ref, g2_ref, wr_ref, rb_ref, wsgu_ref, wsd_ref, zero_ref,
                base_ref, ws_ref, cnt_ref, xs_ref, ptail, run, stage, dvm, dsm, sem, sem_s, *, n_tiles, cap):
    t = x_ref.shape[0]
    i = pl.program_id(0)
    slot = lax.rem(i, 2)

    @pl.when(i == 0)
    def _():
        ptail[...] = jnp.zeros_like(ptail)
        run[...] = jnp.zeros_like(run)
        stage[...] = jnp.zeros_like(stage)

    def issue_rows(s, group, after=None):
        zero = 0
        if after is not None:
            probe = jnp.max(jnp.abs(after[0:SUBLANES, 0:min(LANES, after.shape[1])])).astype(jnp.int32)
            zero = probe * zero_ref[0]
        per = t // MIX_ISSUE_GROUPS
        for tt in range(group * per, (group + 1) * per):
            for k in range(TOP_K):
                pltpu.make_async_copy(stage.at[s, pl.ds(tt * SUBLANES, SUBLANES)],
                                      _tile_rows(xs_ref, dsm[s, k, tt] + zero), sem.at[s]).start()

    def drain_rows(s):
        for _ in range(TOP_K):
            pltpu.make_async_copy(stage.at[s], xs_ref.at[pl.ds(0, t * SUBLANES)], sem.at[s]).wait()

    def tile(s, milestone=None):
        _mix_tile_body(xp_ref, u_ref, ypre_ref, x_ref, wz_ref, wgt_ref, sng_ref, poolw_ref, pscale_ref, wbp_ref, wbs_ref, wout_ref,
                       g1_ref, ng_ref, sh2_ref, sc2_ref, g2_ref, wr_ref, rb_ref, wsgu_ref, wsd_ref,
                       base_ref, ws_ref, cnt_ref, ptail, run, stage, dvm, dsm, sem_s, i, s, cap, milestone)

    @pl.when(i == 0)
    def _():
        tile(0)

    groups = iter(range(1, MIX_ISSUE_GROUPS))

    def milestone(value):
        if value is BEFORE_STAGING:
            assert next(groups, None) is None
            @pl.when(i >= 2)
            def _():
                drain_rows(slot)
        else:
            issue_rows(1 - slot, next(groups), value)

    def wait_slot_rows(s):
        pltpu.make_async_copy(dvm, dsm.at[s], sem_s).wait()

    @pl.when((i >= 1) & (i < n_tiles))
    def _():
        wait_slot_rows(1 - slot)
        issue_rows(1 - slot, 0)
        tile(slot, milestone)

    @pl.when(i == n_tiles)
    def _():
        wait_slot_rows(1 - slot)
        for group in range(MIX_ISSUE_GROUPS):
            issue_rows(1 - slot, group)
        if n_tiles >= 2:
            drain_rows(slot)
        drain_rows(1 - slot)


def _mix_tile_body(xp_ref, u_ref, ypre_ref, x_ref, wz_ref, wgt_ref, sng_ref, poolw_ref, pscale_ref, wbp_ref, wbs_ref, wout_ref,
                   g1_ref, ng_ref, sh2_ref, sc2_ref, g2_ref, wr_ref, rb_ref, wsgu_ref, wsd_ref,
                   base_ref, ws_ref, cnt_ref, ptail, run, stage, dvm, dsm, sem_s, i, slot, cap, milestone):
    t = x_ref.shape[0]
    if milestone is None:
        milestone = lambda value: None

    xp = xp_ref[...].astype(F32)
    ext = jnp.concatenate([ptail[...], xp], axis=0)
    ptail[...] = xp[t - 2 * SUBLANES:t]
    pos = (lax.broadcasted_iota(jnp.int32, (t, 1), 0) + (i * t + 1)).astype(F32)
    pooled = []
    for gi, w in enumerate(POOL_WINDOWS):
        sl = slice(gi * POOL_GDIM, (gi + 1) * POOL_GDIM)
        e = ext[:, sl]
        s = e
        span = 1
        while span < w:
            s = s + pltpu.roll(s, span, axis=0)
            span *= 2
        win = s[2 * SUBLANES:]
        mean = win / jnp.minimum(pos, float(w))
        pg = (mean - xp[:, sl]).astype(BF16)
        mixed = jnp.dot(pg, poolw_ref[gi], preferred_element_type=F32)
        pooled.append(mixed * pscale_ref[:, sl])
        milestone(pooled[-1])
    pooled = jnp.concatenate(pooled, axis=1).astype(BF16)
    y_pool = jnp.dot(pooled, wbp_ref[...], preferred_element_type=F32)
    milestone(y_pool)
    ub = u_ref[...]
    z = jnp.dot(ub, wz_ref[...], preferred_element_type=F32)
    milestone(z)
    yn = []
    for g in range(N_GROUPS):
        sl = slice(g * GROUP_W, (g + 1) * GROUP_W)
        yg = ypre_ref[:, sl].astype(F32) * _silu(z[:, sl])
        inv_g = lax.rsqrt(jnp.mean(yg * yg, axis=-1, keepdims=True) + EPS)
        yn.append((yg * inv_g * sng_ref[:, sl]).astype(BF16))
        milestone(inv_g)
    y_ssd = jnp.dot(jnp.concatenate(yn, axis=1), wbs_ref[...], preferred_element_type=F32)
    milestone(y_ssd)
    gates = jnp.dot(ub, wgt_ref[...], preferred_element_type=F32)
    milestone(gates)
    g_ssd = jax.nn.sigmoid(gates[:, :D_MODEL])
    g_pool = jax.nn.sigmoid(gates[:, D_MODEL:])
    mixed = (g_ssd * y_ssd + g_pool * y_pool).astype(BF16)
    h = x_ref[...] + g1_ref[...] * jnp.dot(mixed, wout_ref[...], preferred_element_type=F32)
    milestone(h)

    inv = lax.rsqrt(jnp.mean(h * h, axis=-1, keepdims=True) + EPS)
    milestone(inv)
    u2 = h * inv * ng_ref[...]
    u2 = u2 * (1.0 + sc2_ref[...]) + sh2_ref[...]
    u2b = u2.astype(BF16)
    packed = [_pack_bf16_pair(u2[:, j * LANES:(j + 1) * LANES],
                              u2[:, (j + PACK_TILES) * LANES:(j + PACK_TILES + 1) * LANES])
              for j in range(PACK_TILES)]

    hs = jnp.dot(u2b, wsgu_ref[...], preferred_element_type=F32)
    milestone(hs)
    act = (_silu(hs[:, :D_SHARED]) * hs[:, D_SHARED:]).astype(BF16)
    shared = jnp.dot(act, wsd_ref[...], preferred_element_type=F32)
    milestone(shared)
    base_ref[...] = h + g2_ref[...] * shared

    logits = jnp.dot(u2b, wr_ref[...], preferred_element_type=F32)
    scores = jax.nn.sigmoid(logits)
    milestone(scores)
    choice = scores + rb_ref[...]
    lane = lax.broadcasted_iota(jnp.int32, (t, N_EXPERTS), 1)
    lane_f = lane.astype(F32)
    neg = -jnp.inf
    choice_t = jnp.concatenate([choice, jnp.full((t, LANES - N_EXPERTS), neg, F32)], axis=1).T
    sub_f = lax.broadcasted_iota(jnp.int32, (EXPERTS_PER_GROUP, t), 0).astype(F32)
    gscore = []
    for g in range(N_EGROUPS):
        vg = choice_t[g * EXPERTS_PER_GROUP:(g + 1) * EXPERTS_PER_GROUP, :]
        m1 = jnp.max(vg, axis=0, keepdims=True)
        i1 = jnp.min(jnp.where(vg == m1, sub_f, float(EXPERTS_PER_GROUP)), axis=0, keepdims=True)
        m2 = jnp.max(jnp.where(sub_f == i1, neg, vg), axis=0, keepdims=True)
        gscore.append(m1 + m2)
        milestone(gscore[-1])
    keep_rows = []
    for g in range(N_EGROUPS):
        rank = jnp.zeros((1, t), F32)
        for g2 in range(N_EGROUPS):
            if g2 == g:
                continue
            better = (gscore[g2] > gscore[g]) | ((gscore[g2] == gscore[g]) & (g2 < g))
            rank = rank + better.astype(F32)
        keep = jnp.where(rank < float(TOPK_GROUPS), 1.0, 0.0)
        keep_rows.append(jnp.broadcast_to(keep, (EXPERTS_PER_GROUP, t)))
    keep_rows.append(jnp.zeros((LANES - N_EXPERTS, t), F32))
    gmask = jnp.concatenate(keep_rows, axis=0).T[:, 0:N_EXPERTS] > 0.5
    work = jnp.where(gmask, choice, neg)
    sel = jnp.zeros((t, N_EXPERTS), jnp.bool_)
    onehots, idxs, sks = [], [], []
    for k in range(TOP_K):
        m = jnp.max(work, axis=-1, keepdims=True)
        idx = jnp.min(jnp.where(work == m, lane_f, float(N_EXPERTS)), axis=-1, keepdims=True)
        oh = lane_f == idx
        onehots.append(oh)
        idxs.append(idx)
        sks.append(jnp.sum(jnp.where(oh, scores, 0.0), axis=-1, keepdims=True))
        if k == TOP_K // 2 - 1:
            milestone(sks[-1])
        sel = sel | oh
        work = jnp.where(oh, neg, work)
    denom = sks[0]
    for k in range(1, TOP_K):
        denom = denom + sks[k]
    milestone(denom)

    ri = lax.broadcasted_iota(jnp.int32, (t, t), 0)
    ci = lax.broadcasted_iota(jnp.int32, (t, t), 1)
    before = (ri > ci).astype(BF16)
    sel_f = jnp.where(sel, 1.0, 0.0)
    sel_b = sel_f.astype(BF16)
    pos_tile = jnp.dot(before, sel_b, preferred_element_type=F32) + run[...]
    er = lax.broadcasted_iota(jnp.int32, (N_EXPERTS, N_EXPERTS), 0)
    ec = lax.broadcasted_iota(jnp.int32, (N_EXPERTS, N_EXPERTS), 1)
    rank_tile = jnp.dot(sel_b, (er < ec).astype(BF16), preferred_element_type=F32)
    lane128 = lax.broadcasted_iota(jnp.int32, (t, LANES), 1)
    k_iota = lax.broadcasted_iota(jnp.int32, (t, TOP_K), 1).astype(F32)
    dest = jnp.zeros((t, LANES), F32)
    w_sorted = jnp.zeros((t, TOP_K), F32)
    for k in range(TOP_K):
        pk = jnp.sum(jnp.where(onehots[k], pos_tile, 0.0), axis=-1, keepdims=True)
        rk = jnp.sum(jnp.where(onehots[k], rank_tile, 0.0), axis=-1, keepdims=True)
        dest = jnp.where(lane128 == k, idxs[k] * float(cap) + pk, dest)
        w_sorted = jnp.where(k_iota == rk, sks[k] / denom * ROUTED_SCALE, w_sorted)
    ws_ref[...] = w_sorted
    total = run[...] + jnp.sum(sel_f, axis=0, keepdims=True)
    run[...] = total
    cnt_ref[...] = total

    milestone(BEFORE_STAGING)
    for j in range(PACK_TILES):
        stage[slot, pl.ds(j, t, stride=SUBLANES), :] = packed[j]
    tok = lax.broadcasted_iota(jnp.int32, (t, LANES), 0) + i * t
    stage[slot, pl.ds(TOK_ROW, t, stride=SUBLANES), :] = tok.astype(jnp.uint32)
    bit = jnp.where(sel, jnp.left_shift(1, lane & 15), 0).astype(F32)
    words = []
    for q in range(N_EXPERTS // 16):
        part = jnp.sum(jnp.where((lane >> 4) == q, bit, 0.0), axis=-1, keepdims=True)
        words.append(part.astype(jnp.int32).astype(jnp.uint32))
    for w, row in enumerate(MASK_ROWS):
        word = words[2 * w] | (words[2 * w + 1] << 16)
        stage[slot, pl.ds(row, t, stride=SUBLANES), :] = jnp.broadcast_to(word, (t, LANES))

    dvm[...] = dest.T[0:TOP_K, :].astype(jnp.int32)
    pltpu.make_async_copy(dvm, dsm.at[slot], sem_s).start()


def _mix(proj, u, ypre, x2, w_z, w_gates, ssd_ng, pool_w, pool_scale, w_br_pool, w_br_ssd, w_out, g1, ng, sh2, sc2,
         g2, w_router, router_bias, ws_gu, ws_down, cap):
    n = x2.shape[0]
    t = 256
    n_tiles = n // t
    xp_blk = (D_SSD + 2 * D_BC) // D_MODEL
    const = lambda shape: pl.BlockSpec(shape, lambda i: (0,) * len(shape), pipeline_mode=pl.Buffered(1))
    tile = lambda i: jnp.minimum(i, n_tiles - 1)
    row = lambda i: (tile(i), 0)
    return pl.pallas_call(
        functools.partial(_mix_kernel, n_tiles=n_tiles, cap=cap),
        grid=(n_tiles + 1,),
        in_specs=[pl.BlockSpec((t, D_MODEL), lambda i: (tile(i), xp_blk)),
                  pl.BlockSpec((t, D_MODEL), row),
                  pl.BlockSpec((t, D_SSD), row),
                  pl.BlockSpec((t, D_MODEL), row),
                  const((D_MODEL, D_SSD)),
                  const((D_MODEL, 2 * D_MODEL)),
                  _full((1, D_SSD)),
                  const((len(POOL_WINDOWS), POOL_GDIM, POOL_GDIM)),
                  _full((1, D_MODEL)),
                  const((D_MODEL, D_MODEL)),
                  const((D_SSD, D_MODEL)),
                  const((D_MODEL, D_MODEL)),
                  _full((1, D_MODEL)), _full((1, D_MODEL)), _full((1, D_MODEL)), _full((1, D_MODEL)),
                  _full((1, D_MODEL)),
                  _full((D_MODEL, N_EXPERTS)),
                  _full((1, N_EXPERTS)),
                  const((D_MODEL, 2 * D_SHARED)),
                  const((D_SHARED, D_MODEL)),
                  pl.BlockSpec(memory_space=pltpu.SMEM)],
        out_specs=[pl.BlockSpec((t, D_MODEL), row),
                   pl.BlockSpec((t, TOP_K), row),
                   _full((1, N_EXPERTS)),
                   pl.BlockSpec(memory_space=pl.ANY)],
        out_shape=[jax.ShapeDtypeStruct((n, D_MODEL), F32),
                   jax.ShapeDtypeStruct((n, TOP_K), F32),
                   jax.ShapeDtypeStruct((1, N_EXPERTS), F32),
                   jax.ShapeDtypeStruct((N_EXPERTS * cap * SUBLANES, LANES), jnp.uint32)],
        scratch_shapes=[pltpu.VMEM((2 * SUBLANES, D_MODEL), F32),
                        pltpu.VMEM((1, N_EXPERTS), F32),
                        pltpu.VMEM((2, t * SUBLANES, LANES), jnp.uint32),
                        pltpu.VMEM((TOP_K, t), jnp.int32),
                        pltpu.SMEM((2, TOP_K, t), jnp.int32),
                        pltpu.SemaphoreType.DMA((2,)),
                        pltpu.SemaphoreType.DMA(())],
        compiler_params=_cparams(),
        name="mix",
    )(proj, u, ypre, x2, w_z, w_gates, ssd_ng, pool_w, pool_scale, w_br_pool, w_br_ssd, w_out, g1, ng, sh2, sc2, g2,
      w_router, router_bias, ws_gu, ws_down, jnp.zeros((1,), jnp.int32))


def _padfill_kernel(cnt_ref, xs_in_ref, xs_ref, padbuf, sem, *, cap):
    del xs_in_ref
    sub = lax.broadcasted_iota(jnp.int32, padbuf.shape, 0) & (SUBLANES - 1)
    padbuf[...] = jnp.where(sub == TOK_ROW, jnp.uint32(PAD_ID), jnp.uint32(0))
    sizes = [1 << bit for bit in range(EXPERT_BLOCK.bit_length() - 1)]

    def pad_copies(e, fn):
        c = cnt_ref[e]
        n_pad = (EXPERT_BLOCK - (c & (EXPERT_BLOCK - 1))) & (EXPERT_BLOCK - 1)
        first = e * cap + c
        for size in sizes:
            @pl.when((n_pad & size) != 0)
            def _():
                start = first + (n_pad & (size - 1))
                dst = xs_ref.at[pl.ds(pl.multiple_of(start * SUBLANES, SUBLANES), size * SUBLANES)]
                fn(pltpu.make_async_copy(padbuf.at[pl.ds(0, size * SUBLANES)], dst, sem))

    def issue(e, carry):
        pad_copies(e, lambda cp: cp.start())
        return carry

    lax.fori_loop(0, N_EXPERTS, issue, 0)

    def drain(e, carry):
        pad_copies(e, lambda cp: cp.wait())
        return carry

    lax.fori_loop(0, N_EXPERTS, drain, 0)


def _padfill(counts, xs, cap):
    return pl.pallas_call(
        functools.partial(_padfill_kernel, cap=cap),
        grid_spec=pltpu.PrefetchScalarGridSpec(
            num_scalar_prefetch=1,
            grid=(1,),
            in_specs=[pl.BlockSpec(memory_space=pl.ANY)],
            out_specs=pl.BlockSpec(memory_space=pl.ANY),
            scratch_shapes=[pltpu.VMEM((EXPERT_BLOCK // 2 * SUBLANES, LANES), jnp.uint32),
                            pltpu.SemaphoreType.DMA(())]),
        out_shape=jax.ShapeDtypeStruct(xs.shape, xs.dtype),
        input_output_aliases={1: 0},
        compiler_params=_cparams(),
        name="padfill",
    )(counts, xs)


def _expert_kernel(be_ref, br_ref, nb_ref, x_ref, wg_ref, wu_ref, wd_ref, ytok_ref, ybuf0, ybuf1, idv, ids0, ids1,
                   wgb, wub, wdb, sem, sem_ids, *, n_ids):
    del br_ref
    m = EXPERT_BLOCK
    b = pl.program_id(0)
    nb = nb_ref[0]
    ybufs, idss = (ybuf0, ybuf1), (ids0, ids1)
    n_phases = m // LANES
    half = D_MODEL // 2

    @pl.when((b == 0) | (be_ref[b] != be_ref[jnp.maximum(b - 1, 0)]))
    def _():
        wgb[...] = wg_ref[0].astype(BF16)
        wub[...] = wu_ref[0].astype(BF16)
        wdb[...] = wd_ref[0].astype(BF16)

    def half_tile(ref, unit):
        return ref.at[pl.ds(pl.multiple_of(unit * PACK_TILES, PACK_TILES), PACK_TILES)]

    def issue(s, a0):
        for col in range(LANES):
            pltpu.make_async_copy(half_tile(ybufs[s], a0 * LANES + col),
                                  half_tile(ytok_ref, idss[s][a0, col]), sem.at[s]).start()

    def step(s, prev):
        tokrep = x_ref[pl.ds(TOK_ROW, m, stride=SUBLANES), :]
        e = be_ref[b]
        below_lo = jnp.where(e >= 32, -1, (1 << jnp.minimum(e, 31)) - 1).astype(jnp.uint32)
        below_hi = jnp.where(e >= 32, (1 << jnp.maximum(e - 32, 0)) - 1, 0).astype(jnp.uint32)
        rank = (lax.population_count(x_ref[pl.ds(MASK_ROWS[0], m, stride=SUBLANES), :] & below_lo)
                + lax.population_count(x_ref[pl.ds(MASK_ROWS[1], m, stride=SUBLANES), :] & below_hi))
        r = lax.broadcasted_iota(jnp.int32, (m, LANES), 0)
        ln = lax.broadcasted_iota(jnp.int32, (m, LANES), 1)
        n_tok = n_ids // TOP_K
        rank = rank.astype(jnp.int32)
        half_rank = TOP_K // 2
        unit = ((rank % half_rank) * n_tok + tokrep.astype(jnp.int32)) * 2 + rank // half_rank
        idi = jnp.where(tokrep == jnp.uint32(PAD_ID), n_ids + r, unit)
        diag = jnp.where((r & (LANES - 1)) == ln, idi, 0).astype(F32)
        idv[...] = jnp.sum(diag.reshape(m // LANES, LANES, LANES), axis=1).astype(jnp.int32)
        ids_copy = pltpu.make_async_copy(idv, idss[s], sem_ids)
        ids_copy.start()

        halves = [_unpack_bf16_pair(x_ref[pl.ds(j, m, stride=SUBLANES), :]) for j in range(PACK_TILES)]
        x = jnp.concatenate([lo for lo, _ in halves] + [hi for _, hi in halves], axis=1).astype(BF16)
        phases = iter(range(n_phases))
        if prev is not None:
            issue(prev, next(phases))
        gate = jnp.dot(x, wgb[...], preferred_element_type=F32)
        if prev is not None:
            issue(prev, next(phases))
        up = jnp.dot(x, wub[...], preferred_element_type=F32)
        hid = (_silu(gate) * up).astype(BF16)

        def drain(which):
            pltpu.make_async_copy(ybufs[which], ytok_ref.at[pl.ds(0, m * PACK_TILES)], sem.at[which]).wait()

        if prev is not None:
            @pl.when(b >= 2)
            def _():
                drain(s)
        ys = []
        for c0 in (0, half):
            if prev is not None:
                issue(prev, next(phases))
            ys.append(jnp.dot(hid, wdb[:, c0:c0 + half], preferred_element_type=F32))
        for j in range(PACK_TILES):
            sl = slice(j * LANES, (j + 1) * LANES)
            ybufs[s][pl.ds(j, m, stride=PACK_TILES), :] = _pack_bf16_pair(ys[0][:, sl], ys[1][:, sl])
        ids_copy.wait()
        if prev is not None:
            for a0 in phases:
                issue(prev, a0)

            @pl.when(b == nb)
            def _():
                drain(prev)

    @pl.when(b == 0)
    def _():
        step(0, None)

    for s in range(2):
        @pl.when((b >= 1) & (b <= nb) & (lax.rem(b, 2) == s))
        def _():
            step(s, 1 - s)


def _experts(block_e, block_row, nb_used, xs, we_gate, we_up, we_down, n_ids):
    n_blocks = block_e.shape[0]
    m = EXPERT_BLOCK
    rows = m * SUBLANES
    return pl.pallas_call(
        functools.partial(_expert_kernel, n_ids=n_ids),
        grid_spec=pltpu.PrefetchScalarGridSpec(
            num_scalar_prefetch=3,
            grid=(n_blocks,),
            in_specs=[pl.BlockSpec((rows, LANES), lambda b, be, br, nb: (br[b], 0)),
                      pl.BlockSpec((1, D_MODEL, D_EXPERT), lambda b, be, br, nb: (be[b], 0, 0)),
                      pl.BlockSpec((1, D_MODEL, D_EXPERT), lambda b, be, br, nb: (be[b], 0, 0)),
                      pl.BlockSpec((1, D_EXPERT, D_MODEL), lambda b, be, br, nb: (be[b], 0, 0))],
            out_specs=pl.BlockSpec(memory_space=pl.ANY),
            scratch_shapes=[pltpu.VMEM((m * PACK_TILES, LANES), jnp.uint32),
                            pltpu.VMEM((m * PACK_TILES, LANES), jnp.uint32),
                            pltpu.VMEM((m // LANES, LANES), jnp.int32),
                            pltpu.SMEM((m // LANES, LANES), jnp.int32),
                            pltpu.SMEM((m // LANES, LANES), jnp.int32),
                            pltpu.VMEM((D_MODEL, D_EXPERT), BF16),
                            pltpu.VMEM((D_MODEL, D_EXPERT), BF16),
                            pltpu.VMEM((D_EXPERT, D_MODEL), BF16),
                            pltpu.SemaphoreType.DMA((2,)),
                            pltpu.SemaphoreType.DMA(())]),
        out_shape=jax.ShapeDtypeStruct(((n_ids + m) * PACK_TILES, LANES), jnp.uint32),
        compiler_params=_cparams(),
        name="experts",
    )(block_e, block_row, nb_used, xs, we_gate, we_up, we_down)


def _combine_kernel(*refs):
    half_rank = TOP_K // 2
    y_refs = refs[:half_rank]
    w8_ref, base_ref, g2_ref, fg_ref, o_ref = refs[half_rank:]
    t7 = base_ref.shape[0]
    w8 = w8_ref[...]
    parts = [jnp.zeros((t7, LANES), F32) for _ in range(ROW_TILES)]
    for p in range(half_rank):
        for h in range(2):
            wk = w8[:, p + h * half_rank:p + h * half_rank + 1]
            for j in range(PACK_TILES):
                lo, hi = _unpack_bf16_pair(y_refs[p][pl.ds(h * PACK_TILES + j, t7, stride=SUBLANES), :])
                parts[j] = parts[j] + wk * lo
                parts[j + PACK_TILES] = parts[j + PACK_TILES] + wk * hi
    routed = jnp.concatenate(parts, axis=1)
    h = base_ref[...] + g2_ref[...] * routed
    inv = lax.rsqrt(jnp.mean(h * h, axis=-1, keepdims=True) + EPS)
    o_ref[...] = h * inv * fg_ref[...]


def _combine(ytok, w8, base, g2, fg):
    n = base.shape[0]
    t7 = 256
    y_specs = [pl.BlockSpec((t7 * SUBLANES, LANES), functools.partial(lambda i, p: (p * (n // t7) + i, 0), p=p))
               for p in range(TOP_K // 2)]
    return pl.pallas_call(
        _combine_kernel,
        grid=(n // t7,),
        in_specs=y_specs + [
                  pl.BlockSpec((t7, TOP_K), lambda i: (i, 0)),
                  pl.BlockSpec((t7, D_MODEL), lambda i: (i, 0)),
                  _full((1, D_MODEL)), _full((1, D_MODEL))],
        out_specs=pl.BlockSpec((t7, D_MODEL), lambda i: (i, 0)),
        out_shape=jax.ShapeDtypeStruct((n, D_MODEL), F32),
        compiler_params=_cparams(),
        name="combine",
    )(*([ytok] * (TOP_K // 2)), w8, base, g2, fg)


def kernel(x, c, w_ada, b_ada, norm_mix_g, w_in, conv_w, conv_b, dt_bias, A_log, D_skip, ssd_norm_g, pool_w,
           pool_scale, w_br_ssd, w_br_pool, w_out, norm_ffn_g, w_router, router_bias, we_gate, we_up, we_down,
           ws_gate, ws_up, ws_down, final_norm_g):
    bsz, seq, _ = x.shape
    assert bsz == 1 and w_ada.shape[0] == 1
    n = seq
    h2 = x.reshape(n, D_MODEL)

    mod = _ada(c, w_ada[0], b_ada[0])
    sh1, sc1, g1, sh2, sc2, g2 = [mod[:, k * D_MODEL:(k + 1) * D_MODEL] for k in range(6)]

    wi = w_in[0]
    o_xbc, o_dt = D_SSD, D_SSD + D_SSD + 2 * D_BC
    o_pool = o_dt + N_HEADS
    o_gates = o_pool + D_MODEL
    w_dt = jnp.pad(wi[:, o_dt:o_pool], ((0, 0), (0, LANES - N_HEADS))).astype(BF16)
    proj, dt_raw, u = _in_proj(h2, norm_mix_g[0][None], sh1, sc1, wi[:, o_xbc:o_dt].astype(BF16),
                               wi[:, o_pool:o_gates].astype(BF16), w_dt)

    pad_h = lambda v: jnp.pad(v[None], ((0, 0), (0, LANES - N_HEADS)))
    ypre = _ssd(proj, dt_raw, conv_w[0], conv_b[0][None], pad_h(dt_bias[0]), pad_h(A_log[0]),
                jnp.repeat(D_skip[0], HEADDIM)[None])

    ws_gu = jnp.concatenate([ws_gate[0], ws_up[0]], axis=1).astype(BF16)
    m = EXPERT_BLOCK
    cap = -(-n // m) * m
    base, w8, cnt, xs = _mix(
        proj, u, ypre, h2, wi[:, :o_xbc].astype(BF16), wi[:, o_gates:].astype(BF16), ssd_norm_g[0][None],
        pool_w[0].astype(BF16), pool_scale[0][None], w_br_pool[0].astype(BF16),
        w_br_ssd[0].astype(BF16), w_out[0].astype(BF16), g1, norm_ffn_g[0][None], sh2, sc2, g2,
        w_router[0].astype(BF16), router_bias[0][None], ws_gu, ws_down[0].astype(BF16), cap)

    counts = cnt[0].astype(jnp.int32)
    nblk = (counts + m - 1) // m
    bends = jnp.cumsum(nblk)
    bstarts = bends - nblk
    nb_used = bends[-1]
    n_blocks = -(-(n * TOP_K) // m) + N_EXPERTS + 1
    b_eff = jnp.minimum(jnp.arange(n_blocks, dtype=jnp.int32), nb_used - 1)
    done = bends[None, :] <= b_eff[:, None]
    block_e = jnp.sum(done.astype(jnp.int32), axis=1)
    block_row = block_e * (cap // m) + b_eff - jnp.sum(jnp.where(done, nblk[None, :], 0), axis=1)

    xs = _padfill(counts, xs, cap)
    ytok = _experts(block_e, block_row, nb_used.reshape(1), xs,
                    we_gate[0], we_up[0], we_down[0], n * TOP_K)
    out = _combine(ytok, w8, base, g2, final_norm_g[None])
    return out.reshape(bsz, seq, D_MODEL)
```

```python
import functools

import jax
import jax.numpy as jnp
from jax import lax
from jax.experimental import pallas as pl
from jax.experimental.pallas import tpu as pltpu

F32 = jnp.float32
BF16 = jnp.bfloat16
HIGHEST = lax.Precision.HIGHEST

D_MODEL = 1024
D_SSD = 2048
HEADDIM = 64
N_HEADS = 32
N_GROUPS = 8
HEADS_PER_GROUP = N_HEADS // N_GROUPS
D_STATE = 128
CONV_K = 4
CHUNK = 128
SSD_CHUNKS_PER_STEP = 2
GROUP_W = D_SSD // N_GROUPS
D_BC = N_GROUPS * D_STATE
POOL_WINDOWS = (2, 4, 8, 16)
POOL_GDIM = 256
N_EXPERTS = 64
TOP_K = 8
N_EGROUPS = 8
EXPERTS_PER_GROUP = 8
TOPK_GROUPS = 4
D_EXPERT = 256
D_SHARED = 256
ROUTED_SCALE = 2.5
EPS = 1e-6

LANES = 128
SUBLANES = 8
ROW_TILES = D_MODEL // LANES
PACK_TILES = ROW_TILES // 2
PACK_W = PACK_TILES * LANES
EXPERT_BLOCK = 512
MIX_ISSUE_GROUPS = 32
BEFORE_STAGING = "before_staging"

PROJ_W = D_SSD + 2 * D_BC + D_MODEL
PROJ_CHUNK = 512

VMEM_LIMIT = 56 * 1024 * 1024


def _cparams(sem=("arbitrary",)):
    return pltpu.CompilerParams(dimension_semantics=sem, vmem_limit_bytes=VMEM_LIMIT)


def _full(shape):
    nd = len(shape)
    return pl.BlockSpec(shape, lambda *_: (0,) * nd)


def _silu(v):
    return v * jax.nn.sigmoid(v)


def _pack_bf16_pair(lo, hi):
    lo_bits = lax.bitcast_convert_type(lo.astype(BF16).astype(F32), jnp.uint32)
    hi_bits = lax.bitcast_convert_type(hi.astype(BF16).astype(F32), jnp.uint32)
    return (lo_bits >> 16) | (hi_bits & jnp.uint32(0xFFFF0000))


def _unpack_bf16_pair(words):
    return (lax.bitcast_convert_type(words << 16, F32),
            lax.bitcast_convert_type(words & jnp.uint32(0xFFFF0000), F32))


def _ada_kernel(c_ref, w_ref, b_ref, o_ref):
    c = c_ref[...]
    o_ref[...] = jnp.dot(_silu(c), w_ref[...], preferred_element_type=F32, precision=HIGHEST) + b_ref[...]


def _ada(c, w_ada, b_ada):
    n_out = w_ada.shape[1]
    tn = 1536
    c8 = jnp.broadcast_to(c, (SUBLANES, D_MODEL))
    out = pl.pallas_call(
        _ada_kernel,
        grid=(n_out // tn,),
        in_specs=[_full((SUBLANES, D_MODEL)),
                  pl.BlockSpec((D_MODEL, tn), lambda j: (0, j)),
                  pl.BlockSpec((1, tn), lambda j: (0, j))],
        out_specs=pl.BlockSpec((SUBLANES, tn), lambda j: (0, j)),
        out_shape=jax.ShapeDtypeStruct((SUBLANES, n_out), F32),
        compiler_params=_cparams(),
        name="ada",
    )(c8, w_ada, b_ada.reshape(1, n_out))
    return out[0:1]


def _inproj_kernel(x_ref, g_ref, sh_ref, sc_ref, wa_ref, wb_ref, wdt_ref, proj_ref, dt_ref, u_ref):
    x = x_ref[...]
    inv = lax.rsqrt(jnp.mean(x * x, axis=-1, keepdims=True) + EPS)
    u = x * inv * g_ref[...]
    u = u * (1.0 + sc_ref[...]) + sh_ref[...]
    ub = u.astype(BF16)
    u_ref[...] = ub
    wa = wa_ref.shape[1]
    for c0 in range(0, PROJ_W, PROJ_CHUNK):
        w = wa_ref[:, c0:c0 + PROJ_CHUNK] if c0 < wa else wb_ref[:, c0 - wa:c0 - wa + PROJ_CHUNK]
        proj_ref[:, c0:c0 + PROJ_CHUNK] = jnp.dot(ub, w, preferred_element_type=F32).astype(BF16)
    dt_ref[...] = jnp.dot(ub, wdt_ref[...], preferred_element_type=F32)


def _in_proj(x2, g, sh, sc, w_a, w_b, w_dt):
    n = x2.shape[0]
    tm = 512
    return pl.pallas_call(
        _inproj_kernel,
        grid=(n // tm,),
        in_specs=[pl.BlockSpec((tm, D_MODEL), lambda i: (i, 0)),
                  _full((1, D_MODEL)), _full((1, D_MODEL)), _full((1, D_MODEL)),
                  pl.BlockSpec(w_a.shape, lambda i: (0, 0), pipeline_mode=pl.Buffered(1)),
                  pl.BlockSpec(w_b.shape, lambda i: (0, 0), pipeline_mode=pl.Buffered(1)),
                  _full((D_MODEL, LANES))],
        out_specs=[pl.BlockSpec((tm, PROJ_W), lambda i: (i, 0)),
                   pl.BlockSpec((tm, LANES), lambda i: (i, 0)),
                   pl.BlockSpec((tm, D_MODEL), lambda i: (i, 0))],
        out_shape=[jax.ShapeDtypeStruct((n, PROJ_W), BF16),
                   jax.ShapeDtypeStruct((n, LANES), F32),
                   jax.ShapeDtypeStruct((n, D_MODEL), BF16)],
        compiler_params=_cparams(),
        name="in_proj",
    )(x2, g, sh, sc, w_a, w_b, w_dt)


def _conv_silu(cur_ref, ext_ref, w_ref, b_ref, out_ref, width, cw=512):
    t = cur_ref.shape[0]
    for c0 in range(0, width, cw):
        sl = slice(c0, c0 + cw)
        cur = cur_ref[:, sl].astype(F32)
        ext_ref[SUBLANES:, sl] = cur
        acc = cur * w_ref[CONV_K - 1:CONV_K, sl] + b_ref[:, sl]
        for s in range(1, CONV_K):
            acc = acc + ext_ref[pl.ds(SUBLANES - s, t), sl] * w_ref[CONV_K - 1 - s:CONV_K - s, sl]
        out_ref[:, sl] = _silu(acc).astype(out_ref.dtype)
        ext_ref[0:SUBLANES, sl] = cur[t - SUBLANES:t]


def _ssd_kernel(xs_ref, bc_ref, dt_ref, cwx_ref, cwbc_ref, cbx_ref, cbbc_ref, dtb_ref, alog_ref,
                dskip_ref, o_ref, tailx, tailbc, state, xc, bcc):
    q = CHUNK
    pair_w = 2 * HEADDIM

    @pl.when(pl.program_id(0) == 0)
    def _():
        tailx[...] = jnp.zeros_like(tailx)
        tailbc[...] = jnp.zeros_like(tailbc)
        state[...] = jnp.zeros_like(state)

    _conv_silu(xs_ref, tailx, cwx_ref, cbx_ref, xc, D_SSD)
    _conv_silu(bc_ref, tailbc, cwbc_ref, cbbc_ref, bcc, 2 * D_BC)

    for c in range(xs_ref.shape[0] // q):
        rows = slice(c * q, (c + 1) * q)
        v = dt_ref[rows, :] + dtb_ref[...]
        dt = jnp.maximum(v, 0.0) + jnp.log(1.0 + jnp.exp(-jnp.abs(v)))
        a = dt * (-jnp.exp(alog_ref[...]))
        ri = lax.broadcasted_iota(jnp.int32, (q, q), 0)
        ci = lax.broadcasted_iota(jnp.int32, (q, q), 1)
        causal = ri >= ci
        a_cs = jnp.dot(causal.astype(F32), a, preferred_element_type=F32, precision=HIGHEST)
        a_cs_t = a_cs.T
        dt_t = dt.T
        first_half = ci < HEADDIM
        first_half_row = first_half[0:1, :]

        for g in range(N_GROUPS):
            b_g = bcc[rows, g * D_STATE:(g + 1) * D_STATE]
            c_g = bcc[rows, D_BC + g * D_STATE:D_BC + (g + 1) * D_STATE]
            cb = lax.dot_general(c_g, b_g, (((1,), (1,)), ((), ())), preferred_element_type=F32)
            b_t = b_g.astype(F32).T
            st_g = state[g]
            y_off = jnp.dot(c_g, st_g.astype(BF16), preferred_element_type=F32)
            for pi in range(HEADS_PER_GROUP // 2):
                h0 = g * HEADS_PER_GROUP + 2 * pi
                lanes = slice(h0 * HEADDIM, h0 * HEADDIM + pair_w)
                gl = slice(pi * pair_w, (pi + 1) * pair_w)
                ms, ws, cols, lasts = [], [], [], []
                for h in (h0, h0 + 1):
                    col = jnp.broadcast_to(a_cs[:, h:h + 1], (q, q))
                    row = a_cs_t[h:h + 1, :]
                    dtrow = dt_t[h:h + 1, :]
                    decay = jnp.exp(jnp.where(causal, col - row, -jnp.inf))
                    ms.append((cb * decay * dtrow).astype(BF16))
                    last = a_cs_t[h:h + 1, q - 1:q]
                    ws.append((b_t * (jnp.exp(last - row) * dtrow)).astype(BF16))
                    cols.append(col)
                    lasts.append(last)
                xp = xc[rows, lanes]
                zero = jnp.zeros_like(xp)
                x_bd = jnp.concatenate([jnp.where(first_half, xp, zero), jnp.where(first_half, zero, xp)], axis=0)
                y_p = jnp.dot(jnp.concatenate(ms, axis=1), x_bd, preferred_element_type=F32)
                y_p = y_p + y_off[:, gl] * jnp.exp(jnp.where(first_half, cols[0], cols[1]))
                o_ref[rows, lanes] = (y_p + dskip_ref[:, lanes] * xp.astype(F32)).astype(BF16)
                s_new = jnp.dot(jnp.concatenate(ws, axis=1), x_bd, preferred_element_type=F32)
                carry = jnp.exp(jnp.where(first_half_row, lasts[0], lasts[1]))
                state[g, :, gl] = st_g[:, gl] * carry + s_new


def _ssd(proj, dt_raw, conv_w, conv_b, dt_bias, a_log, d_skip):
    n = proj.shape[0]
    t = SSD_CHUNKS_PER_STEP * CHUNK
    return pl.pallas_call(
        _ssd_kernel,
        grid=(n // t,),
        in_specs=[pl.BlockSpec((t, D_SSD), lambda i: (i, 0)),
                  pl.BlockSpec((t, 2 * D_BC), lambda i: (i, 1)),
                  pl.BlockSpec((t, LANES), lambda i: (i, 0)),
                  pl.BlockSpec((CONV_K, D_SSD), lambda i: (0, 0)),
                  pl.BlockSpec((CONV_K, 2 * D_BC), lambda i: (0, 1)),
                  pl.BlockSpec((1, D_SSD), lambda i: (0, 0)),
                  pl.BlockSpec((1, 2 * D_BC), lambda i: (0, 1)),
                  _full((1, LANES)), _full((1, LANES)),
                  _full((1, D_SSD))],
        out_specs=pl.BlockSpec((t, D_SSD), lambda i: (i, 0)),
        out_shape=jax.ShapeDtypeStruct((n, D_SSD), BF16),
        scratch_shapes=[pltpu.VMEM((SUBLANES + t, D_SSD), F32),
                        pltpu.VMEM((SUBLANES + t, 2 * D_BC), F32),
                        pltpu.VMEM((N_GROUPS, D_STATE, GROUP_W), F32),
                        pltpu.VMEM((t, D_SSD), BF16),
                        pltpu.VMEM((t, 2 * D_BC), BF16)],
        compiler_params=_cparams(),
        name="ssd",
    )(proj, proj, dt_raw, conv_w, conv_w, conv_b, conv_b, dt_bias, a_log, d_skip)


PAD_ID = 0xFFFFFFFF
TOK_ROW = PACK_TILES
MASK_ROWS = (PACK_TILES + 1, PACK_TILES + 2)


def _tile_rows(ref, first_row):
    return ref.at[pl.ds(pl.multiple_of(first_row * SUBLANES, SUBLANES), SUBLANES)]


def _mix_kernel(xp_ref, u_ref, ypre_ref, x_ref, wz_ref, wgt_ref, sng_ref, poolw_ref, pscale_ref, wbp_ref, wbs_ref, wout_ref,
                g1_ref, ng_ref, sh2_ref, sc2_ref, g2_ref, wr_ref, rb_ref, wsgu_ref, wsd_ref, zero_ref,
                base_ref, ws_ref, cnt_ref, xs_ref, ptail, run, stage, dvm, dsm, sem, sem_s, *, n_tiles, cap):
    t = x_ref.shape[0]
    i = pl.program_id(0)
    slot = lax.rem(i, 2)

    @pl.when(i == 0)
    def _():
        ptail[...] = jnp.zeros_like(ptail)
        run[...] = jnp.zeros_like(run)
        stage[...] = jnp.zeros_like(stage)

    def issue_rows(s, group, after=None):
        zero = 0
        if after is not None:
            probe = jnp.max(jnp.abs(after[0:SUBLANES, 0:min(LANES, after.shape[1])])).astype(jnp.int32)
            zero = probe * zero_ref[0]
        per = t // MIX_ISSUE_GROUPS
        for tt in range(group * per, (group + 1) * per):
            for k in range(TOP_K):
                pltpu.make_async_copy(stage.at[s, pl.ds(tt * SUBLANES, SUBLANES)],
                                      _tile_rows(xs_ref, dsm[s, k, tt] + zero), sem.at[s]).start()

    def drain_rows(s):
        for _ in range(TOP_K):
            pltpu.make_async_copy(stage.at[s], xs_ref.at[pl.ds(0, t * SUBLANES)], sem.at[s]).wait()

    def tile(s, milestone=None):
        _mix_tile_body(xp_ref, u_ref, ypre_ref, x_ref, wz_ref, wgt_ref, sng_ref, poolw_ref, pscale_ref, wbp_ref, wbs_ref, wout_ref,
                       g1_ref, ng_ref, sh2_ref, sc2_ref, g2_ref, wr_ref, rb_ref, wsgu_ref, wsd_ref,
                       base_ref, ws_ref, cnt_ref, ptail, run, stage, dvm, dsm, sem_s, i, s, cap, milestone)

    @pl.when(i == 0)
    def _():
        tile(0)

    groups = iter(range(1, MIX_ISSUE_GROUPS))

    def milestone(value):
        if value is BEFORE_STAGING:
            assert next(groups, None) is None
            @pl.when(i >= 2)
            def _():
                drain_rows(slot)
        else:
            issue_rows(1 - slot, next(groups), value)

    def wait_slot_rows(s):
        pltpu.make_async_copy(dvm, dsm.at[s], sem_s).wait()

    @pl.when((i >= 1) & (i < n_tiles))
    def _():
        wait_slot_rows(1 - slot)
        issue_rows(1 - slot, 0)
        tile(slot, milestone)

    @pl.when(i == n_tiles)
    def _():
        wait_slot_rows(1 - slot)
        for group in range(MIX_ISSUE_GROUPS):
            issue_rows(1 - slot, group)
        if n_tiles >= 2:
            drain_rows(slot)
        drain_rows(1 - slot)


def _mix_tile_body(xp_ref, u_ref, ypre_ref, x_ref, wz_ref, wgt_ref, sng_ref, poolw_ref, pscale_ref, wbp_ref, wbs_ref, wout_ref,
                   g1_ref, ng_ref, sh2_ref, sc2_ref, g2_ref, wr_ref, rb_ref, wsgu_ref, wsd_ref,
                   base_ref, ws_ref, cnt_ref, ptail, run, stage, dvm, dsm, sem_s, i, slot, cap, milestone):
    t = x_ref.shape[0]
    if milestone is None:
        milestone = lambda value: None

    xp = xp_ref[...].astype(F32)
    ext = jnp.concatenate([ptail[...], xp], axis=0)
    ptail[...] = xp[t - 2 * SUBLANES:t]
    pos = (lax.broadcasted_iota(jnp.int32, (t, 1), 0) + (i * t + 1)).astype(F32)
    pooled = []
    for gi, w in enumerate(POOL_WINDOWS):
        sl = slice(gi * POOL_GDIM, (gi + 1) * POOL_GDIM)
        e = ext[:, sl]
        s = e
        span = 1
        while span < w:
            s = s + pltpu.roll(s, span, axis=0)
            span *= 2
        win = s[2 * SUBLANES:]
        mean = win / jnp.minimum(pos, float(w))
        pg = (mean - xp[:, sl]).astype(BF16)
        mixed = jnp.dot(pg, poolw_ref[gi], preferred_element_type=F32)
        pooled.append(mixed * pscale_ref[:, sl])
        milestone(pooled[-1])
    pooled = jnp.concatenate(pooled, axis=1).astype(BF16)
    y_pool = jnp.dot(pooled, wbp_ref[...], preferred_element_type=F32)
    milestone(y_pool)
    ub = u_ref[...]
    z = jnp.dot(ub, wz_ref[...], preferred_element_type=F32)
    milestone(z)
    yn = []
    for g in range(N_GROUPS):
        sl = slice(g * GROUP_W, (g + 1) * GROUP_W)
        yg = ypre_ref[:, sl].astype(F32) * _silu(z[:, sl])
        inv_g = lax.rsqrt(jnp.mean(yg * yg, axis=-1, keepdims=True) + EPS)
        yn.append((yg * inv_g * sng_ref[:, sl]).astype(BF16))
        milestone(inv_g)
    y_ssd = jnp.dot(jnp.concatenate(yn, axis=1), wbs_ref[...], preferred_element_type=F32)
    milestone(y_ssd)
    gates = jnp.dot(ub, wgt_ref[...], preferred_element_type=F32)
    milestone(gates)
    g_ssd = jax.nn.sigmoid(gates[:, :D_MODEL])
    g_pool = jax.nn.sigmoid(gates[:, D_MODEL:])
    mixed = (g_ssd * y_ssd + g_pool * y_pool).astype(BF16)
    h = x_ref[...] + g1_ref[...] * jnp.dot(mixed, wout_ref[...], preferred_element_type=F32)
    milestone(h)

    inv = lax.rsqrt(jnp.mean(h * h, axis=-1, keepdims=True) + EPS)
    milestone(inv)
    u2 = h * inv * ng_ref[...]
    u2 = u2 * (1.0 + sc2_ref[...]) + sh2_ref[...]
    u2b = u2.astype(BF16)
    packed = [_pack_bf16_pair(u2[:, j * LANES:(j + 1) * LANES],
                              u2[:, (j + PACK_TILES) * LANES:(j + PACK_TILES + 1) * LANES])
              for j in range(PACK_TILES)]

    hs = jnp.dot(u2b, wsgu_ref[...], preferred_element_type=F32)
    milestone(hs)
    act = (_silu(hs[:, :D_SHARED]) * hs[:, D_SHARED:]).astype(BF16)
    shared = jnp.dot(act, wsd_ref[...], preferred_element_type=F32)
    milestone(shared)
    base_ref[...] = h + g2_ref[...] * shared

    logits = jnp.dot(u2b, wr_ref[...], preferred_element_type=F32)
    scores = jax.nn.sigmoid(logits)
    milestone(scores)
    choice = scores + rb_ref[...]
    lane = lax.broadcasted_iota(jnp.int32, (t, N_EXPERTS), 1)
    lane_f = lane.astype(F32)
    lane_grp = lane // EXPERTS_PER_GROUP
    neg = -jnp.inf
    gscore = []
    for g in range(N_EGROUPS):
        vg = jnp.where(lane_grp == g, choice, neg)
        m1 = jnp.max(vg, axis=-1, keepdims=True)
        i1 = jnp.min(jnp.where(vg == m1, lane_f, float(N_EXPERTS)), axis=-1, keepdims=True)
        m2 = jnp.max(jnp.where(lane_f == i1, neg, vg), axis=-1, keepdims=True)
        gscore.append(m1 + m2)
        milestone(gscore[-1])
    gmask = jnp.zeros((t, N_EXPERTS), jnp.bool_)
    for g in range(N_EGROUPS):
        rank = jnp.zeros((t, 1), F32)
        for g2 in range(N_EGROUPS):
            if g2 == g:
                continue
            better = (gscore[g2] > gscore[g]) | ((gscore[g2] == gscore[g]) & (g2 < g))
            rank = rank + better.astype(F32)
        gmask = gmask | ((rank < float(TOPK_GROUPS)) & (lane_grp == g))
    work = jnp.where(gmask, choice, neg)
    sel = jnp.zeros((t, N_EXPERTS), jnp.bool_)
    onehots, idxs, sks = [], [], []
    for k in range(TOP_K):
        m = jnp.max(work, axis=-1, keepdims=True)
        idx = jnp.min(jnp.where(work == m, lane_f, float(N_EXPERTS)), axis=-1, keepdims=True)
        oh = lane_f == idx
        onehots.append(oh)
        idxs.append(idx)
        sks.append(jnp.sum(jnp.where(oh, scores, 0.0), axis=-1, keepdims=True))
        if k == TOP_K // 2 - 1:
            milestone(sks[-1])
        sel = sel | oh
        work = jnp.where(oh, neg, work)
    denom = sks[0]
    for k in range(1, TOP_K):
        denom = denom + sks[k]
    milestone(denom)

    ri = lax.broadcasted_iota(jnp.int32, (t, t), 0)
    ci = lax.broadcasted_iota(jnp.int32, (t, t), 1)
    before = (ri > ci).astype(BF16)
    sel_f = jnp.where(sel, 1.0, 0.0)
    sel_b = sel_f.astype(BF16)
    pos_tile = jnp.dot(before, sel_b, preferred_element_type=F32) + run[...]
    er = lax.broadcasted_iota(jnp.int32, (N_EXPERTS, N_EXPERTS), 0)
    ec = lax.broadcasted_iota(jnp.int32, (N_EXPERTS, N_EXPERTS), 1)
    rank_tile = jnp.dot(sel_b, (er < ec).astype(BF16), preferred_element_type=F32)
    lane128 = lax.broadcasted_iota(jnp.int32, (t, LANES), 1)
    k_iota = lax.broadcasted_iota(jnp.int32, (t, TOP_K), 1).astype(F32)
    dest = jnp.zeros((t, LANES), F32)
    w_sorted = jnp.zeros((t, TOP_K), F32)
    for k in range(TOP_K):
        pk = jnp.sum(jnp.where(onehots[k], pos_tile, 0.0), axis=-1, keepdims=True)
        rk = jnp.sum(jnp.where(onehots[k], rank_tile, 0.0), axis=-1, keepdims=True)
        dest = jnp.where(lane128 == k, idxs[k] * float(cap) + pk, dest)
        w_sorted = jnp.where(k_iota == rk, sks[k] / denom * ROUTED_SCALE, w_sorted)
    ws_ref[...] = w_sorted
    total = run[...] + jnp.sum(sel_f, axis=0, keepdims=True)
    run[...] = total
    cnt_ref[...] = total

    milestone(BEFORE_STAGING)
    for j in range(PACK_TILES):
        stage[slot, pl.ds(j, t, stride=SUBLANES), :] = packed[j]
    tok = lax.broadcasted_iota(jnp.int32, (t, LANES), 0) + i * t
    stage[slot, pl.ds(TOK_ROW, t, stride=SUBLANES), :] = tok.astype(jnp.uint32)
    bit = jnp.where(sel, jnp.left_shift(1, lane & 15), 0).astype(F32)
    words = []
    for q in range(N_EXPERTS // 16):
        part = jnp.sum(jnp.where((lane >> 4) == q, bit, 0.0), axis=-1, keepdims=True)
        words.append(part.astype(jnp.int32).astype(jnp.uint32))
    for w, row in enumerate(MASK_ROWS):
        word = words[2 * w] | (words[2 * w + 1] << 16)
        stage[slot, pl.ds(row, t, stride=SUBLANES), :] = jnp.broadcast_to(word, (t, LANES))

    dvm[...] = dest.T[0:TOP_K, :].astype(jnp.int32)
    pltpu.make_async_copy(dvm, dsm.at[slot], sem_s).start()


def _mix(proj, u, ypre, x2, w_z, w_gates, ssd_ng, pool_w, pool_scale, w_br_pool, w_br_ssd, w_out, g1, ng, sh2, sc2,
         g2, w_router, router_bias, ws_gu, ws_down, cap):
    n = x2.shape[0]
    t = 256
    n_tiles = n // t
    xp_blk = (D_SSD + 2 * D_BC) // D_MODEL
    const = lambda shape: pl.BlockSpec(shape, lambda i: (0,) * len(shape), pipeline_mode=pl.Buffered(1))
    tile = lambda i: jnp.minimum(i, n_tiles - 1)
    row = lambda i: (tile(i), 0)
    return pl.pallas_call(
        functools.partial(_mix_kernel, n_tiles=n_tiles, cap=cap),
        grid=(n_tiles + 1,),
        in_specs=[pl.BlockSpec((t, D_MODEL), lambda i: (tile(i), xp_blk)),
                  pl.BlockSpec((t, D_MODEL), row),
                  pl.BlockSpec((t, D_SSD), row),
                  pl.BlockSpec((t, D_MODEL), row),
                  const((D_MODEL, D_SSD)),
                  const((D_MODEL, 2 * D_MODEL)),
                  _full((1, D_SSD)),
                  const((len(POOL_WINDOWS), POOL_GDIM, POOL_GDIM)),
                  _full((1, D_MODEL)),
                  const((D_MODEL, D_MODEL)),
                  const((D_SSD, D_MODEL)),
                  const((D_MODEL, D_MODEL)),
                  _full((1, D_MODEL)), _full((1, D_MODEL)), _full((1, D_MODEL)), _full((1, D_MODEL)),
                  _full((1, D_MODEL)),
                  _full((D_MODEL, N_EXPERTS)),
                  _full((1, N_EXPERTS)),
                  const((D_MODEL, 2 * D_SHARED)),
                  const((D_SHARED, D_MODEL)),
                  pl.BlockSpec(memory_space=pltpu.SMEM)],
        out_specs=[pl.BlockSpec((t, D_MODEL), row),
                   pl.BlockSpec((t, TOP_K), row),
                   _full((1, N_EXPERTS)),
                   pl.BlockSpec(memory_space=pl.ANY)],
        out_shape=[jax.ShapeDtypeStruct((n, D_MODEL), F32),
                   jax.ShapeDtypeStruct((n, TOP_K), F32),
                   jax.ShapeDtypeStruct((1, N_EXPERTS), F32),
                   jax.ShapeDtypeStruct((N_EXPERTS * cap * SUBLANES, LANES), jnp.uint32)],
        scratch_shapes=[pltpu.VMEM((2 * SUBLANES, D_MODEL), F32),
                        pltpu.VMEM((1, N_EXPERTS), F32),
                        pltpu.VMEM((2, t * SUBLANES, LANES), jnp.uint32),
                        pltpu.VMEM((TOP_K, t), jnp.int32),
                        pltpu.SMEM((2, TOP_K, t), jnp.int32),
                        pltpu.SemaphoreType.DMA((2,)),
                        pltpu.SemaphoreType.DMA(())],
        compiler_params=_cparams(),
        name="mix",
    )(proj, u, ypre, x2, w_z, w_gates, ssd_ng, pool_w, pool_scale, w_br_pool, w_br_ssd, w_out, g1, ng, sh2, sc2, g2,
      w_router, router_bias, ws_gu, ws_down, jnp.zeros((1,), jnp.int32))


def _padfill_kernel(cnt_ref, xs_in_ref, xs_ref, padbuf, sem, *, cap):
    del xs_in_ref
    sub = lax.broadcasted_iota(jnp.int32, padbuf.shape, 0) & (SUBLANES - 1)
    padbuf[...] = jnp.where(sub == TOK_ROW, jnp.uint32(PAD_ID), jnp.uint32(0))
    sizes = [1 << bit for bit in range(EXPERT_BLOCK.bit_length() - 1)]

    def pad_copies(e, fn):
        c = cnt_ref[e]
        n_pad = (EXPERT_BLOCK - (c & (EXPERT_BLOCK - 1))) & (EXPERT_BLOCK - 1)
        first = e * cap + c
        for size in sizes:
            @pl.when((n_pad & size) != 0)
            def _():
                start = first + (n_pad & (size - 1))
                dst = xs_ref.at[pl.ds(pl.multiple_of(start * SUBLANES, SUBLANES), size * SUBLANES)]
                fn(pltpu.make_async_copy(padbuf.at[pl.ds(0, size * SUBLANES)], dst, sem))

    def issue(e, carry):
        pad_copies(e, lambda cp: cp.start())
        return carry

    lax.fori_loop(0, N_EXPERTS, issue, 0)

    def drain(e, carry):
        pad_copies(e, lambda cp: cp.wait())
        return carry

    lax.fori_loop(0, N_EXPERTS, drain, 0)


def _padfill(counts, xs, cap):
    return pl.pallas_call(
        functools.partial(_padfill_kernel, cap=cap),
        grid_spec=pltpu.PrefetchScalarGridSpec(
            num_scalar_prefetch=1,
            grid=(1,),
            in_specs=[pl.BlockSpec(memory_space=pl.ANY)],
            out_specs=pl.BlockSpec(memory_space=pl.ANY),
            scratch_shapes=[pltpu.VMEM((EXPERT_BLOCK // 2 * SUBLANES, LANES), jnp.uint32),
                            pltpu.SemaphoreType.DMA(())]),
        out_shape=jax.ShapeDtypeStruct(xs.shape, xs.dtype),
        input_output_aliases={1: 0},
        compiler_params=_cparams(),
        name="padfill",
    )(counts, xs)


def _expert_kernel(be_ref, br_ref, nb_ref, x_ref, wg_ref, wu_ref, wd_ref, ytok_ref, ybuf0, ybuf1, idv, ids0, ids1,
                   wgb, wub, wdb, sem, sem_ids, *, n_ids):
    del br_ref
    m = EXPERT_BLOCK
    b = pl.program_id(0)
    nb = nb_ref[0]
    ybufs, idss = (ybuf0, ybuf1), (ids0, ids1)
    n_phases = m // LANES
    half = D_MODEL // 2

    @pl.when((b == 0) | (be_ref[b] != be_ref[jnp.maximum(b - 1, 0)]))
    def _():
        wgb[...] = wg_ref[0].astype(BF16)
        wub[...] = wu_ref[0].astype(BF16)
        wdb[...] = wd_ref[0].astype(BF16)

    def half_tile(ref, unit):
        return ref.at[pl.ds(pl.multiple_of(unit * PACK_TILES, PACK_TILES), PACK_TILES)]

    def issue(s, a0):
        for col in range(LANES):
            pltpu.make_async_copy(half_tile(ybufs[s], a0 * LANES + col),
                                  half_tile(ytok_ref, idss[s][a0, col]), sem.at[s]).start()

    def step(s, prev):
        tokrep = x_ref[pl.ds(TOK_ROW, m, stride=SUBLANES), :]
        e = be_ref[b]
        below_lo = jnp.where(e >= 32, -1, (1 << jnp.minimum(e, 31)) - 1).astype(jnp.uint32)
        below_hi = jnp.where(e >= 32, (1 << jnp.maximum(e - 32, 0)) - 1, 0).astype(jnp.uint32)
        rank = (lax.population_count(x_ref[pl.ds(MASK_ROWS[0], m, stride=SUBLANES), :] & below_lo)
                + lax.population_count(x_ref[pl.ds(MASK_ROWS[1], m, stride=SUBLANES), :] & below_hi))
        r = lax.broadcasted_iota(jnp.int32, (m, LANES), 0)
        ln = lax.broadcasted_iota(jnp.int32, (m, LANES), 1)
        n_tok = n_ids // TOP_K
        rank = rank.astype(jnp.int32)
        half_rank = TOP_K // 2
        unit = ((rank % half_rank) * n_tok + tokrep.astype(jnp.int32)) * 2 + rank // half_rank
        idi = jnp.where(tokrep == jnp.uint32(PAD_ID), n_ids + r, unit)
        diag = jnp.where((r & (LANES - 1)) == ln, idi, 0).astype(F32)
        idv[...] = jnp.sum(diag.reshape(m // LANES, LANES, LANES), axis=1).astype(jnp.int32)
        ids_copy = pltpu.make_async_copy(idv, idss[s], sem_ids)
        ids_copy.start()

        halves = [_unpack_bf16_pair(x_ref[pl.ds(j, m, stride=SUBLANES), :]) for j in range(PACK_TILES)]
        x = jnp.concatenate([lo for lo, _ in halves] + [hi for _, hi in halves], axis=1).astype(BF16)
        phases = iter(range(n_phases))
        if prev is not None:
            issue(prev, next(phases))
        gate = jnp.dot(x, wgb[...], preferred_element_type=F32)
        if prev is not None:
            issue(prev, next(phases))
        up = jnp.dot(x, wub[...], preferred_element_type=F32)
        hid = (_silu(gate) * up).astype(BF16)

        def drain(which):
            pltpu.make_async_copy(ybufs[which], ytok_ref.at[pl.ds(0, m * PACK_TILES)], sem.at[which]).wait()

        if prev is not None:
            @pl.when(b >= 2)
            def _():
                drain(s)
        ys = []
        for c0 in (0, half):
            if prev is not None:
                issue(prev, next(phases))
            ys.append(jnp.dot(hid, wdb[:, c0:c0 + half], preferred_element_type=F32))
        for j in range(PACK_TILES):
            sl = slice(j * LANES, (j + 1) * LANES)
            ybufs[s][pl.ds(j, m, stride=PACK_TILES), :] = _pack_bf16_pair(ys[0][:, sl], ys[1][:, sl])
        ids_copy.wait()
        if prev is not None:
            for a0 in phases:
                issue(prev, a0)

            @pl.when(b == nb)
            def _():
                drain(prev)

    @pl.when(b == 0)
    def _():
        step(0, None)

    for s in range(2):
        @pl.when((b >= 1) & (b <= nb) & (lax.rem(b, 2) == s))
        def _():
            step(s, 1 - s)


def _experts(block_e, block_row, nb_used, xs, we_gate, we_up, we_down, n_ids):
    n_blocks = block_e.shape[0]
    m = EXPERT_BLOCK
    rows = m * SUBLANES
    return pl.pallas_call(
        functools.partial(_expert_kernel, n_ids=n_ids),
        grid_spec=pltpu.PrefetchScalarGridSpec(
            num_scalar_prefetch=3,
            grid=(n_blocks,),
            in_specs=[pl.BlockSpec((rows, LANES), lambda b, be, br, nb: (br[b], 0)),
                      pl.BlockSpec((1, D_MODEL, D_EXPERT), lambda b, be, br, nb: (be[b], 0, 0)),
                      pl.BlockSpec((1, D_MODEL, D_EXPERT), lambda b, be, br, nb: (be[b], 0, 0)),
                      pl.BlockSpec((1, D_EXPERT, D_MODEL), lambda b, be, br, nb: (be[b], 0, 0))],
            out_specs=pl.BlockSpec(memory_space=pl.ANY),
            scratch_shapes=[pltpu.VMEM((m * PACK_TILES, LANES), jnp.uint32),
                            pltpu.VMEM((m * PACK_TILES, LANES), jnp.uint32),
                            pltpu.VMEM((m // LANES, LANES), jnp.int32),
                            pltpu.SMEM((m // LANES, LANES), jnp.int32),
                            pltpu.SMEM((m // LANES, LANES), jnp.int32),
                            pltpu.VMEM((D_MODEL, D_EXPERT), BF16),
                            pltpu.VMEM((D_MODEL, D_EXPERT), BF16),
                            pltpu.VMEM((D_EXPERT, D_MODEL), BF16),
                            pltpu.SemaphoreType.DMA((2,)),
                            pltpu.SemaphoreType.DMA(())]),
        out_shape=jax.ShapeDtypeStruct(((n_ids + m) * PACK_TILES, LANES), jnp.uint32),
        compiler_params=_cparams(),
        name="experts",
    )(block_e, block_row, nb_used, xs, we_gate, we_up, we_down)


def _combine_kernel(*refs):
    half_rank = TOP_K // 2
    y_refs = refs[:half_rank]
    w8_ref, base_ref, g2_ref, fg_ref, o_ref = refs[half_rank:]
    t7 = base_ref.shape[0]
    w8 = w8_ref[...]
    parts = [jnp.zeros((t7, LANES), F32) for _ in range(ROW_TILES)]
    for p in range(half_rank):
        for h in range(2):
            wk = w8[:, p + h * half_rank:p + h * half_rank + 1]
            for j in range(PACK_TILES):
                lo, hi = _unpack_bf16_pair(y_refs[p][pl.ds(h * PACK_TILES + j, t7, stride=SUBLANES), :])
                parts[j] = parts[j] + wk * lo
                parts[j + PACK_TILES] = parts[j + PACK_TILES] + wk * hi
    routed = jnp.concatenate(parts, axis=1)
    h = base_ref[...] + g2_ref[...] * routed
    inv = lax.rsqrt(jnp.mean(h * h, axis=-1, keepdims=True) + EPS)
    o_ref[...] = h * inv * fg_ref[...]


def _combine(ytok, w8, base, g2, fg):
    n = base.shape[0]
    t7 = 256
    y_specs = [pl.BlockSpec((t7 * SUBLANES, LANES), functools.partial(lambda i, p: (p * (n // t7) + i, 0), p=p))
               for p in range(TOP_K // 2)]
    return pl.pallas_call(
        _combine_kernel,
        grid=(n // t7,),
        in_specs=y_specs + [
                  pl.BlockSpec((t7, TOP_K), lambda i: (i, 0)),
                  pl.BlockSpec((t7, D_MODEL), lambda i: (i, 0)),
                  _full((1, D_MODEL)), _full((1, D_MODEL))],
        out_specs=pl.BlockSpec((t7, D_MODEL), lambda i: (i, 0)),
        out_shape=jax.ShapeDtypeStruct((n, D_MODEL), F32),
        compiler_params=_cparams(),
        name="combine",
    )(*([ytok] * (TOP_K // 2)), w8, base, g2, fg)


def kernel(x, c, w_ada, b_ada, norm_mix_g, w_in, conv_w, conv_b, dt_bias, A_log, D_skip, ssd_norm_g, pool_w,
           pool_scale, w_br_ssd, w_br_pool, w_out, norm_ffn_g, w_router, router_bias, we_gate, we_up, we_down,
           ws_gate, ws_up, ws_down, final_norm_g):
    bsz, seq, _ = x.shape
    assert bsz == 1 and w_ada.shape[0] == 1
    n = seq
    h2 = x.reshape(n, D_MODEL)

    mod = _ada(c, w_ada[0], b_ada[0])
    sh1, sc1, g1, sh2, sc2, g2 = [mod[:, k * D_MODEL:(k + 1) * D_MODEL] for k in range(6)]

    wi = w_in[0]
    o_xbc, o_dt = D_SSD, D_SSD + D_SSD + 2 * D_BC
    o_pool = o_dt + N_HEADS
    o_gates = o_pool + D_MODEL
    w_dt = jnp.pad(wi[:, o_dt:o_pool], ((0, 0), (0, LANES - N_HEADS))).astype(BF16)
    proj, dt_raw, u = _in_proj(h2, norm_mix_g[0][None], sh1, sc1, wi[:, o_xbc:o_dt].astype(BF16),
                               wi[:, o_pool:o_gates].astype(BF16), w_dt)

    pad_h = lambda v: jnp.pad(v[None], ((0, 0), (0, LANES - N_HEADS)))
    ypre = _ssd(proj, dt_raw, conv_w[0], conv_b[0][None], pad_h(dt_bias[0]), pad_h(A_log[0]),
                jnp.repeat(D_skip[0], HEADDIM)[None])

    ws_gu = jnp.concatenate([ws_gate[0], ws_up[0]], axis=1).astype(BF16)
    m = EXPERT_BLOCK
    cap = -(-n // m) * m
    base, w8, cnt, xs = _mix(
        proj, u, ypre, h2, wi[:, :o_xbc].astype(BF16), wi[:, o_gates:].astype(BF16), ssd_norm_g[0][None],
        pool_w[0].astype(BF16), pool_scale[0][None], w_br_pool[0].astype(BF16),
        w_br_ssd[0].astype(BF16), w_out[0].astype(BF16), g1, norm_ffn_g[0][None], sh2, sc2, g2,
        w_router[0].astype(BF16), router_bias[0][None], ws_gu, ws_down[0].astype(BF16), cap)

    counts = cnt[0].astype(jnp.int32)
    nblk = (counts + m - 1) // m
    bends = jnp.cumsum(nblk)
    bstarts = bends - nblk
    nb_used = bends[-1]
    n_blocks = -(-(n * TOP_K) // m) + N_EXPERTS + 1
    b_eff = jnp.minimum(jnp.arange(n_blocks, dtype=jnp.int32), nb_used - 1)
    done = bends[None, :] <= b_eff[:, None]
    block_e = jnp.sum(done.astype(jnp.int32), axis=1)
    block_row = block_e * (cap // m) + b_eff - jnp.sum(jnp.where(done, nblk[None, :], 0), axis=1)

    xs = _padfill(counts, xs, cap)
    ytok = _experts(block_e, block_row, nb_used.reshape(1), xs,
                    we_gate[0], we_up[0], we_down[0], n * TOP_K)
    out = _combine(ytok, w8, base, g2, final_norm_g[None])
    return out.reshape(bsz, seq, D_MODEL)
```

```python
import functools

import jax
import jax.numpy as jnp
from jax import lax
from jax.experimental import pallas as pl
from jax.experimental.pallas import tpu as pltpu

F32 = jnp.float32
BF16 = jnp.bfloat16
HIGHEST = lax.Precision.HIGHEST

D_MODEL = 1024
D_SSD = 2048
HEADDIM = 64
N_HEADS = 32
N_GROUPS = 8
HEADS_PER_GROUP = N_HEADS // N_GROUPS
D_STATE = 128
CONV_K = 4
CHUNK = 128
SSD_CHUNKS_PER_STEP = 4
GROUP_W = D_SSD // N_GROUPS
D_BC = N_GROUPS * D_STATE
POOL_WINDOWS = (2, 4, 8, 16)
POOL_GDIM = 256
N_EXPERTS = 64
TOP_K = 8
N_EGROUPS = 8
EXPERTS_PER_GROUP = 8
TOPK_GROUPS = 4
D_EXPERT = 256
D_SHARED = 256
ROUTED_SCALE = 2.5
EPS = 1e-6

LANES = 128
SUBLANES = 8
ROW_TILES = D_MODEL // LANES
PACK_TILES = ROW_TILES // 2
PACK_W = PACK_TILES * LANES
EXPERT_BLOCK = 512
MIX_ISSUE_GROUPS = 32
BEFORE_STAGING = "before_staging"

PROJ_W = D_SSD + 2 * D_BC + D_MODEL
PROJ_CHUNK = 512

VMEM_LIMIT = 56 * 1024 * 1024


def _cparams(sem=("arbitrary",)):
    return pltpu.CompilerParams(dimension_semantics=sem, vmem_limit_bytes=VMEM_LIMIT)


def _full(shape):
    nd = len(shape)
    return pl.BlockSpec(shape, lambda *_: (0,) * nd)


def _silu(v):
    return v * jax.nn.sigmoid(v)


def _pack_bf16_pair(lo, hi):
    lo_bits = lax.bitcast_convert_type(lo.astype(BF16).astype(F32), jnp.uint32)
    hi_bits = lax.bitcast_convert_type(hi.astype(BF16).astype(F32), jnp.uint32)
    return (lo_bits >> 16) | (hi_bits & jnp.uint32(0xFFFF0000))


def _unpack_bf16_pair(words):
    return (lax.bitcast_convert_type(words << 16, F32),
            lax.bitcast_convert_type(words & jnp.uint32(0xFFFF0000), F32))


def _ada_kernel(c_ref, w_ref, b_ref, o_ref):
    c = c_ref[...]
    o_ref[...] = jnp.dot(_silu(c), w_ref[...], preferred_element_type=F32, precision=HIGHEST) + b_ref[...]


def _ada(c, w_ada, b_ada):
    n_out = w_ada.shape[1]
    tn = 1536
    c8 = jnp.broadcast_to(c, (SUBLANES, D_MODEL))
    out = pl.pallas_call(
        _ada_kernel,
        grid=(n_out // tn,),
        in_specs=[_full((SUBLANES, D_MODEL)),
                  pl.BlockSpec((D_MODEL, tn), lambda j: (0, j)),
                  pl.BlockSpec((1, tn), lambda j: (0, j))],
        out_specs=pl.BlockSpec((SUBLANES, tn), lambda j: (0, j)),
        out_shape=jax.ShapeDtypeStruct((SUBLANES, n_out), F32),
        compiler_params=_cparams(),
        name="ada",
    )(c8, w_ada, b_ada.reshape(1, n_out))
    return out[0:1]


def _inproj_kernel(x_ref, g_ref, sh_ref, sc_ref, wa_ref, wb_ref, wdt_ref, proj_ref, dt_ref, u_ref):
    x = x_ref[...]
    inv = lax.rsqrt(jnp.mean(x * x, axis=-1, keepdims=True) + EPS)
    u = x * inv * g_ref[...]
    u = u * (1.0 + sc_ref[...]) + sh_ref[...]
    ub = u.astype(BF16)
    u_ref[...] = ub
    wa = wa_ref.shape[1]
    for c0 in range(0, PROJ_W, PROJ_CHUNK):
        w = wa_ref[:, c0:c0 + PROJ_CHUNK] if c0 < wa else wb_ref[:, c0 - wa:c0 - wa + PROJ_CHUNK]
        proj_ref[:, c0:c0 + PROJ_CHUNK] = jnp.dot(ub, w, preferred_element_type=F32).astype(BF16)
    dt_ref[...] = jnp.dot(ub, wdt_ref[...], preferred_element_type=F32)


def _in_proj(x2, g, sh, sc, w_a, w_b, w_dt):
    n = x2.shape[0]
    tm = 512
    return pl.pallas_call(
        _inproj_kernel,
        grid=(n // tm,),
        in_specs=[pl.BlockSpec((tm, D_MODEL), lambda i: (i, 0)),
                  _full((1, D_MODEL)), _full((1, D_MODEL)), _full((1, D_MODEL)),
                  pl.BlockSpec(w_a.shape, lambda i: (0, 0), pipeline_mode=pl.Buffered(1)),
                  pl.BlockSpec(w_b.shape, lambda i: (0, 0), pipeline_mode=pl.Buffered(1)),
                  _full((D_MODEL, LANES))],
        out_specs=[pl.BlockSpec((tm, PROJ_W), lambda i: (i, 0)),
                   pl.BlockSpec((tm, LANES), lambda i: (i, 0)),
                   pl.BlockSpec((tm, D_MODEL), lambda i: (i, 0))],
        out_shape=[jax.ShapeDtypeStruct((n, PROJ_W), BF16),
                   jax.ShapeDtypeStruct((n, LANES), F32),
                   jax.ShapeDtypeStruct((n, D_MODEL), BF16)],
        compiler_params=_cparams(),
        name="in_proj",
    )(x2, g, sh, sc, w_a, w_b, w_dt)


def _conv_silu(cur_ref, ext_ref, w_ref, b_ref, out_ref, width, cw=512):
    t = cur_ref.shape[0]
    for c0 in range(0, width, cw):
        sl = slice(c0, c0 + cw)
        cur = cur_ref[:, sl].astype(F32)
        ext_ref[SUBLANES:, sl] = cur
        acc = cur * w_ref[CONV_K - 1:CONV_K, sl] + b_ref[:, sl]
        for s in range(1, CONV_K):
            acc = acc + ext_ref[pl.ds(SUBLANES - s, t), sl] * w_ref[CONV_K - 1 - s:CONV_K - s, sl]
        out_ref[:, sl] = _silu(acc).astype(out_ref.dtype)
        ext_ref[0:SUBLANES, sl] = cur[t - SUBLANES:t]


def _ssd_kernel(xs_ref, bc_ref, dt_ref, cwx_ref, cwbc_ref, cbx_ref, cbbc_ref, dtb_ref, alog_ref,
                dskip_ref, o_ref, tailx, tailbc, state, xc, bcc):
    q = CHUNK
    pair_w = 2 * HEADDIM

    @pl.when(pl.program_id(0) == 0)
    def _():
        tailx[...] = jnp.zeros_like(tailx)
        tailbc[...] = jnp.zeros_like(tailbc)
        state[...] = jnp.zeros_like(state)

    _conv_silu(xs_ref, tailx, cwx_ref, cbx_ref, xc, D_SSD)
    _conv_silu(bc_ref, tailbc, cwbc_ref, cbbc_ref, bcc, 2 * D_BC)

    for c in range(xs_ref.shape[0] // q):
        rows = slice(c * q, (c + 1) * q)
        v = dt_ref[rows, :] + dtb_ref[...]
        dt = jnp.maximum(v, 0.0) + jnp.log(1.0 + jnp.exp(-jnp.abs(v)))
        a = dt * (-jnp.exp(alog_ref[...]))
        ri = lax.broadcasted_iota(jnp.int32, (q, q), 0)
        ci = lax.broadcasted_iota(jnp.int32, (q, q), 1)
        causal = ri >= ci
        a_cs = jnp.dot(causal.astype(F32), a, preferred_element_type=F32, precision=HIGHEST)
        a_cs_t = a_cs.T
        dt_t = dt.T
        first_half = ci < HEADDIM
        first_half_row = first_half[0:1, :]

        for g in range(N_GROUPS):
            b_g = bcc[rows, g * D_STATE:(g + 1) * D_STATE]
            c_g = bcc[rows, D_BC + g * D_STATE:D_BC + (g + 1) * D_STATE]
            cb = lax.dot_general(c_g, b_g, (((1,), (1,)), ((), ())), preferred_element_type=F32)
            b_t = b_g.astype(F32).T
            st_g = state[g]
            y_off = jnp.dot(c_g, st_g.astype(BF16), preferred_element_type=F32)
            for pi in range(HEADS_PER_GROUP // 2):
                h0 = g * HEADS_PER_GROUP + 2 * pi
                lanes = slice(h0 * HEADDIM, h0 * HEADDIM + pair_w)
                gl = slice(pi * pair_w, (pi + 1) * pair_w)
                ms, ws, cols, lasts = [], [], [], []
                for h in (h0, h0 + 1):
                    col = jnp.broadcast_to(a_cs[:, h:h + 1], (q, q))
                    row = a_cs_t[h:h + 1, :]
                    dtrow = dt_t[h:h + 1, :]
                    decay = jnp.exp(jnp.where(causal, col - row, -jnp.inf))
                    ms.append((cb * decay * dtrow).astype(BF16))
                    last = a_cs_t[h:h + 1, q - 1:q]
                    ws.append((b_t * (jnp.exp(last - row) * dtrow)).astype(BF16))
                    cols.append(col)
                    lasts.append(last)
                xp = xc[rows, lanes]
                zero = jnp.zeros_like(xp)
                x_bd = jnp.concatenate([jnp.where(first_half, xp, zero), jnp.where(first_half, zero, xp)], axis=0)
                y_p = jnp.dot(jnp.concatenate(ms, axis=1), x_bd, preferred_element_type=F32)
                y_p = y_p + y_off[:, gl] * jnp.exp(jnp.where(first_half, cols[0], cols[1]))
                o_ref[rows, lanes] = (y_p + dskip_ref[:, lanes] * xp.astype(F32)).astype(BF16)
                s_new = jnp.dot(jnp.concatenate(ws, axis=1), x_bd, preferred_element_type=F32)
                carry = jnp.exp(jnp.where(first_half_row, lasts[0], lasts[1]))
                state[g, :, gl] = st_g[:, gl] * carry + s_new


def _ssd(proj, dt_raw, conv_w, conv_b, dt_bias, a_log, d_skip):
    n = proj.shape[0]
    t = SSD_CHUNKS_PER_STEP * CHUNK
    return pl.pallas_call(
        _ssd_kernel,
        grid=(n // t,),
        in_specs=[pl.BlockSpec((t, D_SSD), lambda i: (i, 0)),
                  pl.BlockSpec((t, 2 * D_BC), lambda i: (i, 1)),
                  pl.BlockSpec((t, LANES), lambda i: (i, 0)),
                  pl.BlockSpec((CONV_K, D_SSD), lambda i: (0, 0)),
                  pl.BlockSpec((CONV_K, 2 * D_BC), lambda i: (0, 1)),
                  pl.BlockSpec((1, D_SSD), lambda i: (0, 0)),
                  pl.BlockSpec((1, 2 * D_BC), lambda i: (0, 1)),
                  _full((1, LANES)), _full((1, LANES)),
                  _full((1, D_SSD))],
        out_specs=pl.BlockSpec((t, D_SSD), lambda i: (i, 0)),
        out_shape=jax.ShapeDtypeStruct((n, D_SSD), BF16),
        scratch_shapes=[pltpu.VMEM((SUBLANES + t, D_SSD), F32),
                        pltpu.VMEM((SUBLANES + t, 2 * D_BC), F32),
                        pltpu.VMEM((N_GROUPS, D_STATE, GROUP_W), F32),
                        pltpu.VMEM((t, D_SSD), BF16),
                        pltpu.VMEM((t, 2 * D_BC), BF16)],
        compiler_params=_cparams(),
        name="ssd",
    )(proj, proj, dt_raw, conv_w, conv_w, conv_b, conv_b, dt_bias, a_log, d_skip)


PAD_ID = 0xFFFFFFFF
TOK_ROW = PACK_TILES
MASK_ROWS = (PACK_TILES + 1, PACK_TILES + 2)


def _tile_rows(ref, first_row):
    return ref.at[pl.ds(pl.multiple_of(first_row * SUBLANES, SUBLANES), SUBLANES)]


def _mix_kernel(xp_ref, u_ref, ypre_ref, x_ref, wz_ref, wgt_ref, sng_ref, poolw_ref, pscale_ref, wbp_ref, wbs_ref, wout_ref,
                g1_ref, ng_ref, sh2_ref, sc2_ref, g2_ref, wr_ref, rb_ref, wsgu_ref, wsd_ref, zero_ref,
                base_ref, ws_ref, cnt_ref, xs_ref, ptail, run, stage, dvm, dsm, sem, sem_s, *, n_tiles, cap):
    t = x_ref.shape[0]
    i = pl.program_id(0)
    slot = lax.rem(i, 2)

    @pl.when(i == 0)
    def _():
        ptail[...] = jnp.zeros_like(ptail)
        run[...] = jnp.zeros_like(run)
        stage[...] = jnp.zeros_like(stage)

    def issue_rows(s, group, after=None):
        zero = 0
        if after is not None:
            probe = jnp.max(jnp.abs(after[0:SUBLANES, 0:min(LANES, after.shape[1])])).astype(jnp.int32)
            zero = probe * zero_ref[0]
        per = t // MIX_ISSUE_GROUPS
        for tt in range(group * per, (group + 1) * per):
            for k in range(TOP_K):
                pltpu.make_async_copy(stage.at[s, pl.ds(tt * SUBLANES, SUBLANES)],
                                      _tile_rows(xs_ref, dsm[s, k, tt] + zero), sem.at[s]).start()

    def drain_rows(s):
        for _ in range(TOP_K):
            pltpu.make_async_copy(stage.at[s], xs_ref.at[pl.ds(0, t * SUBLANES)], sem.at[s]).wait()

    def tile(s, milestone=None):
        _mix_tile_body(xp_ref, u_ref, ypre_ref, x_ref, wz_ref, wgt_ref, sng_ref, poolw_ref, pscale_ref, wbp_ref, wbs_ref, wout_ref,
                       g1_ref, ng_ref, sh2_ref, sc2_ref, g2_ref, wr_ref, rb_ref, wsgu_ref, wsd_ref,
                       base_ref, ws_ref, cnt_ref, ptail, run, stage, dvm, dsm, sem_s, i, s, cap, milestone)

    @pl.when(i == 0)
    def _():
        tile(0)

    groups = iter(range(1, MIX_ISSUE_GROUPS))

    def milestone(value):
        if value is BEFORE_STAGING:
            assert next(groups, None) is None
            @pl.when(i >= 2)
            def _():
                drain_rows(slot)
        else:
            issue_rows(1 - slot, next(groups), value)

    def wait_slot_rows(s):
        pltpu.make_async_copy(dvm, dsm.at[s], sem_s).wait()

    @pl.when((i >= 1) & (i < n_tiles))
    def _():
        wait_slot_rows(1 - slot)
        issue_rows(1 - slot, 0)
        tile(slot, milestone)

    @pl.when(i == n_tiles)
    def _():
        wait_slot_rows(1 - slot)
        for group in range(MIX_ISSUE_GROUPS):
            issue_rows(1 - slot, group)
        if n_tiles >= 2:
            drain_rows(slot)
        drain_rows(1 - slot)


def _mix_tile_body(xp_ref, u_ref, ypre_ref, x_ref, wz_ref, wgt_ref, sng_ref, poolw_ref, pscale_ref, wbp_ref, wbs_ref, wout_ref,
                   g1_ref, ng_ref, sh2_ref, sc2_ref, g2_ref, wr_ref, rb_ref, wsgu_ref, wsd_ref,
                   base_ref, ws_ref, cnt_ref, ptail, run, stage, dvm, dsm, sem_s, i, slot, cap, milestone):
    t = x_ref.shape[0]
    if milestone is None:
        milestone = lambda value: None

    xp = xp_ref[...].astype(F32)
    ext = jnp.concatenate([ptail[...], xp], axis=0)
    ptail[...] = xp[t - 2 * SUBLANES:t]
    pos = (lax.broadcasted_iota(jnp.int32, (t, 1), 0) + (i * t + 1)).astype(F32)
    pooled = []
    for gi, w in enumerate(POOL_WINDOWS):
        sl = slice(gi * POOL_GDIM, (gi + 1) * POOL_GDIM)
        e = ext[:, sl]
        s = e
        span = 1
        while span < w:
            s = s + pltpu.roll(s, span, axis=0)
            span *= 2
        win = s[2 * SUBLANES:]
        mean = win / jnp.minimum(pos, float(w))
        pg = (mean - xp[:, sl]).astype(BF16)
        mixed = jnp.dot(pg, poolw_ref[gi], preferred_element_type=F32)
        pooled.append(mixed * pscale_ref[:, sl])
        milestone(pooled[-1])
    pooled = jnp.concatenate(pooled, axis=1).astype(BF16)
    y_pool = jnp.dot(pooled, wbp_ref[...], preferred_element_type=F32)
    milestone(y_pool)
    ub = u_ref[...]
    z = jnp.dot(ub, wz_ref[...], preferred_element_type=F32)
    milestone(z)
    yn = []
    for g in range(N_GROUPS):
        sl = slice(g * GROUP_W, (g + 1) * GROUP_W)
        yg = ypre_ref[:, sl].astype(F32) * _silu(z[:, sl])
        inv_g = lax.rsqrt(jnp.mean(yg * yg, axis=-1, keepdims=True) + EPS)
        yn.append((yg * inv_g * sng_ref[:, sl]).astype(BF16))
        milestone(inv_g)
    y_ssd = jnp.dot(jnp.concatenate(yn, axis=1), wbs_ref[...], preferred_element_type=F32)
    milestone(y_ssd)
    gates = jnp.dot(ub, wgt_ref[...], preferred_element_type=F32)
    milestone(gates)
    g_ssd = jax.nn.sigmoid(gates[:, :D_MODEL])
    g_pool = jax.nn.sigmoid(gates[:, D_MODEL:])
    mixed = (g_ssd * y_ssd + g_pool * y_pool).astype(BF16)
    h = x_ref[...] + g1_ref[...] * jnp.dot(mixed, wout_ref[...], preferred_element_type=F32)
    milestone(h)

    inv = lax.rsqrt(jnp.mean(h * h, axis=-1, keepdims=True) + EPS)
    milestone(inv)
    u2 = h * inv * ng_ref[...]
    u2 = u2 * (1.0 + sc2_ref[...]) + sh2_ref[...]
    u2b = u2.astype(BF16)
    packed = [_pack_bf16_pair(u2[:, j * LANES:(j + 1) * LANES],
                              u2[:, (j + PACK_TILES) * LANES:(j + PACK_TILES + 1) * LANES])
              for j in range(PACK_TILES)]

    hs = jnp.dot(u2b, wsgu_ref[...], preferred_element_type=F32)
    milestone(hs)
    act = (_silu(hs[:, :D_SHARED]) * hs[:, D_SHARED:]).astype(BF16)
    shared = jnp.dot(act, wsd_ref[...], preferred_element_type=F32)
    milestone(shared)
    base_ref[...] = h + g2_ref[...] * shared

    logits = jnp.dot(u2b, wr_ref[...], preferred_element_type=F32)
    scores = jax.nn.sigmoid(logits)
    milestone(scores)
    choice = scores + rb_ref[...]
    lane = lax.broadcasted_iota(jnp.int32, (t, N_EXPERTS), 1)
    lane_f = lane.astype(F32)
    lane_grp = lane // EXPERTS_PER_GROUP
    neg = -jnp.inf
    gscore = []
    for g in range(N_EGROUPS):
        vg = jnp.where(lane_grp == g, choice, neg)
        m1 = jnp.max(vg, axis=-1, keepdims=True)
        i1 = jnp.min(jnp.where(vg == m1, lane_f, float(N_EXPERTS)), axis=-1, keepdims=True)
        m2 = jnp.max(jnp.where(lane_f == i1, neg, vg), axis=-1, keepdims=True)
        gscore.append(m1 + m2)
        milestone(gscore[-1])
    gmask = jnp.zeros((t, N_EXPERTS), jnp.bool_)
    for g in range(N_EGROUPS):
        rank = jnp.zeros((t, 1), F32)
        for g2 in range(N_EGROUPS):
            if g2 == g:
                continue
            better = (gscore[g2] > gscore[g]) | ((gscore[g2] == gscore[g]) & (g2 < g))
            rank = rank + better.astype(F32)
        gmask = gmask | ((rank < float(TOPK_GROUPS)) & (lane_grp == g))
    work = jnp.where(gmask, choice, neg)
    sel = jnp.zeros((t, N_EXPERTS), jnp.bool_)
    onehots, idxs, sks = [], [], []
    for k in range(TOP_K):
        m = jnp.max(work, axis=-1, keepdims=True)
        idx = jnp.min(jnp.where(work == m, lane_f, float(N_EXPERTS)), axis=-1, keepdims=True)
        oh = lane_f == idx
        onehots.append(oh)
        idxs.append(idx)
        sks.append(jnp.sum(jnp.where(oh, scores, 0.0), axis=-1, keepdims=True))
        if k == TOP_K // 2 - 1:
            milestone(sks[-1])
        sel = sel | oh
        work = jnp.where(oh, neg, work)
    denom = sks[0]
    for k in range(1, TOP_K):
        denom = denom + sks[k]
    milestone(denom)

    ri = lax.broadcasted_iota(jnp.int32, (t, t), 0)
    ci = lax.broadcasted_iota(jnp.int32, (t, t), 1)
    before = (ri > ci).astype(BF16)
    sel_f = jnp.where(sel, 1.0, 0.0)
    sel_b = sel_f.astype(BF16)
    pos_tile = jnp.dot(before, sel_b, preferred_element_type=F32) + run[...]
    er = lax.broadcasted_iota(jnp.int32, (N_EXPERTS, N_EXPERTS), 0)
    ec = lax.broadcasted_iota(jnp.int32, (N_EXPERTS, N_EXPERTS), 1)
    rank_tile = jnp.dot(sel_b, (er < ec).astype(BF16), preferred_element_type=F32)
    lane128 = lax.broadcasted_iota(jnp.int32, (t, LANES), 1)
    k_iota = lax.broadcasted_iota(jnp.int32, (t, TOP_K), 1).astype(F32)
    dest = jnp.zeros((t, LANES), F32)
    w_sorted = jnp.zeros((t, TOP_K), F32)
    for k in range(TOP_K):
        pk = jnp.sum(jnp.where(onehots[k], pos_tile, 0.0), axis=-1, keepdims=True)
        rk = jnp.sum(jnp.where(onehots[k], rank_tile, 0.0), axis=-1, keepdims=True)
        dest = jnp.where(lane128 == k, idxs[k] * float(cap) + pk, dest)
        w_sorted = jnp.where(k_iota == rk, sks[k] / denom * ROUTED_SCALE, w_sorted)
    ws_ref[...] = w_sorted
    total = run[...] + jnp.sum(sel_f, axis=0, keepdims=True)
    run[...] = total
    cnt_ref[...] = total

    milestone(BEFORE_STAGING)
    for j in range(PACK_TILES):
        stage[slot, pl.ds(j, t, stride=SUBLANES), :] = packed[j]
    tok = lax.broadcasted_iota(jnp.int32, (t, LANES), 0) + i * t
    stage[slot, pl.ds(TOK_ROW, t, stride=SUBLANES), :] = tok.astype(jnp.uint32)
    bit = jnp.where(sel, jnp.left_shift(1, lane & 15), 0).astype(F32)
    words = []
    for q in range(N_EXPERTS // 16):
        part = jnp.sum(jnp.where((lane >> 4) == q, bit, 0.0), axis=-1, keepdims=True)
        words.append(part.astype(jnp.int32).astype(jnp.uint32))
    for w, row in enumerate(MASK_ROWS):
        word = words[2 * w] | (words[2 * w + 1] << 16)
        stage[slot, pl.ds(row, t, stride=SUBLANES), :] = jnp.broadcast_to(word, (t, LANES))

    dvm[...] = dest.T[0:TOP_K, :].astype(jnp.int32)
    pltpu.make_async_copy(dvm, dsm.at[slot], sem_s).start()


def _mix(proj, u, ypre, x2, w_z, w_gates, ssd_ng, pool_w, pool_scale, w_br_pool, w_br_ssd, w_out, g1, ng, sh2, sc2,
         g2, w_router, router_bias, ws_gu, ws_down, cap):
    n = x2.shape[0]
    t = 256
    n_tiles = n // t
    xp_blk = (D_SSD + 2 * D_BC) // D_MODEL
    const = lambda shape: pl.BlockSpec(shape, lambda i: (0,) * len(shape), pipeline_mode=pl.Buffered(1))
    tile = lambda i: jnp.minimum(i, n_tiles - 1)
    row = lambda i: (tile(i), 0)
    return pl.pallas_call(
        functools.partial(_mix_kernel, n_tiles=n_tiles, cap=cap),
        grid=(n_tiles + 1,),
        in_specs=[pl.BlockSpec((t, D_MODEL), lambda i: (tile(i), xp_blk)),
                  pl.BlockSpec((t, D_MODEL), row),
                  pl.BlockSpec((t, D_SSD), row),
                  pl.BlockSpec((t, D_MODEL), row),
                  const((D_MODEL, D_SSD)),
                  const((D_MODEL, 2 * D_MODEL)),
                  _full((1, D_SSD)),
                  const((len(POOL_WINDOWS), POOL_GDIM, POOL_GDIM)),
                  _full((1, D_MODEL)),
                  const((D_MODEL, D_MODEL)),
                  const((D_SSD, D_MODEL)),
                  const((D_MODEL, D_MODEL)),
                  _full((1, D_MODEL)), _full((1, D_MODEL)), _full((1, D_MODEL)), _full((1, D_MODEL)),
                  _full((1, D_MODEL)),
                  _full((D_MODEL, N_EXPERTS)),
                  _full((1, N_EXPERTS)),
                  const((D_MODEL, 2 * D_SHARED)),
                  const((D_SHARED, D_MODEL)),
                  pl.BlockSpec(memory_space=pltpu.SMEM)],
        out_specs=[pl.BlockSpec((t, D_MODEL), row),
                   pl.BlockSpec((t, TOP_K), row),
                   _full((1, N_EXPERTS)),
                   pl.BlockSpec(memory_space=pl.ANY)],
        out_shape=[jax.ShapeDtypeStruct((n, D_MODEL), F32),
                   jax.ShapeDtypeStruct((n, TOP_K), F32),
                   jax.ShapeDtypeStruct((1, N_EXPERTS), F32),
                   jax.ShapeDtypeStruct((N_EXPERTS * cap * SUBLANES, LANES), jnp.uint32)],
        scratch_shapes=[pltpu.VMEM((2 * SUBLANES, D_MODEL), F32),
                        pltpu.VMEM((1, N_EXPERTS), F32),
                        pltpu.VMEM((2, t * SUBLANES, LANES), jnp.uint32),
                        pltpu.VMEM((TOP_K, t), jnp.int32),
                        pltpu.SMEM((2, TOP_K, t), jnp.int32),
                        pltpu.SemaphoreType.DMA((2,)),
                        pltpu.SemaphoreType.DMA(())],
        compiler_params=_cparams(),
        name="mix",
    )(proj, u, ypre, x2, w_z, w_gates, ssd_ng, pool_w, pool_scale, w_br_pool, w_br_ssd, w_out, g1, ng, sh2, sc2, g2,
      w_router, router_bias, ws_gu, ws_down, jnp.zeros((1,), jnp.int32))


def _padfill_kernel(cnt_ref, xs_in_ref, xs_ref, padbuf, sem, *, cap):
    del xs_in_ref
    sub = lax.broadcasted_iota(jnp.int32, padbuf.shape, 0) & (SUBLANES - 1)
    padbuf[...] = jnp.where(sub == TOK_ROW, jnp.uint32(PAD_ID), jnp.uint32(0))
    sizes = [1 << bit for bit in range(EXPERT_BLOCK.bit_length() - 1)]

    def pad_copies(e, fn):
        c = cnt_ref[e]
        n_pad = (EXPERT_BLOCK - (c & (EXPERT_BLOCK - 1))) & (EXPERT_BLOCK - 1)
        first = e * cap + c
        for size in sizes:
            @pl.when((n_pad & size) != 0)
            def _():
                start = first + (n_pad & (size - 1))
                dst = xs_ref.at[pl.ds(pl.multiple_of(start * SUBLANES, SUBLANES), size * SUBLANES)]
                fn(pltpu.make_async_copy(padbuf.at[pl.ds(0, size * SUBLANES)], dst, sem))

    def issue(e, carry):
        pad_copies(e, lambda cp: cp.start())
        return carry

    lax.fori_loop(0, N_EXPERTS, issue, 0)

    def drain(e, carry):
        pad_copies(e, lambda cp: cp.wait())
        return carry

    lax.fori_loop(0, N_EXPERTS, drain, 0)


def _padfill(counts, xs, cap):
    return pl.pallas_call(
        functools.partial(_padfill_kernel, cap=cap),
        grid_spec=pltpu.PrefetchScalarGridSpec(
            num_scalar_prefetch=1,
            grid=(1,),
            in_specs=[pl.BlockSpec(memory_space=pl.ANY)],
            out_specs=pl.BlockSpec(memory_space=pl.ANY),
            scratch_shapes=[pltpu.VMEM((EXPERT_BLOCK // 2 * SUBLANES, LANES), jnp.uint32),
                            pltpu.SemaphoreType.DMA(())]),
        out_shape=jax.ShapeDtypeStruct(xs.shape, xs.dtype),
        input_output_aliases={1: 0},
        compiler_params=_cparams(),
        name="padfill",
    )(counts, xs)


def _expert_kernel(be_ref, br_ref, nb_ref, x_ref, wg_ref, wu_ref, wd_ref, ytok_ref, ybuf0, ybuf1, idv, ids0, ids1,
                   wgb, wub, wdb, sem, sem_ids, *, n_ids):
    del br_ref
    m = EXPERT_BLOCK
    b = pl.program_id(0)
    nb = nb_ref[0]
    ybufs, idss = (ybuf0, ybuf1), (ids0, ids1)
    n_phases = m // LANES
    half = D_MODEL // 2

    @pl.when((b == 0) | (be_ref[b] != be_ref[jnp.maximum(b - 1, 0)]))
    def _():
        wgb[...] = wg_ref[0].astype(BF16)
        wub[...] = wu_ref[0].astype(BF16)
        wdb[...] = wd_ref[0].astype(BF16)

    def half_tile(ref, unit):
        return ref.at[pl.ds(pl.multiple_of(unit * PACK_TILES, PACK_TILES), PACK_TILES)]

    def issue(s, a0):
        for col in range(LANES):
            pltpu.make_async_copy(half_tile(ybufs[s], a0 * LANES + col),
                                  half_tile(ytok_ref, idss[s][a0, col]), sem.at[s]).start()

    def step(s, prev):
        tokrep = x_ref[pl.ds(TOK_ROW, m, stride=SUBLANES), :]
        e = be_ref[b]
        below_lo = jnp.where(e >= 32, -1, (1 << jnp.minimum(e, 31)) - 1).astype(jnp.uint32)
        below_hi = jnp.where(e >= 32, (1 << jnp.maximum(e - 32, 0)) - 1, 0).astype(jnp.uint32)
        rank = (lax.population_count(x_ref[pl.ds(MASK_ROWS[0], m, stride=SUBLANES), :] & below_lo)
                + lax.population_count(x_ref[pl.ds(MASK_ROWS[1], m, stride=SUBLANES), :] & below_hi))
        r = lax.broadcasted_iota(jnp.int32, (m, LANES), 0)
        ln = lax.broadcasted_iota(jnp.int32, (m, LANES), 1)
        n_tok = n_ids // TOP_K
        rank = rank.astype(jnp.int32)
        half_rank = TOP_K // 2
        unit = ((rank % half_rank) * n_tok + tokrep.astype(jnp.int32)) * 2 + rank // half_rank
        idi = jnp.where(tokrep == jnp.uint32(PAD_ID), n_ids + r, unit)
        diag = jnp.where((r & (LANES - 1)) == ln, idi, 0).astype(F32)
        idv[...] = jnp.sum(diag.reshape(m // LANES, LANES, LANES), axis=1).astype(jnp.int32)
        ids_copy = pltpu.make_async_copy(idv, idss[s], sem_ids)
        ids_copy.start()

        halves = [_unpack_bf16_pair(x_ref[pl.ds(j, m, stride=SUBLANES), :]) for j in range(PACK_TILES)]
        x = jnp.concatenate([lo for lo, _ in halves] + [hi for _, hi in halves], axis=1).astype(BF16)
        phases = iter(range(n_phases))
        if prev is not None:
            issue(prev, next(phases))
        gate = jnp.dot(x, wgb[...], preferred_element_type=F32)
        if prev is not None:
            issue(prev, next(phases))
        up = jnp.dot(x, wub[...], preferred_element_type=F32)
        hid = (_silu(gate) * up).astype(BF16)

        def drain(which):
            pltpu.make_async_copy(ybufs[which], ytok_ref.at[pl.ds(0, m * PACK_TILES)], sem.at[which]).wait()

        if prev is not None:
            @pl.when(b >= 2)
            def _():
                drain(s)
        ys = []
        for c0 in (0, half):
            if prev is not None:
                issue(prev, next(phases))
            ys.append(jnp.dot(hid, wdb[:, c0:c0 + half], preferred_element_type=F32))
        for j in range(PACK_TILES):
            sl = slice(j * LANES, (j + 1) * LANES)
            ybufs[s][pl.ds(j, m, stride=PACK_TILES), :] = _pack_bf16_pair(ys[0][:, sl], ys[1][:, sl])
        ids_copy.wait()
        if prev is not None:
            for a0 in phases:
                issue(prev, a0)

            @pl.when(b == nb)
            def _():
                drain(prev)

    @pl.when(b == 0)
    def _():
        step(0, None)

    for s in range(2):
        @pl.when((b >= 1) & (b <= nb) & (lax.rem(b, 2) == s))
        def _():
            step(s, 1 - s)


def _experts(block_e, block_row, nb_used, xs, we_gate, we_up, we_down, n_ids):
    n_blocks = block_e.shape[0]
    m = EXPERT_BLOCK
    rows = m * SUBLANES
    return pl.pallas_call(
        functools.partial(_expert_kernel, n_ids=n_ids),
        grid_spec=pltpu.PrefetchScalarGridSpec(
            num_scalar_prefetch=3,
            grid=(n_blocks,),
            in_specs=[pl.BlockSpec((rows, LANES), lambda b, be, br, nb: (br[b], 0)),
                      pl.BlockSpec((1, D_MODEL, D_EXPERT), lambda b, be, br, nb: (be[b], 0, 0)),
                      pl.BlockSpec((1, D_MODEL, D_EXPERT), lambda b, be, br, nb: (be[b], 0, 0)),
                      pl.BlockSpec((1, D_EXPERT, D_MODEL), lambda b, be, br, nb: (be[b], 0, 0))],
            out_specs=pl.BlockSpec(memory_space=pl.ANY),
            scratch_shapes=[pltpu.VMEM((m * PACK_TILES, LANES), jnp.uint32),
                            pltpu.VMEM((m * PACK_TILES, LANES), jnp.uint32),
                            pltpu.VMEM((m // LANES, LANES), jnp.int32),
                            pltpu.SMEM((m // LANES, LANES), jnp.int32),
                            pltpu.SMEM((m // LANES, LANES), jnp.int32),
                            pltpu.VMEM((D_MODEL, D_EXPERT), BF16),
                            pltpu.VMEM((D_MODEL, D_EXPERT), BF16),
                            pltpu.VMEM((D_EXPERT, D_MODEL), BF16),
                            pltpu.SemaphoreType.DMA((2,)),
                            pltpu.SemaphoreType.DMA(())]),
        out_shape=jax.ShapeDtypeStruct(((n_ids + m) * PACK_TILES, LANES), jnp.uint32),
        compiler_params=_cparams(),
        name="experts",
    )(block_e, block_row, nb_used, xs, we_gate, we_up, we_down)


def _combine_kernel(*refs):
    half_rank = TOP_K // 2
    y_refs = refs[:half_rank]
    w8_ref, base_ref, g2_ref, fg_ref, o_ref = refs[half_rank:]
    t7 = base_ref.shape[0]
    w8 = w8_ref[...]
    parts = [jnp.zeros((t7, LANES), F32) for _ in range(ROW_TILES)]
    for p in range(half_rank):
        for h in range(2):
            wk = w8[:, p + h * half_rank:p + h * half_rank + 1]
            for j in range(PACK_TILES):
                lo, hi = _unpack_bf16_pair(y_refs[p][pl.ds(h * PACK_TILES + j, t7, stride=SUBLANES), :])
                parts[j] = parts[j] + wk * lo
                parts[j + PACK_TILES] = parts[j + PACK_TILES] + wk * hi
    routed = jnp.concatenate(parts, axis=1)
    h = base_ref[...] + g2_ref[...] * routed
    inv = lax.rsqrt(jnp.mean(h * h, axis=-1, keepdims=True) + EPS)
    o_ref[...] = h * inv * fg_ref[...]


def _combine(ytok, w8, base, g2, fg):
    n = base.shape[0]
    t7 = 512
    y_specs = [pl.BlockSpec((t7 * SUBLANES, LANES), functools.partial(lambda i, p: (p * (n // t7) + i, 0), p=p))
               for p in range(TOP_K // 2)]
    return pl.pallas_call(
        _combine_kernel,
        grid=(n // t7,),
        in_specs=y_specs + [
                  pl.BlockSpec((t7, TOP_K), lambda i: (i, 0)),
                  pl.BlockSpec((t7, D_MODEL), lambda i: (i, 0)),
                  _full((1, D_MODEL)), _full((1, D_MODEL))],
        out_specs=pl.BlockSpec((t7, D_MODEL), lambda i: (i, 0)),
        out_shape=jax.ShapeDtypeStruct((n, D_MODEL), F32),
        compiler_params=_cparams(),
        name="combine",
    )(*([ytok] * (TOP_K // 2)), w8, base, g2, fg)


def kernel(x, c, w_ada, b_ada, norm_mix_g, w_in, conv_w, conv_b, dt_bias, A_log, D_skip, ssd_norm_g, pool_w,
           pool_scale, w_br_ssd, w_br_pool, w_out, norm_ffn_g, w_router, router_bias, we_gate, we_up, we_down,
           ws_gate, ws_up, ws_down, final_norm_g):
    bsz, seq, _ = x.shape
    assert bsz == 1 and w_ada.shape[0] == 1
    n = seq
    h2 = x.reshape(n, D_MODEL)

    mod = _ada(c, w_ada[0], b_ada[0])
    sh1, sc1, g1, sh2, sc2, g2 = [mod[:, k * D_MODEL:(k + 1) * D_MODEL] for k in range(6)]

    wi = w_in[0]
    o_xbc, o_dt = D_SSD, D_SSD + D_SSD + 2 * D_BC
    o_pool = o_dt + N_HEADS
    o_gates = o_pool + D_MODEL
    w_dt = jnp.pad(wi[:, o_dt:o_pool], ((0, 0), (0, LANES - N_HEADS))).astype(BF16)
    proj, dt_raw, u = _in_proj(h2, norm_mix_g[0][None], sh1, sc1, wi[:, o_xbc:o_dt].astype(BF16),
                               wi[:, o_pool:o_gates].astype(BF16), w_dt)

    pad_h = lambda v: jnp.pad(v[None], ((0, 0), (0, LANES - N_HEADS)))
    ypre = _ssd(proj, dt_raw, conv_w[0], conv_b[0][None], pad_h(dt_bias[0]), pad_h(A_log[0]),
                jnp.repeat(D_skip[0], HEADDIM)[None])

    ws_gu = jnp.concatenate([ws_gate[0], ws_up[0]], axis=1).astype(BF16)
    m = EXPERT_BLOCK
    cap = -(-n // m) * m
    base, w8, cnt, xs = _mix(
        proj, u, ypre, h2, wi[:, :o_xbc].astype(BF16), wi[:, o_gates:].astype(BF16), ssd_norm_g[0][None],
        pool_w[0].astype(BF16), pool_scale[0][None], w_br_pool[0].astype(BF16),
        w_br_ssd[0].astype(BF16), w_out[0].astype(BF16), g1, norm_ffn_g[0][None], sh2, sc2, g2,
        w_router[0].astype(BF16), router_bias[0][None], ws_gu, ws_down[0].astype(BF16), cap)

    counts = cnt[0].astype(jnp.int32)
    nblk = (counts + m - 1) // m
    bends = jnp.cumsum(nblk)
    bstarts = bends - nblk
    nb_used = bends[-1]
    n_blocks = -(-(n * TOP_K) // m) + N_EXPERTS + 1
    b_eff = jnp.minimum(jnp.arange(n_blocks, dtype=jnp.int32), nb_used - 1)
    done = bends[None, :] <= b_eff[:, None]
    block_e = jnp.sum(done.astype(jnp.int32), axis=1)
    block_row = block_e * (cap // m) + b_eff - jnp.sum(jnp.where(done, nblk[None, :], 0), axis=1)

    xs = _padfill(counts, xs, cap)
    ytok = _experts(block_e, block_row, nb_used.reshape(1), xs,
                    we_gate[0], we_up[0], we_down[0], n * TOP_K)
    out = _combine(ytok, w8, base, g2, final_norm_g[None])
    return out.reshape(bsz, seq, D_MODEL)
```

```python
import functools

import jax
import jax.numpy as jnp
from jax import lax
from jax.experimental import pallas as pl
from jax.experimental.pallas import tpu as pltpu

F32 = jnp.float32
BF16 = jnp.bfloat16
HIGHEST = lax.Precision.HIGHEST

D_MODEL = 1024
D_SSD = 2048
HEADDIM = 64
N_HEADS = 32
N_GROUPS = 8
HEADS_PER_GROUP = N_HEADS // N_GROUPS
D_STATE = 128
CONV_K = 4
CHUNK = 128
SSD_CHUNKS_PER_STEP = 4
GROUP_W = D_SSD // N_GROUPS
D_BC = N_GROUPS * D_STATE
POOL_WINDOWS = (2, 4, 8, 16)
POOL_GDIM = 256
N_EXPERTS = 64
TOP_K = 8
N_EGROUPS = 8
EXPERTS_PER_GROUP = 8
TOPK_GROUPS = 4
D_EXPERT = 256
D_SHARED = 256
ROUTED_SCALE = 2.5
EPS = 1e-6

LANES = 128
SUBLANES = 8
ROW_TILES = D_MODEL // LANES
PACK_TILES = ROW_TILES // 2
PACK_W = PACK_TILES * LANES
EXPERT_BLOCK = 512
MIX_ISSUE_GROUPS = 32
BEFORE_STAGING = "before_staging"

PROJ_W = D_SSD + 2 * D_BC + D_MODEL
PROJ_CHUNK = 512

VMEM_LIMIT = 56 * 1024 * 1024


def _cparams(sem=("arbitrary",)):
    return pltpu.CompilerParams(dimension_semantics=sem, vmem_limit_bytes=VMEM_LIMIT)


def _full(shape):
    nd = len(shape)
    return pl.BlockSpec(shape, lambda *_: (0,) * nd)


def _silu(v):
    return v * jax.nn.sigmoid(v)


def _pack_bf16_pair(lo, hi):
    lo_bits = lax.bitcast_convert_type(lo.astype(BF16).astype(F32), jnp.uint32)
    hi_bits = lax.bitcast_convert_type(hi.astype(BF16).astype(F32), jnp.uint32)
    return (lo_bits >> 16) | (hi_bits & jnp.uint32(0xFFFF0000))


def _unpack_bf16_pair(words):
    return (lax.bitcast_convert_type(words << 16, F32),
            lax.bitcast_convert_type(words & jnp.uint32(0xFFFF0000), F32))


def _ada_kernel(c_ref, w_ref, b_ref, o_ref):
    c = c_ref[...]
    o_ref[...] = jnp.dot(_silu(c), w_ref[...], preferred_element_type=F32, precision=HIGHEST) + b_ref[...]


def _ada(c, w_ada, b_ada):
    n_out = w_ada.shape[1]
    tn = 1536
    c8 = jnp.broadcast_to(c, (SUBLANES, D_MODEL))
    out = pl.pallas_call(
        _ada_kernel,
        grid=(n_out // tn,),
        in_specs=[_full((SUBLANES, D_MODEL)),
                  pl.BlockSpec((D_MODEL, tn), lambda j: (0, j)),
                  pl.BlockSpec((1, tn), lambda j: (0, j))],
        out_specs=pl.BlockSpec((SUBLANES, tn), lambda j: (0, j)),
        out_shape=jax.ShapeDtypeStruct((SUBLANES, n_out), F32),
        compiler_params=_cparams(),
        name="ada",
    )(c8, w_ada, b_ada.reshape(1, n_out))
    return out[0:1]


def _inproj_kernel(x_ref, g_ref, sh_ref, sc_ref, wa_ref, wb_ref, wdt_ref, proj_ref, dt_ref, u_ref):
    x = x_ref[...]
    inv = lax.rsqrt(jnp.mean(x * x, axis=-1, keepdims=True) + EPS)
    u = x * inv * g_ref[...]
    u = u * (1.0 + sc_ref[...]) + sh_ref[...]
    ub = u.astype(BF16)
    u_ref[...] = ub
    wa = wa_ref.shape[1]
    for c0 in range(0, PROJ_W, PROJ_CHUNK):
        w = wa_ref[:, c0:c0 + PROJ_CHUNK] if c0 < wa else wb_ref[:, c0 - wa:c0 - wa + PROJ_CHUNK]
        proj_ref[:, c0:c0 + PROJ_CHUNK] = jnp.dot(ub, w, preferred_element_type=F32).astype(BF16)
    dt_ref[...] = jnp.dot(ub, wdt_ref[...], preferred_element_type=F32)


def _in_proj(x2, g, sh, sc, w_a, w_b, w_dt):
    n = x2.shape[0]
    tm = 512
    return pl.pallas_call(
        _inproj_kernel,
        grid=(n // tm,),
        in_specs=[pl.BlockSpec((tm, D_MODEL), lambda i: (i, 0)),
                  _full((1, D_MODEL)), _full((1, D_MODEL)), _full((1, D_MODEL)),
                  pl.BlockSpec(w_a.shape, lambda i: (0, 0), pipeline_mode=pl.Buffered(1)),
                  pl.BlockSpec(w_b.shape, lambda i: (0, 0), pipeline_mode=pl.Buffered(1)),
                  _full((D_MODEL, LANES))],
        out_specs=[pl.BlockSpec((tm, PROJ_W), lambda i: (i, 0)),
                   pl.BlockSpec((tm, LANES), lambda i: (i, 0)),
                   pl.BlockSpec((tm, D_MODEL), lambda i: (i, 0))],
        out_shape=[jax.ShapeDtypeStruct((n, PROJ_W), BF16),
                   jax.ShapeDtypeStruct((n, LANES), F32),
                   jax.ShapeDtypeStruct((n, D_MODEL), BF16)],
        compiler_params=_cparams(),
        name="in_proj",
    )(x2, g, sh, sc, w_a, w_b, w_dt)


def _conv_silu(cur_ref, ext_ref, w_ref, b_ref, out_ref, width, cw=512):
    t = cur_ref.shape[0]
    for c0 in range(0, width, cw):
        sl = slice(c0, c0 + cw)
        cur = cur_ref[:, sl].astype(F32)
        ext_ref[SUBLANES:, sl] = cur
        acc = cur * w_ref[CONV_K - 1:CONV_K, sl] + b_ref[:, sl]
        for s in range(1, CONV_K):
            acc = acc + ext_ref[pl.ds(SUBLANES - s, t), sl] * w_ref[CONV_K - 1 - s:CONV_K - s, sl]
        out_ref[:, sl] = _silu(acc).astype(out_ref.dtype)
        ext_ref[0:SUBLANES, sl] = cur[t - SUBLANES:t]


def _ssd_kernel(xs_ref, bc_ref, dt_ref, cwx_ref, cwbc_ref, cbx_ref, cbbc_ref, dtb_ref, alog_ref,
                dskip_ref, o_ref, tailx, tailbc, state, xc, bcc):
    q = CHUNK
    pair_w = 2 * HEADDIM

    @pl.when(pl.program_id(0) == 0)
    def _():
        tailx[...] = jnp.zeros_like(tailx)
        tailbc[...] = jnp.zeros_like(tailbc)
        state[...] = jnp.zeros_like(state)

    _conv_silu(xs_ref, tailx, cwx_ref, cbx_ref, xc, D_SSD)
    _conv_silu(bc_ref, tailbc, cwbc_ref, cbbc_ref, bcc, 2 * D_BC)

    for c in range(xs_ref.shape[0] // q):
        rows = slice(c * q, (c + 1) * q)
        v = dt_ref[rows, :] + dtb_ref[...]
        dt = jnp.maximum(v, 0.0) + jnp.log(1.0 + jnp.exp(-jnp.abs(v)))
        a = dt * (-jnp.exp(alog_ref[...]))
        ri = lax.broadcasted_iota(jnp.int32, (q, q), 0)
        ci = lax.broadcasted_iota(jnp.int32, (q, q), 1)
        causal = ri >= ci
        a_cs = jnp.dot(causal.astype(F32), a, preferred_element_type=F32, precision=HIGHEST)
        a_cs_t = a_cs.T
        dt_t = dt.T
        first_half = ci < HEADDIM
        first_half_row = first_half[0:1, :]

        for g in range(N_GROUPS):
            b_g = bcc[rows, g * D_STATE:(g + 1) * D_STATE]
            c_g = bcc[rows, D_BC + g * D_STATE:D_BC + (g + 1) * D_STATE]
            cb = lax.dot_general(c_g, b_g, (((1,), (1,)), ((), ())), preferred_element_type=F32)
            b_t = b_g.astype(F32).T
            st_g = state[g]
            y_off = jnp.dot(c_g, st_g.astype(BF16), preferred_element_type=F32)
            for pi in range(HEADS_PER_GROUP // 2):
                h0 = g * HEADS_PER_GROUP + 2 * pi
                lanes = slice(h0 * HEADDIM, h0 * HEADDIM + pair_w)
                gl = slice(pi * pair_w, (pi + 1) * pair_w)
                ms, ws, cols, lasts = [], [], [], []
                for h in (h0, h0 + 1):
                    col = jnp.broadcast_to(a_cs[:, h:h + 1], (q, q))
                    row = a_cs_t[h:h + 1, :]
                    dtrow = dt_t[h:h + 1, :]
                    decay = jnp.exp(jnp.where(causal, col - row, -jnp.inf))
                    ms.append((cb * decay * dtrow).astype(BF16))
                    last = a_cs_t[h:h + 1, q - 1:q]
                    ws.append((b_t * (jnp.exp(last - row) * dtrow)).astype(BF16))
                    cols.append(col)
                    lasts.append(last)
                xp = xc[rows, lanes]
                zero = jnp.zeros_like(xp)
                x_bd = jnp.concatenate([jnp.where(first_half, xp, zero), jnp.where(first_half, zero, xp)], axis=0)
                y_p = jnp.dot(jnp.concatenate(ms, axis=1), x_bd, preferred_element_type=F32)
                y_p = y_p + y_off[:, gl] * jnp.exp(jnp.where(first_half, cols[0], cols[1]))
                o_ref[rows, lanes] = (y_p + dskip_ref[:, lanes] * xp.astype(F32)).astype(BF16)
                s_new = jnp.dot(jnp.concatenate(ws, axis=1), x_bd, preferred_element_type=F32)
                carry = jnp.exp(jnp.where(first_half_row, lasts[0], lasts[1]))
                state[g, :, gl] = st_g[:, gl] * carry + s_new


def _ssd(proj, dt_raw, conv_w, conv_b, dt_bias, a_log, d_skip):
    n = proj.shape[0]
    t = SSD_CHUNKS_PER_STEP * CHUNK
    return pl.pallas_call(
        _ssd_kernel,
        grid=(n // t,),
        in_specs=[pl.BlockSpec((t, D_SSD), lambda i: (i, 0)),
                  pl.BlockSpec((t, 2 * D_BC), lambda i: (i, 1)),
                  pl.BlockSpec((t, LANES), lambda i: (i, 0)),
                  pl.BlockSpec((CONV_K, D_SSD), lambda i: (0, 0)),
                  pl.BlockSpec((CONV_K, 2 * D_BC), lambda i: (0, 1)),
                  pl.BlockSpec((1, D_SSD), lambda i: (0, 0)),
                  pl.BlockSpec((1, 2 * D_BC), lambda i: (0, 1)),
                  _full((1, LANES)), _full((1, LANES)),
                  _full((1, D_SSD))],
        out_specs=pl.BlockSpec((t, D_SSD), lambda i: (i, 0)),
        out_shape=jax.ShapeDtypeStruct((n, D_SSD), BF16),
        scratch_shapes=[pltpu.VMEM((SUBLANES + t, D_SSD), F32),
                        pltpu.VMEM((SUBLANES + t, 2 * D_BC), F32),
                        pltpu.VMEM((N_GROUPS, D_STATE, GROUP_W), F32),
                        pltpu.VMEM((t, D_SSD), BF16),
                        pltpu.VMEM((t, 2 * D_BC), BF16)],
        compiler_params=_cparams(),
        name="ssd",
    )(proj, proj, dt_raw, conv_w, conv_w, conv_b, conv_b, dt_bias, a_log, d_skip)


PAD_ID = 0xFFFFFFFF
TOK_ROW = PACK_TILES
MASK_ROWS = (PACK_TILES + 1, PACK_TILES + 2)


def _tile_rows(ref, first_row):
    return ref.at[pl.ds(pl.multiple_of(first_row * SUBLANES, SUBLANES), SUBLANES)]


def _mix_kernel(xp_ref, u_ref, ypre_ref, x_ref, wz_ref, wgt_ref, sng_ref, poolw_ref, pscale_ref, wbp_ref, wbs_ref, wout_ref,
                g1_ref, ng_ref, sh2_ref, sc2_ref, g2_ref, wr_ref, rb_ref, wsgu_ref, wsd_ref, zero_ref,
                base_ref, ws_ref, cnt_ref, xs_ref, ptail, run, stage, dvm, dsm, sem, sem_s, *, n_tiles, cap):
    t = x_ref.shape[0]
    i = pl.program_id(0)
    slot = lax.rem(i, 2)

    @pl.when(i == 0)
    def _():
        ptail[...] = jnp.zeros_like(ptail)
        run[...] = jnp.zeros_like(run)
        stage[...] = jnp.zeros_like(stage)

    def issue_rows(s, group, after=None):
        zero = 0
        if after is not None:
            probe = jnp.max(jnp.abs(after[0:SUBLANES, 0:min(LANES, after.shape[1])])).astype(jnp.int32)
            zero = probe * zero_ref[0]
        per = t // MIX_ISSUE_GROUPS
        for tt in range(group * per, (group + 1) * per):
            for k in range(TOP_K):
                pltpu.make_async_copy(stage.at[s, pl.ds(tt * SUBLANES, SUBLANES)],
                                      _tile_rows(xs_ref, dsm[s, k, tt] + zero), sem.at[s]).start()

    def drain_rows(s):
        for _ in range(TOP_K):
            pltpu.make_async_copy(stage.at[s], xs_ref.at[pl.ds(0, t * SUBLANES)], sem.at[s]).wait()

    def tile(s, milestone=None):
        _mix_tile_body(xp_ref, u_ref, ypre_ref, x_ref, wz_ref, wgt_ref, sng_ref, poolw_ref, pscale_ref, wbp_ref, wbs_ref, wout_ref,
                       g1_ref, ng_ref, sh2_ref, sc2_ref, g2_ref, wr_ref, rb_ref, wsgu_ref, wsd_ref,
                       base_ref, ws_ref, cnt_ref, ptail, run, stage, dvm, dsm, sem_s, i, s, cap, milestone)

    @pl.when(i == 0)
    def _():
        tile(0)

    groups = iter(range(1, MIX_ISSUE_GROUPS))

    def milestone(value):
        if value is BEFORE_STAGING:
            assert next(groups, None) is None
            @pl.when(i >= 2)
            def _():
                drain_rows(slot)
        else:
            group = next(groups)
            issue_rows(1 - slot, group, value if group >= MIX_ISSUE_GROUPS // 2 else None)

    def wait_slot_rows(s):
        pltpu.make_async_copy(dvm, dsm.at[s], sem_s).wait()

    @pl.when((i >= 1) & (i < n_tiles))
    def _():
        wait_slot_rows(1 - slot)
        issue_rows(1 - slot, 0)
        tile(slot, milestone)

    @pl.when(i == n_tiles)
    def _():
        wait_slot_rows(1 - slot)
        for group in range(MIX_ISSUE_GROUPS):
            issue_rows(1 - slot, group)
        if n_tiles >= 2:
            drain_rows(slot)
        drain_rows(1 - slot)


def _mix_tile_body(xp_ref, u_ref, ypre_ref, x_ref, wz_ref, wgt_ref, sng_ref, poolw_ref, pscale_ref, wbp_ref, wbs_ref, wout_ref,
                   g1_ref, ng_ref, sh2_ref, sc2_ref, g2_ref, wr_ref, rb_ref, wsgu_ref, wsd_ref,
                   base_ref, ws_ref, cnt_ref, ptail, run, stage, dvm, dsm, sem_s, i, slot, cap, milestone):
    t = x_ref.shape[0]
    if milestone is None:
        milestone = lambda value: None

    xp = xp_ref[...].astype(F32)
    ext = jnp.concatenate([ptail[...], xp], axis=0)
    ptail[...] = xp[t - 2 * SUBLANES:t]
    pos = (lax.broadcasted_iota(jnp.int32, (t, 1), 0) + (i * t + 1)).astype(F32)
    pooled = []
    for gi, w in enumerate(POOL_WINDOWS):
        sl = slice(gi * POOL_GDIM, (gi + 1) * POOL_GDIM)
        e = ext[:, sl]
        s = e
        span = 1
        while span < w:
            s = s + pltpu.roll(s, span, axis=0)
            span *= 2
        win = s[2 * SUBLANES:]
        mean = win / jnp.minimum(pos, float(w))
        pg = (mean - xp[:, sl]).astype(BF16)
        mixed = jnp.dot(pg, poolw_ref[gi], preferred_element_type=F32)
        pooled.append(mixed * pscale_ref[:, sl])
        milestone(pooled[-1])
    pooled = jnp.concatenate(pooled, axis=1).astype(BF16)
    y_pool = jnp.dot(pooled, wbp_ref[...], preferred_element_type=F32)
    milestone(y_pool)
    ub = u_ref[...]
    z = jnp.dot(ub, wz_ref[...], preferred_element_type=F32)
    milestone(z)
    yn = []
    for g in range(N_GROUPS):
        sl = slice(g * GROUP_W, (g + 1) * GROUP_W)
        yg = ypre_ref[:, sl].astype(F32) * _silu(z[:, sl])
        inv_g = lax.rsqrt(jnp.mean(yg * yg, axis=-1, keepdims=True) + EPS)
        yn.append((yg * inv_g * sng_ref[:, sl]).astype(BF16))
        milestone(inv_g)
    y_ssd = jnp.dot(jnp.concatenate(yn, axis=1), wbs_ref[...], preferred_element_type=F32)
    milestone(y_ssd)
    gates = jnp.dot(ub, wgt_ref[...], preferred_element_type=F32)
    milestone(gates)
    g_ssd = jax.nn.sigmoid(gates[:, :D_MODEL])
    g_pool = jax.nn.sigmoid(gates[:, D_MODEL:])
    mixed = (g_ssd * y_ssd + g_pool * y_pool).astype(BF16)
    h = x_ref[...] + g1_ref[...] * jnp.dot(mixed, wout_ref[...], preferred_element_type=F32)
    milestone(h)

    inv = lax.rsqrt(jnp.mean(h * h, axis=-1, keepdims=True) + EPS)
    milestone(inv)
    u2 = h * inv * ng_ref[...]
    u2 = u2 * (1.0 + sc2_ref[...]) + sh2_ref[...]
    u2b = u2.astype(BF16)
    packed = [_pack_bf16_pair(u2[:, j * LANES:(j + 1) * LANES],
                              u2[:, (j + PACK_TILES) * LANES:(j + PACK_TILES + 1) * LANES])
              for j in range(PACK_TILES)]

    hs = jnp.dot(u2b, wsgu_ref[...], preferred_element_type=F32)
    milestone(hs)
    act = (_silu(hs[:, :D_SHARED]) * hs[:, D_SHARED:]).astype(BF16)
    shared = jnp.dot(act, wsd_ref[...], preferred_element_type=F32)
    milestone(shared)
    base_ref[...] = h + g2_ref[...] * shared

    logits = jnp.dot(u2b, wr_ref[...], preferred_element_type=F32)
    scores = jax.nn.sigmoid(logits)
    milestone(scores)
    choice = scores + rb_ref[...]
    lane = lax.broadcasted_iota(jnp.int32, (t, N_EXPERTS), 1)
    lane_f = lane.astype(F32)
    lane_grp = lane // EXPERTS_PER_GROUP
    neg = -jnp.inf
    gscore = []
    for g in range(N_EGROUPS):
        vg = jnp.where(lane_grp == g, choice, neg)
        m1 = jnp.max(vg, axis=-1, keepdims=True)
        i1 = jnp.min(jnp.where(vg == m1, lane_f, float(N_EXPERTS)), axis=-1, keepdims=True)
        m2 = jnp.max(jnp.where(lane_f == i1, neg, vg), axis=-1, keepdims=True)
        gscore.append(m1 + m2)
        milestone(gscore[-1])
    gmask = jnp.zeros((t, N_EXPERTS), jnp.bool_)
    for g in range(N_EGROUPS):
        rank = jnp.zeros((t, 1), F32)
        for g2 in range(N_EGROUPS):
            if g2 == g:
                continue
            better = (gscore[g2] > gscore[g]) | ((gscore[g2] == gscore[g]) & (g2 < g))
            rank = rank + better.astype(F32)
        gmask = gmask | ((rank < float(TOPK_GROUPS)) & (lane_grp == g))
    work = jnp.where(gmask, choice, neg)
    sel = jnp.zeros((t, N_EXPERTS), jnp.bool_)
    onehots, idxs, sks = [], [], []
    for k in range(TOP_K):
        m = jnp.max(work, axis=-1, keepdims=True)
        idx = jnp.min(jnp.where(work == m, lane_f, float(N_EXPERTS)), axis=-1, keepdims=True)
        oh = lane_f == idx
        onehots.append(oh)
        idxs.append(idx)
        sks.append(jnp.sum(jnp.where(oh, scores, 0.0), axis=-1, keepdims=True))
        if k == TOP_K // 2 - 1:
            milestone(sks[-1])
        sel = sel | oh
        work = jnp.where(oh, neg, work)
    denom = sks[0]
    for k in range(1, TOP_K):
        denom = denom + sks[k]
    milestone(denom)

    ri = lax.broadcasted_iota(jnp.int32, (t, t), 0)
    ci = lax.broadcasted_iota(jnp.int32, (t, t), 1)
    before = (ri > ci).astype(BF16)
    sel_f = jnp.where(sel, 1.0, 0.0)
    sel_b = sel_f.astype(BF16)
    pos_tile = jnp.dot(before, sel_b, preferred_element_type=F32) + run[...]
    er = lax.broadcasted_iota(jnp.int32, (N_EXPERTS, N_EXPERTS), 0)
    ec = lax.broadcasted_iota(jnp.int32, (N_EXPERTS, N_EXPERTS), 1)
    rank_tile = jnp.dot(sel_b, (er < ec).astype(BF16), preferred_element_type=F32)
    lane128 = lax.broadcasted_iota(jnp.int32, (t, LANES), 1)
    k_iota = lax.broadcasted_iota(jnp.int32, (t, TOP_K), 1).astype(F32)
    dest = jnp.zeros((t, LANES), F32)
    w_sorted = jnp.zeros((t, TOP_K), F32)
    for k in range(TOP_K):
        pk = jnp.sum(jnp.where(onehots[k], pos_tile, 0.0), axis=-1, keepdims=True)
        rk = jnp.sum(jnp.where(onehots[k], rank_tile, 0.0), axis=-1, keepdims=True)
        dest = jnp.where(lane128 == k, idxs[k] * float(cap) + pk, dest)
        w_sorted = jnp.where(k_iota == rk, sks[k] / denom * ROUTED_SCALE, w_sorted)
    ws_ref[...] = w_sorted
    total = run[...] + jnp.sum(sel_f, axis=0, keepdims=True)
    run[...] = total
    cnt_ref[...] = total

    milestone(BEFORE_STAGING)
    for j in range(PACK_TILES):
        stage[slot, pl.ds(j, t, stride=SUBLANES), :] = packed[j]
    tok = lax.broadcasted_iota(jnp.int32, (t, LANES), 0) + i * t
    stage[slot, pl.ds(TOK_ROW, t, stride=SUBLANES), :] = tok.astype(jnp.uint32)
    bit = jnp.where(sel, jnp.left_shift(1, lane & 15), 0).astype(F32)
    words = []
    for q in range(N_EXPERTS // 16):
        part = jnp.sum(jnp.where((lane >> 4) == q, bit, 0.0), axis=-1, keepdims=True)
        words.append(part.astype(jnp.int32).astype(jnp.uint32))
    for w, row in enumerate(MASK_ROWS):
        word = words[2 * w] | (words[2 * w + 1] << 16)
        stage[slot, pl.ds(row, t, stride=SUBLANES), :] = jnp.broadcast_to(word, (t, LANES))

    dvm[...] = dest.T[0:TOP_K, :].astype(jnp.int32)
    pltpu.make_async_copy(dvm, dsm.at[slot], sem_s).start()


def _mix(proj, u, ypre, x2, w_z, w_gates, ssd_ng, pool_w, pool_scale, w_br_pool, w_br_ssd, w_out, g1, ng, sh2, sc2,
         g2, w_router, router_bias, ws_gu, ws_down, cap):
    n = x2.shape[0]
    t = 256
    n_tiles = n // t
    xp_blk = (D_SSD + 2 * D_BC) // D_MODEL
    const = lambda shape: pl.BlockSpec(shape, lambda i: (0,) * len(shape), pipeline_mode=pl.Buffered(1))
    tile = lambda i: jnp.minimum(i, n_tiles - 1)
    row = lambda i: (tile(i), 0)
    return pl.pallas_call(
        functools.partial(_mix_kernel, n_tiles=n_tiles, cap=cap),
        grid=(n_tiles + 1,),
        in_specs=[pl.BlockSpec((t, D_MODEL), lambda i: (tile(i), xp_blk)),
                  pl.BlockSpec((t, D_MODEL), row),
                  pl.BlockSpec((t, D_SSD), row),
                  pl.BlockSpec((t, D_MODEL), row),
                  const((D_MODEL, D_SSD)),
                  const((D_MODEL, 2 * D_MODEL)),
                  _full((1, D_SSD)),
                  const((len(POOL_WINDOWS), POOL_GDIM, POOL_GDIM)),
                  _full((1, D_MODEL)),
                  const((D_MODEL, D_MODEL)),
                  const((D_SSD, D_MODEL)),
                  const((D_MODEL, D_MODEL)),
                  _full((1, D_MODEL)), _full((1, D_MODEL)), _full((1, D_MODEL)), _full((1, D_MODEL)),
                  _full((1, D_MODEL)),
                  _full((D_MODEL, N_EXPERTS)),
                  _full((1, N_EXPERTS)),
                  const((D_MODEL, 2 * D_SHARED)),
                  const((D_SHARED, D_MODEL)),
                  pl.BlockSpec(memory_space=pltpu.SMEM)],
        out_specs=[pl.BlockSpec((t, D_MODEL), row),
                   pl.BlockSpec((t, TOP_K), row),
                   _full((1, N_EXPERTS)),
                   pl.BlockSpec(memory_space=pl.ANY)],
        out_shape=[jax.ShapeDtypeStruct((n, D_MODEL), F32),
                   jax.ShapeDtypeStruct((n, TOP_K), F32),
                   jax.ShapeDtypeStruct((1, N_EXPERTS), F32),
                   jax.ShapeDtypeStruct((N_EXPERTS * cap * SUBLANES, LANES), jnp.uint32)],
        scratch_shapes=[pltpu.VMEM((2 * SUBLANES, D_MODEL), F32),
                        pltpu.VMEM((1, N_EXPERTS), F32),
                        pltpu.VMEM((2, t * SUBLANES, LANES), jnp.uint32),
                        pltpu.VMEM((TOP_K, t), jnp.int32),
                        pltpu.SMEM((2, TOP_K, t), jnp.int32),
                        pltpu.SemaphoreType.DMA((2,)),
                        pltpu.SemaphoreType.DMA(())],
        compiler_params=_cparams(),
        name="mix",
    )(proj, u, ypre, x2, w_z, w_gates, ssd_ng, pool_w, pool_scale, w_br_pool, w_br_ssd, w_out, g1, ng, sh2, sc2, g2,
      w_router, router_bias, ws_gu, ws_down, jnp.zeros((1,), jnp.int32))


def _padfill_kernel(cnt_ref, xs_in_ref, xs_ref, padbuf, sem, *, cap):
    del xs_in_ref
    sub = lax.broadcasted_iota(jnp.int32, padbuf.shape, 0) & (SUBLANES - 1)
    padbuf[...] = jnp.where(sub == TOK_ROW, jnp.uint32(PAD_ID), jnp.uint32(0))
    sizes = [1 << bit for bit in range(EXPERT_BLOCK.bit_length() - 1)]

    def pad_copies(e, fn):
        c = cnt_ref[e]
        n_pad = (EXPERT_BLOCK - (c & (EXPERT_BLOCK - 1))) & (EXPERT_BLOCK - 1)
        first = e * cap + c
        for size in sizes:
            @pl.when((n_pad & size) != 0)
            def _():
                start = first + (n_pad & (size - 1))
                dst = xs_ref.at[pl.ds(pl.multiple_of(start * SUBLANES, SUBLANES), size * SUBLANES)]
                fn(pltpu.make_async_copy(padbuf.at[pl.ds(0, size * SUBLANES)], dst, sem))

    def issue(e, carry):
        pad_copies(e, lambda cp: cp.start())
        return carry

    lax.fori_loop(0, N_EXPERTS, issue, 0)

    def drain(e, carry):
        pad_copies(e, lambda cp: cp.wait())
        return carry

    lax.fori_loop(0, N_EXPERTS, drain, 0)


def _padfill(counts, xs, cap):
    return pl.pallas_call(
        functools.partial(_padfill_kernel, cap=cap),
        grid_spec=pltpu.PrefetchScalarGridSpec(
            num_scalar_prefetch=1,
            grid=(1,),
            in_specs=[pl.BlockSpec(memory_space=pl.ANY)],
            out_specs=pl.BlockSpec(memory_space=pl.ANY),
            scratch_shapes=[pltpu.VMEM((EXPERT_BLOCK // 2 * SUBLANES, LANES), jnp.uint32),
                            pltpu.SemaphoreType.DMA(())]),
        out_shape=jax.ShapeDtypeStruct(xs.shape, xs.dtype),
        input_output_aliases={1: 0},
        compiler_params=_cparams(),
        name="padfill",
    )(counts, xs)


def _expert_kernel(be_ref, br_ref, nb_ref, x_ref, wg_ref, wu_ref, wd_ref, ytok_ref, ybuf0, ybuf1, idv, ids0, ids1,
                   wgb, wub, wdb, sem, sem_ids, *, n_ids):
    del br_ref
    m = EXPERT_BLOCK
    b = pl.program_id(0)
    nb = nb_ref[0]
    ybufs, idss = (ybuf0, ybuf1), (ids0, ids1)
    n_phases = m // LANES
    half = D_MODEL // 2

    @pl.when((b == 0) | (be_ref[b] != be_ref[jnp.maximum(b - 1, 0)]))
    def _():
        wgb[...] = wg_ref[0].astype(BF16)
        wub[...] = wu_ref[0].astype(BF16)
        wdb[...] = wd_ref[0].astype(BF16)

    def half_tile(ref, unit):
        return ref.at[pl.ds(pl.multiple_of(unit * PACK_TILES, PACK_TILES), PACK_TILES)]

    def issue(s, a0):
        for col in range(LANES):
            pltpu.make_async_copy(half_tile(ybufs[s], a0 * LANES + col),
                                  half_tile(ytok_ref, idss[s][a0, col]), sem.at[s]).start()

    def step(s, prev):
        tokrep = x_ref[pl.ds(TOK_ROW, m, stride=SUBLANES), :]
        e = be_ref[b]
        below_lo = jnp.where(e >= 32, -1, (1 << jnp.minimum(e, 31)) - 1).astype(jnp.uint32)
        below_hi = jnp.where(e >= 32, (1 << jnp.maximum(e - 32, 0)) - 1, 0).astype(jnp.uint32)
        rank = (lax.population_count(x_ref[pl.ds(MASK_ROWS[0], m, stride=SUBLANES), :] & below_lo)
                + lax.population_count(x_ref[pl.ds(MASK_ROWS[1], m, stride=SUBLANES), :] & below_hi))
        r = lax.broadcasted_iota(jnp.int32, (m, LANES), 0)
        ln = lax.broadcasted_iota(jnp.int32, (m, LANES), 1)
        n_tok = n_ids // TOP_K
        rank = rank.astype(jnp.int32)
        half_rank = TOP_K // 2
        unit = ((rank % half_rank) * n_tok + tokrep.astype(jnp.int32)) * 2 + rank // half_rank
        idi = jnp.where(tokrep == jnp.uint32(PAD_ID), n_ids + r, unit)
        diag = jnp.where((r & (LANES - 1)) == ln, idi, 0).astype(F32)
        idv[...] = jnp.sum(diag.reshape(m // LANES, LANES, LANES), axis=1).astype(jnp.int32)
        ids_copy = pltpu.make_async_copy(idv, idss[s], sem_ids)
        ids_copy.start()

        halves = [_unpack_bf16_pair(x_ref[pl.ds(j, m, stride=SUBLANES), :]) for j in range(PACK_TILES)]
        x = jnp.concatenate([lo for lo, _ in halves] + [hi for _, hi in halves], axis=1).astype(BF16)
        phases = iter(range(n_phases))
        if prev is not None:
            issue(prev, next(phases))
        gate = jnp.dot(x, wgb[...], preferred_element_type=F32)
        if prev is not None:
            issue(prev, next(phases))
        up = jnp.dot(x, wub[...], preferred_element_type=F32)
        hid = (_silu(gate) * up).astype(BF16)

        def drain(which):
            pltpu.make_async_copy(ybufs[which], ytok_ref.at[pl.ds(0, m * PACK_TILES)], sem.at[which]).wait()

        if prev is not None:
            @pl.when(b >= 2)
            def _():
                drain(s)
        ys = []
        for c0 in (0, half):
            if prev is not None:
                issue(prev, next(phases))
            ys.append(jnp.dot(hid, wdb[:, c0:c0 + half], preferred_element_type=F32))
        for j in range(PACK_TILES):
            sl = slice(j * LANES, (j + 1) * LANES)
            ybufs[s][pl.ds(j, m, stride=PACK_TILES), :] = _pack_bf16_pair(ys[0][:, sl], ys[1][:, sl])
        ids_copy.wait()
        if prev is not None:
            for a0 in phases:
                issue(prev, a0)

            @pl.when(b == nb)
            def _():
                drain(prev)

    @pl.when(b == 0)
    def _():
        step(0, None)

    for s in range(2):
        @pl.when((b >= 1) & (b <= nb) & (lax.rem(b, 2) == s))
        def _():
            step(s, 1 - s)


def _experts(block_e, block_row, nb_used, xs, we_gate, we_up, we_down, n_ids):
    n_blocks = block_e.shape[0]
    m = EXPERT_BLOCK
    rows = m * SUBLANES
    return pl.pallas_call(
        functools.partial(_expert_kernel, n_ids=n_ids),
        grid_spec=pltpu.PrefetchScalarGridSpec(
            num_scalar_prefetch=3,
            grid=(n_blocks,),
            in_specs=[pl.BlockSpec((rows, LANES), lambda b, be, br, nb: (br[b], 0)),
                      pl.BlockSpec((1, D_MODEL, D_EXPERT), lambda b, be, br, nb: (be[b], 0, 0)),
                      pl.BlockSpec((1, D_MODEL, D_EXPERT), lambda b, be, br, nb: (be[b], 0, 0)),
                      pl.BlockSpec((1, D_EXPERT, D_MODEL), lambda b, be, br, nb: (be[b], 0, 0))],
            out_specs=pl.BlockSpec(memory_space=pl.ANY),
            scratch_shapes=[pltpu.VMEM((m * PACK_TILES, LANES), jnp.uint32),
                            pltpu.VMEM((m * PACK_TILES, LANES), jnp.uint32),
                            pltpu.VMEM((m // LANES, LANES), jnp.int32),
                            pltpu.SMEM((m // LANES, LANES), jnp.int32),
                            pltpu.SMEM((m // LANES, LANES), jnp.int32),
                            pltpu.VMEM((D_MODEL, D_EXPERT), BF16),
                            pltpu.VMEM((D_MODEL, D_EXPERT), BF16),
                            pltpu.VMEM((D_EXPERT, D_MODEL), BF16),
                            pltpu.SemaphoreType.DMA((2,)),
                            pltpu.SemaphoreType.DMA(())]),
        out_shape=jax.ShapeDtypeStruct(((n_ids + m) * PACK_TILES, LANES), jnp.uint32),
        compiler_params=_cparams(),
        name="experts",
    )(block_e, block_row, nb_used, xs, we_gate, we_up, we_down)


def _combine_kernel(*refs):
    half_rank = TOP_K // 2
    y_refs = refs[:half_rank]
    w8_ref, base_ref, g2_ref, fg_ref, o_ref = refs[half_rank:]
    t7 = base_ref.shape[0]
    w8 = w8_ref[...]
    parts = [jnp.zeros((t7, LANES), F32) for _ in range(ROW_TILES)]
    for p in range(half_rank):
        for h in range(2):
            wk = w8[:, p + h * half_rank:p + h * half_rank + 1]
            for j in range(PACK_TILES):
                lo, hi = _unpack_bf16_pair(y_refs[p][pl.ds(h * PACK_TILES + j, t7, stride=SUBLANES), :])
                parts[j] = parts[j] + wk * lo
                parts[j + PACK_TILES] = parts[j + PACK_TILES] + wk * hi
    routed = jnp.concatenate(parts, axis=1)
    h = base_ref[...] + g2_ref[...] * routed
    inv = lax.rsqrt(jnp.mean(h * h, axis=-1, keepdims=True) + EPS)
    o_ref[...] = h * inv * fg_ref[...]


def _combine(ytok, w8, base, g2, fg):
    n = base.shape[0]
    t7 = 512
    y_specs = [pl.BlockSpec((t7 * SUBLANES, LANES), functools.partial(lambda i, p: (p * (n // t7) + i, 0), p=p))
               for p in range(TOP_K // 2)]
    return pl.pallas_call(
        _combine_kernel,
        grid=(n // t7,),
        in_specs=y_specs + [
                  pl.BlockSpec((t7, TOP_K), lambda i: (i, 0)),
                  pl.BlockSpec((t7, D_MODEL), lambda i: (i, 0)),
                  _full((1, D_MODEL)), _full((1, D_MODEL))],
        out_specs=pl.BlockSpec((t7, D_MODEL), lambda i: (i, 0)),
        out_shape=jax.ShapeDtypeStruct((n, D_MODEL), F32),
        compiler_params=_cparams(),
        name="combine",
    )(*([ytok] * (TOP_K // 2)), w8, base, g2, fg)


def kernel(x, c, w_ada, b_ada, norm_mix_g, w_in, conv_w, conv_b, dt_bias, A_log, D_skip, ssd_norm_g, pool_w,
           pool_scale, w_br_ssd, w_br_pool, w_out, norm_ffn_g, w_router, router_bias, we_gate, we_up, we_down,
           ws_gate, ws_up, ws_down, final_norm_g):
    bsz, seq, _ = x.shape
    assert bsz == 1 and w_ada.shape[0] == 1
    n = seq
    h2 = x.reshape(n, D_MODEL)

    mod = _ada(c, w_ada[0], b_ada[0])
    sh1, sc1, g1, sh2, sc2, g2 = [mod[:, k * D_MODEL:(k + 1) * D_MODEL] for k in range(6)]

    wi = w_in[0]
    o_xbc, o_dt = D_SSD, D_SSD + D_SSD + 2 * D_BC
    o_pool = o_dt + N_HEADS
    o_gates = o_pool + D_MODEL
    w_dt = jnp.pad(wi[:, o_dt:o_pool], ((0, 0), (0, LANES - N_HEADS))).astype(BF16)
    proj, dt_raw, u = _in_proj(h2, norm_mix_g[0][None], sh1, sc1, wi[:, o_xbc:o_dt].astype(BF16),
                               wi[:, o_pool:o_gates].astype(BF16), w_dt)

    pad_h = lambda v: jnp.pad(v[None], ((0, 0), (0, LANES - N_HEADS)))
    ypre = _ssd(proj, dt_raw, conv_w[0], conv_b[0][None], pad_h(dt_bias[0]), pad_h(A_log[0]),
                jnp.repeat(D_skip[0], HEADDIM)[None])

    ws_gu = jnp.concatenate([ws_gate[0], ws_up[0]], axis=1).astype(BF16)
    m = EXPERT_BLOCK
    cap = -(-n // m) * m
    base, w8, cnt, xs = _mix(
        proj, u, ypre, h2, wi[:, :o_xbc].astype(BF16), wi[:, o_gates:].astype(BF16), ssd_norm_g[0][None],
        pool_w[0].astype(BF16), pool_scale[0][None], w_br_pool[0].astype(BF16),
        w_br_ssd[0].astype(BF16), w_out[0].astype(BF16), g1, norm_ffn_g[0][None], sh2, sc2, g2,
        w_router[0].astype(BF16), router_bias[0][None], ws_gu, ws_down[0].astype(BF16), cap)

    counts = cnt[0].astype(jnp.int32)
    nblk = (counts + m - 1) // m
    bends = jnp.cumsum(nblk)
    bstarts = bends - nblk
    nb_used = bends[-1]
    n_blocks = -(-(n * TOP_K) // m) + N_EXPERTS + 1
    b_eff = jnp.minimum(jnp.arange(n_blocks, dtype=jnp.int32), nb_used - 1)
    done = bends[None, :] <= b_eff[:, None]
    block_e = jnp.sum(done.astype(jnp.int32), axis=1)
    block_row = block_e * (cap // m) + b_eff - jnp.sum(jnp.where(done, nblk[None, :], 0), axis=1)

    xs = _padfill(counts, xs, cap)
    ytok = _experts(block_e, block_row, nb_used.reshape(1), xs,
                    we_gate[0], we_up[0], we_down[0], n * TOP_K)
    out = _combine(ytok, w8, base, g2, final_norm_g[None])
    return out.reshape(bsz, seq, D_MODEL)
```

```python
import functools

import jax
import jax.numpy as jnp
from jax import lax
from jax.experimental import pallas as pl
from jax.experimental.pallas import tpu as pltpu

F32 = jnp.float32
BF16 = jnp.bfloat16
HIGHEST = lax.Precision.HIGHEST

D_MODEL = 1024
D_SSD = 2048
HEADDIM = 64
N_HEADS = 32
N_GROUPS = 8
HEADS_PER_GROUP = N_HEADS // N_GROUPS
D_STATE = 128
CONV_K = 4
CHUNK = 128
SSD_CHUNKS_PER_STEP = 4
GROUP_W = D_SSD // N_GROUPS
D_BC = N_GROUPS * D_STATE
POOL_WINDOWS = (2, 4, 8, 16)
POOL_GDIM = 256
N_EXPERTS = 64
TOP_K = 8
N_EGROUPS = 8
EXPERTS_PER_GROUP = 8
TOPK_GROUPS = 4
D_EXPERT = 256
D_SHARED = 256
ROUTED_SCALE = 2.5
EPS = 1e-6

LANES = 128
SUBLANES = 8
ROW_TILES = D_MODEL // LANES
PACK_TILES = ROW_TILES // 2
PACK_W = PACK_TILES * LANES
EXPERT_BLOCK = 512
MIX_ISSUE_GROUPS = 32
BEFORE_STAGING = "before_staging"

PROJ_W = D_SSD + 2 * D_BC + D_MODEL
PROJ_CHUNK = 512

VMEM_LIMIT = 56 * 1024 * 1024


def _cparams(sem=("arbitrary",)):
    return pltpu.CompilerParams(dimension_semantics=sem, vmem_limit_bytes=VMEM_LIMIT)


def _full(shape):
    nd = len(shape)
    return pl.BlockSpec(shape, lambda *_: (0,) * nd)


def _silu(v):
    h = 0.5 * v
    return h + h * jnp.tanh(h)


def _pack_bf16_pair(lo, hi):
    lo_bits = lax.bitcast_convert_type(lo.astype(BF16).astype(F32), jnp.uint32)
    hi_bits = lax.bitcast_convert_type(hi.astype(BF16).astype(F32), jnp.uint32)
    return (lo_bits >> 16) | (hi_bits & jnp.uint32(0xFFFF0000))


def _unpack_bf16_pair(words):
    return (lax.bitcast_convert_type(words << 16, F32),
            lax.bitcast_convert_type(words & jnp.uint32(0xFFFF0000), F32))


def _ada_kernel(c_ref, w_ref, b_ref, o_ref):
    c = c_ref[...]
    o_ref[...] = jnp.dot(_silu(c), w_ref[...], preferred_element_type=F32, precision=HIGHEST) + b_ref[...]


def _ada(c, w_ada, b_ada):
    n_out = w_ada.shape[1]
    tn = 1536
    c8 = jnp.broadcast_to(c, (SUBLANES, D_MODEL))
    out = pl.pallas_call(
        _ada_kernel,
        grid=(n_out // tn,),
        in_specs=[_full((SUBLANES, D_MODEL)),
                  pl.BlockSpec((D_MODEL, tn), lambda j: (0, j)),
                  pl.BlockSpec((1, tn), lambda j: (0, j))],
        out_specs=pl.BlockSpec((SUBLANES, tn), lambda j: (0, j)),
        out_shape=jax.ShapeDtypeStruct((SUBLANES, n_out), F32),
        compiler_params=_cparams(),
        name="ada",
    )(c8, w_ada, b_ada.reshape(1, n_out))
    return out[0:1]


def _inproj_kernel(x_ref, g_ref, sh_ref, sc_ref, wa_ref, wb_ref, wdt_ref, proj_ref, dt_ref, u_ref):
    x = x_ref[...]
    inv = lax.rsqrt(jnp.mean(x * x, axis=-1, keepdims=True) + EPS)
    u = x * inv * g_ref[...]
    u = u * (1.0 + sc_ref[...]) + sh_ref[...]
    ub = u.astype(BF16)
    u_ref[...] = ub
    wa = wa_ref.shape[1]
    for c0 in range(0, PROJ_W, PROJ_CHUNK):
        w = wa_ref[:, c0:c0 + PROJ_CHUNK] if c0 < wa else wb_ref[:, c0 - wa:c0 - wa + PROJ_CHUNK]
        proj_ref[:, c0:c0 + PROJ_CHUNK] = jnp.dot(ub, w, preferred_element_type=F32).astype(BF16)
    dt_ref[...] = jnp.dot(ub, wdt_ref[...], preferred_element_type=F32)


def _in_proj(x2, g, sh, sc, w_a, w_b, w_dt):
    n = x2.shape[0]
    tm = 512
    return pl.pallas_call(
        _inproj_kernel,
        grid=(n // tm,),
        in_specs=[pl.BlockSpec((tm, D_MODEL), lambda i: (i, 0)),
                  _full((1, D_MODEL)), _full((1, D_MODEL)), _full((1, D_MODEL)),
                  pl.BlockSpec(w_a.shape, lambda i: (0, 0), pipeline_mode=pl.Buffered(1)),
                  pl.BlockSpec(w_b.shape, lambda i: (0, 0), pipeline_mode=pl.Buffered(1)),
                  _full((D_MODEL, LANES))],
        out_specs=[pl.BlockSpec((tm, PROJ_W), lambda i: (i, 0)),
                   pl.BlockSpec((tm, LANES), lambda i: (i, 0)),
                   pl.BlockSpec((tm, D_MODEL), lambda i: (i, 0))],
        out_shape=[jax.ShapeDtypeStruct((n, PROJ_W), BF16),
                   jax.ShapeDtypeStruct((n, LANES), F32),
                   jax.ShapeDtypeStruct((n, D_MODEL), BF16)],
        compiler_params=_cparams(),
        name="in_proj",
    )(x2, g, sh, sc, w_a, w_b, w_dt)


def _conv_silu(cur_ref, ext_ref, w_ref, b_ref, out_ref, width, cw=512):
    t = cur_ref.shape[0]
    for c0 in range(0, width, cw):
        sl = slice(c0, c0 + cw)
        cur = cur_ref[:, sl].astype(F32)
        ext_ref[SUBLANES:, sl] = cur
        acc = cur * w_ref[CONV_K - 1:CONV_K, sl] + b_ref[:, sl]
        for s in range(1, CONV_K):
            acc = acc + ext_ref[pl.ds(SUBLANES - s, t), sl] * w_ref[CONV_K - 1 - s:CONV_K - s, sl]
        out_ref[:, sl] = _silu(acc).astype(out_ref.dtype)
        ext_ref[0:SUBLANES, sl] = cur[t - SUBLANES:t]


def _ssd_kernel(xs_ref, bc_ref, dt_ref, cwx_ref, cwbc_ref, cbx_ref, cbbc_ref, dtb_ref, alog_ref,
                dskip_ref, o_ref, tailx, tailbc, state, xc, bcc):
    q = CHUNK
    pair_w = 2 * HEADDIM

    @pl.when(pl.program_id(0) == 0)
    def _():
        tailx[...] = jnp.zeros_like(tailx)
        tailbc[...] = jnp.zeros_like(tailbc)
        state[...] = jnp.zeros_like(state)

    _conv_silu(xs_ref, tailx, cwx_ref, cbx_ref, xc, D_SSD)
    _conv_silu(bc_ref, tailbc, cwbc_ref, cbbc_ref, bcc, 2 * D_BC)

    for c in range(xs_ref.shape[0] // q):
        rows = slice(c * q, (c + 1) * q)
        v = dt_ref[rows, :] + dtb_ref[...]
        dt = jnp.maximum(v, 0.0) + jnp.log(1.0 + jnp.exp(-jnp.abs(v)))
        a = dt * (-jnp.exp(alog_ref[...]))
        ri = lax.broadcasted_iota(jnp.int32, (q, q), 0)
        ci = lax.broadcasted_iota(jnp.int32, (q, q), 1)
        causal = ri >= ci
        a_cs = jnp.dot(causal.astype(F32), a, preferred_element_type=F32, precision=HIGHEST)
        a_cs_t = a_cs.T
        dt_t = dt.T
        first_half = ci < HEADDIM
        first_half_row = first_half[0:1, :]

        for g in range(N_GROUPS):
            b_g = bcc[rows, g * D_STATE:(g + 1) * D_STATE]
            c_g = bcc[rows, D_BC + g * D_STATE:D_BC + (g + 1) * D_STATE]
            cb = lax.dot_general(c_g, b_g, (((1,), (1,)), ((), ())), preferred_element_type=F32)
            b_t = b_g.astype(F32).T
            st_g = state[g]
            y_off = jnp.dot(c_g, st_g.astype(BF16), preferred_element_type=F32)
            for pi in range(HEADS_PER_GROUP // 2):
                h0 = g * HEADS_PER_GROUP + 2 * pi
                lanes = slice(h0 * HEADDIM, h0 * HEADDIM + pair_w)
                gl = slice(pi * pair_w, (pi + 1) * pair_w)
                ms, ws, cols, lasts = [], [], [], []
                for h in (h0, h0 + 1):
                    col = jnp.broadcast_to(a_cs[:, h:h + 1], (q, q))
                    row = a_cs_t[h:h + 1, :]
                    dtrow = dt_t[h:h + 1, :]
                    decay = jnp.exp(jnp.where(causal, col - row, -jnp.inf))
                    ms.append((cb * decay * dtrow).astype(BF16))
                    last = a_cs_t[h:h + 1, q - 1:q]
                    ws.append((b_t * (jnp.exp(last - row) * dtrow)).astype(BF16))
                    cols.append(col)
                    lasts.append(last)
                xp = xc[rows, lanes]
                zero = jnp.zeros_like(xp)
                x_bd = jnp.concatenate([jnp.where(first_half, xp, zero), jnp.where(first_half, zero, xp)], axis=0)
                y_p = jnp.dot(jnp.concatenate(ms, axis=1), x_bd, preferred_element_type=F32)
                y_p = y_p + y_off[:, gl] * jnp.exp(jnp.where(first_half, cols[0], cols[1]))
                o_ref[rows, lanes] = (y_p + dskip_ref[:, lanes] * xp.astype(F32)).astype(BF16)
                s_new = jnp.dot(jnp.concatenate(ws, axis=1), x_bd, preferred_element_type=F32)
                carry = jnp.exp(jnp.where(first_half_row, lasts[0], lasts[1]))
                state[g, :, gl] = st_g[:, gl] * carry + s_new


def _ssd(proj, dt_raw, conv_w, conv_b, dt_bias, a_log, d_skip):
    n = proj.shape[0]
    t = SSD_CHUNKS_PER_STEP * CHUNK
    return pl.pallas_call(
        _ssd_kernel,
        grid=(n // t,),
        in_specs=[pl.BlockSpec((t, D_SSD), lambda i: (i, 0)),
                  pl.BlockSpec((t, 2 * D_BC), lambda i: (i, 1)),
                  pl.BlockSpec((t, LANES), lambda i: (i, 0)),
                  pl.BlockSpec((CONV_K, D_SSD), lambda i: (0, 0)),
                  pl.BlockSpec((CONV_K, 2 * D_BC), lambda i: (0, 1)),
                  pl.BlockSpec((1, D_SSD), lambda i: (0, 0)),
                  pl.BlockSpec((1, 2 * D_BC), lambda i: (0, 1)),
                  _full((1, LANES)), _full((1, LANES)),
                  _full((1, D_SSD))],
        out_specs=pl.BlockSpec((t, D_SSD), lambda i: (i, 0)),
        out_shape=jax.ShapeDtypeStruct((n, D_SSD), BF16),
        scratch_shapes=[pltpu.VMEM((SUBLANES + t, D_SSD), F32),
                        pltpu.VMEM((SUBLANES + t, 2 * D_BC), F32),
                        pltpu.VMEM((N_GROUPS, D_STATE, GROUP_W), F32),
                        pltpu.VMEM((t, D_SSD), BF16),
                        pltpu.VMEM((t, 2 * D_BC), BF16)],
        compiler_params=_cparams(),
        name="ssd",
    )(proj, proj, dt_raw, conv_w, conv_w, conv_b, conv_b, dt_bias, a_log, d_skip)


PAD_ID = 0xFFFFFFFF
TOK_ROW = PACK_TILES
MASK_ROWS = (PACK_TILES + 1, PACK_TILES + 2)


def _tile_rows(ref, first_row):
    return ref.at[pl.ds(pl.multiple_of(first_row * SUBLANES, SUBLANES), SUBLANES)]


def _mix_kernel(xp_ref, u_ref, ypre_ref, x_ref, wz_ref, wgt_ref, sng_ref, poolw_ref, pscale_ref, wbp_ref, wbs_ref, wout_ref,
                g1_ref, ng_ref, sh2_ref, sc2_ref, g2_ref, wr_ref, rb_ref, wsgu_ref, wsd_ref, zero_ref,
                base_ref, ws_ref, cnt_ref, xs_ref, ptail, run, stage, dvm, dsm, sem, sem_s, *, n_tiles, cap):
    t = x_ref.shape[0]
    i = pl.program_id(0)
    slot = lax.rem(i, 2)

    @pl.when(i == 0)
    def _():
        ptail[...] = jnp.zeros_like(ptail)
        run[...] = jnp.zeros_like(run)
        stage[...] = jnp.zeros_like(stage)

    def issue_rows(s, group, after=None):
        zero = 0
        if after is not None:
            probe = jnp.max(jnp.abs(after[0:SUBLANES, 0:min(LANES, after.shape[1])])).astype(jnp.int32)
            zero = probe * zero_ref[0]
        per = t // MIX_ISSUE_GROUPS
        for tt in range(group * per, (group + 1) * per):
            for k in range(TOP_K):
                pltpu.make_async_copy(stage.at[s, pl.ds(tt * SUBLANES, SUBLANES)],
                                      _tile_rows(xs_ref, dsm[s, k, tt] + zero), sem.at[s]).start()

    def drain_rows(s):
        for _ in range(TOP_K):
            pltpu.make_async_copy(stage.at[s], xs_ref.at[pl.ds(0, t * SUBLANES)], sem.at[s]).wait()

    def tile(s, milestone=None):
        _mix_tile_body(xp_ref, u_ref, ypre_ref, x_ref, wz_ref, wgt_ref, sng_ref, poolw_ref, pscale_ref, wbp_ref, wbs_ref, wout_ref,
                       g1_ref, ng_ref, sh2_ref, sc2_ref, g2_ref, wr_ref, rb_ref, wsgu_ref, wsd_ref,
                       base_ref, ws_ref, cnt_ref, ptail, run, stage, dvm, dsm, sem_s, i, s, cap, milestone)

    @pl.when(i == 0)
    def _():
        tile(0)

    groups = iter(range(1, MIX_ISSUE_GROUPS))

    def milestone(value):
        if value is BEFORE_STAGING:
            assert next(groups, None) is None
            @pl.when(i >= 2)
            def _():
                drain_rows(slot)
        else:
            issue_rows(1 - slot, next(groups), value)

    def wait_slot_rows(s):
        pltpu.make_async_copy(dvm, dsm.at[s], sem_s).wait()

    @pl.when((i >= 1) & (i < n_tiles))
    def _():
        wait_slot_rows(1 - slot)
        issue_rows(1 - slot, 0)
        tile(slot, milestone)

    @pl.when(i == n_tiles)
    def _():
        wait_slot_rows(1 - slot)
        for group in range(MIX_ISSUE_GROUPS):
            issue_rows(1 - slot, group)
        if n_tiles >= 2:
            drain_rows(slot)
        drain_rows(1 - slot)


def _mix_tile_body(xp_ref, u_ref, ypre_ref, x_ref, wz_ref, wgt_ref, sng_ref, poolw_ref, pscale_ref, wbp_ref, wbs_ref, wout_ref,
                   g1_ref, ng_ref, sh2_ref, sc2_ref, g2_ref, wr_ref, rb_ref, wsgu_ref, wsd_ref,
                   base_ref, ws_ref, cnt_ref, ptail, run, stage, dvm, dsm, sem_s, i, slot, cap, milestone):
    t = x_ref.shape[0]
    if milestone is None:
        milestone = lambda value: None

    xp = xp_ref[...].astype(F32)
    ext = jnp.concatenate([ptail[...], xp], axis=0)
    ptail[...] = xp[t - 2 * SUBLANES:t]
    pos = (lax.broadcasted_iota(jnp.int32, (t, 1), 0) + (i * t + 1)).astype(F32)
    pooled = []
    for gi, w in enumerate(POOL_WINDOWS):
        sl = slice(gi * POOL_GDIM, (gi + 1) * POOL_GDIM)
        e = ext[:, sl]
        s = e
        span = 1
        while span < w:
            s = s + pltpu.roll(s, span, axis=0)
            span *= 2
        win = s[2 * SUBLANES:]
        mean = win / jnp.minimum(pos, float(w))
        pg = (mean - xp[:, sl]).astype(BF16)
        mixed = jnp.dot(pg, poolw_ref[gi], preferred_element_type=F32)
        pooled.append(mixed * pscale_ref[:, sl])
        milestone(pooled[-1])
    pooled = jnp.concatenate(pooled, axis=1).astype(BF16)
    y_pool = jnp.dot(pooled, wbp_ref[...], preferred_element_type=F32)
    milestone(y_pool)
    ub = u_ref[...]
    z = jnp.dot(ub, wz_ref[...], preferred_element_type=F32)
    milestone(z)
    yn = []
    for g in range(N_GROUPS):
        sl = slice(g * GROUP_W, (g + 1) * GROUP_W)
        yg = ypre_ref[:, sl].astype(F32) * _silu(z[:, sl])
        inv_g = lax.rsqrt(jnp.mean(yg * yg, axis=-1, keepdims=True) + EPS)
        yn.append((yg * inv_g * sng_ref[:, sl]).astype(BF16))
        milestone(inv_g)
    y_ssd = jnp.dot(jnp.concatenate(yn, axis=1), wbs_ref[...], preferred_element_type=F32)
    milestone(y_ssd)
    gates = jnp.dot(ub, wgt_ref[...], preferred_element_type=F32)
    milestone(gates)
    g_ssd = jax.nn.sigmoid(gates[:, :D_MODEL])
    g_pool = jax.nn.sigmoid(gates[:, D_MODEL:])
    mixed = (g_ssd * y_ssd + g_pool * y_pool).astype(BF16)
    h = x_ref[...] + g1_ref[...] * jnp.dot(mixed, wout_ref[...], preferred_element_type=F32)
    milestone(h)

    inv = lax.rsqrt(jnp.mean(h * h, axis=-1, keepdims=True) + EPS)
    milestone(inv)
    u2 = h * inv * ng_ref[...]
    u2 = u2 * (1.0 + sc2_ref[...]) + sh2_ref[...]
    u2b = u2.astype(BF16)
    packed = [_pack_bf16_pair(u2[:, j * LANES:(j + 1) * LANES],
                              u2[:, (j + PACK_TILES) * LANES:(j + PACK_TILES + 1) * LANES])
              for j in range(PACK_TILES)]

    hs = jnp.dot(u2b, wsgu_ref[...], preferred_element_type=F32)
    milestone(hs)
    act = (_silu(hs[:, :D_SHARED]) * hs[:, D_SHARED:]).astype(BF16)
    shared = jnp.dot(act, wsd_ref[...], preferred_element_type=F32)
    milestone(shared)
    base_ref[...] = h + g2_ref[...] * shared

    logits = jnp.dot(u2b, wr_ref[...], preferred_element_type=F32)
    scores = jax.nn.sigmoid(logits)
    milestone(scores)
    choice = scores + rb_ref[...]
    lane = lax.broadcasted_iota(jnp.int32, (t, N_EXPERTS), 1)
    lane_f = lane.astype(F32)
    lane_grp = lane // EXPERTS_PER_GROUP
    neg = -jnp.inf
    gscore = []
    for g in range(N_EGROUPS):
        vg = jnp.where(lane_grp == g, choice, neg)
        m1 = jnp.max(vg, axis=-1, keepdims=True)
        i1 = jnp.min(jnp.where(vg == m1, lane_f, float(N_EXPERTS)), axis=-1, keepdims=True)
        m2 = jnp.max(jnp.where(lane_f == i1, neg, vg), axis=-1, keepdims=True)
        gscore.append(m1 + m2)
        milestone(gscore[-1])
    gmask = jnp.zeros((t, N_EXPERTS), jnp.bool_)
    for g in range(N_EGROUPS):
        rank = jnp.zeros((t, 1), F32)
        for g2 in range(N_EGROUPS):
            if g2 == g:
                continue
            better = (gscore[g2] > gscore[g]) | ((gscore[g2] == gscore[g]) & (g2 < g))
            rank = rank + better.astype(F32)
        gmask = gmask | ((rank < float(TOPK_GROUPS)) & (lane_grp == g))
    work = jnp.where(gmask, choice, neg)
    sel = jnp.zeros((t, N_EXPERTS), jnp.bool_)
    onehots, idxs, sks = [], [], []
    for k in range(TOP_K):
        m = jnp.max(work, axis=-1, keepdims=True)
        idx = jnp.min(jnp.where(work == m, lane_f, float(N_EXPERTS)), axis=-1, keepdims=True)
        oh = lane_f == idx
        onehots.append(oh)
        idxs.append(idx)
        sks.append(jnp.sum(jnp.where(oh, scores, 0.0), axis=-1, keepdims=True))
        if k == TOP_K // 2 - 1:
            milestone(sks[-1])
        sel = sel | oh
        work = jnp.where(oh, neg, work)
    denom = sks[0]
    for k in range(1, TOP_K):
        denom = denom + sks[k]
    milestone(denom)

    ri = lax.broadcasted_iota(jnp.int32, (t, t), 0)
    ci = lax.broadcasted_iota(jnp.int32, (t, t), 1)
    before = (ri > ci).astype(BF16)
    sel_f = jnp.where(sel, 1.0, 0.0)
    sel_b = sel_f.astype(BF16)
    pos_tile = jnp.dot(before, sel_b, preferred_element_type=F32) + run[...]
    er = lax.broadcasted_iota(jnp.int32, (N_EXPERTS, N_EXPERTS), 0)
    ec = lax.broadcasted_iota(jnp.int32, (N_EXPERTS, N_EXPERTS), 1)
    rank_tile = jnp.dot(sel_b, (er < ec).astype(BF16), preferred_element_type=F32)
    lane128 = lax.broadcasted_iota(jnp.int32, (t, LANES), 1)
    k_iota = lax.broadcasted_iota(jnp.int32, (t, TOP_K), 1).astype(F32)
    dest = jnp.zeros((t, LANES), F32)
    w_sorted = jnp.zeros((t, TOP_K), F32)
    for k in range(TOP_K):
        pk = jnp.sum(jnp.where(onehots[k], pos_tile, 0.0), axis=-1, keepdims=True)
        rk = jnp.sum(jnp.where(onehots[k], rank_tile, 0.0), axis=-1, keepdims=True)
        dest = jnp.where(lane128 == k, idxs[k] * float(cap) + pk, dest)
        w_sorted = jnp.where(k_iota == rk, sks[k] / denom * ROUTED_SCALE, w_sorted)
    ws_ref[...] = w_sorted
    total = run[...] + jnp.sum(sel_f, axis=0, keepdims=True)
    run[...] = total
    cnt_ref[...] = total

    milestone(BEFORE_STAGING)
    for j in range(PACK_TILES):
        stage[slot, pl.ds(j, t, stride=SUBLANES), :] = packed[j]
    tok = lax.broadcasted_iota(jnp.int32, (t, LANES), 0) + i * t
    stage[slot, pl.ds(TOK_ROW, t, stride=SUBLANES), :] = tok.astype(jnp.uint32)
    bit = jnp.where(sel, jnp.left_shift(1, lane & 15), 0).astype(F32)
    words = []
    for q in range(N_EXPERTS // 16):
        part = jnp.sum(jnp.where((lane >> 4) == q, bit, 0.0), axis=-1, keepdims=True)
        words.append(part.astype(jnp.int32).astype(jnp.uint32))
    for w, row in enumerate(MASK_ROWS):
        word = words[2 * w] | (words[2 * w + 1] << 16)
        stage[slot, pl.ds(row, t, stride=SUBLANES), :] = jnp.broadcast_to(word, (t, LANES))

    dvm[...] = dest.T[0:TOP_K, :].astype(jnp.int32)
    pltpu.make_async_copy(dvm, dsm.at[slot], sem_s).start()


def _mix(proj, u, ypre, x2, w_z, w_gates, ssd_ng, pool_w, pool_scale, w_br_pool, w_br_ssd, w_out, g1, ng, sh2, sc2,
         g2, w_router, router_bias, ws_gu, ws_down, cap):
    n = x2.shape[0]
    t = 256
    n_tiles = n // t
    xp_blk = (D_SSD + 2 * D_BC) // D_MODEL
    const = lambda shape: pl.BlockSpec(shape, lambda i: (0,) * len(shape), pipeline_mode=pl.Buffered(1))
    tile = lambda i: jnp.minimum(i, n_tiles - 1)
    row = lambda i: (tile(i), 0)
    return pl.pallas_call(
        functools.partial(_mix_kernel, n_tiles=n_tiles, cap=cap),
        grid=(n_tiles + 1,),
        in_specs=[pl.BlockSpec((t, D_MODEL), lambda i: (tile(i), xp_blk)),
                  pl.BlockSpec((t, D_MODEL), row),
                  pl.BlockSpec((t, D_SSD), row),
                  pl.BlockSpec((t, D_MODEL), row),
                  const((D_MODEL, D_SSD)),
                  const((D_MODEL, 2 * D_MODEL)),
                  _full((1, D_SSD)),
                  const((len(POOL_WINDOWS), POOL_GDIM, POOL_GDIM)),
                  _full((1, D_MODEL)),
                  const((D_MODEL, D_MODEL)),
                  const((D_SSD, D_MODEL)),
                  const((D_MODEL, D_MODEL)),
                  _full((1, D_MODEL)), _full((1, D_MODEL)), _full((1, D_MODEL)), _full((1, D_MODEL)),
                  _full((1, D_MODEL)),
                  _full((D_MODEL, N_EXPERTS)),
                  _full((1, N_EXPERTS)),
                  const((D_MODEL, 2 * D_SHARED)),
                  const((D_SHARED, D_MODEL)),
                  pl.BlockSpec(memory_space=pltpu.SMEM)],
        out_specs=[pl.BlockSpec((t, D_MODEL), row),
                   pl.BlockSpec((t, TOP_K), row),
                   _full((1, N_EXPERTS)),
                   pl.BlockSpec(memory_space=pl.ANY)],
        out_shape=[jax.ShapeDtypeStruct((n, D_MODEL), F32),
                   jax.ShapeDtypeStruct((n, TOP_K), F32),
                   jax.ShapeDtypeStruct((1, N_EXPERTS), F32),
                   jax.ShapeDtypeStruct((N_EXPERTS * cap * SUBLANES, LANES), jnp.uint32)],
        scratch_shapes=[pltpu.VMEM((2 * SUBLANES, D_MODEL), F32),
                        pltpu.VMEM((1, N_EXPERTS), F32),
                        pltpu.VMEM((2, t * SUBLANES, LANES), jnp.uint32),
                        pltpu.VMEM((TOP_K, t), jnp.int32),
                        pltpu.SMEM((2, TOP_K, t), jnp.int32),
                        pltpu.SemaphoreType.DMA((2,)),
                        pltpu.SemaphoreType.DMA(())],
        compiler_params=_cparams(),
        name="mix",
    )(proj, u, ypre, x2, w_z, w_gates, ssd_ng, pool_w, pool_scale, w_br_pool, w_br_ssd, w_out, g1, ng, sh2, sc2, g2,
      w_router, router_bias, ws_gu, ws_down, jnp.zeros((1,), jnp.int32))


def _padfill_kernel(cnt_ref, xs_in_ref, xs_ref, padbuf, sem, *, cap):
    del xs_in_ref
    sub = lax.broadcasted_iota(jnp.int32, padbuf.shape, 0) & (SUBLANES - 1)
    padbuf[...] = jnp.where(sub == TOK_ROW, jnp.uint32(PAD_ID), jnp.uint32(0))
    sizes = [1 << bit for bit in range(EXPERT_BLOCK.bit_length() - 1)]

    def pad_copies(e, fn):
        c = cnt_ref[e]
        n_pad = (EXPERT_BLOCK - (c & (EXPERT_BLOCK - 1))) & (EXPERT_BLOCK - 1)
        first = e * cap + c
        for size in sizes:
            @pl.when((n_pad & size) != 0)
            def _():
                start = first + (n_pad & (size - 1))
                dst = xs_ref.at[pl.ds(pl.multiple_of(start * SUBLANES, SUBLANES), size * SUBLANES)]
                fn(pltpu.make_async_copy(padbuf.at[pl.ds(0, size * SUBLANES)], dst, sem))

    def issue(e, carry):
        pad_copies(e, lambda cp: cp.start())
        return carry

    lax.fori_loop(0, N_EXPERTS, issue, 0)

    def drain(e, carry):
        pad_copies(e, lambda cp: cp.wait())
        return carry

    lax.fori_loop(0, N_EXPERTS, drain, 0)


def _padfill(counts, xs, cap):
    return pl.pallas_call(
        functools.partial(_padfill_kernel, cap=cap),
        grid_spec=pltpu.PrefetchScalarGridSpec(
            num_scalar_prefetch=1,
            grid=(1,),
            in_specs=[pl.BlockSpec(memory_space=pl.ANY)],
            out_specs=pl.BlockSpec(memory_space=pl.ANY),
            scratch_shapes=[pltpu.VMEM((EXPERT_BLOCK // 2 * SUBLANES, LANES), jnp.uint32),
                            pltpu.SemaphoreType.DMA(())]),
        out_shape=jax.ShapeDtypeStruct(xs.shape, xs.dtype),
        input_output_aliases={1: 0},
        compiler_params=_cparams(),
        name="padfill",
    )(counts, xs)


def _expert_kernel(be_ref, br_ref, nb_ref, x_ref, wg_ref, wu_ref, wd_ref, ytok_ref, ybuf0, ybuf1, idv, ids0, ids1,
                   wgb, wub, wdb, sem, sem_ids, *, n_ids):
    del br_ref
    m = EXPERT_BLOCK
    b = pl.program_id(0)
    nb = nb_ref[0]
    ybufs, idss = (ybuf0, ybuf1), (ids0, ids1)
    n_phases = m // LANES
    half = D_MODEL // 2

    @pl.when((b == 0) | (be_ref[b] != be_ref[jnp.maximum(b - 1, 0)]))
    def _():
        wgb[...] = wg_ref[0].astype(BF16)
        wub[...] = wu_ref[0].astype(BF16)
        wdb[...] = wd_ref[0].astype(BF16)

    def half_tile(ref, unit):
        return ref.at[pl.ds(pl.multiple_of(unit * PACK_TILES, PACK_TILES), PACK_TILES)]

    def issue(s, a0):
        for col in range(LANES):
            pltpu.make_async_copy(half_tile(ybufs[s], a0 * LANES + col),
                                  half_tile(ytok_ref, idss[s][a0, col]), sem.at[s]).start()

    def step(s, prev):
        tokrep = x_ref[pl.ds(TOK_ROW, m, stride=SUBLANES), :]
        e = be_ref[b]
        below_lo = jnp.where(e >= 32, -1, (1 << jnp.minimum(e, 31)) - 1).astype(jnp.uint32)
        below_hi = jnp.where(e >= 32, (1 << jnp.maximum(e - 32, 0)) - 1, 0).astype(jnp.uint32)
        rank = (lax.population_count(x_ref[pl.ds(MASK_ROWS[0], m, stride=SUBLANES), :] & below_lo)
                + lax.population_count(x_ref[pl.ds(MASK_ROWS[1], m, stride=SUBLANES), :] & below_hi))
        r = lax.broadcasted_iota(jnp.int32, (m, LANES), 0)
        ln = lax.broadcasted_iota(jnp.int32, (m, LANES), 1)
        n_tok = n_ids // TOP_K
        rank = rank.astype(jnp.int32)
        half_rank = TOP_K // 2
        unit = ((rank % half_rank) * n_tok + tokrep.astype(jnp.int32)) * 2 + rank // half_rank
        idi = jnp.where(tokrep == jnp.uint32(PAD_ID), n_ids + r, unit)
        diag = jnp.where((r & (LANES - 1)) == ln, idi, 0).astype(F32)
        idv[...] = jnp.sum(diag.reshape(m // LANES, LANES, LANES), axis=1).astype(jnp.int32)
        ids_copy = pltpu.make_async_copy(idv, idss[s], sem_ids)
        ids_copy.start()

        halves = [_unpack_bf16_pair(x_ref[pl.ds(j, m, stride=SUBLANES), :]) for j in range(PACK_TILES)]
        x = jnp.concatenate([lo for lo, _ in halves] + [hi for _, hi in halves], axis=1).astype(BF16)
        phases = iter(range(n_phases))
        if prev is not None:
            issue(prev, next(phases))
        gate = jnp.dot(x, wgb[...], preferred_element_type=F32)
        if prev is not None:
            issue(prev, next(phases))
        up = jnp.dot(x, wub[...], preferred_element_type=F32)
        hid = (_silu(gate) * up).astype(BF16)

        def drain(which):
            pltpu.make_async_copy(ybufs[which], ytok_ref.at[pl.ds(0, m * PACK_TILES)], sem.at[which]).wait()

        if prev is not None:
            @pl.when(b >= 2)
            def _():
                drain(s)
        ys = []
        for c0 in (0, half):
            if prev is not None:
                issue(prev, next(phases))
            ys.append(jnp.dot(hid, wdb[:, c0:c0 + half], preferred_element_type=F32))
        for j in range(PACK_TILES):
            sl = slice(j * LANES, (j + 1) * LANES)
            ybufs[s][pl.ds(j, m, stride=PACK_TILES), :] = _pack_bf16_pair(ys[0][:, sl], ys[1][:, sl])
        ids_copy.wait()
        if prev is not None:
            for a0 in phases:
                issue(prev, a0)

            @pl.when(b == nb)
            def _():
                drain(prev)

    @pl.when(b == 0)
    def _():
        step(0, None)

    for s in range(2):
        @pl.when((b >= 1) & (b <= nb) & (lax.rem(b, 2) == s))
        def _():
            step(s, 1 - s)


def _experts(block_e, block_row, nb_used, xs, we_gate, we_up, we_down, n_ids):
    n_blocks = block_e.shape[0]
    m = EXPERT_BLOCK
    rows = m * SUBLANES
    return pl.pallas_call(
        functools.partial(_expert_kernel, n_ids=n_ids),
        grid_spec=pltpu.PrefetchScalarGridSpec(
            num_scalar_prefetch=3,
            grid=(n_blocks,),
            in_specs=[pl.BlockSpec((rows, LANES), lambda b, be, br, nb: (br[b], 0)),
                      pl.BlockSpec((1, D_MODEL, D_EXPERT), lambda b, be, br, nb: (be[b], 0, 0)),
                      pl.BlockSpec((1, D_MODEL, D_EXPERT), lambda b, be, br, nb: (be[b], 0, 0)),
                      pl.BlockSpec((1, D_EXPERT, D_MODEL), lambda b, be, br, nb: (be[b], 0, 0))],
            out_specs=pl.BlockSpec(memory_space=pl.ANY),
            scratch_shapes=[pltpu.VMEM((m * PACK_TILES, LANES), jnp.uint32),
                            pltpu.VMEM((m * PACK_TILES, LANES), jnp.uint32),
                            pltpu.VMEM((m // LANES, LANES), jnp.int32),
                            pltpu.SMEM((m // LANES, LANES), jnp.int32),
                            pltpu.SMEM((m // LANES, LANES), jnp.int32),
                            pltpu.VMEM((D_MODEL, D_EXPERT), BF16),
                            pltpu.VMEM((D_MODEL, D_EXPERT), BF16),
                            pltpu.VMEM((D_EXPERT, D_MODEL), BF16),
                            pltpu.SemaphoreType.DMA((2,)),
                            pltpu.SemaphoreType.DMA(())]),
        out_shape=jax.ShapeDtypeStruct(((n_ids + m) * PACK_TILES, LANES), jnp.uint32),
        compiler_params=_cparams(),
        name="experts",
    )(block_e, block_row, nb_used, xs, we_gate, we_up, we_down)


def _combine_kernel(*refs):
    half_rank = TOP_K // 2
    y_refs = refs[:half_rank]
    w8_ref, base_ref, g2_ref, fg_ref, o_ref = refs[half_rank:]
    t7 = base_ref.shape[0]
    w8 = w8_ref[...]
    parts = [jnp.zeros((t7, LANES), F32) for _ in range(ROW_TILES)]
    for p in range(half_rank):
        for h in range(2):
            wk = w8[:, p + h * half_rank:p + h * half_rank + 1]
            for j in range(PACK_TILES):
                lo, hi = _unpack_bf16_pair(y_refs[p][pl.ds(h * PACK_TILES + j, t7, stride=SUBLANES), :])
                parts[j] = parts[j] + wk * lo
                parts[j + PACK_TILES] = parts[j + PACK_TILES] + wk * hi
    routed = jnp.concatenate(parts, axis=1)
    h = base_ref[...] + g2_ref[...] * routed
    inv = lax.rsqrt(jnp.mean(h * h, axis=-1, keepdims=True) + EPS)
    o_ref[...] = h * inv * fg_ref[...]


def _combine(ytok, w8, base, g2, fg):
    n = base.shape[0]
    t7 = 512
    y_specs = [pl.BlockSpec((t7 * SUBLANES, LANES), functools.partial(lambda i, p: (p * (n // t7) + i, 0), p=p))
               for p in range(TOP_K // 2)]
    return pl.pallas_call(
        _combine_kernel,
        grid=(n // t7,),
        in_specs=y_specs + [
                  pl.BlockSpec((t7, TOP_K), lambda i: (i, 0)),
                  pl.BlockSpec((t7, D_MODEL), lambda i: (i, 0)),
                  _full((1, D_MODEL)), _full((1, D_MODEL))],
        out_specs=pl.BlockSpec((t7, D_MODEL), lambda i: (i, 0)),
        out_shape=jax.ShapeDtypeStruct((n, D_MODEL), F32),
        compiler_params=_cparams(),
        name="combine",
    )(*([ytok] * (TOP_K // 2)), w8, base, g2, fg)


def kernel(x, c, w_ada, b_ada, norm_mix_g, w_in, conv_w, conv_b, dt_bias, A_log, D_skip, ssd_norm_g, pool_w,
           pool_scale, w_br_ssd, w_br_pool, w_out, norm_ffn_g, w_router, router_bias, we_gate, we_up, we_down,
           ws_gate, ws_up, ws_down, final_norm_g):
    bsz, seq, _ = x.shape
    assert bsz == 1 and w_ada.shape[0] == 1
    n = seq
    h2 = x.reshape(n, D_MODEL)

    mod = _ada(c, w_ada[0], b_ada[0])
    sh1, sc1, g1, sh2, sc2, g2 = [mod[:, k * D_MODEL:(k + 1) * D_MODEL] for k in range(6)]

    wi = w_in[0]
    o_xbc, o_dt = D_SSD, D_SSD + D_SSD + 2 * D_BC
    o_pool = o_dt + N_HEADS
    o_gates = o_pool + D_MODEL
    w_dt = jnp.pad(wi[:, o_dt:o_pool], ((0, 0), (0, LANES - N_HEADS))).astype(BF16)
    proj, dt_raw, u = _in_proj(h2, norm_mix_g[0][None], sh1, sc1, wi[:, o_xbc:o_dt].astype(BF16),
                               wi[:, o_pool:o_gates].astype(BF16), w_dt)

    pad_h = lambda v: jnp.pad(v[None], ((0, 0), (0, LANES - N_HEADS)))
    ypre = _ssd(proj, dt_raw, conv_w[0], conv_b[0][None], pad_h(dt_bias[0]), pad_h(A_log[0]),
                jnp.repeat(D_skip[0], HEADDIM)[None])

    ws_gu = jnp.concatenate([ws_gate[0], ws_up[0]], axis=1).astype(BF16)
    m = EXPERT_BLOCK
    cap = -(-n // m) * m
    base, w8, cnt, xs = _mix(
        proj, u, ypre, h2, wi[:, :o_xbc].astype(BF16), wi[:, o_gates:].astype(BF16), ssd_norm_g[0][None],
        pool_w[0].astype(BF16), pool_scale[0][None], w_br_pool[0].astype(BF16),
        w_br_ssd[0].astype(BF16), w_out[0].astype(BF16), g1, norm_ffn_g[0][None], sh2, sc2, g2,
        w_router[0].astype(BF16), router_bias[0][None], ws_gu, ws_down[0].astype(BF16), cap)

    counts = cnt[0].astype(jnp.int32)
    nblk = (counts + m - 1) // m
    bends = jnp.cumsum(nblk)
    bstarts = bends - nblk
    nb_used = bends[-1]
    n_blocks = -(-(n * TOP_K) // m) + N_EXPERTS + 1
    b_eff = jnp.minimum(jnp.arange(n_blocks, dtype=jnp.int32), nb_used - 1)
    done = bends[None, :] <= b_eff[:, None]
    block_e = jnp.sum(done.astype(jnp.int32), axis=1)
    block_row = block_e * (cap // m) + b_eff - jnp.sum(jnp.where(done, nblk[None, :], 0), axis=1)

    xs = _padfill(counts, xs, cap)
    ytok = _experts(block_e, block_row, nb_used.reshape(1), xs,
                    we_gate[0], we_up[0], we_down[0], n * TOP_K)
    out = _combine(ytok, w8, base, g2, final_norm_g[None])
    return out.reshape(bsz, seq, D_MODEL)
```

```python
import functools

import jax
import jax.numpy as jnp
from jax import lax
from jax.experimental import pallas as pl
from jax.experimental.pallas import tpu as pltpu

F32 = jnp.float32
BF16 = jnp.bfloat16
HIGHEST = lax.Precision.HIGHEST

D_MODEL = 1024
D_SSD = 2048
HEADDIM = 64
N_HEADS = 32
N_GROUPS = 8
HEADS_PER_GROUP = N_HEADS // N_GROUPS
D_STATE = 128
CONV_K = 4
CHUNK = 128
SSD_CHUNKS_PER_STEP = 4
GROUP_W = D_SSD // N_GROUPS
D_BC = N_GROUPS * D_STATE
POOL_WINDOWS = (2, 4, 8, 16)
POOL_GDIM = 256
N_EXPERTS = 64
TOP_K = 8
N_EGROUPS = 8
EXPERTS_PER_GROUP = 8
TOPK_GROUPS = 4
D_EXPERT = 256
D_SHARED = 256
ROUTED_SCALE = 2.5
EPS = 1e-6

LANES = 128
SUBLANES = 8
ROW_TILES = D_MODEL // LANES
PACK_TILES = ROW_TILES // 2
PACK_W = PACK_TILES * LANES
EXPERT_BLOCK = 512
MIX_ISSUE_GROUPS = 32
BEFORE_STAGING = "before_staging"

PROJ_W = D_SSD + 2 * D_BC + D_MODEL
PROJ_CHUNK = 512

VMEM_LIMIT = 56 * 1024 * 1024


def _cparams(sem=("arbitrary",)):
    return pltpu.CompilerParams(dimension_semantics=sem, vmem_limit_bytes=VMEM_LIMIT)


def _full(shape):
    nd = len(shape)
    return pl.BlockSpec(shape, lambda *_: (0,) * nd)


def _silu(v):
    h = 0.5 * v
    return h + h * jnp.tanh(h)


def _pack_bf16_pair(lo, hi):
    lo_bits = lax.bitcast_convert_type(lo.astype(BF16).astype(F32), jnp.uint32)
    hi_bits = lax.bitcast_convert_type(hi.astype(BF16).astype(F32), jnp.uint32)
    return (lo_bits >> 16) | (hi_bits & jnp.uint32(0xFFFF0000))


def _unpack_bf16_pair(words):
    return (lax.bitcast_convert_type(words << 16, F32),
            lax.bitcast_convert_type(words & jnp.uint32(0xFFFF0000), F32))


def _ada_kernel(c_ref, w_ref, b_ref, o_ref):
    c = c_ref[...]
    o_ref[...] = jnp.dot(_silu(c), w_ref[...], preferred_element_type=F32, precision=HIGHEST) + b_ref[...]


def _ada(c, w_ada, b_ada):
    n_out = w_ada.shape[1]
    tn = 1536
    c8 = jnp.broadcast_to(c, (SUBLANES, D_MODEL))
    out = pl.pallas_call(
        _ada_kernel,
        grid=(n_out // tn,),
        in_specs=[_full((SUBLANES, D_MODEL)),
                  pl.BlockSpec((D_MODEL, tn), lambda j: (0, j)),
                  pl.BlockSpec((1, tn), lambda j: (0, j))],
        out_specs=pl.BlockSpec((SUBLANES, tn), lambda j: (0, j)),
        out_shape=jax.ShapeDtypeStruct((SUBLANES, n_out), F32),
        compiler_params=_cparams(),
        name="ada",
    )(c8, w_ada, b_ada.reshape(1, n_out))
    return out[0:1]


def _inproj_kernel(x_ref, g_ref, sh_ref, sc_ref, wa_ref, wb_ref, wdt_ref, proj_ref, dt_ref, u_ref):
    x = x_ref[...]
    inv = lax.rsqrt(jnp.mean(x * x, axis=-1, keepdims=True) + EPS)
    u = x * inv * g_ref[...]
    u = u * (1.0 + sc_ref[...]) + sh_ref[...]
    ub = u.astype(BF16)
    u_ref[...] = ub
    wa = wa_ref.shape[1]
    for c0 in range(0, PROJ_W, PROJ_CHUNK):
        w = wa_ref[:, c0:c0 + PROJ_CHUNK] if c0 < wa else wb_ref[:, c0 - wa:c0 - wa + PROJ_CHUNK]
        proj_ref[:, c0:c0 + PROJ_CHUNK] = jnp.dot(ub, w, preferred_element_type=F32).astype(BF16)
    dt_ref[...] = jnp.dot(ub, wdt_ref[...], preferred_element_type=F32)


def _in_proj(x2, g, sh, sc, w_a, w_b, w_dt):
    n = x2.shape[0]
    tm = 512
    return pl.pallas_call(
        _inproj_kernel,
        grid=(n // tm,),
        in_specs=[pl.BlockSpec((tm, D_MODEL), lambda i: (i, 0)),
                  _full((1, D_MODEL)), _full((1, D_MODEL)), _full((1, D_MODEL)),
                  pl.BlockSpec(w_a.shape, lambda i: (0, 0), pipeline_mode=pl.Buffered(1)),
                  pl.BlockSpec(w_b.shape, lambda i: (0, 0), pipeline_mode=pl.Buffered(1)),
                  _full((D_MODEL, LANES))],
        out_specs=[pl.BlockSpec((tm, PROJ_W), lambda i: (i, 0)),
                   pl.BlockSpec((tm, LANES), lambda i: (i, 0)),
                   pl.BlockSpec((tm, D_MODEL), lambda i: (i, 0))],
        out_shape=[jax.ShapeDtypeStruct((n, PROJ_W), BF16),
                   jax.ShapeDtypeStruct((n, LANES), F32),
                   jax.ShapeDtypeStruct((n, D_MODEL), BF16)],
        compiler_params=_cparams(),
        name="in_proj",
    )(x2, g, sh, sc, w_a, w_b, w_dt)


def _conv_silu(cur_ref, ext_ref, w_ref, b_ref, out_ref, width, cw=512):
    t = cur_ref.shape[0]
    for c0 in range(0, width, cw):
        sl = slice(c0, c0 + cw)
        cur = cur_ref[:, sl].astype(F32)
        ext_ref[SUBLANES:, sl] = cur
        acc = cur * w_ref[CONV_K - 1:CONV_K, sl] + b_ref[:, sl]
        for s in range(1, CONV_K):
            acc = acc + ext_ref[pl.ds(SUBLANES - s, t), sl] * w_ref[CONV_K - 1 - s:CONV_K - s, sl]
        out_ref[:, sl] = _silu(acc).astype(out_ref.dtype)
        ext_ref[0:SUBLANES, sl] = cur[t - SUBLANES:t]


def _ssd_kernel(xs_ref, bc_ref, dt_ref, cwx_ref, cwbc_ref, cbx_ref, cbbc_ref, dtb_ref, alog_ref,
                dskip_ref, o_ref, tailx, tailbc, state, xc, bcc):
    q = CHUNK
    pair_w = 2 * HEADDIM

    @pl.when(pl.program_id(0) == 0)
    def _():
        tailx[...] = jnp.zeros_like(tailx)
        tailbc[...] = jnp.zeros_like(tailbc)
        state[...] = jnp.zeros_like(state)

    _conv_silu(xs_ref, tailx, cwx_ref, cbx_ref, xc, D_SSD)
    _conv_silu(bc_ref, tailbc, cwbc_ref, cbbc_ref, bcc, 2 * D_BC)

    for c in range(xs_ref.shape[0] // q):
        rows = slice(c * q, (c + 1) * q)
        v = dt_ref[rows, :] + dtb_ref[...]
        dt = jnp.maximum(v, 0.0) + jnp.log(1.0 + jnp.exp(-jnp.abs(v)))
        a = dt * (-jnp.exp(alog_ref[...]))
        ri = lax.broadcasted_iota(jnp.int32, (q, q), 0)
        ci = lax.broadcasted_iota(jnp.int32, (q, q), 1)
        causal = ri >= ci
        a_cs = jnp.dot(causal.astype(F32), a, preferred_element_type=F32, precision=HIGHEST)
        a_cs_t = a_cs.T
        dt_t = dt.T
        first_half = ci < HEADDIM
        first_half_row = first_half[0:1, :]

        for g in range(N_GROUPS):
            b_g = bcc[rows, g * D_STATE:(g + 1) * D_STATE]
            c_g = bcc[rows, D_BC + g * D_STATE:D_BC + (g + 1) * D_STATE]
            cb = lax.dot_general(c_g, b_g, (((1,), (1,)), ((), ())), preferred_element_type=F32)
            b_t = b_g.astype(F32).T
            st_g = state[g]
            y_off = jnp.dot(c_g, st_g.astype(BF16), preferred_element_type=F32)
            for pi in range(HEADS_PER_GROUP // 2):
                h0 = g * HEADS_PER_GROUP + 2 * pi
                lanes = slice(h0 * HEADDIM, h0 * HEADDIM + pair_w)
                gl = slice(pi * pair_w, (pi + 1) * pair_w)
                ms, ws, cols, lasts = [], [], [], []
                for h in (h0, h0 + 1):
                    col = jnp.broadcast_to(a_cs[:, h:h + 1], (q, q))
                    row = a_cs_t[h:h + 1, :]
                    dtrow = dt_t[h:h + 1, :]
                    decay = jnp.exp(jnp.where(causal, col - row, -jnp.inf))
                    ms.append((cb * decay * dtrow).astype(BF16))
                    last = a_cs_t[h:h + 1, q - 1:q]
                    ws.append((b_t * (jnp.exp(last - row) * dtrow)).astype(BF16))
                    cols.append(col)
                    lasts.append(last)
                xp = xc[rows, lanes]
                zero = jnp.zeros_like(xp)
                x_bd = jnp.concatenate([jnp.where(first_half, xp, zero), jnp.where(first_half, zero, xp)], axis=0)
                y_p = jnp.dot(jnp.concatenate(ms, axis=1), x_bd, preferred_element_type=F32)
                y_p = y_p + y_off[:, gl] * jnp.exp(jnp.where(first_half, cols[0], cols[1]))
                o_ref[rows, lanes] = (y_p + dskip_ref[:, lanes] * xp.astype(F32)).astype(BF16)
                s_new = jnp.dot(jnp.concatenate(ws, axis=1), x_bd, preferred_element_type=F32)
                carry = jnp.exp(jnp.where(first_half_row, lasts[0], lasts[1]))
                state[g, :, gl] = st_g[:, gl] * carry + s_new


def _ssd(proj, dt_raw, conv_w, conv_b, dt_bias, a_log, d_skip):
    n = proj.shape[0]
    t = SSD_CHUNKS_PER_STEP * CHUNK
    return pl.pallas_call(
        _ssd_kernel,
        grid=(n // t,),
        in_specs=[pl.BlockSpec((t, D_SSD), lambda i: (i, 0)),
                  pl.BlockSpec((t, 2 * D_BC), lambda i: (i, 1)),
                  pl.BlockSpec((t, LANES), lambda i: (i, 0)),
                  pl.BlockSpec((CONV_K, D_SSD), lambda i: (0, 0)),
                  pl.BlockSpec((CONV_K, 2 * D_BC), lambda i: (0, 1)),
                  pl.BlockSpec((1, D_SSD), lambda i: (0, 0)),
                  pl.BlockSpec((1, 2 * D_BC), lambda i: (0, 1)),
                  _full((1, LANES)), _full((1, LANES)),
                  _full((1, D_SSD))],
        out_specs=pl.BlockSpec((t, D_SSD), lambda i: (i, 0)),
        out_shape=jax.ShapeDtypeStruct((n, D_SSD), BF16),
        scratch_shapes=[pltpu.VMEM((SUBLANES + t, D_SSD), F32),
                        pltpu.VMEM((SUBLANES + t, 2 * D_BC), F32),
                        pltpu.VMEM((N_GROUPS, D_STATE, GROUP_W), F32),
                        pltpu.VMEM((t, D_SSD), BF16),
                        pltpu.VMEM((t, 2 * D_BC), BF16)],
        compiler_params=_cparams(),
        name="ssd",
    )(proj, proj, dt_raw, conv_w, conv_w, conv_b, conv_b, dt_bias, a_log, d_skip)


PAD_ID = 0xFFFFFFFF
TOK_ROW = PACK_TILES
MASK_ROWS = (PACK_TILES + 1, PACK_TILES + 2)


def _tile_rows(ref, first_row):
    return ref.at[pl.ds(pl.multiple_of(first_row * SUBLANES, SUBLANES), SUBLANES)]


def _mix_kernel(xp_ref, u_ref, ypre_ref, x_ref, wz_ref, wgt_ref, sng_ref, poolw_ref, pscale_ref, wbp_ref, wbs_ref, wout_ref,
                g1_ref, ng_ref, sh2_ref, sc2_ref, g2_ref, wr_ref, rb_ref, wsgu_ref, wsd_ref, zero_ref,
                base_ref, ws_ref, cnt_ref, xs_ref, ptail, run, stage, dvm, dsm, sem, sem_s, *, n_tiles, cap):
    t = x_ref.shape[0]
    i = pl.program_id(0)
    slot = lax.rem(i, 2)

    @pl.when(i == 0)
    def _():
        ptail[...] = jnp.zeros_like(ptail)
        run[...] = jnp.zeros_like(run)
        stage[...] = jnp.zeros_like(stage)

    def issue_rows(s, group, after=None):
        zero = 0
        if after is not None:
            probe = jnp.max(jnp.abs(after[0:SUBLANES, 0:min(LANES, after.shape[1])])).astype(jnp.int32)
            zero = probe * zero_ref[0]
        per = t // MIX_ISSUE_GROUPS
        for tt in range(group * per, (group + 1) * per):
            for k in range(TOP_K):
                pltpu.make_async_copy(stage.at[s, pl.ds(tt * SUBLANES, SUBLANES)],
                                      _tile_rows(xs_ref, dsm[s, k, tt] + zero), sem.at[s]).start()

    def drain_rows(s):
        for _ in range(TOP_K):
            pltpu.make_async_copy(stage.at[s], xs_ref.at[pl.ds(0, t * SUBLANES)], sem.at[s]).wait()

    def tile(s, milestone=None):
        _mix_tile_body(xp_ref, u_ref, ypre_ref, x_ref, wz_ref, wgt_ref, sng_ref, poolw_ref, pscale_ref, wbp_ref, wbs_ref, wout_ref,
                       g1_ref, ng_ref, sh2_ref, sc2_ref, g2_ref, wr_ref, rb_ref, wsgu_ref, wsd_ref,
                       base_ref, ws_ref, cnt_ref, ptail, run, stage, dvm, dsm, sem_s, i, s, cap, milestone)

    @pl.when(i == 0)
    def _():
        tile(0)

    groups = iter(range(1, MIX_ISSUE_GROUPS))

    def milestone(value):
        if value is BEFORE_STAGING:
            assert next(groups, None) is None
            @pl.when(i >= 2)
            def _():
                drain_rows(slot)
        else:
            issue_rows(1 - slot, next(groups), value)

    def wait_slot_rows(s):
        pltpu.make_async_copy(dvm, dsm.at[s], sem_s).wait()

    @pl.when((i >= 1) & (i < n_tiles))
    def _():
        wait_slot_rows(1 - slot)
        issue_rows(1 - slot, 0)
        tile(slot, milestone)

    @pl.when(i == n_tiles)
    def _():
        wait_slot_rows(1 - slot)
        for group in range(MIX_ISSUE_GROUPS):
            issue_rows(1 - slot, group)
        if n_tiles >= 2:
            drain_rows(slot)
        drain_rows(1 - slot)


def _mix_tile_body(xp_ref, u_ref, ypre_ref, x_ref, wz_ref, wgt_ref, sng_ref, poolw_ref, pscale_ref, wbp_ref, wbs_ref, wout_ref,
                   g1_ref, ng_ref, sh2_ref, sc2_ref, g2_ref, wr_ref, rb_ref, wsgu_ref, wsd_ref,
                   base_ref, ws_ref, cnt_ref, ptail, run, stage, dvm, dsm, sem_s, i, slot, cap, milestone):
    t = x_ref.shape[0]
    if milestone is None:
        milestone = lambda value: None

    xp = xp_ref[...].astype(F32)
    ext = jnp.concatenate([ptail[...], xp], axis=0)
    ptail[...] = xp[t - 2 * SUBLANES:t]
    pos = (lax.broadcasted_iota(jnp.int32, (t, 1), 0) + (i * t + 1)).astype(F32)
    pooled = []
    for gi, w in enumerate(POOL_WINDOWS):
        sl = slice(gi * POOL_GDIM, (gi + 1) * POOL_GDIM)
        e = ext[:, sl]
        s = e
        span = 1
        while span < w:
            s = s + pltpu.roll(s, span, axis=0)
            span *= 2
        win = s[2 * SUBLANES:]
        mean = win / jnp.minimum(pos, float(w))
        pg = (mean - xp[:, sl]).astype(BF16)
        mixed = jnp.dot(pg, poolw_ref[gi], preferred_element_type=F32)
        pooled.append(mixed * pscale_ref[:, sl])
        milestone(pooled[-1])
    pooled = jnp.concatenate(pooled, axis=1).astype(BF16)
    y_pool = jnp.dot(pooled, wbp_ref[...], preferred_element_type=F32)
    milestone(y_pool)
    ub = u_ref[...]
    z = jnp.dot(ub, wz_ref[...], preferred_element_type=F32)
    milestone(z)
    yn = []
    for g in range(N_GROUPS):
        sl = slice(g * GROUP_W, (g + 1) * GROUP_W)
        yg = ypre_ref[:, sl].astype(F32) * _silu(z[:, sl])
        inv_g = lax.rsqrt(jnp.mean(yg * yg, axis=-1, keepdims=True) + EPS)
        yn.append((yg * inv_g * sng_ref[:, sl]).astype(BF16))
        milestone(inv_g)
    y_ssd = jnp.dot(jnp.concatenate(yn, axis=1), wbs_ref[...], preferred_element_type=F32)
    milestone(y_ssd)
    gates = jnp.dot(ub, wgt_ref[...], preferred_element_type=F32)
    milestone(gates)
    g_half = 0.5 * jnp.tanh(0.5 * gates)
    g_ssd = 0.5 + g_half[:, :D_MODEL]
    g_pool = 0.5 + g_half[:, D_MODEL:]
    mixed = (g_ssd * y_ssd + g_pool * y_pool).astype(BF16)
    h = x_ref[...] + g1_ref[...] * jnp.dot(mixed, wout_ref[...], preferred_element_type=F32)
    milestone(h)

    inv = lax.rsqrt(jnp.mean(h * h, axis=-1, keepdims=True) + EPS)
    milestone(inv)
    u2 = h * inv * ng_ref[...]
    u2 = u2 * (1.0 + sc2_ref[...]) + sh2_ref[...]
    u2b = u2.astype(BF16)
    packed = [_pack_bf16_pair(u2[:, j * LANES:(j + 1) * LANES],
                              u2[:, (j + PACK_TILES) * LANES:(j + PACK_TILES + 1) * LANES])
              for j in range(PACK_TILES)]

    hs = jnp.dot(u2b, wsgu_ref[...], preferred_element_type=F32)
    milestone(hs)
    act = (_silu(hs[:, :D_SHARED]) * hs[:, D_SHARED:]).astype(BF16)
    shared = jnp.dot(act, wsd_ref[...], preferred_element_type=F32)
    milestone(shared)
    base_ref[...] = h + g2_ref[...] * shared

    logits = jnp.dot(u2b, wr_ref[...], preferred_element_type=F32)
    scores = jax.nn.sigmoid(logits)
    milestone(scores)
    choice = scores + rb_ref[...]
    lane = lax.broadcasted_iota(jnp.int32, (t, N_EXPERTS), 1)
    lane_f = lane.astype(F32)
    lane_grp = lane // EXPERTS_PER_GROUP
    neg = -jnp.inf
    gscore = []
    for g in range(N_EGROUPS):
        vg = jnp.where(lane_grp == g, choice, neg)
        m1 = jnp.max(vg, axis=-1, keepdims=True)
        i1 = jnp.min(jnp.where(vg == m1, lane_f, float(N_EXPERTS)), axis=-1, keepdims=True)
        m2 = jnp.max(jnp.where(lane_f == i1, neg, vg), axis=-1, keepdims=True)
        gscore.append(m1 + m2)
        milestone(gscore[-1])
    gmask = jnp.zeros((t, N_EXPERTS), jnp.bool_)
    for g in range(N_EGROUPS):
        rank = jnp.zeros((t, 1), F32)
        for g2 in range(N_EGROUPS):
            if g2 == g:
                continue
            better = (gscore[g2] > gscore[g]) | ((gscore[g2] == gscore[g]) & (g2 < g))
            rank = rank + better.astype(F32)
        gmask = gmask | ((rank < float(TOPK_GROUPS)) & (lane_grp == g))
    work = jnp.where(gmask, choice, neg)
    sel = jnp.zeros((t, N_EXPERTS), jnp.bool_)
    onehots, idxs, sks = [], [], []
    for k in range(TOP_K):
        m = jnp.max(work, axis=-1, keepdims=True)
        idx = jnp.min(jnp.where(work == m, lane_f, float(N_EXPERTS)), axis=-1, keepdims=True)
        oh = lane_f == idx
        onehots.append(oh)
        idxs.append(idx)
        sks.append(jnp.sum(jnp.where(oh, scores, 0.0), axis=-1, keepdims=True))
        if k == TOP_K // 2 - 1:
            milestone(sks[-1])
        sel = sel | oh
        work = jnp.where(oh, neg, work)
    denom = sks[0]
    for k in range(1, TOP_K):
        denom = denom + sks[k]
    milestone(denom)

    ri = lax.broadcasted_iota(jnp.int32, (t, t), 0)
    ci = lax.broadcasted_iota(jnp.int32, (t, t), 1)
    before = (ri > ci).astype(BF16)
    sel_f = jnp.where(sel, 1.0, 0.0)
    sel_b = sel_f.astype(BF16)
    pos_tile = jnp.dot(before, sel_b, preferred_element_type=F32) + run[...]
    er = lax.broadcasted_iota(jnp.int32, (N_EXPERTS, N_EXPERTS), 0)
    ec = lax.broadcasted_iota(jnp.int32, (N_EXPERTS, N_EXPERTS), 1)
    rank_tile = jnp.dot(sel_b, (er < ec).astype(BF16), preferred_element_type=F32)
    lane128 = lax.broadcasted_iota(jnp.int32, (t, LANES), 1)
    k_iota = lax.broadcasted_iota(jnp.int32, (t, TOP_K), 1).astype(F32)
    dest = jnp.zeros((t, LANES), F32)
    w_sorted = jnp.zeros((t, TOP_K), F32)
    for k in range(TOP_K):
        pk = jnp.sum(jnp.where(onehots[k], pos_tile, 0.0), axis=-1, keepdims=True)
        rk = jnp.sum(jnp.where(onehots[k], rank_tile, 0.0), axis=-1, keepdims=True)
        dest = jnp.where(lane128 == k, idxs[k] * float(cap) + pk, dest)
        w_sorted = jnp.where(k_iota == rk, sks[k] / denom * ROUTED_SCALE, w_sorted)
    ws_ref[...] = w_sorted
    total = run[...] + jnp.sum(sel_f, axis=0, keepdims=True)
    run[...] = total
    cnt_ref[...] = total

    milestone(BEFORE_STAGING)
    for j in range(PACK_TILES):
        stage[slot, pl.ds(j, t, stride=SUBLANES), :] = packed[j]
    tok = lax.broadcasted_iota(jnp.int32, (t, LANES), 0) + i * t
    stage[slot, pl.ds(TOK_ROW, t, stride=SUBLANES), :] = tok.astype(jnp.uint32)
    bit = jnp.where(sel, jnp.left_shift(1, lane & 15), 0).astype(F32)
    words = []
    for q in range(N_EXPERTS // 16):
        part = jnp.sum(jnp.where((lane >> 4) == q, bit, 0.0), axis=-1, keepdims=True)
        words.append(part.astype(jnp.int32).astype(jnp.uint32))
    for w, row in enumerate(MASK_ROWS):
        word = words[2 * w] | (words[2 * w + 1] << 16)
        stage[slot, pl.ds(row, t, stride=SUBLANES), :] = jnp.broadcast_to(word, (t, LANES))

    dvm[...] = dest.T[0:TOP_K, :].astype(jnp.int32)
    pltpu.make_async_copy(dvm, dsm.at[slot], sem_s).start()


def _mix(proj, u, ypre, x2, w_z, w_gates, ssd_ng, pool_w, pool_scale, w_br_pool, w_br_ssd, w_out, g1, ng, sh2, sc2,
         g2, w_router, router_bias, ws_gu, ws_down, cap):
    n = x2.shape[0]
    t = 256
    n_tiles = n // t
    xp_blk = (D_SSD + 2 * D_BC) // D_MODEL
    const = lambda shape: pl.BlockSpec(shape, lambda i: (0,) * len(shape), pipeline_mode=pl.Buffered(1))
    tile = lambda i: jnp.minimum(i, n_tiles - 1)
    row = lambda i: (tile(i), 0)
    return pl.pallas_call(
        functools.partial(_mix_kernel, n_tiles=n_tiles, cap=cap),
        grid=(n_tiles + 1,),
        in_specs=[pl.BlockSpec((t, D_MODEL), lambda i: (tile(i), xp_blk)),
                  pl.BlockSpec((t, D_MODEL), row),
                  pl.BlockSpec((t, D_SSD), row),
                  pl.BlockSpec((t, D_MODEL), row),
                  const((D_MODEL, D_SSD)),
                  const((D_MODEL, 2 * D_MODEL)),
                  _full((1, D_SSD)),
                  const((len(POOL_WINDOWS), POOL_GDIM, POOL_GDIM)),
                  _full((1, D_MODEL)),
                  const((D_MODEL, D_MODEL)),
                  const((D_SSD, D_MODEL)),
                  const((D_MODEL, D_MODEL)),
                  _full((1, D_MODEL)), _full((1, D_MODEL)), _full((1, D_MODEL)), _full((1, D_MODEL)),
                  _full((1, D_MODEL)),
                  _full((D_MODEL, N_EXPERTS)),
                  _full((1, N_EXPERTS)),
                  const((D_MODEL, 2 * D_SHARED)),
                  const((D_SHARED, D_MODEL)),
                  pl.BlockSpec(memory_space=pltpu.SMEM)],
        out_specs=[pl.BlockSpec((t, D_MODEL), row),
                   pl.BlockSpec((t, TOP_K), row),
                   _full((1, N_EXPERTS)),
                   pl.BlockSpec(memory_space=pl.ANY)],
        out_shape=[jax.ShapeDtypeStruct((n, D_MODEL), F32),
                   jax.ShapeDtypeStruct((n, TOP_K), F32),
                   jax.ShapeDtypeStruct((1, N_EXPERTS), F32),
                   jax.ShapeDtypeStruct((N_EXPERTS * cap * SUBLANES, LANES), jnp.uint32)],
        scratch_shapes=[pltpu.VMEM((2 * SUBLANES, D_MODEL), F32),
                        pltpu.VMEM((1, N_EXPERTS), F32),
                        pltpu.VMEM((2, t * SUBLANES, LANES), jnp.uint32),
                        pltpu.VMEM((TOP_K, t), jnp.int32),
                        pltpu.SMEM((2, TOP_K, t), jnp.int32),
                        pltpu.SemaphoreType.DMA((2,)),
                        pltpu.SemaphoreType.DMA(())],
        compiler_params=_cparams(),
        name="mix",
    )(proj, u, ypre, x2, w_z, w_gates, ssd_ng, pool_w, pool_scale, w_br_pool, w_br_ssd, w_out, g1, ng, sh2, sc2, g2,
      w_router, router_bias, ws_gu, ws_down, jnp.zeros((1,), jnp.int32))


def _padfill_kernel(cnt_ref, xs_in_ref, xs_ref, padbuf, sem, *, cap):
    del xs_in_ref
    sub = lax.broadcasted_iota(jnp.int32, padbuf.shape, 0) & (SUBLANES - 1)
    padbuf[...] = jnp.where(sub == TOK_ROW, jnp.uint32(PAD_ID), jnp.uint32(0))
    sizes = [1 << bit for bit in range(EXPERT_BLOCK.bit_length() - 1)]

    def pad_copies(e, fn):
        c = cnt_ref[e]
        n_pad = (EXPERT_BLOCK - (c & (EXPERT_BLOCK - 1))) & (EXPERT_BLOCK - 1)
        first = e * cap + c
        for size in sizes:
            @pl.when((n_pad & size) != 0)
            def _():
                start = first + (n_pad & (size - 1))
                dst = xs_ref.at[pl.ds(pl.multiple_of(start * SUBLANES, SUBLANES), size * SUBLANES)]
                fn(pltpu.make_async_copy(padbuf.at[pl.ds(0, size * SUBLANES)], dst, sem))

    def issue(e, carry):
        pad_copies(e, lambda cp: cp.start())
        return carry

    lax.fori_loop(0, N_EXPERTS, issue, 0)

    def drain(e, carry):
        pad_copies(e, lambda cp: cp.wait())
        return carry

    lax.fori_loop(0, N_EXPERTS, drain, 0)


def _padfill(counts, xs, cap):
    return pl.pallas_call(
        functools.partial(_padfill_kernel, cap=cap),
        grid_spec=pltpu.PrefetchScalarGridSpec(
            num_scalar_prefetch=1,
            grid=(1,),
            in_specs=[pl.BlockSpec(memory_space=pl.ANY)],
            out_specs=pl.BlockSpec(memory_space=pl.ANY),
            scratch_shapes=[pltpu.VMEM((EXPERT_BLOCK // 2 * SUBLANES, LANES), jnp.uint32),
                            pltpu.SemaphoreType.DMA(())]),
        out_shape=jax.ShapeDtypeStruct(xs.shape, xs.dtype),
        input_output_aliases={1: 0},
        compiler_params=_cparams(),
        name="padfill",
    )(counts, xs)


def _expert_kernel(be_ref, br_ref, nb_ref, x_ref, wg_ref, wu_ref, wd_ref, ytok_ref, ybuf0, ybuf1, idv, ids0, ids1,
                   wgb, wub, wdb, sem, sem_ids, *, n_ids):
    del br_ref
    m = EXPERT_BLOCK
    b = pl.program_id(0)
    nb = nb_ref[0]
    ybufs, idss = (ybuf0, ybuf1), (ids0, ids1)
    n_phases = m // LANES
    half = D_MODEL // 2

    @pl.when((b == 0) | (be_ref[b] != be_ref[jnp.maximum(b - 1, 0)]))
    def _():
        wgb[...] = wg_ref[0].astype(BF16)
        wub[...] = wu_ref[0].astype(BF16)
        wdb[...] = wd_ref[0].astype(BF16)

    def half_tile(ref, unit):
        return ref.at[pl.ds(pl.multiple_of(unit * PACK_TILES, PACK_TILES), PACK_TILES)]

    def issue(s, a0):
        for col in range(LANES):
            pltpu.make_async_copy(half_tile(ybufs[s], a0 * LANES + col),
                                  half_tile(ytok_ref, idss[s][a0, col]), sem.at[s]).start()

    def step(s, prev):
        tokrep = x_ref[pl.ds(TOK_ROW, m, stride=SUBLANES), :]
        e = be_ref[b]
        below_lo = jnp.where(e >= 32, -1, (1 << jnp.minimum(e, 31)) - 1).astype(jnp.uint32)
        below_hi = jnp.where(e >= 32, (1 << jnp.maximum(e - 32, 0)) - 1, 0).astype(jnp.uint32)
        rank = (lax.population_count(x_ref[pl.ds(MASK_ROWS[0], m, stride=SUBLANES), :] & below_lo)
                + lax.population_count(x_ref[pl.ds(MASK_ROWS[1], m, stride=SUBLANES), :] & below_hi))
        r = lax.broadcasted_iota(jnp.int32, (m, LANES), 0)
        ln = lax.broadcasted_iota(jnp.int32, (m, LANES), 1)
        n_tok = n_ids // TOP_K
        rank = rank.astype(jnp.int32)
        half_rank = TOP_K // 2
        unit = ((rank % half_rank) * n_tok + tokrep.astype(jnp.int32)) * 2 + rank // half_rank
        idi = jnp.where(tokrep == jnp.uint32(PAD_ID), n_ids + r, unit)
        diag = jnp.where((r & (LANES - 1)) == ln, idi, 0).astype(F32)
        idv[...] = jnp.sum(diag.reshape(m // LANES, LANES, LANES), axis=1).astype(jnp.int32)
        ids_copy = pltpu.make_async_copy(idv, idss[s], sem_ids)
        ids_copy.start()

        halves = [_unpack_bf16_pair(x_ref[pl.ds(j, m, stride=SUBLANES), :]) for j in range(PACK_TILES)]
        x = jnp.concatenate([lo for lo, _ in halves] + [hi for _, hi in halves], axis=1).astype(BF16)
        phases = iter(range(n_phases))
        if prev is not None:
            issue(prev, next(phases))
        gate = jnp.dot(x, wgb[...], preferred_element_type=F32)
        if prev is not None:
            issue(prev, next(phases))
        up = jnp.dot(x, wub[...], preferred_element_type=F32)
        hid = (_silu(gate) * up).astype(BF16)

        def drain(which):
            pltpu.make_async_copy(ybufs[which], ytok_ref.at[pl.ds(0, m * PACK_TILES)], sem.at[which]).wait()

        if prev is not None:
            @pl.when(b >= 2)
            def _():
                drain(s)
        ys = []
        for c0 in (0, half):
            if prev is not None:
                issue(prev, next(phases))
            ys.append(jnp.dot(hid, wdb[:, c0:c0 + half], preferred_element_type=F32))
        for j in range(PACK_TILES):
            sl = slice(j * LANES, (j + 1) * LANES)
            ybufs[s][pl.ds(j, m, stride=PACK_TILES), :] = _pack_bf16_pair(ys[0][:, sl], ys[1][:, sl])
        ids_copy.wait()
        if prev is not None:
            for a0 in phases:
                issue(prev, a0)

            @pl.when(b == nb)
            def _():
                drain(prev)

    @pl.when(b == 0)
    def _():
        step(0, None)

    for s in range(2):
        @pl.when((b >= 1) & (b <= nb) & (lax.rem(b, 2) == s))
        def _():
            step(s, 1 - s)


def _experts(block_e, block_row, nb_used, xs, we_gate, we_up, we_down, n_ids):
    n_blocks = block_e.shape[0]
    m = EXPERT_BLOCK
    rows = m * SUBLANES
    return pl.pallas_call(
        functools.partial(_expert_kernel, n_ids=n_ids),
        grid_spec=pltpu.PrefetchScalarGridSpec(
            num_scalar_prefetch=3,
            grid=(n_blocks,),
            in_specs=[pl.BlockSpec((rows, LANES), lambda b, be, br, nb: (br[b], 0)),
                      pl.BlockSpec((1, D_MODEL, D_EXPERT), lambda b, be, br, nb: (be[b], 0, 0)),
                      pl.BlockSpec((1, D_MODEL, D_EXPERT), lambda b, be, br, nb: (be[b], 0, 0)),
                      pl.BlockSpec((1, D_EXPERT, D_MODEL), lambda b, be, br, nb: (be[b], 0, 0))],
            out_specs=pl.BlockSpec(memory_space=pl.ANY),
            scratch_shapes=[pltpu.VMEM((m * PACK_TILES, LANES), jnp.uint32),
                            pltpu.VMEM((m * PACK_TILES, LANES), jnp.uint32),
                            pltpu.VMEM((m // LANES, LANES), jnp.int32),
                            pltpu.SMEM((m // LANES, LANES), jnp.int32),
                            pltpu.SMEM((m // LANES, LANES), jnp.int32),
                            pltpu.VMEM((D_MODEL, D_EXPERT), BF16),
                            pltpu.VMEM((D_MODEL, D_EXPERT), BF16),
                            pltpu.VMEM((D_EXPERT, D_MODEL), BF16),
                            pltpu.SemaphoreType.DMA((2,)),
                            pltpu.SemaphoreType.DMA(())]),
        out_shape=jax.ShapeDtypeStruct(((n_ids + m) * PACK_TILES, LANES), jnp.uint32),
        compiler_params=_cparams(),
        name="experts",
    )(block_e, block_row, nb_used, xs, we_gate, we_up, we_down)


def _combine_kernel(*refs):
    half_rank = TOP_K // 2
    y_refs = refs[:half_rank]
    w8_ref, base_ref, g2_ref, fg_ref, o_ref = refs[half_rank:]
    t7 = base_ref.shape[0]
    w8 = w8_ref[...]
    parts = [jnp.zeros((t7, LANES), F32) for _ in range(ROW_TILES)]
    for p in range(half_rank):
        for h in range(2):
            wk = w8[:, p + h * half_rank:p + h * half_rank + 1]
            for j in range(PACK_TILES):
                lo, hi = _unpack_bf16_pair(y_refs[p][pl.ds(h * PACK_TILES + j, t7, stride=SUBLANES), :])
                parts[j] = parts[j] + wk * lo
                parts[j + PACK_TILES] = parts[j + PACK_TILES] + wk * hi
    routed = jnp.concatenate(parts, axis=1)
    h = base_ref[...] + g2_ref[...] * routed
    inv = lax.rsqrt(jnp.mean(h * h, axis=-1, keepdims=True) + EPS)
    o_ref[...] = h * inv * fg_ref[...]


def _combine(ytok, w8, base, g2, fg):
    n = base.shape[0]
    t7 = 512
    y_specs = [pl.BlockSpec((t7 * SUBLANES, LANES), functools.partial(lambda i, p: (p * (n // t7) + i, 0), p=p))
               for p in range(TOP_K // 2)]
    return pl.pallas_call(
        _combine_kernel,
        grid=(n // t7,),
        in_specs=y_specs + [
                  pl.BlockSpec((t7, TOP_K), lambda i: (i, 0)),
                  pl.BlockSpec((t7, D_MODEL), lambda i: (i, 0)),
                  _full((1, D_MODEL)), _full((1, D_MODEL))],
        out_specs=pl.BlockSpec((t7, D_MODEL), lambda i: (i, 0)),
        out_shape=jax.ShapeDtypeStruct((n, D_MODEL), F32),
        compiler_params=_cparams(),
        name="combine",
    )(*([ytok] * (TOP_K // 2)), w8, base, g2, fg)


def kernel(x, c, w_ada, b_ada, norm_mix_g, w_in, conv_w, conv_b, dt_bias, A_log, D_skip, ssd_norm_g, pool_w,
           pool_scale, w_br_ssd, w_br_pool, w_out, norm_ffn_g, w_router, router_bias, we_gate, we_up, we_down,
           ws_gate, ws_up, ws_down, final_norm_g):
    bsz, seq, _ = x.shape
    assert bsz == 1 and w_ada.shape[0] == 1
    n = seq
    h2 = x.reshape(n, D_MODEL)

    mod = _ada(c, w_ada[0], b_ada[0])
    sh1, sc1, g1, sh2, sc2, g2 = [mod[:, k * D_MODEL:(k + 1) * D_MODEL] for k in range(6)]

    wi = w_in[0]
    o_xbc, o_dt = D_SSD, D_SSD + D_SSD + 2 * D_BC
    o_pool = o_dt + N_HEADS
    o_gates = o_pool + D_MODEL
    w_dt = jnp.pad(wi[:, o_dt:o_pool], ((0, 0), (0, LANES - N_HEADS))).astype(BF16)
    proj, dt_raw, u = _in_proj(h2, norm_mix_g[0][None], sh1, sc1, wi[:, o_xbc:o_dt].astype(BF16),
                               wi[:, o_pool:o_gates].astype(BF16), w_dt)

    pad_h = lambda v: jnp.pad(v[None], ((0, 0), (0, LANES - N_HEADS)))
    ypre = _ssd(proj, dt_raw, conv_w[0], conv_b[0][None], pad_h(dt_bias[0]), pad_h(A_log[0]),
                jnp.repeat(D_skip[0], HEADDIM)[None])

    ws_gu = jnp.concatenate([ws_gate[0], ws_up[0]], axis=1).astype(BF16)
    m = EXPERT_BLOCK
    cap = -(-n // m) * m
    base, w8, cnt, xs = _mix(
        proj, u, ypre, h2, wi[:, :o_xbc].astype(BF16), wi[:, o_gates:].astype(BF16), ssd_norm_g[0][None],
        pool_w[0].astype(BF16), pool_scale[0][None], w_br_pool[0].astype(BF16),
        w_br_ssd[0].astype(BF16), w_out[0].astype(BF16), g1, norm_ffn_g[0][None], sh2, sc2, g2,
        w_router[0].astype(BF16), router_bias[0][None], ws_gu, ws_down[0].astype(BF16), cap)

    counts = cnt[0].astype(jnp.int32)
    nblk = (counts + m - 1) // m
    bends = jnp.cumsum(nblk)
    bstarts = bends - nblk
    nb_used = bends[-1]
    n_blocks = -(-(n * TOP_K) // m) + N_EXPERTS + 1
    b_eff = jnp.minimum(jnp.arange(n_blocks, dtype=jnp.int32), nb_used - 1)
    done = bends[None, :] <= b_eff[:, None]
    block_e = jnp.sum(done.astype(jnp.int32), axis=1)
    block_row = block_e * (cap // m) + b_eff - jnp.sum(jnp.where(done, nblk[None, :], 0), axis=1)

    xs = _padfill(counts, xs, cap)
    ytok = _experts(block_e, block_row, nb_used.reshape(1), xs,
                    we_gate[0], we_up[0], we_down[0], n * TOP_K)
    out = _combine(ytok, w8, base, g2, final_norm_g[None])
    return out.reshape(bsz, seq, D_MODEL)
```

```python
import functools

import jax
import jax.numpy as jnp
from jax import lax
from jax.experimental import pallas as pl
from jax.experimental.pallas import tpu as pltpu

F32 = jnp.float32
BF16 = jnp.bfloat16
HIGHEST = lax.Precision.HIGHEST

D_MODEL = 1024
D_SSD = 2048
HEADDIM = 64
N_HEADS = 32
N_GROUPS = 8
HEADS_PER_GROUP = N_HEADS // N_GROUPS
D_STATE = 128
CONV_K = 4
CHUNK = 128
SSD_CHUNKS_PER_STEP = 4
GROUP_W = D_SSD // N_GROUPS
D_BC = N_GROUPS * D_STATE
POOL_WINDOWS = (2, 4, 8, 16)
POOL_GDIM = 256
N_EXPERTS = 64
TOP_K = 8
N_EGROUPS = 8
EXPERTS_PER_GROUP = 8
TOPK_GROUPS = 4
D_EXPERT = 256
D_SHARED = 256
ROUTED_SCALE = 2.5
EPS = 1e-6

LANES = 128
SUBLANES = 8
ROW_TILES = D_MODEL // LANES
PACK_TILES = ROW_TILES // 2
PACK_W = PACK_TILES * LANES
EXPERT_BLOCK = 512
MIX_ISSUE_GROUPS = 32
BEFORE_STAGING = "before_staging"

PROJ_W = D_SSD + 2 * D_BC + D_MODEL
PROJ_CHUNK = 512

VMEM_LIMIT = 56 * 1024 * 1024


def _cparams(sem=("arbitrary",)):
    return pltpu.CompilerParams(dimension_semantics=sem, vmem_limit_bytes=VMEM_LIMIT)


def _full(shape):
    nd = len(shape)
    return pl.BlockSpec(shape, lambda *_: (0,) * nd)


def _silu(v):
    h = 0.5 * v
    return h + h * jnp.tanh(h)


def _pack_bf16_pair(lo, hi):
    lo_bits = lax.bitcast_convert_type(lo.astype(BF16).astype(F32), jnp.uint32)
    hi_bits = lax.bitcast_convert_type(hi.astype(BF16).astype(F32), jnp.uint32)
    return (lo_bits >> 16) | (hi_bits & jnp.uint32(0xFFFF0000))


def _unpack_bf16_pair(words):
    return (lax.bitcast_convert_type(words << 16, F32),
            lax.bitcast_convert_type(words & jnp.uint32(0xFFFF0000), F32))


def _ada_kernel(c_ref, w_ref, b_ref, o_ref):
    c = c_ref[...]
    o_ref[...] = jnp.dot(_silu(c), w_ref[...], preferred_element_type=F32, precision=HIGHEST) + b_ref[...]


def _ada(c, w_ada, b_ada):
    n_out = w_ada.shape[1]
    tn = 1536
    c8 = jnp.broadcast_to(c, (SUBLANES, D_MODEL))
    out = pl.pallas_call(
        _ada_kernel,
        grid=(n_out // tn,),
        in_specs=[_full((SUBLANES, D_MODEL)),
                  pl.BlockSpec((D_MODEL, tn), lambda j: (0, j)),
                  pl.BlockSpec((1, tn), lambda j: (0, j))],
        out_specs=pl.BlockSpec((SUBLANES, tn), lambda j: (0, j)),
        out_shape=jax.ShapeDtypeStruct((SUBLANES, n_out), F32),
        compiler_params=_cparams(),
        name="ada",
    )(c8, w_ada, b_ada.reshape(1, n_out))
    return out[0:1]


def _inproj_kernel(x_ref, g_ref, sh_ref, sc_ref, wa0_ref, wa1_ref, wb_ref, wdt_ref, proj_ref, dt_ref, u_ref):
    x = x_ref[...]
    inv = lax.rsqrt(jnp.mean(x * x, axis=-1, keepdims=True) + EPS)
    u = x * inv * g_ref[...]
    u = u * (1.0 + sc_ref[...]) + sh_ref[...]
    ub = u.astype(BF16)
    u_ref[...] = ub
    w_refs = (wa0_ref, wa1_ref, wb_ref)
    starts = [0]
    for w_ref in w_refs:
        starts.append(starts[-1] + w_ref.shape[1])
    for c0 in range(0, PROJ_W, PROJ_CHUNK):
        k = max(j for j in range(len(w_refs)) if starts[j] <= c0)
        w = w_refs[k][:, c0 - starts[k]:c0 - starts[k] + PROJ_CHUNK]
        proj_ref[:, c0:c0 + PROJ_CHUNK] = jnp.dot(ub, w, preferred_element_type=F32).astype(BF16)
    dt_ref[...] = jnp.dot(ub, wdt_ref[...], preferred_element_type=F32)


def _in_proj(x2, g, sh, sc, w_in_b, w_b, w_dt):
    n = x2.shape[0]
    tm = 512
    return pl.pallas_call(
        _inproj_kernel,
        grid=(n // tm,),
        in_specs=[pl.BlockSpec((tm, D_MODEL), lambda i: (i, 0)),
                  _full((1, D_MODEL)), _full((1, D_MODEL)), _full((1, D_MODEL)),
                  pl.BlockSpec((D_MODEL, D_SSD), lambda i: (0, 1), pipeline_mode=pl.Buffered(1)),
                  pl.BlockSpec((D_MODEL, D_SSD), lambda i: (0, 2), pipeline_mode=pl.Buffered(1)),
                  pl.BlockSpec(w_b.shape, lambda i: (0, 0), pipeline_mode=pl.Buffered(1)),
                  _full((D_MODEL, LANES))],
        out_specs=[pl.BlockSpec((tm, PROJ_W), lambda i: (i, 0)),
                   pl.BlockSpec((tm, LANES), lambda i: (i, 0)),
                   pl.BlockSpec((tm, D_MODEL), lambda i: (i, 0))],
        out_shape=[jax.ShapeDtypeStruct((n, PROJ_W), BF16),
                   jax.ShapeDtypeStruct((n, LANES), F32),
                   jax.ShapeDtypeStruct((n, D_MODEL), BF16)],
        compiler_params=_cparams(),
        name="in_proj",
    )(x2, g, sh, sc, w_in_b, w_in_b, w_b, w_dt)


def _conv_silu(cur_ref, ext_ref, w_ref, b_ref, out_ref, width, cw=512):
    t = cur_ref.shape[0]
    for c0 in range(0, width, cw):
        sl = slice(c0, c0 + cw)
        cur = cur_ref[:, sl].astype(F32)
        ext_ref[SUBLANES:, sl] = cur
        acc = cur * w_ref[CONV_K - 1:CONV_K, sl] + b_ref[:, sl]
        for s in range(1, CONV_K):
            acc = acc + ext_ref[pl.ds(SUBLANES - s, t), sl] * w_ref[CONV_K - 1 - s:CONV_K - s, sl]
        out_ref[:, sl] = _silu(acc).astype(out_ref.dtype)
        ext_ref[0:SUBLANES, sl] = cur[t - SUBLANES:t]


def _ssd_kernel(xs_ref, bc_ref, dt_ref, cwx_ref, cwbc_ref, cbx_ref, cbbc_ref, dtb_ref, alog_ref,
                dskip_ref, o_ref, tailx, tailbc, state, xc, bcc):
    q = CHUNK
    pair_w = 2 * HEADDIM

    @pl.when(pl.program_id(0) == 0)
    def _():
        tailx[...] = jnp.zeros_like(tailx)
        tailbc[...] = jnp.zeros_like(tailbc)
        state[...] = jnp.zeros_like(state)

    _conv_silu(xs_ref, tailx, cwx_ref, cbx_ref, xc, D_SSD)
    _conv_silu(bc_ref, tailbc, cwbc_ref, cbbc_ref, bcc, 2 * D_BC)

    for c in range(xs_ref.shape[0] // q):
        rows = slice(c * q, (c + 1) * q)
        v = dt_ref[rows, :] + dtb_ref[...]
        dt = jnp.maximum(v, 0.0) + jnp.log(1.0 + jnp.exp(-jnp.abs(v)))
        a = dt * (-jnp.exp(alog_ref[...]))
        ri = lax.broadcasted_iota(jnp.int32, (q, q), 0)
        ci = lax.broadcasted_iota(jnp.int32, (q, q), 1)
        causal = ri >= ci
        a_cs = jnp.dot(causal.astype(F32), a, preferred_element_type=F32, precision=HIGHEST)
        a_cs_t = a_cs.T
        dt_t = dt.T
        first_half = ci < HEADDIM
        first_half_row = first_half[0:1, :]

        for g in range(N_GROUPS):
            b_g = bcc[rows, g * D_STATE:(g + 1) * D_STATE]
            c_g = bcc[rows, D_BC + g * D_STATE:D_BC + (g + 1) * D_STATE]
            cb = lax.dot_general(c_g, b_g, (((1,), (1,)), ((), ())), preferred_element_type=F32)
            b_t = b_g.astype(F32).T
            st_g = state[g]
            y_off = jnp.dot(c_g, st_g.astype(BF16), preferred_element_type=F32)
            for pi in range(HEADS_PER_GROUP // 2):
                h0 = g * HEADS_PER_GROUP + 2 * pi
                lanes = slice(h0 * HEADDIM, h0 * HEADDIM + pair_w)
                gl = slice(pi * pair_w, (pi + 1) * pair_w)
                ms, ws, cols, lasts = [], [], [], []
                for h in (h0, h0 + 1):
                    col = jnp.broadcast_to(a_cs[:, h:h + 1], (q, q))
                    row = a_cs_t[h:h + 1, :]
                    dtrow = dt_t[h:h + 1, :]
                    decay = jnp.exp(jnp.where(causal, col - row, -jnp.inf))
                    ms.append((cb * decay * dtrow).astype(BF16))
                    last = a_cs_t[h:h + 1, q - 1:q]
                    ws.append((b_t * (jnp.exp(last - row) * dtrow)).astype(BF16))
                    cols.append(col)
                    lasts.append(last)
                xp = xc[rows, lanes]
                zero = jnp.zeros_like(xp)
                x_bd = jnp.concatenate([jnp.where(first_half, xp, zero), jnp.where(first_half, zero, xp)], axis=0)
                y_p = jnp.dot(jnp.concatenate(ms, axis=1), x_bd, preferred_element_type=F32)
                y_p = y_p + y_off[:, gl] * jnp.exp(jnp.where(first_half, cols[0], cols[1]))
                o_ref[rows, lanes] = (y_p + dskip_ref[:, lanes] * xp.astype(F32)).astype(BF16)
                s_new = jnp.dot(jnp.concatenate(ws, axis=1), x_bd, preferred_element_type=F32)
                carry = jnp.exp(jnp.where(first_half_row, lasts[0], lasts[1]))
                state[g, :, gl] = st_g[:, gl] * carry + s_new


def _ssd(proj, dt_raw, conv_w, conv_b, dt_bias, a_log, d_skip):
    n = proj.shape[0]
    t = SSD_CHUNKS_PER_STEP * CHUNK
    return pl.pallas_call(
        _ssd_kernel,
        grid=(n // t,),
        in_specs=[pl.BlockSpec((t, D_SSD), lambda i: (i, 0)),
                  pl.BlockSpec((t, 2 * D_BC), lambda i: (i, 1)),
                  pl.BlockSpec((t, LANES), lambda i: (i, 0)),
                  pl.BlockSpec((CONV_K, D_SSD), lambda i: (0, 0)),
                  pl.BlockSpec((CONV_K, 2 * D_BC), lambda i: (0, 1)),
                  pl.BlockSpec((1, D_SSD), lambda i: (0, 0)),
                  pl.BlockSpec((1, 2 * D_BC), lambda i: (0, 1)),
                  _full((1, LANES)), _full((1, LANES)),
                  _full((1, D_SSD))],
        out_specs=pl.BlockSpec((t, D_SSD), lambda i: (i, 0)),
        out_shape=jax.ShapeDtypeStruct((n, D_SSD), BF16),
        scratch_shapes=[pltpu.VMEM((SUBLANES + t, D_SSD), F32),
                        pltpu.VMEM((SUBLANES + t, 2 * D_BC), F32),
                        pltpu.VMEM((N_GROUPS, D_STATE, GROUP_W), F32),
                        pltpu.VMEM((t, D_SSD), BF16),
                        pltpu.VMEM((t, 2 * D_BC), BF16)],
        compiler_params=_cparams(),
        name="ssd",
    )(proj, proj, dt_raw, conv_w, conv_w, conv_b, conv_b, dt_bias, a_log, d_skip)


PAD_ID = 0xFFFFFFFF
TOK_ROW = PACK_TILES
MASK_ROWS = (PACK_TILES + 1, PACK_TILES + 2)


def _tile_rows(ref, first_row):
    return ref.at[pl.ds(pl.multiple_of(first_row * SUBLANES, SUBLANES), SUBLANES)]


def _mix_kernel(xp_ref, u_ref, ypre_ref, x_ref, wz_ref, wgt_ref, sng_ref, poolw_ref, pscale_ref, wbp_ref, wbs_ref, wout_ref,
                g1_ref, ng_ref, sh2_ref, sc2_ref, g2_ref, wr_ref, rb_ref, wsgu_ref, wsd_ref, zero_ref,
                base_ref, ws_ref, cnt_ref, xs_ref, ptail, run, stage, dvm, dsm, sem, sem_s, *, n_tiles, cap):
    t = x_ref.shape[0]
    i = pl.program_id(0)
    slot = lax.rem(i, 2)

    @pl.when(i == 0)
    def _():
        ptail[...] = jnp.zeros_like(ptail)
        run[...] = jnp.zeros_like(run)
        stage[...] = jnp.zeros_like(stage)

    def issue_rows(s, group, after=None):
        zero = 0
        if after is not None:
            probe = jnp.max(jnp.abs(after[0:SUBLANES, 0:min(LANES, after.shape[1])])).astype(jnp.int32)
            zero = probe * zero_ref[0]
        per = t // MIX_ISSUE_GROUPS
        for tt in range(group * per, (group + 1) * per):
            for k in range(TOP_K):
                pltpu.make_async_copy(stage.at[s, pl.ds(tt * SUBLANES, SUBLANES)],
                                      _tile_rows(xs_ref, dsm[s, k, tt] + zero), sem.at[s]).start()

    def drain_rows(s):
        for _ in range(TOP_K):
            pltpu.make_async_copy(stage.at[s], xs_ref.at[pl.ds(0, t * SUBLANES)], sem.at[s]).wait()

    def tile(s, milestone=None):
        _mix_tile_body(xp_ref, u_ref, ypre_ref, x_ref, wz_ref, wgt_ref, sng_ref, poolw_ref, pscale_ref, wbp_ref, wbs_ref, wout_ref,
                       g1_ref, ng_ref, sh2_ref, sc2_ref, g2_ref, wr_ref, rb_ref, wsgu_ref, wsd_ref,
                       base_ref, ws_ref, cnt_ref, ptail, run, stage, dvm, dsm, sem_s, i, s, cap, milestone)

    @pl.when(i == 0)
    def _():
        tile(0)

    groups = iter(range(1, MIX_ISSUE_GROUPS))

    def milestone(value):
        if value is BEFORE_STAGING:
            assert next(groups, None) is None
            @pl.when(i >= 2)
            def _():
                drain_rows(slot)
        else:
            issue_rows(1 - slot, next(groups), value)

    def wait_slot_rows(s):
        pltpu.make_async_copy(dvm, dsm.at[s], sem_s).wait()

    @pl.when((i >= 1) & (i < n_tiles))
    def _():
        wait_slot_rows(1 - slot)
        issue_rows(1 - slot, 0)
        tile(slot, milestone)

    @pl.when(i == n_tiles)
    def _():
        wait_slot_rows(1 - slot)
        for group in range(MIX_ISSUE_GROUPS):
            issue_rows(1 - slot, group)
        if n_tiles >= 2:
            drain_rows(slot)
        drain_rows(1 - slot)


def _mix_tile_body(xp_ref, u_ref, ypre_ref, x_ref, wz_ref, wgt_ref, sng_ref, poolw_ref, pscale_ref, wbp_ref, wbs_ref, wout_ref,
                   g1_ref, ng_ref, sh2_ref, sc2_ref, g2_ref, wr_ref, rb_ref, wsgu_ref, wsd_ref,
                   base_ref, ws_ref, cnt_ref, ptail, run, stage, dvm, dsm, sem_s, i, slot, cap, milestone):
    t = x_ref.shape[0]
    if milestone is None:
        milestone = lambda value: None

    xp = xp_ref[...].astype(F32)
    ext = jnp.concatenate([ptail[...], xp], axis=0)
    ptail[...] = xp[t - 2 * SUBLANES:t]
    pos = (lax.broadcasted_iota(jnp.int32, (t, 1), 0) + (i * t + 1)).astype(F32)
    pooled = []
    for gi, w in enumerate(POOL_WINDOWS):
        sl = slice(gi * POOL_GDIM, (gi + 1) * POOL_GDIM)
        e = ext[:, sl]
        s = e
        span = 1
        while span < w:
            s = s + pltpu.roll(s, span, axis=0)
            span *= 2
        win = s[2 * SUBLANES:]
        mean = win / jnp.minimum(pos, float(w))
        pg = (mean - xp[:, sl]).astype(BF16)
        mixed = jnp.dot(pg, poolw_ref[gi], preferred_element_type=F32)
        pooled.append(mixed * pscale_ref[:, sl])
        milestone(pooled[-1])
    pooled = jnp.concatenate(pooled, axis=1).astype(BF16)
    y_pool = jnp.dot(pooled, wbp_ref[...], preferred_element_type=F32)
    milestone(y_pool)
    ub = u_ref[...]
    z = jnp.dot(ub, wz_ref[...], preferred_element_type=F32)
    milestone(z)
    yn = []
    for g in range(N_GROUPS):
        sl = slice(g * GROUP_W, (g + 1) * GROUP_W)
        yg = ypre_ref[:, sl].astype(F32) * _silu(z[:, sl])
        inv_g = lax.rsqrt(jnp.mean(yg * yg, axis=-1, keepdims=True) + EPS)
        yn.append((yg * inv_g * sng_ref[:, sl]).astype(BF16))
        milestone(inv_g)
    y_ssd = jnp.dot(jnp.concatenate(yn, axis=1), wbs_ref[...], preferred_element_type=F32)
    milestone(y_ssd)
    gates = jnp.dot(ub, wgt_ref[...], preferred_element_type=F32)
    milestone(gates)
    g_ssd = jax.nn.sigmoid(gates[:, :D_MODEL])
    g_pool = jax.nn.sigmoid(gates[:, D_MODEL:])
    mixed = (g_ssd * y_ssd + g_pool * y_pool).astype(BF16)
    h = x_ref[...] + g1_ref[...] * jnp.dot(mixed, wout_ref[...], preferred_element_type=F32)
    milestone(h)

    inv = lax.rsqrt(jnp.mean(h * h, axis=-1, keepdims=True) + EPS)
    milestone(inv)
    u2 = h * inv * ng_ref[...]
    u2 = u2 * (1.0 + sc2_ref[...]) + sh2_ref[...]
    u2b = u2.astype(BF16)
    packed = [_pack_bf16_pair(u2[:, j * LANES:(j + 1) * LANES],
                              u2[:, (j + PACK_TILES) * LANES:(j + PACK_TILES + 1) * LANES])
              for j in range(PACK_TILES)]

    hs = jnp.dot(u2b, wsgu_ref[...], preferred_element_type=F32)
    milestone(hs)
    act = (_silu(hs[:, :D_SHARED]) * hs[:, D_SHARED:]).astype(BF16)
    shared = jnp.dot(act, wsd_ref[...], preferred_element_type=F32)
    milestone(shared)
    base_ref[...] = h + g2_ref[...] * shared

    logits = jnp.dot(u2b, wr_ref[...], preferred_element_type=F32)
    scores = jax.nn.sigmoid(logits)
    milestone(scores)
    choice = scores + rb_ref[...]
    lane = lax.broadcasted_iota(jnp.int32, (t, N_EXPERTS), 1)
    lane_f = lane.astype(F32)
    lane_grp = lane // EXPERTS_PER_GROUP
    neg = -jnp.inf
    gscore = []
    for g in range(N_EGROUPS):
        vg = jnp.where(lane_grp == g, choice, neg)
        m1 = jnp.max(vg, axis=-1, keepdims=True)
        i1 = jnp.min(jnp.where(vg == m1, lane_f, float(N_EXPERTS)), axis=-1, keepdims=True)
        m2 = jnp.max(jnp.where(lane_f == i1, neg, vg), axis=-1, keepdims=True)
        gscore.append(m1 + m2)
        milestone(gscore[-1])
    gmask = jnp.zeros((t, N_EXPERTS), jnp.bool_)
    for g in range(N_EGROUPS):
        rank = jnp.zeros((t, 1), F32)
        for g2 in range(N_EGROUPS):
            if g2 == g:
                continue
            better = (gscore[g2] > gscore[g]) | ((gscore[g2] == gscore[g]) & (g2 < g))
            rank = rank + better.astype(F32)
        gmask = gmask | ((rank < float(TOPK_GROUPS)) & (lane_grp == g))
    work = jnp.where(gmask, choice, neg)
    sel = jnp.zeros((t, N_EXPERTS), jnp.bool_)
    onehots, idxs, sks = [], [], []
    for k in range(TOP_K):
        m = jnp.max(work, axis=-1, keepdims=True)
        idx = jnp.min(jnp.where(work == m, lane_f, float(N_EXPERTS)), axis=-1, keepdims=True)
        oh = lane_f == idx
        onehots.append(oh)
        idxs.append(idx)
        sks.append(jnp.sum(jnp.where(oh, scores, 0.0), axis=-1, keepdims=True))
        if k == TOP_K // 2 - 1:
            milestone(sks[-1])
        sel = sel | oh
        work = jnp.where(oh, neg, work)
    denom = sks[0]
    for k in range(1, TOP_K):
        denom = denom + sks[k]
    milestone(denom)

    ri = lax.broadcasted_iota(jnp.int32, (t, t), 0)
    ci = lax.broadcasted_iota(jnp.int32, (t, t), 1)
    before = (ri > ci).astype(BF16)
    sel_f = jnp.where(sel, 1.0, 0.0)
    sel_b = sel_f.astype(BF16)
    pos_tile = jnp.dot(before, sel_b, preferred_element_type=F32) + run[...]
    er = lax.broadcasted_iota(jnp.int32, (N_EXPERTS, N_EXPERTS), 0)
    ec = lax.broadcasted_iota(jnp.int32, (N_EXPERTS, N_EXPERTS), 1)
    rank_tile = jnp.dot(sel_b, (er < ec).astype(BF16), preferred_element_type=F32)
    lane128 = lax.broadcasted_iota(jnp.int32, (t, LANES), 1)
    k_iota = lax.broadcasted_iota(jnp.int32, (t, TOP_K), 1).astype(F32)
    dest = jnp.zeros((t, LANES), F32)
    w_sorted = jnp.zeros((t, TOP_K), F32)
    for k in range(TOP_K):
        pk = jnp.sum(jnp.where(onehots[k], pos_tile, 0.0), axis=-1, keepdims=True)
        rk = jnp.sum(jnp.where(onehots[k], rank_tile, 0.0), axis=-1, keepdims=True)
        dest = jnp.where(lane128 == k, idxs[k] * float(cap) + pk, dest)
        w_sorted = jnp.where(k_iota == rk, sks[k] / denom * ROUTED_SCALE, w_sorted)
    ws_ref[...] = w_sorted
    total = run[...] + jnp.sum(sel_f, axis=0, keepdims=True)
    run[...] = total
    cnt_ref[...] = total

    milestone(BEFORE_STAGING)
    for j in range(PACK_TILES):
        stage[slot, pl.ds(j, t, stride=SUBLANES), :] = packed[j]
    tok = lax.broadcasted_iota(jnp.int32, (t, LANES), 0) + i * t
    stage[slot, pl.ds(TOK_ROW, t, stride=SUBLANES), :] = tok.astype(jnp.uint32)
    bit = jnp.where(sel, jnp.left_shift(1, lane & 15), 0).astype(F32)
    words = []
    for q in range(N_EXPERTS // 16):
        part = jnp.sum(jnp.where((lane >> 4) == q, bit, 0.0), axis=-1, keepdims=True)
        words.append(part.astype(jnp.int32).astype(jnp.uint32))
    for w, row in enumerate(MASK_ROWS):
        word = words[2 * w] | (words[2 * w + 1] << 16)
        stage[slot, pl.ds(row, t, stride=SUBLANES), :] = jnp.broadcast_to(word, (t, LANES))

    dvm[...] = dest.T[0:TOP_K, :].astype(jnp.int32)
    pltpu.make_async_copy(dvm, dsm.at[slot], sem_s).start()


def _mix(proj, u, ypre, x2, w_z, w_gates, ssd_ng, pool_w, pool_scale, w_br_pool, w_br_ssd, w_out, g1, ng, sh2, sc2,
         g2, w_router, router_bias, ws_gu, ws_down, cap):
    n = x2.shape[0]
    t = 256
    n_tiles = n // t
    xp_blk = (D_SSD + 2 * D_BC) // D_MODEL
    const = lambda shape: pl.BlockSpec(shape, lambda i: (0,) * len(shape), pipeline_mode=pl.Buffered(1))
    tile = lambda i: jnp.minimum(i, n_tiles - 1)
    row = lambda i: (tile(i), 0)
    return pl.pallas_call(
        functools.partial(_mix_kernel, n_tiles=n_tiles, cap=cap),
        grid=(n_tiles + 1,),
        in_specs=[pl.BlockSpec((t, D_MODEL), lambda i: (tile(i), xp_blk)),
                  pl.BlockSpec((t, D_MODEL), row),
                  pl.BlockSpec((t, D_SSD), row),
                  pl.BlockSpec((t, D_MODEL), row),
                  const((D_MODEL, D_SSD)),
                  const((D_MODEL, 2 * D_MODEL)),
                  _full((1, D_SSD)),
                  const((len(POOL_WINDOWS), POOL_GDIM, POOL_GDIM)),
                  _full((1, D_MODEL)),
                  const((D_MODEL, D_MODEL)),
                  const((D_SSD, D_MODEL)),
                  const((D_MODEL, D_MODEL)),
                  _full((1, D_MODEL)), _full((1, D_MODEL)), _full((1, D_MODEL)), _full((1, D_MODEL)),
                  _full((1, D_MODEL)),
                  _full((D_MODEL, N_EXPERTS)),
                  _full((1, N_EXPERTS)),
                  const((D_MODEL, 2 * D_SHARED)),
                  const((D_SHARED, D_MODEL)),
                  pl.BlockSpec(memory_space=pltpu.SMEM)],
        out_specs=[pl.BlockSpec((t, D_MODEL), row),
                   pl.BlockSpec((t, TOP_K), row),
                   _full((1, N_EXPERTS)),
                   pl.BlockSpec(memory_space=pl.ANY)],
        out_shape=[jax.ShapeDtypeStruct((n, D_MODEL), F32),
                   jax.ShapeDtypeStruct((n, TOP_K), F32),
                   jax.ShapeDtypeStruct((1, N_EXPERTS), F32),
                   jax.ShapeDtypeStruct((N_EXPERTS * cap * SUBLANES, LANES), jnp.uint32)],
        scratch_shapes=[pltpu.VMEM((2 * SUBLANES, D_MODEL), F32),
                        pltpu.VMEM((1, N_EXPERTS), F32),
                        pltpu.VMEM((2, t * SUBLANES, LANES), jnp.uint32),
                        pltpu.VMEM((TOP_K, t), jnp.int32),
                        pltpu.SMEM((2, TOP_K, t), jnp.int32),
                        pltpu.SemaphoreType.DMA((2,)),
                        pltpu.SemaphoreType.DMA(())],
        compiler_params=_cparams(),
        name="mix",
    )(proj, u, ypre, x2, w_z, w_gates, ssd_ng, pool_w, pool_scale, w_br_pool, w_br_ssd, w_out, g1, ng, sh2, sc2, g2,
      w_router, router_bias, ws_gu, ws_down, jnp.zeros((1,), jnp.int32))


def _padfill_kernel(cnt_ref, xs_in_ref, xs_ref, padbuf, sem, *, cap):
    del xs_in_ref
    sub = lax.broadcasted_iota(jnp.int32, padbuf.shape, 0) & (SUBLANES - 1)
    padbuf[...] = jnp.where(sub == TOK_ROW, jnp.uint32(PAD_ID), jnp.uint32(0))
    sizes = [1 << bit for bit in range(EXPERT_BLOCK.bit_length() - 1)]

    def pad_copies(e, fn):
        c = cnt_ref[e]
        n_pad = (EXPERT_BLOCK - (c & (EXPERT_BLOCK - 1))) & (EXPERT_BLOCK - 1)
        first = e * cap + c
        for size in sizes:
            @pl.when((n_pad & size) != 0)
            def _():
                start = first + (n_pad & (size - 1))
                dst = xs_ref.at[pl.ds(pl.multiple_of(start * SUBLANES, SUBLANES), size * SUBLANES)]
                fn(pltpu.make_async_copy(padbuf.at[pl.ds(0, size * SUBLANES)], dst, sem))

    def issue(e, carry):
        pad_copies(e, lambda cp: cp.start())
        return carry

    lax.fori_loop(0, N_EXPERTS, issue, 0)

    def drain(e, carry):
        pad_copies(e, lambda cp: cp.wait())
        return carry

    lax.fori_loop(0, N_EXPERTS, drain, 0)


def _padfill(counts, xs, cap):
    return pl.pallas_call(
        functools.partial(_padfill_kernel, cap=cap),
        grid_spec=pltpu.PrefetchScalarGridSpec(
            num_scalar_prefetch=1,
            grid=(1,),
            in_specs=[pl.BlockSpec(memory_space=pl.ANY)],
            out_specs=pl.BlockSpec(memory_space=pl.ANY),
            scratch_shapes=[pltpu.VMEM((EXPERT_BLOCK // 2 * SUBLANES, LANES), jnp.uint32),
                            pltpu.SemaphoreType.DMA(())]),
        out_shape=jax.ShapeDtypeStruct(xs.shape, xs.dtype),
        input_output_aliases={1: 0},
        compiler_params=_cparams(),
        name="padfill",
    )(counts, xs)


def _expert_kernel(be_ref, br_ref, nb_ref, x_ref, wg_ref, wu_ref, wd_ref, ytok_ref, ybuf0, ybuf1, idv, ids0, ids1,
                   wgb, wub, wdb, sem, sem_ids, *, n_ids):
    del br_ref
    m = EXPERT_BLOCK
    b = pl.program_id(0)
    nb = nb_ref[0]
    ybufs, idss = (ybuf0, ybuf1), (ids0, ids1)
    n_phases = m // LANES
    half = D_MODEL // 2

    @pl.when((b == 0) | (be_ref[b] != be_ref[jnp.maximum(b - 1, 0)]))
    def _():
        wgb[...] = wg_ref[0].astype(BF16)
        wub[...] = wu_ref[0].astype(BF16)
        wdb[...] = wd_ref[0].astype(BF16)

    def half_tile(ref, unit):
        return ref.at[pl.ds(pl.multiple_of(unit * PACK_TILES, PACK_TILES), PACK_TILES)]

    def issue(s, a0):
        for col in range(LANES):
            pltpu.make_async_copy(half_tile(ybufs[s], a0 * LANES + col),
                                  half_tile(ytok_ref, idss[s][a0, col]), sem.at[s]).start()

    def step(s, prev):
        tokrep = x_ref[pl.ds(TOK_ROW, m, stride=SUBLANES), :]
        e = be_ref[b]
        below_lo = jnp.where(e >= 32, -1, (1 << jnp.minimum(e, 31)) - 1).astype(jnp.uint32)
        below_hi = jnp.where(e >= 32, (1 << jnp.maximum(e - 32, 0)) - 1, 0).astype(jnp.uint32)
        rank = (lax.population_count(x_ref[pl.ds(MASK_ROWS[0], m, stride=SUBLANES), :] & below_lo)
                + lax.population_count(x_ref[pl.ds(MASK_ROWS[1], m, stride=SUBLANES), :] & below_hi))
        r = lax.broadcasted_iota(jnp.int32, (m, LANES), 0)
        ln = lax.broadcasted_iota(jnp.int32, (m, LANES), 1)
        n_tok = n_ids // TOP_K
        rank = rank.astype(jnp.int32)
        half_rank = TOP_K // 2
        unit = ((rank % half_rank) * n_tok + tokrep.astype(jnp.int32)) * 2 + rank // half_rank
        idi = jnp.where(tokrep == jnp.uint32(PAD_ID), n_ids + r, unit)
        diag = jnp.where((r & (LANES - 1)) == ln, idi, 0).astype(F32)
        idv[...] = jnp.sum(diag.reshape(m // LANES, LANES, LANES), axis=1).astype(jnp.int32)
        ids_copy = pltpu.make_async_copy(idv, idss[s], sem_ids)
        ids_copy.start()

        halves = [_unpack_bf16_pair(x_ref[pl.ds(j, m, stride=SUBLANES), :]) for j in range(PACK_TILES)]
        x = jnp.concatenate([lo for lo, _ in halves] + [hi for _, hi in halves], axis=1).astype(BF16)
        phases = iter(range(n_phases))
        if prev is not None:
            issue(prev, next(phases))
        gate = jnp.dot(x, wgb[...], preferred_element_type=F32)
        if prev is not None:
            issue(prev, next(phases))
        up = jnp.dot(x, wub[...], preferred_element_type=F32)
        hid = (_silu(gate) * up).astype(BF16)

        def drain(which):
            pltpu.make_async_copy(ybufs[which], ytok_ref.at[pl.ds(0, m * PACK_TILES)], sem.at[which]).wait()

        if prev is not None:
            @pl.when(b >= 2)
            def _():
                drain(s)
        ys = []
        for c0 in (0, half):
            if prev is not None:
                issue(prev, next(phases))
            ys.append(jnp.dot(hid, wdb[:, c0:c0 + half], preferred_element_type=F32))
        for j in range(PACK_TILES):
            sl = slice(j * LANES, (j + 1) * LANES)
            ybufs[s][pl.ds(j, m, stride=PACK_TILES), :] = _pack_bf16_pair(ys[0][:, sl], ys[1][:, sl])
        ids_copy.wait()
        if prev is not None:
            for a0 in phases:
                issue(prev, a0)

            @pl.when(b == nb)
            def _():
                drain(prev)

    @pl.when(b == 0)
    def _():
        step(0, None)

    for s in range(2):
        @pl.when((b >= 1) & (b <= nb) & (lax.rem(b, 2) == s))
        def _():
            step(s, 1 - s)


def _experts(block_e, block_row, nb_used, xs, we_gate, we_up, we_down, n_ids):
    n_blocks = block_e.shape[0]
    m = EXPERT_BLOCK
    rows = m * SUBLANES
    return pl.pallas_call(
        functools.partial(_expert_kernel, n_ids=n_ids),
        grid_spec=pltpu.PrefetchScalarGridSpec(
            num_scalar_prefetch=3,
            grid=(n_blocks,),
            in_specs=[pl.BlockSpec((rows, LANES), lambda b, be, br, nb: (br[b], 0)),
                      pl.BlockSpec((1, D_MODEL, D_EXPERT), lambda b, be, br, nb: (be[b], 0, 0)),
                      pl.BlockSpec((1, D_MODEL, D_EXPERT), lambda b, be, br, nb: (be[b], 0, 0)),
                      pl.BlockSpec((1, D_EXPERT, D_MODEL), lambda b, be, br, nb: (be[b], 0, 0))],
            out_specs=pl.BlockSpec(memory_space=pl.ANY),
            scratch_shapes=[pltpu.VMEM((m * PACK_TILES, LANES), jnp.uint32),
                            pltpu.VMEM((m * PACK_TILES, LANES), jnp.uint32),
                            pltpu.VMEM((m // LANES, LANES), jnp.int32),
                            pltpu.SMEM((m // LANES, LANES), jnp.int32),
                            pltpu.SMEM((m // LANES, LANES), jnp.int32),
                            pltpu.VMEM((D_MODEL, D_EXPERT), BF16),
                            pltpu.VMEM((D_MODEL, D_EXPERT), BF16),
                            pltpu.VMEM((D_EXPERT, D_MODEL), BF16),
                            pltpu.SemaphoreType.DMA((2,)),
                            pltpu.SemaphoreType.DMA(())]),
        out_shape=jax.ShapeDtypeStruct(((n_ids + m) * PACK_TILES, LANES), jnp.uint32),
        compiler_params=_cparams(),
        name="experts",
    )(block_e, block_row, nb_used, xs, we_gate, we_up, we_down)


def _combine_kernel(*refs):
    half_rank = TOP_K // 2
    y_refs = refs[:half_rank]
    w8_ref, base_ref, g2_ref, fg_ref, o_ref = refs[half_rank:]
    t7 = base_ref.shape[0]
    w8 = w8_ref[...]
    parts = [jnp.zeros((t7, LANES), F32) for _ in range(ROW_TILES)]
    for p in range(half_rank):
        for h in range(2):
            wk = w8[:, p + h * half_rank:p + h * half_rank + 1]
            for j in range(PACK_TILES):
                lo, hi = _unpack_bf16_pair(y_refs[p][pl.ds(h * PACK_TILES + j, t7, stride=SUBLANES), :])
                parts[j] = parts[j] + wk * lo
                parts[j + PACK_TILES] = parts[j + PACK_TILES] + wk * hi
    routed = jnp.concatenate(parts, axis=1)
    h = base_ref[...] + g2_ref[...] * routed
    inv = lax.rsqrt(jnp.mean(h * h, axis=-1, keepdims=True) + EPS)
    o_ref[...] = h * inv * fg_ref[...]


def _combine(ytok, w8, base, g2, fg):
    n = base.shape[0]
    t7 = 512
    y_specs = [pl.BlockSpec((t7 * SUBLANES, LANES), functools.partial(lambda i, p: (p * (n // t7) + i, 0), p=p))
               for p in range(TOP_K // 2)]
    return pl.pallas_call(
        _combine_kernel,
        grid=(n // t7,),
        in_specs=y_specs + [
                  pl.BlockSpec((t7, TOP_K), lambda i: (i, 0)),
                  pl.BlockSpec((t7, D_MODEL), lambda i: (i, 0)),
                  _full((1, D_MODEL)), _full((1, D_MODEL))],
        out_specs=pl.BlockSpec((t7, D_MODEL), lambda i: (i, 0)),
        out_shape=jax.ShapeDtypeStruct((n, D_MODEL), F32),
        compiler_params=_cparams(),
        name="combine",
    )(*([ytok] * (TOP_K // 2)), w8, base, g2, fg)


def kernel(x, c, w_ada, b_ada, norm_mix_g, w_in, conv_w, conv_b, dt_bias, A_log, D_skip, ssd_norm_g, pool_w,
           pool_scale, w_br_ssd, w_br_pool, w_out, norm_ffn_g, w_router, router_bias, we_gate, we_up, we_down,
           ws_gate, ws_up, ws_down, final_norm_g):
    bsz, seq, _ = x.shape
    assert bsz == 1 and w_ada.shape[0] == 1
    n = seq
    h2 = x.reshape(n, D_MODEL)

    mod = _ada(c, w_ada[0], b_ada[0])
    sh1, sc1, g1, sh2, sc2, g2 = [mod[:, k * D_MODEL:(k + 1) * D_MODEL] for k in range(6)]

    wi = w_in[0]
    o_xbc, o_dt = D_SSD, D_SSD + D_SSD + 2 * D_BC
    o_pool = o_dt + N_HEADS
    o_gates = o_pool + D_MODEL
    w_dt = jnp.pad(wi[:, o_dt:o_pool], ((0, 0), (0, LANES - N_HEADS))).astype(BF16)
    wi_b = wi.astype(BF16)
    proj, dt_raw, u = _in_proj(h2, norm_mix_g[0][None], sh1, sc1, wi_b, wi_b[:, o_pool:o_gates], w_dt)

    pad_h = lambda v: jnp.pad(v[None], ((0, 0), (0, LANES - N_HEADS)))
    ypre = _ssd(proj, dt_raw, conv_w[0], conv_b[0][None], pad_h(dt_bias[0]), pad_h(A_log[0]),
                jnp.repeat(D_skip[0], HEADDIM)[None])

    ws_gu = jnp.concatenate([ws_gate[0], ws_up[0]], axis=1).astype(BF16)
    m = EXPERT_BLOCK
    cap = -(-n // m) * m
    base, w8, cnt, xs = _mix(
        proj, u, ypre, h2, wi_b, wi_b[:, o_gates:], ssd_norm_g[0][None],
        pool_w[0].astype(BF16), pool_scale[0][None], w_br_pool[0].astype(BF16),
        w_br_ssd[0].astype(BF16), w_out[0].astype(BF16), g1, norm_ffn_g[0][None], sh2, sc2, g2,
        w_router[0].astype(BF16), router_bias[0][None], ws_gu, ws_down[0].astype(BF16), cap)

    counts = cnt[0].astype(jnp.int32)
    nblk = (counts + m - 1) // m
    bends = jnp.cumsum(nblk)
    bstarts = bends - nblk
    nb_used = bends[-1]
    n_blocks = -(-(n * TOP_K) // m) + N_EXPERTS + 1
    b_eff = jnp.minimum(jnp.arange(n_blocks, dtype=jnp.int32), nb_used - 1)
    done = bends[None, :] <= b_eff[:, None]
    block_e = jnp.sum(done.astype(jnp.int32), axis=1)
    block_row = block_e * (cap // m) + b_eff - jnp.sum(jnp.where(done, nblk[None, :], 0), axis=1)

    xs = _padfill(counts, xs, cap)
    ytok = _experts(block_e, block_row, nb_used.reshape(1), xs,
                    we_gate[0], we_up[0], we_down[0], n * TOP_K)
    out = _combine(ytok, w8, base, g2, final_norm_g[None])
    return out.reshape(bsz, seq, D_MODEL)
```

```python
import functools

import jax
import jax.numpy as jnp
from jax import lax
from jax.experimental import pallas as pl
from jax.experimental.pallas import tpu as pltpu

F32 = jnp.float32
BF16 = jnp.bfloat16
HIGHEST = lax.Precision.HIGHEST

D_MODEL = 1024
D_SSD = 2048
HEADDIM = 64
N_HEADS = 32
N_GROUPS = 8
HEADS_PER_GROUP = N_HEADS // N_GROUPS
D_STATE = 128
CONV_K = 4
CHUNK = 128
SSD_CHUNKS_PER_STEP = 4
GROUP_W = D_SSD // N_GROUPS
D_BC = N_GROUPS * D_STATE
POOL_WINDOWS = (2, 4, 8, 16)
POOL_GDIM = 256
N_EXPERTS = 64
TOP_K = 8
N_EGROUPS = 8
EXPERTS_PER_GROUP = 8
TOPK_GROUPS = 4
D_EXPERT = 256
D_SHARED = 256
ROUTED_SCALE = 2.5
EPS = 1e-6

LANES = 128
SUBLANES = 8
ROW_TILES = D_MODEL // LANES
PACK_TILES = ROW_TILES // 2
PACK_W = PACK_TILES * LANES
EXPERT_BLOCK = 512
MIX_ISSUE_GROUPS = 32
BEFORE_STAGING = "before_staging"

PROJ_W = D_SSD + 2 * D_BC + D_MODEL
PROJ_CHUNK = 512

VMEM_LIMIT = 56 * 1024 * 1024


def _cparams(sem=("arbitrary",)):
    return pltpu.CompilerParams(dimension_semantics=sem, vmem_limit_bytes=VMEM_LIMIT)


def _full(shape):
    nd = len(shape)
    return pl.BlockSpec(shape, lambda *_: (0,) * nd)


def _silu(v):
    h = 0.5 * v
    return h + h * jnp.tanh(h)


def _pack_bf16_pair(lo, hi):
    lo_bits = lax.bitcast_convert_type(lo.astype(BF16).astype(F32), jnp.uint32)
    hi_bits = lax.bitcast_convert_type(hi.astype(BF16).astype(F32), jnp.uint32)
    return (lo_bits >> 16) | (hi_bits & jnp.uint32(0xFFFF0000))


def _unpack_bf16_pair(words):
    return (lax.bitcast_convert_type(words << 16, F32),
            lax.bitcast_convert_type(words & jnp.uint32(0xFFFF0000), F32))


def _ada_kernel(c_ref, w_ref, b_ref, o_ref):
    c = c_ref[...]
    o_ref[...] = jnp.dot(_silu(c), w_ref[...], preferred_element_type=F32, precision=HIGHEST) + b_ref[...]


def _ada(c, w_ada, b_ada):
    n_out = w_ada.shape[1]
    tn = 1536
    c8 = jnp.broadcast_to(c, (SUBLANES, D_MODEL))
    out = pl.pallas_call(
        _ada_kernel,
        grid=(n_out // tn,),
        in_specs=[_full((SUBLANES, D_MODEL)),
                  pl.BlockSpec((D_MODEL, tn), lambda j: (0, j)),
                  pl.BlockSpec((1, tn), lambda j: (0, j))],
        out_specs=pl.BlockSpec((SUBLANES, tn), lambda j: (0, j)),
        out_shape=jax.ShapeDtypeStruct((SUBLANES, n_out), F32),
        compiler_params=_cparams(),
        name="ada",
    )(c8, w_ada, b_ada.reshape(1, n_out))
    return out[0:1]


def _inproj_kernel(x_ref, g_ref, sh_ref, sc_ref, wa0_ref, wa1_ref, wb_ref, wdt_ref, proj_ref, dt_ref, u_ref):
    x = x_ref[...]
    inv = lax.rsqrt(jnp.mean(x * x, axis=-1, keepdims=True) + EPS)
    u = x * inv * g_ref[...]
    u = u * (1.0 + sc_ref[...]) + sh_ref[...]
    ub = u.astype(BF16)
    u_ref[...] = ub
    w_refs = (wa0_ref, wa1_ref, wb_ref)
    starts = [0]
    for w_ref in w_refs:
        starts.append(starts[-1] + w_ref.shape[1])
    for c0 in range(0, PROJ_W, PROJ_CHUNK):
        k = max(j for j in range(len(w_refs)) if starts[j] <= c0)
        w = w_refs[k][:, c0 - starts[k]:c0 - starts[k] + PROJ_CHUNK]
        proj_ref[:, c0:c0 + PROJ_CHUNK] = jnp.dot(ub, w, preferred_element_type=F32).astype(BF16)
    dt_ref[...] = jnp.dot(ub, wdt_ref[...], preferred_element_type=F32)


def _in_proj(x2, g, sh, sc, w_in_b, w_b, w_dt):
    n = x2.shape[0]
    tm = 512
    return pl.pallas_call(
        _inproj_kernel,
        grid=(n // tm,),
        in_specs=[pl.BlockSpec((tm, D_MODEL), lambda i: (i, 0)),
                  _full((1, D_MODEL)), _full((1, D_MODEL)), _full((1, D_MODEL)),
                  pl.BlockSpec((D_MODEL, D_SSD), lambda i: (0, 1), pipeline_mode=pl.Buffered(1)),
                  pl.BlockSpec((D_MODEL, D_SSD), lambda i: (0, 2), pipeline_mode=pl.Buffered(1)),
                  pl.BlockSpec(w_b.shape, lambda i: (0, 0), pipeline_mode=pl.Buffered(1)),
                  _full((D_MODEL, LANES))],
        out_specs=[pl.BlockSpec((tm, PROJ_W), lambda i: (i, 0)),
                   pl.BlockSpec((tm, LANES), lambda i: (i, 0)),
                   pl.BlockSpec((tm, D_MODEL), lambda i: (i, 0))],
        out_shape=[jax.ShapeDtypeStruct((n, PROJ_W), BF16),
                   jax.ShapeDtypeStruct((n, LANES), F32),
                   jax.ShapeDtypeStruct((n, D_MODEL), BF16)],
        compiler_params=_cparams(),
        name="in_proj",
    )(x2, g, sh, sc, w_in_b, w_in_b, w_b, w_dt)


def _conv_silu(cur_ref, ext_ref, w_ref, b_ref, out_ref, width, cw=512):
    t = cur_ref.shape[0]
    for c0 in range(0, width, cw):
        sl = slice(c0, c0 + cw)
        cur = cur_ref[:, sl].astype(F32)
        ext_ref[SUBLANES:, sl] = cur
        acc = cur * w_ref[CONV_K - 1:CONV_K, sl] + b_ref[:, sl]
        for s in range(1, CONV_K):
            acc = acc + ext_ref[pl.ds(SUBLANES - s, t), sl] * w_ref[CONV_K - 1 - s:CONV_K - s, sl]
        out_ref[:, sl] = _silu(acc).astype(out_ref.dtype)
        ext_ref[0:SUBLANES, sl] = cur[t - SUBLANES:t]


def _ssd_kernel(xs_ref, bc_ref, dt_ref, cwx_ref, cwbc_ref, cbx_ref, cbbc_ref, dtb_ref, alog_ref,
                dskip_ref, o_ref, tailx, tailbc, state, xc, bcc):
    q = CHUNK
    pair_w = 2 * HEADDIM

    @pl.when(pl.program_id(0) == 0)
    def _():
        tailx[...] = jnp.zeros_like(tailx)
        tailbc[...] = jnp.zeros_like(tailbc)
        state[...] = jnp.zeros_like(state)

    _conv_silu(xs_ref, tailx, cwx_ref, cbx_ref, xc, D_SSD)
    _conv_silu(bc_ref, tailbc, cwbc_ref, cbbc_ref, bcc, 2 * D_BC)

    for c in range(xs_ref.shape[0] // q):
        rows = slice(c * q, (c + 1) * q)
        v = dt_ref[rows, :] + dtb_ref[...]
        dt = jnp.maximum(v, 0.0) + jnp.log(1.0 + jnp.exp(-jnp.abs(v)))
        a = dt * (-jnp.exp(alog_ref[...]))
        ri = lax.broadcasted_iota(jnp.int32, (q, q), 0)
        ci = lax.broadcasted_iota(jnp.int32, (q, q), 1)
        causal = ri >= ci
        a_cs = jnp.dot(causal.astype(F32), a, preferred_element_type=F32, precision=HIGHEST)
        a_cs_t = a_cs.T
        dt_t = dt.T
        first_half = ci < HEADDIM
        first_half_row = first_half[0:1, :]

        for g in range(N_GROUPS):
            b_g = bcc[rows, g * D_STATE:(g + 1) * D_STATE]
            c_g = bcc[rows, D_BC + g * D_STATE:D_BC + (g + 1) * D_STATE]
            cb = lax.dot_general(c_g, b_g, (((1,), (1,)), ((), ())), preferred_element_type=F32)
            b_t = b_g.astype(F32).T
            st_g = state[g]
            y_off = jnp.dot(c_g, st_g.astype(BF16), preferred_element_type=F32)
            for pi in range(HEADS_PER_GROUP // 2):
                h0 = g * HEADS_PER_GROUP + 2 * pi
                lanes = slice(h0 * HEADDIM, h0 * HEADDIM + pair_w)
                gl = slice(pi * pair_w, (pi + 1) * pair_w)
                ms, ws, cols, lasts = [], [], [], []
                for h in (h0, h0 + 1):
                    col = jnp.broadcast_to(a_cs[:, h:h + 1], (q, q))
                    row = a_cs_t[h:h + 1, :]
                    dtrow = dt_t[h:h + 1, :]
                    decay = jnp.exp(jnp.where(causal, col - row, -jnp.inf))
                    ms.append((cb * decay * dtrow).astype(BF16))
                    last = a_cs_t[h:h + 1, q - 1:q]
                    ws.append((b_t * (jnp.exp(last - row) * dtrow)).astype(BF16))
                    cols.append(col)
                    lasts.append(last)
                xp = xc[rows, lanes]
                zero = jnp.zeros_like(xp)
                x_bd = jnp.concatenate([jnp.where(first_half, xp, zero), jnp.where(first_half, zero, xp)], axis=0)
                y_p = jnp.dot(jnp.concatenate(ms, axis=1), x_bd, preferred_element_type=F32)
                y_p = y_p + y_off[:, gl] * jnp.exp(jnp.where(first_half, cols[0], cols[1]))
                o_ref[rows, lanes] = (y_p + dskip_ref[:, lanes] * xp.astype(F32)).astype(BF16)
                s_new = jnp.dot(jnp.concatenate(ws, axis=1), x_bd, preferred_element_type=F32)
                carry = jnp.exp(jnp.where(first_half_row, lasts[0], lasts[1]))
                state[g, :, gl] = st_g[:, gl] * carry + s_new


def _ssd(proj, dt_raw, conv_w, conv_b, dt_bias, a_log, d_skip):
    n = proj.shape[0]
    t = SSD_CHUNKS_PER_STEP * CHUNK
    return pl.pallas_call(
        _ssd_kernel,
        grid=(n // t,),
        in_specs=[pl.BlockSpec((t, D_SSD), lambda i: (i, 0)),
                  pl.BlockSpec((t, 2 * D_BC), lambda i: (i, 1)),
                  pl.BlockSpec((t, LANES), lambda i: (i, 0)),
                  pl.BlockSpec((CONV_K, D_SSD), lambda i: (0, 0)),
                  pl.BlockSpec((CONV_K, 2 * D_BC), lambda i: (0, 1)),
                  pl.BlockSpec((1, D_SSD), lambda i: (0, 0)),
                  pl.BlockSpec((1, 2 * D_BC), lambda i: (0, 1)),
                  _full((1, LANES)), _full((1, LANES)),
                  _full((1, D_SSD))],
        out_specs=pl.BlockSpec((t, D_SSD), lambda i: (i, 0)),
        out_shape=jax.ShapeDtypeStruct((n, D_SSD), BF16),
        scratch_shapes=[pltpu.VMEM((SUBLANES + t, D_SSD), F32),
                        pltpu.VMEM((SUBLANES + t, 2 * D_BC), F32),
                        pltpu.VMEM((N_GROUPS, D_STATE, GROUP_W), F32),
                        pltpu.VMEM((t, D_SSD), BF16),
                        pltpu.VMEM((t, 2 * D_BC), BF16)],
        compiler_params=_cparams(),
        name="ssd",
    )(proj, proj, dt_raw, conv_w, conv_w, conv_b, conv_b, dt_bias, a_log, d_skip)


PAD_ID = 0xFFFFFFFF
TOK_ROW = PACK_TILES
MASK_ROWS = (PACK_TILES + 1, PACK_TILES + 2)


def _tile_rows(ref, first_row):
    return ref.at[pl.ds(pl.multiple_of(first_row * SUBLANES, SUBLANES), SUBLANES)]


def _mix_kernel(xp_ref, u_ref, ypre_ref, x_ref, wz_ref, wgt_ref, sng_ref, poolw_ref, pscale_ref, wbp_ref, wbs_ref, wout_ref,
                g1_ref, ng_ref, sh2_ref, sc2_ref, g2_ref, wr_ref, rb_ref, wsgu_ref, wsd_ref, zero_ref,
                base_ref, ws_ref, cnt_ref, xs_ref, ptail, run, stage, dvm, dsm, sem, sem_s, *, n_tiles, cap):
    t = x_ref.shape[0]
    i = pl.program_id(0)
    slot = lax.rem(i, 2)

    @pl.when(i == 0)
    def _():
        ptail[...] = jnp.zeros_like(ptail)
        run[...] = jnp.zeros_like(run)
        stage[...] = jnp.zeros_like(stage)

    def issue_rows(s, group, after=None):
        zero = 0
        if after is not None:
            probe = jnp.max(jnp.abs(after[0:SUBLANES, 0:min(LANES, after.shape[1])])).astype(jnp.int32)
            zero = probe * zero_ref[0]
        per = t // MIX_ISSUE_GROUPS
        for tt in range(group * per, (group + 1) * per):
            for k in range(TOP_K):
                pltpu.make_async_copy(stage.at[s, pl.ds(tt * SUBLANES, SUBLANES)],
                                      _tile_rows(xs_ref, dsm[s, k, tt] + zero),
                                      sem.at[s]).start(priority=k % 2)

    def drain_rows(s):
        for _ in range(TOP_K):
            pltpu.make_async_copy(stage.at[s], xs_ref.at[pl.ds(0, t * SUBLANES)], sem.at[s]).wait()

    def tile(s, milestone=None):
        _mix_tile_body(xp_ref, u_ref, ypre_ref, x_ref, wz_ref, wgt_ref, sng_ref, poolw_ref, pscale_ref, wbp_ref, wbs_ref, wout_ref,
                       g1_ref, ng_ref, sh2_ref, sc2_ref, g2_ref, wr_ref, rb_ref, wsgu_ref, wsd_ref,
                       base_ref, ws_ref, cnt_ref, ptail, run, stage, dvm, dsm, sem_s, i, s, cap, milestone)

    @pl.when(i == 0)
    def _():
        tile(0)

    groups = iter(range(1, MIX_ISSUE_GROUPS))

    def milestone(value):
        if value is BEFORE_STAGING:
            assert next(groups, None) is None
            @pl.when(i >= 2)
            def _():
                drain_rows(slot)
        else:
            issue_rows(1 - slot, next(groups), value)

    def wait_slot_rows(s):
        pltpu.make_async_copy(dvm, dsm.at[s], sem_s).wait()

    @pl.when((i >= 1) & (i < n_tiles))
    def _():
        wait_slot_rows(1 - slot)
        issue_rows(1 - slot, 0)
        tile(slot, milestone)

    @pl.when(i == n_tiles)
    def _():
        wait_slot_rows(1 - slot)
        for group in range(MIX_ISSUE_GROUPS):
            issue_rows(1 - slot, group)
        if n_tiles >= 2:
            drain_rows(slot)
        drain_rows(1 - slot)


def _mix_tile_body(xp_ref, u_ref, ypre_ref, x_ref, wz_ref, wgt_ref, sng_ref, poolw_ref, pscale_ref, wbp_ref, wbs_ref, wout_ref,
                   g1_ref, ng_ref, sh2_ref, sc2_ref, g2_ref, wr_ref, rb_ref, wsgu_ref, wsd_ref,
                   base_ref, ws_ref, cnt_ref, ptail, run, stage, dvm, dsm, sem_s, i, slot, cap, milestone):
    t = x_ref.shape[0]
    if milestone is None:
        milestone = lambda value: None

    xp = xp_ref[...].astype(F32)
    ext = jnp.concatenate([ptail[...], xp], axis=0)
    ptail[...] = xp[t - 2 * SUBLANES:t]
    pos = (lax.broadcasted_iota(jnp.int32, (t, 1), 0) + (i * t + 1)).astype(F32)
    pooled = []
    for gi, w in enumerate(POOL_WINDOWS):
        sl = slice(gi * POOL_GDIM, (gi + 1) * POOL_GDIM)
        e = ext[:, sl]
        s = e
        span = 1
        while span < w:
            s = s + pltpu.roll(s, span, axis=0)
            span *= 2
        win = s[2 * SUBLANES:]
        mean = win / jnp.minimum(pos, float(w))
        pg = (mean - xp[:, sl]).astype(BF16)
        mixed = jnp.dot(pg, poolw_ref[gi], preferred_element_type=F32)
        pooled.append(mixed * pscale_ref[:, sl])
        milestone(pooled[-1])
    pooled = jnp.concatenate(pooled, axis=1).astype(BF16)
    y_pool = jnp.dot(pooled, wbp_ref[...], preferred_element_type=F32)
    milestone(y_pool)
    ub = u_ref[...]
    z = jnp.dot(ub, wz_ref[...], preferred_element_type=F32)
    milestone(z)
    yn = []
    for g in range(N_GROUPS):
        sl = slice(g * GROUP_W, (g + 1) * GROUP_W)
        yg = ypre_ref[:, sl].astype(F32) * _silu(z[:, sl])
        inv_g = lax.rsqrt(jnp.mean(yg * yg, axis=-1, keepdims=True) + EPS)
        yn.append((yg * inv_g * sng_ref[:, sl]).astype(BF16))
        milestone(inv_g)
    y_ssd = jnp.dot(jnp.concatenate(yn, axis=1), wbs_ref[...], preferred_element_type=F32)
    milestone(y_ssd)
    gates = jnp.dot(ub, wgt_ref[...], preferred_element_type=F32)
    milestone(gates)
    g_ssd = jax.nn.sigmoid(gates[:, :D_MODEL])
    g_pool = jax.nn.sigmoid(gates[:, D_MODEL:])
    mixed = (g_ssd * y_ssd + g_pool * y_pool).astype(BF16)
    h = x_ref[...] + g1_ref[...] * jnp.dot(mixed, wout_ref[...], preferred_element_type=F32)
    milestone(h)

    inv = lax.rsqrt(jnp.mean(h * h, axis=-1, keepdims=True) + EPS)
    milestone(inv)
    u2 = h * inv * ng_ref[...]
    u2 = u2 * (1.0 + sc2_ref[...]) + sh2_ref[...]
    u2b = u2.astype(BF16)
    packed = [_pack_bf16_pair(u2[:, j * LANES:(j + 1) * LANES],
                              u2[:, (j + PACK_TILES) * LANES:(j + PACK_TILES + 1) * LANES])
              for j in range(PACK_TILES)]

    hs = jnp.dot(u2b, wsgu_ref[...], preferred_element_type=F32)
    milestone(hs)
    act = (_silu(hs[:, :D_SHARED]) * hs[:, D_SHARED:]).astype(BF16)
    shared = jnp.dot(act, wsd_ref[...], preferred_element_type=F32)
    milestone(shared)
    base_ref[...] = h + g2_ref[...] * shared

    logits = jnp.dot(u2b, wr_ref[...], preferred_element_type=F32)
    scores = jax.nn.sigmoid(logits)
    milestone(scores)
    choice = scores + rb_ref[...]
    lane = lax.broadcasted_iota(jnp.int32, (t, N_EXPERTS), 1)
    lane_f = lane.astype(F32)
    lane_grp = lane // EXPERTS_PER_GROUP
    neg = -jnp.inf
    gscore = []
    for g in range(N_EGROUPS):
        vg = jnp.where(lane_grp == g, choice, neg)
        m1 = jnp.max(vg, axis=-1, keepdims=True)
        i1 = jnp.min(jnp.where(vg == m1, lane_f, float(N_EXPERTS)), axis=-1, keepdims=True)
        m2 = jnp.max(jnp.where(lane_f == i1, neg, vg), axis=-1, keepdims=True)
        gscore.append(m1 + m2)
        milestone(gscore[-1])
    gmask = jnp.zeros((t, N_EXPERTS), jnp.bool_)
    for g in range(N_EGROUPS):
        rank = jnp.zeros((t, 1), F32)
        for g2 in range(N_EGROUPS):
            if g2 == g:
                continue
            better = (gscore[g2] > gscore[g]) | ((gscore[g2] == gscore[g]) & (g2 < g))
            rank = rank + better.astype(F32)
        gmask = gmask | ((rank < float(TOPK_GROUPS)) & (lane_grp == g))
    work = jnp.where(gmask, choice, neg)
    sel = jnp.zeros((t, N_EXPERTS), jnp.bool_)
    onehots, idxs, sks = [], [], []
    for k in range(TOP_K):
        m = jnp.max(work, axis=-1, keepdims=True)
        idx = jnp.min(jnp.where(work == m, lane_f, float(N_EXPERTS)), axis=-1, keepdims=True)
        oh = lane_f == idx
        onehots.append(oh)
        idxs.append(idx)
        sks.append(jnp.sum(jnp.where(oh, scores, 0.0), axis=-1, keepdims=True))
        if k == TOP_K // 2 - 1:
            milestone(sks[-1])
        sel = sel | oh
        work = jnp.where(oh, neg, work)
    denom = sks[0]
    for k in range(1, TOP_K):
        denom = denom + sks[k]
    milestone(denom)

    ri = lax.broadcasted_iota(jnp.int32, (t, t), 0)
    ci = lax.broadcasted_iota(jnp.int32, (t, t), 1)
    before = (ri > ci).astype(BF16)
    sel_f = jnp.where(sel, 1.0, 0.0)
    sel_b = sel_f.astype(BF16)
    pos_tile = jnp.dot(before, sel_b, preferred_element_type=F32) + run[...]
    er = lax.broadcasted_iota(jnp.int32, (N_EXPERTS, N_EXPERTS), 0)
    ec = lax.broadcasted_iota(jnp.int32, (N_EXPERTS, N_EXPERTS), 1)
    rank_tile = jnp.dot(sel_b, (er < ec).astype(BF16), preferred_element_type=F32)
    lane128 = lax.broadcasted_iota(jnp.int32, (t, LANES), 1)
    k_iota = lax.broadcasted_iota(jnp.int32, (t, TOP_K), 1).astype(F32)
    dest = jnp.zeros((t, LANES), F32)
    w_sorted = jnp.zeros((t, TOP_K), F32)
    for k in range(TOP_K):
        pk = jnp.sum(jnp.where(onehots[k], pos_tile, 0.0), axis=-1, keepdims=True)
        rk = jnp.sum(jnp.where(onehots[k], rank_tile, 0.0), axis=-1, keepdims=True)
        dest = jnp.where(lane128 == k, idxs[k] * float(cap) + pk, dest)
        w_sorted = jnp.where(k_iota == rk, sks[k] / denom * ROUTED_SCALE, w_sorted)
    ws_ref[...] = w_sorted
    total = run[...] + jnp.sum(sel_f, axis=0, keepdims=True)
    run[...] = total
    cnt_ref[...] = total

    milestone(BEFORE_STAGING)
    for j in range(PACK_TILES):
        stage[slot, pl.ds(j, t, stride=SUBLANES), :] = packed[j]
    tok = lax.broadcasted_iota(jnp.int32, (t, LANES), 0) + i * t
    stage[slot, pl.ds(TOK_ROW, t, stride=SUBLANES), :] = tok.astype(jnp.uint32)
    bit = jnp.where(sel, jnp.left_shift(1, lane & 15), 0).astype(F32)
    words = []
    for q in range(N_EXPERTS // 16):
        part = jnp.sum(jnp.where((lane >> 4) == q, bit, 0.0), axis=-1, keepdims=True)
        words.append(part.astype(jnp.int32).astype(jnp.uint32))
    for w, row in enumerate(MASK_ROWS):
        word = words[2 * w] | (words[2 * w + 1] << 16)
        stage[slot, pl.ds(row, t, stride=SUBLANES), :] = jnp.broadcast_to(word, (t, LANES))

    dvm[...] = dest.T[0:TOP_K, :].astype(jnp.int32)
    pltpu.make_async_copy(dvm, dsm.at[slot], sem_s).start()


def _mix(proj, u, ypre, x2, w_z, w_gates, ssd_ng, pool_w, pool_scale, w_br_pool, w_br_ssd, w_out, g1, ng, sh2, sc2,
         g2, w_router, router_bias, ws_gu, ws_down, cap):
    n = x2.shape[0]
    t = 256
    n_tiles = n // t
    xp_blk = (D_SSD + 2 * D_BC) // D_MODEL
    const = lambda shape: pl.BlockSpec(shape, lambda i: (0,) * len(shape), pipeline_mode=pl.Buffered(1))
    tile = lambda i: jnp.minimum(i, n_tiles - 1)
    row = lambda i: (tile(i), 0)
    return pl.pallas_call(
        functools.partial(_mix_kernel, n_tiles=n_tiles, cap=cap),
        grid=(n_tiles + 1,),
        in_specs=[pl.BlockSpec((t, D_MODEL), lambda i: (tile(i), xp_blk)),
                  pl.BlockSpec((t, D_MODEL), row),
                  pl.BlockSpec((t, D_SSD), row),
                  pl.BlockSpec((t, D_MODEL), row),
                  const((D_MODEL, D_SSD)),
                  const((D_MODEL, 2 * D_MODEL)),
                  _full((1, D_SSD)),
                  const((len(POOL_WINDOWS), POOL_GDIM, POOL_GDIM)),
                  _full((1, D_MODEL)),
                  const((D_MODEL, D_MODEL)),
                  const((D_SSD, D_MODEL)),
                  const((D_MODEL, D_MODEL)),
                  _full((1, D_MODEL)), _full((1, D_MODEL)), _full((1, D_MODEL)), _full((1, D_MODEL)),
                  _full((1, D_MODEL)),
                  _full((D_MODEL, N_EXPERTS)),
                  _full((1, N_EXPERTS)),
                  const((D_MODEL, 2 * D_SHARED)),
                  const((D_SHARED, D_MODEL)),
                  pl.BlockSpec(memory_space=pltpu.SMEM)],
        out_specs=[pl.BlockSpec((t, D_MODEL), row),
                   pl.BlockSpec((t, TOP_K), row),
                   _full((1, N_EXPERTS)),
                   pl.BlockSpec(memory_space=pl.ANY)],
        out_shape=[jax.ShapeDtypeStruct((n, D_MODEL), F32),
                   jax.ShapeDtypeStruct((n, TOP_K), F32),
                   jax.ShapeDtypeStruct((1, N_EXPERTS), F32),
                   jax.ShapeDtypeStruct((N_EXPERTS * cap * SUBLANES, LANES), jnp.uint32)],
        scratch_shapes=[pltpu.VMEM((2 * SUBLANES, D_MODEL), F32),
                        pltpu.VMEM((1, N_EXPERTS), F32),
                        pltpu.VMEM((2, t * SUBLANES, LANES), jnp.uint32),
                        pltpu.VMEM((TOP_K, t), jnp.int32),
                        pltpu.SMEM((2, TOP_K, t), jnp.int32),
                        pltpu.SemaphoreType.DMA((2,)),
                        pltpu.SemaphoreType.DMA(())],
        compiler_params=_cparams(),
        name="mix",
    )(proj, u, ypre, x2, w_z, w_gates, ssd_ng, pool_w, pool_scale, w_br_pool, w_br_ssd, w_out, g1, ng, sh2, sc2, g2,
      w_router, router_bias, ws_gu, ws_down, jnp.zeros((1,), jnp.int32))


def _padfill_kernel(cnt_ref, xs_in_ref, xs_ref, padbuf, sem, *, cap):
    del xs_in_ref
    sub = lax.broadcasted_iota(jnp.int32, padbuf.shape, 0) & (SUBLANES - 1)
    padbuf[...] = jnp.where(sub == TOK_ROW, jnp.uint32(PAD_ID), jnp.uint32(0))
    sizes = [1 << bit for bit in range(EXPERT_BLOCK.bit_length() - 1)]

    def pad_copies(e, fn):
        c = cnt_ref[e]
        n_pad = (EXPERT_BLOCK - (c & (EXPERT_BLOCK - 1))) & (EXPERT_BLOCK - 1)
        first = e * cap + c
        for size in sizes:
            @pl.when((n_pad & size) != 0)
            def _():
                start = first + (n_pad & (size - 1))
                dst = xs_ref.at[pl.ds(pl.multiple_of(start * SUBLANES, SUBLANES), size * SUBLANES)]
                fn(pltpu.make_async_copy(padbuf.at[pl.ds(0, size * SUBLANES)], dst, sem))

    def issue(e, carry):
        pad_copies(e, lambda cp: cp.start())
        return carry

    lax.fori_loop(0, N_EXPERTS, issue, 0)

    def drain(e, carry):
        pad_copies(e, lambda cp: cp.wait())
        return carry

    lax.fori_loop(0, N_EXPERTS, drain, 0)


def _padfill(counts, xs, cap):
    return pl.pallas_call(
        functools.partial(_padfill_kernel, cap=cap),
        grid_spec=pltpu.PrefetchScalarGridSpec(
            num_scalar_prefetch=1,
            grid=(1,),
            in_specs=[pl.BlockSpec(memory_space=pl.ANY)],
            out_specs=pl.BlockSpec(memory_space=pl.ANY),
            scratch_shapes=[pltpu.VMEM((EXPERT_BLOCK // 2 * SUBLANES, LANES), jnp.uint32),
                            pltpu.SemaphoreType.DMA(())]),
        out_shape=jax.ShapeDtypeStruct(xs.shape, xs.dtype),
        input_output_aliases={1: 0},
        compiler_params=_cparams(),
        name="padfill",
    )(counts, xs)


def _expert_kernel(be_ref, br_ref, nb_ref, x_ref, wg_ref, wu_ref, wd_ref, ytok_ref, ybuf0, ybuf1, idv, ids0, ids1,
                   wgb, wub, wdb, sem, sem_ids, *, n_ids):
    del br_ref
    m = EXPERT_BLOCK
    b = pl.program_id(0)
    nb = nb_ref[0]
    ybufs, idss = (ybuf0, ybuf1), (ids0, ids1)
    n_phases = m // LANES
    half = D_MODEL // 2

    @pl.when((b == 0) | (be_ref[b] != be_ref[jnp.maximum(b - 1, 0)]))
    def _():
        wgb[...] = wg_ref[0].astype(BF16)
        wub[...] = wu_ref[0].astype(BF16)
        wdb[...] = wd_ref[0].astype(BF16)

    def half_tile(ref, unit):
        return ref.at[pl.ds(pl.multiple_of(unit * PACK_TILES, PACK_TILES), PACK_TILES)]

    def issue(s, a0):
        for col in range(LANES):
            pltpu.make_async_copy(half_tile(ybufs[s], a0 * LANES + col),
                                  half_tile(ytok_ref, idss[s][a0, col]),
                                  sem.at[s]).start(priority=col % 2)

    def step(s, prev):
        tokrep = x_ref[pl.ds(TOK_ROW, m, stride=SUBLANES), :]
        e = be_ref[b]
        below_lo = jnp.where(e >= 32, -1, (1 << jnp.minimum(e, 31)) - 1).astype(jnp.uint32)
        below_hi = jnp.where(e >= 32, (1 << jnp.maximum(e - 32, 0)) - 1, 0).astype(jnp.uint32)
        rank = (lax.population_count(x_ref[pl.ds(MASK_ROWS[0], m, stride=SUBLANES), :] & below_lo)
                + lax.population_count(x_ref[pl.ds(MASK_ROWS[1], m, stride=SUBLANES), :] & below_hi))
        r = lax.broadcasted_iota(jnp.int32, (m, LANES), 0)
        ln = lax.broadcasted_iota(jnp.int32, (m, LANES), 1)
        n_tok = n_ids // TOP_K
        rank = rank.astype(jnp.int32)
        half_rank = TOP_K // 2
        unit = ((rank % half_rank) * n_tok + tokrep.astype(jnp.int32)) * 2 + rank // half_rank
        idi = jnp.where(tokrep == jnp.uint32(PAD_ID), n_ids + r, unit)
        diag = jnp.where((r & (LANES - 1)) == ln, idi, 0).astype(F32)
        idv[...] = jnp.sum(diag.reshape(m // LANES, LANES, LANES), axis=1).astype(jnp.int32)
        ids_copy = pltpu.make_async_copy(idv, idss[s], sem_ids)
        ids_copy.start()

        halves = [_unpack_bf16_pair(x_ref[pl.ds(j, m, stride=SUBLANES), :]) for j in range(PACK_TILES)]
        x = jnp.concatenate([lo for lo, _ in halves] + [hi for _, hi in halves], axis=1).astype(BF16)
        phases = iter(range(n_phases))
        if prev is not None:
            issue(prev, next(phases))
        gate = jnp.dot(x, wgb[...], preferred_element_type=F32)
        if prev is not None:
            issue(prev, next(phases))
        up = jnp.dot(x, wub[...], preferred_element_type=F32)
        hid = (_silu(gate) * up).astype(BF16)

        def drain(which):
            pltpu.make_async_copy(ybufs[which], ytok_ref.at[pl.ds(0, m * PACK_TILES)], sem.at[which]).wait()

        if prev is not None:
            @pl.when(b >= 2)
            def _():
                drain(s)
        ys = []
        for c0 in (0, half):
            if prev is not None:
                issue(prev, next(phases))
            ys.append(jnp.dot(hid, wdb[:, c0:c0 + half], preferred_element_type=F32))
        for j in range(PACK_TILES):
            sl = slice(j * LANES, (j + 1) * LANES)
            ybufs[s][pl.ds(j, m, stride=PACK_TILES), :] = _pack_bf16_pair(ys[0][:, sl], ys[1][:, sl])
        ids_copy.wait()
        if prev is not None:
            for a0 in phases:
                issue(prev, a0)

            @pl.when(b == nb)
            def _():
                drain(prev)

    @pl.when(b == 0)
    def _():
        step(0, None)

    for s in range(2):
        @pl.when((b >= 1) & (b <= nb) & (lax.rem(b, 2) == s))
        def _():
            step(s, 1 - s)


def _experts(block_e, block_row, nb_used, xs, we_gate, we_up, we_down, n_ids):
    n_blocks = block_e.shape[0]
    m = EXPERT_BLOCK
    rows = m * SUBLANES
    return pl.pallas_call(
        functools.partial(_expert_kernel, n_ids=n_ids),
        grid_spec=pltpu.PrefetchScalarGridSpec(
            num_scalar_prefetch=3,
            grid=(n_blocks,),
            in_specs=[pl.BlockSpec((rows, LANES), lambda b, be, br, nb: (br[b], 0)),
                      pl.BlockSpec((1, D_MODEL, D_EXPERT), lambda b, be, br, nb: (be[b], 0, 0)),
                      pl.BlockSpec((1, D_MODEL, D_EXPERT), lambda b, be, br, nb: (be[b], 0, 0)),
                      pl.BlockSpec((1, D_EXPERT, D_MODEL), lambda b, be, br, nb: (be[b], 0, 0))],
            out_specs=pl.BlockSpec(memory_space=pl.ANY),
            scratch_shapes=[pltpu.VMEM((m * PACK_TILES, LANES), jnp.uint32),
                            pltpu.VMEM((m * PACK_TILES, LANES), jnp.uint32),
                            pltpu.VMEM((m // LANES, LANES), jnp.int32),
                            pltpu.SMEM((m // LANES, LANES), jnp.int32),
                            pltpu.SMEM((m // LANES, LANES), jnp.int32),
                            pltpu.VMEM((D_MODEL, D_EXPERT), BF16),
                            pltpu.VMEM((D_MODEL, D_EXPERT), BF16),
                            pltpu.VMEM((D_EXPERT, D_MODEL), BF16),
                            pltpu.SemaphoreType.DMA((2,)),
                            pltpu.SemaphoreType.DMA(())]),
        out_shape=jax.ShapeDtypeStruct(((n_ids + m) * PACK_TILES, LANES), jnp.uint32),
        compiler_params=_cparams(),
        name="experts",
    )(block_e, block_row, nb_used, xs, we_gate, we_up, we_down)


def _combine_kernel(*refs):
    half_rank = TOP_K // 2
    y_refs = refs[:half_rank]
    w8_ref, base_ref, g2_ref, fg_ref, o_ref = refs[half_rank:]
    t7 = base_ref.shape[0]
    w8 = w8_ref[...]
    parts = [jnp.zeros((t7, LANES), F32) for _ in range(ROW_TILES)]
    for p in range(half_rank):
        for h in range(2):
            wk = w8[:, p + h * half_rank:p + h * half_rank + 1]
            for j in range(PACK_TILES):
                lo, hi = _unpack_bf16_pair(y_refs[p][pl.ds(h * PACK_TILES + j, t7, stride=SUBLANES), :])
                parts[j] = parts[j] + wk * lo
                parts[j + PACK_TILES] = parts[j + PACK_TILES] + wk * hi
    routed = jnp.concatenate(parts, axis=1)
    h = base_ref[...] + g2_ref[...] * routed
    inv = lax.rsqrt(jnp.mean(h * h, axis=-1, keepdims=True) + EPS)
    o_ref[...] = h * inv * fg_ref[...]


def _combine(ytok, w8, base, g2, fg):
    n = base.shape[0]
    t7 = 512
    y_specs = [pl.BlockSpec((t7 * SUBLANES, LANES), functools.partial(lambda i, p: (p * (n // t7) + i, 0), p=p))
               for p in range(TOP_K // 2)]
    return pl.pallas_call(
        _combine_kernel,
        grid=(n // t7,),
        in_specs=y_specs + [
                  pl.BlockSpec((t7, TOP_K), lambda i: (i, 0)),
                  pl.BlockSpec((t7, D_MODEL), lambda i: (i, 0)),
                  _full((1, D_MODEL)), _full((1, D_MODEL))],
        out_specs=pl.BlockSpec((t7, D_MODEL), lambda i: (i, 0)),
        out_shape=jax.ShapeDtypeStruct((n, D_MODEL), F32),
        compiler_params=_cparams(),
        name="combine",
    )(*([ytok] * (TOP_K // 2)), w8, base, g2, fg)


def kernel(x, c, w_ada, b_ada, norm_mix_g, w_in, conv_w, conv_b, dt_bias, A_log, D_skip, ssd_norm_g, pool_w,
           pool_scale, w_br_ssd, w_br_pool, w_out, norm_ffn_g, w_router, router_bias, we_gate, we_up, we_down,
           ws_gate, ws_up, ws_down, final_norm_g):
    bsz, seq, _ = x.shape
    assert bsz == 1 and w_ada.shape[0] == 1
    n = seq
    h2 = x.reshape(n, D_MODEL)

    mod = _ada(c, w_ada[0], b_ada[0])
    sh1, sc1, g1, sh2, sc2, g2 = [mod[:, k * D_MODEL:(k + 1) * D_MODEL] for k in range(6)]

    wi = w_in[0]
    o_xbc, o_dt = D_SSD, D_SSD + D_SSD + 2 * D_BC
    o_pool = o_dt + N_HEADS
    o_gates = o_pool + D_MODEL
    w_dt = jnp.pad(wi[:, o_dt:o_pool], ((0, 0), (0, LANES - N_HEADS))).astype(BF16)
    wi_b = wi.astype(BF16)
    proj, dt_raw, u = _in_proj(h2, norm_mix_g[0][None], sh1, sc1, wi_b, wi_b[:, o_pool:o_gates], w_dt)

    pad_h = lambda v: jnp.pad(v[None], ((0, 0), (0, LANES - N_HEADS)))
    ypre = _ssd(proj, dt_raw, conv_w[0], conv_b[0][None], pad_h(dt_bias[0]), pad_h(A_log[0]),
                jnp.repeat(D_skip[0], HEADDIM)[None])

    ws_gu = jnp.concatenate([ws_gate[0], ws_up[0]], axis=1).astype(BF16)
    m = EXPERT_BLOCK
    cap = -(-n // m) * m
    base, w8, cnt, xs = _mix(
        proj, u, ypre, h2, wi_b, wi_b[:, o_gates:], ssd_norm_g[0][None],
        pool_w[0].astype(BF16), pool_scale[0][None], w_br_pool[0].astype(BF16),
        w_br_ssd[0].astype(BF16), w_out[0].astype(BF16), g1, norm_ffn_g[0][None], sh2, sc2, g2,
        w_router[0].astype(BF16), router_bias[0][None], ws_gu, ws_down[0].astype(BF16), cap)

    counts = cnt[0].astype(jnp.int32)
    nblk = (counts + m - 1) // m
    bends = jnp.cumsum(nblk)
    bstarts = bends - nblk
    nb_used = bends[-1]
    n_blocks = -(-(n * TOP_K) // m) + N_EXPERTS + 1
    b_eff = jnp.minimum(jnp.arange(n_blocks, dtype=jnp.int32), nb_used - 1)
    done = bends[None, :] <= b_eff[:, None]
    block_e = jnp.sum(done.astype(jnp.int32), axis=1)
    block_row = block_e * (cap // m) + b_eff - jnp.sum(jnp.where(done, nblk[None, :], 0), axis=1)

    xs = _padfill(counts, xs, cap)
    ytok = _experts(block_e, block_row, nb_used.reshape(1), xs,
                    we_gate[0], we_up[0], we_down[0], n * TOP_K)
    out = _combine(ytok, w8, base, g2, final_norm_g[None])
    return out.reshape(bsz, seq, D_MODEL)
```
